```python
import math
import jax, jax.numpy as jnp
from jax import lax
import numpy as np

D_MODEL = 1024
BATCH = 8
SEQ = 4096
DEPTH = 2

HEAD_DIM = 64
N_HEADS_TOTAL = D_MODEL // HEAD_DIM
N_SB_HEADS = N_HEADS_TOTAL // 4
DIL_GROUPS = ((128, 1), (512, 4), (2048, 16))
N_DIL_GROUPS = len(DIL_GROUPS)
HEADS_PER_GROUP = (N_HEADS_TOTAL - N_SB_HEADS) // N_DIL_GROUPS
N_DIL_HEADS = HEADS_PER_GROUP * N_DIL_GROUPS
D_DIL = N_DIL_HEADS * HEAD_DIM
D_DIL_OUT = HEADS_PER_GROUP * HEAD_DIM
D_SB = N_SB_HEADS * HEAD_DIM
D_IN = 3 * D_DIL + 3 * D_SB + 2 * D_MODEL
D_FF = 128 * ((8 * D_MODEL // 3 + 127) // 128)
ROPE_THETA = 500000.0
ROPE_DIM = HEAD_DIM // 4
Q_BLOCK = 128
RMS_EPS = 1e-6

kernel_name = "hybrid_dilated_stickbreaking_macaron"


def rms_norm(x, gain):
    xf = x.astype(jnp.float32)
    xf = xf * lax.rsqrt(jnp.mean(xf * xf, axis=-1, keepdims=True) + RMS_EPS)
    return (xf * gain.astype(jnp.float32)).astype(x.dtype)


def swiglu(x, w_gate, w_up, w_down):
    return (jax.nn.silu(x @ w_gate) * (x @ w_up)) @ w_down


def rope_tables(seq_len):
    pos = jnp.arange(seq_len, dtype=jnp.float32)
    inv_freq = ROPE_THETA ** (-jnp.arange(0, ROPE_DIM, 2, dtype=jnp.float32) / ROPE_DIM)
    ang = pos[:, None] * inv_freq[None, :]
    return jnp.cos(ang), jnp.sin(ang)


def apply_partial_rope(x, cos, sin):
    half = ROPE_DIM // 2
    x1 = x[..., :half]
    x2 = x[..., half:ROPE_DIM]
    c = cos.astype(x.dtype)
    s = sin.astype(x.dtype)
    return jnp.concatenate([x1 * c - x2 * s, x2 * c + x1 * s, x[..., ROPE_DIM:]], axis=-1)


def dilated_window_attention(q, k, v, window, dilation):
    B, H, T, dh = q.shape
    span = window // dilation
    unit = span * dilation
    t_pad = -(-T // unit) * unit
    n_sub = t_pad // dilation
    n_blk = n_sub // span

    def to_blocks(a):
        a = jnp.pad(a, ((0, 0), (0, 0), (0, t_pad - T), (0, 0)))
        a = a.reshape(B, H, n_sub, dilation, dh).transpose(0, 1, 3, 2, 4)
        return a.reshape(B, H, dilation, n_blk, span, dh)

    qb, kb, vb = to_blocks(q), to_blocks(k), to_blocks(v)

    def with_prev(a):
        prev = jnp.pad(a, ((0, 0), (0, 0), (0, 0), (1, 0), (0, 0), (0, 0)))[:, :, :, :-1]
        return jnp.concatenate([prev, a], axis=4)

    kw, vw = with_prev(kb), with_prev(vb)
    s = jnp.einsum('bhrnqd,bhrnkd->bhrnqk', qb, kw).astype(jnp.float32) * (dh ** -0.5)
    qi = jnp.arange(span)[:, None]
    kj = jnp.arange(2 * span)[None, :]
    dist = qi + span - kj
    band = (dist >= 0) & (dist <= span)
    blk = jnp.arange(n_blk)[:, None, None]
    valid = band[None] & ((blk > 0) | (kj >= span)[None])
    s = jnp.where(valid, s, -jnp.inf)
    m = jnp.max(s, axis=-1, keepdims=True)
    p = jnp.exp(s - m)
    denom = jnp.sum(p, axis=-1, keepdims=True)
    o = jnp.einsum('bhrnqk,bhrnkd->bhrnqd', p, vw.astype(jnp.float32)) / denom
    lse = (m + jnp.log(denom))[..., 0]
    o = o.reshape(B, H, dilation, n_sub, dh).transpose(0, 1, 3, 2, 4).reshape(B, H, t_pad, dh)[:, :, :T]
    lse = lse.reshape(B, H, dilation, n_sub).transpose(0, 1, 3, 2).reshape(B, H, t_pad)[:, :, :T]
    return o, lse


def stick_breaking_attention(q, k, v):
    B, H, T, dh = q.shape
    n_blk = T // Q_BLOCK
    qb = q.reshape(B, H, n_blk, Q_BLOCK, dh).transpose(2, 0, 1, 3, 4)
    kpos = jnp.arange(T)
    vf = v.astype(jnp.float32)

    def block(args):
        q_blk, b = args
        z = jnp.einsum('bhqd,bhkd->bhqk', q_blk, k).astype(jnp.float32) * (dh ** -0.5)
        qpos = b * Q_BLOCK + jnp.arange(Q_BLOCK)
        past = kpos[None, :] < qpos[:, None]
        log_beta = jax.nn.log_sigmoid(z)
        log_keep = jnp.where(past, jax.nn.log_sigmoid(-z), 0.0)
        after = lax.cumsum(log_keep, axis=3, reverse=True) - log_keep
        w = jnp.where(past, jnp.exp(log_beta + after), 0.0)
        return jnp.einsum('bhqk,bhkd->bhqd', w, vf)

    o = lax.map(block, (qb, jnp.arange(n_blk)))
    return o.transpose(1, 2, 0, 3, 4).reshape(B, H, T, dh).astype(q.dtype)


def hybrid_mixer(h, w_in, w_proj_dil, w_proj_sb, w_out, cos, sin):
    B, T, _ = h.shape
    proj = h @ w_in
    o1 = 3 * D_DIL
    o2 = o1 + 3 * D_SB
    o3 = o2 + D_MODEL
    qkv_d = proj[..., :o1].reshape(B, T, 3, N_DIL_HEADS, HEAD_DIM).transpose(2, 0, 3, 1, 4)
    q_d = apply_partial_rope(qkv_d[0], cos, sin)
    k_d = apply_partial_rope(qkv_d[1], cos, sin)
    v_d = qkv_d[2]
    outs, lses = [], []
    for g, (window, dilation) in enumerate(DIL_GROUPS):
        hs = slice(g * HEADS_PER_GROUP, (g + 1) * HEADS_PER_GROUP)
        o, lse = dilated_window_attention(q_d[:, hs], k_d[:, hs], v_d[:, hs], window, dilation)
        outs.append(o)
        lses.append(lse)
    w_grp = jax.nn.softmax(jnp.stack(lses, axis=0), axis=0)
    o_dil = jnp.sum(w_grp[..., None] * jnp.stack(outs, axis=0), axis=0).astype(h.dtype)
    o_dil = o_dil.transpose(0, 2, 1, 3).reshape(B, T, D_DIL_OUT)
    qkv_s = proj[..., o1:o2].reshape(B, T, 3, N_SB_HEADS, HEAD_DIM).transpose(2, 0, 3, 1, 4)
    o_sb = stick_breaking_attention(qkv_s[0], qkv_s[1], qkv_s[2])
    o_sb = o_sb.transpose(0, 2, 1, 3).reshape(B, T, D_SB)
    gate_dil = jax.nn.sigmoid(proj[..., o2:o3])
    gate_sb = jax.nn.sigmoid(proj[..., o3:])
    y = gate_dil * (o_dil @ w_proj_dil) + gate_sb * (o_sb @ w_proj_sb)
    return y @ w_out


def _fwd_setup_inputs(seed: int = 0) -> dict:
    key = jax.random.key(seed)
    ks = jax.random.split(key, 16)
    f32 = jnp.float32

    def dense(k, shape, fan_in):
        return jax.random.normal(k, shape, f32) * (fan_in ** -0.5)

    def gain(k, shape):
        return 1.0 + 0.05 * jax.random.normal(k, shape, f32)

    return {
        "x": jax.random.normal(ks[0], (BATCH, SEQ, D_MODEL), f32),
        "norm_ffn1": gain(ks[1], (DEPTH, D_MODEL)),
        "ffn1_w_gate": dense(ks[2], (DEPTH, D_MODEL, D_FF), D_MODEL),
        "ffn1_w_up": dense(ks[3], (DEPTH, D_MODEL, D_FF), D_MODEL),
        "ffn1_w_down": dense(ks[4], (DEPTH, D_FF, D_MODEL), D_FF),
        "norm_mix": gain(ks[5], (DEPTH, D_MODEL)),
        "w_in": dense(ks[6], (DEPTH, D_MODEL, D_IN), D_MODEL),
        "w_proj_dil": dense(ks[7], (DEPTH, D_DIL_OUT, D_MODEL), D_DIL_OUT),
        "w_proj_sb": dense(ks[8], (DEPTH, D_SB, D_MODEL), D_SB),
        "w_out": dense(ks[9], (DEPTH, D_MODEL, D_MODEL), D_MODEL),
        "norm_ffn2": gain(ks[10], (DEPTH, D_MODEL)),
        "ffn2_w_gate": dense(ks[11], (DEPTH, D_MODEL, D_FF), D_MODEL),
        "ffn2_w_up": dense(ks[12], (DEPTH, D_MODEL, D_FF), D_MODEL),
        "ffn2_w_down": dense(ks[13], (DEPTH, D_FF, D_MODEL), D_FF),
        "norm_final": gain(ks[14], (D_MODEL,)),
    }


def _fwd_reference(x, norm_ffn1, ffn1_w_gate, ffn1_w_up, ffn1_w_down, norm_mix, w_in,
              w_proj_dil, w_proj_sb, w_out, norm_ffn2, ffn2_w_gate, ffn2_w_up,
              ffn2_w_down, norm_final):
    T = x.shape[1]
    cos, sin = rope_tables(T)
    for l in range(DEPTH):
        x = x + 0.5 * swiglu(rms_norm(x, norm_ffn1[l]), ffn1_w_gate[l], ffn1_w_up[l], ffn1_w_down[l])
        x = x + hybrid_mixer(rms_norm(x, norm_mix[l]), w_in[l], w_proj_dil[l], w_proj_sb[l],
                             w_out[l], cos, sin)
        x = x + 0.5 * swiglu(rms_norm(x, norm_ffn2[l]), ffn2_w_gate[l], ffn2_w_up[l], ffn2_w_down[l])
    return rms_norm(x, norm_final)


import jax as _jax
import jax.numpy as _jnp

TWIN_FORMAT = 'train_step'
FWD_PARAMS = ['x', 'norm_ffn1', 'ffn1_w_gate', 'ffn1_w_up', 'ffn1_w_down', 'norm_mix', 'w_in', 'w_proj_dil', 'w_proj_sb', 'w_out', 'norm_ffn2', 'ffn2_w_gate', 'ffn2_w_up', 'ffn2_w_down', 'norm_final']
TWIN_WEIGHTS = ['norm_ffn1', 'ffn1_w_gate', 'ffn1_w_up', 'ffn1_w_down', 'norm_mix', 'w_in', 'w_proj_dil', 'w_proj_sb', 'w_out', 'norm_ffn2', 'ffn2_w_gate', 'ffn2_w_up', 'ffn2_w_down', 'norm_final']
TWIN_DIFF_INPUT = 'x'
TWIN_INPUTS = ['x', 'norm_ffn1', 'ffn1_w_gate', 'ffn1_w_up', 'ffn1_w_down', 'norm_mix', 'w_in', 'w_proj_dil', 'w_proj_sb', 'w_out', 'norm_ffn2', 'ffn2_w_gate', 'ffn2_w_up', 'ffn2_w_down', 'norm_final', 'loss_target', 'm_norm_ffn1', 'm_ffn1_w_gate', 'm_ffn1_w_up', 'm_ffn1_w_down', 'm_norm_mix', 'm_w_in', 'm_w_proj_dil', 'm_w_proj_sb', 'm_w_out', 'm_norm_ffn2', 'm_ffn2_w_gate', 'm_ffn2_w_up', 'm_ffn2_w_down', 'm_norm_final', 'v_norm_ffn1', 'v_ffn1_w_gate', 'v_ffn1_w_up', 'v_ffn1_w_down', 'v_norm_mix', 'v_w_in', 'v_w_proj_dil', 'v_w_proj_sb', 'v_w_out', 'v_norm_ffn2', 'v_ffn2_w_gate', 'v_ffn2_w_up', 'v_ffn2_w_down', 'v_norm_final']
TWIN_OUTPUTS = ['loss', 'grad_x', 'grad_norm_ffn1', 'grad_ffn1_w_gate', 'grad_ffn1_w_up', 'grad_ffn1_w_down', 'grad_norm_mix', 'grad_w_in', 'grad_w_proj_dil', 'grad_w_proj_sb', 'grad_w_out', 'grad_norm_ffn2', 'grad_ffn2_w_gate', 'grad_ffn2_w_up', 'grad_ffn2_w_down', 'grad_norm_final', 'delta_norm_ffn1', 'delta_ffn1_w_gate', 'delta_ffn1_w_up', 'delta_ffn1_w_down', 'delta_norm_mix', 'delta_w_in', 'delta_w_proj_dil', 'delta_w_proj_sb', 'delta_w_out', 'delta_norm_ffn2', 'delta_ffn2_w_gate', 'delta_ffn2_w_up', 'delta_ffn2_w_down', 'delta_norm_final', 'new_m_norm_ffn1', 'new_m_ffn1_w_gate', 'new_m_ffn1_w_up', 'new_m_ffn1_w_down', 'new_m_norm_mix', 'new_m_w_in', 'new_m_w_proj_dil', 'new_m_w_proj_sb', 'new_m_w_out', 'new_m_norm_ffn2', 'new_m_ffn2_w_gate', 'new_m_ffn2_w_up', 'new_m_ffn2_w_down', 'new_m_norm_final', 'new_v_norm_ffn1', 'new_v_ffn1_w_gate', 'new_v_ffn1_w_up', 'new_v_ffn1_w_down', 'new_v_norm_mix', 'new_v_w_in', 'new_v_w_proj_dil', 'new_v_w_proj_sb', 'new_v_w_out', 'new_v_norm_ffn2', 'new_v_ffn2_w_gate', 'new_v_ffn2_w_up', 'new_v_ffn2_w_down', 'new_v_norm_final']
TWIN_LEAF_KINDS = {'loss': 'loss', 'grad_x': 'grad_x', 'grad_norm_ffn1': 'grad_w', 'grad_ffn1_w_gate': 'grad_w', 'grad_ffn1_w_up': 'grad_w', 'grad_ffn1_w_down': 'grad_w', 'grad_norm_mix': 'grad_w', 'grad_w_in': 'grad_w', 'grad_w_proj_dil': 'grad_w', 'grad_w_proj_sb': 'grad_w', 'grad_w_out': 'grad_w', 'grad_norm_ffn2': 'grad_w', 'grad_ffn2_w_gate': 'grad_w', 'grad_ffn2_w_up': 'grad_w', 'grad_ffn2_w_down': 'grad_w', 'grad_norm_final': 'grad_w', 'delta_norm_ffn1': 'delta_w', 'delta_ffn1_w_gate': 'delta_w', 'delta_ffn1_w_up': 'delta_w', 'delta_ffn1_w_down': 'delta_w', 'delta_norm_mix': 'delta_w', 'delta_w_in': 'delta_w', 'delta_w_proj_dil': 'delta_w', 'delta_w_proj_sb': 'delta_w', 'delta_w_out': 'delta_w', 'delta_norm_ffn2': 'delta_w', 'delta_ffn2_w_gate': 'delta_w', 'delta_ffn2_w_up': 'delta_w', 'delta_ffn2_w_down': 'delta_w', 'delta_norm_final': 'delta_w', 'new_m_norm_ffn1': 'new_m', 'new_m_ffn1_w_gate': 'new_m', 'new_m_ffn1_w_up': 'new_m', 'new_m_ffn1_w_down': 'new_m', 'new_m_norm_mix': 'new_m', 'new_m_w_in': 'new_m', 'new_m_w_proj_dil': 'new_m', 'new_m_w_proj_sb': 'new_m', 'new_m_w_out': 'new_m', 'new_m_norm_ffn2': 'new_m', 'new_m_ffn2_w_gate': 'new_m', 'new_m_ffn2_w_up': 'new_m', 'new_m_ffn2_w_down': 'new_m', 'new_m_norm_final': 'new_m', 'new_v_norm_ffn1': 'new_v', 'new_v_ffn1_w_gate': 'new_v', 'new_v_ffn1_w_up': 'new_v', 'new_v_ffn1_w_down': 'new_v', 'new_v_norm_mix': 'new_v', 'new_v_w_in': 'new_v', 'new_v_w_proj_dil': 'new_v', 'new_v_w_proj_sb': 'new_v', 'new_v_w_out': 'new_v', 'new_v_norm_ffn2': 'new_v', 'new_v_ffn2_w_gate': 'new_v', 'new_v_ffn2_w_up': 'new_v', 'new_v_ffn2_w_down': 'new_v', 'new_v_norm_final': 'new_v'}


def _forward(args):
    return _fwd_reference(*[args[k] for k in FWD_PARAMS])


def _output_shape():
    def fwd():
        inp = _fwd_setup_inputs(0)
        return _fwd_reference(*[inp[k] for k in FWD_PARAMS])
    out = _jax.eval_shape(fwd)
    return out.shape, out.dtype

N_MICROBATCH = 1
ADAM_LR = 0.001
ADAM_B1 = 0.9
ADAM_B2 = 0.999
ADAM_EPS = 1e-08
ADAM_WD = 0.01
ADAM_STEP = 10
PER_EXAMPLE_BATCH_AXIS = {'x': 0, 'loss_target': 0}
SHARED_INPUTS = []
_WEIGHT_DTYPES = {'norm_ffn1': _jnp.float32, 'ffn1_w_gate': _jnp.float32, 'ffn1_w_up': _jnp.float32, 'ffn1_w_down': _jnp.float32, 'norm_mix': _jnp.float32, 'w_in': _jnp.float32, 'w_proj_dil': _jnp.float32, 'w_proj_sb': _jnp.float32, 'w_out': _jnp.float32, 'norm_ffn2': _jnp.float32, 'ffn2_w_gate': _jnp.float32, 'ffn2_w_up': _jnp.float32, 'ffn2_w_down': _jnp.float32, 'norm_final': _jnp.float32}
MOMENT_SCALE = {'norm_ffn1': 8.180850e-02, 'ffn1_w_gate': 3.369193e-02, 'ffn1_w_up': 3.265168e-02, 'ffn1_w_down': 5.407230e-02, 'norm_mix': 8.151876e-02, 'w_in': 3.507670e-02, 'w_proj_dil': 1.742200e-02, 'w_proj_sb': 5.959321e-02, 'w_out': 6.095253e-02, 'norm_ffn2': 6.880589e-02, 'ffn2_w_gate': 2.908169e-02, 'ffn2_w_up': 2.835792e-02, 'ffn2_w_down': 4.701067e-02, 'norm_final': 3.205208e+01}


def _to_microbatches(a, axis):
    t = _jnp.moveaxis(a, axis, 0)
    t = t.reshape((N_MICROBATCH, t.shape[0] // N_MICROBATCH) + t.shape[1:])
    return _jnp.moveaxis(t, 1, axis + 1)


def setup_inputs(seed: int = 0) -> dict:
    inp = _fwd_setup_inputs(seed)
    key = _jax.random.fold_in(_jax.random.key(seed), 7919)
    shape, _ = _output_shape()
    out = dict(inp)
    out["loss_target"] = _jax.random.normal(_jax.random.fold_in(key, 0), shape, _jnp.float32)
    for i, name in enumerate(TWIN_WEIGHTS):
        w = inp[name].astype(_jnp.float32)
        if MOMENT_SCALE is None:
            s = _jnp.sqrt(_jnp.mean(_jnp.square(w)) + 1e-30)
        else:
            s = MOMENT_SCALE[name]
        km, kv = _jax.random.split(_jax.random.fold_in(key, i + 1))
        out[name] = w
        out["m_" + name] = s * _jax.random.normal(km, w.shape, _jnp.float32)
        out["v_" + name] = (s * s) * _jax.random.uniform(kv, w.shape, _jnp.float32, 0.5, 1.5)
    if N_MICROBATCH > 1:
        for name, axis in PER_EXAMPLE_BATCH_AXIS.items():
            out[name] = _to_microbatches(out[name], axis)
    return {'x': out['x'], 'norm_ffn1': out['norm_ffn1'], 'ffn1_w_gate': out['ffn1_w_gate'], 'ffn1_w_up': out['ffn1_w_up'], 'ffn1_w_down': out['ffn1_w_down'], 'norm_mix': out['norm_mix'], 'w_in': out['w_in'], 'w_proj_dil': out['w_proj_dil'], 'w_proj_sb': out['w_proj_sb'], 'w_out': out['w_out'], 'norm_ffn2': out['norm_ffn2'], 'ffn2_w_gate': out['ffn2_w_gate'], 'ffn2_w_up': out['ffn2_w_up'], 'ffn2_w_down': out['ffn2_w_down'], 'norm_final': out['norm_final'], 'loss_target': out['loss_target'], 'm_norm_ffn1': out['m_norm_ffn1'], 'm_ffn1_w_gate': out['m_ffn1_w_gate'], 'm_ffn1_w_up': out['m_ffn1_w_up'], 'm_ffn1_w_down': out['m_ffn1_w_down'], 'm_norm_mix': out['m_norm_mix'], 'm_w_in': out['m_w_in'], 'm_w_proj_dil': out['m_w_proj_dil'], 'm_w_proj_sb': out['m_w_proj_sb'], 'm_w_out': out['m_w_out'], 'm_norm_ffn2': out['m_norm_ffn2'], 'm_ffn2_w_gate': out['m_ffn2_w_gate'], 'm_ffn2_w_up': out['m_ffn2_w_up'], 'm_ffn2_w_down': out['m_ffn2_w_down'], 'm_norm_final': out['m_norm_final'], 'v_norm_ffn1': out['v_norm_ffn1'], 'v_ffn1_w_gate': out['v_ffn1_w_gate'], 'v_ffn1_w_up': out['v_ffn1_w_up'], 'v_ffn1_w_down': out['v_ffn1_w_down'], 'v_norm_mix': out['v_norm_mix'], 'v_w_in': out['v_w_in'], 'v_w_proj_dil': out['v_w_proj_dil'], 'v_w_proj_sb': out['v_w_proj_sb'], 'v_w_out': out['v_w_out'], 'v_norm_ffn2': out['v_norm_ffn2'], 'v_ffn2_w_gate': out['v_ffn2_w_gate'], 'v_ffn2_w_up': out['v_ffn2_w_up'], 'v_ffn2_w_down': out['v_ffn2_w_down'], 'v_norm_final': out['v_norm_final']}


def _loss(weights, diff, rest, loss_target):
    with _jax.named_scope("forward"):
        args = {**rest, TWIN_DIFF_INPUT: diff, **{k: w.astype(_WEIGHT_DTYPES[k]) for k, w in weights.items()}}
        y = _forward(args)
    with _jax.named_scope("loss_head"):
        err = _jnp.square(y.astype(_jnp.float32) - loss_target)
        return 0.5 * _jnp.sum(_jnp.mean(err, axis=-1)) if err.ndim else 0.5 * err


def _adamw(w, g, m, v):
    m = ADAM_B1 * m + (1.0 - ADAM_B1) * g
    v = ADAM_B2 * v + (1.0 - ADAM_B2) * _jnp.square(g)
    m_hat = m / (1.0 - ADAM_B1 ** ADAM_STEP)
    v_hat = v / (1.0 - ADAM_B2 ** ADAM_STEP)
    delta = -ADAM_LR * (m_hat / (_jnp.sqrt(v_hat) + ADAM_EPS) + ADAM_WD * w)
    return delta, m, v


def reference(x, norm_ffn1, ffn1_w_gate, ffn1_w_up, ffn1_w_down, norm_mix, w_in, w_proj_dil, w_proj_sb, w_out, norm_ffn2, ffn2_w_gate, ffn2_w_up, ffn2_w_down, norm_final, loss_target, m_norm_ffn1, m_ffn1_w_gate, m_ffn1_w_up, m_ffn1_w_down, m_norm_mix, m_w_in, m_w_proj_dil, m_w_proj_sb, m_w_out, m_norm_ffn2, m_ffn2_w_gate, m_ffn2_w_up, m_ffn2_w_down, m_norm_final, v_norm_ffn1, v_ffn1_w_gate, v_ffn1_w_up, v_ffn1_w_down, v_norm_mix, v_w_in, v_w_proj_dil, v_w_proj_sb, v_w_out, v_norm_ffn2, v_ffn2_w_gate, v_ffn2_w_up, v_ffn2_w_down, v_norm_final):
    given = dict(x=x, norm_ffn1=norm_ffn1, ffn1_w_gate=ffn1_w_gate, ffn1_w_up=ffn1_w_up, ffn1_w_down=ffn1_w_down, norm_mix=norm_mix, w_in=w_in, w_proj_dil=w_proj_dil, w_proj_sb=w_proj_sb, w_out=w_out, norm_ffn2=norm_ffn2, ffn2_w_gate=ffn2_w_gate, ffn2_w_up=ffn2_w_up, ffn2_w_down=ffn2_w_down, norm_final=norm_final, loss_target=loss_target, m_norm_ffn1=m_norm_ffn1, m_ffn1_w_gate=m_ffn1_w_gate, m_ffn1_w_up=m_ffn1_w_up, m_ffn1_w_down=m_ffn1_w_down, m_norm_mix=m_norm_mix, m_w_in=m_w_in, m_w_proj_dil=m_w_proj_dil, m_w_proj_sb=m_w_proj_sb, m_w_out=m_w_out, m_norm_ffn2=m_norm_ffn2, m_ffn2_w_gate=m_ffn2_w_gate, m_ffn2_w_up=m_ffn2_w_up, m_ffn2_w_down=m_ffn2_w_down, m_norm_final=m_norm_final, v_norm_ffn1=v_norm_ffn1, v_ffn1_w_gate=v_ffn1_w_gate, v_ffn1_w_up=v_ffn1_w_up, v_ffn1_w_down=v_ffn1_w_down, v_norm_mix=v_norm_mix, v_w_in=v_w_in, v_w_proj_dil=v_w_proj_dil, v_w_proj_sb=v_w_proj_sb, v_w_out=v_w_out, v_norm_ffn2=v_norm_ffn2, v_ffn2_w_gate=v_ffn2_w_gate, v_ffn2_w_up=v_ffn2_w_up, v_ffn2_w_down=v_ffn2_w_down, v_norm_final=v_norm_final)
    weights = {n: given[n] for n in TWIN_WEIGHTS}
    shared = {n: given[n] for n in SHARED_INPUTS}
    per_example = {n: given[n] for n in ['x']}
    grad_fn = _jax.value_and_grad(_loss, argnums=(0, 1))

    def one_microbatch(ex, loss_target):
        ex = dict(ex)
        diff = ex.pop(TWIN_DIFF_INPUT)
        return grad_fn(weights, diff, {**shared, **ex}, loss_target)

    if N_MICROBATCH == 1:
        loss, (grad_w, grad_x) = one_microbatch(per_example, given["loss_target"])
    else:
        def body(carry, xs):
            loss_sum, grad_sum = carry
            l_k, (gw_k, gx_k) = one_microbatch(xs[0], xs[1])
            with _jax.named_scope("update"):
                return (loss_sum + l_k, _jax.tree.map(_jnp.add, grad_sum, gw_k)), gx_k

        init = (_jnp.zeros((), _jnp.float32), _jax.tree.map(_jnp.zeros_like, weights))
        (loss, grad_w), grad_x = _jax.lax.scan(body, init, (per_example, given["loss_target"]))
    with _jax.named_scope("update"):
        delta_w, new_m, new_v = {}, {}, {}
        for n in TWIN_WEIGHTS:
            delta_w[n], new_m[n], new_v[n] = _adamw(weights[n], grad_w[n], given["m_" + n], given["v_" + n])
    return (loss, grad_x, *[grad_w[n] for n in TWIN_WEIGHTS], *[delta_w[n] for n in TWIN_WEIGHTS],
            *[new_m[n] for n in TWIN_WEIGHTS], *[new_v[n] for n in TWIN_WEIGHTS])
```

```python
import functools

import jax
import jax.numpy as jnp
from jax import lax
from jax.experimental import pallas as pl
from jax.experimental.pallas import tpu as pltpu

F32 = jnp.float32
BF16 = jnp.bfloat16

D_MODEL = 1024
HEAD_DIM = 64
GROUP_W = 256
D_IN = 5120
N_DIL_GROUPS = 3
DIL_SPAN = 128
DILATIONS = (1, 4, 16)
ROPE_THETA = 500000.0
ROPE_DIM = 16
RMS_EPS = 1e-6
ATT_SCALE = HEAD_DIM ** -0.5
QS_BLK, KS_BLK, VS_BLK = 9, 10, 11
GATE_DIL_BLK, GATE_SB_BLK = 3, 4

ADAM_LR, ADAM_B1, ADAM_B2, ADAM_EPS, ADAM_WD, ADAM_STEP = 0.001, 0.9, 0.999, 1e-08, 0.01, 10

N_DEV = 8
LANES = 1024
VMEM_PHYSICAL_V7X = 64 << 20
VMEM_TEMP_HEADROOM = 20 << 20

PACK_LAYOUT = (
    ("ffn1_w_gate", 352, True, (1024, 2816)),
    ("ffn1_w_up", 352, True, (1024, 2816)),
    ("ffn1_w_down", 352, False, (2816, 1024)),
    ("w_in", 640, True, (1024, 5120)),
    ("w_proj_dil", 32, True, (256, 1024)),
    ("w_proj_sb", 32, True, (256, 1024)),
    ("w_out", 128, False, (1024, 1024)),
    ("ffn2_w_gate", 352, True, (1024, 2816)),
    ("ffn2_w_up", 352, True, (1024, 2816)),
    ("ffn2_w_down", 352, False, (2816, 1024)),
)
ROWS_PER_LAYER = sum(r for _, r, _, _ in PACK_LAYOUT)
NORM_ROWS = ("norm_ffn1", "norm_mix", "norm_ffn2")


def _nbytes(shape, dtype):
    n = 1
    for s in shape:
        n *= s
    return n * jnp.dtype(dtype).itemsize


def _pcall(body, *, name, grid, in_specs, out_specs, out_shape, blocks, scratch_shapes=(), scratch_bytes=0):
    need = 2 * sum(_nbytes(s, d) for s, d in blocks) + scratch_bytes + VMEM_TEMP_HEADROOM
    limit = min(need, VMEM_PHYSICAL_V7X - (4 << 20))
    return pl.pallas_call(
        body, name=name, grid=grid, in_specs=in_specs, out_specs=out_specs, out_shape=out_shape,
        scratch_shapes=scratch_shapes,
        compiler_params=pltpu.CompilerParams(vmem_limit_bytes=limit),
    )


def _dot(a, b, form):
    dn = {"nn": (((1,), (0,)), ((), ())), "nt": (((1,), (1,)), ((), ())), "tn": (((0,), (0,)), ((), ()))}[form]
    return lax.dot_general(a.astype(BF16), b.astype(BF16), dn, preferred_element_type=F32)


def _sigmoid(x):
    return 1.0 / (1.0 + jnp.exp(-x))


def matmul(pairs, form, *, tm, tn, tk, out_dtype, name, scale=1.0, res=None):
    a0, b0 = pairs[0]
    if form == "tn":
        kdim, m = a0.shape
        n = b0.shape[1]
    else:
        m, kdim = a0.shape
        n = b0.shape[1] if form == "nn" else b0.shape[0]
    tm, tn, tk = min(tm, m), min(tn, n), min(tk, kdim)
    assert m % tm == 0 and n % tn == 0 and kdim % tk == 0, (name, m, n, kdim, tm, tn, tk)
    nk = kdim // tk
    npairs = len(pairs)

    if form == "tn":
        a_blk, a_map = (tk, tm), (lambda j, i, k: (k, i))
    else:
        a_blk, a_map = (tm, tk), (lambda j, i, k: (i, k))
    if form == "nt":
        b_blk, b_map = (tn, tk), (lambda j, i, k: (j, k))
    else:
        b_blk, b_map = (tk, tn), (lambda j, i, k: (k, j))
    o_map = lambda j, i, k: (i, j)

    def body(*refs):
        ab = refs[:2 * npairs]
        rest = refs[2 * npairs:]
        if res is not None:
            r_ref, o_ref = rest[0], rest[1]
            rest = rest[2:]
        else:
            r_ref, o_ref = None, rest[0]
            rest = rest[1:]

        def partial_sum():
            p = _dot(ab[0][...], ab[1][...], form)
            for q in range(1, npairs):
                p = p + _dot(ab[2 * q][...], ab[2 * q + 1][...], form)
            return p

        def finish(acc):
            out = acc * scale if scale != 1.0 else acc
            if r_ref is not None:
                out = r_ref[...] + out
            o_ref[...] = out.astype(out_dtype)

        if nk == 1:
            finish(partial_sum())
        else:
            acc_ref = rest[0]
            k = pl.program_id(2)

            @pl.when(k == 0)
            def _():
                acc_ref[...] = partial_sum()

            @pl.when(k > 0)
            def _():
                acc_ref[...] += partial_sum()

            @pl.when(k == nk - 1)
            def _():
                finish(acc_ref[...])

    in_specs, args, blocks = [], [], []
    for a, b in pairs:
        in_specs += [pl.BlockSpec(a_blk, a_map), pl.BlockSpec(b_blk, b_map)]
        args += [a, b]
        blocks += [(a_blk, a.dtype), (b_blk, b.dtype)]
    if res is not None:
        in_specs.append(pl.BlockSpec((tm, tn), o_map))
        args.append(res)
        blocks.append(((tm, tn), res.dtype))
    blocks.append(((tm, tn), out_dtype))
    scratch = [pltpu.VMEM((tm, tn), F32)] if nk > 1 else []
    return _pcall(
        body, name=name, grid=(n // tn, m // tm, nk), in_specs=in_specs,
        out_specs=pl.BlockSpec((tm, tn), o_map), out_shape=jax.ShapeDtypeStruct((m, n), out_dtype),
        blocks=blocks, scratch_shapes=scratch, scratch_bytes=(tm * tn * 4 if nk > 1 else 0),
    )(*args)


def swiglu_fwd(h, wg, wu, *, name, tm=512, tn=1408):
    t, d = h.shape
    f = wg.shape[1]
    tm, tn = min(tm, t), min(tn, f)

    def body(h_ref, wg_ref, wu_ref, a_ref, b_ref, s_ref):
        hh = h_ref[...]
        a = _dot(hh, wg_ref[...], "nn")
        b = _dot(hh, wu_ref[...], "nn")
        a_ref[...] = a.astype(BF16)
        b_ref[...] = b.astype(BF16)
        s_ref[...] = (a * _sigmoid(a) * b).astype(BF16)

    w_spec = pl.BlockSpec((d, tn), lambda j, i: (0, j))
    o_spec = pl.BlockSpec((tm, tn), lambda j, i: (i, j))
    o_shape = jax.ShapeDtypeStruct((t, f), BF16)
    return _pcall(
        body, name=name, grid=(f // tn, t // tm),
        in_specs=[pl.BlockSpec((tm, d), lambda j, i: (i, 0)), w_spec, w_spec],
        out_specs=[o_spec, o_spec, o_spec], out_shape=[o_shape, o_shape, o_shape],
        blocks=[((tm, d), BF16), ((d, tn), BF16), ((d, tn), BF16)] + [((tm, tn), BF16)] * 3,
    )(h, wg, wu)


def swiglu_bwd(dyb, wd, a, b, *, name, scale, tm=512, tn=1408):
    t, d = dyb.shape
    f = wd.shape[0]
    tm, tn = min(tm, t), min(tn, f)

    def body(dy_ref, wd_ref, a_ref, b_ref, da_ref, db_ref):
        ds = _dot(dy_ref[...], wd_ref[...], "nt") * scale
        av = a_ref[...].astype(F32)
        bv = b_ref[...].astype(F32)
        sg = _sigmoid(av)
        da_ref[...] = (ds * bv * (sg * (1.0 + av * (1.0 - sg)))).astype(BF16)
        db_ref[...] = (ds * (av * sg)).astype(BF16)

    o_spec = pl.BlockSpec((tm, tn), lambda j, i: (i, j))
    o_shape = jax.ShapeDtypeStruct((t, f), BF16)
    return _pcall(
        body, name=name, grid=(f // tn, t // tm),
        in_specs=[pl.BlockSpec((tm, d), lambda j, i: (i, 0)), pl.BlockSpec((tn, d), lambda j, i: (j, 0)), o_spec, o_spec],
        out_specs=[o_spec, o_spec], out_shape=[o_shape, o_shape],
        blocks=[((tm, d), BF16), ((tn, d), BF16)] + [((tm, tn), BF16)] * 4,
    )(dyb, wd, a, b)


def gate_fwd(odil, osb, wpd, wps, proj, *, name, tm=512):
    t = odil.shape[0]
    tm = min(tm, t)

    def body(od_ref, os_ref, wpd_ref, wps_ref, g1_ref, g2_ref, y_ref, u1_ref, u2_ref):
        u1 = _dot(od_ref[...], wpd_ref[...], "nn")
        u2 = _dot(os_ref[...], wps_ref[...], "nn")
        y = _sigmoid(g1_ref[...].astype(F32)) * u1 + _sigmoid(g2_ref[...].astype(F32)) * u2
        y_ref[...] = y.astype(BF16)
        u1_ref[...] = u1.astype(BF16)
        u2_ref[...] = u2.astype(BF16)

    o_spec = pl.BlockSpec((tm, D_MODEL), lambda i: (i, 0))
    w_spec = pl.BlockSpec((GROUP_W, D_MODEL), lambda i: (0, 0))
    a_spec = pl.BlockSpec((tm, GROUP_W), lambda i: (i, 0))
    o_shape = jax.ShapeDtypeStruct((t, D_MODEL), BF16)
    return _pcall(
        body, name=name, grid=(t // tm,),
        in_specs=[a_spec, a_spec, w_spec, w_spec,
                  pl.BlockSpec((tm, D_MODEL), lambda i: (i, GATE_DIL_BLK)),
                  pl.BlockSpec((tm, D_MODEL), lambda i: (i, GATE_SB_BLK))],
        out_specs=[o_spec, o_spec, o_spec], out_shape=[o_shape, o_shape, o_shape],
        blocks=[((tm, GROUP_W), F32)] * 2 + [((GROUP_W, D_MODEL), BF16)] * 2 + [((tm, D_MODEL), BF16)] * 5,
    )(odil, osb, wpd, wps, proj, proj)


def gate_bwd(dxb, wout, u1, u2, proj, *, name, tm=512):
    t = dxb.shape[0]
    tm = min(tm, t)

    def body(dx_ref, w_ref, u1_ref, u2_ref, g1_ref, g2_ref, du1_ref, du2_ref, dg_ref):
        dy = _dot(dx_ref[...], w_ref[...], "nt")
        s1 = _sigmoid(g1_ref[...].astype(F32))
        s2 = _sigmoid(g2_ref[...].astype(F32))
        du1_ref[...] = (dy * s1).astype(BF16)
        du2_ref[...] = (dy * s2).astype(BF16)
        dg_ref[:, :D_MODEL] = (dy * u1_ref[...].astype(F32) * s1 * (1.0 - s1)).astype(BF16)
        dg_ref[:, D_MODEL:] = (dy * u2_ref[...].astype(F32) * s2 * (1.0 - s2)).astype(BF16)

    o_spec = pl.BlockSpec((tm, D_MODEL), lambda i: (i, 0))
    o_shape = jax.ShapeDtypeStruct((t, D_MODEL), BF16)
    return _pcall(
        body, name=name, grid=(t // tm,),
        in_specs=[o_spec, pl.BlockSpec((D_MODEL, D_MODEL), lambda i: (0, 0)), o_spec, o_spec,
                  pl.BlockSpec((tm, D_MODEL), lambda i: (i, GATE_DIL_BLK)),
                  pl.BlockSpec((tm, D_MODEL), lambda i: (i, GATE_SB_BLK))],
        out_specs=[o_spec, o_spec, pl.BlockSpec((tm, 2 * D_MODEL), lambda i: (i, 0))],
        out_shape=[o_shape, o_shape, jax.ShapeDtypeStruct((t, 2 * D_MODEL), BF16)],
        blocks=[((tm, D_MODEL), BF16)] * 9 + [((D_MODEL, D_MODEL), BF16)],
    )(dxb, wout, u1, u2, proj, proj)


def rms_fwd(x, gain, *, name, tm=512):
    t, d = x.shape
    tm = min(tm, t)

    def body(x_ref, g_ref, h_ref):
        xv = x_ref[...]
        rstd = lax.rsqrt(jnp.mean(xv * xv, axis=1, keepdims=True) + RMS_EPS)
        h_ref[...] = (xv * rstd * g_ref[...]).astype(BF16)

    return _pcall(
        body, name=name, grid=(t // tm,),
        in_specs=[pl.BlockSpec((tm, d), lambda i: (i, 0)), pl.BlockSpec((1, d), lambda i: (0, 0))],
        out_specs=pl.BlockSpec((tm, d), lambda i: (i, 0)), out_shape=jax.ShapeDtypeStruct((t, d), BF16),
        blocks=[((tm, d), F32), ((tm, d), BF16)],
    )(x, gain)


def rms_bwd(dh, x, gain, dres, *, name, tm=512):
    t, d = x.shape
    tm = min(tm, t)

    def body(dh_ref, x_ref, g_ref, dr_ref, dx_ref, dxb_ref, dg_ref):
        xv = x_ref[...]
        dhv = dh_ref[...]
        rstd = lax.rsqrt(jnp.mean(xv * xv, axis=1, keepdims=True) + RMS_EPS)
        xh = xv * rstd
        dxh = dhv * g_ref[...]
        dx = dr_ref[...] + rstd * (dxh - xh * jnp.mean(dxh * xh, axis=1, keepdims=True))
        dx_ref[...] = dx
        dxb_ref[...] = dx.astype(BF16)
        part = jnp.sum(dhv * xh, axis=0, keepdims=True)

        @pl.when(pl.program_id(0) == 0)
        def _():
            dg_ref[...] = part

        @pl.when(pl.program_id(0) > 0)
        def _():
            dg_ref[...] += part

    row = pl.BlockSpec((tm, d), lambda i: (i, 0))
    vec = pl.BlockSpec((1, d), lambda i: (0, 0))
    return _pcall(
        body, name=name, grid=(t // tm,), in_specs=[row, row, vec, row], out_specs=[row, row, vec],
        out_shape=[jax.ShapeDtypeStruct((t, d), F32), jax.ShapeDtypeStruct((t, d), BF16), jax.ShapeDtypeStruct((1, d), F32)],
        blocks=[((tm, d), F32)] * 4 + [((tm, d), BF16)],
    )(dh, x, gain, dres)


def final_loss(x, gain, target, *, name, tm=512):
    t, d = x.shape
    tm = min(tm, t)

    def body(x_ref, g_ref, t_ref, dx_ref, dxb_ref, dg_ref, loss_ref):
        xv = x_ref[...]
        g = g_ref[...]
        rstd = lax.rsqrt(jnp.mean(xv * xv, axis=1, keepdims=True) + RMS_EPS)
        xh = xv * rstd
        err = xh * g - t_ref[...]
        dy = err * (1.0 / d)
        dxh = dy * g
        dx = rstd * (dxh - xh * jnp.mean(dxh * xh, axis=1, keepdims=True))
        dx_ref[...] = dx
        dxb_ref[...] = dx.astype(BF16)
        part = jnp.sum(dy * xh, axis=0, keepdims=True)
        sq = jnp.sum(jnp.sum(err * err, axis=1, keepdims=True), axis=0, keepdims=True) * (0.5 / d)
        lpart = jnp.broadcast_to(sq, (1, 128))

        @pl.when(pl.program_id(0) == 0)
        def _():
            dg_ref[...] = part
            loss_ref[...] = lpart

        @pl.when(pl.program_id(0) > 0)
        def _():
            dg_ref[...] += part
            loss_ref[...] += lpart

    row = pl.BlockSpec((tm, d), lambda i: (i, 0))
    vec = pl.BlockSpec((1, d), lambda i: (0, 0))
    return _pcall(
        body, name=name, grid=(t // tm,), in_specs=[row, vec, row],
        out_specs=[row, row, vec, pl.BlockSpec((1, 128), lambda i: (0, 0))],
        out_shape=[jax.ShapeDtypeStruct((t, d), F32), jax.ShapeDtypeStruct((t, d), BF16),
                   jax.ShapeDtypeStruct((1, d), F32), jax.ShapeDtypeStruct((1, 128), F32)],
        blocks=[((tm, d), F32)] * 3 + [((tm, d), BF16)],
    )(x, gain, target)


def rope_tables(t):
    pos = jnp.arange(t, dtype=F32)
    inv_freq = ROPE_THETA ** (-jnp.arange(0, ROPE_DIM, 2, dtype=F32) / ROPE_DIM)
    ang = pos[:, None] * inv_freq[None, :]
    cos, sin = jnp.cos(ang), jnp.sin(ang)
    half = ROPE_DIM // 2
    pad = HEAD_DIM - ROPE_DIM
    one_head = lambda lo, hi, fill: jnp.concatenate([lo, hi, jnp.full((t, pad), fill, F32)], axis=1)
    zeros = jnp.zeros((t, half), F32)
    c = one_head(cos, cos, 1.0)
    sa = one_head(-sin, zeros, 0.0)
    sb = one_head(zeros, sin, 0.0)
    two = lambda a: jnp.concatenate([a, a], axis=1)
    return two(c), two(sa), two(sb)


def rope_apply(pieces, tables, sign, *, name, tm=512):
    c, sa, sb = tables
    t = c.shape[0]
    tm = min(tm, t)
    n = len(pieces)

    def body(*refs):
        c_ref, sa_ref, sb_ref = refs[n:n + 3]
        o_ref = refs[n + 3]
        cv = c_ref[...]
        sav = sa_ref[...] * sign
        sbv = sb_ref[...] * sign
        for p in range(n):
            for half in range(2):
                xv = refs[p][:, 128 * half:128 * (half + 1)].astype(F32)
                out = xv * cv + pltpu.roll(xv, 120, 1) * sav + pltpu.roll(xv, 8, 1) * sbv
                o_ref[:, GROUP_W * p + 128 * half:GROUP_W * p + 128 * (half + 1)] = out.astype(BF16)

    tab = pl.BlockSpec((tm, 128), lambda i: (i, 0))
    in_specs = [pl.BlockSpec((tm, GROUP_W), functools.partial(lambda i, cb: (i, cb), cb=cb)) for _, cb in pieces]
    return _pcall(
        body, name=name, grid=(t // tm,), in_specs=in_specs + [tab, tab, tab],
        out_specs=pl.BlockSpec((tm, GROUP_W * n), lambda i: (i, 0)),
        out_shape=jax.ShapeDtypeStruct((t, GROUP_W * n), BF16),
        blocks=[((tm, GROUP_W), F32)] * n + [((tm, 128), F32)] * 3 + [((tm, GROUP_W * n), BF16)],
    )(*[a for a, _ in pieces], c, sa, sb)


def _head_mask(h):
    lane = lax.broadcasted_iota(jnp.int32, (1, GROUP_W), 1)
    return (lane // HEAD_DIM) == h


def _band_masks():
    ri = lax.broadcasted_iota(jnp.int32, (DIL_SPAN, DIL_SPAN), 0)
    ci = lax.broadcasted_iota(jnp.int32, (DIL_SPAN, DIL_SPAN), 1)
    return ci <= ri, ci >= ri


def dil_fwd(qkr, proj, g, *, name):
    t = qkr.shape[0]
    d = DILATIONS[g]
    nsub = t // d
    nblk = nsub // DIL_SPAN
    qk2 = qkr.reshape(nsub, d * 6 * GROUP_W)
    pj2 = proj.reshape(nsub, d * D_IN)
    vcol = 6 + g
    pjw = D_IN // GROUP_W

    def body(q_ref, kc_ref, kp_ref, vc_ref, vp_ref, o_ref, lse_ref):
        nb = pl.program_id(0) % nblk
        own, prev = _band_masks()
        prev = prev & (nb > 0)
        q, kc, kp, vc, vp = q_ref[...], kc_ref[...], kp_ref[...], vc_ref[...], vp_ref[...]
        o_acc = jnp.zeros((DIL_SPAN, GROUP_W), F32)
        for h in range(4):
            hm = _head_mask(h)
            qh = jnp.where(hm, q, jnp.zeros_like(q))
            sc = jnp.where(own, _dot(qh, kc, "nt") * ATT_SCALE, -jnp.inf)
            sp = jnp.where(prev, _dot(qh, kp, "nt") * ATT_SCALE, -jnp.inf)
            m = jnp.maximum(jnp.max(sc, axis=1, keepdims=True), jnp.max(sp, axis=1, keepdims=True))
            pc = jnp.exp(sc - m)
            pp = jnp.exp(sp - m)
            den = jnp.sum(pc, axis=1, keepdims=True) + jnp.sum(pp, axis=1, keepdims=True)
            oh = (_dot(pc, vc, "nn") + _dot(pp, vp, "nn")) / den
            o_acc = jnp.where(hm, oh, o_acc)
            lse_ref[:, 128 * h:128 * (h + 1)] = jnp.broadcast_to(m + jnp.log(den), (DIL_SPAN, 128))
        o_ref[...] = o_acc

    blk = (DIL_SPAN, GROUP_W)
    cur = lambda i: i % nblk
    prv = lambda i: jnp.maximum(i % nblk - 1, 0)
    res = lambda i: i // nblk
    return _pcall(
        body, name=name, grid=(d * nblk,),
        in_specs=[pl.BlockSpec(blk, lambda i: (cur(i), res(i) * 6 + g)),
                  pl.BlockSpec(blk, lambda i: (cur(i), res(i) * 6 + 3 + g)),
                  pl.BlockSpec(blk, lambda i: (prv(i), res(i) * 6 + 3 + g)),
                  pl.BlockSpec(blk, lambda i: (cur(i), res(i) * pjw + vcol)),
                  pl.BlockSpec(blk, lambda i: (prv(i), res(i) * pjw + vcol))],
        out_specs=[pl.BlockSpec(blk, lambda i: (cur(i), res(i))),
                   pl.BlockSpec((DIL_SPAN, 512), lambda i: (cur(i), res(i)))],
        out_shape=[jax.ShapeDtypeStruct((nsub, d * GROUP_W), F32), jax.ShapeDtypeStruct((nsub, d * 512), F32)],
        blocks=[(blk, BF16)] * 5 + [(blk, F32), ((DIL_SPAN, 512), F32)],
    )(qk2, qk2, qk2, pj2, pj2)


def dil_merge(outs, lses, *, name, tm=512):
    t = outs[0].shape[0]
    tm = min(tm, t)

    def body(o0, o1, o2, l0, l1, l2, o_ref, lse_ref):
        ls = [l0[...], l1[...], l2[...]]
        m = jnp.maximum(jnp.maximum(ls[0], ls[1]), ls[2])
        tot = m + jnp.log(jnp.exp(ls[0] - m) + jnp.exp(ls[1] - m) + jnp.exp(ls[2] - m))
        lse_ref[...] = tot
        lane = lax.broadcasted_iota(jnp.int32, (1, 128), 1)
        first = lane < HEAD_DIM
        acc = jnp.zeros((tm, GROUP_W), F32)
        for og, lg in zip((o0, o1, o2), ls):
            w = jnp.exp(lg - tot)
            wide = jnp.concatenate([jnp.where(first, w[:, 0:128], w[:, 128:256]),
                                    jnp.where(first, w[:, 256:384], w[:, 384:512])], axis=1)
            acc = acc + wide * og[...]
        o_ref[...] = acc

    os_, ls_ = pl.BlockSpec((tm, GROUP_W), lambda i: (i, 0)), pl.BlockSpec((tm, 512), lambda i: (i, 0))
    return _pcall(
        body, name=name, grid=(t // tm,), in_specs=[os_] * 3 + [ls_] * 3, out_specs=[os_, ls_],
        out_shape=[jax.ShapeDtypeStruct((t, GROUP_W), F32), jax.ShapeDtypeStruct((t, 512), F32)],
        blocks=[((tm, GROUP_W), F32)] * 4 + [((tm, 512), F32)] * 4,
    )(*outs, *lses)


def head_sums(a, b, *, name, round_a=False, tm=512):
    t = a.shape[0]
    tm = min(tm, t)

    def body(a_ref, b_ref, o_ref):
        av = a_ref[...]
        if round_a:
            av = av.astype(BF16).astype(F32)
        prod = av * b_ref[...]
        for h in range(4):
            s = jnp.sum(jnp.where(_head_mask(h), prod, 0.0), axis=1, keepdims=True)
            o_ref[:, 128 * h:128 * (h + 1)] = jnp.broadcast_to(s, (tm, 128))

    spec = pl.BlockSpec((tm, GROUP_W), lambda i: (i, 0))
    return _pcall(
        body, name=name, grid=(t // tm,), in_specs=[spec, spec],
        out_specs=pl.BlockSpec((tm, 512), lambda i: (i, 0)), out_shape=jax.ShapeDtypeStruct((t, 512), F32),
        blocks=[((tm, GROUP_W), F32)] * 2 + [((tm, 512), F32)],
    )(a, b)


def dil_bwd(qkr, proj, do, lse, dsum, g, *, name):
    t = qkr.shape[0]
    d = DILATIONS[g]
    nsub = t // d
    nblk = nsub // DIL_SPAN
    qk2 = qkr.reshape(nsub, d * 6 * GROUP_W)
    pj2 = proj.reshape(nsub, d * D_IN)
    do2 = do.reshape(nsub, d * GROUP_W)
    lse2 = lse.reshape(nsub, d * 512)
    ds2 = dsum.reshape(nsub, d * 512)
    vcol = 6 + g
    pjw = D_IN // GROUP_W

    def body(qa_ref, qb_ref, kc_ref, kp_ref, vc_ref, vp_ref, doa_ref, dob_ref, la_ref, lb_ref, sa_ref, sb_ref,
             dq_ref, dk_ref, dv_ref):
        nb = pl.program_id(0) % nblk
        own, band = _band_masks()
        prev = band & (nb > 0)
        nxt = band & (nb < nblk - 1)
        qa, qb, kc, kp, vc, vp = qa_ref[...], qb_ref[...], kc_ref[...], kp_ref[...], vc_ref[...], vp_ref[...]
        doa, dob = doa_ref[...].astype(BF16), dob_ref[...].astype(BF16)
        dq = jnp.zeros((DIL_SPAN, GROUP_W), F32)
        dk = jnp.zeros((DIL_SPAN, GROUP_W), F32)
        dv = jnp.zeros((DIL_SPAN, GROUP_W), F32)
        for h in range(4):
            hm = _head_mask(h)
            sl = slice(128 * h, 128 * (h + 1))
            qah = jnp.where(hm, qa, jnp.zeros_like(qa))
            qbh = jnp.where(hm, qb, jnp.zeros_like(qb))
            dah = jnp.where(hm, doa, jnp.zeros_like(doa))
            dbh = jnp.where(hm, dob, jnp.zeros_like(dob))
            la, lb, sa, sb = la_ref[:, sl], lb_ref[:, sl], sa_ref[:, sl], sb_ref[:, sl]

            def probs(qh, dh, k, v, mask, l, s):
                p = jnp.where(mask, jnp.exp(_dot(qh, k, "nt") * ATT_SCALE - l), 0.0)
                dsc = p * (_dot(dh, v, "nt") - s) * ATT_SCALE
                return p.astype(BF16), dsc.astype(BF16)

            p_cc, ds_cc = probs(qah, dah, kc, vc, own, la, sa)
            _, ds_cp = probs(qah, dah, kp, vp, prev, la, sa)
            p_nc, ds_nc = probs(qbh, dbh, kc, vc, nxt, lb, sb)
            dq = jnp.where(hm, _dot(ds_cc, kc, "nn") + _dot(ds_cp, kp, "nn"), dq)
            dk = dk + _dot(ds_cc, qah, "tn") + _dot(ds_nc, qbh, "tn")
            dv = dv + _dot(p_cc, dah, "tn") + _dot(p_nc, dbh, "tn")
        dq_ref[...] = dq
        dk_ref[...] = dk
        dv_ref[...] = dv

    blk = (DIL_SPAN, GROUP_W)
    sblk = (DIL_SPAN, 512)
    cur = lambda i: i % nblk
    prv = lambda i: jnp.maximum(i % nblk - 1, 0)
    nxt_ = lambda i: jnp.minimum(i % nblk + 1, nblk - 1)
    res = lambda i: i // nblk
    o_spec = pl.BlockSpec(blk, lambda i: (cur(i), res(i)))
    o_shape = jax.ShapeDtypeStruct((nsub, d * GROUP_W), F32)
    return _pcall(
        body, name=name, grid=(d * nblk,),
        in_specs=[pl.BlockSpec(blk, lambda i: (cur(i), res(i) * 6 + g)),
                  pl.BlockSpec(blk, lambda i: (nxt_(i), res(i) * 6 + g)),
                  pl.BlockSpec(blk, lambda i: (cur(i), res(i) * 6 + 3 + g)),
                  pl.BlockSpec(blk, lambda i: (prv(i), res(i) * 6 + 3 + g)),
                  pl.BlockSpec(blk, lambda i: (cur(i), res(i) * pjw + vcol)),
                  pl.BlockSpec(blk, lambda i: (prv(i), res(i) * pjw + vcol)),
                  pl.BlockSpec(blk, lambda i: (cur(i), res(i))),
                  pl.BlockSpec(blk, lambda i: (nxt_(i), res(i))),
                  pl.BlockSpec(sblk, lambda i: (cur(i), res(i))),
                  pl.BlockSpec(sblk, lambda i: (nxt_(i), res(i))),
                  pl.BlockSpec(sblk, lambda i: (cur(i), res(i))),
                  pl.BlockSpec(sblk, lambda i: (nxt_(i), res(i)))],
        out_specs=[o_spec, o_spec, o_spec], out_shape=[o_shape, o_shape, o_shape],
        blocks=[(blk, BF16)] * 6 + [(blk, F32)] * 5 + [(sblk, F32)] * 4,
    )(qk2, qk2, qk2, qk2, pj2, pj2, do2, do2, lse2, lse2, ds2, ds2)


def _tri_dot(x, b):
    hi = x.astype(BF16)
    lo = (x - hi.astype(F32)).astype(BF16)
    return _dot(hi, b, "nn") + _dot(lo, b, "nn")


def _sb_tile(qm, kt, rows, cols):
    z = _dot(qm, kt, "nt") * ATT_SCALE
    past = cols < rows
    e = jnp.exp(-jnp.abs(z))
    lbeta = jnp.minimum(z, 0.0) - jnp.log(1.0 + e)
    lkeep = jnp.where(past, lbeta - z, 0.0)
    return z, past, lbeta, lkeep, e


def sb_fwd(proj, *, name, tq=128, tk=256):
    t = proj.shape[0]
    tq, tk = min(tq, t), min(tk, t)

    def body(q_ref, k_ref, v_ref, o_ref):
        qb, h = pl.program_id(0), pl.program_id(1)
        hm = _head_mask(h)
        q = q_ref[...]
        qm = jnp.where(hm, q, jnp.zeros_like(q))
        after = (lax.broadcasted_iota(jnp.int32, (tk, tk), 0) > lax.broadcasted_iota(jnp.int32, (tk, tk), 1)).astype(BF16)
        rows = qb * tq + lax.broadcasted_iota(jnp.int32, (tq, tk), 0)
        col0 = lax.broadcasted_iota(jnp.int32, (tq, tk), 1)
        ntile = ((qb + 1) * tq + tk - 1) // tk

        def step(it, carry):
            c_keep, acc = carry
            off = pl.multiple_of((ntile - 1 - it) * tk, tk)
            kt = k_ref[pl.ds(off, tk), :]
            vt = v_ref[pl.ds(off, tk), :]
            _, past, lbeta, lkeep, _ = _sb_tile(qm, kt, rows, col0 + off)
            w = jnp.where(past, jnp.exp(lbeta + _tri_dot(lkeep, after) + c_keep), 0.0)
            return c_keep + jnp.sum(lkeep, axis=1, keepdims=True), acc + _tri_dot(w, vt)

        _, acc = lax.fori_loop(0, ntile, step, (jnp.zeros((tq, 1), F32), jnp.zeros((tq, GROUP_W), F32)))

        @pl.when(h == 0)
        def _():
            o_ref[...] = jnp.where(hm, acc, 0.0)

        @pl.when(h > 0)
        def _():
            o_ref[...] = jnp.where(hm, acc, o_ref[...])

    full = lambda cb: pl.BlockSpec((t, GROUP_W), functools.partial(lambda i, h, cb: (0, cb), cb=cb))
    return _pcall(
        body, name=name, grid=(t // tq, 4),
        in_specs=[pl.BlockSpec((tq, GROUP_W), lambda i, h: (i, QS_BLK)), full(KS_BLK), full(VS_BLK)],
        out_specs=pl.BlockSpec((tq, GROUP_W), lambda i, h: (i, 0)), out_shape=jax.ShapeDtypeStruct((t, GROUP_W), F32),
        blocks=[((tq, GROUP_W), BF16), ((t, GROUP_W), BF16), ((t, GROUP_W), BF16), ((tq, GROUP_W), F32)],
    )(proj, proj, proj)


def sb_bwd(proj, do, gtot, *, name, tq=128, tk=256):
    t = proj.shape[0]
    tq, tk = min(tq, t), min(tk, t)

    def body(q_ref, k_ref, v_ref, do_ref, gt_ref, dq_ref, dk_ref, dv_ref):
        qb, h = pl.program_id(0), pl.program_id(1)

        @pl.when((qb == 0) & (h == 0))
        def _():
            dk_ref[...] = jnp.zeros_like(dk_ref)
            dv_ref[...] = jnp.zeros_like(dv_ref)

        hm = _head_mask(h)
        q = q_ref[...]
        qm = jnp.where(hm, q, jnp.zeros_like(q))
        dob = do_ref[...].astype(BF16)
        dom = jnp.where(hm, dob, jnp.zeros_like(dob))
        gt = jnp.max(gt_ref[...], axis=1, keepdims=True)
        ri = lax.broadcasted_iota(jnp.int32, (tk, tk), 0)
        ci = lax.broadcasted_iota(jnp.int32, (tk, tk), 1)
        after = (ri > ci).astype(BF16)
        from_on = (ri >= ci).astype(BF16)
        rows = qb * tq + lax.broadcasted_iota(jnp.int32, (tq, tk), 0)
        col0 = lax.broadcasted_iota(jnp.int32, (tq, tk), 1)
        ntile = ((qb + 1) * tq + tk - 1) // tk

        def step(it, carry):
            c_keep, c_g, dq = carry
            off = pl.multiple_of((ntile - 1 - it) * tk, tk)
            kt = k_ref[pl.ds(off, tk), :]
            vt = v_ref[pl.ds(off, tk), :]
            z, past, lbeta, lkeep, e = _sb_tile(qm, kt, rows, col0 + off)
            w = jnp.where(past, jnp.exp(lbeta + _tri_dot(lkeep, after) + c_keep), 0.0)
            gw = w * _dot(dom, vt, "nt")
            big_g = gt - (_tri_dot(gw, from_on) + c_g)
            inv = 1.0 / (1.0 + e)
            sig_pos = jnp.where(z >= 0.0, inv, e * inv)
            sig_neg = jnp.where(z >= 0.0, e * inv, inv)
            dz = (jnp.where(past, gw * sig_neg - big_g * sig_pos, 0.0) * ATT_SCALE).astype(BF16)
            dk_ref[pl.ds(off, tk), :] += _dot(dz, qm, "tn")
            dv_ref[pl.ds(off, tk), :] += _dot(w, dom, "tn")
            return (c_keep + jnp.sum(lkeep, axis=1, keepdims=True), c_g + jnp.sum(gw, axis=1, keepdims=True),
                    dq + _dot(dz, kt, "nn"))

        zero_col = jnp.zeros((tq, 1), F32)
        _, _, dq = lax.fori_loop(0, ntile, step, (zero_col, zero_col, jnp.zeros((tq, GROUP_W), F32)))

        @pl.when(h == 0)
        def _():
            dq_ref[...] = jnp.where(hm, dq, 0.0)

        @pl.when(h > 0)
        def _():
            dq_ref[...] = jnp.where(hm, dq, dq_ref[...])

    full = lambda cb: pl.BlockSpec((t, GROUP_W), functools.partial(lambda i, h, cb: (0, cb), cb=cb))
    whole = pl.BlockSpec((t, GROUP_W), lambda i, h: (0, 0))
    rowblk = pl.BlockSpec((tq, GROUP_W), lambda i, h: (i, 0))
    shape = jax.ShapeDtypeStruct((t, GROUP_W), F32)
    return _pcall(
        body, name=name, grid=(t // tq, 4),
        in_specs=[pl.BlockSpec((tq, GROUP_W), lambda i, h: (i, QS_BLK)), full(KS_BLK), full(VS_BLK), rowblk,
                  pl.BlockSpec((tq, 128), lambda i, h: (i, h))],
        out_specs=[rowblk, whole, whole], out_shape=[shape, shape, shape],
        blocks=[((tq, GROUP_W), BF16), ((t, GROUP_W), BF16), ((t, GROUP_W), BF16), ((tq, GROUP_W), F32),
                ((tq, 128), F32), ((tq, GROUP_W), F32), ((t, GROUP_W), F32), ((t, GROUP_W), F32)],
    )(proj, proj, proj, do, gtot)


def _mesh_place():
    return lax.axis_index("x"), lax.axis_index("y"), lax.axis_index("c")


def _flip(place, mask):
    x, y, c = place
    return ((1 - x) if mask & 4 else x, (1 - y) if mask & 2 else y, (1 - c) if mask & 1 else c)


def _dev_index(place):
    x, y, c = place
    return 4 * x + 2 * y + c


HBM_SPEC = pl.BlockSpec(memory_space=pltpu.HBM)


def all_gather_rows(shard, *, name):
    rows, lanes = shard.shape

    def body(x_ref, out_ref, send_sems, recv_sems, local_sem):
        me = _mesh_place()
        x, y, c = me
        sibling = _flip(me, 1)
        chips = [_flip(me, 4), _flip(me, 2), _flip(me, 6)]

        def copy(k, block, to, src=None):
            dst = out_ref.at[_dev_index(block)]
            return pltpu.make_async_remote_copy(
                src_ref=dst if src is None else src, dst_ref=dst, send_sem=send_sems.at[k], recv_sem=recv_sems.at[k],
                device_id=to, device_id_type=pl.DeviceIdType.MESH)

        mine = pltpu.make_async_copy(x_ref, out_ref.at[_dev_index(me)], local_sem)
        mine.start()
        first = [copy(0, me, sibling, src=x_ref)] + [copy(1 + j, me, chip, src=x_ref) for j, chip in enumerate(chips)]
        for cp in first:
            cp.start()
        passed = [copy(4 + j, chip, sibling) for j, chip in enumerate(chips)]
        for j, chip in enumerate(chips):
            copy(1 + j, chip, me).wait_recv()
            passed[j].start()
        copy(0, sibling, me).wait_recv()
        for j, chip in enumerate(chips):
            copy(4 + j, _flip(chip, 1), me).wait_recv()
        for cp in first + passed:
            cp.wait_send()
        mine.wait()

    return pl.pallas_call(
        body, name=name, in_specs=[HBM_SPEC], out_specs=HBM_SPEC,
        out_shape=jax.ShapeDtypeStruct((N_DEV, rows, lanes), shard.dtype),
        scratch_shapes=[pltpu.SemaphoreType.DMA((7,)), pltpu.SemaphoreType.DMA((7,)), pltpu.SemaphoreType.DMA],
    )(shard)


def exchange_partials(big, small, *, name):
    def body(big_ref, small_ref, big_out, small_out, send_sems, recv_sems, local_sems):
        me = _mesh_place()
        my = _dev_index(me)
        own_big = pltpu.make_async_copy(big_ref.at[my], big_out.at[my], local_sems.at[0])
        own_small = pltpu.make_async_copy(small_ref, small_out.at[my], local_sems.at[1])
        own_big.start()
        own_small.start()

        def copies(mask):
            peer = _flip(me, mask)
            k = mask - 1
            kw = dict(device_id=peer, device_id_type=pl.DeviceIdType.MESH)
            return (pltpu.make_async_remote_copy(src_ref=big_ref.at[_dev_index(peer)], dst_ref=big_out.at[my],
                                                 send_sem=send_sems.at[k], recv_sem=recv_sems.at[k], **kw),
                    pltpu.make_async_remote_copy(src_ref=small_ref, dst_ref=small_out.at[my],
                                                 send_sem=send_sems.at[7 + k], recv_sem=recv_sems.at[7 + k], **kw))

        def arrivals(mask):
            peer = _flip(me, mask)
            k = mask - 1
            their = _dev_index(peer)
            kw = dict(device_id=peer, device_id_type=pl.DeviceIdType.MESH)
            return (pltpu.make_async_remote_copy(src_ref=big_ref.at[their], dst_ref=big_out.at[their],
                                                 send_sem=send_sems.at[k], recv_sem=recv_sems.at[k], **kw),
                    pltpu.make_async_remote_copy(src_ref=small_ref, dst_ref=small_out.at[their],
                                                 send_sem=send_sems.at[7 + k], recv_sem=recv_sems.at[7 + k], **kw))

        sent = [cp for mask in range(1, N_DEV) for cp in copies(mask)]
        for cp in sent:
            cp.start()
        for mask in range(1, N_DEV):
            for cp in arrivals(mask):
                cp.wait_recv()
        for cp in sent:
            cp.wait_send()
        own_big.wait()
        own_small.wait()

    return pl.pallas_call(
        body, name=name, in_specs=[HBM_SPEC, HBM_SPEC], out_specs=[HBM_SPEC, HBM_SPEC],
        out_shape=[jax.ShapeDtypeStruct(big.shape, big.dtype), jax.ShapeDtypeStruct((N_DEV,) + small.shape, small.dtype)],
        scratch_shapes=[pltpu.SemaphoreType.DMA((14,)), pltpu.SemaphoreType.DMA((14,)), pltpu.SemaphoreType.DMA((2,))],
    )(big, small)


def sum_adamw(parts, w, m, v, *, name, tr=256):
    rows, lanes = w.shape
    tr = min(tr, rows)
    assert rows % tr == 0
    bc1 = 1.0 - ADAM_B1 ** ADAM_STEP
    bc2 = 1.0 - ADAM_B2 ** ADAM_STEP

    def body(p_ref, w_ref, m_ref, v_ref, g_ref, d_ref, mo_ref, vo_ref):
        g = p_ref[0].astype(F32)
        for k in range(1, N_DEV):
            g = g + p_ref[k].astype(F32)
        m_new = ADAM_B1 * m_ref[...] + (1.0 - ADAM_B1) * g
        v_new = ADAM_B2 * v_ref[...] + (1.0 - ADAM_B2) * (g * g)
        g_ref[...] = g
        mo_ref[...] = m_new
        vo_ref[...] = v_new
        d_ref[...] = -ADAM_LR * ((m_new / bc1) / (jnp.sqrt(v_new / bc2) + ADAM_EPS) + ADAM_WD * w_ref[...])

    spec = pl.BlockSpec((tr, lanes), lambda i: (i, 0))
    shape = jax.ShapeDtypeStruct((rows, lanes), F32)
    return _pcall(
        body, name=name, grid=(rows // tr,),
        in_specs=[pl.BlockSpec((N_DEV, tr, lanes), lambda i: (0, i, 0)), spec, spec, spec],
        out_specs=[spec] * 4, out_shape=[shape] * 4,
        blocks=[((N_DEV, tr, lanes), parts.dtype)] + [((tr, lanes), F32)] * 7,
    )(parts, w, m, v)


def pack_shards(tensors, dtype):
    rows = [tensors[name][l].astype(dtype).reshape(r, LANES) for l in range(2) for name, r, _, _ in PACK_LAYOUT]
    return jnp.concatenate(rows, axis=0)


def unpack_shards(packed, like):
    out, r0 = {}, 0
    per_layer = {name: [] for name, _, _, _ in PACK_LAYOUT}
    for l in range(2):
        for name, r, _, _ in PACK_LAYOUT:
            per_layer[name].append(packed[r0:r0 + r].reshape(like[name].shape[1:]))
            r0 += r
    for name in per_layer:
        out[name] = jnp.stack(per_layer[name], axis=0)
    return out


def unpack_gathered(gathered):
    layers, r0 = [], 0
    for l in range(2):
        ws = {}
        for name, r, by_cols, (k, n) in PACK_LAYOUT:
            seg = gathered[:, r0:r0 + r, :]
            if by_cols:
                ws[name] = seg.reshape(N_DEV, k, n // N_DEV).transpose(1, 0, 2).reshape(k, n)
            else:
                ws[name] = seg.reshape(k, n)
            r0 += r
        layers.append(ws)
    return layers


def pack_full_grads(grads):
    segs = []
    for l in range(2):
        for name, r, by_cols, (k, n) in PACK_LAYOUT:
            gw = grads[l][name]
            if by_cols:
                gw = gw.reshape(k, N_DEV, n // N_DEV).transpose(1, 0, 2)
            segs.append(gw.reshape(N_DEV, r, LANES))
    return jnp.concatenate(segs, axis=1)


def _row(v):
    return v.reshape(1, -1)


def ffn_fwd(x, gain, w, pre, tag):
    h = rms_fwd(x, gain, name=f"{tag}_norm")
    a, b, s = swiglu_fwd(h, w[pre + "_w_gate"], w[pre + "_w_up"], name=f"{tag}_gateup")
    out = matmul([(s, w[pre + "_w_down"])], "nn", tm=512, tn=1024, tk=2816, out_dtype=F32, scale=0.5, res=x, name=f"{tag}_down")
    return out, (x, h, a, b, s)


def ffn_bwd(dx, dxb, saved, gain, w, pre, tag):
    x, h, a, b, s = saved
    da, db = swiglu_bwd(dxb, w[pre + "_w_down"], a, b, scale=0.5, name=f"{tag}_dgateup")
    g_down = matmul([(s, dxb)], "tn", tm=1408, tn=1024, tk=2048, out_dtype=BF16, scale=0.5, name=f"{tag}_gdown")
    dh = matmul([(da, w[pre + "_w_gate"]), (db, w[pre + "_w_up"])], "nt", tm=256, tn=1024, tk=2816, out_dtype=F32, name=f"{tag}_dh")
    g_gate = matmul([(h, da)], "tn", tm=1024, tn=1408, tk=2048, out_dtype=BF16, name=f"{tag}_ggate")
    g_up = matmul([(h, db)], "tn", tm=1024, tn=1408, tk=2048, out_dtype=BF16, name=f"{tag}_gup")
    dx_in, dxb_in, g_gain = rms_bwd(dh, x, gain, dx, name=f"{tag}_dnorm")
    return dx_in, dxb_in, g_gain, {pre + "_w_gate": g_gate, pre + "_w_up": g_up, pre + "_w_down": g_down}


def mixer_fwd(x, gain, w, tables, tag):
    h = rms_fwd(x, gain, name=f"{tag}_norm")
    proj = matmul([(h, w["w_in"])], "nn", tm=512, tn=1280, tk=1024, out_dtype=BF16, name=f"{tag}_in")
    qkr = rope_apply([(proj, cb) for cb in range(6)], tables, 1.0, name=f"{tag}_rope")
    outs, lses = [], []
    for g in range(N_DIL_GROUPS):
        o, lse = dil_fwd(qkr, proj, g, name=f"{tag}_dil{g}")
        t = x.shape[0]
        outs.append(o.reshape(t, GROUP_W))
        lses.append(lse.reshape(t, 512))
    odil, lse = dil_merge(outs, lses, name=f"{tag}_merge")
    osb = sb_fwd(proj, name=f"{tag}_sb")
    y, u1, u2 = gate_fwd(odil, osb, w["w_proj_dil"], w["w_proj_sb"], proj, name=f"{tag}_gate")
    out = matmul([(y, w["w_out"])], "nn", tm=512, tn=1024, tk=1024, out_dtype=F32, res=x, name=f"{tag}_out")
    return out, (x, h, proj, qkr, odil, lse, osb, u1, u2, y)


def mixer_bwd(dx, dxb, saved, gain, w, tables, tag):
    x, h, proj, qkr, odil, lse, osb, u1, u2, y = saved
    t = x.shape[0]
    g_out = matmul([(y, dxb)], "tn", tm=1024, tn=1024, tk=2048, out_dtype=BF16, name=f"{tag}_gout")
    du1, du2, dgate = gate_bwd(dxb, w["w_out"], u1, u2, proj, name=f"{tag}_dgate")
    g_pd = matmul([(odil, du1)], "tn", tm=256, tn=1024, tk=2048, out_dtype=BF16, name=f"{tag}_gpd")
    g_ps = matmul([(osb, du2)], "tn", tm=256, tn=1024, tk=2048, out_dtype=BF16, name=f"{tag}_gps")
    dodil = matmul([(du1, w["w_proj_dil"])], "nt", tm=512, tn=256, tk=1024, out_dtype=F32, name=f"{tag}_dodil")
    dosb = matmul([(du2, w["w_proj_sb"])], "nt", tm=512, tn=256, tk=1024, out_dtype=F32, name=f"{tag}_dosb")
    dsum = head_sums(dodil, odil, name=f"{tag}_dsum")
    dqs, dks, dvs = [], [], []
    for g in range(N_DIL_GROUPS):
        dq, dk, dv = dil_bwd(qkr, proj, dodil, lse, dsum, g, name=f"{tag}_ddil{g}")
        dqs.append(dq.reshape(t, GROUP_W))
        dks.append(dk.reshape(t, GROUP_W))
        dvs.append(dv.reshape(t, GROUP_W).astype(BF16))
    dqk = rope_apply([(a, 0) for a in dqs + dks], tables, -1.0, name=f"{tag}_drope")
    gtot = head_sums(dosb, osb, round_a=True, name=f"{tag}_gsum")
    dq_s, dk_s, dv_s = sb_bwd(proj, dosb, gtot, name=f"{tag}_dsb")
    dproj = jnp.concatenate([dqk] + dvs + [dq_s.astype(BF16), dk_s.astype(BF16), dv_s.astype(BF16), dgate], axis=1)
    g_in = matmul([(h, dproj)], "tn", tm=1024, tn=1280, tk=2048, out_dtype=BF16, name=f"{tag}_gin")
    dh = matmul([(dproj, w["w_in"])], "nt", tm=256, tn=1024, tk=5120, out_dtype=F32, name=f"{tag}_dh")
    dx_in, dxb_in, g_gain = rms_bwd(dh, x, gain, dx, name=f"{tag}_dnorm")
    return dx_in, dxb_in, g_gain, {"w_in": g_in, "w_proj_dil": g_pd, "w_proj_sb": g_ps, "w_out": g_out}


def kernel(x, norm_ffn1, ffn1_w_gate, ffn1_w_up, ffn1_w_down, norm_mix, w_in, w_proj_dil, w_proj_sb, w_out, norm_ffn2, ffn2_w_gate, ffn2_w_up, ffn2_w_down, norm_final, loss_target, m_norm_ffn1, m_ffn1_w_gate, m_ffn1_w_up, m_ffn1_w_down, m_norm_mix, m_w_in, m_w_proj_dil, m_w_proj_sb, m_w_out, m_norm_ffn2, m_ffn2_w_gate, m_ffn2_w_up, m_ffn2_w_down, m_norm_final, v_norm_ffn1, v_ffn1_w_gate, v_ffn1_w_up, v_ffn1_w_down, v_norm_mix, v_w_in, v_w_proj_dil, v_w_proj_sb, v_w_out, v_norm_ffn2, v_ffn2_w_gate, v_ffn2_w_up, v_ffn2_w_down, v_norm_final):
    args = dict(locals())
    names = [name for name, _, _, _ in PACK_LAYOUT]
    t = x.shape[1]
    xs = x.reshape(t, D_MODEL)
    target = loss_target.reshape(t, D_MODEL)
    tables = rope_tables(t)

    w_shards = {n: args[n] for n in names}
    gathered = all_gather_rows(pack_shards(w_shards, BF16), name="gather_weights")
    weights = unpack_gathered(gathered)
    gains = {n: args[n] for n in NORM_ROWS}

    saved = []
    act = xs
    for l in range(2):
        act, s1 = ffn_fwd(act, _row(gains["norm_ffn1"][l]), weights[l], "ffn1", f"l{l}_ffn1")
        act, s2 = mixer_fwd(act, _row(gains["norm_mix"][l]), weights[l], tables, f"l{l}_mix")
        act, s3 = ffn_fwd(act, _row(gains["norm_ffn2"][l]), weights[l], "ffn2", f"l{l}_ffn2")
        saved.append((s1, s2, s3))
    dx, dxb, g_final, loss_part = final_loss(act, _row(norm_final), target, name="loss_head")

    grads = [None, None]
    gain_grads = {}
    for l in (1, 0):
        s1, s2, s3 = saved[l]
        dx, dxb, gg2, gw2 = ffn_bwd(dx, dxb, s3, _row(gains["norm_ffn2"][l]), weights[l], "ffn2", f"l{l}_ffn2")
        dx, dxb, ggm, gwm = mixer_bwd(dx, dxb, s2, _row(gains["norm_mix"][l]), weights[l], tables, f"l{l}_mix")
        dx, dxb, gg1, gw1 = ffn_bwd(dx, dxb, s1, _row(gains["norm_ffn1"][l]), weights[l], "ffn1", f"l{l}_ffn1")
        grads[l] = {**gw1, **gwm, **gw2}
        gain_grads[("norm_ffn1", l)], gain_grads[("norm_mix", l)], gain_grads[("norm_ffn2", l)] = gg1, ggm, gg2

    loss_row = jnp.pad(loss_part[:, :1], ((0, 0), (0, LANES - 1)))
    small = jnp.concatenate([gain_grads[(n, l)] for n in NORM_ROWS for l in range(2)] + [g_final, loss_row], axis=0)
    big_parts, small_parts = exchange_partials(pack_full_grads(grads), small, name="exchange_grads")

    m_shards = {n: args["m_" + n] for n in names}
    v_shards = {n: args["v_" + n] for n in names}
    g_p, d_p, m_p, v_p = sum_adamw(big_parts, pack_shards(w_shards, F32), pack_shards(m_shards, F32),
                                   pack_shards(v_shards, F32), name="update_shards")
    g_w, d_w, m_w, v_w = (unpack_shards(p, w_shards) for p in (g_p, d_p, m_p, v_p))

    zero_row = jnp.zeros((1, LANES), F32)
    small_of = lambda pre: jnp.concatenate([args[pre + n] for n in NORM_ROWS] + [_row(args[pre + "norm_final"]), zero_row], axis=0)
    g_s, d_s, m_s, v_s = sum_adamw(small_parts, small_of(""), small_of("m_"), small_of("v_"), name="update_gains")

    def gains_of(s):
        out = {n: s[2 * i:2 * i + 2] for i, n in enumerate(NORM_ROWS)}
        out["norm_final"] = s[6]
        return out

    order = ["norm_ffn1", "ffn1_w_gate", "ffn1_w_up", "ffn1_w_down", "norm_mix", "w_in", "w_proj_dil", "w_proj_sb", "w_out",
             "norm_ffn2", "ffn2_w_gate", "ffn2_w_up", "ffn2_w_down", "norm_final"]
    results = []
    for big_d, small_d in ((g_w, g_s), (d_w, d_s), (m_w, m_s), (v_w, v_s)):
        both = {**big_d, **gains_of(small_d)}
        results += [both[n] for n in order]
    loss = g_s[7, 0]
    return (loss, dx.reshape(1, t, D_MODEL), *results)
```

```python
import functools

import jax
import jax.numpy as jnp
from jax import lax
from jax.experimental import pallas as pl
from jax.experimental.pallas import tpu as pltpu

F32 = jnp.float32
BF16 = jnp.bfloat16

D_MODEL = 1024
HEAD_DIM = 64
GROUP_W = 256
D_IN = 5120
N_DIL_GROUPS = 3
DIL_SPAN = 128
DILATIONS = (1, 4, 16)
ROPE_THETA = 500000.0
ROPE_DIM = 16
RMS_EPS = 1e-6
ATT_SCALE = HEAD_DIM ** -0.5
QS_BLK, KS_BLK, VS_BLK = 9, 10, 11
GATE_DIL_BLK, GATE_SB_BLK = 3, 4

ADAM_LR, ADAM_B1, ADAM_B2, ADAM_EPS, ADAM_WD, ADAM_STEP = 0.001, 0.9, 0.999, 1e-08, 0.01, 10

N_DEV = 8
LANES = 1024
VMEM_PHYSICAL_V7X = 64 << 20
VMEM_TEMP_HEADROOM = 20 << 20

PACK_LAYOUT = (
    ("ffn1_w_gate", 352, True, (1024, 2816)),
    ("ffn1_w_up", 352, True, (1024, 2816)),
    ("ffn1_w_down", 352, False, (2816, 1024)),
    ("w_in", 640, True, (1024, 5120)),
    ("w_proj_dil", 32, True, (256, 1024)),
    ("w_proj_sb", 32, True, (256, 1024)),
    ("w_out", 128, False, (1024, 1024)),
    ("ffn2_w_gate", 352, True, (1024, 2816)),
    ("ffn2_w_up", 352, True, (1024, 2816)),
    ("ffn2_w_down", 352, False, (2816, 1024)),
)
SUBBLOCKS = {"ffn1": PACK_LAYOUT[0:3], "mix": PACK_LAYOUT[3:7], "ffn2": PACK_LAYOUT[7:10]}
UPDATE_ROWS = {"ffn1": 352, "mix": 416, "ffn2": 352}
NORM_ROWS = ("norm_ffn1", "norm_mix", "norm_ffn2")


def _nbytes(shape, dtype):
    n = 1
    for s in shape:
        n *= s
    return n * jnp.dtype(dtype).itemsize


def _pcall(body, *, name, grid, in_specs, out_specs, out_shape, blocks, scratch_shapes=(), scratch_bytes=0):
    need = 2 * sum(_nbytes(s, d) for s, d in blocks) + scratch_bytes + VMEM_TEMP_HEADROOM
    limit = min(need, VMEM_PHYSICAL_V7X - (4 << 20))
    return pl.pallas_call(
        body, name=name, grid=grid, in_specs=in_specs, out_specs=out_specs, out_shape=out_shape,
        scratch_shapes=scratch_shapes,
        compiler_params=pltpu.CompilerParams(vmem_limit_bytes=limit),
    )


def _dot(a, b, form):
    dn = {"nn": (((1,), (0,)), ((), ())), "nt": (((1,), (1,)), ((), ())), "tn": (((0,), (0,)), ((), ()))}[form]
    return lax.dot_general(a.astype(BF16), b.astype(BF16), dn, preferred_element_type=F32)


def _sigmoid(x):
    return 1.0 / (1.0 + jnp.exp(-x))


def matmul(pairs, form, *, tm, tn, tk, out_dtype, name, scale=1.0, res=None):
    a0, b0 = pairs[0]
    if form == "tn":
        kdim, m = a0.shape
        n = b0.shape[1]
    else:
        m, kdim = a0.shape
        n = b0.shape[1] if form == "nn" else b0.shape[0]
    tm, tn, tk = min(tm, m), min(tn, n), min(tk, kdim)
    assert m % tm == 0 and n % tn == 0 and kdim % tk == 0, (name, m, n, kdim, tm, tn, tk)
    nk = kdim // tk
    npairs = len(pairs)

    if form == "tn":
        a_blk, a_map = (tk, tm), (lambda j, i, k: (k, i))
    else:
        a_blk, a_map = (tm, tk), (lambda j, i, k: (i, k))
    if form == "nt":
        b_blk, b_map = (tn, tk), (lambda j, i, k: (j, k))
    else:
        b_blk, b_map = (tk, tn), (lambda j, i, k: (k, j))
    o_map = lambda j, i, k: (i, j)

    def body(*refs):
        ab = refs[:2 * npairs]
        rest = refs[2 * npairs:]
        if res is not None:
            r_ref, o_ref = rest[0], rest[1]
            rest = rest[2:]
        else:
            r_ref, o_ref = None, rest[0]
            rest = rest[1:]

        def partial_sum():
            p = _dot(ab[0][...], ab[1][...], form)
            for q in range(1, npairs):
                p = p + _dot(ab[2 * q][...], ab[2 * q + 1][...], form)
            return p

        def finish(acc):
            out = acc * scale if scale != 1.0 else acc
            if r_ref is not None:
                out = r_ref[...] + out
            o_ref[...] = out.astype(out_dtype)

        if nk == 1:
            finish(partial_sum())
        else:
            acc_ref = rest[0]
            k = pl.program_id(2)

            @pl.when(k == 0)
            def _():
                acc_ref[...] = partial_sum()

            @pl.when(k > 0)
            def _():
                acc_ref[...] += partial_sum()

            @pl.when(k == nk - 1)
            def _():
                finish(acc_ref[...])

    in_specs, args, blocks = [], [], []
    for a, b in pairs:
        in_specs += [pl.BlockSpec(a_blk, a_map), pl.BlockSpec(b_blk, b_map)]
        args += [a, b]
        blocks += [(a_blk, a.dtype), (b_blk, b.dtype)]
    if res is not None:
        in_specs.append(pl.BlockSpec((tm, tn), o_map))
        args.append(res)
        blocks.append(((tm, tn), res.dtype))
    blocks.append(((tm, tn), out_dtype))
    scratch = [pltpu.VMEM((tm, tn), F32)] if nk > 1 else []
    return _pcall(
        body, name=name, grid=(n // tn, m // tm, nk), in_specs=in_specs,
        out_specs=pl.BlockSpec((tm, tn), o_map), out_shape=jax.ShapeDtypeStruct((m, n), out_dtype),
        blocks=blocks, scratch_shapes=scratch, scratch_bytes=(tm * tn * 4 if nk > 1 else 0),
    )(*args)


def swiglu_fwd(h, wg, wu, *, name, tm=512, tn=1408):
    t, d = h.shape
    f = wg.shape[1]
    tm, tn = min(tm, t), min(tn, f)

    def body(h_ref, wg_ref, wu_ref, a_ref, b_ref, s_ref):
        hh = h_ref[...]
        a = _dot(hh, wg_ref[...], "nn")
        b = _dot(hh, wu_ref[...], "nn")
        a_ref[...] = a.astype(BF16)
        b_ref[...] = b.astype(BF16)
        s_ref[...] = (a * _sigmoid(a) * b).astype(BF16)

    w_spec = pl.BlockSpec((d, tn), lambda j, i: (0, j))
    o_spec = pl.BlockSpec((tm, tn), lambda j, i: (i, j))
    o_shape = jax.ShapeDtypeStruct((t, f), BF16)
    return _pcall(
        body, name=name, grid=(f // tn, t // tm),
        in_specs=[pl.BlockSpec((tm, d), lambda j, i: (i, 0)), w_spec, w_spec],
        out_specs=[o_spec, o_spec, o_spec], out_shape=[o_shape, o_shape, o_shape],
        blocks=[((tm, d), BF16), ((d, tn), BF16), ((d, tn), BF16)] + [((tm, tn), BF16)] * 3,
    )(h, wg, wu)


def swiglu_bwd(dyb, wd, a, b, *, name, scale, tm=512, tn=1408):
    t, d = dyb.shape
    f = wd.shape[0]
    tm, tn = min(tm, t), min(tn, f)

    def body(dy_ref, wd_ref, a_ref, b_ref, da_ref, db_ref):
        ds = _dot(dy_ref[...], wd_ref[...], "nt") * scale
        av = a_ref[...].astype(F32)
        bv = b_ref[...].astype(F32)
        sg = _sigmoid(av)
        da_ref[...] = (ds * bv * (sg * (1.0 + av * (1.0 - sg)))).astype(BF16)
        db_ref[...] = (ds * (av * sg)).astype(BF16)

    o_spec = pl.BlockSpec((tm, tn), lambda j, i: (i, j))
    o_shape = jax.ShapeDtypeStruct((t, f), BF16)
    return _pcall(
        body, name=name, grid=(f // tn, t // tm),
        in_specs=[pl.BlockSpec((tm, d), lambda j, i: (i, 0)), pl.BlockSpec((tn, d), lambda j, i: (j, 0)), o_spec, o_spec],
        out_specs=[o_spec, o_spec], out_shape=[o_shape, o_shape],
        blocks=[((tm, d), BF16), ((tn, d), BF16)] + [((tm, tn), BF16)] * 4,
    )(dyb, wd, a, b)


def gate_fwd(odil, osb, wpd, wps, proj, *, name, tm=512):
    t = odil.shape[0]
    tm = min(tm, t)

    def body(od_ref, os_ref, wpd_ref, wps_ref, g1_ref, g2_ref, y_ref, u1_ref, u2_ref):
        u1 = _dot(od_ref[...], wpd_ref[...], "nn")
        u2 = _dot(os_ref[...], wps_ref[...], "nn")
        y = _sigmoid(g1_ref[...].astype(F32)) * u1 + _sigmoid(g2_ref[...].astype(F32)) * u2
        y_ref[...] = y.astype(BF16)
        u1_ref[...] = u1.astype(BF16)
        u2_ref[...] = u2.astype(BF16)

    o_spec = pl.BlockSpec((tm, D_MODEL), lambda i: (i, 0))
    w_spec = pl.BlockSpec((GROUP_W, D_MODEL), lambda i: (0, 0))
    a_spec = pl.BlockSpec((tm, GROUP_W), lambda i: (i, 0))
    o_shape = jax.ShapeDtypeStruct((t, D_MODEL), BF16)
    return _pcall(
        body, name=name, grid=(t // tm,),
        in_specs=[a_spec, a_spec, w_spec, w_spec,
                  pl.BlockSpec((tm, D_MODEL), lambda i: (i, GATE_DIL_BLK)),
                  pl.BlockSpec((tm, D_MODEL), lambda i: (i, GATE_SB_BLK))],
        out_specs=[o_spec, o_spec, o_spec], out_shape=[o_shape, o_shape, o_shape],
        blocks=[((tm, GROUP_W), F32)] * 2 + [((GROUP_W, D_MODEL), BF16)] * 2 + [((tm, D_MODEL), BF16)] * 5,
    )(odil, osb, wpd, wps, proj, proj)


def gate_bwd(dxb, wout, u1, u2, proj, *, name, tm=512):
    t = dxb.shape[0]
    tm = min(tm, t)

    def body(dx_ref, w_ref, u1_ref, u2_ref, g1_ref, g2_ref, du1_ref, du2_ref, dg_ref):
        dy = _dot(dx_ref[...], w_ref[...], "nt")
        s1 = _sigmoid(g1_ref[...].astype(F32))
        s2 = _sigmoid(g2_ref[...].astype(F32))
        du1_ref[...] = (dy * s1).astype(BF16)
        du2_ref[...] = (dy * s2).astype(BF16)
        dg_ref[:, :D_MODEL] = (dy * u1_ref[...].astype(F32) * s1 * (1.0 - s1)).astype(BF16)
        dg_ref[:, D_MODEL:] = (dy * u2_ref[...].astype(F32) * s2 * (1.0 - s2)).astype(BF16)

    o_spec = pl.BlockSpec((tm, D_MODEL), lambda i: (i, 0))
    o_shape = jax.ShapeDtypeStruct((t, D_MODEL), BF16)
    return _pcall(
        body, name=name, grid=(t // tm,),
        in_specs=[o_spec, pl.BlockSpec((D_MODEL, D_MODEL), lambda i: (0, 0)), o_spec, o_spec,
                  pl.BlockSpec((tm, D_MODEL), lambda i: (i, GATE_DIL_BLK)),
                  pl.BlockSpec((tm, D_MODEL), lambda i: (i, GATE_SB_BLK))],
        out_specs=[o_spec, o_spec, pl.BlockSpec((tm, 2 * D_MODEL), lambda i: (i, 0))],
        out_shape=[o_shape, o_shape, jax.ShapeDtypeStruct((t, 2 * D_MODEL), BF16)],
        blocks=[((tm, D_MODEL), BF16)] * 9 + [((D_MODEL, D_MODEL), BF16)],
    )(dxb, wout, u1, u2, proj, proj)


def rms_fwd(x, gain, *, name, tm=512):
    t, d = x.shape
    tm = min(tm, t)

    def body(x_ref, g_ref, h_ref):
        xv = x_ref[...]
        rstd = lax.rsqrt(jnp.mean(xv * xv, axis=1, keepdims=True) + RMS_EPS)
        h_ref[...] = (xv * rstd * g_ref[...]).astype(BF16)

    return _pcall(
        body, name=name, grid=(t // tm,),
        in_specs=[pl.BlockSpec((tm, d), lambda i: (i, 0)), pl.BlockSpec((1, d), lambda i: (0, 0))],
        out_specs=pl.BlockSpec((tm, d), lambda i: (i, 0)), out_shape=jax.ShapeDtypeStruct((t, d), BF16),
        blocks=[((tm, d), F32), ((tm, d), BF16)],
    )(x, gain)


def rms_bwd(dh, x, gain, dres, *, name, tm=512):
    t, d = x.shape
    tm = min(tm, t)

    def body(dh_ref, x_ref, g_ref, dr_ref, dx_ref, dxb_ref, dg_ref):
        xv = x_ref[...]
        dhv = dh_ref[...]
        rstd = lax.rsqrt(jnp.mean(xv * xv, axis=1, keepdims=True) + RMS_EPS)
        xh = xv * rstd
        dxh = dhv * g_ref[...]
        dx = dr_ref[...] + rstd * (dxh - xh * jnp.mean(dxh * xh, axis=1, keepdims=True))
        dx_ref[...] = dx
        dxb_ref[...] = dx.astype(BF16)
        part = jnp.sum(dhv * xh, axis=0, keepdims=True)

        @pl.when(pl.program_id(0) == 0)
        def _():
            dg_ref[...] = part

        @pl.when(pl.program_id(0) > 0)
        def _():
            dg_ref[...] += part

    row = pl.BlockSpec((tm, d), lambda i: (i, 0))
    vec = pl.BlockSpec((1, d), lambda i: (0, 0))
    return _pcall(
        body, name=name, grid=(t // tm,), in_specs=[row, row, vec, row], out_specs=[row, row, vec],
        out_shape=[jax.ShapeDtypeStruct((t, d), F32), jax.ShapeDtypeStruct((t, d), BF16), jax.ShapeDtypeStruct((1, d), F32)],
        blocks=[((tm, d), F32)] * 4 + [((tm, d), BF16)],
    )(dh, x, gain, dres)


def final_loss(x, gain, target, *, name, tm=512):
    t, d = x.shape
    tm = min(tm, t)

    def body(x_ref, g_ref, t_ref, dx_ref, dxb_ref, dg_ref, loss_ref):
        xv = x_ref[...]
        g = g_ref[...]
        rstd = lax.rsqrt(jnp.mean(xv * xv, axis=1, keepdims=True) + RMS_EPS)
        xh = xv * rstd
        err = xh * g - t_ref[...]
        dy = err * (1.0 / d)
        dxh = dy * g
        dx = rstd * (dxh - xh * jnp.mean(dxh * xh, axis=1, keepdims=True))
        dx_ref[...] = dx
        dxb_ref[...] = dx.astype(BF16)
        part = jnp.sum(dy * xh, axis=0, keepdims=True)
        sq = jnp.sum(jnp.sum(err * err, axis=1, keepdims=True), axis=0, keepdims=True) * (0.5 / d)
        lpart = jnp.broadcast_to(sq, (1, 128))

        @pl.when(pl.program_id(0) == 0)
        def _():
            dg_ref[...] = part
            loss_ref[...] = lpart

        @pl.when(pl.program_id(0) > 0)
        def _():
            dg_ref[...] += part
            loss_ref[...] += lpart

    row = pl.BlockSpec((tm, d), lambda i: (i, 0))
    vec = pl.BlockSpec((1, d), lambda i: (0, 0))
    return _pcall(
        body, name=name, grid=(t // tm,), in_specs=[row, vec, row],
        out_specs=[row, row, vec, pl.BlockSpec((1, 128), lambda i: (0, 0))],
        out_shape=[jax.ShapeDtypeStruct((t, d), F32), jax.ShapeDtypeStruct((t, d), BF16),
                   jax.ShapeDtypeStruct((1, d), F32), jax.ShapeDtypeStruct((1, 128), F32)],
        blocks=[((tm, d), F32)] * 3 + [((tm, d), BF16)],
    )(x, gain, target)


def rope_tables(t):
    pos = jnp.arange(t, dtype=F32)
    inv_freq = ROPE_THETA ** (-jnp.arange(0, ROPE_DIM, 2, dtype=F32) / ROPE_DIM)
    ang = pos[:, None] * inv_freq[None, :]
    cos, sin = jnp.cos(ang), jnp.sin(ang)
    half = ROPE_DIM // 2
    pad = HEAD_DIM - ROPE_DIM
    one_head = lambda lo, hi, fill: jnp.concatenate([lo, hi, jnp.full((t, pad), fill, F32)], axis=1)
    zeros = jnp.zeros((t, half), F32)
    c = one_head(cos, cos, 1.0)
    sa = one_head(-sin, zeros, 0.0)
    sb = one_head(zeros, sin, 0.0)
    two = lambda a: jnp.concatenate([a, a], axis=1)
    return two(c), two(sa), two(sb)


def rope_apply(pieces, tables, sign, *, name, tm=512):
    c, sa, sb = tables
    t = c.shape[0]
    tm = min(tm, t)
    n = len(pieces)

    def body(*refs):
        c_ref, sa_ref, sb_ref = refs[n:n + 3]
        o_ref = refs[n + 3]
        cv = c_ref[...]
        sav = sa_ref[...] * sign
        sbv = sb_ref[...] * sign
        for p in range(n):
            for half in range(2):
                xv = refs[p][:, 128 * half:128 * (half + 1)].astype(F32)
                out = xv * cv + pltpu.roll(xv, 120, 1) * sav + pltpu.roll(xv, 8, 1) * sbv
                o_ref[:, GROUP_W * p + 128 * half:GROUP_W * p + 128 * (half + 1)] = out.astype(BF16)

    tab = pl.BlockSpec((tm, 128), lambda i: (i, 0))
    in_specs = [pl.BlockSpec((tm, GROUP_W), functools.partial(lambda i, cb: (i, cb), cb=cb)) for _, cb in pieces]
    return _pcall(
        body, name=name, grid=(t // tm,), in_specs=in_specs + [tab, tab, tab],
        out_specs=pl.BlockSpec((tm, GROUP_W * n), lambda i: (i, 0)),
        out_shape=jax.ShapeDtypeStruct((t, GROUP_W * n), BF16),
        blocks=[((tm, GROUP_W), F32)] * n + [((tm, 128), F32)] * 3 + [((tm, GROUP_W * n), BF16)],
    )(*[a for a, _ in pieces], c, sa, sb)


def _head_mask(h):
    lane = lax.broadcasted_iota(jnp.int32, (1, GROUP_W), 1)
    return (lane // HEAD_DIM) == h


def _band_masks():
    ri = lax.broadcasted_iota(jnp.int32, (DIL_SPAN, DIL_SPAN), 0)
    ci = lax.broadcasted_iota(jnp.int32, (DIL_SPAN, DIL_SPAN), 1)
    return ci <= ri, ci >= ri


def dil_fwd(qkr, proj, g, *, name):
    t = qkr.shape[0]
    d = DILATIONS[g]
    nsub = t // d
    nblk = nsub // DIL_SPAN
    qk2 = qkr.reshape(nsub, d * 6 * GROUP_W)
    pj2 = proj.reshape(nsub, d * D_IN)
    vcol = 6 + g
    pjw = D_IN // GROUP_W

    def body(q_ref, kc_ref, kp_ref, vc_ref, vp_ref, o_ref, lse_ref):
        nb = pl.program_id(0) % nblk
        own, prev = _band_masks()
        prev = prev & (nb > 0)
        q, kc, kp, vc, vp = q_ref[...], kc_ref[...], kp_ref[...], vc_ref[...], vp_ref[...]
        o_acc = jnp.zeros((DIL_SPAN, GROUP_W), F32)
        for h in range(4):
            hm = _head_mask(h)
            qh = jnp.where(hm, q, jnp.zeros_like(q))
            sc = jnp.where(own, _dot(qh, kc, "nt") * ATT_SCALE, -jnp.inf)
            sp = jnp.where(prev, _dot(qh, kp, "nt") * ATT_SCALE, -jnp.inf)
            m = jnp.maximum(jnp.max(sc, axis=1, keepdims=True), jnp.max(sp, axis=1, keepdims=True))
            pc = jnp.exp(sc - m)
            pp = jnp.exp(sp - m)
            den = jnp.sum(pc, axis=1, keepdims=True) + jnp.sum(pp, axis=1, keepdims=True)
            oh = (_dot(pc, vc, "nn") + _dot(pp, vp, "nn")) / den
            o_acc = jnp.where(hm, oh, o_acc)
            lse_ref[:, 128 * h:128 * (h + 1)] = jnp.broadcast_to(m + jnp.log(den), (DIL_SPAN, 128))
        o_ref[...] = o_acc

    blk = (DIL_SPAN, GROUP_W)
    cur = lambda i: i % nblk
    prv = lambda i: jnp.maximum(i % nblk - 1, 0)
    res = lambda i: i // nblk
    return _pcall(
        body, name=name, grid=(d * nblk,),
        in_specs=[pl.BlockSpec(blk, lambda i: (cur(i), res(i) * 6 + g)),
                  pl.BlockSpec(blk, lambda i: (cur(i), res(i) * 6 + 3 + g)),
                  pl.BlockSpec(blk, lambda i: (prv(i), res(i) * 6 + 3 + g)),
                  pl.BlockSpec(blk, lambda i: (cur(i), res(i) * pjw + vcol)),
                  pl.BlockSpec(blk, lambda i: (prv(i), res(i) * pjw + vcol))],
        out_specs=[pl.BlockSpec(blk, lambda i: (cur(i), res(i))),
                   pl.BlockSpec((DIL_SPAN, 512), lambda i: (cur(i), res(i)))],
        out_shape=[jax.ShapeDtypeStruct((nsub, d * GROUP_W), F32), jax.ShapeDtypeStruct((nsub, d * 512), F32)],
        blocks=[(blk, BF16)] * 5 + [(blk, F32), ((DIL_SPAN, 512), F32)],
    )(qk2, qk2, qk2, pj2, pj2)


def dil_merge(outs, lses, *, name, tm=512):
    t = outs[0].shape[0]
    tm = min(tm, t)

    def body(o0, o1, o2, l0, l1, l2, o_ref, lse_ref):
        ls = [l0[...], l1[...], l2[...]]
        m = jnp.maximum(jnp.maximum(ls[0], ls[1]), ls[2])
        tot = m + jnp.log(jnp.exp(ls[0] - m) + jnp.exp(ls[1] - m) + jnp.exp(ls[2] - m))
        lse_ref[...] = tot
        lane = lax.broadcasted_iota(jnp.int32, (1, 128), 1)
        first = lane < HEAD_DIM
        acc = jnp.zeros((tm, GROUP_W), F32)
        for og, lg in zip((o0, o1, o2), ls):
            w = jnp.exp(lg - tot)
            wide = jnp.concatenate([jnp.where(first, w[:, 0:128], w[:, 128:256]),
                                    jnp.where(first, w[:, 256:384], w[:, 384:512])], axis=1)
            acc = acc + wide * og[...]
        o_ref[...] = acc

    os_, ls_ = pl.BlockSpec((tm, GROUP_W), lambda i: (i, 0)), pl.BlockSpec((tm, 512), lambda i: (i, 0))
    return _pcall(
        body, name=name, grid=(t // tm,), in_specs=[os_] * 3 + [ls_] * 3, out_specs=[os_, ls_],
        out_shape=[jax.ShapeDtypeStruct((t, GROUP_W), F32), jax.ShapeDtypeStruct((t, 512), F32)],
        blocks=[((tm, GROUP_W), F32)] * 4 + [((tm, 512), F32)] * 4,
    )(*outs, *lses)


def head_sums(a, b, *, name, round_a=False, tm=512):
    t = a.shape[0]
    tm = min(tm, t)

    def body(a_ref, b_ref, o_ref):
        av = a_ref[...]
        if round_a:
            av = av.astype(BF16).astype(F32)
        prod = av * b_ref[...]
        for h in range(4):
            s = jnp.sum(jnp.where(_head_mask(h), prod, 0.0), axis=1, keepdims=True)
            o_ref[:, 128 * h:128 * (h + 1)] = jnp.broadcast_to(s, (tm, 128))

    spec = pl.BlockSpec((tm, GROUP_W), lambda i: (i, 0))
    return _pcall(
        body, name=name, grid=(t // tm,), in_specs=[spec, spec],
        out_specs=pl.BlockSpec((tm, 512), lambda i: (i, 0)), out_shape=jax.ShapeDtypeStruct((t, 512), F32),
        blocks=[((tm, GROUP_W), F32)] * 2 + [((tm, 512), F32)],
    )(a, b)


def dil_bwd(qkr, proj, do, lse, dsum, g, *, name):
    t = qkr.shape[0]
    d = DILATIONS[g]
    nsub = t // d
    nblk = nsub // DIL_SPAN
    qk2 = qkr.reshape(nsub, d * 6 * GROUP_W)
    pj2 = proj.reshape(nsub, d * D_IN)
    do2 = do.reshape(nsub, d * GROUP_W)
    lse2 = lse.reshape(nsub, d * 512)
    ds2 = dsum.reshape(nsub, d * 512)
    vcol = 6 + g
    pjw = D_IN // GROUP_W

    def body(qa_ref, qb_ref, kc_ref, kp_ref, vc_ref, vp_ref, doa_ref, dob_ref, la_ref, lb_ref, sa_ref, sb_ref,
             dq_ref, dk_ref, dv_ref):
        nb = pl.program_id(0) % nblk
        own, band = _band_masks()
        prev = band & (nb > 0)
        nxt = band & (nb < nblk - 1)
        qa, qb, kc, kp, vc, vp = qa_ref[...], qb_ref[...], kc_ref[...], kp_ref[...], vc_ref[...], vp_ref[...]
        doa, dob = doa_ref[...].astype(BF16), dob_ref[...].astype(BF16)
        dq = jnp.zeros((DIL_SPAN, GROUP_W), F32)
        dk = jnp.zeros((DIL_SPAN, GROUP_W), F32)
        dv = jnp.zeros((DIL_SPAN, GROUP_W), F32)
        for h in range(4):
            hm = _head_mask(h)
            sl = slice(128 * h, 128 * (h + 1))
            qah = jnp.where(hm, qa, jnp.zeros_like(qa))
            qbh = jnp.where(hm, qb, jnp.zeros_like(qb))
            dah = jnp.where(hm, doa, jnp.zeros_like(doa))
            dbh = jnp.where(hm, dob, jnp.zeros_like(dob))
            la, lb, sa, sb = la_ref[:, sl], lb_ref[:, sl], sa_ref[:, sl], sb_ref[:, sl]

            def probs(qh, dh, k, v, mask, l, s):
                p = jnp.where(mask, jnp.exp(_dot(qh, k, "nt") * ATT_SCALE - l), 0.0)
                dsc = p * (_dot(dh, v, "nt") - s) * ATT_SCALE
                return p.astype(BF16), dsc.astype(BF16)

            p_cc, ds_cc = probs(qah, dah, kc, vc, own, la, sa)
            _, ds_cp = probs(qah, dah, kp, vp, prev, la, sa)
            p_nc, ds_nc = probs(qbh, dbh, kc, vc, nxt, lb, sb)
            dq = jnp.where(hm, _dot(ds_cc, kc, "nn") + _dot(ds_cp, kp, "nn"), dq)
            dk = dk + _dot(ds_cc, qah, "tn") + _dot(ds_nc, qbh, "tn")
            dv = dv + _dot(p_cc, dah, "tn") + _dot(p_nc, dbh, "tn")
        dq_ref[...] = dq
        dk_ref[...] = dk
        dv_ref[...] = dv

    blk = (DIL_SPAN, GROUP_W)
    sblk = (DIL_SPAN, 512)
    cur = lambda i: i % nblk
    prv = lambda i: jnp.maximum(i % nblk - 1, 0)
    nxt_ = lambda i: jnp.minimum(i % nblk + 1, nblk - 1)
    res = lambda i: i // nblk
    o_spec = pl.BlockSpec(blk, lambda i: (cur(i), res(i)))
    o_shape = jax.ShapeDtypeStruct((nsub, d * GROUP_W), F32)
    return _pcall(
        body, name=name, grid=(d * nblk,),
        in_specs=[pl.BlockSpec(blk, lambda i: (cur(i), res(i) * 6 + g)),
                  pl.BlockSpec(blk, lambda i: (nxt_(i), res(i) * 6 + g)),
                  pl.BlockSpec(blk, lambda i: (cur(i), res(i) * 6 + 3 + g)),
                  pl.BlockSpec(blk, lambda i: (prv(i), res(i) * 6 + 3 + g)),
                  pl.BlockSpec(blk, lambda i: (cur(i), res(i) * pjw + vcol)),
                  pl.BlockSpec(blk, lambda i: (prv(i), res(i) * pjw + vcol)),
                  pl.BlockSpec(blk, lambda i: (cur(i), res(i))),
                  pl.BlockSpec(blk, lambda i: (nxt_(i), res(i))),
                  pl.BlockSpec(sblk, lambda i: (cur(i), res(i))),
                  pl.BlockSpec(sblk, lambda i: (nxt_(i), res(i))),
                  pl.BlockSpec(sblk, lambda i: (cur(i), res(i))),
                  pl.BlockSpec(sblk, lambda i: (nxt_(i), res(i)))],
        out_specs=[o_spec, o_spec, o_spec], out_shape=[o_shape, o_shape, o_shape],
        blocks=[(blk, BF16)] * 6 + [(blk, F32)] * 5 + [(sblk, F32)] * 4,
    )(qk2, qk2, qk2, qk2, pj2, pj2, do2, do2, lse2, lse2, ds2, ds2)


def _tri_dot(x, b):
    hi = x.astype(BF16)
    lo = (x - hi.astype(F32)).astype(BF16)
    return _dot(hi, b, "nn") + _dot(lo, b, "nn")


SB_TILE = 256


def _stack_heads(a):
    return jnp.concatenate([jnp.where(_head_mask(h), a, jnp.zeros_like(a)) for h in range(4)], axis=0)


def _unstack_heads(acc, rows):
    out = acc[0:rows]
    for h in range(1, 4):
        out = jnp.where(_head_mask(h), acc[h * rows:(h + 1) * rows], out)
    return out


def _tri_masks(n):
    ri = lax.broadcasted_iota(jnp.int32, (n, n), 0)
    ci = lax.broadcasted_iota(jnp.int32, (n, n), 1)
    return (ri > ci).astype(BF16), (ri >= ci).astype(BF16)


def _sb_weights(qs, kt, after, c_keep, diagonal):
    z = _dot(qs, kt, "nt")
    lbeta = jnp.minimum(z, 0.0) - jnp.log(1.0 + jnp.exp(-jnp.abs(z)))
    lkeep = lbeta - z
    past = None
    if diagonal:
        n = SB_TILE
        past = lax.broadcasted_iota(jnp.int32, z.shape, 1) < lax.broadcasted_iota(jnp.int32, z.shape, 0) % n
        lkeep = jnp.where(past, lkeep, 0.0)
    w = jnp.exp(lbeta + _tri_dot(lkeep, after) + c_keep)
    if diagonal:
        w = jnp.where(past, w, 0.0)
    return z, past, lbeta, lkeep, w


def sb_fwd(proj, *, name):
    t = proj.shape[0]
    n = SB_TILE
    assert t % n == 0

    def body(q_ref, k_ref, v_ref, o_ref, acc_ref):
        qb = pl.program_id(0)
        qs = _stack_heads(q_ref[...] * ATT_SCALE)
        after, _ = _tri_masks(n)

        def tile(off, diagonal, c_keep):
            kt = k_ref[pl.ds(off, n), :]
            vt = v_ref[pl.ds(off, n), :]
            _, _, _, lkeep, w = _sb_weights(qs, kt, after, c_keep, diagonal)
            pv = _tri_dot(w, vt)
            if diagonal:
                acc_ref[...] = pv
            else:
                acc_ref[...] += pv
            return c_keep + jnp.sum(lkeep, axis=1, keepdims=True)

        c0 = tile(pl.multiple_of(qb * n, n), True, jnp.zeros((4 * n, 1), F32))
        lax.fori_loop(0, qb, lambda it, c: tile(pl.multiple_of((qb - 1 - it) * n, n), False, c), c0)
        o_ref[...] = _unstack_heads(acc_ref[...], n)

    full = lambda cb: pl.BlockSpec((t, GROUP_W), functools.partial(lambda i, cb: (0, cb), cb=cb))
    return _pcall(
        body, name=name, grid=(t // n,),
        in_specs=[pl.BlockSpec((n, GROUP_W), lambda i: (i, QS_BLK)), full(KS_BLK), full(VS_BLK)],
        out_specs=pl.BlockSpec((n, GROUP_W), lambda i: (i, 0)), out_shape=jax.ShapeDtypeStruct((t, GROUP_W), F32),
        blocks=[((n, GROUP_W), BF16), ((t, GROUP_W), BF16), ((t, GROUP_W), BF16), ((n, GROUP_W), F32)],
        scratch_shapes=[pltpu.VMEM((4 * n, GROUP_W), F32)], scratch_bytes=4 * n * GROUP_W * 4,
    )(proj, proj, proj)


def sb_bwd(proj, do, gtot, *, name):
    t = proj.shape[0]
    n = SB_TILE
    assert t % n == 0

    def body(q_ref, k_ref, v_ref, do_ref, gt_ref, dq_ref, dk_ref, dv_ref, acc_ref):
        qb = pl.program_id(0)

        @pl.when(qb == 0)
        def _():
            dk_ref[...] = jnp.zeros_like(dk_ref)
            dv_ref[...] = jnp.zeros_like(dv_ref)

        qs = _stack_heads(q_ref[...] * ATT_SCALE)
        dos = _stack_heads(do_ref[...].astype(BF16))
        gt = jnp.concatenate([jnp.max(gt_ref[:, 128 * h:128 * (h + 1)], axis=1, keepdims=True) for h in range(4)], axis=0)
        after, from_on = _tri_masks(n)

        def tile(off, diagonal, carry):
            c_keep, c_g = carry
            kt = k_ref[pl.ds(off, n), :]
            vt = v_ref[pl.ds(off, n), :]
            z, past, lbeta, lkeep, w = _sb_weights(qs, kt, after, c_keep, diagonal)
            gw = w * _dot(dos, vt, "nt")
            big_g = gt - (_tri_dot(gw, from_on) + c_g)
            dz = gw * jnp.exp(lbeta - z) - big_g * jnp.exp(lbeta)
            if diagonal:
                dz = jnp.where(past, dz, 0.0)
            dz = dz.astype(BF16)
            dk_ref[pl.ds(off, n), :] += _dot(dz, qs, "tn")
            dv_ref[pl.ds(off, n), :] += _dot(w, dos, "tn")
            dq = _dot(dz, kt, "nn")
            if diagonal:
                acc_ref[...] = dq
            else:
                acc_ref[...] += dq
            return c_keep + jnp.sum(lkeep, axis=1, keepdims=True), c_g + jnp.sum(gw, axis=1, keepdims=True)

        zero_col = jnp.zeros((4 * n, 1), F32)
        c0 = tile(pl.multiple_of(qb * n, n), True, (zero_col, zero_col))
        lax.fori_loop(0, qb, lambda it, c: tile(pl.multiple_of((qb - 1 - it) * n, n), False, c), c0)
        dq_ref[...] = _unstack_heads(acc_ref[...], n) * ATT_SCALE

    full = lambda cb: pl.BlockSpec((t, GROUP_W), functools.partial(lambda i, cb: (0, cb), cb=cb))
    whole = pl.BlockSpec((t, GROUP_W), lambda i: (0, 0))
    rowblk = pl.BlockSpec((n, GROUP_W), lambda i: (i, 0))
    shape = jax.ShapeDtypeStruct((t, GROUP_W), F32)
    return _pcall(
        body, name=name, grid=(t // n,),
        in_specs=[pl.BlockSpec((n, GROUP_W), lambda i: (i, QS_BLK)), full(KS_BLK), full(VS_BLK), rowblk,
                  pl.BlockSpec((n, 512), lambda i: (i, 0))],
        out_specs=[rowblk, whole, whole], out_shape=[shape, shape, shape],
        blocks=[((n, GROUP_W), BF16), ((t, GROUP_W), BF16), ((t, GROUP_W), BF16), ((n, GROUP_W), F32),
                ((n, 512), F32), ((n, GROUP_W), F32), ((t, GROUP_W), F32), ((t, GROUP_W), F32)],
        scratch_shapes=[pltpu.VMEM((4 * n, GROUP_W), F32)], scratch_bytes=4 * n * GROUP_W * 4,
    )(proj, proj, proj, do, gtot)


def _mesh_place():
    return lax.axis_index("x"), lax.axis_index("y"), lax.axis_index("c")


def _flip(place, mask):
    x, y, c = place
    return ((1 - x) if mask & 4 else x, (1 - y) if mask & 2 else y, (1 - c) if mask & 1 else c)


def _dev_index(place):
    x, y, c = place
    return 4 * x + 2 * y + c


HBM_SPEC = pl.BlockSpec(memory_space=pltpu.HBM)


def all_gather_rows(shard, *, name):
    rows, lanes = shard.shape

    def body(x_ref, out_ref, send_sems, recv_sems, local_sem):
        me = _mesh_place()
        x, y, c = me
        sibling = _flip(me, 1)
        chips = [_flip(me, 4), _flip(me, 2), _flip(me, 6)]

        def copy(k, block, to, src=None):
            dst = out_ref.at[_dev_index(block)]
            return pltpu.make_async_remote_copy(
                src_ref=dst if src is None else src, dst_ref=dst, send_sem=send_sems.at[k], recv_sem=recv_sems.at[k],
                device_id=to, device_id_type=pl.DeviceIdType.MESH)

        mine = pltpu.make_async_copy(x_ref, out_ref.at[_dev_index(me)], local_sem)
        mine.start()
        first = [copy(0, me, sibling, src=x_ref)] + [copy(1 + j, me, chip, src=x_ref) for j, chip in enumerate(chips)]
        for cp in first:
            cp.start()
        passed = [copy(4 + j, chip, sibling) for j, chip in enumerate(chips)]
        for j, chip in enumerate(chips):
            copy(1 + j, chip, me).wait_recv()
            passed[j].start()
        copy(0, sibling, me).wait_recv()
        for j, chip in enumerate(chips):
            copy(4 + j, _flip(chip, 1), me).wait_recv()
        for cp in first + passed:
            cp.wait_send()
        mine.wait()

    return pl.pallas_call(
        body, name=name, in_specs=[HBM_SPEC], out_specs=HBM_SPEC,
        out_shape=jax.ShapeDtypeStruct((N_DEV, rows, lanes), shard.dtype),
        scratch_shapes=[pltpu.SemaphoreType.DMA((7,)), pltpu.SemaphoreType.DMA((7,)), pltpu.SemaphoreType.DMA],
    )(shard)


SEM_SPEC = pl.BlockSpec(memory_space=pltpu.SEMAPHORE)
DATAFLOW_EFFECT = pltpu.SideEffectType.DATAFLOW_SIDE_EFFECTING


def _spread_copies(src_ref, land_ref, send_sems, recv_sems, per_peer, arriving):
    me = _mesh_place()
    out = []
    for mask in range(1, N_DEV):
        peer = _flip(me, mask)
        data_of = _dev_index(me) if arriving else _dev_index(peer)
        slot = _dev_index(peer) if arriving else _dev_index(me)
        out.append(pltpu.make_async_remote_copy(
            src_ref=src_ref.at[data_of] if per_peer else src_ref, dst_ref=land_ref.at[slot],
            send_sem=send_sems.at[mask - 1], recv_sem=recv_sems.at[mask - 1],
            device_id=peer, device_id_type=pl.DeviceIdType.MESH))
    return out


def spread_start(src, land, *, per_peer, name):
    def body(src_ref, land_ref, send_sems, recv_sems, src_thru, land_thru, token):
        for cp in _spread_copies(src_ref, land_ref, send_sems, recv_sems, per_peer, arriving=False):
            cp.start()
        token[...] = jnp.zeros_like(token)

    return pl.pallas_call(
        body, name=name, in_specs=(HBM_SPEC, HBM_SPEC),
        out_shape=(pltpu.SemaphoreType.DMA((N_DEV - 1,)), pltpu.SemaphoreType.DMA((N_DEV - 1,)),
                   pltpu.HBM(src.shape, src.dtype), pltpu.HBM(land.shape, land.dtype), jax.ShapeDtypeStruct((8, 128), F32)),
        out_specs=(SEM_SPEC, SEM_SPEC, HBM_SPEC, HBM_SPEC, pl.BlockSpec(memory_space=pltpu.VMEM)),
        input_output_aliases={0: 2, 1: 3},
        compiler_params=pltpu.CompilerParams(has_side_effects=DATAFLOW_EFFECT),
    )(pltpu.with_memory_space_constraint(src, pltpu.HBM), pltpu.with_memory_space_constraint(land, pltpu.HBM))


def spread_wait(started, after, *, per_peer, name):
    send_sems, recv_sems, src_thru, land_thru, _ = started

    def body(src_ref, land_ref, send_sems, recv_sems, after_ref, src_dead, got_ref):
        for cp in _spread_copies(src_ref, land_ref, send_sems, recv_sems, per_peer, arriving=True):
            cp.wait_send()
            cp.wait_recv()

    return pl.pallas_call(
        body, name=name, in_specs=(HBM_SPEC, HBM_SPEC, SEM_SPEC, SEM_SPEC, pl.BlockSpec(memory_space=pl.ANY)),
        out_shape=(pltpu.HBM(src_thru.shape, src_thru.dtype), pltpu.HBM(land_thru.shape, land_thru.dtype)),
        out_specs=(HBM_SPEC, HBM_SPEC), input_output_aliases={0: 0, 1: 1},
        compiler_params=pltpu.CompilerParams(has_side_effects=DATAFLOW_EFFECT),
    )(src_thru, land_thru, send_sems, recv_sems, after)[1]


def landing_zone(own_block, my_index):
    zone = lax.empty((N_DEV,) + own_block.shape, own_block.dtype)
    return lax.dynamic_update_slice(zone, own_block[None], (my_index,) + (0,) * own_block.ndim)


def sum_adamw(parts, w, m, v, *, name, tr=256):
    rows, lanes = w.shape
    tr = min(tr, rows)
    assert rows % tr == 0
    bc1 = 1.0 - ADAM_B1 ** ADAM_STEP
    bc2 = 1.0 - ADAM_B2 ** ADAM_STEP

    def body(p_ref, w_ref, m_ref, v_ref, g_ref, d_ref, mo_ref, vo_ref):
        g = p_ref[0].astype(F32)
        for k in range(1, N_DEV):
            g = g + p_ref[k].astype(F32)
        m_new = ADAM_B1 * m_ref[...] + (1.0 - ADAM_B1) * g
        v_new = ADAM_B2 * v_ref[...] + (1.0 - ADAM_B2) * (g * g)
        g_ref[...] = g
        mo_ref[...] = m_new
        vo_ref[...] = v_new
        d_ref[...] = -ADAM_LR * ((m_new / bc1) / (jnp.sqrt(v_new / bc2) + ADAM_EPS) + ADAM_WD * w_ref[...])

    spec = pl.BlockSpec((tr, lanes), lambda i: (i, 0))
    shape = jax.ShapeDtypeStruct((rows, lanes), F32)
    return _pcall(
        body, name=name, grid=(rows // tr,),
        in_specs=[pl.BlockSpec((N_DEV, tr, lanes), lambda i: (0, i, 0)), spec, spec, spec],
        out_specs=[spec] * 4, out_shape=[shape] * 4,
        blocks=[((N_DEV, tr, lanes), parts.dtype)] + [((tr, lanes), F32)] * 7,
    )(parts, w, m, v)


def pack_shards(tensors, layer, part, dtype):
    return jnp.concatenate([tensors[name][layer].astype(dtype).reshape(r, LANES) for name, r, _, _ in SUBBLOCKS[part]], axis=0)


def unpack_shards(packed, part, like):
    out, r0 = {}, 0
    for name, r, _, _ in SUBBLOCKS[part]:
        out[name] = packed[r0:r0 + r].reshape(like[name].shape[1:])
        r0 += r
    return out


def unpack_gathered(gathered, part):
    ws, r0 = {}, 0
    for name, r, by_cols, (k, n) in SUBBLOCKS[part]:
        seg = gathered[:, r0:r0 + r, :]
        if by_cols:
            ws[name] = seg.reshape(N_DEV, k, n // N_DEV).transpose(1, 0, 2).reshape(k, n)
        else:
            ws[name] = seg.reshape(k, n)
        r0 += r
    return ws


def pack_full_grads(grads, part):
    segs = []
    for name, r, by_cols, (k, n) in SUBBLOCKS[part]:
        gw = grads[name]
        if by_cols:
            gw = gw.reshape(k, N_DEV, n // N_DEV).transpose(1, 0, 2)
        segs.append(gw.reshape(N_DEV, r, LANES))
    return jnp.concatenate(segs, axis=1)


def _row(v):
    return v.reshape(1, -1)


def ffn_fwd(x, gain, w, pre, tag):
    h = rms_fwd(x, gain, name=f"{tag}_norm")
    a, b, s = swiglu_fwd(h, w[pre + "_w_gate"], w[pre + "_w_up"], name=f"{tag}_gateup")
    out = matmul([(s, w[pre + "_w_down"])], "nn", tm=512, tn=1024, tk=2816, out_dtype=F32, scale=0.5, res=x, name=f"{tag}_down")
    return out, (x, h, a, b, s)


def ffn_bwd_weights(dxb, saved, w, pre, tag):
    x, h, a, b, s = saved
    da, db = swiglu_bwd(dxb, w[pre + "_w_down"], a, b, scale=0.5, name=f"{tag}_dgateup")
    g_down = matmul([(s, dxb)], "tn", tm=1408, tn=1024, tk=2048, out_dtype=BF16, scale=0.5, name=f"{tag}_gdown")
    g_gate = matmul([(h, da)], "tn", tm=1024, tn=1408, tk=2048, out_dtype=BF16, name=f"{tag}_ggate")
    g_up = matmul([(h, db)], "tn", tm=1024, tn=1408, tk=2048, out_dtype=BF16, name=f"{tag}_gup")
    return {pre + "_w_gate": g_gate, pre + "_w_up": g_up, pre + "_w_down": g_down}, (da, db)


def ffn_bwd_input(dx, rest, saved, gain, w, pre, tag):
    da, db = rest
    x = saved[0]
    dh = matmul([(da, w[pre + "_w_gate"]), (db, w[pre + "_w_up"])], "nt", tm=256, tn=1024, tk=2816, out_dtype=F32, name=f"{tag}_dh")
    return rms_bwd(dh, x, gain, dx, name=f"{tag}_dnorm")


def mixer_fwd(x, gain, w, tables, tag):
    h = rms_fwd(x, gain, name=f"{tag}_norm")
    proj = matmul([(h, w["w_in"])], "nn", tm=512, tn=1280, tk=1024, out_dtype=BF16, name=f"{tag}_in")
    qkr = rope_apply([(proj, cb) for cb in range(6)], tables, 1.0, name=f"{tag}_rope")
    outs, lses = [], []
    for g in range(N_DIL_GROUPS):
        o, lse = dil_fwd(qkr, proj, g, name=f"{tag}_dil{g}")
        t = x.shape[0]
        outs.append(o.reshape(t, GROUP_W))
        lses.append(lse.reshape(t, 512))
    odil, lse = dil_merge(outs, lses, name=f"{tag}_merge")
    osb = sb_fwd(proj, name=f"{tag}_sb")
    y, u1, u2 = gate_fwd(odil, osb, w["w_proj_dil"], w["w_proj_sb"], proj, name=f"{tag}_gate")
    out = matmul([(y, w["w_out"])], "nn", tm=512, tn=1024, tk=1024, out_dtype=F32, res=x, name=f"{tag}_out")
    return out, (x, h, proj, qkr, odil, lse, osb, u1, u2, y)


def mixer_bwd_weights(dxb, saved, w, tables, tag):
    x, h, proj, qkr, odil, lse, osb, u1, u2, y = saved
    t = x.shape[0]
    g_out = matmul([(y, dxb)], "tn", tm=1024, tn=1024, tk=2048, out_dtype=BF16, name=f"{tag}_gout")
    du1, du2, dgate = gate_bwd(dxb, w["w_out"], u1, u2, proj, name=f"{tag}_dgate")
    g_pd = matmul([(odil, du1)], "tn", tm=256, tn=1024, tk=2048, out_dtype=BF16, name=f"{tag}_gpd")
    g_ps = matmul([(osb, du2)], "tn", tm=256, tn=1024, tk=2048, out_dtype=BF16, name=f"{tag}_gps")
    dodil = matmul([(du1, w["w_proj_dil"])], "nt", tm=512, tn=256, tk=1024, out_dtype=F32, name=f"{tag}_dodil")
    dosb = matmul([(du2, w["w_proj_sb"])], "nt", tm=512, tn=256, tk=1024, out_dtype=F32, name=f"{tag}_dosb")
    dsum = head_sums(dodil, odil, name=f"{tag}_dsum")
    dqs, dks, dvs = [], [], []
    for g in range(N_DIL_GROUPS):
        dq, dk, dv = dil_bwd(qkr, proj, dodil, lse, dsum, g, name=f"{tag}_ddil{g}")
        dqs.append(dq.reshape(t, GROUP_W))
        dks.append(dk.reshape(t, GROUP_W))
        dvs.append(dv.reshape(t, GROUP_W).astype(BF16))
    dqk = rope_apply([(a, 0) for a in dqs + dks], tables, -1.0, name=f"{tag}_drope")
    gtot = head_sums(dosb, osb, round_a=True, name=f"{tag}_gsum")
    dq_s, dk_s, dv_s = sb_bwd(proj, dosb, gtot, name=f"{tag}_dsb")
    dproj = jnp.concatenate([dqk] + dvs + [dq_s.astype(BF16), dk_s.astype(BF16), dv_s.astype(BF16), dgate], axis=1)
    g_in = matmul([(h, dproj)], "tn", tm=1024, tn=1280, tk=2048, out_dtype=BF16, name=f"{tag}_gin")
    return {"w_in": g_in, "w_proj_dil": g_pd, "w_proj_sb": g_ps, "w_out": g_out}, dproj


def mixer_bwd_input(dx, dproj, saved, gain, w, tag):
    x = saved[0]
    dh = matmul([(dproj, w["w_in"])], "nt", tm=256, tn=1024, tk=5120, out_dtype=F32, name=f"{tag}_dh")
    return rms_bwd(dh, x, gain, dx, name=f"{tag}_dnorm")


def kernel(x, norm_ffn1, ffn1_w_gate, ffn1_w_up, ffn1_w_down, norm_mix, w_in, w_proj_dil, w_proj_sb, w_out, norm_ffn2, ffn2_w_gate, ffn2_w_up, ffn2_w_down, norm_final, loss_target, m_norm_ffn1, m_ffn1_w_gate, m_ffn1_w_up, m_ffn1_w_down, m_norm_mix, m_w_in, m_w_proj_dil, m_w_proj_sb, m_w_out, m_norm_ffn2, m_ffn2_w_gate, m_ffn2_w_up, m_ffn2_w_down, m_norm_final, v_norm_ffn1, v_ffn1_w_gate, v_ffn1_w_up, v_ffn1_w_down, v_norm_mix, v_w_in, v_w_proj_dil, v_w_proj_sb, v_w_out, v_norm_ffn2, v_ffn2_w_gate, v_ffn2_w_up, v_ffn2_w_down, v_norm_final):
    args = dict(locals())
    names = [name for name, _, _, _ in PACK_LAYOUT]
    t = x.shape[1]
    xs = x.reshape(t, D_MODEL)
    target = loss_target.reshape(t, D_MODEL)
    tables = rope_tables(t)

    my_index = 4 * lax.axis_index("x") + 2 * lax.axis_index("y") + lax.axis_index("c")
    parts = [(l, p) for l in range(2) for p in SUBBLOCKS]
    w_shards = {n: args[n] for n in names}
    gains = {n: args[n] for n in NORM_ROWS}

    packed_w = {lp: pack_shards(w_shards, lp[0], lp[1], BF16) for lp in parts}
    gathered = {parts[0]: all_gather_rows(packed_w[parts[0]], name="gather_l0_ffn1")}
    in_flight, order_token = {}, jnp.zeros((1, 1), F32)
    for l, p in parts[1:]:
        in_flight[(l, p)] = spread_start(packed_w[(l, p)], landing_zone(packed_w[(l, p)], my_index), per_peer=False,
                                         name=f"gather_start_l{l}_{p}")
        order_token = order_token + in_flight[(l, p)][4][0:1, 0:1]

    def weights_of(l, p, after):
        if (l, p) not in gathered:
            gathered[(l, p)] = spread_wait(in_flight[(l, p)], after, per_peer=False, name=f"gather_wait_l{l}_{p}")
        return unpack_gathered(gathered[(l, p)], p)

    saved, weights = {}, {}
    act = xs
    for l in range(2):
        for p in SUBBLOCKS:
            weights[(l, p)] = weights_of(l, p, act)
            gain = _row(gains["norm_" + p][l])
            if (l, p) == parts[0]:
                gain = gain + order_token
            if p == "mix":
                act, saved[(l, p)] = mixer_fwd(act, gain, weights[(l, p)], tables, f"l{l}_mix")
            else:
                act, saved[(l, p)] = ffn_fwd(act, gain, weights[(l, p)], p, f"l{l}_{p}")
    dx, dxb, g_final, loss_part = final_loss(act, _row(norm_final), target, name="loss_head")

    gain_grads, sent = {}, {}
    order_token = jnp.zeros((1, 1), F32)
    for l, p in reversed(parts):
        w, sv = weights[(l, p)], saved[(l, p)]
        if p == "mix":
            gw, rest = mixer_bwd_weights(dxb, sv, w, tables, f"l{l}_mix")
        else:
            gw, rest = ffn_bwd_weights(dxb, sv, w, p, f"l{l}_{p}")
        slices = pack_full_grads(gw, p)
        own = lax.dynamic_index_in_dim(slices, my_index, 0, keepdims=False)
        sent[(l, p)] = spread_start(slices, landing_zone(own, my_index), per_peer=True, name=f"reduce_start_l{l}_{p}")
        gain = _row(gains["norm_" + p][l]) + sent[(l, p)][4][0:1, 0:1]
        if p == "mix":
            dx, dxb, gain_grads[("norm_mix", l)] = mixer_bwd_input(dx, rest, sv, gain, w, f"l{l}_mix")
        else:
            dx, dxb, gain_grads[("norm_" + p, l)] = ffn_bwd_input(dx, rest, sv, gain, w, p, f"l{l}_{p}")

    loss_row = jnp.pad(loss_part[:, :1], ((0, 0), (0, LANES - 1)))
    small = jnp.concatenate([gain_grads[(n, l)] for n in NORM_ROWS for l in range(2)] + [g_final, loss_row], axis=0)
    small_parts = all_gather_rows(small, name="gather_gain_grads")
    zero_row = jnp.zeros((1, LANES), F32)
    small_of = lambda pre: jnp.concatenate([args[pre + n] for n in NORM_ROWS] + [_row(args[pre + "norm_final"]), zero_row], axis=0)
    small_out = sum_adamw(small_parts, small_of(""), small_of("m_"), small_of("v_"), tr=8, name="update_gains")

    m_shards = {n: args["m_" + n] for n in names}
    v_shards = {n: args["v_" + n] for n in names}
    big_out = [{}, {}, {}, {}]
    for l, p in reversed(parts):
        partials = spread_wait(sent[(l, p)], dx, per_peer=True, name=f"reduce_wait_l{l}_{p}")
        outs = sum_adamw(partials, pack_shards(w_shards, l, p, F32), pack_shards(m_shards, l, p, F32),
                         pack_shards(v_shards, l, p, F32), tr=UPDATE_ROWS[p], name=f"update_l{l}_{p}")
        for kind, packed in enumerate(outs):
            for n, arr in unpack_shards(packed, p, w_shards).items():
                big_out[kind].setdefault(n, [None, None])[l] = arr

    def gains_of(s):
        out = {n: s[2 * i:2 * i + 2] for i, n in enumerate(NORM_ROWS)}
        out["norm_final"] = s[6]
        return out

    order = ["norm_ffn1", "ffn1_w_gate", "ffn1_w_up", "ffn1_w_down", "norm_mix", "w_in", "w_proj_dil", "w_proj_sb", "w_out",
             "norm_ffn2", "ffn2_w_gate", "ffn2_w_up", "ffn2_w_down", "norm_final"]
    results = []
    for kind in range(4):
        both = {n: jnp.stack(pair, axis=0) for n, pair in big_out[kind].items()}
        both.update(gains_of(small_out[kind]))
        results += [both[n] for n in order]
    loss = small_out[0][7, 0]
    return (loss, dx.reshape(1, t, D_MODEL), *results)
```

```python
import functools

import jax
import jax.numpy as jnp
from jax import lax
from jax.experimental import pallas as pl
from jax.experimental.pallas import tpu as pltpu

F32 = jnp.float32
BF16 = jnp.bfloat16

D_MODEL = 1024
HEAD_DIM = 64
GROUP_W = 256
D_IN = 5120
N_DIL_GROUPS = 3
DIL_SPAN = 128
DILATIONS = (1, 4, 16)
ROPE_THETA = 500000.0
ROPE_DIM = 16
RMS_EPS = 1e-6
ATT_SCALE = HEAD_DIM ** -0.5
QS_BLK, KS_BLK, VS_BLK = 9, 10, 11
GATE_DIL_BLK, GATE_SB_BLK = 3, 4

ADAM_LR, ADAM_B1, ADAM_B2, ADAM_EPS, ADAM_WD, ADAM_STEP = 0.001, 0.9, 0.999, 1e-08, 0.01, 10

N_DEV = 8
LANES = 1024
VMEM_PHYSICAL_V7X = 64 << 20
VMEM_TEMP_HEADROOM = 20 << 20

PACK_LAYOUT = (
    ("ffn1_w_gate", 352, True, (1024, 2816)),
    ("ffn1_w_up", 352, True, (1024, 2816)),
    ("ffn1_w_down", 352, False, (2816, 1024)),
    ("w_in", 640, True, (1024, 5120)),
    ("w_proj_dil", 32, True, (256, 1024)),
    ("w_proj_sb", 32, True, (256, 1024)),
    ("w_out", 128, False, (1024, 1024)),
    ("ffn2_w_gate", 352, True, (1024, 2816)),
    ("ffn2_w_up", 352, True, (1024, 2816)),
    ("ffn2_w_down", 352, False, (2816, 1024)),
)
SUBBLOCKS = {"ffn1": PACK_LAYOUT[0:3], "mix": PACK_LAYOUT[3:7], "ffn2": PACK_LAYOUT[7:10]}
UPDATE_ROWS = {"ffn1": 352, "mix": 416, "ffn2": 352}
NORM_ROWS = ("norm_ffn1", "norm_mix", "norm_ffn2")


def _nbytes(shape, dtype):
    n = 1
    for s in shape:
        n *= s
    return n * jnp.dtype(dtype).itemsize


def _pcall(body, *, name, grid, in_specs, out_specs, out_shape, blocks, scratch_shapes=(), scratch_bytes=0):
    need = 2 * sum(_nbytes(s, d) for s, d in blocks) + scratch_bytes + VMEM_TEMP_HEADROOM
    limit = min(need, VMEM_PHYSICAL_V7X - (4 << 20))
    return pl.pallas_call(
        body, name=name, grid=grid, in_specs=in_specs, out_specs=out_specs, out_shape=out_shape,
        scratch_shapes=scratch_shapes,
        compiler_params=pltpu.CompilerParams(vmem_limit_bytes=limit),
    )


def _dot(a, b, form):
    dn = {"nn": (((1,), (0,)), ((), ())), "nt": (((1,), (1,)), ((), ())), "tn": (((0,), (0,)), ((), ()))}[form]
    return lax.dot_general(a.astype(BF16), b.astype(BF16), dn, preferred_element_type=F32)


def _sigmoid(x):
    return 1.0 / (1.0 + jnp.exp(-x))


def matmul(pairs, form, *, tm, tn, tk, out_dtype, name, scale=1.0, res=None):
    a0, b0 = pairs[0]
    if form == "tn":
        kdim, m = a0.shape
        n = b0.shape[1]
    else:
        m, kdim = a0.shape
        n = b0.shape[1] if form == "nn" else b0.shape[0]
    tm, tn, tk = min(tm, m), min(tn, n), min(tk, kdim)
    assert m % tm == 0 and n % tn == 0 and kdim % tk == 0, (name, m, n, kdim, tm, tn, tk)
    nk = kdim // tk
    npairs = len(pairs)

    if form == "tn":
        a_blk, a_map = (tk, tm), (lambda j, i, k: (k, i))
    else:
        a_blk, a_map = (tm, tk), (lambda j, i, k: (i, k))
    if form == "nt":
        b_blk, b_map = (tn, tk), (lambda j, i, k: (j, k))
    else:
        b_blk, b_map = (tk, tn), (lambda j, i, k: (k, j))
    o_map = lambda j, i, k: (i, j)

    def body(*refs):
        ab = refs[:2 * npairs]
        rest = refs[2 * npairs:]
        if res is not None:
            r_ref, o_ref = rest[0], rest[1]
            rest = rest[2:]
        else:
            r_ref, o_ref = None, rest[0]
            rest = rest[1:]

        def partial_sum():
            p = _dot(ab[0][...], ab[1][...], form)
            for q in range(1, npairs):
                p = p + _dot(ab[2 * q][...], ab[2 * q + 1][...], form)
            return p

        def finish(acc):
            out = acc * scale if scale != 1.0 else acc
            if r_ref is not None:
                out = r_ref[...] + out
            o_ref[...] = out.astype(out_dtype)

        if nk == 1:
            finish(partial_sum())
        else:
            acc_ref = rest[0]
            k = pl.program_id(2)

            @pl.when(k == 0)
            def _():
                acc_ref[...] = partial_sum()

            @pl.when(k > 0)
            def _():
                acc_ref[...] += partial_sum()

            @pl.when(k == nk - 1)
            def _():
                finish(acc_ref[...])

    in_specs, args, blocks = [], [], []
    for a, b in pairs:
        in_specs += [pl.BlockSpec(a_blk, a_map), pl.BlockSpec(b_blk, b_map)]
        args += [a, b]
        blocks += [(a_blk, a.dtype), (b_blk, b.dtype)]
    if res is not None:
        in_specs.append(pl.BlockSpec((tm, tn), o_map))
        args.append(res)
        blocks.append(((tm, tn), res.dtype))
    blocks.append(((tm, tn), out_dtype))
    scratch = [pltpu.VMEM((tm, tn), F32)] if nk > 1 else []
    return _pcall(
        body, name=name, grid=(n // tn, m // tm, nk), in_specs=in_specs,
        out_specs=pl.BlockSpec((tm, tn), o_map), out_shape=jax.ShapeDtypeStruct((m, n), out_dtype),
        blocks=blocks, scratch_shapes=scratch, scratch_bytes=(tm * tn * 4 if nk > 1 else 0),
    )(*args)


def swiglu_fwd(h, wg_t, wu_t, *, name, tm=512, tn=1408):
    t, d = h.shape
    f = wg_t.shape[0]
    tm, tn = min(tm, t), min(tn, f)

    def body(h_ref, wg_ref, wu_ref, a_ref, b_ref, s_ref):
        hh = h_ref[...]
        a = _dot(hh, wg_ref[...], "nt")
        b = _dot(hh, wu_ref[...], "nt")
        a_ref[...] = a.astype(BF16)
        b_ref[...] = b.astype(BF16)
        s_ref[...] = (a * _sigmoid(a) * b).astype(BF16)

    w_spec = pl.BlockSpec((tn, d), lambda j, i: (j, 0))
    o_spec = pl.BlockSpec((tm, tn), lambda j, i: (i, j))
    o_shape = jax.ShapeDtypeStruct((t, f), BF16)
    return _pcall(
        body, name=name, grid=(f // tn, t // tm),
        in_specs=[pl.BlockSpec((tm, d), lambda j, i: (i, 0)), w_spec, w_spec],
        out_specs=[o_spec, o_spec, o_spec], out_shape=[o_shape, o_shape, o_shape],
        blocks=[((tm, d), BF16), ((tn, d), BF16), ((tn, d), BF16)] + [((tm, tn), BF16)] * 3,
    )(h, wg_t, wu_t)


def swiglu_bwd(dyb, wd, a, b, *, name, scale, tm=512, tn=1408):
    t, d = dyb.shape
    f = wd.shape[0]
    tm, tn = min(tm, t), min(tn, f)

    def body(dy_ref, wd_ref, a_ref, b_ref, da_ref, db_ref):
        ds = _dot(dy_ref[...], wd_ref[...], "nt") * scale
        av = a_ref[...].astype(F32)
        bv = b_ref[...].astype(F32)
        sg = _sigmoid(av)
        da_ref[...] = (ds * bv * (sg * (1.0 + av * (1.0 - sg)))).astype(BF16)
        db_ref[...] = (ds * (av * sg)).astype(BF16)

    o_spec = pl.BlockSpec((tm, tn), lambda j, i: (i, j))
    o_shape = jax.ShapeDtypeStruct((t, f), BF16)
    return _pcall(
        body, name=name, grid=(f // tn, t // tm),
        in_specs=[pl.BlockSpec((tm, d), lambda j, i: (i, 0)), pl.BlockSpec((tn, d), lambda j, i: (j, 0)), o_spec, o_spec],
        out_specs=[o_spec, o_spec], out_shape=[o_shape, o_shape],
        blocks=[((tm, d), BF16), ((tn, d), BF16)] + [((tm, tn), BF16)] * 4,
    )(dyb, wd, a, b)


def gate_fwd(odil, osb, wpd_t, wps_t, proj, *, name, tm=512):
    t = odil.shape[0]
    tm = min(tm, t)

    def body(od_ref, os_ref, wpd_ref, wps_ref, g1_ref, g2_ref, y_ref, u1_ref, u2_ref):
        u1 = _dot(od_ref[...], wpd_ref[...], "nt")
        u2 = _dot(os_ref[...], wps_ref[...], "nt")
        y = _sigmoid(g1_ref[...].astype(F32)) * u1 + _sigmoid(g2_ref[...].astype(F32)) * u2
        y_ref[...] = y.astype(BF16)
        u1_ref[...] = u1.astype(BF16)
        u2_ref[...] = u2.astype(BF16)

    o_spec = pl.BlockSpec((tm, D_MODEL), lambda i: (i, 0))
    w_spec = pl.BlockSpec((D_MODEL, GROUP_W), lambda i: (0, 0))
    a_spec = pl.BlockSpec((tm, GROUP_W), lambda i: (i, 0))
    o_shape = jax.ShapeDtypeStruct((t, D_MODEL), BF16)
    return _pcall(
        body, name=name, grid=(t // tm,),
        in_specs=[a_spec, a_spec, w_spec, w_spec,
                  pl.BlockSpec((tm, D_MODEL), lambda i: (i, GATE_DIL_BLK)),
                  pl.BlockSpec((tm, D_MODEL), lambda i: (i, GATE_SB_BLK))],
        out_specs=[o_spec, o_spec, o_spec], out_shape=[o_shape, o_shape, o_shape],
        blocks=[((tm, GROUP_W), F32)] * 2 + [((D_MODEL, GROUP_W), BF16)] * 2 + [((tm, D_MODEL), BF16)] * 5,
    )(odil, osb, wpd_t, wps_t, proj, proj)


def gate_bwd(dxb, wout, u1, u2, proj, *, name, tm=512):
    t = dxb.shape[0]
    tm = min(tm, t)

    def body(dx_ref, w_ref, u1_ref, u2_ref, g1_ref, g2_ref, du1_ref, du2_ref, dg_ref):
        dy = _dot(dx_ref[...], w_ref[...], "nt")
        s1 = _sigmoid(g1_ref[...].astype(F32))
        s2 = _sigmoid(g2_ref[...].astype(F32))
        du1_ref[...] = (dy * s1).astype(BF16)
        du2_ref[...] = (dy * s2).astype(BF16)
        dg_ref[:, :D_MODEL] = (dy * u1_ref[...].astype(F32) * s1 * (1.0 - s1)).astype(BF16)
        dg_ref[:, D_MODEL:] = (dy * u2_ref[...].astype(F32) * s2 * (1.0 - s2)).astype(BF16)

    o_spec = pl.BlockSpec((tm, D_MODEL), lambda i: (i, 0))
    o_shape = jax.ShapeDtypeStruct((t, D_MODEL), BF16)
    return _pcall(
        body, name=name, grid=(t // tm,),
        in_specs=[o_spec, pl.BlockSpec((D_MODEL, D_MODEL), lambda i: (0, 0)), o_spec, o_spec,
                  pl.BlockSpec((tm, D_MODEL), lambda i: (i, GATE_DIL_BLK)),
                  pl.BlockSpec((tm, D_MODEL), lambda i: (i, GATE_SB_BLK))],
        out_specs=[o_spec, o_spec, pl.BlockSpec((tm, 2 * D_MODEL), lambda i: (i, 0))],
        out_shape=[o_shape, o_shape, jax.ShapeDtypeStruct((t, 2 * D_MODEL), BF16)],
        blocks=[((tm, D_MODEL), BF16)] * 9 + [((D_MODEL, D_MODEL), BF16)],
    )(dxb, wout, u1, u2, proj, proj)


def rms_fwd(x, gain, *, name, tm=512):
    t, d = x.shape
    tm = min(tm, t)

    def body(x_ref, g_ref, h_ref):
        xv = x_ref[...]
        rstd = lax.rsqrt(jnp.mean(xv * xv, axis=1, keepdims=True) + RMS_EPS)
        h_ref[...] = (xv * rstd * g_ref[...]).astype(BF16)

    return _pcall(
        body, name=name, grid=(t // tm,),
        in_specs=[pl.BlockSpec((tm, d), lambda i: (i, 0)), pl.BlockSpec((1, d), lambda i: (0, 0))],
        out_specs=pl.BlockSpec((tm, d), lambda i: (i, 0)), out_shape=jax.ShapeDtypeStruct((t, d), BF16),
        blocks=[((tm, d), F32), ((tm, d), BF16)],
    )(x, gain)


def rms_bwd(dh, x, gain, dres, *, name, tm=512):
    t, d = x.shape
    tm = min(tm, t)

    def body(dh_ref, x_ref, g_ref, dr_ref, dx_ref, dxb_ref, dg_ref):
        xv = x_ref[...]
        dhv = dh_ref[...]
        rstd = lax.rsqrt(jnp.mean(xv * xv, axis=1, keepdims=True) + RMS_EPS)
        xh = xv * rstd
        dxh = dhv * g_ref[...]
        dx = dr_ref[...] + rstd * (dxh - xh * jnp.mean(dxh * xh, axis=1, keepdims=True))
        dx_ref[...] = dx
        dxb_ref[...] = dx.astype(BF16)
        part = jnp.sum(dhv * xh, axis=0, keepdims=True)

        @pl.when(pl.program_id(0) == 0)
        def _():
            dg_ref[...] = part

        @pl.when(pl.program_id(0) > 0)
        def _():
            dg_ref[...] += part

    row = pl.BlockSpec((tm, d), lambda i: (i, 0))
    vec = pl.BlockSpec((1, d), lambda i: (0, 0))
    return _pcall(
        body, name=name, grid=(t // tm,), in_specs=[row, row, vec, row], out_specs=[row, row, vec],
        out_shape=[jax.ShapeDtypeStruct((t, d), F32), jax.ShapeDtypeStruct((t, d), BF16), jax.ShapeDtypeStruct((1, d), F32)],
        blocks=[((tm, d), F32)] * 4 + [((tm, d), BF16)],
    )(dh, x, gain, dres)


def final_loss(x, gain, target, *, name, tm=512):
    t, d = x.shape
    tm = min(tm, t)

    def body(x_ref, g_ref, t_ref, dx_ref, dxb_ref, dg_ref, loss_ref):
        xv = x_ref[...]
        g = g_ref[...]
        rstd = lax.rsqrt(jnp.mean(xv * xv, axis=1, keepdims=True) + RMS_EPS)
        xh = xv * rstd
        err = xh * g - t_ref[...]
        dy = err * (1.0 / d)
        dxh = dy * g
        dx = rstd * (dxh - xh * jnp.mean(dxh * xh, axis=1, keepdims=True))
        dx_ref[...] = dx
        dxb_ref[...] = dx.astype(BF16)
        part = jnp.sum(dy * xh, axis=0, keepdims=True)
        sq = jnp.sum(jnp.sum(err * err, axis=1, keepdims=True), axis=0, keepdims=True) * (0.5 / d)
        lpart = jnp.broadcast_to(sq, (1, 128))

        @pl.when(pl.program_id(0) == 0)
        def _():
            dg_ref[...] = part
            loss_ref[...] = lpart

        @pl.when(pl.program_id(0) > 0)
        def _():
            dg_ref[...] += part
            loss_ref[...] += lpart

    row = pl.BlockSpec((tm, d), lambda i: (i, 0))
    vec = pl.BlockSpec((1, d), lambda i: (0, 0))
    return _pcall(
        body, name=name, grid=(t // tm,), in_specs=[row, vec, row],
        out_specs=[row, row, vec, pl.BlockSpec((1, 128), lambda i: (0, 0))],
        out_shape=[jax.ShapeDtypeStruct((t, d), F32), jax.ShapeDtypeStruct((t, d), BF16),
                   jax.ShapeDtypeStruct((1, d), F32), jax.ShapeDtypeStruct((1, 128), F32)],
        blocks=[((tm, d), F32)] * 3 + [((tm, d), BF16)],
    )(x, gain, target)


def rope_tables(t):
    pos = jnp.arange(t, dtype=F32)
    inv_freq = ROPE_THETA ** (-jnp.arange(0, ROPE_DIM, 2, dtype=F32) / ROPE_DIM)
    ang = pos[:, None] * inv_freq[None, :]
    cos, sin = jnp.cos(ang), jnp.sin(ang)
    half = ROPE_DIM // 2
    pad = HEAD_DIM - ROPE_DIM
    one_head = lambda lo, hi, fill: jnp.concatenate([lo, hi, jnp.full((t, pad), fill, F32)], axis=1)
    zeros = jnp.zeros((t, half), F32)
    c = one_head(cos, cos, 1.0)
    sa = one_head(-sin, zeros, 0.0)
    sb = one_head(zeros, sin, 0.0)
    two = lambda a: jnp.concatenate([a, a], axis=1)
    return two(c), two(sa), two(sb)


def _rotate(xv, cv, sav, sbv):
    halves = []
    for half in range(2):
        x = xv[:, 128 * half:128 * (half + 1)]
        halves.append(x * cv + pltpu.roll(x, 120, 1) * sav + pltpu.roll(x, 8, 1) * sbv)
    return jnp.concatenate(halves, axis=1)


STAGE_CHUNKS = 4


def _stage(tm):
    return dict(scratch_shapes=[pltpu.VMEM((STAGE_CHUNKS, tm, 128), F32)], scratch_bytes=STAGE_CHUNKS * tm * 128 * 4)


def _split_residues(stage_ref, val, out_ref, d, col, dtype):
    rows, width = val.shape
    if d == 1:
        out_ref[0, :, col:col + width] = val.astype(dtype)
        return
    chunks = width // 128
    for c in range(chunks):
        stage_ref[c] = val[:, 128 * c:128 * (c + 1)]
    for r in range(d):
        for c in range(chunks):
            out_ref[r, :, col + 128 * c:col + 128 * (c + 1)] = stage_ref[c, pl.ds(r, rows // d, stride=d), :].astype(dtype)


def _join_residues(stage_ref, in_ref, d, col=0, width=GROUP_W):
    if d == 1:
        return in_ref[0, :, col:col + width].astype(F32)
    rows = in_ref.shape[1] * d
    chunks = width // 128
    for r in range(d):
        for c in range(chunks):
            stage_ref[c, pl.ds(r, rows // d, stride=d), :] = in_ref[r, :, col + 128 * c:col + 128 * (c + 1)].astype(F32)
    return jnp.concatenate([stage_ref[c] for c in range(chunks)], axis=1)


def rope_split(proj, tables, *, name, tm=512):
    c, sa, sb = tables
    t = c.shape[0]
    tm = min(tm, t)

    def body(*refs):
        pieces = refs[0:9]
        c_ref, sa_ref, sb_ref = refs[9:12]
        qk_out, v_out = refs[12:15], refs[15:18]
        stage = refs[18]
        cv, sav, sbv = c_ref[...], sa_ref[...], sb_ref[...]
        for g, d in enumerate(DILATIONS):
            for kind in range(3):
                xv = pieces[3 * kind + g][...].astype(F32)
                if kind < 2:
                    _split_residues(stage, _rotate(xv, cv, sav, sbv), qk_out[g], d, GROUP_W * kind, BF16)
                else:
                    _split_residues(stage, xv, v_out[g], d, 0, BF16)

    tab = pl.BlockSpec((tm, 128), lambda i: (i, 0))
    in_specs = [pl.BlockSpec((tm, GROUP_W), functools.partial(lambda i, cb: (i, cb), cb=cb)) for cb in range(9)]
    out_specs = ([pl.BlockSpec((d, tm // d, 2 * GROUP_W), lambda i: (0, i, 0)) for d in DILATIONS]
                 + [pl.BlockSpec((d, tm // d, GROUP_W), lambda i: (0, i, 0)) for d in DILATIONS])
    out_shape = ([jax.ShapeDtypeStruct((d, t // d, 2 * GROUP_W), BF16) for d in DILATIONS]
                 + [jax.ShapeDtypeStruct((d, t // d, GROUP_W), BF16) for d in DILATIONS])
    outs = _pcall(
        body, name=name, grid=(t // tm,), in_specs=in_specs + [tab, tab, tab], out_specs=out_specs, out_shape=out_shape,
        blocks=[((tm, GROUP_W), BF16)] * 18 + [((tm, 128), F32)] * 3,
        **_stage(tm),
    )(*([proj] * 9), c, sa, sb)
    return outs[0:3], outs[3:6]


def rope_join(dqs, dks, dvs, tables, *, name, tm=512):
    c, sa, sb = tables
    t = c.shape[0]
    tm = min(tm, t)

    def body(*refs):
        pieces = refs[0:9]
        c_ref, sa_ref, sb_ref = refs[9:12]
        o_ref, stage = refs[12], refs[13]
        cv, sav, sbv = c_ref[...], -sa_ref[...], -sb_ref[...]
        for kind in range(3):
            for g, d in enumerate(DILATIONS):
                xv = _join_residues(stage, pieces[3 * kind + g], d)
                if kind < 2:
                    xv = _rotate(xv, cv, sav, sbv)
                col = GROUP_W * (3 * kind + g)
                o_ref[:, col:col + GROUP_W] = xv.astype(BF16)

    tab = pl.BlockSpec((tm, 128), lambda i: (i, 0))
    in_specs = [pl.BlockSpec((d, tm // d, GROUP_W), lambda i: (0, i, 0)) for _ in range(3) for d in DILATIONS]
    return _pcall(
        body, name=name, grid=(t // tm,), in_specs=in_specs + [tab, tab, tab],
        out_specs=pl.BlockSpec((tm, 9 * GROUP_W), lambda i: (i, 0)), out_shape=jax.ShapeDtypeStruct((t, 9 * GROUP_W), BF16),
        blocks=[((tm, GROUP_W), F32)] * 9 + [((tm, 128), F32)] * 3 + [((tm, 9 * GROUP_W), BF16)],
        **_stage(tm),
    )(*dqs, *dks, *dvs, c, sa, sb)


def _head_mask(h):
    lane = lax.broadcasted_iota(jnp.int32, (1, GROUP_W), 1)
    return (lane // HEAD_DIM) == h


def _band_masks():
    ri = lax.broadcasted_iota(jnp.int32, (DIL_SPAN, DIL_SPAN), 0)
    ci = lax.broadcasted_iota(jnp.int32, (DIL_SPAN, DIL_SPAN), 1)
    return ci <= ri, ci >= ri


def dil_fwd(qk, v, *, name):
    d, nsub, _ = qk.shape
    nblk = nsub // DIL_SPAN

    def body(q_ref, kc_ref, kp_ref, vc_ref, vp_ref, o_ref, lse_ref):
        nb = pl.program_id(1)
        own, prev = _band_masks()
        prev = prev & (nb > 0)
        q, kc, kp, vc, vp = q_ref[0], kc_ref[0], kp_ref[0], vc_ref[0], vp_ref[0]
        o_acc = jnp.zeros((DIL_SPAN, GROUP_W), F32)
        for h in range(4):
            hm = _head_mask(h)
            qh = jnp.where(hm, q, jnp.zeros_like(q))
            sc = jnp.where(own, _dot(qh, kc, "nt") * ATT_SCALE, -jnp.inf)
            sp = jnp.where(prev, _dot(qh, kp, "nt") * ATT_SCALE, -jnp.inf)
            m = jnp.maximum(jnp.max(sc, axis=1, keepdims=True), jnp.max(sp, axis=1, keepdims=True))
            pc = jnp.exp(sc - m)
            pp = jnp.exp(sp - m)
            den = jnp.sum(pc, axis=1, keepdims=True) + jnp.sum(pp, axis=1, keepdims=True)
            oh = (_dot(pc, vc, "nn") + _dot(pp, vp, "nn")) / den
            o_acc = jnp.where(hm, oh, o_acc)
            lse_ref[0, :, 128 * h:128 * (h + 1)] = jnp.broadcast_to(m + jnp.log(den), (DIL_SPAN, 128))
        o_ref[0] = o_acc

    blk = (1, DIL_SPAN, GROUP_W)
    sblk = (1, DIL_SPAN, 512)
    prv = lambda nb: jnp.maximum(nb - 1, 0)
    return _pcall(
        body, name=name, grid=(d, nblk),
        in_specs=[pl.BlockSpec(blk, lambda r, nb: (r, nb, 0)),
                  pl.BlockSpec(blk, lambda r, nb: (r, nb, 1)),
                  pl.BlockSpec(blk, lambda r, nb: (r, prv(nb), 1)),
                  pl.BlockSpec(blk, lambda r, nb: (r, nb, 0)),
                  pl.BlockSpec(blk, lambda r, nb: (r, prv(nb), 0))],
        out_specs=[pl.BlockSpec(blk, lambda r, nb: (r, nb, 0)), pl.BlockSpec(sblk, lambda r, nb: (r, nb, 0))],
        out_shape=[jax.ShapeDtypeStruct((d, nsub, GROUP_W), F32), jax.ShapeDtypeStruct((d, nsub, 512), F32)],
        blocks=[(blk, BF16)] * 5 + [(blk, F32), (sblk, F32)],
    )(qk, qk, qk, v, v)


def dil_merge(outs, lses, *, name, tm=512):
    t = outs[0].shape[0] * outs[0].shape[1]
    tm = min(tm, t)

    def body(o0, o1, o2, l0, l1, l2, o_ref, lse_ref, stage):
        ls = [_join_residues(stage, l, d, 0, 512) for l, d in zip((l0, l1, l2), DILATIONS)]
        m = jnp.maximum(jnp.maximum(ls[0], ls[1]), ls[2])
        tot = m + jnp.log(jnp.exp(ls[0] - m) + jnp.exp(ls[1] - m) + jnp.exp(ls[2] - m))
        lse_ref[...] = tot
        lane = lax.broadcasted_iota(jnp.int32, (1, 128), 1)
        first = lane < HEAD_DIM
        acc = jnp.zeros((tm, GROUP_W), F32)
        for og, lg, d in zip((o0, o1, o2), ls, DILATIONS):
            w = jnp.exp(lg - tot)
            wide = jnp.concatenate([jnp.where(first, w[:, 0:128], w[:, 128:256]),
                                    jnp.where(first, w[:, 256:384], w[:, 384:512])], axis=1)
            acc = acc + wide * _join_residues(stage, og, d)
        o_ref[...] = acc

    o_in = [pl.BlockSpec((d, tm // d, GROUP_W), lambda i: (0, i, 0)) for d in DILATIONS]
    l_in = [pl.BlockSpec((d, tm // d, 512), lambda i: (0, i, 0)) for d in DILATIONS]
    return _pcall(
        body, name=name, grid=(t // tm,), in_specs=o_in + l_in,
        out_specs=[pl.BlockSpec((tm, GROUP_W), lambda i: (i, 0)), pl.BlockSpec((tm, 512), lambda i: (i, 0))],
        out_shape=[jax.ShapeDtypeStruct((t, GROUP_W), F32), jax.ShapeDtypeStruct((t, 512), F32)],
        blocks=[((tm, GROUP_W), F32)] * 4 + [((tm, 512), F32)] * 4,
        **_stage(tm),
    )(*outs, *lses)


def dil_bwd_prep(do, o, lse, *, name, tm=512):
    t = do.shape[0]
    tm = min(tm, t)
    wide = DILATIONS[1:]

    def body(do_ref, o_ref, lse_ref, ds_ref, *rest):
        do_out, lse_out, ds_out = rest[0:2], rest[2:4], rest[4:6]
        stage = rest[6]
        dov = do_ref[...]
        prod = dov * o_ref[...]
        for h in range(4):
            s = jnp.sum(jnp.where(_head_mask(h), prod, 0.0), axis=1, keepdims=True)
            ds_ref[:, 128 * h:128 * (h + 1)] = jnp.broadcast_to(s, (tm, 128))
        for i, d in enumerate(wide):
            _split_residues(stage, dov, do_out[i], d, 0, BF16)
            _split_residues(stage, lse_ref[...], lse_out[i], d, 0, F32)
            _split_residues(stage, ds_ref[...], ds_out[i], d, 0, F32)

    nat = lambda w: pl.BlockSpec((tm, w), lambda i: (i, 0))
    res = lambda d, w: pl.BlockSpec((d, tm // d, w), lambda i: (0, i, 0))
    shape = lambda d, w, dt: jax.ShapeDtypeStruct((d, t // d, w), dt)
    outs = _pcall(
        body, name=name, grid=(t // tm,), in_specs=[nat(GROUP_W), nat(GROUP_W), nat(512)],
        out_specs=[nat(512)] + [res(d, GROUP_W) for d in wide] + [res(d, 512) for d in wide] * 2,
        out_shape=([jax.ShapeDtypeStruct((t, 512), F32)] + [shape(d, GROUP_W, BF16) for d in wide]
                   + [shape(d, 512, F32) for d in wide] * 2),
        blocks=[((tm, GROUP_W), F32)] * 3 + [((tm, 512), F32)] * 6,
        **_stage(tm),
    )(do, o, lse)
    return outs[0], outs[1:3], outs[3:5], outs[5:7]


def head_sums(a, b, *, name, round_a=False, tm=512):
    t = a.shape[0]
    tm = min(tm, t)

    def body(a_ref, b_ref, o_ref):
        av = a_ref[...]
        if round_a:
            av = av.astype(BF16).astype(F32)
        prod = av * b_ref[...]
        for h in range(4):
            s = jnp.sum(jnp.where(_head_mask(h), prod, 0.0), axis=1, keepdims=True)
            o_ref[:, 128 * h:128 * (h + 1)] = jnp.broadcast_to(s, (tm, 128))

    spec = pl.BlockSpec((tm, GROUP_W), lambda i: (i, 0))
    return _pcall(
        body, name=name, grid=(t // tm,), in_specs=[spec, spec],
        out_specs=pl.BlockSpec((tm, 512), lambda i: (i, 0)), out_shape=jax.ShapeDtypeStruct((t, 512), F32),
        blocks=[((tm, GROUP_W), F32)] * 2 + [((tm, 512), F32)],
    )(a, b)


def dil_bwd(qk, v, do, lse, dsum, *, name):
    d, nsub, _ = qk.shape
    nblk = nsub // DIL_SPAN

    def body(qa_ref, qb_ref, kc_ref, kp_ref, vc_ref, vp_ref, doa_ref, dob_ref, la_ref, lb_ref, sa_ref, sb_ref,
             dq_ref, dk_ref, dv_ref):
        nb = pl.program_id(1)
        own, band = _band_masks()
        prev = band & (nb > 0)
        nxt = band & (nb < nblk - 1)
        qa, qb, kc, kp, vc, vp = qa_ref[0], qb_ref[0], kc_ref[0], kp_ref[0], vc_ref[0], vp_ref[0]
        doa, dob = doa_ref[0].astype(BF16), dob_ref[0].astype(BF16)
        dq = jnp.zeros((DIL_SPAN, GROUP_W), F32)
        dk = jnp.zeros((DIL_SPAN, GROUP_W), F32)
        dv = jnp.zeros((DIL_SPAN, GROUP_W), F32)
        for h in range(4):
            hm = _head_mask(h)
            sl = slice(128 * h, 128 * (h + 1))
            qah = jnp.where(hm, qa, jnp.zeros_like(qa))
            qbh = jnp.where(hm, qb, jnp.zeros_like(qb))
            dah = jnp.where(hm, doa, jnp.zeros_like(doa))
            dbh = jnp.where(hm, dob, jnp.zeros_like(dob))
            la, lb, sa, sb = la_ref[0, :, sl], lb_ref[0, :, sl], sa_ref[0, :, sl], sb_ref[0, :, sl]

            def probs(qh, dh, k, v, mask, l, s):
                p = jnp.where(mask, jnp.exp(_dot(qh, k, "nt") * ATT_SCALE - l), 0.0)
                dsc = p * (_dot(dh, v, "nt") - s) * ATT_SCALE
                return p.astype(BF16), dsc.astype(BF16)

            p_cc, ds_cc = probs(qah, dah, kc, vc, own, la, sa)
            _, ds_cp = probs(qah, dah, kp, vp, prev, la, sa)
            p_nc, ds_nc = probs(qbh, dbh, kc, vc, nxt, lb, sb)
            dq = jnp.where(hm, _dot(ds_cc, kc, "nn") + _dot(ds_cp, kp, "nn"), dq)
            dk = dk + _dot(ds_cc, qah, "tn") + _dot(ds_nc, qbh, "tn")
            dv = dv + _dot(p_cc, dah, "tn") + _dot(p_nc, dbh, "tn")
        dq_ref[0] = dq
        dk_ref[0] = dk
        dv_ref[0] = dv

    blk = (1, DIL_SPAN, GROUP_W)
    sblk = (1, DIL_SPAN, 512)
    prv = lambda nb: jnp.maximum(nb - 1, 0)
    nxt_ = lambda nb: jnp.minimum(nb + 1, nblk - 1)
    cur_at = lambda c: pl.BlockSpec(blk, functools.partial(lambda r, nb, c: (r, nb, c), c=c))
    prv_at = lambda c: pl.BlockSpec(blk, functools.partial(lambda r, nb, c: (r, prv(nb), c), c=c))
    nxt_at = lambda c: pl.BlockSpec(blk, functools.partial(lambda r, nb, c: (r, nxt_(nb), c), c=c))
    s_cur = pl.BlockSpec(sblk, lambda r, nb: (r, nb, 0))
    s_nxt = pl.BlockSpec(sblk, lambda r, nb: (r, nxt_(nb), 0))
    o_spec = pl.BlockSpec(blk, lambda r, nb: (r, nb, 0))
    o_shape = jax.ShapeDtypeStruct((d, nsub, GROUP_W), F32)
    return _pcall(
        body, name=name, grid=(d, nblk),
        in_specs=[cur_at(0), nxt_at(0), cur_at(1), prv_at(1), cur_at(0), prv_at(0), cur_at(0), nxt_at(0),
                  s_cur, s_nxt, s_cur, s_nxt],
        out_specs=[o_spec, o_spec, o_spec], out_shape=[o_shape, o_shape, o_shape],
        blocks=[(blk, BF16)] * 6 + [(blk, F32)] * 5 + [(sblk, F32)] * 4,
    )(qk, qk, qk, qk, v, v, do, do, lse, lse, dsum, dsum)


def _tri_dot(x, b):
    hi = x.astype(BF16)
    lo = (x - hi.astype(F32)).astype(BF16)
    return _dot(hi, b, "nn") + _dot(lo, b, "nn")


SB_TILE = 256


def _stack_heads(a):
    return jnp.concatenate([jnp.where(_head_mask(h), a, jnp.zeros_like(a)) for h in range(4)], axis=0)


def _unstack_heads(acc, rows):
    out = acc[0:rows]
    for h in range(1, 4):
        out = jnp.where(_head_mask(h), acc[h * rows:(h + 1) * rows], out)
    return out


def _tri_masks(n):
    ri = lax.broadcasted_iota(jnp.int32, (n, n), 0)
    ci = lax.broadcasted_iota(jnp.int32, (n, n), 1)
    return (ri > ci).astype(BF16), (ri >= ci).astype(BF16)


def _sb_weights(qs, kt, after, c_keep, diagonal):
    z = _dot(qs, kt, "nt")
    lbeta = jnp.minimum(z, 0.0) - jnp.log(1.0 + jnp.exp(-jnp.abs(z)))
    lkeep = lbeta - z
    past = None
    if diagonal:
        n = SB_TILE
        past = lax.broadcasted_iota(jnp.int32, z.shape, 1) < lax.broadcasted_iota(jnp.int32, z.shape, 0) % n
        lkeep = jnp.where(past, lkeep, 0.0)
    w = jnp.exp(lbeta + _tri_dot(lkeep, after) + c_keep)
    if diagonal:
        w = jnp.where(past, w, 0.0)
    return z, past, lbeta, lkeep, w


def sb_fwd(proj, *, name):
    t = proj.shape[0]
    n = SB_TILE
    assert t % n == 0

    def body(q_ref, k_ref, v_ref, o_ref, acc_ref):
        qb = pl.program_id(0)
        qs = _stack_heads(q_ref[...] * ATT_SCALE)
        after, _ = _tri_masks(n)

        def tile(off, diagonal, c_keep):
            kt = k_ref[pl.ds(off, n), :]
            vt = v_ref[pl.ds(off, n), :]
            _, _, _, lkeep, w = _sb_weights(qs, kt, after, c_keep, diagonal)
            pv = _tri_dot(w, vt)
            if diagonal:
                acc_ref[...] = pv
            else:
                acc_ref[...] += pv
            return c_keep + jnp.sum(lkeep, axis=1, keepdims=True)

        c0 = tile(pl.multiple_of(qb * n, n), True, jnp.zeros((4 * n, 1), F32))
        lax.fori_loop(0, qb, lambda it, c: tile(pl.multiple_of((qb - 1 - it) * n, n), False, c), c0)
        o_ref[...] = _unstack_heads(acc_ref[...], n)

    full = lambda cb: pl.BlockSpec((t, GROUP_W), functools.partial(lambda i, cb: (0, cb), cb=cb))
    return _pcall(
        body, name=name, grid=(t // n,),
        in_specs=[pl.BlockSpec((n, GROUP_W), lambda i: (i, QS_BLK)), full(KS_BLK), full(VS_BLK)],
        out_specs=pl.BlockSpec((n, GROUP_W), lambda i: (i, 0)), out_shape=jax.ShapeDtypeStruct((t, GROUP_W), F32),
        blocks=[((n, GROUP_W), BF16), ((t, GROUP_W), BF16), ((t, GROUP_W), BF16), ((n, GROUP_W), F32)],
        scratch_shapes=[pltpu.VMEM((4 * n, GROUP_W), F32)], scratch_bytes=4 * n * GROUP_W * 4,
    )(proj, proj, proj)


def sb_bwd(proj, do, gtot, *, name):
    t = proj.shape[0]
    n = SB_TILE
    assert t % n == 0

    def body(q_ref, k_ref, v_ref, do_ref, gt_ref, dq_ref, dk_ref, dv_ref, acc_ref):
        qb = pl.program_id(0)

        @pl.when(qb == 0)
        def _():
            dk_ref[...] = jnp.zeros_like(dk_ref)
            dv_ref[...] = jnp.zeros_like(dv_ref)

        qs = _stack_heads(q_ref[...] * ATT_SCALE)
        dos = _stack_heads(do_ref[...].astype(BF16))
        gt = jnp.concatenate([jnp.max(gt_ref[:, 128 * h:128 * (h + 1)], axis=1, keepdims=True) for h in range(4)], axis=0)
        after, from_on = _tri_masks(n)

        def tile(off, diagonal, carry):
            c_keep, c_g = carry
            kt = k_ref[pl.ds(off, n), :]
            vt = v_ref[pl.ds(off, n), :]
            z, past, lbeta, lkeep, w = _sb_weights(qs, kt, after, c_keep, diagonal)
            gw = w * _dot(dos, vt, "nt")
            big_g = gt - (_tri_dot(gw, from_on) + c_g)
            dz = gw * jnp.exp(lbeta - z) - big_g * jnp.exp(lbeta)
            if diagonal:
                dz = jnp.where(past, dz, 0.0)
            dz = dz.astype(BF16)
            dk_ref[pl.ds(off, n), :] += _dot(dz, qs, "tn")
            dv_ref[pl.ds(off, n), :] += _dot(w, dos, "tn")
            dq = _dot(dz, kt, "nn")
            if diagonal:
                acc_ref[...] = dq
            else:
                acc_ref[...] += dq
            return c_keep + jnp.sum(lkeep, axis=1, keepdims=True), c_g + jnp.sum(gw, axis=1, keepdims=True)

        zero_col = jnp.zeros((4 * n, 1), F32)
        c0 = tile(pl.multiple_of(qb * n, n), True, (zero_col, zero_col))
        lax.fori_loop(0, qb, lambda it, c: tile(pl.multiple_of((qb - 1 - it) * n, n), False, c), c0)
        dq_ref[...] = _unstack_heads(acc_ref[...], n) * ATT_SCALE

    full = lambda cb: pl.BlockSpec((t, GROUP_W), functools.partial(lambda i, cb: (0, cb), cb=cb))
    whole = pl.BlockSpec((t, GROUP_W), lambda i: (0, 0))
    rowblk = pl.BlockSpec((n, GROUP_W), lambda i: (i, 0))
    shape = jax.ShapeDtypeStruct((t, GROUP_W), F32)
    return _pcall(
        body, name=name, grid=(t // n,),
        in_specs=[pl.BlockSpec((n, GROUP_W), lambda i: (i, QS_BLK)), full(KS_BLK), full(VS_BLK), rowblk,
                  pl.BlockSpec((n, 512), lambda i: (i, 0))],
        out_specs=[rowblk, whole, whole], out_shape=[shape, shape, shape],
        blocks=[((n, GROUP_W), BF16), ((t, GROUP_W), BF16), ((t, GROUP_W), BF16), ((n, GROUP_W), F32),
                ((n, 512), F32), ((n, GROUP_W), F32), ((t, GROUP_W), F32), ((t, GROUP_W), F32)],
        scratch_shapes=[pltpu.VMEM((4 * n, GROUP_W), F32)], scratch_bytes=4 * n * GROUP_W * 4,
    )(proj, proj, proj, do, gtot)


def _mesh_place():
    return lax.axis_index("x"), lax.axis_index("y"), lax.axis_index("c")


def _flip(place, mask):
    x, y, c = place
    return ((1 - x) if mask & 4 else x, (1 - y) if mask & 2 else y, (1 - c) if mask & 1 else c)


def _dev_index(place):
    x, y, c = place
    return 4 * x + 2 * y + c


HBM_SPEC = pl.BlockSpec(memory_space=pltpu.HBM)


def all_gather_rows(shard, *, name):
    rows, lanes = shard.shape

    def body(x_ref, out_ref, send_sems, recv_sems, local_sem):
        me = _mesh_place()
        x, y, c = me
        sibling = _flip(me, 1)
        chips = [_flip(me, 4), _flip(me, 2), _flip(me, 6)]

        def copy(k, block, to, src=None):
            dst = out_ref.at[_dev_index(block)]
            return pltpu.make_async_remote_copy(
                src_ref=dst if src is None else src, dst_ref=dst, send_sem=send_sems.at[k], recv_sem=recv_sems.at[k],
                device_id=to, device_id_type=pl.DeviceIdType.MESH)

        mine = pltpu.make_async_copy(x_ref, out_ref.at[_dev_index(me)], local_sem)
        mine.start()
        first = [copy(0, me, sibling, src=x_ref)] + [copy(1 + j, me, chip, src=x_ref) for j, chip in enumerate(chips)]
        for cp in first:
            cp.start()
        passed = [copy(4 + j, chip, sibling) for j, chip in enumerate(chips)]
        for j, chip in enumerate(chips):
            copy(1 + j, chip, me).wait_recv()
            passed[j].start()
        copy(0, sibling, me).wait_recv()
        for j, chip in enumerate(chips):
            copy(4 + j, _flip(chip, 1), me).wait_recv()
        for cp in first + passed:
            cp.wait_send()
        mine.wait()

    return pl.pallas_call(
        body, name=name, in_specs=[HBM_SPEC], out_specs=HBM_SPEC,
        out_shape=jax.ShapeDtypeStruct((N_DEV, rows, lanes), shard.dtype),
        scratch_shapes=[pltpu.SemaphoreType.DMA((7,)), pltpu.SemaphoreType.DMA((7,)), pltpu.SemaphoreType.DMA],
    )(shard)


SEM_SPEC = pl.BlockSpec(memory_space=pltpu.SEMAPHORE)
DATAFLOW_EFFECT = pltpu.SideEffectType.DATAFLOW_SIDE_EFFECTING


def _spread_copies(src_ref, land_ref, send_sems, recv_sems, per_peer, arriving):
    me = _mesh_place()
    out = []
    for mask in range(1, N_DEV):
        peer = _flip(me, mask)
        data_of = _dev_index(me) if arriving else _dev_index(peer)
        slot = _dev_index(peer) if arriving else _dev_index(me)
        out.append(pltpu.make_async_remote_copy(
            src_ref=src_ref.at[data_of] if per_peer else src_ref, dst_ref=land_ref.at[slot],
            send_sem=send_sems.at[mask - 1], recv_sem=recv_sems.at[mask - 1],
            device_id=peer, device_id_type=pl.DeviceIdType.MESH))
    return out


def spread_start(src, land, *, per_peer, name):
    def body(src_ref, land_ref, send_sems, recv_sems, src_thru, land_thru, token):
        for cp in _spread_copies(src_ref, land_ref, send_sems, recv_sems, per_peer, arriving=False):
            cp.start()
        token[...] = jnp.zeros_like(token)

    return pl.pallas_call(
        body, name=name, in_specs=(HBM_SPEC, HBM_SPEC),
        out_shape=(pltpu.SemaphoreType.DMA((N_DEV - 1,)), pltpu.SemaphoreType.DMA((N_DEV - 1,)),
                   pltpu.HBM(src.shape, src.dtype), pltpu.HBM(land.shape, land.dtype), jax.ShapeDtypeStruct((8, 128), F32)),
        out_specs=(SEM_SPEC, SEM_SPEC, HBM_SPEC, HBM_SPEC, pl.BlockSpec(memory_space=pltpu.VMEM)),
        input_output_aliases={0: 2, 1: 3},
        compiler_params=pltpu.CompilerParams(has_side_effects=DATAFLOW_EFFECT),
    )(pltpu.with_memory_space_constraint(src, pltpu.HBM), pltpu.with_memory_space_constraint(land, pltpu.HBM))


def spread_wait(started, after, *, per_peer, name):
    send_sems, recv_sems, src_thru, land_thru, _ = started

    def body(src_ref, land_ref, send_sems, recv_sems, after_ref, src_dead, got_ref):
        for cp in _spread_copies(src_ref, land_ref, send_sems, recv_sems, per_peer, arriving=True):
            cp.wait_send()
            cp.wait_recv()

    return pl.pallas_call(
        body, name=name, in_specs=(HBM_SPEC, HBM_SPEC, SEM_SPEC, SEM_SPEC, pl.BlockSpec(memory_space=pl.ANY)),
        out_shape=(pltpu.HBM(src_thru.shape, src_thru.dtype), pltpu.HBM(land_thru.shape, land_thru.dtype)),
        out_specs=(HBM_SPEC, HBM_SPEC), input_output_aliases={0: 0, 1: 1},
        compiler_params=pltpu.CompilerParams(has_side_effects=DATAFLOW_EFFECT),
    )(src_thru, land_thru, send_sems, recv_sems, after)[1]


def landing_zone(own_block, my_index):
    zone = lax.empty((N_DEV,) + own_block.shape, own_block.dtype)
    return lax.dynamic_update_slice(zone, own_block[None], (my_index,) + (0,) * own_block.ndim)


def sum_partials(parts, *, name, tr):
    _, rows, lanes = parts.shape
    assert rows % tr == 0

    def body(p_ref, g_ref):
        g = p_ref[0].astype(F32)
        for k in range(1, N_DEV):
            g = g + p_ref[k].astype(F32)
        g_ref[...] = g

    return _pcall(
        body, name=name, grid=(rows // tr,),
        in_specs=[pl.BlockSpec((N_DEV, tr, lanes), lambda i: (0, i, 0))],
        out_specs=pl.BlockSpec((tr, lanes), lambda i: (i, 0)), out_shape=jax.ShapeDtypeStruct((rows, lanes), F32),
        blocks=[((N_DEV, tr, lanes), parts.dtype), ((tr, lanes), F32)],
    )(parts)


def adamw(g, w, m, v, *, name, tr):
    nl, k, n = w.shape
    tr = min(tr, k)
    assert k % tr == 0
    bc1 = 1.0 - ADAM_B1 ** ADAM_STEP
    bc2 = 1.0 - ADAM_B2 ** ADAM_STEP

    def body(g_ref, w_ref, m_ref, v_ref, d_ref, mo_ref, vo_ref):
        gv = g_ref[...]
        m_new = ADAM_B1 * m_ref[...] + (1.0 - ADAM_B1) * gv
        v_new = ADAM_B2 * v_ref[...] + (1.0 - ADAM_B2) * (gv * gv)
        mo_ref[...] = m_new
        vo_ref[...] = v_new
        d_ref[...] = -ADAM_LR * ((m_new / bc1) / (jnp.sqrt(v_new / bc2) + ADAM_EPS) + ADAM_WD * w_ref[...])

    spec = pl.BlockSpec((1, tr, n), lambda l, i: (l, i, 0))
    shape = jax.ShapeDtypeStruct(w.shape, F32)
    return _pcall(
        body, name=name, grid=(nl, k // tr), in_specs=[spec] * 4, out_specs=[spec] * 3, out_shape=[shape] * 3,
        blocks=[((1, tr, n), F32)] * 7,
    )(g, w, m, v)


def pack_shards(tensors, layer, part):
    rows = []
    for name, r, by_cols, _ in SUBBLOCKS[part]:
        shard = tensors[name][layer].astype(BF16)
        rows.append((shard.T if by_cols else shard).reshape(r, LANES))
    return jnp.concatenate(rows, axis=0)


def unpack_gathered(gathered, part):
    ws, r0 = {}, 0
    for name, r, by_cols, (k, n) in SUBBLOCKS[part]:
        ws[name] = gathered[:, r0:r0 + r, :].reshape((n, k) if by_cols else (k, n))
        r0 += r
    return ws


def pack_full_grads(grads, part):
    return jnp.concatenate([grads[name].reshape(N_DEV, r, LANES) for name, r, _, _ in SUBBLOCKS[part]], axis=1)


def unpack_summed(summed, part, like):
    out, r0 = {}, 0
    for name, r, by_cols, _ in SUBBLOCKS[part]:
        k, n = like[name].shape[1:]
        seg = summed[r0:r0 + r]
        out[name] = seg.reshape(n, k).T if by_cols else seg.reshape(k, n)
        r0 += r
    return out


def _row(v):
    return v.reshape(1, -1)


def ffn_fwd(x, gain, w, pre, tag):
    h = rms_fwd(x, gain, name=f"{tag}_norm")
    a, b, s = swiglu_fwd(h, w[pre + "_w_gate"], w[pre + "_w_up"], name=f"{tag}_gateup")
    out = matmul([(s, w[pre + "_w_down"])], "nn", tm=512, tn=1024, tk=2816, out_dtype=F32, scale=0.5, res=x, name=f"{tag}_down")
    return out, (x, h, a, b, s)


def ffn_bwd_weights(dxb, saved, w, pre, tag):
    x, h, a, b, s = saved
    da, db = swiglu_bwd(dxb, w[pre + "_w_down"], a, b, scale=0.5, name=f"{tag}_dgateup")
    g_down = matmul([(s, dxb)], "tn", tm=1408, tn=1024, tk=2048, out_dtype=BF16, scale=0.5, name=f"{tag}_gdown")
    g_gate = matmul([(da, h)], "tn", tm=1408, tn=1024, tk=2048, out_dtype=BF16, name=f"{tag}_ggate")
    g_up = matmul([(db, h)], "tn", tm=1408, tn=1024, tk=2048, out_dtype=BF16, name=f"{tag}_gup")
    return {pre + "_w_gate": g_gate, pre + "_w_up": g_up, pre + "_w_down": g_down}, (da, db)


def ffn_bwd_input(dx, rest, saved, gain, w, pre, tag):
    da, db = rest
    x = saved[0]
    dh = matmul([(da, w[pre + "_w_gate"]), (db, w[pre + "_w_up"])], "nn", tm=256, tn=1024, tk=2816, out_dtype=F32, name=f"{tag}_dh")
    return rms_bwd(dh, x, gain, dx, name=f"{tag}_dnorm")


def mixer_fwd(x, gain, w, tables, tag):
    h = rms_fwd(x, gain, name=f"{tag}_norm")
    proj = matmul([(h, w["w_in"])], "nt", tm=512, tn=1280, tk=1024, out_dtype=BF16, name=f"{tag}_in")
    qks, vs = rope_split(proj, tables, name=f"{tag}_rope")
    outs, lses = [], []
    for g in range(N_DIL_GROUPS):
        o, lse = dil_fwd(qks[g], vs[g], name=f"{tag}_dil{g}")
        outs.append(o)
        lses.append(lse)
    odil, lse = dil_merge(outs, lses, name=f"{tag}_merge")
    osb = sb_fwd(proj, name=f"{tag}_sb")
    y, u1, u2 = gate_fwd(odil, osb, w["w_proj_dil"], w["w_proj_sb"], proj, name=f"{tag}_gate")
    out = matmul([(y, w["w_out"])], "nn", tm=512, tn=1024, tk=1024, out_dtype=F32, res=x, name=f"{tag}_out")
    return out, (x, h, proj, qks, vs, odil, lse, osb, u1, u2, y)


def mixer_bwd_weights(dxb, saved, w, tables, tag):
    x, h, proj, qks, vs, odil, lse, osb, u1, u2, y = saved
    t = x.shape[0]
    g_out = matmul([(y, dxb)], "tn", tm=1024, tn=1024, tk=2048, out_dtype=BF16, name=f"{tag}_gout")
    du1, du2, dgate = gate_bwd(dxb, w["w_out"], u1, u2, proj, name=f"{tag}_dgate")
    g_pd = matmul([(du1, odil)], "tn", tm=1024, tn=256, tk=2048, out_dtype=BF16, name=f"{tag}_gpd")
    g_ps = matmul([(du2, osb)], "tn", tm=1024, tn=256, tk=2048, out_dtype=BF16, name=f"{tag}_gps")
    dodil = matmul([(du1, w["w_proj_dil"])], "nn", tm=512, tn=256, tk=1024, out_dtype=F32, name=f"{tag}_dodil")
    dosb = matmul([(du2, w["w_proj_sb"])], "nn", tm=512, tn=256, tk=1024, out_dtype=F32, name=f"{tag}_dosb")
    dsum, do_wide, lse_wide, dsum_wide = dil_bwd_prep(dodil, odil, lse, name=f"{tag}_dprep")
    dos = [dodil[None]] + list(do_wide)
    lss = [lse[None]] + list(lse_wide)
    dss = [dsum[None]] + list(dsum_wide)
    dqs, dks, dvs = [], [], []
    for g in range(N_DIL_GROUPS):
        dq, dk, dv = dil_bwd(qks[g], vs[g], dos[g], lss[g], dss[g], name=f"{tag}_ddil{g}")
        dqs.append(dq)
        dks.append(dk)
        dvs.append(dv)
    dqkv = rope_join(dqs, dks, dvs, tables, name=f"{tag}_drope")
    gtot = head_sums(dosb, osb, round_a=True, name=f"{tag}_gsum")
    dq_s, dk_s, dv_s = sb_bwd(proj, dosb, gtot, name=f"{tag}_dsb")
    dproj = jnp.concatenate([dqkv, dq_s.astype(BF16), dk_s.astype(BF16), dv_s.astype(BF16), dgate], axis=1)
    g_in = matmul([(dproj, h)], "tn", tm=1280, tn=1024, tk=2048, out_dtype=BF16, name=f"{tag}_gin")
    return {"w_in": g_in, "w_proj_dil": g_pd, "w_proj_sb": g_ps, "w_out": g_out}, dproj


def mixer_bwd_input(dx, dproj, saved, gain, w, tag):
    x = saved[0]
    dh = matmul([(dproj, w["w_in"])], "nn", tm=256, tn=1024, tk=5120, out_dtype=F32, name=f"{tag}_dh")
    return rms_bwd(dh, x, gain, dx, name=f"{tag}_dnorm")


def kernel(x, norm_ffn1, ffn1_w_gate, ffn1_w_up, ffn1_w_down, norm_mix, w_in, w_proj_dil, w_proj_sb, w_out, norm_ffn2, ffn2_w_gate, ffn2_w_up, ffn2_w_down, norm_final, loss_target, m_norm_ffn1, m_ffn1_w_gate, m_ffn1_w_up, m_ffn1_w_down, m_norm_mix, m_w_in, m_w_proj_dil, m_w_proj_sb, m_w_out, m_norm_ffn2, m_ffn2_w_gate, m_ffn2_w_up, m_ffn2_w_down, m_norm_final, v_norm_ffn1, v_ffn1_w_gate, v_ffn1_w_up, v_ffn1_w_down, v_norm_mix, v_w_in, v_w_proj_dil, v_w_proj_sb, v_w_out, v_norm_ffn2, v_ffn2_w_gate, v_ffn2_w_up, v_ffn2_w_down, v_norm_final):
    args = dict(locals())
    names = [name for name, _, _, _ in PACK_LAYOUT]
    t = x.shape[1]
    xs = x.reshape(t, D_MODEL)
    target = loss_target.reshape(t, D_MODEL)
    tables = rope_tables(t)

    my_index = 4 * lax.axis_index("x") + 2 * lax.axis_index("y") + lax.axis_index("c")
    parts = [(l, p) for l in range(2) for p in SUBBLOCKS]
    w_shards = {n: args[n] for n in names}
    gains = {n: args[n] for n in NORM_ROWS}

    packed_w = {lp: pack_shards(w_shards, lp[0], lp[1]) for lp in parts}
    gathered = {parts[0]: all_gather_rows(packed_w[parts[0]], name="gather_l0_ffn1")}
    in_flight, order_token = {}, jnp.zeros((1, 1), F32)
    for l, p in parts[1:]:
        in_flight[(l, p)] = spread_start(packed_w[(l, p)], landing_zone(packed_w[(l, p)], my_index), per_peer=False,
                                         name=f"gather_start_l{l}_{p}")
        order_token = order_token + in_flight[(l, p)][4][0:1, 0:1]

    def weights_of(l, p, after):
        if (l, p) not in gathered:
            gathered[(l, p)] = spread_wait(in_flight[(l, p)], after, per_peer=False, name=f"gather_wait_l{l}_{p}")
        return unpack_gathered(gathered[(l, p)], p)

    saved, weights = {}, {}
    act = xs
    for l in range(2):
        for p in SUBBLOCKS:
            weights[(l, p)] = weights_of(l, p, act)
            gain = _row(gains["norm_" + p][l])
            if (l, p) == parts[0]:
                gain = gain + order_token
            if p == "mix":
                act, saved[(l, p)] = mixer_fwd(act, gain, weights[(l, p)], tables, f"l{l}_mix")
            else:
                act, saved[(l, p)] = ffn_fwd(act, gain, weights[(l, p)], p, f"l{l}_{p}")
    dx, dxb, g_final, loss_part = final_loss(act, _row(norm_final), target, name="loss_head")

    gain_grads, sent = {}, {}
    order_token = jnp.zeros((1, 1), F32)
    for l, p in reversed(parts):
        w, sv = weights[(l, p)], saved[(l, p)]
        if p == "mix":
            gw, rest = mixer_bwd_weights(dxb, sv, w, tables, f"l{l}_mix")
        else:
            gw, rest = ffn_bwd_weights(dxb, sv, w, p, f"l{l}_{p}")
        slices = pack_full_grads(gw, p)
        own = lax.dynamic_index_in_dim(slices, my_index, 0, keepdims=False)
        sent[(l, p)] = spread_start(slices, landing_zone(own, my_index), per_peer=True, name=f"reduce_start_l{l}_{p}")
        gain = _row(gains["norm_" + p][l]) + sent[(l, p)][4][0:1, 0:1]
        if p == "mix":
            dx, dxb, gain_grads[("norm_mix", l)] = mixer_bwd_input(dx, rest, sv, gain, w, f"l{l}_mix")
        else:
            dx, dxb, gain_grads[("norm_" + p, l)] = ffn_bwd_input(dx, rest, sv, gain, w, p, f"l{l}_{p}")

    loss_row = jnp.pad(loss_part[:, :1], ((0, 0), (0, LANES - 1)))
    small = jnp.concatenate([gain_grads[(n, l)] for n in NORM_ROWS for l in range(2)] + [g_final, loss_row], axis=0)
    small_g = sum_partials(all_gather_rows(small, name="gather_gain_grads"), tr=8, name="sum_gain_grads")
    zero_row = jnp.zeros((1, LANES), F32)
    small_of = lambda pre: jnp.concatenate([args[pre + n] for n in NORM_ROWS] + [_row(args[pre + "norm_final"]), zero_row], axis=0)[None]
    small_out = adamw(small_g[None], small_of(""), small_of("m_"), small_of("v_"), tr=8, name="update_gains")
    small_all = [small_g] + [o[0] for o in small_out]

    grad_of = {}
    for l, p in reversed(parts):
        partials = spread_wait(sent[(l, p)], dx, per_peer=True, name=f"reduce_wait_l{l}_{p}")
        summed = sum_partials(partials, tr=UPDATE_ROWS[p], name=f"sum_l{l}_{p}")
        for n, g in unpack_summed(summed, p, w_shards).items():
            grad_of.setdefault(n, [None, None])[l] = g
    big_all = [{}, {}, {}, {}]
    for p in ("ffn2", "mix", "ffn1"):
        for n, _, _, _ in SUBBLOCKS[p]:
            g = jnp.stack(grad_of[n], axis=0)
            outs = adamw(g, args[n], args["m_" + n], args["v_" + n], tr=512, name=f"update_{n}")
            for kind, arr in enumerate([g] + list(outs)):
                big_all[kind][n] = arr

    def gains_of(s):
        out = {n: s[2 * i:2 * i + 2] for i, n in enumerate(NORM_ROWS)}
        out["norm_final"] = s[6]
        return out

    order = ["norm_ffn1", "ffn1_w_gate", "ffn1_w_up", "ffn1_w_down", "norm_mix", "w_in", "w_proj_dil", "w_proj_sb", "w_out",
             "norm_ffn2", "ffn2_w_gate", "ffn2_w_up", "ffn2_w_down", "norm_final"]
    results = []
    for kind in range(4):
        both = {**big_all[kind], **gains_of(small_all[kind])}
        results += [both[n] for n in order]
    loss = small_g[7, 0]
    return (loss, dx.reshape(1, t, D_MODEL), *results)
```

```python
import functools

import jax
import jax.numpy as jnp
from jax import lax
from jax.experimental import pallas as pl
from jax.experimental.pallas import tpu as pltpu

F32 = jnp.float32
BF16 = jnp.bfloat16

D_MODEL = 1024
HEAD_DIM = 64
GROUP_W = 256
D_IN = 5120
N_DIL_GROUPS = 3
DIL_SPAN = 128
DILATIONS = (1, 4, 16)
ROPE_THETA = 500000.0
ROPE_DIM = 16
RMS_EPS = 1e-6
ATT_SCALE = HEAD_DIM ** -0.5
QS_BLK, KS_BLK, VS_BLK = 9, 10, 11
GATE_DIL_BLK, GATE_SB_BLK = 3, 4

ADAM_LR, ADAM_B1, ADAM_B2, ADAM_EPS, ADAM_WD, ADAM_STEP = 0.001, 0.9, 0.999, 1e-08, 0.01, 10

N_DEV = 8
LANES = 1024
VMEM_PHYSICAL_V7X = 64 << 20
VMEM_TEMP_HEADROOM = 20 << 20

PACK_LAYOUT = (
    ("ffn1_w_gate", 352, True, (1024, 2816)),
    ("ffn1_w_up", 352, True, (1024, 2816)),
    ("ffn1_w_down", 352, False, (2816, 1024)),
    ("w_in", 640, True, (1024, 5120)),
    ("w_proj_dil", 32, True, (256, 1024)),
    ("w_proj_sb", 32, True, (256, 1024)),
    ("w_out", 128, False, (1024, 1024)),
    ("ffn2_w_gate", 352, True, (1024, 2816)),
    ("ffn2_w_up", 352, True, (1024, 2816)),
    ("ffn2_w_down", 352, False, (2816, 1024)),
)
SUBBLOCKS = {"ffn1": PACK_LAYOUT[0:3], "mix": PACK_LAYOUT[3:7], "ffn2": PACK_LAYOUT[7:10]}
UPDATE_ROWS = {"ffn1": 352, "mix": 416, "ffn2": 352}
NORM_ROWS = ("norm_ffn1", "norm_mix", "norm_ffn2")


def _nbytes(shape, dtype):
    n = 1
    for s in shape:
        n *= s
    return n * jnp.dtype(dtype).itemsize


def _pcall(body, *, name, grid, in_specs, out_specs, out_shape, blocks, scratch_shapes=(), scratch_bytes=0):
    need = 2 * sum(_nbytes(s, d) for s, d in blocks) + scratch_bytes + VMEM_TEMP_HEADROOM
    limit = min(need, VMEM_PHYSICAL_V7X - (4 << 20))
    in_hbm = lambda s: pltpu.HBM(s.shape, s.dtype)
    out_shape = [in_hbm(s) for s in out_shape] if isinstance(out_shape, (list, tuple)) else in_hbm(out_shape)
    call = pl.pallas_call(
        body, name=name, grid=grid, in_specs=in_specs, out_specs=out_specs, out_shape=out_shape,
        scratch_shapes=scratch_shapes,
        compiler_params=pltpu.CompilerParams(vmem_limit_bytes=limit),
    )
    return lambda *args: call(*[pltpu.with_memory_space_constraint(a, pltpu.HBM) for a in args])


def _dot(a, b, form):
    dn = {"nn": (((1,), (0,)), ((), ())), "nt": (((1,), (1,)), ((), ())), "tn": (((0,), (0,)), ((), ()))}[form]
    return lax.dot_general(a.astype(BF16), b.astype(BF16), dn, preferred_element_type=F32)


def _sigmoid(x):
    return 1.0 / (1.0 + jnp.exp(-x))


def matmul(pairs, form, *, tm, tn, tk, out_dtype, name, scale=1.0, res=None):
    a0, b0 = pairs[0]
    if form == "tn":
        kdim, m = a0.shape
        n = b0.shape[1]
    else:
        m, kdim = a0.shape
        n = b0.shape[1] if form == "nn" else b0.shape[0]
    tm, tn, tk = min(tm, m), min(tn, n), min(tk, kdim)
    assert m % tm == 0 and n % tn == 0 and kdim % tk == 0, (name, m, n, kdim, tm, tn, tk)
    nk = kdim // tk
    npairs = len(pairs)

    if form == "tn":
        a_blk, a_map = (tk, tm), (lambda j, i, k: (k, i))
    else:
        a_blk, a_map = (tm, tk), (lambda j, i, k: (i, k))
    if form == "nt":
        b_blk, b_map = (tn, tk), (lambda j, i, k: (j, k))
    else:
        b_blk, b_map = (tk, tn), (lambda j, i, k: (k, j))
    o_map = lambda j, i, k: (i, j)

    def body(*refs):
        ab = refs[:2 * npairs]
        rest = refs[2 * npairs:]
        if res is not None:
            r_ref, o_ref = rest[0], rest[1]
            rest = rest[2:]
        else:
            r_ref, o_ref = None, rest[0]
            rest = rest[1:]

        def partial_sum():
            p = _dot(ab[0][...], ab[1][...], form)
            for q in range(1, npairs):
                p = p + _dot(ab[2 * q][...], ab[2 * q + 1][...], form)
            return p

        def finish(acc):
            out = acc * scale if scale != 1.0 else acc
            if r_ref is not None:
                out = r_ref[...] + out
            o_ref[...] = out.astype(out_dtype)

        if nk == 1:
            finish(partial_sum())
        else:
            acc_ref = rest[0]
            k = pl.program_id(2)

            @pl.when(k == 0)
            def _():
                acc_ref[...] = partial_sum()

            @pl.when(k > 0)
            def _():
                acc_ref[...] += partial_sum()

            @pl.when(k == nk - 1)
            def _():
                finish(acc_ref[...])

    in_specs, args, blocks = [], [], []
    for a, b in pairs:
        in_specs += [pl.BlockSpec(a_blk, a_map), pl.BlockSpec(b_blk, b_map)]
        args += [a, b]
        blocks += [(a_blk, a.dtype), (b_blk, b.dtype)]
    if res is not None:
        in_specs.append(pl.BlockSpec((tm, tn), o_map))
        args.append(res)
        blocks.append(((tm, tn), res.dtype))
    blocks.append(((tm, tn), out_dtype))
    scratch = [pltpu.VMEM((tm, tn), F32)] if nk > 1 else []
    return _pcall(
        body, name=name, grid=(n // tn, m // tm, nk), in_specs=in_specs,
        out_specs=pl.BlockSpec((tm, tn), o_map), out_shape=jax.ShapeDtypeStruct((m, n), out_dtype),
        blocks=blocks, scratch_shapes=scratch, scratch_bytes=(tm * tn * 4 if nk > 1 else 0),
    )(*args)


def swiglu_fwd(h, wg_t, wu_t, *, name, tm=512, tn=1408):
    t, d = h.shape
    f = wg_t.shape[0]
    tm, tn = min(tm, t), min(tn, f)

    def body(h_ref, wg_ref, wu_ref, a_ref, b_ref, s_ref):
        hh = h_ref[...]
        a = _dot(hh, wg_ref[...], "nt")
        b = _dot(hh, wu_ref[...], "nt")
        a_ref[...] = a.astype(BF16)
        b_ref[...] = b.astype(BF16)
        s_ref[...] = (a * _sigmoid(a) * b).astype(BF16)

    w_spec = pl.BlockSpec((tn, d), lambda j, i: (j, 0))
    o_spec = pl.BlockSpec((tm, tn), lambda j, i: (i, j))
    o_shape = jax.ShapeDtypeStruct((t, f), BF16)
    return _pcall(
        body, name=name, grid=(f // tn, t // tm),
        in_specs=[pl.BlockSpec((tm, d), lambda j, i: (i, 0)), w_spec, w_spec],
        out_specs=[o_spec, o_spec, o_spec], out_shape=[o_shape, o_shape, o_shape],
        blocks=[((tm, d), BF16), ((tn, d), BF16), ((tn, d), BF16)] + [((tm, tn), BF16)] * 3,
    )(h, wg_t, wu_t)


def swiglu_bwd(dyb, wd, a, b, *, name, scale, tm=512, tn=1408):
    t, d = dyb.shape
    f = wd.shape[0]
    tm, tn = min(tm, t), min(tn, f)

    def body(dy_ref, wd_ref, a_ref, b_ref, da_ref, db_ref):
        ds = _dot(dy_ref[...], wd_ref[...], "nt") * scale
        av = a_ref[...].astype(F32)
        bv = b_ref[...].astype(F32)
        sg = _sigmoid(av)
        da_ref[...] = (ds * bv * (sg * (1.0 + av * (1.0 - sg)))).astype(BF16)
        db_ref[...] = (ds * (av * sg)).astype(BF16)

    o_spec = pl.BlockSpec((tm, tn), lambda j, i: (i, j))
    o_shape = jax.ShapeDtypeStruct((t, f), BF16)
    return _pcall(
        body, name=name, grid=(f // tn, t // tm),
        in_specs=[pl.BlockSpec((tm, d), lambda j, i: (i, 0)), pl.BlockSpec((tn, d), lambda j, i: (j, 0)), o_spec, o_spec],
        out_specs=[o_spec, o_spec], out_shape=[o_shape, o_shape],
        blocks=[((tm, d), BF16), ((tn, d), BF16)] + [((tm, tn), BF16)] * 4,
    )(dyb, wd, a, b)


def gate_fwd(odil, osb, wpd_t, wps_t, proj, *, name, tm=512):
    t = odil.shape[0]
    tm = min(tm, t)

    def body(od_ref, os_ref, wpd_ref, wps_ref, g1_ref, g2_ref, y_ref, u1_ref, u2_ref):
        u1 = _dot(od_ref[...], wpd_ref[...], "nt")
        u2 = _dot(os_ref[...], wps_ref[...], "nt")
        y = _sigmoid(g1_ref[...].astype(F32)) * u1 + _sigmoid(g2_ref[...].astype(F32)) * u2
        y_ref[...] = y.astype(BF16)
        u1_ref[...] = u1.astype(BF16)
        u2_ref[...] = u2.astype(BF16)

    o_spec = pl.BlockSpec((tm, D_MODEL), lambda i: (i, 0))
    w_spec = pl.BlockSpec((D_MODEL, GROUP_W), lambda i: (0, 0))
    a_spec = pl.BlockSpec((tm, GROUP_W), lambda i: (i, 0))
    o_shape = jax.ShapeDtypeStruct((t, D_MODEL), BF16)
    return _pcall(
        body, name=name, grid=(t // tm,),
        in_specs=[a_spec, a_spec, w_spec, w_spec,
                  pl.BlockSpec((tm, D_MODEL), lambda i: (i, GATE_DIL_BLK)),
                  pl.BlockSpec((tm, D_MODEL), lambda i: (i, GATE_SB_BLK))],
        out_specs=[o_spec, o_spec, o_spec], out_shape=[o_shape, o_shape, o_shape],
        blocks=[((tm, GROUP_W), F32)] * 2 + [((D_MODEL, GROUP_W), BF16)] * 2 + [((tm, D_MODEL), BF16)] * 5,
    )(odil, osb, wpd_t, wps_t, proj, proj)


def gate_bwd(dxb, wout, u1, u2, proj, *, name, tm=512):
    t = dxb.shape[0]
    tm = min(tm, t)

    def body(dx_ref, w_ref, u1_ref, u2_ref, g1_ref, g2_ref, du1_ref, du2_ref, dg_ref):
        dy = _dot(dx_ref[...], w_ref[...], "nt")
        s1 = _sigmoid(g1_ref[...].astype(F32))
        s2 = _sigmoid(g2_ref[...].astype(F32))
        du1_ref[...] = (dy * s1).astype(BF16)
        du2_ref[...] = (dy * s2).astype(BF16)
        dg_ref[:, :D_MODEL] = (dy * u1_ref[...].astype(F32) * s1 * (1.0 - s1)).astype(BF16)
        dg_ref[:, D_MODEL:] = (dy * u2_ref[...].astype(F32) * s2 * (1.0 - s2)).astype(BF16)

    o_spec = pl.BlockSpec((tm, D_MODEL), lambda i: (i, 0))
    o_shape = jax.ShapeDtypeStruct((t, D_MODEL), BF16)
    return _pcall(
        body, name=name, grid=(t // tm,),
        in_specs=[o_spec, pl.BlockSpec((D_MODEL, D_MODEL), lambda i: (0, 0)), o_spec, o_spec,
                  pl.BlockSpec((tm, D_MODEL), lambda i: (i, GATE_DIL_BLK)),
                  pl.BlockSpec((tm, D_MODEL), lambda i: (i, GATE_SB_BLK))],
        out_specs=[o_spec, o_spec, pl.BlockSpec((tm, 2 * D_MODEL), lambda i: (i, 0))],
        out_shape=[o_shape, o_shape, jax.ShapeDtypeStruct((t, 2 * D_MODEL), BF16)],
        blocks=[((tm, D_MODEL), BF16)] * 9 + [((D_MODEL, D_MODEL), BF16)],
    )(dxb, wout, u1, u2, proj, proj)


def rms_fwd(x, gain, *, name, tm=512):
    t, d = x.shape
    tm = min(tm, t)

    def body(x_ref, g_ref, h_ref):
        xv = x_ref[...]
        rstd = lax.rsqrt(jnp.mean(xv * xv, axis=1, keepdims=True) + RMS_EPS)
        h_ref[...] = (xv * rstd * g_ref[...]).astype(BF16)

    return _pcall(
        body, name=name, grid=(t // tm,),
        in_specs=[pl.BlockSpec((tm, d), lambda i: (i, 0)), pl.BlockSpec((1, d), lambda i: (0, 0))],
        out_specs=pl.BlockSpec((tm, d), lambda i: (i, 0)), out_shape=jax.ShapeDtypeStruct((t, d), BF16),
        blocks=[((tm, d), F32), ((tm, d), BF16)],
    )(x, gain)


def rms_bwd(dh, x, gain, dres, *, name, tm=512):
    t, d = x.shape
    tm = min(tm, t)

    def body(dh_ref, x_ref, g_ref, dr_ref, dx_ref, dxb_ref, dg_ref):
        xv = x_ref[...]
        dhv = dh_ref[...]
        rstd = lax.rsqrt(jnp.mean(xv * xv, axis=1, keepdims=True) + RMS_EPS)
        xh = xv * rstd
        dxh = dhv * g_ref[...]
        dx = dr_ref[...] + rstd * (dxh - xh * jnp.mean(dxh * xh, axis=1, keepdims=True))
        dx_ref[...] = dx
        dxb_ref[...] = dx.astype(BF16)
        part = jnp.sum(dhv * xh, axis=0, keepdims=True)

        @pl.when(pl.program_id(0) == 0)
        def _():
            dg_ref[...] = part

        @pl.when(pl.program_id(0) > 0)
        def _():
            dg_ref[...] += part

    row = pl.BlockSpec((tm, d), lambda i: (i, 0))
    vec = pl.BlockSpec((1, d), lambda i: (0, 0))
    return _pcall(
        body, name=name, grid=(t // tm,), in_specs=[row, row, vec, row], out_specs=[row, row, vec],
        out_shape=[jax.ShapeDtypeStruct((t, d), F32), jax.ShapeDtypeStruct((t, d), BF16), jax.ShapeDtypeStruct((1, d), F32)],
        blocks=[((tm, d), F32)] * 4 + [((tm, d), BF16)],
    )(dh, x, gain, dres)


def final_loss(x, gain, target, *, name, tm=512):
    t, d = x.shape
    tm = min(tm, t)

    def body(x_ref, g_ref, t_ref, dx_ref, dxb_ref, dg_ref, loss_ref):
        xv = x_ref[...]
        g = g_ref[...]
        rstd = lax.rsqrt(jnp.mean(xv * xv, axis=1, keepdims=True) + RMS_EPS)
        xh = xv * rstd
        err = xh * g - t_ref[...]
        dy = err * (1.0 / d)
        dxh = dy * g
        dx = rstd * (dxh - xh * jnp.mean(dxh * xh, axis=1, keepdims=True))
        dx_ref[...] = dx
        dxb_ref[...] = dx.astype(BF16)
        part = jnp.sum(dy * xh, axis=0, keepdims=True)
        sq = jnp.sum(jnp.sum(err * err, axis=1, keepdims=True), axis=0, keepdims=True) * (0.5 / d)
        lpart = jnp.broadcast_to(sq, (1, 128))

        @pl.when(pl.program_id(0) == 0)
        def _():
            dg_ref[...] = part
            loss_ref[...] = lpart

        @pl.when(pl.program_id(0) > 0)
        def _():
            dg_ref[...] += part
            loss_ref[...] += lpart

    row = pl.BlockSpec((tm, d), lambda i: (i, 0))
    vec = pl.BlockSpec((1, d), lambda i: (0, 0))
    return _pcall(
        body, name=name, grid=(t // tm,), in_specs=[row, vec, row],
        out_specs=[row, row, vec, pl.BlockSpec((1, 128), lambda i: (0, 0))],
        out_shape=[jax.ShapeDtypeStruct((t, d), F32), jax.ShapeDtypeStruct((t, d), BF16),
                   jax.ShapeDtypeStruct((1, d), F32), jax.ShapeDtypeStruct((1, 128), F32)],
        blocks=[((tm, d), F32)] * 3 + [((tm, d), BF16)],
    )(x, gain, target)


def rope_tables(t):
    pos = jnp.arange(t, dtype=F32)
    inv_freq = ROPE_THETA ** (-jnp.arange(0, ROPE_DIM, 2, dtype=F32) / ROPE_DIM)
    ang = pos[:, None] * inv_freq[None, :]
    cos, sin = jnp.cos(ang), jnp.sin(ang)
    half = ROPE_DIM // 2
    pad = HEAD_DIM - ROPE_DIM
    one_head = lambda lo, hi, fill: jnp.concatenate([lo, hi, jnp.full((t, pad), fill, F32)], axis=1)
    zeros = jnp.zeros((t, half), F32)
    c = one_head(cos, cos, 1.0)
    sa = one_head(-sin, zeros, 0.0)
    sb = one_head(zeros, sin, 0.0)
    two = lambda a: jnp.concatenate([a, a], axis=1)
    return two(c), two(sa), two(sb)


def _rotate(xv, cv, sav, sbv):
    halves = []
    for half in range(2):
        x = xv[:, 128 * half:128 * (half + 1)]
        halves.append(x * cv + pltpu.roll(x, 120, 1) * sav + pltpu.roll(x, 8, 1) * sbv)
    return jnp.concatenate(halves, axis=1)


STAGE_CHUNKS = 4


def _stage(tm):
    return dict(scratch_shapes=[pltpu.VMEM((STAGE_CHUNKS, tm, 128), F32)], scratch_bytes=STAGE_CHUNKS * tm * 128 * 4)


def _split_residues(stage_ref, val, out_ref, d, col, dtype):
    rows, width = val.shape
    if d == 1:
        out_ref[0, :, col:col + width] = val.astype(dtype)
        return
    chunks = width // 128
    for c in range(chunks):
        stage_ref[c] = val[:, 128 * c:128 * (c + 1)]
    for r in range(d):
        for c in range(chunks):
            out_ref[r, :, col + 128 * c:col + 128 * (c + 1)] = stage_ref[c, pl.ds(r, rows // d, stride=d), :].astype(dtype)


def _join_residues(stage_ref, in_ref, d, col=0, width=GROUP_W):
    if d == 1:
        return in_ref[0, :, col:col + width].astype(F32)
    rows = in_ref.shape[1] * d
    chunks = width // 128
    for r in range(d):
        for c in range(chunks):
            stage_ref[c, pl.ds(r, rows // d, stride=d), :] = in_ref[r, :, col + 128 * c:col + 128 * (c + 1)].astype(F32)
    return jnp.concatenate([stage_ref[c] for c in range(chunks)], axis=1)


def rope_split(proj, tables, *, name, tm=512):
    c, sa, sb = tables
    t = c.shape[0]
    tm = min(tm, t)

    def body(*refs):
        pieces = refs[0:9]
        c_ref, sa_ref, sb_ref = refs[9:12]
        qk_out, v_out = refs[12:15], refs[15:18]
        stage = refs[18]
        cv, sav, sbv = c_ref[...], sa_ref[...], sb_ref[...]
        for g, d in enumerate(DILATIONS):
            for kind in range(3):
                xv = pieces[3 * kind + g][...].astype(F32)
                if kind < 2:
                    _split_residues(stage, _rotate(xv, cv, sav, sbv), qk_out[g], d, GROUP_W * kind, BF16)
                else:
                    _split_residues(stage, xv, v_out[g], d, 0, BF16)

    tab = pl.BlockSpec((tm, 128), lambda i: (i, 0))
    in_specs = [pl.BlockSpec((tm, GROUP_W), functools.partial(lambda i, cb: (i, cb), cb=cb)) for cb in range(9)]
    out_specs = ([pl.BlockSpec((d, tm // d, 2 * GROUP_W), lambda i: (0, i, 0)) for d in DILATIONS]
                 + [pl.BlockSpec((d, tm // d, GROUP_W), lambda i: (0, i, 0)) for d in DILATIONS])
    out_shape = ([jax.ShapeDtypeStruct((d, t // d, 2 * GROUP_W), BF16) for d in DILATIONS]
                 + [jax.ShapeDtypeStruct((d, t // d, GROUP_W), BF16) for d in DILATIONS])
    outs = _pcall(
        body, name=name, grid=(t // tm,), in_specs=in_specs + [tab, tab, tab], out_specs=out_specs, out_shape=out_shape,
        blocks=[((tm, GROUP_W), BF16)] * 18 + [((tm, 128), F32)] * 3,
        **_stage(tm),
    )(*([proj] * 9), c, sa, sb)
    return outs[0:3], outs[3:6]


def rope_join(dqs, dks, dvs, tables, *, name, tm=512):
    c, sa, sb = tables
    t = c.shape[0]
    tm = min(tm, t)

    def body(*refs):
        pieces = refs[0:9]
        c_ref, sa_ref, sb_ref = refs[9:12]
        o_ref, stage = refs[12], refs[13]
        cv, sav, sbv = c_ref[...], -sa_ref[...], -sb_ref[...]
        for kind in range(3):
            for g, d in enumerate(DILATIONS):
                xv = _join_residues(stage, pieces[3 * kind + g], d)
                if kind < 2:
                    xv = _rotate(xv, cv, sav, sbv)
                col = GROUP_W * (3 * kind + g)
                o_ref[:, col:col + GROUP_W] = xv.astype(BF16)

    tab = pl.BlockSpec((tm, 128), lambda i: (i, 0))
    in_specs = [pl.BlockSpec((d, tm // d, GROUP_W), lambda i: (0, i, 0)) for _ in range(3) for d in DILATIONS]
    return _pcall(
        body, name=name, grid=(t // tm,), in_specs=in_specs + [tab, tab, tab],
        out_specs=pl.BlockSpec((tm, 9 * GROUP_W), lambda i: (i, 0)), out_shape=jax.ShapeDtypeStruct((t, 9 * GROUP_W), BF16),
        blocks=[((tm, GROUP_W), F32)] * 9 + [((tm, 128), F32)] * 3 + [((tm, 9 * GROUP_W), BF16)],
        **_stage(tm),
    )(*dqs, *dks, *dvs, c, sa, sb)


def _head_mask(h):
    lane = lax.broadcasted_iota(jnp.int32, (1, GROUP_W), 1)
    return (lane // HEAD_DIM) == h


def _band_masks():
    ri = lax.broadcasted_iota(jnp.int32, (DIL_SPAN, DIL_SPAN), 0)
    ci = lax.broadcasted_iota(jnp.int32, (DIL_SPAN, DIL_SPAN), 1)
    return ci <= ri, ci >= ri


def dil_fwd(qk, v, *, name):
    d, nsub, _ = qk.shape
    nblk = nsub // DIL_SPAN

    def body(q_ref, kc_ref, kp_ref, vc_ref, vp_ref, o_ref, lse_ref):
        nb = pl.program_id(1)
        own, prev = _band_masks()
        prev = prev & (nb > 0)
        q, kc, kp, vc, vp = q_ref[0], kc_ref[0], kp_ref[0], vc_ref[0], vp_ref[0]
        o_acc = jnp.zeros((DIL_SPAN, GROUP_W), F32)
        for h in range(4):
            hm = _head_mask(h)
            qh = jnp.where(hm, q, jnp.zeros_like(q))
            sc = jnp.where(own, _dot(qh, kc, "nt") * ATT_SCALE, -jnp.inf)
            sp = jnp.where(prev, _dot(qh, kp, "nt") * ATT_SCALE, -jnp.inf)
            m = jnp.maximum(jnp.max(sc, axis=1, keepdims=True), jnp.max(sp, axis=1, keepdims=True))
            pc = jnp.exp(sc - m)
            pp = jnp.exp(sp - m)
            den = jnp.sum(pc, axis=1, keepdims=True) + jnp.sum(pp, axis=1, keepdims=True)
            oh = (_dot(pc, vc, "nn") + _dot(pp, vp, "nn")) / den
            o_acc = jnp.where(hm, oh, o_acc)
            lse_ref[0, :, 128 * h:128 * (h + 1)] = jnp.broadcast_to(m + jnp.log(den), (DIL_SPAN, 128))
        o_ref[0] = o_acc

    blk = (1, DIL_SPAN, GROUP_W)
    sblk = (1, DIL_SPAN, 512)
    prv = lambda nb: jnp.maximum(nb - 1, 0)
    return _pcall(
        body, name=name, grid=(d, nblk),
        in_specs=[pl.BlockSpec(blk, lambda r, nb: (r, nb, 0)),
                  pl.BlockSpec(blk, lambda r, nb: (r, nb, 1)),
                  pl.BlockSpec(blk, lambda r, nb: (r, prv(nb), 1)),
                  pl.BlockSpec(blk, lambda r, nb: (r, nb, 0)),
                  pl.BlockSpec(blk, lambda r, nb: (r, prv(nb), 0))],
        out_specs=[pl.BlockSpec(blk, lambda r, nb: (r, nb, 0)), pl.BlockSpec(sblk, lambda r, nb: (r, nb, 0))],
        out_shape=[jax.ShapeDtypeStruct((d, nsub, GROUP_W), F32), jax.ShapeDtypeStruct((d, nsub, 512), F32)],
        blocks=[(blk, BF16)] * 5 + [(blk, F32), (sblk, F32)],
    )(qk, qk, qk, v, v)


def dil_merge(outs, lses, *, name, tm=512):
    t = outs[0].shape[0] * outs[0].shape[1]
    tm = min(tm, t)

    def body(o0, o1, o2, l0, l1, l2, o_ref, lse_ref, stage):
        ls = [_join_residues(stage, l, d, 0, 512) for l, d in zip((l0, l1, l2), DILATIONS)]
        m = jnp.maximum(jnp.maximum(ls[0], ls[1]), ls[2])
        tot = m + jnp.log(jnp.exp(ls[0] - m) + jnp.exp(ls[1] - m) + jnp.exp(ls[2] - m))
        lse_ref[...] = tot
        lane = lax.broadcasted_iota(jnp.int32, (1, 128), 1)
        first = lane < HEAD_DIM
        acc = jnp.zeros((tm, GROUP_W), F32)
        for og, lg, d in zip((o0, o1, o2), ls, DILATIONS):
            w = jnp.exp(lg - tot)
            wide = jnp.concatenate([jnp.where(first, w[:, 0:128], w[:, 128:256]),
                                    jnp.where(first, w[:, 256:384], w[:, 384:512])], axis=1)
            acc = acc + wide * _join_residues(stage, og, d)
        o_ref[...] = acc

    o_in = [pl.BlockSpec((d, tm // d, GROUP_W), lambda i: (0, i, 0)) for d in DILATIONS]
    l_in = [pl.BlockSpec((d, tm // d, 512), lambda i: (0, i, 0)) for d in DILATIONS]
    return _pcall(
        body, name=name, grid=(t // tm,), in_specs=o_in + l_in,
        out_specs=[pl.BlockSpec((tm, GROUP_W), lambda i: (i, 0)), pl.BlockSpec((tm, 512), lambda i: (i, 0))],
        out_shape=[jax.ShapeDtypeStruct((t, GROUP_W), F32), jax.ShapeDtypeStruct((t, 512), F32)],
        blocks=[((tm, GROUP_W), F32)] * 4 + [((tm, 512), F32)] * 4,
        **_stage(tm),
    )(*outs, *lses)


def dil_bwd_prep(do, o, lse, *, name, tm=512):
    t = do.shape[0]
    tm = min(tm, t)
    wide = DILATIONS[1:]

    def body(do_ref, o_ref, lse_ref, ds_ref, *rest):
        do_out, lse_out, ds_out = rest[0:2], rest[2:4], rest[4:6]
        stage = rest[6]
        dov = do_ref[...]
        prod = dov * o_ref[...]
        for h in range(4):
            s = jnp.sum(jnp.where(_head_mask(h), prod, 0.0), axis=1, keepdims=True)
            ds_ref[:, 128 * h:128 * (h + 1)] = jnp.broadcast_to(s, (tm, 128))
        for i, d in enumerate(wide):
            _split_residues(stage, dov, do_out[i], d, 0, BF16)
            _split_residues(stage, lse_ref[...], lse_out[i], d, 0, F32)
            _split_residues(stage, ds_ref[...], ds_out[i], d, 0, F32)

    nat = lambda w: pl.BlockSpec((tm, w), lambda i: (i, 0))
    res = lambda d, w: pl.BlockSpec((d, tm // d, w), lambda i: (0, i, 0))
    shape = lambda d, w, dt: jax.ShapeDtypeStruct((d, t // d, w), dt)
    outs = _pcall(
        body, name=name, grid=(t // tm,), in_specs=[nat(GROUP_W), nat(GROUP_W), nat(512)],
        out_specs=[nat(512)] + [res(d, GROUP_W) for d in wide] + [res(d, 512) for d in wide] * 2,
        out_shape=([jax.ShapeDtypeStruct((t, 512), F32)] + [shape(d, GROUP_W, BF16) for d in wide]
                   + [shape(d, 512, F32) for d in wide] * 2),
        blocks=[((tm, GROUP_W), F32)] * 3 + [((tm, 512), F32)] * 6,
        **_stage(tm),
    )(do, o, lse)
    return outs[0], outs[1:3], outs[3:5], outs[5:7]


def head_sums(a, b, *, name, round_a=False, tm=512):
    t = a.shape[0]
    tm = min(tm, t)

    def body(a_ref, b_ref, o_ref):
        av = a_ref[...]
        if round_a:
            av = av.astype(BF16).astype(F32)
        prod = av * b_ref[...]
        for h in range(4):
            s = jnp.sum(jnp.where(_head_mask(h), prod, 0.0), axis=1, keepdims=True)
            o_ref[:, 128 * h:128 * (h + 1)] = jnp.broadcast_to(s, (tm, 128))

    spec = pl.BlockSpec((tm, GROUP_W), lambda i: (i, 0))
    return _pcall(
        body, name=name, grid=(t // tm,), in_specs=[spec, spec],
        out_specs=pl.BlockSpec((tm, 512), lambda i: (i, 0)), out_shape=jax.ShapeDtypeStruct((t, 512), F32),
        blocks=[((tm, GROUP_W), F32)] * 2 + [((tm, 512), F32)],
    )(a, b)


def dil_bwd(qk, v, do, lse, dsum, *, name):
    d, nsub, _ = qk.shape
    nblk = nsub // DIL_SPAN

    def body(qa_ref, qb_ref, kc_ref, kp_ref, vc_ref, vp_ref, doa_ref, dob_ref, la_ref, lb_ref, sa_ref, sb_ref,
             dq_ref, dk_ref, dv_ref):
        nb = pl.program_id(1)
        own, band = _band_masks()
        prev = band & (nb > 0)
        nxt = band & (nb < nblk - 1)
        qa, qb, kc, kp, vc, vp = qa_ref[0], qb_ref[0], kc_ref[0], kp_ref[0], vc_ref[0], vp_ref[0]
        doa, dob = doa_ref[0].astype(BF16), dob_ref[0].astype(BF16)
        dq = jnp.zeros((DIL_SPAN, GROUP_W), F32)
        dk = jnp.zeros((DIL_SPAN, GROUP_W), F32)
        dv = jnp.zeros((DIL_SPAN, GROUP_W), F32)
        for h in range(4):
            hm = _head_mask(h)
            sl = slice(128 * h, 128 * (h + 1))
            qah = jnp.where(hm, qa, jnp.zeros_like(qa))
            qbh = jnp.where(hm, qb, jnp.zeros_like(qb))
            dah = jnp.where(hm, doa, jnp.zeros_like(doa))
            dbh = jnp.where(hm, dob, jnp.zeros_like(dob))
            la, lb, sa, sb = la_ref[0, :, sl], lb_ref[0, :, sl], sa_ref[0, :, sl], sb_ref[0, :, sl]

            def probs(qh, dh, k, v, mask, l, s):
                p = jnp.where(mask, jnp.exp(_dot(qh, k, "nt") * ATT_SCALE - l), 0.0)
                dsc = p * (_dot(dh, v, "nt") - s) * ATT_SCALE
                return p.astype(BF16), dsc.astype(BF16)

            p_cc, ds_cc = probs(qah, dah, kc, vc, own, la, sa)
            _, ds_cp = probs(qah, dah, kp, vp, prev, la, sa)
            p_nc, ds_nc = probs(qbh, dbh, kc, vc, nxt, lb, sb)
            dq = jnp.where(hm, _dot(ds_cc, kc, "nn") + _dot(ds_cp, kp, "nn"), dq)
            dk = dk + _dot(ds_cc, qah, "tn") + _dot(ds_nc, qbh, "tn")
            dv = dv + _dot(p_cc, dah, "tn") + _dot(p_nc, dbh, "tn")
        dq_ref[0] = dq
        dk_ref[0] = dk
        dv_ref[0] = dv

    blk = (1, DIL_SPAN, GROUP_W)
    sblk = (1, DIL_SPAN, 512)
    prv = lambda nb: jnp.maximum(nb - 1, 0)
    nxt_ = lambda nb: jnp.minimum(nb + 1, nblk - 1)
    cur_at = lambda c: pl.BlockSpec(blk, functools.partial(lambda r, nb, c: (r, nb, c), c=c))
    prv_at = lambda c: pl.BlockSpec(blk, functools.partial(lambda r, nb, c: (r, prv(nb), c), c=c))
    nxt_at = lambda c: pl.BlockSpec(blk, functools.partial(lambda r, nb, c: (r, nxt_(nb), c), c=c))
    s_cur = pl.BlockSpec(sblk, lambda r, nb: (r, nb, 0))
    s_nxt = pl.BlockSpec(sblk, lambda r, nb: (r, nxt_(nb), 0))
    o_spec = pl.BlockSpec(blk, lambda r, nb: (r, nb, 0))
    o_shape = jax.ShapeDtypeStruct((d, nsub, GROUP_W), F32)
    return _pcall(
        body, name=name, grid=(d, nblk),
        in_specs=[cur_at(0), nxt_at(0), cur_at(1), prv_at(1), cur_at(0), prv_at(0), cur_at(0), nxt_at(0),
                  s_cur, s_nxt, s_cur, s_nxt],
        out_specs=[o_spec, o_spec, o_spec], out_shape=[o_shape, o_shape, o_shape],
        blocks=[(blk, BF16)] * 6 + [(blk, F32)] * 5 + [(sblk, F32)] * 4,
    )(qk, qk, qk, qk, v, v, do, do, lse, lse, dsum, dsum)


def _tri_dot(x, b):
    hi = x.astype(BF16)
    lo = (x - hi.astype(F32)).astype(BF16)
    return _dot(hi, b, "nn") + _dot(lo, b, "nn")


SB_TILE = 256


def _stack_heads(a):
    return jnp.concatenate([jnp.where(_head_mask(h), a, jnp.zeros_like(a)) for h in range(4)], axis=0)


def _unstack_heads(acc, rows):
    out = acc[0:rows]
    for h in range(1, 4):
        out = jnp.where(_head_mask(h), acc[h * rows:(h + 1) * rows], out)
    return out


def _tri_masks(n):
    ri = lax.broadcasted_iota(jnp.int32, (n, n), 0)
    ci = lax.broadcasted_iota(jnp.int32, (n, n), 1)
    return (ri > ci).astype(BF16), (ri >= ci).astype(BF16)


def _sb_weights(qs, kt, after, c_keep, diagonal):
    z = _dot(qs, kt, "nt")
    lbeta = jnp.minimum(z, 0.0) - jnp.log(1.0 + jnp.exp(-jnp.abs(z)))
    lkeep = lbeta - z
    past = None
    if diagonal:
        n = SB_TILE
        past = lax.broadcasted_iota(jnp.int32, z.shape, 1) < lax.broadcasted_iota(jnp.int32, z.shape, 0) % n
        lkeep = jnp.where(past, lkeep, 0.0)
    w = jnp.exp(lbeta + _tri_dot(lkeep, after) + c_keep)
    if diagonal:
        w = jnp.where(past, w, 0.0)
    return z, past, lbeta, lkeep, w


def sb_fwd(proj, *, name):
    t = proj.shape[0]
    n = SB_TILE
    assert t % n == 0

    def body(q_ref, k_ref, v_ref, o_ref, acc_ref):
        qb = pl.program_id(0)
        qs = _stack_heads(q_ref[...] * ATT_SCALE)
        after, _ = _tri_masks(n)

        def tile(off, diagonal, c_keep):
            kt = k_ref[pl.ds(off, n), :]
            vt = v_ref[pl.ds(off, n), :]
            _, _, _, lkeep, w = _sb_weights(qs, kt, after, c_keep, diagonal)
            pv = _tri_dot(w, vt)
            if diagonal:
                acc_ref[...] = pv
            else:
                acc_ref[...] += pv
            return c_keep + jnp.sum(lkeep, axis=1, keepdims=True)

        c0 = tile(pl.multiple_of(qb * n, n), True, jnp.zeros((4 * n, 1), F32))
        lax.fori_loop(0, qb, lambda it, c: tile(pl.multiple_of((qb - 1 - it) * n, n), False, c), c0)
        o_ref[...] = _unstack_heads(acc_ref[...], n)

    full = lambda cb: pl.BlockSpec((t, GROUP_W), functools.partial(lambda i, cb: (0, cb), cb=cb))
    return _pcall(
        body, name=name, grid=(t // n,),
        in_specs=[pl.BlockSpec((n, GROUP_W), lambda i: (i, QS_BLK)), full(KS_BLK), full(VS_BLK)],
        out_specs=pl.BlockSpec((n, GROUP_W), lambda i: (i, 0)), out_shape=jax.ShapeDtypeStruct((t, GROUP_W), F32),
        blocks=[((n, GROUP_W), BF16), ((t, GROUP_W), BF16), ((t, GROUP_W), BF16), ((n, GROUP_W), F32)],
        scratch_shapes=[pltpu.VMEM((4 * n, GROUP_W), F32)], scratch_bytes=4 * n * GROUP_W * 4,
    )(proj, proj, proj)


def sb_bwd(proj, do, gtot, *, name):
    t = proj.shape[0]
    n = SB_TILE
    assert t % n == 0

    def body(q_ref, k_ref, v_ref, do_ref, gt_ref, dq_ref, dk_ref, dv_ref, acc_ref):
        qb = pl.program_id(0)

        @pl.when(qb == 0)
        def _():
            dk_ref[...] = jnp.zeros_like(dk_ref)
            dv_ref[...] = jnp.zeros_like(dv_ref)

        qs = _stack_heads(q_ref[...] * ATT_SCALE)
        dos = _stack_heads(do_ref[...].astype(BF16))
        gt = jnp.concatenate([jnp.max(gt_ref[:, 128 * h:128 * (h + 1)], axis=1, keepdims=True) for h in range(4)], axis=0)
        after, from_on = _tri_masks(n)

        def tile(off, diagonal, carry):
            c_keep, c_g = carry
            kt = k_ref[pl.ds(off, n), :]
            vt = v_ref[pl.ds(off, n), :]
            z, past, lbeta, lkeep, w = _sb_weights(qs, kt, after, c_keep, diagonal)
            gw = w * _dot(dos, vt, "nt")
            big_g = gt - (_tri_dot(gw, from_on) + c_g)
            dz = gw * jnp.exp(lbeta - z) - big_g * jnp.exp(lbeta)
            if diagonal:
                dz = jnp.where(past, dz, 0.0)
            dz = dz.astype(BF16)
            dk_ref[pl.ds(off, n), :] += _dot(dz, qs, "tn")
            dv_ref[pl.ds(off, n), :] += _dot(w, dos, "tn")
            dq = _dot(dz, kt, "nn")
            if diagonal:
                acc_ref[...] = dq
            else:
                acc_ref[...] += dq
            return c_keep + jnp.sum(lkeep, axis=1, keepdims=True), c_g + jnp.sum(gw, axis=1, keepdims=True)

        zero_col = jnp.zeros((4 * n, 1), F32)
        c0 = tile(pl.multiple_of(qb * n, n), True, (zero_col, zero_col))
        lax.fori_loop(0, qb, lambda it, c: tile(pl.multiple_of((qb - 1 - it) * n, n), False, c), c0)
        dq_ref[...] = _unstack_heads(acc_ref[...], n) * ATT_SCALE

    full = lambda cb: pl.BlockSpec((t, GROUP_W), functools.partial(lambda i, cb: (0, cb), cb=cb))
    whole = pl.BlockSpec((t, GROUP_W), lambda i: (0, 0))
    rowblk = pl.BlockSpec((n, GROUP_W), lambda i: (i, 0))
    shape = jax.ShapeDtypeStruct((t, GROUP_W), F32)
    return _pcall(
        body, name=name, grid=(t // n,),
        in_specs=[pl.BlockSpec((n, GROUP_W), lambda i: (i, QS_BLK)), full(KS_BLK), full(VS_BLK), rowblk,
                  pl.BlockSpec((n, 512), lambda i: (i, 0))],
        out_specs=[rowblk, whole, whole], out_shape=[shape, shape, shape],
        blocks=[((n, GROUP_W), BF16), ((t, GROUP_W), BF16), ((t, GROUP_W), BF16), ((n, GROUP_W), F32),
                ((n, 512), F32), ((n, GROUP_W), F32), ((t, GROUP_W), F32), ((t, GROUP_W), F32)],
        scratch_shapes=[pltpu.VMEM((4 * n, GROUP_W), F32)], scratch_bytes=4 * n * GROUP_W * 4,
    )(proj, proj, proj, do, gtot)


def _mesh_place():
    return lax.axis_index("x"), lax.axis_index("y"), lax.axis_index("c")


def _flip(place, mask):
    x, y, c = place
    return ((1 - x) if mask & 4 else x, (1 - y) if mask & 2 else y, (1 - c) if mask & 1 else c)


def _dev_index(place):
    x, y, c = place
    return 4 * x + 2 * y + c


HBM_SPEC = pl.BlockSpec(memory_space=pltpu.HBM)


def all_gather_rows(shard, *, name):
    rows, lanes = shard.shape

    def body(x_ref, out_ref, send_sems, recv_sems, local_sem):
        me = _mesh_place()
        x, y, c = me
        sibling = _flip(me, 1)
        chips = [_flip(me, 4), _flip(me, 2), _flip(me, 6)]

        def copy(k, block, to, src=None):
            dst = out_ref.at[_dev_index(block)]
            return pltpu.make_async_remote_copy(
                src_ref=dst if src is None else src, dst_ref=dst, send_sem=send_sems.at[k], recv_sem=recv_sems.at[k],
                device_id=to, device_id_type=pl.DeviceIdType.MESH)

        mine = pltpu.make_async_copy(x_ref, out_ref.at[_dev_index(me)], local_sem)
        mine.start()
        first = [copy(0, me, sibling, src=x_ref)] + [copy(1 + j, me, chip, src=x_ref) for j, chip in enumerate(chips)]
        for cp in first:
            cp.start()
        passed = [copy(4 + j, chip, sibling) for j, chip in enumerate(chips)]
        for j, chip in enumerate(chips):
            copy(1 + j, chip, me).wait_recv()
            passed[j].start()
        copy(0, sibling, me).wait_recv()
        for j, chip in enumerate(chips):
            copy(4 + j, _flip(chip, 1), me).wait_recv()
        for cp in first + passed:
            cp.wait_send()
        mine.wait()

    return pl.pallas_call(
        body, name=name, in_specs=[HBM_SPEC], out_specs=HBM_SPEC,
        out_shape=jax.ShapeDtypeStruct((N_DEV, rows, lanes), shard.dtype),
        scratch_shapes=[pltpu.SemaphoreType.DMA((7,)), pltpu.SemaphoreType.DMA((7,)), pltpu.SemaphoreType.DMA],
    )(shard)


SEM_SPEC = pl.BlockSpec(memory_space=pltpu.SEMAPHORE)
DATAFLOW_EFFECT = pltpu.SideEffectType.DATAFLOW_SIDE_EFFECTING


def _spread_copies(src_ref, land_ref, send_sems, recv_sems, per_peer, arriving):
    me = _mesh_place()
    out = []
    for mask in range(1, N_DEV):
        peer = _flip(me, mask)
        data_of = _dev_index(me) if arriving else _dev_index(peer)
        slot = _dev_index(peer) if arriving else _dev_index(me)
        out.append(pltpu.make_async_remote_copy(
            src_ref=src_ref.at[data_of] if per_peer else src_ref, dst_ref=land_ref.at[slot],
            send_sem=send_sems.at[mask - 1], recv_sem=recv_sems.at[mask - 1],
            device_id=peer, device_id_type=pl.DeviceIdType.MESH))
    return out


def spread_start(src, land, *, per_peer, name):
    def body(src_ref, land_ref, send_sems, recv_sems, src_thru, land_thru, token):
        for cp in _spread_copies(src_ref, land_ref, send_sems, recv_sems, per_peer, arriving=False):
            cp.start()
        token[...] = jnp.zeros_like(token)

    return pl.pallas_call(
        body, name=name, in_specs=(HBM_SPEC, HBM_SPEC),
        out_shape=(pltpu.SemaphoreType.DMA((N_DEV - 1,)), pltpu.SemaphoreType.DMA((N_DEV - 1,)),
                   pltpu.HBM(src.shape, src.dtype), pltpu.HBM(land.shape, land.dtype), jax.ShapeDtypeStruct((8, 128), F32)),
        out_specs=(SEM_SPEC, SEM_SPEC, HBM_SPEC, HBM_SPEC, pl.BlockSpec(memory_space=pltpu.VMEM)),
        input_output_aliases={0: 2, 1: 3},
        compiler_params=pltpu.CompilerParams(has_side_effects=DATAFLOW_EFFECT),
    )(pltpu.with_memory_space_constraint(src, pltpu.HBM), pltpu.with_memory_space_constraint(land, pltpu.HBM))


def spread_wait(started, after, *, per_peer, name):
    send_sems, recv_sems, src_thru, land_thru, _ = started

    def body(src_ref, land_ref, send_sems, recv_sems, after_ref, src_dead, got_ref):
        for cp in _spread_copies(src_ref, land_ref, send_sems, recv_sems, per_peer, arriving=True):
            cp.wait_send()
            cp.wait_recv()

    return pl.pallas_call(
        body, name=name, in_specs=(HBM_SPEC, HBM_SPEC, SEM_SPEC, SEM_SPEC, pl.BlockSpec(memory_space=pl.ANY)),
        out_shape=(pltpu.HBM(src_thru.shape, src_thru.dtype), pltpu.HBM(land_thru.shape, land_thru.dtype)),
        out_specs=(HBM_SPEC, HBM_SPEC), input_output_aliases={0: 0, 1: 1},
        compiler_params=pltpu.CompilerParams(has_side_effects=DATAFLOW_EFFECT),
    )(src_thru, land_thru, send_sems, recv_sems, after)[1]


def landing_zone(own_block, my_index):
    zone = lax.empty((N_DEV,) + own_block.shape, own_block.dtype)
    return lax.dynamic_update_slice(zone, own_block[None], (my_index,) + (0,) * own_block.ndim)


def sum_partials(parts, *, name, tr):
    _, rows, lanes = parts.shape
    assert rows % tr == 0

    def body(p_ref, g_ref):
        g = p_ref[0].astype(F32)
        for k in range(1, N_DEV):
            g = g + p_ref[k].astype(F32)
        g_ref[...] = g

    return _pcall(
        body, name=name, grid=(rows // tr,),
        in_specs=[pl.BlockSpec((N_DEV, tr, lanes), lambda i: (0, i, 0))],
        out_specs=pl.BlockSpec((tr, lanes), lambda i: (i, 0)), out_shape=jax.ShapeDtypeStruct((rows, lanes), F32),
        blocks=[((N_DEV, tr, lanes), parts.dtype), ((tr, lanes), F32)],
    )(parts)


def adamw(g, w, m, v, *, name, tr):
    nl, k, n = w.shape
    tr = min(tr, k)
    assert k % tr == 0
    bc1 = 1.0 - ADAM_B1 ** ADAM_STEP
    bc2 = 1.0 - ADAM_B2 ** ADAM_STEP

    def body(g_ref, w_ref, m_ref, v_ref, d_ref, mo_ref, vo_ref):
        gv = g_ref[...]
        m_new = ADAM_B1 * m_ref[...] + (1.0 - ADAM_B1) * gv
        v_new = ADAM_B2 * v_ref[...] + (1.0 - ADAM_B2) * (gv * gv)
        mo_ref[...] = m_new
        vo_ref[...] = v_new
        d_ref[...] = -ADAM_LR * ((m_new / bc1) / (jnp.sqrt(v_new / bc2) + ADAM_EPS) + ADAM_WD * w_ref[...])

    spec = pl.BlockSpec((1, tr, n), lambda l, i: (l, i, 0))
    shape = jax.ShapeDtypeStruct(w.shape, F32)
    return _pcall(
        body, name=name, grid=(nl, k // tr), in_specs=[spec] * 4, out_specs=[spec] * 3, out_shape=[shape] * 3,
        blocks=[((1, tr, n), F32)] * 7,
    )(g, w, m, v)


def pack_shards(tensors, layer, part):
    rows = []
    for name, r, by_cols, _ in SUBBLOCKS[part]:
        shard = tensors[name][layer].astype(BF16)
        rows.append((shard.T if by_cols else shard).reshape(r, LANES))
    return jnp.concatenate(rows, axis=0)


def unpack_gathered(gathered, part):
    ws, r0 = {}, 0
    for name, r, by_cols, (k, n) in SUBBLOCKS[part]:
        ws[name] = gathered[:, r0:r0 + r, :].reshape((n, k) if by_cols else (k, n))
        r0 += r
    return ws


def pack_full_grads(grads, part):
    return jnp.concatenate([grads[name].reshape(N_DEV, r, LANES) for name, r, _, _ in SUBBLOCKS[part]], axis=1)


def unpack_summed(summed, part, like):
    out, r0 = {}, 0
    for name, r, by_cols, _ in SUBBLOCKS[part]:
        k, n = like[name].shape[1:]
        seg = summed[r0:r0 + r]
        out[name] = seg.reshape(n, k).T if by_cols else seg.reshape(k, n)
        r0 += r
    return out


def _row(v):
    return v.reshape(1, -1)


def ffn_fwd(x, gain, w, pre, tag):
    h = rms_fwd(x, gain, name=f"{tag}_norm")
    a, b, s = swiglu_fwd(h, w[pre + "_w_gate"], w[pre + "_w_up"], name=f"{tag}_gateup")
    out = matmul([(s, w[pre + "_w_down"])], "nn", tm=512, tn=1024, tk=2816, out_dtype=F32, scale=0.5, res=x, name=f"{tag}_down")
    return out, (x, h, a, b, s)


def ffn_bwd_weights(dxb, saved, w, pre, tag):
    x, h, a, b, s = saved
    da, db = swiglu_bwd(dxb, w[pre + "_w_down"], a, b, scale=0.5, name=f"{tag}_dgateup")
    g_down = matmul([(s, dxb)], "tn", tm=1408, tn=1024, tk=2048, out_dtype=BF16, scale=0.5, name=f"{tag}_gdown")
    g_gate = matmul([(da, h)], "tn", tm=1408, tn=1024, tk=2048, out_dtype=BF16, name=f"{tag}_ggate")
    g_up = matmul([(db, h)], "tn", tm=1408, tn=1024, tk=2048, out_dtype=BF16, name=f"{tag}_gup")
    return {pre + "_w_gate": g_gate, pre + "_w_up": g_up, pre + "_w_down": g_down}, (da, db)


def ffn_bwd_input(dx, rest, saved, gain, w, pre, tag):
    da, db = rest
    x = saved[0]
    dh = matmul([(da, w[pre + "_w_gate"]), (db, w[pre + "_w_up"])], "nn", tm=256, tn=1024, tk=2816, out_dtype=F32, name=f"{tag}_dh")
    return rms_bwd(dh, x, gain, dx, name=f"{tag}_dnorm")


def mixer_fwd(x, gain, w, tables, tag):
    h = rms_fwd(x, gain, name=f"{tag}_norm")
    proj = matmul([(h, w["w_in"])], "nt", tm=512, tn=1280, tk=1024, out_dtype=BF16, name=f"{tag}_in")
    qks, vs = rope_split(proj, tables, name=f"{tag}_rope")
    outs, lses = [], []
    for g in range(N_DIL_GROUPS):
        o, lse = dil_fwd(qks[g], vs[g], name=f"{tag}_dil{g}")
        outs.append(o)
        lses.append(lse)
    odil, lse = dil_merge(outs, lses, name=f"{tag}_merge")
    osb = sb_fwd(proj, name=f"{tag}_sb")
    y, u1, u2 = gate_fwd(odil, osb, w["w_proj_dil"], w["w_proj_sb"], proj, name=f"{tag}_gate")
    out = matmul([(y, w["w_out"])], "nn", tm=512, tn=1024, tk=1024, out_dtype=F32, res=x, name=f"{tag}_out")
    return out, (x, h, proj, qks, vs, odil, lse, osb, u1, u2, y)


def mixer_bwd_weights(dxb, saved, w, tables, tag):
    x, h, proj, qks, vs, odil, lse, osb, u1, u2, y = saved
    t = x.shape[0]
    g_out = matmul([(y, dxb)], "tn", tm=1024, tn=1024, tk=2048, out_dtype=BF16, name=f"{tag}_gout")
    du1, du2, dgate = gate_bwd(dxb, w["w_out"], u1, u2, proj, name=f"{tag}_dgate")
    g_pd = matmul([(du1, odil)], "tn", tm=1024, tn=256, tk=2048, out_dtype=BF16, name=f"{tag}_gpd")
    g_ps = matmul([(du2, osb)], "tn", tm=1024, tn=256, tk=2048, out_dtype=BF16, name=f"{tag}_gps")
    dodil = matmul([(du1, w["w_proj_dil"])], "nn", tm=512, tn=256, tk=1024, out_dtype=F32, name=f"{tag}_dodil")
    dosb = matmul([(du2, w["w_proj_sb"])], "nn", tm=512, tn=256, tk=1024, out_dtype=F32, name=f"{tag}_dosb")
    dsum, do_wide, lse_wide, dsum_wide = dil_bwd_prep(dodil, odil, lse, name=f"{tag}_dprep")
    dos = [dodil[None]] + list(do_wide)
    lss = [lse[None]] + list(lse_wide)
    dss = [dsum[None]] + list(dsum_wide)
    dqs, dks, dvs = [], [], []
    for g in range(N_DIL_GROUPS):
        dq, dk, dv = dil_bwd(qks[g], vs[g], dos[g], lss[g], dss[g], name=f"{tag}_ddil{g}")
        dqs.append(dq)
        dks.append(dk)
        dvs.append(dv)
    dqkv = rope_join(dqs, dks, dvs, tables, name=f"{tag}_drope")
    gtot = head_sums(dosb, osb, round_a=True, name=f"{tag}_gsum")
    dq_s, dk_s, dv_s = sb_bwd(proj, dosb, gtot, name=f"{tag}_dsb")
    dproj = jnp.concatenate([dqkv, dq_s.astype(BF16), dk_s.astype(BF16), dv_s.astype(BF16), dgate], axis=1)
    g_in = matmul([(dproj, h)], "tn", tm=1280, tn=1024, tk=2048, out_dtype=BF16, name=f"{tag}_gin")
    return {"w_in": g_in, "w_proj_dil": g_pd, "w_proj_sb": g_ps, "w_out": g_out}, dproj


def mixer_bwd_input(dx, dproj, saved, gain, w, tag):
    x = saved[0]
    dh = matmul([(dproj, w["w_in"])], "nn", tm=256, tn=1024, tk=5120, out_dtype=F32, name=f"{tag}_dh")
    return rms_bwd(dh, x, gain, dx, name=f"{tag}_dnorm")


def kernel(x, norm_ffn1, ffn1_w_gate, ffn1_w_up, ffn1_w_down, norm_mix, w_in, w_proj_dil, w_proj_sb, w_out, norm_ffn2, ffn2_w_gate, ffn2_w_up, ffn2_w_down, norm_final, loss_target, m_norm_ffn1, m_ffn1_w_gate, m_ffn1_w_up, m_ffn1_w_down, m_norm_mix, m_w_in, m_w_proj_dil, m_w_proj_sb, m_w_out, m_norm_ffn2, m_ffn2_w_gate, m_ffn2_w_up, m_ffn2_w_down, m_norm_final, v_norm_ffn1, v_ffn1_w_gate, v_ffn1_w_up, v_ffn1_w_down, v_norm_mix, v_w_in, v_w_proj_dil, v_w_proj_sb, v_w_out, v_norm_ffn2, v_ffn2_w_gate, v_ffn2_w_up, v_ffn2_w_down, v_norm_final):
    args = dict(locals())
    names = [name for name, _, _, _ in PACK_LAYOUT]
    t = x.shape[1]
    xs = x.reshape(t, D_MODEL)
    target = loss_target.reshape(t, D_MODEL)
    tables = rope_tables(t)

    my_index = 4 * lax.axis_index("x") + 2 * lax.axis_index("y") + lax.axis_index("c")
    parts = [(l, p) for l in range(2) for p in SUBBLOCKS]
    w_shards = {n: args[n] for n in names}
    gains = {n: args[n] for n in NORM_ROWS}

    in_flight, gathered, order_token = {}, {}, jnp.zeros((1, 1), F32)
    for l, p in parts:
        packed = pack_shards(w_shards, l, p) + order_token.astype(BF16)
        in_flight[(l, p)] = spread_start(packed, landing_zone(packed, my_index), per_peer=False, name=f"gather_start_l{l}_{p}")
        order_token = in_flight[(l, p)][4][0:1, 0:1]

    def weights_of(l, p, after):
        gathered[(l, p)] = spread_wait(in_flight[(l, p)], after, per_peer=False, name=f"gather_wait_l{l}_{p}")
        return unpack_gathered(gathered[(l, p)], p)

    saved, weights = {}, {}
    act = xs
    for l in range(2):
        for p in SUBBLOCKS:
            weights[(l, p)] = weights_of(l, p, act)
            gain = _row(gains["norm_" + p][l])
            if (l, p) == parts[0]:
                gain = gain + order_token
            if p == "mix":
                act, saved[(l, p)] = mixer_fwd(act, gain, weights[(l, p)], tables, f"l{l}_mix")
            else:
                act, saved[(l, p)] = ffn_fwd(act, gain, weights[(l, p)], p, f"l{l}_{p}")
    dx, dxb, g_final, loss_part = final_loss(act, _row(norm_final), target, name="loss_head")

    gain_grads, sent = {}, {}
    order_token = jnp.zeros((1, 1), F32)
    for l, p in reversed(parts):
        w, sv = weights[(l, p)], saved[(l, p)]
        if p == "mix":
            gw, rest = mixer_bwd_weights(dxb, sv, w, tables, f"l{l}_mix")
        else:
            gw, rest = ffn_bwd_weights(dxb, sv, w, p, f"l{l}_{p}")
        slices = pack_full_grads(gw, p)
        own = lax.dynamic_index_in_dim(slices, my_index, 0, keepdims=False)
        sent[(l, p)] = spread_start(slices, landing_zone(own, my_index), per_peer=True, name=f"reduce_start_l{l}_{p}")
        gain = _row(gains["norm_" + p][l]) + sent[(l, p)][4][0:1, 0:1]
        if p == "mix":
            dx, dxb, gain_grads[("norm_mix", l)] = mixer_bwd_input(dx, rest, sv, gain, w, f"l{l}_mix")
        else:
            dx, dxb, gain_grads[("norm_" + p, l)] = ffn_bwd_input(dx, rest, sv, gain, w, p, f"l{l}_{p}")

    loss_row = jnp.pad(loss_part[:, :1], ((0, 0), (0, LANES - 1)))
    small = jnp.concatenate([gain_grads[(n, l)] for n in NORM_ROWS for l in range(2)] + [g_final, loss_row], axis=0)
    small_g = sum_partials(all_gather_rows(small, name="gather_gain_grads"), tr=8, name="sum_gain_grads")
    zero_row = jnp.zeros((1, LANES), F32)
    small_of = lambda pre: jnp.concatenate([args[pre + n] for n in NORM_ROWS] + [_row(args[pre + "norm_final"]), zero_row], axis=0)[None]
    small_out = adamw(small_g[None], small_of(""), small_of("m_"), small_of("v_"), tr=8, name="update_gains")
    small_all = [small_g] + [o[0] for o in small_out]

    grad_of = {}
    for l, p in reversed(parts):
        partials = spread_wait(sent[(l, p)], dx, per_peer=True, name=f"reduce_wait_l{l}_{p}")
        summed = sum_partials(partials, tr=UPDATE_ROWS[p], name=f"sum_l{l}_{p}")
        for n, g in unpack_summed(summed, p, w_shards).items():
            grad_of.setdefault(n, [None, None])[l] = g
    big_all = [{}, {}, {}, {}]
    for p in ("ffn2", "mix", "ffn1"):
        for n, _, _, _ in SUBBLOCKS[p]:
            g = jnp.stack(grad_of[n], axis=0)
            outs = adamw(g, args[n], args["m_" + n], args["v_" + n], tr=512, name=f"update_{n}")
            for kind, arr in enumerate([g] + list(outs)):
                big_all[kind][n] = arr

    def gains_of(s):
        out = {n: s[2 * i:2 * i + 2] for i, n in enumerate(NORM_ROWS)}
        out["norm_final"] = s[6]
        return out

    order = ["norm_ffn1", "ffn1_w_gate", "ffn1_w_up", "ffn1_w_down", "norm_mix", "w_in", "w_proj_dil", "w_proj_sb", "w_out",
             "norm_ffn2", "ffn2_w_gate", "ffn2_w_up", "ffn2_w_down", "norm_final"]
    results = []
    for kind in range(4):
        both = {**big_all[kind], **gains_of(small_all[kind])}
        results += [both[n] for n in order]
    loss = small_g[7, 0]
    return (loss, dx.reshape(1, t, D_MODEL), *results)
```

```python
import functools

import jax
import jax.numpy as jnp
from jax import lax
from jax.experimental import pallas as pl
from jax.experimental.pallas import tpu as pltpu

F32 = jnp.float32
BF16 = jnp.bfloat16

D_MODEL = 1024
HEAD_DIM = 64
GROUP_W = 256
D_IN = 5120
N_DIL_GROUPS = 3
DIL_SPAN = 128
DILATIONS = (1, 4, 16)
ROPE_THETA = 500000.0
ROPE_DIM = 16
RMS_EPS = 1e-6
ATT_SCALE = HEAD_DIM ** -0.5
QS_BLK, KS_BLK, VS_BLK = 9, 10, 11
GATE_DIL_BLK, GATE_SB_BLK = 3, 4

ADAM_LR, ADAM_B1, ADAM_B2, ADAM_EPS, ADAM_WD, ADAM_STEP = 0.001, 0.9, 0.999, 1e-08, 0.01, 10

N_DEV = 8
LANES = 1024
VMEM_PHYSICAL_V7X = 64 << 20
VMEM_TEMP_HEADROOM = 20 << 20

PACK_LAYOUT = (
    ("ffn1_w_gate", 352, True, (1024, 2816)),
    ("ffn1_w_up", 352, True, (1024, 2816)),
    ("ffn1_w_down", 352, False, (2816, 1024)),
    ("w_in", 640, True, (1024, 5120)),
    ("w_proj_dil", 32, True, (256, 1024)),
    ("w_proj_sb", 32, True, (256, 1024)),
    ("w_out", 128, False, (1024, 1024)),
    ("ffn2_w_gate", 352, True, (1024, 2816)),
    ("ffn2_w_up", 352, True, (1024, 2816)),
    ("ffn2_w_down", 352, False, (2816, 1024)),
)
SUBBLOCKS = {"ffn1": PACK_LAYOUT[0:3], "mix": PACK_LAYOUT[3:7], "ffn2": PACK_LAYOUT[7:10]}
UPDATE_ROWS = {"ffn1": 352, "mix": 416, "ffn2": 352}
NORM_ROWS = ("norm_ffn1", "norm_mix", "norm_ffn2")


def _nbytes(shape, dtype):
    n = 1
    for s in shape:
        n *= s
    return n * jnp.dtype(dtype).itemsize


def _pcall(body, *, name, grid, in_specs, out_specs, out_shape, blocks, scratch_shapes=(), scratch_bytes=0):
    need = 2 * sum(_nbytes(s, d) for s, d in blocks) + scratch_bytes + VMEM_TEMP_HEADROOM
    limit = min(need, VMEM_PHYSICAL_V7X - (4 << 20))
    in_hbm = lambda s: pltpu.HBM(s.shape, s.dtype)
    out_shape = [in_hbm(s) for s in out_shape] if isinstance(out_shape, (list, tuple)) else in_hbm(out_shape)
    call = pl.pallas_call(
        body, name=name, grid=grid, in_specs=in_specs, out_specs=out_specs, out_shape=out_shape,
        scratch_shapes=scratch_shapes,
        compiler_params=pltpu.CompilerParams(vmem_limit_bytes=limit),
    )
    return lambda *args: call(*[pltpu.with_memory_space_constraint(a, pltpu.HBM) for a in args])


def _dot(a, b, form):
    dn = {"nn": (((1,), (0,)), ((), ())), "nt": (((1,), (1,)), ((), ())), "tn": (((0,), (0,)), ((), ()))}[form]
    return lax.dot_general(a.astype(BF16), b.astype(BF16), dn, preferred_element_type=F32)


def _sigmoid(x):
    return 1.0 / (1.0 + jnp.exp(-x))


def matmul(pairs, form, *, tm, tn, tk, out_dtype, name, scale=1.0, res=None):
    a0, b0 = pairs[0]
    if form == "tn":
        kdim, m = a0.shape
        n = b0.shape[1]
    else:
        m, kdim = a0.shape
        n = b0.shape[1] if form == "nn" else b0.shape[0]
    tm, tn, tk = min(tm, m), min(tn, n), min(tk, kdim)
    assert m % tm == 0 and n % tn == 0 and kdim % tk == 0, (name, m, n, kdim, tm, tn, tk)
    nk = kdim // tk
    npairs = len(pairs)

    if form == "tn":
        a_blk, a_map = (tk, tm), (lambda j, i, k: (k, i))
    else:
        a_blk, a_map = (tm, tk), (lambda j, i, k: (i, k))
    if form == "nt":
        b_blk, b_map = (tn, tk), (lambda j, i, k: (j, k))
    else:
        b_blk, b_map = (tk, tn), (lambda j, i, k: (k, j))
    o_map = lambda j, i, k: (i, j)

    def body(*refs):
        ab = refs[:2 * npairs]
        rest = refs[2 * npairs:]
        if res is not None:
            r_ref, o_ref = rest[0], rest[1]
            rest = rest[2:]
        else:
            r_ref, o_ref = None, rest[0]
            rest = rest[1:]

        def partial_sum():
            p = _dot(ab[0][...], ab[1][...], form)
            for q in range(1, npairs):
                p = p + _dot(ab[2 * q][...], ab[2 * q + 1][...], form)
            return p

        def finish(acc):
            out = acc * scale if scale != 1.0 else acc
            if r_ref is not None:
                out = r_ref[...] + out
            o_ref[...] = out.astype(out_dtype)

        if nk == 1:
            finish(partial_sum())
        else:
            acc_ref = rest[0]
            k = pl.program_id(2)

            @pl.when(k == 0)
            def _():
                acc_ref[...] = partial_sum()

            @pl.when(k > 0)
            def _():
                acc_ref[...] += partial_sum()

            @pl.when(k == nk - 1)
            def _():
                finish(acc_ref[...])

    in_specs, args, blocks = [], [], []
    for a, b in pairs:
        in_specs += [pl.BlockSpec(a_blk, a_map), pl.BlockSpec(b_blk, b_map)]
        args += [a, b]
        blocks += [(a_blk, a.dtype), (b_blk, b.dtype)]
    if res is not None:
        in_specs.append(pl.BlockSpec((tm, tn), o_map))
        args.append(res)
        blocks.append(((tm, tn), res.dtype))
    blocks.append(((tm, tn), out_dtype))
    scratch = [pltpu.VMEM((tm, tn), F32)] if nk > 1 else []
    return _pcall(
        body, name=name, grid=(n // tn, m // tm, nk), in_specs=in_specs,
        out_specs=pl.BlockSpec((tm, tn), o_map), out_shape=jax.ShapeDtypeStruct((m, n), out_dtype),
        blocks=blocks, scratch_shapes=scratch, scratch_bytes=(tm * tn * 4 if nk > 1 else 0),
    )(*args)


def swiglu_fwd(h, wg_t, wu_t, *, name, tm=512, tn=1408):
    t, d = h.shape
    f = wg_t.shape[0]
    tm, tn = min(tm, t), min(tn, f)

    def body(h_ref, wg_ref, wu_ref, ga_ref, gb_ref, s_ref):
        hh = h_ref[...]
        a = _dot(hh, wg_ref[...], "nt")
        b = _dot(hh, wu_ref[...], "nt")
        sg = _sigmoid(a)
        silu = a * sg
        ga_ref[...] = (b * (sg * (1.0 + a * (1.0 - sg)))).astype(BF16)
        gb_ref[...] = silu.astype(BF16)
        s_ref[...] = (silu * b).astype(BF16)

    w_spec = pl.BlockSpec((tn, d), lambda j, i: (j, 0))
    o_spec = pl.BlockSpec((tm, tn), lambda j, i: (i, j))
    o_shape = jax.ShapeDtypeStruct((t, f), BF16)
    return _pcall(
        body, name=name, grid=(f // tn, t // tm),
        in_specs=[pl.BlockSpec((tm, d), lambda j, i: (i, 0)), w_spec, w_spec],
        out_specs=[o_spec, o_spec, o_spec], out_shape=[o_shape, o_shape, o_shape],
        blocks=[((tm, d), BF16), ((tn, d), BF16), ((tn, d), BF16)] + [((tm, tn), BF16)] * 3,
    )(h, wg_t, wu_t)


def swiglu_bwd(dyb, wd, ga, gb, *, name, scale, tm=512, tn=1408):
    t, d = dyb.shape
    f = wd.shape[0]
    tm, tn = min(tm, t), min(tn, f)

    def body(dy_ref, wd_ref, ga_ref, gb_ref, da_ref, db_ref):
        ds = _dot(dy_ref[...], wd_ref[...], "nt") * scale
        da_ref[...] = (ds * ga_ref[...].astype(F32)).astype(BF16)
        db_ref[...] = (ds * gb_ref[...].astype(F32)).astype(BF16)

    o_spec = pl.BlockSpec((tm, tn), lambda j, i: (i, j))
    o_shape = jax.ShapeDtypeStruct((t, f), BF16)
    return _pcall(
        body, name=name, grid=(f // tn, t // tm),
        in_specs=[pl.BlockSpec((tm, d), lambda j, i: (i, 0)), pl.BlockSpec((tn, d), lambda j, i: (j, 0)), o_spec, o_spec],
        out_specs=[o_spec, o_spec], out_shape=[o_shape, o_shape],
        blocks=[((tm, d), BF16), ((tn, d), BF16)] + [((tm, tn), BF16)] * 4,
    )(dyb, wd, ga, gb)


def gate_fwd(odil, osb, wpd_t, wps_t, proj, *, name, tm=512):
    t = odil.shape[0]
    tm = min(tm, t)

    def body(od_ref, os_ref, wpd_ref, wps_ref, g1_ref, g2_ref, y_ref, u1_ref, u2_ref):
        u1 = _dot(od_ref[...], wpd_ref[...], "nt")
        u2 = _dot(os_ref[...], wps_ref[...], "nt")
        y = _sigmoid(g1_ref[...].astype(F32)) * u1 + _sigmoid(g2_ref[...].astype(F32)) * u2
        y_ref[...] = y.astype(BF16)
        u1_ref[...] = u1.astype(BF16)
        u2_ref[...] = u2.astype(BF16)

    o_spec = pl.BlockSpec((tm, D_MODEL), lambda i: (i, 0))
    w_spec = pl.BlockSpec((D_MODEL, GROUP_W), lambda i: (0, 0))
    a_spec = pl.BlockSpec((tm, GROUP_W), lambda i: (i, 0))
    o_shape = jax.ShapeDtypeStruct((t, D_MODEL), BF16)
    return _pcall(
        body, name=name, grid=(t // tm,),
        in_specs=[a_spec, a_spec, w_spec, w_spec,
                  pl.BlockSpec((tm, D_MODEL), lambda i: (i, GATE_DIL_BLK)),
                  pl.BlockSpec((tm, D_MODEL), lambda i: (i, GATE_SB_BLK))],
        out_specs=[o_spec, o_spec, o_spec], out_shape=[o_shape, o_shape, o_shape],
        blocks=[((tm, GROUP_W), F32)] * 2 + [((D_MODEL, GROUP_W), BF16)] * 2 + [((tm, D_MODEL), BF16)] * 5,
    )(odil, osb, wpd_t, wps_t, proj, proj)


def gate_bwd(dxb, wout, u1, u2, proj, *, name, tm=512):
    t = dxb.shape[0]
    tm = min(tm, t)

    def body(dx_ref, w_ref, u1_ref, u2_ref, g1_ref, g2_ref, du1_ref, du2_ref, dg_ref):
        dy = _dot(dx_ref[...], w_ref[...], "nt")
        s1 = _sigmoid(g1_ref[...].astype(F32))
        s2 = _sigmoid(g2_ref[...].astype(F32))
        du1_ref[...] = (dy * s1).astype(BF16)
        du2_ref[...] = (dy * s2).astype(BF16)
        dg_ref[:, :D_MODEL] = (dy * u1_ref[...].astype(F32) * s1 * (1.0 - s1)).astype(BF16)
        dg_ref[:, D_MODEL:] = (dy * u2_ref[...].astype(F32) * s2 * (1.0 - s2)).astype(BF16)

    o_spec = pl.BlockSpec((tm, D_MODEL), lambda i: (i, 0))
    o_shape = jax.ShapeDtypeStruct((t, D_MODEL), BF16)
    return _pcall(
        body, name=name, grid=(t // tm,),
        in_specs=[o_spec, pl.BlockSpec((D_MODEL, D_MODEL), lambda i: (0, 0)), o_spec, o_spec,
                  pl.BlockSpec((tm, D_MODEL), lambda i: (i, GATE_DIL_BLK)),
                  pl.BlockSpec((tm, D_MODEL), lambda i: (i, GATE_SB_BLK))],
        out_specs=[o_spec, o_spec, pl.BlockSpec((tm, 2 * D_MODEL), lambda i: (i, 0))],
        out_shape=[o_shape, o_shape, jax.ShapeDtypeStruct((t, 2 * D_MODEL), BF16)],
        blocks=[((tm, D_MODEL), BF16)] * 9 + [((D_MODEL, D_MODEL), BF16)],
    )(dxb, wout, u1, u2, proj, proj)


def rms_fwd(x, gain, *, name, tm=512):
    t, d = x.shape
    tm = min(tm, t)

    def body(x_ref, g_ref, h_ref):
        xv = x_ref[...]
        rstd = lax.rsqrt(jnp.mean(xv * xv, axis=1, keepdims=True) + RMS_EPS)
        h_ref[...] = (xv * rstd * g_ref[...]).astype(BF16)

    return _pcall(
        body, name=name, grid=(t // tm,),
        in_specs=[pl.BlockSpec((tm, d), lambda i: (i, 0)), pl.BlockSpec((1, d), lambda i: (0, 0))],
        out_specs=pl.BlockSpec((tm, d), lambda i: (i, 0)), out_shape=jax.ShapeDtypeStruct((t, d), BF16),
        blocks=[((tm, d), F32), ((tm, d), BF16)],
    )(x, gain)


def matmul_res_norm(a, b, res, next_gain, *, scale, tm, name):
    t, k = a.shape
    d = b.shape[1]
    tm = min(tm, t)
    with_norm = next_gain is not None

    def body(a_ref, b_ref, r_ref, *rest):
        out = r_ref[...] + _dot(a_ref[...], b_ref[...], "nn") * scale
        if with_norm:
            g_ref, o_ref, h_ref = rest
            rstd = lax.rsqrt(jnp.mean(out * out, axis=1, keepdims=True) + RMS_EPS)
            h_ref[...] = (out * rstd * g_ref[...]).astype(BF16)
        else:
            o_ref, = rest
        o_ref[...] = out

    row = pl.BlockSpec((tm, d), lambda i: (i, 0))
    in_specs = [pl.BlockSpec((tm, k), lambda i: (i, 0)), pl.BlockSpec((k, d), lambda i: (0, 0)), row]
    args = [a, b, res]
    out_specs, out_shape = [row], [jax.ShapeDtypeStruct((t, d), F32)]
    if with_norm:
        in_specs.append(pl.BlockSpec((1, d), lambda i: (0, 0)))
        args.append(next_gain)
        out_specs.append(row)
        out_shape.append(jax.ShapeDtypeStruct((t, d), BF16))
    outs = _pcall(
        body, name=name, grid=(t // tm,), in_specs=in_specs, out_specs=out_specs, out_shape=out_shape,
        blocks=[((tm, k), a.dtype), ((k, d), b.dtype), ((tm, d), F32), ((tm, d), F32), ((tm, d), BF16)],
    )(*args)
    return (outs[0], outs[1]) if with_norm else (outs[0], None)


def _rms_bwd_rows(dhv, xv, g, drv):
    rstd = lax.rsqrt(jnp.mean(xv * xv, axis=1, keepdims=True) + RMS_EPS)
    xh = xv * rstd
    dxh = dhv * g
    dx = drv + rstd * (dxh - xh * jnp.mean(dxh * xh, axis=1, keepdims=True))
    return dx, jnp.sum(dhv * xh, axis=0, keepdims=True)


def matmul_rms_bwd(pairs, x, gain, dres, *, tm, name):
    t, d = x.shape
    tm = min(tm, t)
    npairs = len(pairs)

    def body(*refs):
        ab = refs[:2 * npairs]
        x_ref, g_ref, dr_ref, dx_ref, dxb_ref, dg_ref = refs[2 * npairs:]
        dh = _dot(ab[0][...], ab[1][...], "nn")
        for q in range(1, npairs):
            dh = dh + _dot(ab[2 * q][...], ab[2 * q + 1][...], "nn")
        dx, part = _rms_bwd_rows(dh, x_ref[...], g_ref[...], dr_ref[...])
        dx_ref[...] = dx
        dxb_ref[...] = dx.astype(BF16)

        @pl.when(pl.program_id(0) == 0)
        def _():
            dg_ref[...] = part

        @pl.when(pl.program_id(0) > 0)
        def _():
            dg_ref[...] += part

    in_specs, args, blocks = [], [], []
    for a, b in pairs:
        k = a.shape[1]
        in_specs += [pl.BlockSpec((tm, k), lambda i: (i, 0)), pl.BlockSpec((k, d), lambda i: (0, 0))]
        args += [a, b]
        blocks += [((tm, k), a.dtype), ((k, d), b.dtype)]
    row = pl.BlockSpec((tm, d), lambda i: (i, 0))
    vec = pl.BlockSpec((1, d), lambda i: (0, 0))
    return _pcall(
        body, name=name, grid=(t // tm,), in_specs=in_specs + [row, vec, row], out_specs=[row, row, vec],
        out_shape=[jax.ShapeDtypeStruct((t, d), F32), jax.ShapeDtypeStruct((t, d), BF16), jax.ShapeDtypeStruct((1, d), F32)],
        blocks=blocks + [((tm, d), F32)] * 3 + [((tm, d), BF16)],
    )(*args, x, gain, dres)


def final_loss(x, gain, target, *, name, tm=512):
    t, d = x.shape
    tm = min(tm, t)

    def body(x_ref, g_ref, t_ref, dx_ref, dxb_ref, dg_ref, loss_ref):
        xv = x_ref[...]
        g = g_ref[...]
        rstd = lax.rsqrt(jnp.mean(xv * xv, axis=1, keepdims=True) + RMS_EPS)
        xh = xv * rstd
        err = xh * g - t_ref[...]
        dy = err * (1.0 / d)
        dxh = dy * g
        dx = rstd * (dxh - xh * jnp.mean(dxh * xh, axis=1, keepdims=True))
        dx_ref[...] = dx
        dxb_ref[...] = dx.astype(BF16)
        part = jnp.sum(dy * xh, axis=0, keepdims=True)
        sq = jnp.sum(jnp.sum(err * err, axis=1, keepdims=True), axis=0, keepdims=True) * (0.5 / d)
        lpart = jnp.broadcast_to(sq, (1, 128))

        @pl.when(pl.program_id(0) == 0)
        def _():
            dg_ref[...] = part
            loss_ref[...] = lpart

        @pl.when(pl.program_id(0) > 0)
        def _():
            dg_ref[...] += part
            loss_ref[...] += lpart

    row = pl.BlockSpec((tm, d), lambda i: (i, 0))
    vec = pl.BlockSpec((1, d), lambda i: (0, 0))
    return _pcall(
        body, name=name, grid=(t // tm,), in_specs=[row, vec, row],
        out_specs=[row, row, vec, pl.BlockSpec((1, 128), lambda i: (0, 0))],
        out_shape=[jax.ShapeDtypeStruct((t, d), F32), jax.ShapeDtypeStruct((t, d), BF16),
                   jax.ShapeDtypeStruct((1, d), F32), jax.ShapeDtypeStruct((1, 128), F32)],
        blocks=[((tm, d), F32)] * 3 + [((tm, d), BF16)],
    )(x, gain, target)


def rope_tables(t):
    pos = jnp.arange(t, dtype=F32)
    inv_freq = ROPE_THETA ** (-jnp.arange(0, ROPE_DIM, 2, dtype=F32) / ROPE_DIM)
    ang = pos[:, None] * inv_freq[None, :]
    cos, sin = jnp.cos(ang), jnp.sin(ang)
    half = ROPE_DIM // 2
    pad = HEAD_DIM - ROPE_DIM
    one_head = lambda lo, hi, fill: jnp.concatenate([lo, hi, jnp.full((t, pad), fill, F32)], axis=1)
    zeros = jnp.zeros((t, half), F32)
    c = one_head(cos, cos, 1.0)
    sa = one_head(-sin, zeros, 0.0)
    sb = one_head(zeros, sin, 0.0)
    two = lambda a: jnp.concatenate([a, a], axis=1)
    return two(c), two(sa), two(sb)


def _rotate(xv, cv, sav, sbv):
    halves = []
    for half in range(2):
        x = xv[:, 128 * half:128 * (half + 1)]
        halves.append(x * cv + pltpu.roll(x, 120, 1) * sav + pltpu.roll(x, 8, 1) * sbv)
    return jnp.concatenate(halves, axis=1)


STAGE_CHUNKS = 4


def _stage(tm):
    return dict(scratch_shapes=[pltpu.VMEM((STAGE_CHUNKS, tm, 128), F32)], scratch_bytes=STAGE_CHUNKS * tm * 128 * 4)


def _split_residues(stage_ref, val, out_ref, d, col, dtype):
    rows, width = val.shape
    if d == 1:
        out_ref[0, :, col:col + width] = val.astype(dtype)
        return
    chunks = width // 128
    for c in range(chunks):
        stage_ref[c] = val[:, 128 * c:128 * (c + 1)]
    for r in range(d):
        for c in range(chunks):
            out_ref[r, :, col + 128 * c:col + 128 * (c + 1)] = stage_ref[c, pl.ds(r, rows // d, stride=d), :].astype(dtype)


def _join_residues(stage_ref, in_ref, d, col=0, width=GROUP_W):
    if d == 1:
        return in_ref[0, :, col:col + width].astype(F32)
    rows = in_ref.shape[1] * d
    chunks = width // 128
    for r in range(d):
        for c in range(chunks):
            stage_ref[c, pl.ds(r, rows // d, stride=d), :] = in_ref[r, :, col + 128 * c:col + 128 * (c + 1)].astype(F32)
    return jnp.concatenate([stage_ref[c] for c in range(chunks)], axis=1)


def rope_split(proj, tables, *, name, tm=512):
    c, sa, sb = tables
    t = c.shape[0]
    tm = min(tm, t)

    def body(*refs):
        pieces = refs[0:9]
        c_ref, sa_ref, sb_ref = refs[9:12]
        qk_out, v_out = refs[12:15], refs[15:18]
        stage = refs[18]
        cv, sav, sbv = c_ref[...], sa_ref[...], sb_ref[...]
        for g, d in enumerate(DILATIONS):
            for kind in range(3):
                xv = pieces[3 * kind + g][...].astype(F32)
                if kind < 2:
                    _split_residues(stage, _rotate(xv, cv, sav, sbv), qk_out[g], d, GROUP_W * kind, BF16)
                else:
                    _split_residues(stage, xv, v_out[g], d, 0, BF16)

    tab = pl.BlockSpec((tm, 128), lambda i: (i, 0))
    in_specs = [pl.BlockSpec((tm, GROUP_W), functools.partial(lambda i, cb: (i, cb), cb=cb)) for cb in range(9)]
    out_specs = ([pl.BlockSpec((d, tm // d, 2 * GROUP_W), lambda i: (0, i, 0)) for d in DILATIONS]
                 + [pl.BlockSpec((d, tm // d, GROUP_W), lambda i: (0, i, 0)) for d in DILATIONS])
    out_shape = ([jax.ShapeDtypeStruct((d, t // d, 2 * GROUP_W), BF16) for d in DILATIONS]
                 + [jax.ShapeDtypeStruct((d, t // d, GROUP_W), BF16) for d in DILATIONS])
    outs = _pcall(
        body, name=name, grid=(t // tm,), in_specs=in_specs + [tab, tab, tab], out_specs=out_specs, out_shape=out_shape,
        blocks=[((tm, GROUP_W), BF16)] * 18 + [((tm, 128), F32)] * 3,
        **_stage(tm),
    )(*([proj] * 9), c, sa, sb)
    return outs[0:3], outs[3:6]


def rope_join(dqs, dks, dvs, tables, *, name, tm=512):
    c, sa, sb = tables
    t = c.shape[0]
    tm = min(tm, t)

    def body(*refs):
        pieces = refs[0:9]
        c_ref, sa_ref, sb_ref = refs[9:12]
        o_ref, stage = refs[12], refs[13]
        cv, sav, sbv = c_ref[...], -sa_ref[...], -sb_ref[...]
        for kind in range(3):
            for g, d in enumerate(DILATIONS):
                xv = _join_residues(stage, pieces[3 * kind + g], d)
                if kind < 2:
                    xv = _rotate(xv, cv, sav, sbv)
                col = GROUP_W * (3 * kind + g)
                o_ref[:, col:col + GROUP_W] = xv.astype(BF16)

    tab = pl.BlockSpec((tm, 128), lambda i: (i, 0))
    in_specs = [pl.BlockSpec((d, tm // d, GROUP_W), lambda i: (0, i, 0)) for _ in range(3) for d in DILATIONS]
    return _pcall(
        body, name=name, grid=(t // tm,), in_specs=in_specs + [tab, tab, tab],
        out_specs=pl.BlockSpec((tm, 9 * GROUP_W), lambda i: (i, 0)), out_shape=jax.ShapeDtypeStruct((t, 9 * GROUP_W), BF16),
        blocks=[((tm, GROUP_W), F32)] * 9 + [((tm, 128), F32)] * 3 + [((tm, 9 * GROUP_W), BF16)],
        **_stage(tm),
    )(*dqs, *dks, *dvs, c, sa, sb)


def _head_mask(h):
    lane = lax.broadcasted_iota(jnp.int32, (1, GROUP_W), 1)
    return (lane // HEAD_DIM) == h


def _band_masks(heads):
    ri = lax.broadcasted_iota(jnp.int32, (heads * DIL_SPAN, DIL_SPAN), 0) % DIL_SPAN
    ci = lax.broadcasted_iota(jnp.int32, (heads * DIL_SPAN, DIL_SPAN), 1)
    return ci <= ri, ci >= ri


def dil_fwd(qk, v, *, name):
    d, nsub, _ = qk.shape
    nblk = nsub // DIL_SPAN

    def body(q_ref, kc_ref, kp_ref, vc_ref, vp_ref, o_ref, lse_ref):
        nb = pl.program_id(1)
        own, prev = _band_masks(1)
        prev = prev & (nb > 0)
        q, kc, kp, vc, vp = q_ref[0] * ATT_SCALE, kc_ref[0], kp_ref[0], vc_ref[0], vp_ref[0]
        o_acc = jnp.zeros((DIL_SPAN, GROUP_W), F32)
        for h in range(4):
            hm = _head_mask(h)
            qh = jnp.where(hm, q, jnp.zeros_like(q))
            sc = jnp.where(own, _dot(qh, kc, "nt"), -jnp.inf)
            sp = jnp.where(prev, _dot(qh, kp, "nt"), -jnp.inf)
            m = jnp.maximum(jnp.max(sc, axis=1, keepdims=True), jnp.max(sp, axis=1, keepdims=True))
            pc = jnp.exp(sc - m)
            pp = jnp.exp(sp - m)
            den = jnp.sum(pc, axis=1, keepdims=True) + jnp.sum(pp, axis=1, keepdims=True)
            oh = (_dot(pc, vc, "nn") + _dot(pp, vp, "nn")) / den
            o_acc = jnp.where(hm, oh, o_acc)
            lse_ref[0, :, 128 * h:128 * (h + 1)] = jnp.broadcast_to(m + jnp.log(den), (DIL_SPAN, 128))
        o_ref[0] = o_acc

    blk = (1, DIL_SPAN, GROUP_W)
    sblk = (1, DIL_SPAN, 512)
    prv = lambda nb: jnp.maximum(nb - 1, 0)
    return _pcall(
        body, name=name, grid=(d, nblk),
        in_specs=[pl.BlockSpec(blk, lambda r, nb: (r, nb, 0)),
                  pl.BlockSpec(blk, lambda r, nb: (r, nb, 1)),
                  pl.BlockSpec(blk, lambda r, nb: (r, prv(nb), 1)),
                  pl.BlockSpec(blk, lambda r, nb: (r, nb, 0)),
                  pl.BlockSpec(blk, lambda r, nb: (r, prv(nb), 0))],
        out_specs=[pl.BlockSpec(blk, lambda r, nb: (r, nb, 0)), pl.BlockSpec(sblk, lambda r, nb: (r, nb, 0))],
        out_shape=[jax.ShapeDtypeStruct((d, nsub, GROUP_W), F32), jax.ShapeDtypeStruct((d, nsub, 512), F32)],
        blocks=[(blk, BF16)] * 5 + [(blk, F32), (sblk, F32)],
    )(qk, qk, qk, v, v)


def dil_merge(outs, lses, *, name, tm=512):
    t = outs[0].shape[0] * outs[0].shape[1]
    tm = min(tm, t)

    def body(o0, o1, o2, l0, l1, l2, o_ref, lse_ref, stage):
        ls = [_join_residues(stage, l, d, 0, 512) for l, d in zip((l0, l1, l2), DILATIONS)]
        m = jnp.maximum(jnp.maximum(ls[0], ls[1]), ls[2])
        tot = m + jnp.log(jnp.exp(ls[0] - m) + jnp.exp(ls[1] - m) + jnp.exp(ls[2] - m))
        lse_ref[...] = tot
        lane = lax.broadcasted_iota(jnp.int32, (1, 128), 1)
        first = lane < HEAD_DIM
        acc = jnp.zeros((tm, GROUP_W), F32)
        for og, lg, d in zip((o0, o1, o2), ls, DILATIONS):
            w = jnp.exp(lg - tot)
            wide = jnp.concatenate([jnp.where(first, w[:, 0:128], w[:, 128:256]),
                                    jnp.where(first, w[:, 256:384], w[:, 384:512])], axis=1)
            acc = acc + wide * _join_residues(stage, og, d)
        o_ref[...] = acc

    o_in = [pl.BlockSpec((d, tm // d, GROUP_W), lambda i: (0, i, 0)) for d in DILATIONS]
    l_in = [pl.BlockSpec((d, tm // d, 512), lambda i: (0, i, 0)) for d in DILATIONS]
    return _pcall(
        body, name=name, grid=(t // tm,), in_specs=o_in + l_in,
        out_specs=[pl.BlockSpec((tm, GROUP_W), lambda i: (i, 0)), pl.BlockSpec((tm, 512), lambda i: (i, 0))],
        out_shape=[jax.ShapeDtypeStruct((t, GROUP_W), F32), jax.ShapeDtypeStruct((t, 512), F32)],
        blocks=[((tm, GROUP_W), F32)] * 4 + [((tm, 512), F32)] * 4,
        **_stage(tm),
    )(*outs, *lses)


def dil_bwd_prep(do, o, lse, *, name, tm=512):
    t = do.shape[0]
    tm = min(tm, t)
    wide = DILATIONS[1:]

    def body(do_ref, o_ref, lse_ref, ds_ref, *rest):
        do_out, lse_out, ds_out = rest[0:2], rest[2:4], rest[4:6]
        stage = rest[6]
        dov = do_ref[...]
        prod = dov * o_ref[...]
        for h in range(4):
            s = jnp.sum(jnp.where(_head_mask(h), prod, 0.0), axis=1, keepdims=True)
            ds_ref[:, 128 * h:128 * (h + 1)] = jnp.broadcast_to(s, (tm, 128))
        for i, d in enumerate(wide):
            _split_residues(stage, dov, do_out[i], d, 0, BF16)
            _split_residues(stage, lse_ref[...], lse_out[i], d, 0, F32)
            _split_residues(stage, ds_ref[...], ds_out[i], d, 0, F32)

    nat = lambda w: pl.BlockSpec((tm, w), lambda i: (i, 0))
    res = lambda d, w: pl.BlockSpec((d, tm // d, w), lambda i: (0, i, 0))
    shape = lambda d, w, dt: jax.ShapeDtypeStruct((d, t // d, w), dt)
    outs = _pcall(
        body, name=name, grid=(t // tm,), in_specs=[nat(GROUP_W), nat(GROUP_W), nat(512)],
        out_specs=[nat(512)] + [res(d, GROUP_W) for d in wide] + [res(d, 512) for d in wide] * 2,
        out_shape=([jax.ShapeDtypeStruct((t, 512), F32)] + [shape(d, GROUP_W, BF16) for d in wide]
                   + [shape(d, 512, F32) for d in wide] * 2),
        blocks=[((tm, GROUP_W), F32)] * 3 + [((tm, 512), F32)] * 6,
        **_stage(tm),
    )(do, o, lse)
    return outs[0], outs[1:3], outs[3:5], outs[5:7]


def head_sums(a, b, *, name, round_a=False, tm=512):
    t = a.shape[0]
    tm = min(tm, t)

    def body(a_ref, b_ref, o_ref):
        av = a_ref[...]
        if round_a:
            av = av.astype(BF16).astype(F32)
        prod = av * b_ref[...]
        for h in range(4):
            s = jnp.sum(jnp.where(_head_mask(h), prod, 0.0), axis=1, keepdims=True)
            o_ref[:, 128 * h:128 * (h + 1)] = jnp.broadcast_to(s, (tm, 128))

    spec = pl.BlockSpec((tm, GROUP_W), lambda i: (i, 0))
    return _pcall(
        body, name=name, grid=(t // tm,), in_specs=[spec, spec],
        out_specs=pl.BlockSpec((tm, 512), lambda i: (i, 0)), out_shape=jax.ShapeDtypeStruct((t, 512), F32),
        blocks=[((tm, GROUP_W), F32)] * 2 + [((tm, 512), F32)],
    )(a, b)


def dil_bwd(qk, v, do, lse, dsum, *, name):
    d, nsub, _ = qk.shape
    nblk = nsub // DIL_SPAN

    def body(qa_ref, qb_ref, kc_ref, kp_ref, vc_ref, vp_ref, doa_ref, dob_ref, la_ref, lb_ref, sa_ref, sb_ref,
             dq_ref, dk_ref, dv_ref):
        nb = pl.program_id(1)
        own, band = _band_masks(4)
        prev = band & (nb > 0)
        nxt = band & (nb < nblk - 1)
        kc, kp, vc, vp = kc_ref[0], kp_ref[0], vc_ref[0], vp_ref[0]
        qas, qbs = _stack_heads(qa_ref[0] * ATT_SCALE), _stack_heads(qb_ref[0] * ATT_SCALE)
        das, dbs = _stack_heads(doa_ref[0].astype(BF16)), _stack_heads(dob_ref[0].astype(BF16))
        stat = lambda ref: jnp.concatenate([ref[0, :, 128 * h:128 * (h + 1)] for h in range(4)], axis=0)
        la, lb, sa, sb = stat(la_ref), stat(lb_ref), stat(sa_ref), stat(sb_ref)

        def probs(qs, ds_, k, v, mask, l, s):
            p = jnp.where(mask, jnp.exp(_dot(qs, k, "nt") - l), 0.0)
            dsc = p * (_dot(ds_, v, "nt") - s)
            return p.astype(BF16), dsc.astype(BF16)

        p_cc, ds_cc = probs(qas, das, kc, vc, own, la, sa)
        _, ds_cp = probs(qas, das, kp, vp, prev, la, sa)
        p_nc, ds_nc = probs(qbs, dbs, kc, vc, nxt, lb, sb)
        dq_ref[0] = _unstack_heads(_dot(ds_cc, kc, "nn") + _dot(ds_cp, kp, "nn"), DIL_SPAN) * ATT_SCALE
        dk_ref[0] = _dot(ds_cc, qas, "tn") + _dot(ds_nc, qbs, "tn")
        dv_ref[0] = _dot(p_cc, das, "tn") + _dot(p_nc, dbs, "tn")

    blk = (1, DIL_SPAN, GROUP_W)
    sblk = (1, DIL_SPAN, 512)
    prv = lambda nb: jnp.maximum(nb - 1, 0)
    nxt_ = lambda nb: jnp.minimum(nb + 1, nblk - 1)
    cur_at = lambda c: pl.BlockSpec(blk, functools.partial(lambda r, nb, c: (r, nb, c), c=c))
    prv_at = lambda c: pl.BlockSpec(blk, functools.partial(lambda r, nb, c: (r, prv(nb), c), c=c))
    nxt_at = lambda c: pl.BlockSpec(blk, functools.partial(lambda r, nb, c: (r, nxt_(nb), c), c=c))
    s_cur = pl.BlockSpec(sblk, lambda r, nb: (r, nb, 0))
    s_nxt = pl.BlockSpec(sblk, lambda r, nb: (r, nxt_(nb), 0))
    o_spec = pl.BlockSpec(blk, lambda r, nb: (r, nb, 0))
    o_shape = jax.ShapeDtypeStruct((d, nsub, GROUP_W), F32)
    return _pcall(
        body, name=name, grid=(d, nblk),
        in_specs=[cur_at(0), nxt_at(0), cur_at(1), prv_at(1), cur_at(0), prv_at(0), cur_at(0), nxt_at(0),
                  s_cur, s_nxt, s_cur, s_nxt],
        out_specs=[o_spec, o_spec, o_spec], out_shape=[o_shape, o_shape, o_shape],
        blocks=[(blk, BF16)] * 6 + [(blk, F32)] * 5 + [(sblk, F32)] * 4,
    )(qk, qk, qk, qk, v, v, do, do, lse, lse, dsum, dsum)


def _tri_dot(x, b):
    hi = x.astype(BF16)
    lo = (x - hi.astype(F32)).astype(BF16)
    return _dot(hi, b, "nn") + _dot(lo, b, "nn")


SB_TILE = 256


def _stack_heads(a):
    return jnp.concatenate([jnp.where(_head_mask(h), a, jnp.zeros_like(a)) for h in range(4)], axis=0)


def _unstack_heads(acc, rows):
    out = acc[0:rows]
    for h in range(1, 4):
        out = jnp.where(_head_mask(h), acc[h * rows:(h + 1) * rows], out)
    return out


def _tri_masks(n):
    ri = lax.broadcasted_iota(jnp.int32, (n, n), 0)
    ci = lax.broadcasted_iota(jnp.int32, (n, n), 1)
    return (ri > ci).astype(BF16), (ri >= ci).astype(BF16)


def _sb_weights(qs, kt, after, c_keep, diagonal):
    z = _dot(qs, kt, "nt")
    lbeta = jnp.minimum(z, 0.0) - jnp.log(1.0 + jnp.exp(-jnp.abs(z)))
    lkeep = lbeta - z
    past = None
    if diagonal:
        n = SB_TILE
        past = lax.broadcasted_iota(jnp.int32, z.shape, 1) < lax.broadcasted_iota(jnp.int32, z.shape, 0) % n
        lkeep = jnp.where(past, lkeep, 0.0)
    w = jnp.exp(lbeta + _tri_dot(lkeep, after) + c_keep)
    if diagonal:
        w = jnp.where(past, w, 0.0)
    return z, past, lbeta, lkeep, w


def sb_fwd(proj, *, name):
    t = proj.shape[0]
    n = SB_TILE
    assert t % n == 0

    def body(q_ref, k_ref, v_ref, o_ref, acc_ref):
        qb = pl.program_id(0)
        qs = _stack_heads(q_ref[...] * ATT_SCALE)
        after, _ = _tri_masks(n)

        def tile(off, diagonal, c_keep):
            kt = k_ref[pl.ds(off, n), :]
            vt = v_ref[pl.ds(off, n), :]
            _, _, _, lkeep, w = _sb_weights(qs, kt, after, c_keep, diagonal)
            pv = _tri_dot(w, vt)
            if diagonal:
                acc_ref[...] = pv
            else:
                acc_ref[...] += pv
            return c_keep + jnp.sum(lkeep, axis=1, keepdims=True)

        c0 = tile(pl.multiple_of(qb * n, n), True, jnp.zeros((4 * n, 1), F32))
        lax.fori_loop(0, qb, lambda it, c: tile(pl.multiple_of((qb - 1 - it) * n, n), False, c), c0)
        o_ref[...] = _unstack_heads(acc_ref[...], n)

    full = lambda cb: pl.BlockSpec((t, GROUP_W), functools.partial(lambda i, cb: (0, cb), cb=cb))
    return _pcall(
        body, name=name, grid=(t // n,),
        in_specs=[pl.BlockSpec((n, GROUP_W), lambda i: (i, QS_BLK)), full(KS_BLK), full(VS_BLK)],
        out_specs=pl.BlockSpec((n, GROUP_W), lambda i: (i, 0)), out_shape=jax.ShapeDtypeStruct((t, GROUP_W), F32),
        blocks=[((n, GROUP_W), BF16), ((t, GROUP_W), BF16), ((t, GROUP_W), BF16), ((n, GROUP_W), F32)],
        scratch_shapes=[pltpu.VMEM((4 * n, GROUP_W), F32)], scratch_bytes=4 * n * GROUP_W * 4,
    )(proj, proj, proj)


def sb_bwd(proj, do, gtot, *, name):
    t = proj.shape[0]
    n = SB_TILE
    assert t % n == 0

    def body(q_ref, k_ref, v_ref, do_ref, gt_ref, dq_ref, dk_ref, dv_ref, acc_ref):
        qb = pl.program_id(0)

        @pl.when(qb == 0)
        def _():
            dk_ref[...] = jnp.zeros_like(dk_ref)
            dv_ref[...] = jnp.zeros_like(dv_ref)

        qs = _stack_heads(q_ref[...] * ATT_SCALE)
        dos = _stack_heads(do_ref[...].astype(BF16))
        gt = jnp.concatenate([jnp.max(gt_ref[:, 128 * h:128 * (h + 1)], axis=1, keepdims=True) for h in range(4)], axis=0)
        after, from_on = _tri_masks(n)

        def tile(off, diagonal, carry):
            c_keep, c_g = carry
            kt = k_ref[pl.ds(off, n), :]
            vt = v_ref[pl.ds(off, n), :]
            z, past, lbeta, lkeep, w = _sb_weights(qs, kt, after, c_keep, diagonal)
            gw = w * _dot(dos, vt, "nt")
            big_g = gt - (_tri_dot(gw, from_on) + c_g)
            dz = gw * jnp.exp(lbeta - z) - big_g * jnp.exp(lbeta)
            if diagonal:
                dz = jnp.where(past, dz, 0.0)
            dz = dz.astype(BF16)
            dk_ref[pl.ds(off, n), :] += _dot(dz, qs, "tn")
            dv_ref[pl.ds(off, n), :] += _dot(w, dos, "tn")
            dq = _dot(dz, kt, "nn")
            if diagonal:
                acc_ref[...] = dq
            else:
                acc_ref[...] += dq
            return c_keep + jnp.sum(lkeep, axis=1, keepdims=True), c_g + jnp.sum(gw, axis=1, keepdims=True)

        zero_col = jnp.zeros((4 * n, 1), F32)
        c0 = tile(pl.multiple_of(qb * n, n), True, (zero_col, zero_col))
        lax.fori_loop(0, qb, lambda it, c: tile(pl.multiple_of((qb - 1 - it) * n, n), False, c), c0)
        dq_ref[...] = _unstack_heads(acc_ref[...], n) * ATT_SCALE

    full = lambda cb: pl.BlockSpec((t, GROUP_W), functools.partial(lambda i, cb: (0, cb), cb=cb))
    whole = pl.BlockSpec((t, GROUP_W), lambda i: (0, 0))
    rowblk = pl.BlockSpec((n, GROUP_W), lambda i: (i, 0))
    shape = jax.ShapeDtypeStruct((t, GROUP_W), F32)
    return _pcall(
        body, name=name, grid=(t // n,),
        in_specs=[pl.BlockSpec((n, GROUP_W), lambda i: (i, QS_BLK)), full(KS_BLK), full(VS_BLK), rowblk,
                  pl.BlockSpec((n, 512), lambda i: (i, 0))],
        out_specs=[rowblk, whole, whole], out_shape=[shape, shape, shape],
        blocks=[((n, GROUP_W), BF16), ((t, GROUP_W), BF16), ((t, GROUP_W), BF16), ((n, GROUP_W), F32),
                ((n, 512), F32), ((n, GROUP_W), F32), ((t, GROUP_W), F32), ((t, GROUP_W), F32)],
        scratch_shapes=[pltpu.VMEM((4 * n, GROUP_W), F32)], scratch_bytes=4 * n * GROUP_W * 4,
    )(proj, proj, proj, do, gtot)


def _mesh_place():
    return lax.axis_index("x"), lax.axis_index("y"), lax.axis_index("c")


def _flip(place, mask):
    x, y, c = place
    return ((1 - x) if mask & 4 else x, (1 - y) if mask & 2 else y, (1 - c) if mask & 1 else c)


def _dev_index(place):
    x, y, c = place
    return 4 * x + 2 * y + c


HBM_SPEC = pl.BlockSpec(memory_space=pltpu.HBM)


def all_gather_rows(shard, *, name):
    rows, lanes = shard.shape

    def body(x_ref, out_ref, send_sems, recv_sems, local_sem):
        me = _mesh_place()
        x, y, c = me
        sibling = _flip(me, 1)
        chips = [_flip(me, 4), _flip(me, 2), _flip(me, 6)]

        def copy(k, block, to, src=None):
            dst = out_ref.at[_dev_index(block)]
            return pltpu.make_async_remote_copy(
                src_ref=dst if src is None else src, dst_ref=dst, send_sem=send_sems.at[k], recv_sem=recv_sems.at[k],
                device_id=to, device_id_type=pl.DeviceIdType.MESH)

        mine = pltpu.make_async_copy(x_ref, out_ref.at[_dev_index(me)], local_sem)
        mine.start()
        first = [copy(0, me, sibling, src=x_ref)] + [copy(1 + j, me, chip, src=x_ref) for j, chip in enumerate(chips)]
        for cp in first:
            cp.start()
        passed = [copy(4 + j, chip, sibling) for j, chip in enumerate(chips)]
        for j, chip in enumerate(chips):
            copy(1 + j, chip, me).wait_recv()
            passed[j].start()
        copy(0, sibling, me).wait_recv()
        for j, chip in enumerate(chips):
            copy(4 + j, _flip(chip, 1), me).wait_recv()
        for cp in first + passed:
            cp.wait_send()
        mine.wait()

    return pl.pallas_call(
        body, name=name, in_specs=[HBM_SPEC], out_specs=HBM_SPEC,
        out_shape=jax.ShapeDtypeStruct((N_DEV, rows, lanes), shard.dtype),
        scratch_shapes=[pltpu.SemaphoreType.DMA((7,)), pltpu.SemaphoreType.DMA((7,)), pltpu.SemaphoreType.DMA],
    )(shard)


SEM_SPEC = pl.BlockSpec(memory_space=pltpu.SEMAPHORE)
DATAFLOW_EFFECT = pltpu.SideEffectType.DATAFLOW_SIDE_EFFECTING


def _spread_copies(src_ref, land_ref, send_sems, recv_sems, per_peer, arriving):
    me = _mesh_place()
    out = []
    for mask in range(1, N_DEV):
        peer = _flip(me, mask)
        data_of = _dev_index(me) if arriving else _dev_index(peer)
        slot = _dev_index(peer) if arriving else _dev_index(me)
        out.append(pltpu.make_async_remote_copy(
            src_ref=src_ref.at[data_of] if per_peer else src_ref, dst_ref=land_ref.at[slot],
            send_sem=send_sems.at[mask - 1], recv_sem=recv_sems.at[mask - 1],
            device_id=peer, device_id_type=pl.DeviceIdType.MESH))
    return out


def spread_start(src, land, *, per_peer, name):
    def body(src_ref, land_ref, send_sems, recv_sems, src_thru, land_thru, token):
        for cp in _spread_copies(src_ref, land_ref, send_sems, recv_sems, per_peer, arriving=False):
            cp.start()
        token[...] = jnp.zeros_like(token)

    return pl.pallas_call(
        body, name=name, in_specs=(HBM_SPEC, HBM_SPEC),
        out_shape=(pltpu.SemaphoreType.DMA((N_DEV - 1,)), pltpu.SemaphoreType.DMA((N_DEV - 1,)),
                   pltpu.HBM(src.shape, src.dtype), pltpu.HBM(land.shape, land.dtype), jax.ShapeDtypeStruct((8, 128), F32)),
        out_specs=(SEM_SPEC, SEM_SPEC, HBM_SPEC, HBM_SPEC, pl.BlockSpec(memory_space=pltpu.VMEM)),
        input_output_aliases={0: 2, 1: 3},
        compiler_params=pltpu.CompilerParams(has_side_effects=DATAFLOW_EFFECT),
    )(pltpu.with_memory_space_constraint(src, pltpu.HBM), pltpu.with_memory_space_constraint(land, pltpu.HBM))


def spread_wait(started, after, *, per_peer, name):
    send_sems, recv_sems, src_thru, land_thru, _ = started

    def body(src_ref, land_ref, send_sems, recv_sems, after_ref, src_dead, got_ref):
        for cp in _spread_copies(src_ref, land_ref, send_sems, recv_sems, per_peer, arriving=True):
            cp.wait_send()
            cp.wait_recv()

    return pl.pallas_call(
        body, name=name, in_specs=(HBM_SPEC, HBM_SPEC, SEM_SPEC, SEM_SPEC, pl.BlockSpec(memory_space=pl.ANY)),
        out_shape=(pltpu.HBM(src_thru.shape, src_thru.dtype), pltpu.HBM(land_thru.shape, land_thru.dtype)),
        out_specs=(HBM_SPEC, HBM_SPEC), input_output_aliases={0: 0, 1: 1},
        compiler_params=pltpu.CompilerParams(has_side_effects=DATAFLOW_EFFECT),
    )(src_thru, land_thru, send_sems, recv_sems, after)[1]


def landing_zone(own_block, my_index):
    zone = lax.empty((N_DEV,) + own_block.shape, own_block.dtype)
    return lax.dynamic_update_slice(zone, own_block[None], (my_index,) + (0,) * own_block.ndim)


def sum_partials(parts, *, name, tr):
    _, rows, lanes = parts.shape
    assert rows % tr == 0

    def body(p_ref, g_ref):
        g = p_ref[0].astype(F32)
        for k in range(1, N_DEV):
            g = g + p_ref[k].astype(F32)
        g_ref[...] = g

    return _pcall(
        body, name=name, grid=(rows // tr,),
        in_specs=[pl.BlockSpec((N_DEV, tr, lanes), lambda i: (0, i, 0))],
        out_specs=pl.BlockSpec((tr, lanes), lambda i: (i, 0)), out_shape=jax.ShapeDtypeStruct((rows, lanes), F32),
        blocks=[((N_DEV, tr, lanes), parts.dtype), ((tr, lanes), F32)],
    )(parts)


def adamw(g, w, m, v, *, name, tr):
    nl, k, n = w.shape
    tr = min(tr, k)
    assert k % tr == 0
    bc1 = 1.0 - ADAM_B1 ** ADAM_STEP
    bc2 = 1.0 - ADAM_B2 ** ADAM_STEP

    def body(g_ref, w_ref, m_ref, v_ref, d_ref, mo_ref, vo_ref):
        gv = g_ref[...]
        m_new = ADAM_B1 * m_ref[...] + (1.0 - ADAM_B1) * gv
        v_new = ADAM_B2 * v_ref[...] + (1.0 - ADAM_B2) * (gv * gv)
        mo_ref[...] = m_new
        vo_ref[...] = v_new
        d_ref[...] = -ADAM_LR * ((m_new / bc1) / (jnp.sqrt(v_new / bc2) + ADAM_EPS) + ADAM_WD * w_ref[...])

    spec = pl.BlockSpec((1, tr, n), lambda l, i: (l, i, 0))
    shape = jax.ShapeDtypeStruct(w.shape, F32)
    return _pcall(
        body, name=name, grid=(nl, k // tr), in_specs=[spec] * 4, out_specs=[spec] * 3, out_shape=[shape] * 3,
        blocks=[((1, tr, n), F32)] * 7,
    )(g, w, m, v)


def pack_shards(tensors, layer, part):
    rows = []
    for name, r, by_cols, _ in SUBBLOCKS[part]:
        shard = tensors[name][layer].astype(BF16)
        rows.append((shard.T if by_cols else shard).reshape(r, LANES))
    return jnp.concatenate(rows, axis=0)


def unpack_gathered(gathered, part):
    ws, r0 = {}, 0
    for name, r, by_cols, (k, n) in SUBBLOCKS[part]:
        ws[name] = gathered[:, r0:r0 + r, :].reshape((n, k) if by_cols else (k, n))
        r0 += r
    return ws


def pack_full_grads(grads, part):
    return jnp.concatenate([grads[name].reshape(N_DEV, r, LANES) for name, r, _, _ in SUBBLOCKS[part]], axis=1)


def unpack_summed(summed, part, like):
    out, r0 = {}, 0
    for name, r, by_cols, _ in SUBBLOCKS[part]:
        k, n = like[name].shape[1:]
        seg = summed[r0:r0 + r]
        out[name] = seg.reshape(n, k).T if by_cols else seg.reshape(k, n)
        r0 += r
    return out


def _row(v):
    return v.reshape(1, -1)


def ffn_fwd(x, h, w, pre, tag, next_gain):
    ga, gb, s = swiglu_fwd(h, w[pre + "_w_gate"], w[pre + "_w_up"], name=f"{tag}_gateup")
    out, h_next = matmul_res_norm(s, w[pre + "_w_down"], x, next_gain, scale=0.5, tm=512, name=f"{tag}_down")
    return out, h_next, (x, h, ga, gb, s)


def ffn_bwd_weights(dxb, saved, w, pre, tag):
    x, h, a, b, s = saved
    da, db = swiglu_bwd(dxb, w[pre + "_w_down"], a, b, scale=0.5, name=f"{tag}_dgateup")
    g_down = matmul([(s, dxb)], "tn", tm=1408, tn=1024, tk=2048, out_dtype=BF16, scale=0.5, name=f"{tag}_gdown")
    g_gate = matmul([(da, h)], "tn", tm=1408, tn=1024, tk=2048, out_dtype=BF16, name=f"{tag}_ggate")
    g_up = matmul([(db, h)], "tn", tm=1408, tn=1024, tk=2048, out_dtype=BF16, name=f"{tag}_gup")
    return {pre + "_w_gate": g_gate, pre + "_w_up": g_up, pre + "_w_down": g_down}, (da, db)


def ffn_bwd_input(dx, rest, saved, gain, w, pre, tag):
    da, db = rest
    x = saved[0]
    return matmul_rms_bwd([(da, w[pre + "_w_gate"]), (db, w[pre + "_w_up"])], x, gain, dx, tm=256, name=f"{tag}_dh")


def mixer_fwd(x, h, w, tables, tag, next_gain):
    proj = matmul([(h, w["w_in"])], "nt", tm=512, tn=1280, tk=1024, out_dtype=BF16, name=f"{tag}_in")
    qks, vs = rope_split(proj, tables, name=f"{tag}_rope")
    outs, lses = [], []
    for g in range(N_DIL_GROUPS):
        o, lse = dil_fwd(qks[g], vs[g], name=f"{tag}_dil{g}")
        outs.append(o)
        lses.append(lse)
    odil, lse = dil_merge(outs, lses, name=f"{tag}_merge")
    osb = sb_fwd(proj, name=f"{tag}_sb")
    y, u1, u2 = gate_fwd(odil, osb, w["w_proj_dil"], w["w_proj_sb"], proj, name=f"{tag}_gate")
    out, h_next = matmul_res_norm(y, w["w_out"], x, next_gain, scale=1.0, tm=512, name=f"{tag}_out")
    return out, h_next, (x, h, proj, qks, vs, odil, lse, osb, u1, u2, y)


def mixer_bwd_weights(dxb, saved, w, tables, tag):
    x, h, proj, qks, vs, odil, lse, osb, u1, u2, y = saved
    t = x.shape[0]
    g_out = matmul([(y, dxb)], "tn", tm=1024, tn=1024, tk=2048, out_dtype=BF16, name=f"{tag}_gout")
    du1, du2, dgate = gate_bwd(dxb, w["w_out"], u1, u2, proj, name=f"{tag}_dgate")
    g_pd = matmul([(du1, odil)], "tn", tm=1024, tn=256, tk=2048, out_dtype=BF16, name=f"{tag}_gpd")
    g_ps = matmul([(du2, osb)], "tn", tm=1024, tn=256, tk=2048, out_dtype=BF16, name=f"{tag}_gps")
    dodil = matmul([(du1, w["w_proj_dil"])], "nn", tm=512, tn=256, tk=1024, out_dtype=F32, name=f"{tag}_dodil")
    dosb = matmul([(du2, w["w_proj_sb"])], "nn", tm=512, tn=256, tk=1024, out_dtype=F32, name=f"{tag}_dosb")
    dsum, do_wide, lse_wide, dsum_wide = dil_bwd_prep(dodil, odil, lse, name=f"{tag}_dprep")
    dos = [dodil[None]] + list(do_wide)
    lss = [lse[None]] + list(lse_wide)
    dss = [dsum[None]] + list(dsum_wide)
    dqs, dks, dvs = [], [], []
    for g in range(N_DIL_GROUPS):
        dq, dk, dv = dil_bwd(qks[g], vs[g], dos[g], lss[g], dss[g], name=f"{tag}_ddil{g}")
        dqs.append(dq)
        dks.append(dk)
        dvs.append(dv)
    dqkv = rope_join(dqs, dks, dvs, tables, name=f"{tag}_drope")
    gtot = head_sums(dosb, osb, round_a=True, name=f"{tag}_gsum")
    dq_s, dk_s, dv_s = sb_bwd(proj, dosb, gtot, name=f"{tag}_dsb")
    dproj = jnp.concatenate([dqkv, dq_s.astype(BF16), dk_s.astype(BF16), dv_s.astype(BF16), dgate], axis=1)
    g_in = matmul([(dproj, h)], "tn", tm=1280, tn=1024, tk=2048, out_dtype=BF16, name=f"{tag}_gin")
    return {"w_in": g_in, "w_proj_dil": g_pd, "w_proj_sb": g_ps, "w_out": g_out}, dproj


def mixer_bwd_input(dx, dproj, saved, gain, w, tag):
    x = saved[0]
    return matmul_rms_bwd([(dproj, w["w_in"])], x, gain, dx, tm=256, name=f"{tag}_dh")


def kernel(x, norm_ffn1, ffn1_w_gate, ffn1_w_up, ffn1_w_down, norm_mix, w_in, w_proj_dil, w_proj_sb, w_out, norm_ffn2, ffn2_w_gate, ffn2_w_up, ffn2_w_down, norm_final, loss_target, m_norm_ffn1, m_ffn1_w_gate, m_ffn1_w_up, m_ffn1_w_down, m_norm_mix, m_w_in, m_w_proj_dil, m_w_proj_sb, m_w_out, m_norm_ffn2, m_ffn2_w_gate, m_ffn2_w_up, m_ffn2_w_down, m_norm_final, v_norm_ffn1, v_ffn1_w_gate, v_ffn1_w_up, v_ffn1_w_down, v_norm_mix, v_w_in, v_w_proj_dil, v_w_proj_sb, v_w_out, v_norm_ffn2, v_ffn2_w_gate, v_ffn2_w_up, v_ffn2_w_down, v_norm_final):
    args = dict(locals())
    names = [name for name, _, _, _ in PACK_LAYOUT]
    t = x.shape[1]
    xs = x.reshape(t, D_MODEL)
    target = loss_target.reshape(t, D_MODEL)
    tables = rope_tables(t)

    my_index = 4 * lax.axis_index("x") + 2 * lax.axis_index("y") + lax.axis_index("c")
    parts = [(l, p) for l in range(2) for p in SUBBLOCKS]
    w_shards = {n: args[n] for n in names}
    gains = {n: args[n] for n in NORM_ROWS}

    in_flight, gathered, order_token = {}, {}, jnp.zeros((1, 1), F32)
    for l, p in parts:
        packed = pack_shards(w_shards, l, p) + order_token.astype(BF16)
        in_flight[(l, p)] = spread_start(packed, landing_zone(packed, my_index), per_peer=False, name=f"gather_start_l{l}_{p}")
        order_token = in_flight[(l, p)][4][0:1, 0:1]

    def weights_of(l, p, after):
        gathered[(l, p)] = spread_wait(in_flight[(l, p)], after, per_peer=False, name=f"gather_wait_l{l}_{p}")
        return unpack_gathered(gathered[(l, p)], p)

    saved, weights = {}, {}
    act = xs
    h = rms_fwd(xs, _row(gains["norm_ffn1"][0]) + order_token, name="l0_ffn1_norm")
    for i, (l, p) in enumerate(parts):
        weights[(l, p)] = weights_of(l, p, act)
        nl, np_ = parts[i + 1] if i + 1 < len(parts) else (None, None)
        next_gain = _row(gains["norm_" + np_][nl]) if np_ else None
        if p == "mix":
            act, h, saved[(l, p)] = mixer_fwd(act, h, weights[(l, p)], tables, f"l{l}_mix", next_gain)
        else:
            act, h, saved[(l, p)] = ffn_fwd(act, h, weights[(l, p)], p, f"l{l}_{p}", next_gain)
    dx, dxb, g_final, loss_part = final_loss(act, _row(norm_final), target, name="loss_head")

    gain_grads, sent = {}, {}
    order_token = jnp.zeros((1, 1), F32)
    for l, p in reversed(parts):
        w, sv = weights[(l, p)], saved[(l, p)]
        if p == "mix":
            gw, rest = mixer_bwd_weights(dxb, sv, w, tables, f"l{l}_mix")
        else:
            gw, rest = ffn_bwd_weights(dxb, sv, w, p, f"l{l}_{p}")
        slices = pack_full_grads(gw, p)
        own = lax.dynamic_index_in_dim(slices, my_index, 0, keepdims=False)
        sent[(l, p)] = spread_start(slices, landing_zone(own, my_index), per_peer=True, name=f"reduce_start_l{l}_{p}")
        gain = _row(gains["norm_" + p][l]) + sent[(l, p)][4][0:1, 0:1]
        if p == "mix":
            dx, dxb, gain_grads[("norm_mix", l)] = mixer_bwd_input(dx, rest, sv, gain, w, f"l{l}_mix")
        else:
            dx, dxb, gain_grads[("norm_" + p, l)] = ffn_bwd_input(dx, rest, sv, gain, w, p, f"l{l}_{p}")

    loss_row = jnp.pad(loss_part[:, :1], ((0, 0), (0, LANES - 1)))
    small = jnp.concatenate([gain_grads[(n, l)] for n in NORM_ROWS for l in range(2)] + [g_final, loss_row], axis=0)
    small_g = sum_partials(all_gather_rows(small, name="gather_gain_grads"), tr=8, name="sum_gain_grads")
    zero_row = jnp.zeros((1, LANES), F32)
    small_of = lambda pre: jnp.concatenate([args[pre + n] for n in NORM_ROWS] + [_row(args[pre + "norm_final"]), zero_row], axis=0)[None]
    small_out = adamw(small_g[None], small_of(""), small_of("m_"), small_of("v_"), tr=8, name="update_gains")
    small_all = [small_g] + [o[0] for o in small_out]

    grad_of = {}
    for l, p in reversed(parts):
        partials = spread_wait(sent[(l, p)], dx, per_peer=True, name=f"reduce_wait_l{l}_{p}")
        summed = sum_partials(partials, tr=UPDATE_ROWS[p], name=f"sum_l{l}_{p}")
        for n, g in unpack_summed(summed, p, w_shards).items():
            grad_of.setdefault(n, [None, None])[l] = g
    big_all = [{}, {}, {}, {}]
    for p in ("ffn2", "mix", "ffn1"):
        for n, _, _, _ in SUBBLOCKS[p]:
            g = jnp.stack(grad_of[n], axis=0)
            outs = adamw(g, args[n], args["m_" + n], args["v_" + n], tr=512, name=f"update_{n}")
            for kind, arr in enumerate([g] + list(outs)):
                big_all[kind][n] = arr

    def gains_of(s):
        out = {n: s[2 * i:2 * i + 2] for i, n in enumerate(NORM_ROWS)}
        out["norm_final"] = s[6]
        return out

    order = ["norm_ffn1", "ffn1_w_gate", "ffn1_w_up", "ffn1_w_down", "norm_mix", "w_in", "w_proj_dil", "w_proj_sb", "w_out",
             "norm_ffn2", "ffn2_w_gate", "ffn2_w_up", "ffn2_w_down", "norm_final"]
    results = []
    for kind in range(4):
        both = {**big_all[kind], **gains_of(small_all[kind])}
        results += [both[n] for n in order]
    loss = small_g[7, 0]
    return (loss, dx.reshape(1, t, D_MODEL), *results)
```

```python
import functools

import jax
import jax.numpy as jnp
from jax import lax
from jax.experimental import pallas as pl
from jax.experimental.pallas import tpu as pltpu

F32 = jnp.float32
BF16 = jnp.bfloat16

D_MODEL = 1024
HEAD_DIM = 64
GROUP_W = 256
D_IN = 5120
N_DIL_GROUPS = 3
DIL_SPAN = 128
DILATIONS = (1, 4, 16)
ROPE_THETA = 500000.0
ROPE_DIM = 16
RMS_EPS = 1e-6
ATT_SCALE = HEAD_DIM ** -0.5
QS_BLK, KS_BLK, VS_BLK = 9, 10, 11
GATE_DIL_BLK, GATE_SB_BLK = 3, 4

ADAM_LR, ADAM_B1, ADAM_B2, ADAM_EPS, ADAM_WD, ADAM_STEP = 0.001, 0.9, 0.999, 1e-08, 0.01, 10

N_DEV = 8
LANES = 1024
VMEM_PHYSICAL_V7X = 64 << 20
VMEM_TEMP_HEADROOM = 20 << 20

PACK_LAYOUT = (
    ("ffn1_w_gate", 352, True, (1024, 2816)),
    ("ffn1_w_up", 352, True, (1024, 2816)),
    ("ffn1_w_down", 352, False, (2816, 1024)),
    ("w_in", 640, True, (1024, 5120)),
    ("w_proj_dil", 32, True, (256, 1024)),
    ("w_proj_sb", 32, True, (256, 1024)),
    ("w_out", 128, False, (1024, 1024)),
    ("ffn2_w_gate", 352, True, (1024, 2816)),
    ("ffn2_w_up", 352, True, (1024, 2816)),
    ("ffn2_w_down", 352, False, (2816, 1024)),
)
SUBBLOCKS = {"ffn1": PACK_LAYOUT[0:3], "mix": PACK_LAYOUT[3:7], "ffn2": PACK_LAYOUT[7:10]}
UPDATE_ROWS = {"ffn1": 352, "mix": 416, "ffn2": 352}
NORM_ROWS = ("norm_ffn1", "norm_mix", "norm_ffn2")


def _nbytes(shape, dtype):
    n = 1
    for s in shape:
        n *= s
    return n * jnp.dtype(dtype).itemsize


def _pcall(body, *, name, grid, in_specs, out_specs, out_shape, blocks, scratch_shapes=(), scratch_bytes=0):
    need = 2 * sum(_nbytes(s, d) for s, d in blocks) + scratch_bytes + VMEM_TEMP_HEADROOM
    limit = min(need, VMEM_PHYSICAL_V7X - (4 << 20))
    in_hbm = lambda s: pltpu.HBM(s.shape, s.dtype)
    out_shape = [in_hbm(s) for s in out_shape] if isinstance(out_shape, (list, tuple)) else in_hbm(out_shape)
    call = pl.pallas_call(
        body, name=name, grid=grid, in_specs=in_specs, out_specs=out_specs, out_shape=out_shape,
        scratch_shapes=scratch_shapes,
        compiler_params=pltpu.CompilerParams(vmem_limit_bytes=limit),
    )
    return lambda *args: call(*[pltpu.with_memory_space_constraint(a, pltpu.HBM) for a in args])


def _dot(a, b, form):
    dn = {"nn": (((1,), (0,)), ((), ())), "nt": (((1,), (1,)), ((), ())), "tn": (((0,), (0,)), ((), ()))}[form]
    return lax.dot_general(a.astype(BF16), b.astype(BF16), dn, preferred_element_type=F32)


def _sigmoid(x):
    return 1.0 / (1.0 + jnp.exp(-x))


def matmul(pairs, form, *, tm, tn, tk, out_dtype, name, scale=1.0, res=None):
    a0, b0 = pairs[0]
    if form == "tn":
        kdim, m = a0.shape
        n = b0.shape[1]
    else:
        m, kdim = a0.shape
        n = b0.shape[1] if form == "nn" else b0.shape[0]
    tm, tn, tk = min(tm, m), min(tn, n), min(tk, kdim)
    assert m % tm == 0 and n % tn == 0 and kdim % tk == 0, (name, m, n, kdim, tm, tn, tk)
    nk = kdim // tk
    npairs = len(pairs)

    if form == "tn":
        a_blk, a_map = (tk, tm), (lambda j, i, k: (k, i))
    else:
        a_blk, a_map = (tm, tk), (lambda j, i, k: (i, k))
    if form == "nt":
        b_blk, b_map = (tn, tk), (lambda j, i, k: (j, k))
    else:
        b_blk, b_map = (tk, tn), (lambda j, i, k: (k, j))
    o_map = lambda j, i, k: (i, j)

    def body(*refs):
        ab = refs[:2 * npairs]
        rest = refs[2 * npairs:]
        if res is not None:
            r_ref, o_ref = rest[0], rest[1]
            rest = rest[2:]
        else:
            r_ref, o_ref = None, rest[0]
            rest = rest[1:]

        def partial_sum():
            p = _dot(ab[0][...], ab[1][...], form)
            for q in range(1, npairs):
                p = p + _dot(ab[2 * q][...], ab[2 * q + 1][...], form)
            return p

        def finish(acc):
            out = acc * scale if scale != 1.0 else acc
            if r_ref is not None:
                out = r_ref[...] + out
            o_ref[...] = out.astype(out_dtype)

        if nk == 1:
            finish(partial_sum())
        else:
            acc_ref = rest[0]
            k = pl.program_id(2)

            @pl.when(k == 0)
            def _():
                acc_ref[...] = partial_sum()

            @pl.when(k > 0)
            def _():
                acc_ref[...] += partial_sum()

            @pl.when(k == nk - 1)
            def _():
                finish(acc_ref[...])

    in_specs, args, blocks = [], [], []
    for a, b in pairs:
        in_specs += [pl.BlockSpec(a_blk, a_map), pl.BlockSpec(b_blk, b_map)]
        args += [a, b]
        blocks += [(a_blk, a.dtype), (b_blk, b.dtype)]
    if res is not None:
        in_specs.append(pl.BlockSpec((tm, tn), o_map))
        args.append(res)
        blocks.append(((tm, tn), res.dtype))
    blocks.append(((tm, tn), out_dtype))
    scratch = [pltpu.VMEM((tm, tn), F32)] if nk > 1 else []
    return _pcall(
        body, name=name, grid=(n // tn, m // tm, nk), in_specs=in_specs,
        out_specs=pl.BlockSpec((tm, tn), o_map), out_shape=jax.ShapeDtypeStruct((m, n), out_dtype),
        blocks=blocks, scratch_shapes=scratch, scratch_bytes=(tm * tn * 4 if nk > 1 else 0),
    )(*args)


def swiglu_fwd(h, wg_t, wu_t, *, name, tm=512, tn=1408):
    t, d = h.shape
    f = wg_t.shape[0]
    tm, tn = min(tm, t), min(tn, f)

    def body(h_ref, wg_ref, wu_ref, ga_ref, gb_ref, s_ref):
        hh = h_ref[...]
        a = _dot(hh, wg_ref[...], "nt")
        b = _dot(hh, wu_ref[...], "nt")
        sg = _sigmoid(a)
        silu = a * sg
        ga_ref[...] = (b * (sg * (1.0 + a * (1.0 - sg)))).astype(BF16)
        gb_ref[...] = silu.astype(BF16)
        s_ref[...] = (silu * b).astype(BF16)

    w_spec = pl.BlockSpec((tn, d), lambda j, i: (j, 0))
    o_spec = pl.BlockSpec((tm, tn), lambda j, i: (i, j))
    o_shape = jax.ShapeDtypeStruct((t, f), BF16)
    return _pcall(
        body, name=name, grid=(f // tn, t // tm),
        in_specs=[pl.BlockSpec((tm, d), lambda j, i: (i, 0)), w_spec, w_spec],
        out_specs=[o_spec, o_spec, o_spec], out_shape=[o_shape, o_shape, o_shape],
        blocks=[((tm, d), BF16), ((tn, d), BF16), ((tn, d), BF16)] + [((tm, tn), BF16)] * 3,
    )(h, wg_t, wu_t)


def swiglu_bwd(dyb, wd, ga, gb, *, name, scale, tm=512, tn=1408):
    t, d = dyb.shape
    f = wd.shape[0]
    tm, tn = min(tm, t), min(tn, f)

    def body(dy_ref, wd_ref, ga_ref, gb_ref, da_ref, db_ref):
        ds = _dot(dy_ref[...], wd_ref[...], "nt") * scale
        da_ref[...] = (ds * ga_ref[...].astype(F32)).astype(BF16)
        db_ref[...] = (ds * gb_ref[...].astype(F32)).astype(BF16)

    o_spec = pl.BlockSpec((tm, tn), lambda j, i: (i, j))
    o_shape = jax.ShapeDtypeStruct((t, f), BF16)
    return _pcall(
        body, name=name, grid=(f // tn, t // tm),
        in_specs=[pl.BlockSpec((tm, d), lambda j, i: (i, 0)), pl.BlockSpec((tn, d), lambda j, i: (j, 0)), o_spec, o_spec],
        out_specs=[o_spec, o_spec], out_shape=[o_shape, o_shape],
        blocks=[((tm, d), BF16), ((tn, d), BF16)] + [((tm, tn), BF16)] * 4,
    )(dyb, wd, ga, gb)


def gate_fwd(odil, osb, wpd_t, wps_t, proj, *, name, tm=512):
    t = odil.shape[0]
    tm = min(tm, t)

    def body(od_ref, os_ref, wpd_ref, wps_ref, g1_ref, g2_ref, y_ref, u1_ref, u2_ref):
        u1 = _dot(od_ref[...], wpd_ref[...], "nt")
        u2 = _dot(os_ref[...], wps_ref[...], "nt")
        y = _sigmoid(g1_ref[...].astype(F32)) * u1 + _sigmoid(g2_ref[...].astype(F32)) * u2
        y_ref[...] = y.astype(BF16)
        u1_ref[...] = u1.astype(BF16)
        u2_ref[...] = u2.astype(BF16)

    o_spec = pl.BlockSpec((tm, D_MODEL), lambda i: (i, 0))
    w_spec = pl.BlockSpec((D_MODEL, GROUP_W), lambda i: (0, 0))
    a_spec = pl.BlockSpec((tm, GROUP_W), lambda i: (i, 0))
    o_shape = jax.ShapeDtypeStruct((t, D_MODEL), BF16)
    return _pcall(
        body, name=name, grid=(t // tm,),
        in_specs=[a_spec, a_spec, w_spec, w_spec,
                  pl.BlockSpec((tm, D_MODEL), lambda i: (i, GATE_DIL_BLK)),
                  pl.BlockSpec((tm, D_MODEL), lambda i: (i, GATE_SB_BLK))],
        out_specs=[o_spec, o_spec, o_spec], out_shape=[o_shape, o_shape, o_shape],
        blocks=[((tm, GROUP_W), F32)] * 2 + [((D_MODEL, GROUP_W), BF16)] * 2 + [((tm, D_MODEL), BF16)] * 5,
    )(odil, osb, wpd_t, wps_t, proj, proj)


def gate_bwd(dxb, wout, u1, u2, proj, *, name, tm=512):
    t = dxb.shape[0]
    tm = min(tm, t)

    def body(dx_ref, w_ref, u1_ref, u2_ref, g1_ref, g2_ref, du1_ref, du2_ref, dg_ref):
        dy = _dot(dx_ref[...], w_ref[...], "nt")
        s1 = _sigmoid(g1_ref[...].astype(F32))
        s2 = _sigmoid(g2_ref[...].astype(F32))
        du1_ref[...] = (dy * s1).astype(BF16)
        du2_ref[...] = (dy * s2).astype(BF16)
        dg_ref[:, :D_MODEL] = (dy * u1_ref[...].astype(F32) * s1 * (1.0 - s1)).astype(BF16)
        dg_ref[:, D_MODEL:] = (dy * u2_ref[...].astype(F32) * s2 * (1.0 - s2)).astype(BF16)

    o_spec = pl.BlockSpec((tm, D_MODEL), lambda i: (i, 0))
    o_shape = jax.ShapeDtypeStruct((t, D_MODEL), BF16)
    return _pcall(
        body, name=name, grid=(t // tm,),
        in_specs=[o_spec, pl.BlockSpec((D_MODEL, D_MODEL), lambda i: (0, 0)), o_spec, o_spec,
                  pl.BlockSpec((tm, D_MODEL), lambda i: (i, GATE_DIL_BLK)),
                  pl.BlockSpec((tm, D_MODEL), lambda i: (i, GATE_SB_BLK))],
        out_specs=[o_spec, o_spec, pl.BlockSpec((tm, 2 * D_MODEL), lambda i: (i, 0))],
        out_shape=[o_shape, o_shape, jax.ShapeDtypeStruct((t, 2 * D_MODEL), BF16)],
        blocks=[((tm, D_MODEL), BF16)] * 9 + [((D_MODEL, D_MODEL), BF16)],
    )(dxb, wout, u1, u2, proj, proj)


def rms_fwd(x, gain, *, name, tm=512):
    t, d = x.shape
    tm = min(tm, t)

    def body(x_ref, g_ref, h_ref):
        xv = x_ref[...]
        rstd = lax.rsqrt(jnp.mean(xv * xv, axis=1, keepdims=True) + RMS_EPS)
        h_ref[...] = (xv * rstd * g_ref[...]).astype(BF16)

    return _pcall(
        body, name=name, grid=(t // tm,),
        in_specs=[pl.BlockSpec((tm, d), lambda i: (i, 0)), pl.BlockSpec((1, d), lambda i: (0, 0))],
        out_specs=pl.BlockSpec((tm, d), lambda i: (i, 0)), out_shape=jax.ShapeDtypeStruct((t, d), BF16),
        blocks=[((tm, d), F32), ((tm, d), BF16)],
    )(x, gain)


def matmul_res_norm(a, b, res, next_gain, *, scale, tm, name):
    t, k = a.shape
    d = b.shape[1]
    tm = min(tm, t)
    with_norm = next_gain is not None

    def body(a_ref, b_ref, r_ref, *rest):
        out = r_ref[...] + _dot(a_ref[...], b_ref[...], "nn") * scale
        if with_norm:
            g_ref, o_ref, h_ref = rest
            rstd = lax.rsqrt(jnp.mean(out * out, axis=1, keepdims=True) + RMS_EPS)
            h_ref[...] = (out * rstd * g_ref[...]).astype(BF16)
        else:
            o_ref, = rest
        o_ref[...] = out

    row = pl.BlockSpec((tm, d), lambda i: (i, 0))
    in_specs = [pl.BlockSpec((tm, k), lambda i: (i, 0)), pl.BlockSpec((k, d), lambda i: (0, 0)), row]
    args = [a, b, res]
    out_specs, out_shape = [row], [jax.ShapeDtypeStruct((t, d), F32)]
    if with_norm:
        in_specs.append(pl.BlockSpec((1, d), lambda i: (0, 0)))
        args.append(next_gain)
        out_specs.append(row)
        out_shape.append(jax.ShapeDtypeStruct((t, d), BF16))
    outs = _pcall(
        body, name=name, grid=(t // tm,), in_specs=in_specs, out_specs=out_specs, out_shape=out_shape,
        blocks=[((tm, k), a.dtype), ((k, d), b.dtype), ((tm, d), F32), ((tm, d), F32), ((tm, d), BF16)],
    )(*args)
    return (outs[0], outs[1]) if with_norm else (outs[0], None)


def _rms_bwd_rows(dhv, xv, g, drv):
    rstd = lax.rsqrt(jnp.mean(xv * xv, axis=1, keepdims=True) + RMS_EPS)
    xh = xv * rstd
    dxh = dhv * g
    dx = drv + rstd * (dxh - xh * jnp.mean(dxh * xh, axis=1, keepdims=True))
    return dx, jnp.sum(dhv * xh, axis=0, keepdims=True)


def matmul_rms_bwd(pairs, x, gain, dres, *, tm, name):
    t, d = x.shape
    tm = min(tm, t)
    npairs = len(pairs)

    def body(*refs):
        ab = refs[:2 * npairs]
        x_ref, g_ref, dr_ref, dx_ref, dxb_ref, dg_ref = refs[2 * npairs:]
        dh = _dot(ab[0][...], ab[1][...], "nn")
        for q in range(1, npairs):
            dh = dh + _dot(ab[2 * q][...], ab[2 * q + 1][...], "nn")
        dx, part = _rms_bwd_rows(dh, x_ref[...], g_ref[...], dr_ref[...])
        dx_ref[...] = dx
        dxb_ref[...] = dx.astype(BF16)

        @pl.when(pl.program_id(0) == 0)
        def _():
            dg_ref[...] = part

        @pl.when(pl.program_id(0) > 0)
        def _():
            dg_ref[...] += part

    in_specs, args, blocks = [], [], []
    for a, b in pairs:
        k = a.shape[1]
        in_specs += [pl.BlockSpec((tm, k), lambda i: (i, 0)), pl.BlockSpec((k, d), lambda i: (0, 0))]
        args += [a, b]
        blocks += [((tm, k), a.dtype), ((k, d), b.dtype)]
    row = pl.BlockSpec((tm, d), lambda i: (i, 0))
    vec = pl.BlockSpec((1, d), lambda i: (0, 0))
    return _pcall(
        body, name=name, grid=(t // tm,), in_specs=in_specs + [row, vec, row], out_specs=[row, row, vec],
        out_shape=[jax.ShapeDtypeStruct((t, d), F32), jax.ShapeDtypeStruct((t, d), BF16), jax.ShapeDtypeStruct((1, d), F32)],
        blocks=blocks + [((tm, d), F32)] * 3 + [((tm, d), BF16)],
    )(*args, x, gain, dres)


def final_loss(x, gain, target, *, name, tm=512):
    t, d = x.shape
    tm = min(tm, t)

    def body(x_ref, g_ref, t_ref, dx_ref, dxb_ref, dg_ref, loss_ref):
        xv = x_ref[...]
        g = g_ref[...]
        rstd = lax.rsqrt(jnp.mean(xv * xv, axis=1, keepdims=True) + RMS_EPS)
        xh = xv * rstd
        err = xh * g - t_ref[...]
        dy = err * (1.0 / d)
        dxh = dy * g
        dx = rstd * (dxh - xh * jnp.mean(dxh * xh, axis=1, keepdims=True))
        dx_ref[...] = dx
        dxb_ref[...] = dx.astype(BF16)
        part = jnp.sum(dy * xh, axis=0, keepdims=True)
        sq = jnp.sum(jnp.sum(err * err, axis=1, keepdims=True), axis=0, keepdims=True) * (0.5 / d)
        lpart = jnp.broadcast_to(sq, (1, 128))

        @pl.when(pl.program_id(0) == 0)
        def _():
            dg_ref[...] = part
            loss_ref[...] = lpart

        @pl.when(pl.program_id(0) > 0)
        def _():
            dg_ref[...] += part
            loss_ref[...] += lpart

    row = pl.BlockSpec((tm, d), lambda i: (i, 0))
    vec = pl.BlockSpec((1, d), lambda i: (0, 0))
    return _pcall(
        body, name=name, grid=(t // tm,), in_specs=[row, vec, row],
        out_specs=[row, row, vec, pl.BlockSpec((1, 128), lambda i: (0, 0))],
        out_shape=[jax.ShapeDtypeStruct((t, d), F32), jax.ShapeDtypeStruct((t, d), BF16),
                   jax.ShapeDtypeStruct((1, d), F32), jax.ShapeDtypeStruct((1, 128), F32)],
        blocks=[((tm, d), F32)] * 3 + [((tm, d), BF16)],
    )(x, gain, target)


def rope_tables(t):
    pos = jnp.arange(t, dtype=F32)
    inv_freq = ROPE_THETA ** (-jnp.arange(0, ROPE_DIM, 2, dtype=F32) / ROPE_DIM)
    ang = pos[:, None] * inv_freq[None, :]
    cos, sin = jnp.cos(ang), jnp.sin(ang)
    half = ROPE_DIM // 2
    pad = HEAD_DIM - ROPE_DIM
    one_head = lambda lo, hi, fill: jnp.concatenate([lo, hi, jnp.full((t, pad), fill, F32)], axis=1)
    zeros = jnp.zeros((t, half), F32)
    c = one_head(cos, cos, 1.0)
    sa = one_head(-sin, zeros, 0.0)
    sb = one_head(zeros, sin, 0.0)
    two = lambda a: jnp.concatenate([a, a], axis=1)
    return two(c), two(sa), two(sb)


def _rotate(xv, cv, sav, sbv):
    halves = []
    for half in range(2):
        x = xv[:, 128 * half:128 * (half + 1)]
        halves.append(x * cv + pltpu.roll(x, 120, 1) * sav + pltpu.roll(x, 8, 1) * sbv)
    return jnp.concatenate(halves, axis=1)


STAGE_CHUNKS = 4


def _stage(tm):
    return dict(scratch_shapes=[pltpu.VMEM((STAGE_CHUNKS, tm, 128), F32)], scratch_bytes=STAGE_CHUNKS * tm * 128 * 4)


def _split_residues(stage_ref, val, out_ref, d, col, dtype):
    rows, width = val.shape
    if d == 1:
        out_ref[0, :, col:col + width] = val.astype(dtype)
        return
    chunks = width // 128
    for c in range(chunks):
        stage_ref[c] = val[:, 128 * c:128 * (c + 1)]
    for r in range(d):
        for c in range(chunks):
            out_ref[r, :, col + 128 * c:col + 128 * (c + 1)] = stage_ref[c, pl.ds(r, rows // d, stride=d), :].astype(dtype)


def _join_residues(stage_ref, in_ref, d, col=0, width=GROUP_W):
    if d == 1:
        return in_ref[0, :, col:col + width].astype(F32)
    rows = in_ref.shape[1] * d
    chunks = width // 128
    for r in range(d):
        for c in range(chunks):
            stage_ref[c, pl.ds(r, rows // d, stride=d), :] = in_ref[r, :, col + 128 * c:col + 128 * (c + 1)].astype(F32)
    return jnp.concatenate([stage_ref[c] for c in range(chunks)], axis=1)


def rope_split(proj, tables, *, name, tm=512):
    c, sa, sb = tables
    t = c.shape[0]
    tm = min(tm, t)

    def body(*refs):
        pieces = refs[0:9]
        c_ref, sa_ref, sb_ref = refs[9:12]
        qk_out, v_out = refs[12:15], refs[15:18]
        stage = refs[18]
        cv, sav, sbv = c_ref[...], sa_ref[...], sb_ref[...]
        for g, d in enumerate(DILATIONS):
            for kind in range(3):
                xv = pieces[3 * kind + g][...].astype(F32)
                if kind < 2:
                    _split_residues(stage, _rotate(xv, cv, sav, sbv), qk_out[g], d, GROUP_W * kind, BF16)
                else:
                    _split_residues(stage, xv, v_out[g], d, 0, BF16)

    tab = pl.BlockSpec((tm, 128), lambda i: (i, 0))
    in_specs = [pl.BlockSpec((tm, GROUP_W), functools.partial(lambda i, cb: (i, cb), cb=cb)) for cb in range(9)]
    out_specs = ([pl.BlockSpec((d, tm // d, 2 * GROUP_W), lambda i: (0, i, 0)) for d in DILATIONS]
                 + [pl.BlockSpec((d, tm // d, GROUP_W), lambda i: (0, i, 0)) for d in DILATIONS])
    out_shape = ([jax.ShapeDtypeStruct((d, t // d, 2 * GROUP_W), BF16) for d in DILATIONS]
                 + [jax.ShapeDtypeStruct((d, t // d, GROUP_W), BF16) for d in DILATIONS])
    outs = _pcall(
        body, name=name, grid=(t // tm,), in_specs=in_specs + [tab, tab, tab], out_specs=out_specs, out_shape=out_shape,
        blocks=[((tm, GROUP_W), BF16)] * 18 + [((tm, 128), F32)] * 3,
        **_stage(tm),
    )(*([proj] * 9), c, sa, sb)
    return outs[0:3], outs[3:6]


def rope_join(dqs, dks, dvs, tables, *, name, tm=512):
    c, sa, sb = tables
    t = c.shape[0]
    tm = min(tm, t)

    def body(*refs):
        pieces = refs[0:9]
        c_ref, sa_ref, sb_ref = refs[9:12]
        o_ref, stage = refs[12], refs[13]
        cv, sav, sbv = c_ref[...], -sa_ref[...], -sb_ref[...]
        for kind in range(3):
            for g, d in enumerate(DILATIONS):
                xv = _join_residues(stage, pieces[3 * kind + g], d)
                if kind < 2:
                    xv = _rotate(xv, cv, sav, sbv)
                col = GROUP_W * (3 * kind + g)
                o_ref[:, col:col + GROUP_W] = xv.astype(BF16)

    tab = pl.BlockSpec((tm, 128), lambda i: (i, 0))
    in_specs = [pl.BlockSpec((d, tm // d, GROUP_W), lambda i: (0, i, 0)) for _ in range(3) for d in DILATIONS]
    return _pcall(
        body, name=name, grid=(t // tm,), in_specs=in_specs + [tab, tab, tab],
        out_specs=pl.BlockSpec((tm, 9 * GROUP_W), lambda i: (i, 0)), out_shape=jax.ShapeDtypeStruct((t, 9 * GROUP_W), BF16),
        blocks=[((tm, GROUP_W), F32)] * 9 + [((tm, 128), F32)] * 3 + [((tm, 9 * GROUP_W), BF16)],
        **_stage(tm),
    )(*dqs, *dks, *dvs, c, sa, sb)


def _head_mask(h):
    lane = lax.broadcasted_iota(jnp.int32, (1, GROUP_W), 1)
    return (lane // HEAD_DIM) == h


def _band_masks(heads):
    ri = lax.broadcasted_iota(jnp.int32, (heads * DIL_SPAN, DIL_SPAN), 0) % DIL_SPAN
    ci = lax.broadcasted_iota(jnp.int32, (heads * DIL_SPAN, DIL_SPAN), 1)
    return ci <= ri, ci >= ri


def dil_fwd(qk, v, *, name):
    d, nsub, _ = qk.shape
    nblk = nsub // DIL_SPAN

    def body(q_ref, kc_ref, kp_ref, vc_ref, vp_ref, o_ref, lse_ref):
        nb = pl.program_id(1)
        own, prev = _band_masks(1)
        prev = prev & (nb > 0)
        q, kc, kp, vc, vp = q_ref[0] * ATT_SCALE, kc_ref[0], kp_ref[0], vc_ref[0], vp_ref[0]
        o_acc = jnp.zeros((DIL_SPAN, GROUP_W), F32)
        for h in range(4):
            hm = _head_mask(h)
            qh = jnp.where(hm, q, jnp.zeros_like(q))
            sc = jnp.where(own, _dot(qh, kc, "nt"), -jnp.inf)
            sp = jnp.where(prev, _dot(qh, kp, "nt"), -jnp.inf)
            m = jnp.maximum(jnp.max(sc, axis=1, keepdims=True), jnp.max(sp, axis=1, keepdims=True))
            pc = jnp.exp(sc - m)
            pp = jnp.exp(sp - m)
            den = jnp.sum(pc, axis=1, keepdims=True) + jnp.sum(pp, axis=1, keepdims=True)
            oh = (_dot(pc, vc, "nn") + _dot(pp, vp, "nn")) / den
            o_acc = jnp.where(hm, oh, o_acc)
            lse_ref[0, :, 128 * h:128 * (h + 1)] = jnp.broadcast_to(m + jnp.log(den), (DIL_SPAN, 128))
        o_ref[0] = o_acc

    blk = (1, DIL_SPAN, GROUP_W)
    sblk = (1, DIL_SPAN, 512)
    prv = lambda nb: jnp.maximum(nb - 1, 0)
    return _pcall(
        body, name=name, grid=(d, nblk),
        in_specs=[pl.BlockSpec(blk, lambda r, nb: (r, nb, 0)),
                  pl.BlockSpec(blk, lambda r, nb: (r, nb, 1)),
                  pl.BlockSpec(blk, lambda r, nb: (r, prv(nb), 1)),
                  pl.BlockSpec(blk, lambda r, nb: (r, nb, 0)),
                  pl.BlockSpec(blk, lambda r, nb: (r, prv(nb), 0))],
        out_specs=[pl.BlockSpec(blk, lambda r, nb: (r, nb, 0)), pl.BlockSpec(sblk, lambda r, nb: (r, nb, 0))],
        out_shape=[jax.ShapeDtypeStruct((d, nsub, GROUP_W), F32), jax.ShapeDtypeStruct((d, nsub, 512), F32)],
        blocks=[(blk, BF16)] * 5 + [(blk, F32), (sblk, F32)],
    )(qk, qk, qk, v, v)


def dil_merge(outs, lses, *, name, tm=512):
    t = outs[0].shape[0] * outs[0].shape[1]
    tm = min(tm, t)

    def body(o0, o1, o2, l0, l1, l2, o_ref, lse_ref, stage):
        ls = [_join_residues(stage, l, d, 0, 512) for l, d in zip((l0, l1, l2), DILATIONS)]
        m = jnp.maximum(jnp.maximum(ls[0], ls[1]), ls[2])
        tot = m + jnp.log(jnp.exp(ls[0] - m) + jnp.exp(ls[1] - m) + jnp.exp(ls[2] - m))
        lse_ref[...] = tot
        lane = lax.broadcasted_iota(jnp.int32, (1, 128), 1)
        first = lane < HEAD_DIM
        acc = jnp.zeros((tm, GROUP_W), F32)
        for og, lg, d in zip((o0, o1, o2), ls, DILATIONS):
            w = jnp.exp(lg - tot)
            wide = jnp.concatenate([jnp.where(first, w[:, 0:128], w[:, 128:256]),
                                    jnp.where(first, w[:, 256:384], w[:, 384:512])], axis=1)
            acc = acc + wide * _join_residues(stage, og, d)
        o_ref[...] = acc

    o_in = [pl.BlockSpec((d, tm // d, GROUP_W), lambda i: (0, i, 0)) for d in DILATIONS]
    l_in = [pl.BlockSpec((d, tm // d, 512), lambda i: (0, i, 0)) for d in DILATIONS]
    return _pcall(
        body, name=name, grid=(t // tm,), in_specs=o_in + l_in,
        out_specs=[pl.BlockSpec((tm, GROUP_W), lambda i: (i, 0)), pl.BlockSpec((tm, 512), lambda i: (i, 0))],
        out_shape=[jax.ShapeDtypeStruct((t, GROUP_W), F32), jax.ShapeDtypeStruct((t, 512), F32)],
        blocks=[((tm, GROUP_W), F32)] * 4 + [((tm, 512), F32)] * 4,
        **_stage(tm),
    )(*outs, *lses)


def dil_bwd_prep(do, o, lse, *, name, tm=512):
    t = do.shape[0]
    tm = min(tm, t)
    wide = DILATIONS[1:]

    def body(do_ref, o_ref, lse_ref, ds_ref, *rest):
        do_out, lse_out, ds_out = rest[0:2], rest[2:4], rest[4:6]
        stage = rest[6]
        dov = do_ref[...]
        prod = dov * o_ref[...]
        for h in range(4):
            s = jnp.sum(jnp.where(_head_mask(h), prod, 0.0), axis=1, keepdims=True)
            ds_ref[:, 128 * h:128 * (h + 1)] = jnp.broadcast_to(s, (tm, 128))
        for i, d in enumerate(wide):
            _split_residues(stage, dov, do_out[i], d, 0, BF16)
            _split_residues(stage, lse_ref[...], lse_out[i], d, 0, F32)
            _split_residues(stage, ds_ref[...], ds_out[i], d, 0, F32)

    nat = lambda w: pl.BlockSpec((tm, w), lambda i: (i, 0))
    res = lambda d, w: pl.BlockSpec((d, tm // d, w), lambda i: (0, i, 0))
    shape = lambda d, w, dt: jax.ShapeDtypeStruct((d, t // d, w), dt)
    outs = _pcall(
        body, name=name, grid=(t // tm,), in_specs=[nat(GROUP_W), nat(GROUP_W), nat(512)],
        out_specs=[nat(512)] + [res(d, GROUP_W) for d in wide] + [res(d, 512) for d in wide] * 2,
        out_shape=([jax.ShapeDtypeStruct((t, 512), F32)] + [shape(d, GROUP_W, BF16) for d in wide]
                   + [shape(d, 512, F32) for d in wide] * 2),
        blocks=[((tm, GROUP_W), F32)] * 3 + [((tm, 512), F32)] * 6,
        **_stage(tm),
    )(do, o, lse)
    return outs[0], outs[1:3], outs[3:5], outs[5:7]


def head_sums(a, b, *, name, round_a=False, tm=512):
    t = a.shape[0]
    tm = min(tm, t)

    def body(a_ref, b_ref, o_ref):
        av = a_ref[...]
        if round_a:
            av = av.astype(BF16).astype(F32)
        prod = av * b_ref[...]
        for h in range(4):
            s = jnp.sum(jnp.where(_head_mask(h), prod, 0.0), axis=1, keepdims=True)
            o_ref[:, 128 * h:128 * (h + 1)] = jnp.broadcast_to(s, (tm, 128))

    spec = pl.BlockSpec((tm, GROUP_W), lambda i: (i, 0))
    return _pcall(
        body, name=name, grid=(t // tm,), in_specs=[spec, spec],
        out_specs=pl.BlockSpec((tm, 512), lambda i: (i, 0)), out_shape=jax.ShapeDtypeStruct((t, 512), F32),
        blocks=[((tm, GROUP_W), F32)] * 2 + [((tm, 512), F32)],
    )(a, b)


def dil_bwd(qk, v, do, lse, dsum, *, name):
    d, nsub, _ = qk.shape
    nblk = nsub // DIL_SPAN

    def body(qa_ref, qb_ref, kc_ref, kp_ref, vc_ref, vp_ref, doa_ref, dob_ref, la_ref, lb_ref, sa_ref, sb_ref,
             dq_ref, dk_ref, dv_ref):
        nb = pl.program_id(1)
        own, band = _band_masks(4)
        prev = band & (nb > 0)
        nxt = band & (nb < nblk - 1)
        kc, kp, vc, vp = kc_ref[0], kp_ref[0], vc_ref[0], vp_ref[0]
        qas, qbs = _stack_heads(qa_ref[0] * ATT_SCALE), _stack_heads(qb_ref[0] * ATT_SCALE)
        das, dbs = _stack_heads(doa_ref[0].astype(BF16)), _stack_heads(dob_ref[0].astype(BF16))
        stat = lambda ref: jnp.concatenate([ref[0, :, 128 * h:128 * (h + 1)] for h in range(4)], axis=0)
        la, lb, sa, sb = stat(la_ref), stat(lb_ref), stat(sa_ref), stat(sb_ref)

        def probs(qs, ds_, k, v, mask, l, s):
            p = jnp.where(mask, jnp.exp(_dot(qs, k, "nt") - l), 0.0)
            dsc = p * (_dot(ds_, v, "nt") - s)
            return p.astype(BF16), dsc.astype(BF16)

        p_cc, ds_cc = probs(qas, das, kc, vc, own, la, sa)
        _, ds_cp = probs(qas, das, kp, vp, prev, la, sa)
        p_nc, ds_nc = probs(qbs, dbs, kc, vc, nxt, lb, sb)
        dq_ref[0] = _unstack_heads(_dot(ds_cc, kc, "nn") + _dot(ds_cp, kp, "nn"), DIL_SPAN) * ATT_SCALE
        dk_ref[0] = _dot(ds_cc, qas, "tn") + _dot(ds_nc, qbs, "tn")
        dv_ref[0] = _dot(p_cc, das, "tn") + _dot(p_nc, dbs, "tn")

    blk = (1, DIL_SPAN, GROUP_W)
    sblk = (1, DIL_SPAN, 512)
    prv = lambda nb: jnp.maximum(nb - 1, 0)
    nxt_ = lambda nb: jnp.minimum(nb + 1, nblk - 1)
    cur_at = lambda c: pl.BlockSpec(blk, functools.partial(lambda r, nb, c: (r, nb, c), c=c))
    prv_at = lambda c: pl.BlockSpec(blk, functools.partial(lambda r, nb, c: (r, prv(nb), c), c=c))
    nxt_at = lambda c: pl.BlockSpec(blk, functools.partial(lambda r, nb, c: (r, nxt_(nb), c), c=c))
    s_cur = pl.BlockSpec(sblk, lambda r, nb: (r, nb, 0))
    s_nxt = pl.BlockSpec(sblk, lambda r, nb: (r, nxt_(nb), 0))
    o_spec = pl.BlockSpec(blk, lambda r, nb: (r, nb, 0))
    o_shape = jax.ShapeDtypeStruct((d, nsub, GROUP_W), F32)
    return _pcall(
        body, name=name, grid=(d, nblk),
        in_specs=[cur_at(0), nxt_at(0), cur_at(1), prv_at(1), cur_at(0), prv_at(0), cur_at(0), nxt_at(0),
                  s_cur, s_nxt, s_cur, s_nxt],
        out_specs=[o_spec, o_spec, o_spec], out_shape=[o_shape, o_shape, o_shape],
        blocks=[(blk, BF16)] * 6 + [(blk, F32)] * 5 + [(sblk, F32)] * 4,
    )(qk, qk, qk, qk, v, v, do, do, lse, lse, dsum, dsum)


def _tri_dot(x, b):
    hi = x.astype(BF16)
    lo = (x - hi.astype(F32)).astype(BF16)
    return _dot(hi, b, "nn") + _dot(lo, b, "nn")


SB_TILE = 256


def _stack_heads(a):
    return jnp.concatenate([jnp.where(_head_mask(h), a, jnp.zeros_like(a)) for h in range(4)], axis=0)


def _unstack_heads(acc, rows):
    out = acc[0:rows]
    for h in range(1, 4):
        out = jnp.where(_head_mask(h), acc[h * rows:(h + 1) * rows], out)
    return out


def _tri_masks(n):
    ri = lax.broadcasted_iota(jnp.int32, (n, n), 0)
    ci = lax.broadcasted_iota(jnp.int32, (n, n), 1)
    return (ri > ci).astype(BF16), (ri >= ci).astype(BF16)


def _sb_weights(qs, kt, after, c_keep, diagonal):
    z = _dot(qs, kt, "nt")
    lbeta = jnp.minimum(z, 0.0) - jnp.log(1.0 + jnp.exp(-jnp.abs(z)))
    lkeep = lbeta - z
    past = None
    if diagonal:
        n = SB_TILE
        past = lax.broadcasted_iota(jnp.int32, z.shape, 1) < lax.broadcasted_iota(jnp.int32, z.shape, 0) % n
        lkeep = jnp.where(past, lkeep, 0.0)
    w = jnp.exp(lbeta + _tri_dot(lkeep, after) + c_keep)
    if diagonal:
        w = jnp.where(past, w, 0.0)
    return z, past, lbeta, lkeep, w


def sb_fwd(proj, *, name):
    t = proj.shape[0]
    n = SB_TILE
    assert t % n == 0

    def body(q_ref, k_ref, v_ref, o_ref, acc_ref):
        qb = pl.program_id(0)
        qs = _stack_heads(q_ref[...] * ATT_SCALE)
        after, _ = _tri_masks(n)

        def tile(off, diagonal, c_keep):
            kt = k_ref[pl.ds(off, n), :]
            vt = v_ref[pl.ds(off, n), :]
            _, _, _, lkeep, w = _sb_weights(qs, kt, after, c_keep, diagonal)
            pv = _tri_dot(w, vt)
            if diagonal:
                acc_ref[...] = pv
            else:
                acc_ref[...] += pv
            return c_keep + jnp.sum(lkeep, axis=1, keepdims=True)

        c0 = tile(pl.multiple_of(qb * n, n), True, jnp.zeros((4 * n, 1), F32))
        lax.fori_loop(0, qb, lambda it, c: tile(pl.multiple_of((qb - 1 - it) * n, n), False, c), c0)
        o_ref[...] = _unstack_heads(acc_ref[...], n)

    full = lambda cb: pl.BlockSpec((t, GROUP_W), functools.partial(lambda i, cb: (0, cb), cb=cb))
    return _pcall(
        body, name=name, grid=(t // n,),
        in_specs=[pl.BlockSpec((n, GROUP_W), lambda i: (i, QS_BLK)), full(KS_BLK), full(VS_BLK)],
        out_specs=pl.BlockSpec((n, GROUP_W), lambda i: (i, 0)), out_shape=jax.ShapeDtypeStruct((t, GROUP_W), F32),
        blocks=[((n, GROUP_W), BF16), ((t, GROUP_W), BF16), ((t, GROUP_W), BF16), ((n, GROUP_W), F32)],
        scratch_shapes=[pltpu.VMEM((4 * n, GROUP_W), F32)], scratch_bytes=4 * n * GROUP_W * 4,
    )(proj, proj, proj)


def sb_bwd(proj, do, gtot, *, name):
    t = proj.shape[0]
    n = SB_TILE
    assert t % n == 0

    def body(q_ref, k_ref, v_ref, do_ref, gt_ref, dq_ref, dk_ref, dv_ref, acc_ref):
        qb = pl.program_id(0)

        @pl.when(qb == 0)
        def _():
            dk_ref[...] = jnp.zeros_like(dk_ref)
            dv_ref[...] = jnp.zeros_like(dv_ref)

        qs = _stack_heads(q_ref[...] * ATT_SCALE)
        dos = _stack_heads(do_ref[...].astype(BF16))
        gt = jnp.concatenate([jnp.max(gt_ref[:, 128 * h:128 * (h + 1)], axis=1, keepdims=True) for h in range(4)], axis=0)
        after, from_on = _tri_masks(n)

        def tile(off, diagonal, carry):
            c_keep, c_g = carry
            kt = k_ref[pl.ds(off, n), :]
            vt = v_ref[pl.ds(off, n), :]
            z, past, lbeta, lkeep, w = _sb_weights(qs, kt, after, c_keep, diagonal)
            gw = w * _dot(dos, vt, "nt")
            big_g = gt - (_tri_dot(gw, from_on) + c_g)
            dz = gw * jnp.exp(lbeta - z) - big_g * jnp.exp(lbeta)
            if diagonal:
                dz = jnp.where(past, dz, 0.0)
            dz = dz.astype(BF16)
            dk_ref[pl.ds(off, n), :] += _dot(dz, qs, "tn")
            dv_ref[pl.ds(off, n), :] += _dot(w, dos, "tn")
            dq = _dot(dz, kt, "nn")
            if diagonal:
                acc_ref[...] = dq
            else:
                acc_ref[...] += dq
            return c_keep + jnp.sum(lkeep, axis=1, keepdims=True), c_g + jnp.sum(gw, axis=1, keepdims=True)

        zero_col = jnp.zeros((4 * n, 1), F32)
        c0 = tile(pl.multiple_of(qb * n, n), True, (zero_col, zero_col))
        lax.fori_loop(0, qb, lambda it, c: tile(pl.multiple_of((qb - 1 - it) * n, n), False, c), c0)
        dq_ref[...] = _unstack_heads(acc_ref[...], n) * ATT_SCALE

    full = lambda cb: pl.BlockSpec((t, GROUP_W), functools.partial(lambda i, cb: (0, cb), cb=cb))
    whole = pl.BlockSpec((t, GROUP_W), lambda i: (0, 0))
    rowblk = pl.BlockSpec((n, GROUP_W), lambda i: (i, 0))
    shape = jax.ShapeDtypeStruct((t, GROUP_W), F32)
    return _pcall(
        body, name=name, grid=(t // n,),
        in_specs=[pl.BlockSpec((n, GROUP_W), lambda i: (i, QS_BLK)), full(KS_BLK), full(VS_BLK), rowblk,
                  pl.BlockSpec((n, 512), lambda i: (i, 0))],
        out_specs=[rowblk, whole, whole], out_shape=[shape, shape, shape],
        blocks=[((n, GROUP_W), BF16), ((t, GROUP_W), BF16), ((t, GROUP_W), BF16), ((n, GROUP_W), F32),
                ((n, 512), F32), ((n, GROUP_W), F32), ((t, GROUP_W), F32), ((t, GROUP_W), F32)],
        scratch_shapes=[pltpu.VMEM((4 * n, GROUP_W), F32)], scratch_bytes=4 * n * GROUP_W * 4,
    )(proj, proj, proj, do, gtot)


def _mesh_place():
    return lax.axis_index("x"), lax.axis_index("y"), lax.axis_index("c")


def _flip(place, mask):
    x, y, c = place
    return ((1 - x) if mask & 4 else x, (1 - y) if mask & 2 else y, (1 - c) if mask & 1 else c)


def _dev_index(place):
    x, y, c = place
    return 4 * x + 2 * y + c


HBM_SPEC = pl.BlockSpec(memory_space=pltpu.HBM)


def all_gather_rows(shard, *, name):
    rows, lanes = shard.shape

    def body(x_ref, out_ref, send_sems, recv_sems, local_sem):
        me = _mesh_place()
        x, y, c = me
        sibling = _flip(me, 1)
        chips = [_flip(me, 4), _flip(me, 2), _flip(me, 6)]

        def copy(k, block, to, src=None):
            dst = out_ref.at[_dev_index(block)]
            return pltpu.make_async_remote_copy(
                src_ref=dst if src is None else src, dst_ref=dst, send_sem=send_sems.at[k], recv_sem=recv_sems.at[k],
                device_id=to, device_id_type=pl.DeviceIdType.MESH)

        mine = pltpu.make_async_copy(x_ref, out_ref.at[_dev_index(me)], local_sem)
        mine.start()
        first = [copy(0, me, sibling, src=x_ref)] + [copy(1 + j, me, chip, src=x_ref) for j, chip in enumerate(chips)]
        for cp in first:
            cp.start()
        passed = [copy(4 + j, chip, sibling) for j, chip in enumerate(chips)]
        for j, chip in enumerate(chips):
            copy(1 + j, chip, me).wait_recv()
            passed[j].start()
        copy(0, sibling, me).wait_recv()
        for j, chip in enumerate(chips):
            copy(4 + j, _flip(chip, 1), me).wait_recv()
        for cp in first + passed:
            cp.wait_send()
        mine.wait()

    return pl.pallas_call(
        body, name=name, in_specs=[HBM_SPEC], out_specs=HBM_SPEC,
        out_shape=jax.ShapeDtypeStruct((N_DEV, rows, lanes), shard.dtype),
        scratch_shapes=[pltpu.SemaphoreType.DMA((7,)), pltpu.SemaphoreType.DMA((7,)), pltpu.SemaphoreType.DMA],
    )(shard)


SEM_SPEC = pl.BlockSpec(memory_space=pltpu.SEMAPHORE)
DATAFLOW_EFFECT = pltpu.SideEffectType.DATAFLOW_SIDE_EFFECTING


ALL_PEERS = tuple(range(1, N_DEV))
CHIP_PEERS = (1, 4, 2, 6)
OTHER_CHIPS = (4, 2, 6)


def _spread_copies(src_ref, land_ref, send_sems, recv_sems, per_peer, masks, arriving):
    me = _mesh_place()
    out = []
    for i, mask in enumerate(masks):
        peer = _flip(me, mask)
        data_of = _dev_index(me) if arriving else _dev_index(peer)
        slot = _dev_index(peer) if arriving else _dev_index(me)
        out.append(pltpu.make_async_remote_copy(
            src_ref=src_ref.at[data_of] if per_peer else src_ref, dst_ref=land_ref.at[slot],
            send_sem=send_sems.at[i], recv_sem=recv_sems.at[i],
            device_id=peer, device_id_type=pl.DeviceIdType.MESH))
    return out


def _sem_pair(n):
    return pltpu.SemaphoreType.DMA((n,)), pltpu.SemaphoreType.DMA((n,))


def spread_start(src, land, *, per_peer, name, masks=ALL_PEERS):
    def body(src_ref, land_ref, send_sems, recv_sems, src_thru, land_thru, token):
        for cp in _spread_copies(src_ref, land_ref, send_sems, recv_sems, per_peer, masks, arriving=False):
            cp.start()
        token[...] = jnp.zeros_like(token)

    return pl.pallas_call(
        body, name=name, in_specs=(HBM_SPEC, HBM_SPEC),
        out_shape=(*_sem_pair(len(masks)), pltpu.HBM(src.shape, src.dtype), pltpu.HBM(land.shape, land.dtype),
                   jax.ShapeDtypeStruct((8, 128), F32)),
        out_specs=(SEM_SPEC, SEM_SPEC, HBM_SPEC, HBM_SPEC, pl.BlockSpec(memory_space=pltpu.VMEM)),
        input_output_aliases={0: 2, 1: 3},
        compiler_params=pltpu.CompilerParams(has_side_effects=DATAFLOW_EFFECT),
    )(pltpu.with_memory_space_constraint(src, pltpu.HBM), pltpu.with_memory_space_constraint(land, pltpu.HBM))


def spread_wait(started, after, *, per_peer, name, masks=ALL_PEERS):
    send_sems, recv_sems, src_thru, land_thru, _ = started

    def body(src_ref, land_ref, send_sems, recv_sems, after_ref, src_dead, got_ref):
        for cp in _spread_copies(src_ref, land_ref, send_sems, recv_sems, per_peer, masks, arriving=True):
            cp.wait_send()
            cp.wait_recv()

    return pl.pallas_call(
        body, name=name, in_specs=(HBM_SPEC, HBM_SPEC, SEM_SPEC, SEM_SPEC, pl.BlockSpec(memory_space=pl.ANY)),
        out_shape=(pltpu.HBM(src_thru.shape, src_thru.dtype), pltpu.HBM(land_thru.shape, land_thru.dtype)),
        out_specs=(HBM_SPEC, HBM_SPEC), input_output_aliases={0: 0, 1: 1},
        compiler_params=pltpu.CompilerParams(has_side_effects=DATAFLOW_EFFECT),
    )(src_thru, land_thru, send_sems, recv_sems, after)[1]


def _relay_copies(land_ref, send_sems, recv_sems, arriving):
    me = _mesh_place()
    sibling = _flip(me, 1)
    out = []
    for i, mask in enumerate(OTHER_CHIPS):
        slot = _dev_index(_flip(sibling if arriving else me, mask))
        out.append(pltpu.make_async_remote_copy(
            src_ref=land_ref.at[slot], dst_ref=land_ref.at[slot], send_sem=send_sems.at[i], recv_sem=recv_sems.at[i],
            device_id=sibling, device_id_type=pl.DeviceIdType.MESH))
    return out


def relay_start(land, *, name):
    def body(land_ref, send_sems, recv_sems, land_thru):
        for cp in _relay_copies(land_ref, send_sems, recv_sems, arriving=False):
            cp.start()

    return pl.pallas_call(
        body, name=name, in_specs=(HBM_SPEC,),
        out_shape=(*_sem_pair(len(OTHER_CHIPS)), pltpu.HBM(land.shape, land.dtype)),
        out_specs=(SEM_SPEC, SEM_SPEC, HBM_SPEC), input_output_aliases={0: 2},
        compiler_params=pltpu.CompilerParams(has_side_effects=DATAFLOW_EFFECT),
    )(pltpu.with_memory_space_constraint(land, pltpu.HBM))


def relay_wait(started, *, name):
    send_sems, recv_sems, land_thru = started

    def body(land_ref, send_sems, recv_sems, got_ref):
        for cp in _relay_copies(land_ref, send_sems, recv_sems, arriving=True):
            cp.wait_send()
            cp.wait_recv()

    return pl.pallas_call(
        body, name=name, in_specs=(HBM_SPEC, SEM_SPEC, SEM_SPEC),
        out_shape=pltpu.HBM(land_thru.shape, land_thru.dtype), out_specs=HBM_SPEC, input_output_aliases={0: 0},
        compiler_params=pltpu.CompilerParams(has_side_effects=DATAFLOW_EFFECT),
    )(land_thru, send_sems, recv_sems)


def landing_zone(own_block, my_index):
    zone = lax.empty((N_DEV,) + own_block.shape, own_block.dtype)
    return lax.dynamic_update_slice(zone, own_block[None], (my_index,) + (0,) * own_block.ndim)


def sum_partials(parts, *, name, tr):
    _, rows, lanes = parts.shape
    assert rows % tr == 0

    def body(p_ref, g_ref):
        g = p_ref[0].astype(F32)
        for k in range(1, N_DEV):
            g = g + p_ref[k].astype(F32)
        g_ref[...] = g

    return _pcall(
        body, name=name, grid=(rows // tr,),
        in_specs=[pl.BlockSpec((N_DEV, tr, lanes), lambda i: (0, i, 0))],
        out_specs=pl.BlockSpec((tr, lanes), lambda i: (i, 0)), out_shape=jax.ShapeDtypeStruct((rows, lanes), F32),
        blocks=[((N_DEV, tr, lanes), parts.dtype), ((tr, lanes), F32)],
    )(parts)


def adamw(g, w, m, v, *, name, tr):
    nl, k, n = w.shape
    tr = max(c for c in range(8, min(tr, k) + 1, 8) if k % c == 0)
    bc1 = 1.0 - ADAM_B1 ** ADAM_STEP
    bc2 = 1.0 - ADAM_B2 ** ADAM_STEP

    def body(g_ref, w_ref, m_ref, v_ref, d_ref, mo_ref, vo_ref):
        gv = g_ref[...]
        m_new = ADAM_B1 * m_ref[...] + (1.0 - ADAM_B1) * gv
        v_new = ADAM_B2 * v_ref[...] + (1.0 - ADAM_B2) * (gv * gv)
        mo_ref[...] = m_new
        vo_ref[...] = v_new
        d_ref[...] = -ADAM_LR * ((m_new / bc1) / (jnp.sqrt(v_new / bc2) + ADAM_EPS) + ADAM_WD * w_ref[...])

    spec = pl.BlockSpec((1, tr, n), lambda l, i: (l, i, 0))
    shape = jax.ShapeDtypeStruct(w.shape, F32)
    return _pcall(
        body, name=name, grid=(nl, k // tr), in_specs=[spec] * 4, out_specs=[spec] * 3, out_shape=[shape] * 3,
        blocks=[((1, tr, n), F32)] * 7,
    )(g, w, m, v)


def pack_shards(tensors, layer, part):
    rows = []
    for name, r, by_cols, _ in SUBBLOCKS[part]:
        shard = tensors[name][layer].astype(BF16)
        rows.append((shard.T if by_cols else shard).reshape(r, LANES))
    return jnp.concatenate(rows, axis=0)


def unpack_gathered(gathered, part):
    ws, r0 = {}, 0
    for name, r, by_cols, (k, n) in SUBBLOCKS[part]:
        ws[name] = gathered[:, r0:r0 + r, :].reshape((n, k) if by_cols else (k, n))
        r0 += r
    return ws


def pack_full_grads(grads, part):
    return jnp.concatenate([grads[name].reshape(N_DEV, r, LANES) for name, r, _, _ in SUBBLOCKS[part]], axis=1)


def unpack_summed(summed, part, like):
    out, r0 = {}, 0
    for name, r, by_cols, _ in SUBBLOCKS[part]:
        k, n = like[name].shape[1:]
        out[name] = summed[r0:r0 + r].reshape((n, k) if by_cols else (k, n))
        r0 += r
    return out


def _row(v):
    return v.reshape(1, -1)


def ffn_fwd(x, h, w, pre, tag, next_gain):
    ga, gb, s = swiglu_fwd(h, w[pre + "_w_gate"], w[pre + "_w_up"], name=f"{tag}_gateup")
    out, h_next = matmul_res_norm(s, w[pre + "_w_down"], x, next_gain, scale=0.5, tm=512, name=f"{tag}_down")
    return out, h_next, (x, h, ga, gb, s)


def ffn_bwd_weights(dxb, saved, w, pre, tag):
    x, h, a, b, s = saved
    da, db = swiglu_bwd(dxb, w[pre + "_w_down"], a, b, scale=0.5, name=f"{tag}_dgateup")
    g_down = matmul([(s, dxb)], "tn", tm=1408, tn=1024, tk=2048, out_dtype=BF16, scale=0.5, name=f"{tag}_gdown")
    g_gate = matmul([(da, h)], "tn", tm=1408, tn=1024, tk=2048, out_dtype=BF16, name=f"{tag}_ggate")
    g_up = matmul([(db, h)], "tn", tm=1408, tn=1024, tk=2048, out_dtype=BF16, name=f"{tag}_gup")
    return {pre + "_w_gate": g_gate, pre + "_w_up": g_up, pre + "_w_down": g_down}, (da, db)


def ffn_bwd_input(dx, rest, saved, gain, w, pre, tag):
    da, db = rest
    x = saved[0]
    return matmul_rms_bwd([(da, w[pre + "_w_gate"]), (db, w[pre + "_w_up"])], x, gain, dx, tm=256, name=f"{tag}_dh")


def mixer_fwd(x, h, w, tables, tag, next_gain):
    proj = matmul([(h, w["w_in"])], "nt", tm=512, tn=1280, tk=1024, out_dtype=BF16, name=f"{tag}_in")
    qks, vs = rope_split(proj, tables, name=f"{tag}_rope")
    outs, lses = [], []
    for g in range(N_DIL_GROUPS):
        o, lse = dil_fwd(qks[g], vs[g], name=f"{tag}_dil{g}")
        outs.append(o)
        lses.append(lse)
    odil, lse = dil_merge(outs, lses, name=f"{tag}_merge")
    osb = sb_fwd(proj, name=f"{tag}_sb")
    y, u1, u2 = gate_fwd(odil, osb, w["w_proj_dil"], w["w_proj_sb"], proj, name=f"{tag}_gate")
    out, h_next = matmul_res_norm(y, w["w_out"], x, next_gain, scale=1.0, tm=512, name=f"{tag}_out")
    return out, h_next, (x, h, proj, qks, vs, odil, lse, osb, u1, u2, y)


def mixer_bwd_weights(dxb, saved, w, tables, tag):
    x, h, proj, qks, vs, odil, lse, osb, u1, u2, y = saved
    t = x.shape[0]
    g_out = matmul([(y, dxb)], "tn", tm=1024, tn=1024, tk=2048, out_dtype=BF16, name=f"{tag}_gout")
    du1, du2, dgate = gate_bwd(dxb, w["w_out"], u1, u2, proj, name=f"{tag}_dgate")
    g_pd = matmul([(du1, odil)], "tn", tm=1024, tn=256, tk=2048, out_dtype=BF16, name=f"{tag}_gpd")
    g_ps = matmul([(du2, osb)], "tn", tm=1024, tn=256, tk=2048, out_dtype=BF16, name=f"{tag}_gps")
    dodil = matmul([(du1, w["w_proj_dil"])], "nn", tm=512, tn=256, tk=1024, out_dtype=F32, name=f"{tag}_dodil")
    dosb = matmul([(du2, w["w_proj_sb"])], "nn", tm=512, tn=256, tk=1024, out_dtype=F32, name=f"{tag}_dosb")
    dsum, do_wide, lse_wide, dsum_wide = dil_bwd_prep(dodil, odil, lse, name=f"{tag}_dprep")
    dos = [dodil[None]] + list(do_wide)
    lss = [lse[None]] + list(lse_wide)
    dss = [dsum[None]] + list(dsum_wide)
    dqs, dks, dvs = [], [], []
    for g in range(N_DIL_GROUPS):
        dq, dk, dv = dil_bwd(qks[g], vs[g], dos[g], lss[g], dss[g], name=f"{tag}_ddil{g}")
        dqs.append(dq)
        dks.append(dk)
        dvs.append(dv)
    dqkv = rope_join(dqs, dks, dvs, tables, name=f"{tag}_drope")
    gtot = head_sums(dosb, osb, round_a=True, name=f"{tag}_gsum")
    dq_s, dk_s, dv_s = sb_bwd(proj, dosb, gtot, name=f"{tag}_dsb")
    dproj = jnp.concatenate([dqkv, dq_s.astype(BF16), dk_s.astype(BF16), dv_s.astype(BF16), dgate], axis=1)
    g_in = matmul([(dproj, h)], "tn", tm=1280, tn=1024, tk=2048, out_dtype=BF16, name=f"{tag}_gin")
    return {"w_in": g_in, "w_proj_dil": g_pd, "w_proj_sb": g_ps, "w_out": g_out}, dproj


def mixer_bwd_input(dx, dproj, saved, gain, w, tag):
    x = saved[0]
    return matmul_rms_bwd([(dproj, w["w_in"])], x, gain, dx, tm=256, name=f"{tag}_dh")


def kernel(x, norm_ffn1, ffn1_w_gate, ffn1_w_up, ffn1_w_down, norm_mix, w_in, w_proj_dil, w_proj_sb, w_out, norm_ffn2, ffn2_w_gate, ffn2_w_up, ffn2_w_down, norm_final, loss_target, m_norm_ffn1, m_ffn1_w_gate, m_ffn1_w_up, m_ffn1_w_down, m_norm_mix, m_w_in, m_w_proj_dil, m_w_proj_sb, m_w_out, m_norm_ffn2, m_ffn2_w_gate, m_ffn2_w_up, m_ffn2_w_down, m_norm_final, v_norm_ffn1, v_ffn1_w_gate, v_ffn1_w_up, v_ffn1_w_down, v_norm_mix, v_w_in, v_w_proj_dil, v_w_proj_sb, v_w_out, v_norm_ffn2, v_ffn2_w_gate, v_ffn2_w_up, v_ffn2_w_down, v_norm_final):
    args = dict(locals())
    names = [name for name, _, _, _ in PACK_LAYOUT]
    t = x.shape[1]
    xs = x.reshape(t, D_MODEL)
    target = loss_target.reshape(t, D_MODEL)
    tables = rope_tables(t)

    my_index = 4 * lax.axis_index("x") + 2 * lax.axis_index("y") + lax.axis_index("c")
    parts = [(l, p) for l in range(2) for p in SUBBLOCKS]
    w_shards = {n: args[n] for n in names}
    gains = {n: args[n] for n in NORM_ROWS}

    in_flight, order_token = {}, jnp.zeros((1, 1), F32)
    masks_of = {lp: (CHIP_PEERS if lp == parts[0] else ALL_PEERS) for lp in parts}
    for l, p in parts:
        packed = pack_shards(w_shards, l, p) + order_token.astype(BF16)
        in_flight[(l, p)] = spread_start(packed, landing_zone(packed, my_index), per_peer=False, masks=masks_of[(l, p)],
                                         name=f"gather_start_l{l}_{p}")
        order_token = in_flight[(l, p)][4][0:1, 0:1]

    def weights_of(l, p, after):
        land = spread_wait(in_flight[(l, p)], after, per_peer=False, masks=masks_of[(l, p)], name=f"gather_wait_l{l}_{p}")
        if masks_of[(l, p)] is CHIP_PEERS:
            land = relay_wait(relay_start(land, name=f"gather_relay_l{l}_{p}"), name=f"gather_relayed_l{l}_{p}")
        return unpack_gathered(land, p)

    saved, weights = {}, {}
    act = xs
    h = rms_fwd(xs, _row(gains["norm_ffn1"][0]) + order_token, name="l0_ffn1_norm")
    for i, (l, p) in enumerate(parts):
        weights[(l, p)] = weights_of(l, p, act)
        nl, np_ = parts[i + 1] if i + 1 < len(parts) else (None, None)
        next_gain = _row(gains["norm_" + np_][nl]) if np_ else None
        if p == "mix":
            act, h, saved[(l, p)] = mixer_fwd(act, h, weights[(l, p)], tables, f"l{l}_mix", next_gain)
        else:
            act, h, saved[(l, p)] = ffn_fwd(act, h, weights[(l, p)], p, f"l{l}_{p}", next_gain)
    dx, dxb, g_final, loss_part = final_loss(act, _row(norm_final), target, name="loss_head")

    gain_grads, sent = {}, {}
    order_token = jnp.zeros((1, 1), F32)
    for l, p in reversed(parts):
        w, sv = weights[(l, p)], saved[(l, p)]
        if p == "mix":
            gw, rest = mixer_bwd_weights(dxb, sv, w, tables, f"l{l}_mix")
        else:
            gw, rest = ffn_bwd_weights(dxb, sv, w, p, f"l{l}_{p}")
        slices = pack_full_grads(gw, p)
        own = lax.dynamic_index_in_dim(slices, my_index, 0, keepdims=False)
        sent[(l, p)] = spread_start(slices, landing_zone(own, my_index), per_peer=True, name=f"reduce_start_l{l}_{p}")
        gain = _row(gains["norm_" + p][l]) + sent[(l, p)][4][0:1, 0:1]
        if p == "mix":
            dx, dxb, gain_grads[("norm_mix", l)] = mixer_bwd_input(dx, rest, sv, gain, w, f"l{l}_mix")
        else:
            dx, dxb, gain_grads[("norm_" + p, l)] = ffn_bwd_input(dx, rest, sv, gain, w, p, f"l{l}_{p}")

    loss_row = jnp.pad(loss_part[:, :1], ((0, 0), (0, LANES - 1)))
    small = jnp.concatenate([gain_grads[(n, l)] for n in NORM_ROWS for l in range(2)] + [g_final, loss_row], axis=0)
    small_g = sum_partials(all_gather_rows(small, name="gather_gain_grads"), tr=8, name="sum_gain_grads")
    zero_row = jnp.zeros((1, LANES), F32)
    small_of = lambda pre: jnp.concatenate([args[pre + n] for n in NORM_ROWS] + [_row(args[pre + "norm_final"]), zero_row], axis=0)[None]
    small_out = adamw(small_g[None], small_of(""), small_of("m_"), small_of("v_"), tr=8, name="update_gains")
    small_all = [small_g] + [o[0] for o in small_out]

    grad_of = {}
    for l, p in reversed(parts):
        partials = spread_wait(sent[(l, p)], dx, per_peer=True, name=f"reduce_wait_l{l}_{p}")
        summed = sum_partials(partials, tr=UPDATE_ROWS[p], name=f"sum_l{l}_{p}")
        for n, g in unpack_summed(summed, p, w_shards).items():
            grad_of.setdefault(n, [None, None])[l] = g
    big_all = [{}, {}, {}, {}]
    for p in ("ffn2", "mix", "ffn1"):
        for n, _, by_cols, _ in SUBBLOCKS[p]:
            view = (lambda a: jnp.swapaxes(a, 1, 2)) if by_cols else (lambda a: a)
            g = jnp.stack(grad_of[n], axis=0)
            outs = adamw(g, view(args[n]), view(args["m_" + n]), view(args["v_" + n]), tr=512, name=f"update_{n}")
            for kind, arr in enumerate([g] + list(outs)):
                big_all[kind][n] = view(arr)

    def gains_of(s):
        out = {n: s[2 * i:2 * i + 2] for i, n in enumerate(NORM_ROWS)}
        out["norm_final"] = s[6]
        return out

    order = ["norm_ffn1", "ffn1_w_gate", "ffn1_w_up", "ffn1_w_down", "norm_mix", "w_in", "w_proj_dil", "w_proj_sb", "w_out",
             "norm_ffn2", "ffn2_w_gate", "ffn2_w_up", "ffn2_w_down", "norm_final"]
    results = []
    for kind in range(4):
        both = {**big_all[kind], **gains_of(small_all[kind])}
        results += [both[n] for n in order]
    loss = small_g[7, 0]
    return (loss, dx.reshape(1, t, D_MODEL), *results)
```

```python
import functools

import jax
import jax.numpy as jnp
from jax import lax
from jax.experimental import pallas as pl
from jax.experimental.pallas import tpu as pltpu

F32 = jnp.float32
BF16 = jnp.bfloat16

D_MODEL = 1024
HEAD_DIM = 64
GROUP_W = 256
D_IN = 5120
N_DIL_GROUPS = 3
DIL_SPAN = 128
DILATIONS = (1, 4, 16)
ROPE_THETA = 500000.0
ROPE_DIM = 16
RMS_EPS = 1e-6
ATT_SCALE = HEAD_DIM ** -0.5
QS_BLK, KS_BLK, VS_BLK = 9, 10, 11
GATE_DIL_BLK, GATE_SB_BLK = 3, 4

ADAM_LR, ADAM_B1, ADAM_B2, ADAM_EPS, ADAM_WD, ADAM_STEP = 0.001, 0.9, 0.999, 1e-08, 0.01, 10

N_DEV = 8
LANES = 1024
VMEM_PHYSICAL_V7X = 64 << 20
VMEM_TEMP_HEADROOM = 20 << 20

PACK_LAYOUT = (
    ("ffn1_w_gate", 352, True, (1024, 2816)),
    ("ffn1_w_up", 352, True, (1024, 2816)),
    ("ffn1_w_down", 352, False, (2816, 1024)),
    ("w_in", 640, True, (1024, 5120)),
    ("w_proj_dil", 32, True, (256, 1024)),
    ("w_proj_sb", 32, True, (256, 1024)),
    ("w_out", 128, False, (1024, 1024)),
    ("ffn2_w_gate", 352, True, (1024, 2816)),
    ("ffn2_w_up", 352, True, (1024, 2816)),
    ("ffn2_w_down", 352, False, (2816, 1024)),
)
SUBBLOCKS = {"ffn1": PACK_LAYOUT[0:3], "mix": PACK_LAYOUT[3:7], "ffn2": PACK_LAYOUT[7:10]}
UPDATE_ROWS = {"ffn1": 352, "mix": 416, "ffn2": 352}
NORM_ROWS = ("norm_ffn1", "norm_mix", "norm_ffn2")


def _nbytes(shape, dtype):
    n = 1
    for s in shape:
        n *= s
    return n * jnp.dtype(dtype).itemsize


def _pcall(body, *, name, grid, in_specs, out_specs, out_shape, blocks, scratch_shapes=(), scratch_bytes=0):
    need = 2 * sum(_nbytes(s, d) for s, d in blocks) + scratch_bytes + VMEM_TEMP_HEADROOM
    limit = min(need, VMEM_PHYSICAL_V7X - (4 << 20))
    in_hbm = lambda s: pltpu.HBM(s.shape, s.dtype)
    out_shape = [in_hbm(s) for s in out_shape] if isinstance(out_shape, (list, tuple)) else in_hbm(out_shape)
    call = pl.pallas_call(
        body, name=name, grid=grid, in_specs=in_specs, out_specs=out_specs, out_shape=out_shape,
        scratch_shapes=scratch_shapes,
        compiler_params=pltpu.CompilerParams(vmem_limit_bytes=limit),
    )
    return lambda *args: call(*[pltpu.with_memory_space_constraint(a, pltpu.HBM) for a in args])


def _dot(a, b, form):
    dn = {"nn": (((1,), (0,)), ((), ())), "nt": (((1,), (1,)), ((), ())), "tn": (((0,), (0,)), ((), ()))}[form]
    return lax.dot_general(a.astype(BF16), b.astype(BF16), dn, preferred_element_type=F32)


def _sigmoid(x):
    return 1.0 / (1.0 + jnp.exp(-x))


def matmul(pairs, form, *, tm, tn, tk, out_dtype, name, scale=1.0, res=None):
    a0, b0 = pairs[0]
    if form == "tn":
        kdim, m = a0.shape
        n = b0.shape[1]
    else:
        m, kdim = a0.shape
        n = b0.shape[1] if form == "nn" else b0.shape[0]
    tm, tn, tk = min(tm, m), min(tn, n), min(tk, kdim)
    assert m % tm == 0 and n % tn == 0 and kdim % tk == 0, (name, m, n, kdim, tm, tn, tk)
    nk = kdim // tk
    npairs = len(pairs)

    if form == "tn":
        a_blk, a_map = (tk, tm), (lambda j, i, k: (k, i))
    else:
        a_blk, a_map = (tm, tk), (lambda j, i, k: (i, k))
    if form == "nt":
        b_blk, b_map = (tn, tk), (lambda j, i, k: (j, k))
    else:
        b_blk, b_map = (tk, tn), (lambda j, i, k: (k, j))
    o_map = lambda j, i, k: (i, j)

    def body(*refs):
        ab = refs[:2 * npairs]
        rest = refs[2 * npairs:]
        if res is not None:
            r_ref, o_ref = rest[0], rest[1]
            rest = rest[2:]
        else:
            r_ref, o_ref = None, rest[0]
            rest = rest[1:]

        def partial_sum():
            p = _dot(ab[0][...], ab[1][...], form)
            for q in range(1, npairs):
                p = p + _dot(ab[2 * q][...], ab[2 * q + 1][...], form)
            return p

        def finish(acc):
            out = acc * scale if scale != 1.0 else acc
            if r_ref is not None:
                out = r_ref[...] + out
            o_ref[...] = out.astype(out_dtype)

        if nk == 1:
            finish(partial_sum())
        else:
            acc_ref = rest[0]
            k = pl.program_id(2)

            @pl.when(k == 0)
            def _():
                acc_ref[...] = partial_sum()

            @pl.when(k > 0)
            def _():
                acc_ref[...] += partial_sum()

            @pl.when(k == nk - 1)
            def _():
                finish(acc_ref[...])

    in_specs, args, blocks = [], [], []
    for a, b in pairs:
        in_specs += [pl.BlockSpec(a_blk, a_map), pl.BlockSpec(b_blk, b_map)]
        args += [a, b]
        blocks += [(a_blk, a.dtype), (b_blk, b.dtype)]
    if res is not None:
        in_specs.append(pl.BlockSpec((tm, tn), o_map))
        args.append(res)
        blocks.append(((tm, tn), res.dtype))
    blocks.append(((tm, tn), out_dtype))
    scratch = [pltpu.VMEM((tm, tn), F32)] if nk > 1 else []
    return _pcall(
        body, name=name, grid=(n // tn, m // tm, nk), in_specs=in_specs,
        out_specs=pl.BlockSpec((tm, tn), o_map), out_shape=jax.ShapeDtypeStruct((m, n), out_dtype),
        blocks=blocks, scratch_shapes=scratch, scratch_bytes=(tm * tn * 4 if nk > 1 else 0),
    )(*args)


def swiglu_fwd(h, wg_t, wu_t, *, name, tm=512, tn=1408):
    t, d = h.shape
    f = wg_t.shape[0]
    tm, tn = min(tm, t), min(tn, f)

    def body(h_ref, wg_ref, wu_ref, ga_ref, gb_ref, s_ref):
        hh = h_ref[...]
        a = _dot(hh, wg_ref[...], "nt")
        b = _dot(hh, wu_ref[...], "nt")
        sg = _sigmoid(a)
        silu = a * sg
        ga_ref[...] = (b * (sg * (1.0 + a * (1.0 - sg)))).astype(BF16)
        gb_ref[...] = silu.astype(BF16)
        s_ref[...] = (silu * b).astype(BF16)

    w_spec = pl.BlockSpec((tn, d), lambda j, i: (j, 0))
    o_spec = pl.BlockSpec((tm, tn), lambda j, i: (i, j))
    o_shape = jax.ShapeDtypeStruct((t, f), BF16)
    return _pcall(
        body, name=name, grid=(f // tn, t // tm),
        in_specs=[pl.BlockSpec((tm, d), lambda j, i: (i, 0)), w_spec, w_spec],
        out_specs=[o_spec, o_spec, o_spec], out_shape=[o_shape, o_shape, o_shape],
        blocks=[((tm, d), BF16), ((tn, d), BF16), ((tn, d), BF16)] + [((tm, tn), BF16)] * 3,
    )(h, wg_t, wu_t)


def swiglu_bwd(dyb, wd, ga, gb, *, name, scale, tm=512, tn=1408):
    t, d = dyb.shape
    f = wd.shape[0]
    tm, tn = min(tm, t), min(tn, f)

    def body(dy_ref, wd_ref, ga_ref, gb_ref, da_ref, db_ref):
        ds = _dot(dy_ref[...], wd_ref[...], "nt") * scale
        da_ref[...] = (ds * ga_ref[...].astype(F32)).astype(BF16)
        db_ref[...] = (ds * gb_ref[...].astype(F32)).astype(BF16)

    o_spec = pl.BlockSpec((tm, tn), lambda j, i: (i, j))
    o_shape = jax.ShapeDtypeStruct((t, f), BF16)
    return _pcall(
        body, name=name, grid=(f // tn, t // tm),
        in_specs=[pl.BlockSpec((tm, d), lambda j, i: (i, 0)), pl.BlockSpec((tn, d), lambda j, i: (j, 0)), o_spec, o_spec],
        out_specs=[o_spec, o_spec], out_shape=[o_shape, o_shape],
        blocks=[((tm, d), BF16), ((tn, d), BF16)] + [((tm, tn), BF16)] * 4,
    )(dyb, wd, ga, gb)


def gate_fwd(odil, osb, wpd_t, wps_t, proj, *, name, tm=512):
    t = odil.shape[0]
    tm = min(tm, t)

    def body(od_ref, os_ref, wpd_ref, wps_ref, g1_ref, g2_ref, y_ref, u1_ref, u2_ref):
        u1 = _dot(od_ref[...], wpd_ref[...], "nt")
        u2 = _dot(os_ref[...], wps_ref[...], "nt")
        y = _sigmoid(g1_ref[...].astype(F32)) * u1 + _sigmoid(g2_ref[...].astype(F32)) * u2
        y_ref[...] = y.astype(BF16)
        u1_ref[...] = u1.astype(BF16)
        u2_ref[...] = u2.astype(BF16)

    o_spec = pl.BlockSpec((tm, D_MODEL), lambda i: (i, 0))
    w_spec = pl.BlockSpec((D_MODEL, GROUP_W), lambda i: (0, 0))
    a_spec = pl.BlockSpec((tm, GROUP_W), lambda i: (i, 0))
    o_shape = jax.ShapeDtypeStruct((t, D_MODEL), BF16)
    return _pcall(
        body, name=name, grid=(t // tm,),
        in_specs=[a_spec, a_spec, w_spec, w_spec,
                  pl.BlockSpec((tm, D_MODEL), lambda i: (i, GATE_DIL_BLK)),
                  pl.BlockSpec((tm, D_MODEL), lambda i: (i, GATE_SB_BLK))],
        out_specs=[o_spec, o_spec, o_spec], out_shape=[o_shape, o_shape, o_shape],
        blocks=[((tm, GROUP_W), F32)] * 2 + [((D_MODEL, GROUP_W), BF16)] * 2 + [((tm, D_MODEL), BF16)] * 5,
    )(odil, osb, wpd_t, wps_t, proj, proj)


def gate_bwd(dxb, wout, u1, u2, proj, *, name, tm=512):
    t = dxb.shape[0]
    tm = min(tm, t)

    def body(dx_ref, w_ref, u1_ref, u2_ref, g1_ref, g2_ref, du1_ref, du2_ref, dg_ref):
        dy = _dot(dx_ref[...], w_ref[...], "nt")
        s1 = _sigmoid(g1_ref[...].astype(F32))
        s2 = _sigmoid(g2_ref[...].astype(F32))
        du1_ref[...] = (dy * s1).astype(BF16)
        du2_ref[...] = (dy * s2).astype(BF16)
        dg_ref[:, :D_MODEL] = (dy * u1_ref[...].astype(F32) * s1 * (1.0 - s1)).astype(BF16)
        dg_ref[:, D_MODEL:] = (dy * u2_ref[...].astype(F32) * s2 * (1.0 - s2)).astype(BF16)

    o_spec = pl.BlockSpec((tm, D_MODEL), lambda i: (i, 0))
    o_shape = jax.ShapeDtypeStruct((t, D_MODEL), BF16)
    return _pcall(
        body, name=name, grid=(t // tm,),
        in_specs=[o_spec, pl.BlockSpec((D_MODEL, D_MODEL), lambda i: (0, 0)), o_spec, o_spec,
                  pl.BlockSpec((tm, D_MODEL), lambda i: (i, GATE_DIL_BLK)),
                  pl.BlockSpec((tm, D_MODEL), lambda i: (i, GATE_SB_BLK))],
        out_specs=[o_spec, o_spec, pl.BlockSpec((tm, 2 * D_MODEL), lambda i: (i, 0))],
        out_shape=[o_shape, o_shape, jax.ShapeDtypeStruct((t, 2 * D_MODEL), BF16)],
        blocks=[((tm, D_MODEL), BF16)] * 9 + [((D_MODEL, D_MODEL), BF16)],
    )(dxb, wout, u1, u2, proj, proj)


def rms_fwd(x, gain, *, name, tm=512):
    t, d = x.shape
    tm = min(tm, t)

    def body(x_ref, g_ref, h_ref):
        xv = x_ref[...]
        rstd = lax.rsqrt(jnp.mean(xv * xv, axis=1, keepdims=True) + RMS_EPS)
        h_ref[...] = (xv * rstd * g_ref[...]).astype(BF16)

    return _pcall(
        body, name=name, grid=(t // tm,),
        in_specs=[pl.BlockSpec((tm, d), lambda i: (i, 0)), pl.BlockSpec((1, d), lambda i: (0, 0))],
        out_specs=pl.BlockSpec((tm, d), lambda i: (i, 0)), out_shape=jax.ShapeDtypeStruct((t, d), BF16),
        blocks=[((tm, d), F32), ((tm, d), BF16)],
    )(x, gain)


def matmul_res_norm(a, b, res, next_gain, *, scale, tm, name):
    t, k = a.shape
    d = b.shape[1]
    tm = min(tm, t)
    with_norm = next_gain is not None

    def body(a_ref, b_ref, r_ref, *rest):
        out = r_ref[...] + _dot(a_ref[...], b_ref[...], "nn") * scale
        if with_norm:
            g_ref, o_ref, h_ref = rest
            rstd = lax.rsqrt(jnp.mean(out * out, axis=1, keepdims=True) + RMS_EPS)
            h_ref[...] = (out * rstd * g_ref[...]).astype(BF16)
        else:
            o_ref, = rest
        o_ref[...] = out

    row = pl.BlockSpec((tm, d), lambda i: (i, 0))
    in_specs = [pl.BlockSpec((tm, k), lambda i: (i, 0)), pl.BlockSpec((k, d), lambda i: (0, 0)), row]
    args = [a, b, res]
    out_specs, out_shape = [row], [jax.ShapeDtypeStruct((t, d), F32)]
    if with_norm:
        in_specs.append(pl.BlockSpec((1, d), lambda i: (0, 0)))
        args.append(next_gain)
        out_specs.append(row)
        out_shape.append(jax.ShapeDtypeStruct((t, d), BF16))
    outs = _pcall(
        body, name=name, grid=(t // tm,), in_specs=in_specs, out_specs=out_specs, out_shape=out_shape,
        blocks=[((tm, k), a.dtype), ((k, d), b.dtype), ((tm, d), F32), ((tm, d), F32), ((tm, d), BF16)],
    )(*args)
    return (outs[0], outs[1]) if with_norm else (outs[0], None)


def _rms_bwd_rows(dhv, xv, g, drv):
    rstd = lax.rsqrt(jnp.mean(xv * xv, axis=1, keepdims=True) + RMS_EPS)
    xh = xv * rstd
    dxh = dhv * g
    dx = drv + rstd * (dxh - xh * jnp.mean(dxh * xh, axis=1, keepdims=True))
    return dx, jnp.sum(dhv * xh, axis=0, keepdims=True)


def matmul_rms_bwd(pairs, x, gain, dres, *, tm, name):
    t, d = x.shape
    tm = min(tm, t)
    npairs = len(pairs)

    def body(*refs):
        ab = refs[:2 * npairs]
        x_ref, g_ref, dr_ref, dx_ref, dxb_ref, dg_ref = refs[2 * npairs:]
        dh = _dot(ab[0][...], ab[1][...], "nn")
        for q in range(1, npairs):
            dh = dh + _dot(ab[2 * q][...], ab[2 * q + 1][...], "nn")
        dx, part = _rms_bwd_rows(dh, x_ref[...], g_ref[...], dr_ref[...])
        dx_ref[...] = dx
        dxb_ref[...] = dx.astype(BF16)

        @pl.when(pl.program_id(0) == 0)
        def _():
            dg_ref[...] = part

        @pl.when(pl.program_id(0) > 0)
        def _():
            dg_ref[...] += part

    in_specs, args, blocks = [], [], []
    for a, b in pairs:
        k = a.shape[1]
        in_specs += [pl.BlockSpec((tm, k), lambda i: (i, 0)), pl.BlockSpec((k, d), lambda i: (0, 0))]
        args += [a, b]
        blocks += [((tm, k), a.dtype), ((k, d), b.dtype)]
    row = pl.BlockSpec((tm, d), lambda i: (i, 0))
    vec = pl.BlockSpec((1, d), lambda i: (0, 0))
    return _pcall(
        body, name=name, grid=(t // tm,), in_specs=in_specs + [row, vec, row], out_specs=[row, row, vec],
        out_shape=[jax.ShapeDtypeStruct((t, d), F32), jax.ShapeDtypeStruct((t, d), BF16), jax.ShapeDtypeStruct((1, d), F32)],
        blocks=blocks + [((tm, d), F32)] * 3 + [((tm, d), BF16)],
    )(*args, x, gain, dres)


def final_loss(x, gain, target, *, name, tm=512):
    t, d = x.shape
    tm = min(tm, t)

    def body(x_ref, g_ref, t_ref, dx_ref, dxb_ref, dg_ref, loss_ref):
        xv = x_ref[...]
        g = g_ref[...]
        rstd = lax.rsqrt(jnp.mean(xv * xv, axis=1, keepdims=True) + RMS_EPS)
        xh = xv * rstd
        err = xh * g - t_ref[...]
        dy = err * (1.0 / d)
        dxh = dy * g
        dx = rstd * (dxh - xh * jnp.mean(dxh * xh, axis=1, keepdims=True))
        dx_ref[...] = dx
        dxb_ref[...] = dx.astype(BF16)
        part = jnp.sum(dy * xh, axis=0, keepdims=True)
        sq = jnp.sum(jnp.sum(err * err, axis=1, keepdims=True), axis=0, keepdims=True) * (0.5 / d)
        lpart = jnp.broadcast_to(sq, (1, 128))

        @pl.when(pl.program_id(0) == 0)
        def _():
            dg_ref[...] = part
            loss_ref[...] = lpart

        @pl.when(pl.program_id(0) > 0)
        def _():
            dg_ref[...] += part
            loss_ref[...] += lpart

    row = pl.BlockSpec((tm, d), lambda i: (i, 0))
    vec = pl.BlockSpec((1, d), lambda i: (0, 0))
    return _pcall(
        body, name=name, grid=(t // tm,), in_specs=[row, vec, row],
        out_specs=[row, row, vec, pl.BlockSpec((1, 128), lambda i: (0, 0))],
        out_shape=[jax.ShapeDtypeStruct((t, d), F32), jax.ShapeDtypeStruct((t, d), BF16),
                   jax.ShapeDtypeStruct((1, d), F32), jax.ShapeDtypeStruct((1, 128), F32)],
        blocks=[((tm, d), F32)] * 3 + [((tm, d), BF16)],
    )(x, gain, target)


def rope_tables(t):
    pos = jnp.arange(t, dtype=F32)
    inv_freq = ROPE_THETA ** (-jnp.arange(0, ROPE_DIM, 2, dtype=F32) / ROPE_DIM)
    ang = pos[:, None] * inv_freq[None, :]
    cos, sin = jnp.cos(ang), jnp.sin(ang)
    half = ROPE_DIM // 2
    pad = HEAD_DIM - ROPE_DIM
    one_head = lambda lo, hi, fill: jnp.concatenate([lo, hi, jnp.full((t, pad), fill, F32)], axis=1)
    zeros = jnp.zeros((t, half), F32)
    c = one_head(cos, cos, 1.0)
    sa = one_head(-sin, zeros, 0.0)
    sb = one_head(zeros, sin, 0.0)
    two = lambda a: jnp.concatenate([a, a], axis=1)
    return two(c), two(sa), two(sb)


def _rotate(xv, cv, sav, sbv):
    halves = []
    for half in range(2):
        x = xv[:, 128 * half:128 * (half + 1)]
        halves.append(x * cv + pltpu.roll(x, 120, 1) * sav + pltpu.roll(x, 8, 1) * sbv)
    return jnp.concatenate(halves, axis=1)


STAGE_CHUNKS = 4


def _stage(tm):
    return dict(scratch_shapes=[pltpu.VMEM((STAGE_CHUNKS, tm, 128), F32)], scratch_bytes=STAGE_CHUNKS * tm * 128 * 4)


def _split_residues(stage_ref, val, out_ref, d, col, dtype):
    rows, width = val.shape
    if d == 1:
        out_ref[0, :, col:col + width] = val.astype(dtype)
        return
    chunks = width // 128
    for c in range(chunks):
        stage_ref[c] = val[:, 128 * c:128 * (c + 1)]
    for r in range(d):
        for c in range(chunks):
            out_ref[r, :, col + 128 * c:col + 128 * (c + 1)] = stage_ref[c, pl.ds(r, rows // d, stride=d), :].astype(dtype)


def _join_residues(stage_ref, in_ref, d, col=0, width=GROUP_W):
    if d == 1:
        return in_ref[0, :, col:col + width].astype(F32)
    rows = in_ref.shape[1] * d
    chunks = width // 128
    for r in range(d):
        for c in range(chunks):
            stage_ref[c, pl.ds(r, rows // d, stride=d), :] = in_ref[r, :, col + 128 * c:col + 128 * (c + 1)].astype(F32)
    return jnp.concatenate([stage_ref[c] for c in range(chunks)], axis=1)


def rope_split(proj, tables, *, name, tm=512):
    c, sa, sb = tables
    t = c.shape[0]
    tm = min(tm, t)

    def body(*refs):
        pieces = refs[0:9]
        c_ref, sa_ref, sb_ref = refs[9:12]
        qk_out, v_out = refs[12:15], refs[15:18]
        stage = refs[18]
        cv, sav, sbv = c_ref[...], sa_ref[...], sb_ref[...]
        for g, d in enumerate(DILATIONS):
            for kind in range(3):
                xv = pieces[3 * kind + g][...].astype(F32)
                if kind < 2:
                    _split_residues(stage, _rotate(xv, cv, sav, sbv), qk_out[g], d, GROUP_W * kind, BF16)
                else:
                    _split_residues(stage, xv, v_out[g], d, 0, BF16)

    tab = pl.BlockSpec((tm, 128), lambda i: (i, 0))
    in_specs = [pl.BlockSpec((tm, GROUP_W), functools.partial(lambda i, cb: (i, cb), cb=cb)) for cb in range(9)]
    out_specs = ([pl.BlockSpec((d, tm // d, 2 * GROUP_W), lambda i: (0, i, 0)) for d in DILATIONS]
                 + [pl.BlockSpec((d, tm // d, GROUP_W), lambda i: (0, i, 0)) for d in DILATIONS])
    out_shape = ([jax.ShapeDtypeStruct((d, t // d, 2 * GROUP_W), BF16) for d in DILATIONS]
                 + [jax.ShapeDtypeStruct((d, t // d, GROUP_W), BF16) for d in DILATIONS])
    outs = _pcall(
        body, name=name, grid=(t // tm,), in_specs=in_specs + [tab, tab, tab], out_specs=out_specs, out_shape=out_shape,
        blocks=[((tm, GROUP_W), BF16)] * 18 + [((tm, 128), F32)] * 3,
        **_stage(tm),
    )(*([proj] * 9), c, sa, sb)
    return outs[0:3], outs[3:6]


def rope_join(dqs, dks, dvs, tables, *, name, tm=512):
    c, sa, sb = tables
    t = c.shape[0]
    tm = min(tm, t)

    def body(*refs):
        pieces = refs[0:9]
        c_ref, sa_ref, sb_ref = refs[9:12]
        o_ref, stage = refs[12], refs[13]
        cv, sav, sbv = c_ref[...], -sa_ref[...], -sb_ref[...]
        for kind in range(3):
            for g, d in enumerate(DILATIONS):
                xv = _join_residues(stage, pieces[3 * kind + g], d)
                if kind < 2:
                    xv = _rotate(xv, cv, sav, sbv)
                col = GROUP_W * (3 * kind + g)
                o_ref[:, col:col + GROUP_W] = xv.astype(BF16)

    tab = pl.BlockSpec((tm, 128), lambda i: (i, 0))
    in_specs = [pl.BlockSpec((d, tm // d, GROUP_W), lambda i: (0, i, 0)) for _ in range(3) for d in DILATIONS]
    return _pcall(
        body, name=name, grid=(t // tm,), in_specs=in_specs + [tab, tab, tab],
        out_specs=pl.BlockSpec((tm, 9 * GROUP_W), lambda i: (i, 0)), out_shape=jax.ShapeDtypeStruct((t, 9 * GROUP_W), BF16),
        blocks=[((tm, GROUP_W), F32)] * 9 + [((tm, 128), F32)] * 3 + [((tm, 9 * GROUP_W), BF16)],
        **_stage(tm),
    )(*dqs, *dks, *dvs, c, sa, sb)


def _head_mask(h):
    lane = lax.broadcasted_iota(jnp.int32, (1, GROUP_W), 1)
    return (lane // HEAD_DIM) == h


def _band_masks(heads):
    ri = lax.broadcasted_iota(jnp.int32, (heads * DIL_SPAN, DIL_SPAN), 0) % DIL_SPAN
    ci = lax.broadcasted_iota(jnp.int32, (heads * DIL_SPAN, DIL_SPAN), 1)
    return ci <= ri, ci >= ri


def dil_fwd(qk, v, *, name):
    d, nsub, _ = qk.shape
    nblk = nsub // DIL_SPAN

    def body(q_ref, kc_ref, kp_ref, vc_ref, vp_ref, o_ref, lse_ref):
        nb = pl.program_id(1)
        own, prev = _band_masks(1)
        prev = prev & (nb > 0)
        q, kc, kp, vc, vp = q_ref[0] * ATT_SCALE, kc_ref[0], kp_ref[0], vc_ref[0], vp_ref[0]
        o_acc = jnp.zeros((DIL_SPAN, GROUP_W), F32)
        for h in range(4):
            hm = _head_mask(h)
            qh = jnp.where(hm, q, jnp.zeros_like(q))
            sc = jnp.where(own, _dot(qh, kc, "nt"), -jnp.inf)
            sp = jnp.where(prev, _dot(qh, kp, "nt"), -jnp.inf)
            m = jnp.maximum(jnp.max(sc, axis=1, keepdims=True), jnp.max(sp, axis=1, keepdims=True))
            pc = jnp.exp(sc - m)
            pp = jnp.exp(sp - m)
            den = jnp.sum(pc, axis=1, keepdims=True) + jnp.sum(pp, axis=1, keepdims=True)
            oh = (_dot(pc, vc, "nn") + _dot(pp, vp, "nn")) / den
            o_acc = jnp.where(hm, oh, o_acc)
            lse_ref[0, :, 128 * h:128 * (h + 1)] = jnp.broadcast_to(m + jnp.log(den), (DIL_SPAN, 128))
        o_ref[0] = o_acc

    blk = (1, DIL_SPAN, GROUP_W)
    sblk = (1, DIL_SPAN, 512)
    prv = lambda nb: jnp.maximum(nb - 1, 0)
    return _pcall(
        body, name=name, grid=(d, nblk),
        in_specs=[pl.BlockSpec(blk, lambda r, nb: (r, nb, 0)),
                  pl.BlockSpec(blk, lambda r, nb: (r, nb, 1)),
                  pl.BlockSpec(blk, lambda r, nb: (r, prv(nb), 1)),
                  pl.BlockSpec(blk, lambda r, nb: (r, nb, 0)),
                  pl.BlockSpec(blk, lambda r, nb: (r, prv(nb), 0))],
        out_specs=[pl.BlockSpec(blk, lambda r, nb: (r, nb, 0)), pl.BlockSpec(sblk, lambda r, nb: (r, nb, 0))],
        out_shape=[jax.ShapeDtypeStruct((d, nsub, GROUP_W), F32), jax.ShapeDtypeStruct((d, nsub, 512), F32)],
        blocks=[(blk, BF16)] * 5 + [(blk, F32), (sblk, F32)],
    )(qk, qk, qk, v, v)


def dil_merge(outs, lses, *, name, tm=512):
    t = outs[0].shape[0] * outs[0].shape[1]
    tm = min(tm, t)

    def body(o0, o1, o2, l0, l1, l2, o_ref, lse_ref, stage):
        ls = [_join_residues(stage, l, d, 0, 512) for l, d in zip((l0, l1, l2), DILATIONS)]
        m = jnp.maximum(jnp.maximum(ls[0], ls[1]), ls[2])
        tot = m + jnp.log(jnp.exp(ls[0] - m) + jnp.exp(ls[1] - m) + jnp.exp(ls[2] - m))
        lse_ref[...] = tot
        lane = lax.broadcasted_iota(jnp.int32, (1, 128), 1)
        first = lane < HEAD_DIM
        acc = jnp.zeros((tm, GROUP_W), F32)
        for og, lg, d in zip((o0, o1, o2), ls, DILATIONS):
            w = jnp.exp(lg - tot)
            wide = jnp.concatenate([jnp.where(first, w[:, 0:128], w[:, 128:256]),
                                    jnp.where(first, w[:, 256:384], w[:, 384:512])], axis=1)
            acc = acc + wide * _join_residues(stage, og, d)
        o_ref[...] = acc

    o_in = [pl.BlockSpec((d, tm // d, GROUP_W), lambda i: (0, i, 0)) for d in DILATIONS]
    l_in = [pl.BlockSpec((d, tm // d, 512), lambda i: (0, i, 0)) for d in DILATIONS]
    return _pcall(
        body, name=name, grid=(t // tm,), in_specs=o_in + l_in,
        out_specs=[pl.BlockSpec((tm, GROUP_W), lambda i: (i, 0)), pl.BlockSpec((tm, 512), lambda i: (i, 0))],
        out_shape=[jax.ShapeDtypeStruct((t, GROUP_W), F32), jax.ShapeDtypeStruct((t, 512), F32)],
        blocks=[((tm, GROUP_W), F32)] * 4 + [((tm, 512), F32)] * 4,
        **_stage(tm),
    )(*outs, *lses)


def dil_bwd_prep(do, o, lse, *, name, tm=512):
    t = do.shape[0]
    tm = min(tm, t)
    wide = DILATIONS[1:]

    def body(do_ref, o_ref, lse_ref, ds_ref, *rest):
        do_out, lse_out, ds_out = rest[0:2], rest[2:4], rest[4:6]
        stage = rest[6]
        dov = do_ref[...]
        prod = dov * o_ref[...]
        for h in range(4):
            s = jnp.sum(jnp.where(_head_mask(h), prod, 0.0), axis=1, keepdims=True)
            ds_ref[:, 128 * h:128 * (h + 1)] = jnp.broadcast_to(s, (tm, 128))
        for i, d in enumerate(wide):
            _split_residues(stage, dov, do_out[i], d, 0, BF16)
            _split_residues(stage, lse_ref[...], lse_out[i], d, 0, F32)
            _split_residues(stage, ds_ref[...], ds_out[i], d, 0, F32)

    nat = lambda w: pl.BlockSpec((tm, w), lambda i: (i, 0))
    res = lambda d, w: pl.BlockSpec((d, tm // d, w), lambda i: (0, i, 0))
    shape = lambda d, w, dt: jax.ShapeDtypeStruct((d, t // d, w), dt)
    outs = _pcall(
        body, name=name, grid=(t // tm,), in_specs=[nat(GROUP_W), nat(GROUP_W), nat(512)],
        out_specs=[nat(512)] + [res(d, GROUP_W) for d in wide] + [res(d, 512) for d in wide] * 2,
        out_shape=([jax.ShapeDtypeStruct((t, 512), F32)] + [shape(d, GROUP_W, BF16) for d in wide]
                   + [shape(d, 512, F32) for d in wide] * 2),
        blocks=[((tm, GROUP_W), F32)] * 3 + [((tm, 512), F32)] * 6,
        **_stage(tm),
    )(do, o, lse)
    return outs[0], outs[1:3], outs[3:5], outs[5:7]


def head_sums(a, b, *, name, round_a=False, tm=512):
    t = a.shape[0]
    tm = min(tm, t)

    def body(a_ref, b_ref, o_ref):
        av = a_ref[...]
        if round_a:
            av = av.astype(BF16).astype(F32)
        prod = av * b_ref[...]
        for h in range(4):
            s = jnp.sum(jnp.where(_head_mask(h), prod, 0.0), axis=1, keepdims=True)
            o_ref[:, 128 * h:128 * (h + 1)] = jnp.broadcast_to(s, (tm, 128))

    spec = pl.BlockSpec((tm, GROUP_W), lambda i: (i, 0))
    return _pcall(
        body, name=name, grid=(t // tm,), in_specs=[spec, spec],
        out_specs=pl.BlockSpec((tm, 512), lambda i: (i, 0)), out_shape=jax.ShapeDtypeStruct((t, 512), F32),
        blocks=[((tm, GROUP_W), F32)] * 2 + [((tm, 512), F32)],
    )(a, b)


def dil_bwd(qk, v, do, lse, dsum, *, name):
    d, nsub, _ = qk.shape
    nblk = nsub // DIL_SPAN

    def body(qa_ref, qb_ref, kc_ref, kp_ref, vc_ref, vp_ref, doa_ref, dob_ref, la_ref, lb_ref, sa_ref, sb_ref,
             dq_ref, dk_ref, dv_ref):
        nb = pl.program_id(1)
        own, band = _band_masks(4)
        prev = band & (nb > 0)
        nxt = band & (nb < nblk - 1)
        kc, kp, vc, vp = kc_ref[0], kp_ref[0], vc_ref[0], vp_ref[0]
        qas, qbs = _stack_heads(qa_ref[0] * ATT_SCALE), _stack_heads(qb_ref[0] * ATT_SCALE)
        das, dbs = _stack_heads(doa_ref[0].astype(BF16)), _stack_heads(dob_ref[0].astype(BF16))
        stat = lambda ref: jnp.concatenate([ref[0, :, 128 * h:128 * (h + 1)] for h in range(4)], axis=0)
        la, lb, sa, sb = stat(la_ref), stat(lb_ref), stat(sa_ref), stat(sb_ref)

        def probs(qs, ds_, k, v, mask, l, s):
            p = jnp.where(mask, jnp.exp(_dot(qs, k, "nt") - l), 0.0)
            dsc = p * (_dot(ds_, v, "nt") - s)
            return p.astype(BF16), dsc.astype(BF16)

        p_cc, ds_cc = probs(qas, das, kc, vc, own, la, sa)
        _, ds_cp = probs(qas, das, kp, vp, prev, la, sa)
        p_nc, ds_nc = probs(qbs, dbs, kc, vc, nxt, lb, sb)
        dq_ref[0] = _unstack_heads(_dot(ds_cc, kc, "nn") + _dot(ds_cp, kp, "nn"), DIL_SPAN) * ATT_SCALE
        dk_ref[0] = _dot(ds_cc, qas, "tn") + _dot(ds_nc, qbs, "tn")
        dv_ref[0] = _dot(p_cc, das, "tn") + _dot(p_nc, dbs, "tn")

    blk = (1, DIL_SPAN, GROUP_W)
    sblk = (1, DIL_SPAN, 512)
    prv = lambda nb: jnp.maximum(nb - 1, 0)
    nxt_ = lambda nb: jnp.minimum(nb + 1, nblk - 1)
    cur_at = lambda c: pl.BlockSpec(blk, functools.partial(lambda r, nb, c: (r, nb, c), c=c))
    prv_at = lambda c: pl.BlockSpec(blk, functools.partial(lambda r, nb, c: (r, prv(nb), c), c=c))
    nxt_at = lambda c: pl.BlockSpec(blk, functools.partial(lambda r, nb, c: (r, nxt_(nb), c), c=c))
    s_cur = pl.BlockSpec(sblk, lambda r, nb: (r, nb, 0))
    s_nxt = pl.BlockSpec(sblk, lambda r, nb: (r, nxt_(nb), 0))
    o_spec = pl.BlockSpec(blk, lambda r, nb: (r, nb, 0))
    o_shape = jax.ShapeDtypeStruct((d, nsub, GROUP_W), F32)
    return _pcall(
        body, name=name, grid=(d, nblk),
        in_specs=[cur_at(0), nxt_at(0), cur_at(1), prv_at(1), cur_at(0), prv_at(0), cur_at(0), nxt_at(0),
                  s_cur, s_nxt, s_cur, s_nxt],
        out_specs=[o_spec, o_spec, o_spec], out_shape=[o_shape, o_shape, o_shape],
        blocks=[(blk, BF16)] * 6 + [(blk, F32)] * 5 + [(sblk, F32)] * 4,
    )(qk, qk, qk, qk, v, v, do, do, lse, lse, dsum, dsum)


def _tri_dot(x, b):
    hi = x.astype(BF16)
    lo = (x - hi.astype(F32)).astype(BF16)
    return _dot(jnp.concatenate([hi, lo], axis=1), jnp.concatenate([b, b], axis=0), "nn")


SB_TILE = 256


def _stack_heads(a):
    return jnp.concatenate([jnp.where(_head_mask(h), a, jnp.zeros_like(a)) for h in range(4)], axis=0)


def _unstack_heads(acc, rows):
    out = acc[0:rows]
    for h in range(1, 4):
        out = jnp.where(_head_mask(h), acc[h * rows:(h + 1) * rows], out)
    return out


def _tri_masks(n):
    ri = lax.broadcasted_iota(jnp.int32, (n, n), 0)
    ci = lax.broadcasted_iota(jnp.int32, (n, n), 1)
    return (ri > ci).astype(BF16), (ri >= ci).astype(BF16)


def _sb_weights(qs, kt, after, c_keep, diagonal):
    z = _dot(qs, kt, "nt")
    lbeta = jnp.minimum(z, 0.0) - jnp.log(1.0 + jnp.exp(-jnp.abs(z)))
    lkeep = lbeta - z
    past = None
    if diagonal:
        n = SB_TILE
        past = lax.broadcasted_iota(jnp.int32, z.shape, 1) < lax.broadcasted_iota(jnp.int32, z.shape, 0) % n
        lkeep = jnp.where(past, lkeep, 0.0)
    w = jnp.exp(lbeta + _tri_dot(lkeep, after) + c_keep)
    if diagonal:
        w = jnp.where(past, w, 0.0)
    return z, past, lbeta, lkeep, w


def sb_fwd(proj, *, name):
    t = proj.shape[0]
    n = SB_TILE
    assert t % n == 0

    def body(q_ref, k_ref, v_ref, o_ref, acc_ref):
        qb = pl.program_id(0)
        qs = _stack_heads(q_ref[...] * ATT_SCALE)
        after, _ = _tri_masks(n)

        def tile(off, diagonal, c_keep):
            kt = k_ref[pl.ds(off, n), :]
            vt = v_ref[pl.ds(off, n), :]
            _, _, _, lkeep, w = _sb_weights(qs, kt, after, c_keep, diagonal)
            pv = _tri_dot(w, vt)
            if diagonal:
                acc_ref[...] = pv
            else:
                acc_ref[...] += pv
            return c_keep + jnp.sum(lkeep, axis=1, keepdims=True)

        c0 = tile(pl.multiple_of(qb * n, n), True, jnp.zeros((4 * n, 1), F32))
        lax.fori_loop(0, qb, lambda it, c: tile(pl.multiple_of((qb - 1 - it) * n, n), False, c), c0)
        o_ref[...] = _unstack_heads(acc_ref[...], n)

    full = lambda cb: pl.BlockSpec((t, GROUP_W), functools.partial(lambda i, cb: (0, cb), cb=cb))
    return _pcall(
        body, name=name, grid=(t // n,),
        in_specs=[pl.BlockSpec((n, GROUP_W), lambda i: (i, QS_BLK)), full(KS_BLK), full(VS_BLK)],
        out_specs=pl.BlockSpec((n, GROUP_W), lambda i: (i, 0)), out_shape=jax.ShapeDtypeStruct((t, GROUP_W), F32),
        blocks=[((n, GROUP_W), BF16), ((t, GROUP_W), BF16), ((t, GROUP_W), BF16), ((n, GROUP_W), F32)],
        scratch_shapes=[pltpu.VMEM((4 * n, GROUP_W), F32)], scratch_bytes=4 * n * GROUP_W * 4,
    )(proj, proj, proj)


def sb_bwd(proj, do, gtot, *, name):
    t = proj.shape[0]
    n = SB_TILE
    assert t % n == 0

    def body(q_ref, k_ref, v_ref, do_ref, gt_ref, dq_ref, dk_ref, dv_ref, acc_ref):
        qb = pl.program_id(0)

        @pl.when(qb == 0)
        def _():
            dk_ref[...] = jnp.zeros_like(dk_ref)
            dv_ref[...] = jnp.zeros_like(dv_ref)

        qs = _stack_heads(q_ref[...] * ATT_SCALE)
        dos = _stack_heads(do_ref[...].astype(BF16))
        gt = jnp.concatenate([jnp.max(gt_ref[:, 128 * h:128 * (h + 1)], axis=1, keepdims=True) for h in range(4)], axis=0)
        after, from_on = _tri_masks(n)

        def tile(off, diagonal, carry):
            c_keep, c_g = carry
            kt = k_ref[pl.ds(off, n), :]
            vt = v_ref[pl.ds(off, n), :]
            z, past, lbeta, lkeep, w = _sb_weights(qs, kt, after, c_keep, diagonal)
            gw = w * _dot(dos, vt, "nt")
            big_g = gt - (_tri_dot(gw, from_on) + c_g)
            dz = gw * jnp.exp(lbeta - z) - big_g * jnp.exp(lbeta)
            if diagonal:
                dz = jnp.where(past, dz, 0.0)
            dz = dz.astype(BF16)
            dk_ref[pl.ds(off, n), :] += _dot(dz, qs, "tn")
            dv_ref[pl.ds(off, n), :] += _dot(w, dos, "tn")
            dq = _dot(dz, kt, "nn")
            if diagonal:
                acc_ref[...] = dq
            else:
                acc_ref[...] += dq
            return c_keep + jnp.sum(lkeep, axis=1, keepdims=True), c_g + jnp.sum(gw, axis=1, keepdims=True)

        zero_col = jnp.zeros((4 * n, 1), F32)
        c0 = tile(pl.multiple_of(qb * n, n), True, (zero_col, zero_col))
        lax.fori_loop(0, qb, lambda it, c: tile(pl.multiple_of((qb - 1 - it) * n, n), False, c), c0)
        dq_ref[...] = _unstack_heads(acc_ref[...], n) * ATT_SCALE

    full = lambda cb: pl.BlockSpec((t, GROUP_W), functools.partial(lambda i, cb: (0, cb), cb=cb))
    whole = pl.BlockSpec((t, GROUP_W), lambda i: (0, 0))
    rowblk = pl.BlockSpec((n, GROUP_W), lambda i: (i, 0))
    shape = jax.ShapeDtypeStruct((t, GROUP_W), F32)
    return _pcall(
        body, name=name, grid=(t // n,),
        in_specs=[pl.BlockSpec((n, GROUP_W), lambda i: (i, QS_BLK)), full(KS_BLK), full(VS_BLK), rowblk,
                  pl.BlockSpec((n, 512), lambda i: (i, 0))],
        out_specs=[rowblk, whole, whole], out_shape=[shape, shape, shape],
        blocks=[((n, GROUP_W), BF16), ((t, GROUP_W), BF16), ((t, GROUP_W), BF16), ((n, GROUP_W), F32),
                ((n, 512), F32), ((n, GROUP_W), F32), ((t, GROUP_W), F32), ((t, GROUP_W), F32)],
        scratch_shapes=[pltpu.VMEM((4 * n, GROUP_W), F32)], scratch_bytes=4 * n * GROUP_W * 4,
    )(proj, proj, proj, do, gtot)


def _mesh_place():
    return lax.axis_index("x"), lax.axis_index("y"), lax.axis_index("c")


def _flip(place, mask):
    x, y, c = place
    return ((1 - x) if mask & 4 else x, (1 - y) if mask & 2 else y, (1 - c) if mask & 1 else c)


def _dev_index(place):
    x, y, c = place
    return 4 * x + 2 * y + c


HBM_SPEC = pl.BlockSpec(memory_space=pltpu.HBM)


def all_gather_rows(shard, *, name):
    rows, lanes = shard.shape

    def body(x_ref, out_ref, send_sems, recv_sems, local_sem):
        me = _mesh_place()
        x, y, c = me
        sibling = _flip(me, 1)
        chips = [_flip(me, 4), _flip(me, 2), _flip(me, 6)]

        def copy(k, block, to, src=None):
            dst = out_ref.at[_dev_index(block)]
            return pltpu.make_async_remote_copy(
                src_ref=dst if src is None else src, dst_ref=dst, send_sem=send_sems.at[k], recv_sem=recv_sems.at[k],
                device_id=to, device_id_type=pl.DeviceIdType.MESH)

        mine = pltpu.make_async_copy(x_ref, out_ref.at[_dev_index(me)], local_sem)
        mine.start()
        first = [copy(0, me, sibling, src=x_ref)] + [copy(1 + j, me, chip, src=x_ref) for j, chip in enumerate(chips)]
        for cp in first:
            cp.start()
        passed = [copy(4 + j, chip, sibling) for j, chip in enumerate(chips)]
        for j, chip in enumerate(chips):
            copy(1 + j, chip, me).wait_recv()
            passed[j].start()
        copy(0, sibling, me).wait_recv()
        for j, chip in enumerate(chips):
            copy(4 + j, _flip(chip, 1), me).wait_recv()
        for cp in first + passed:
            cp.wait_send()
        mine.wait()

    return pl.pallas_call(
        body, name=name, in_specs=[HBM_SPEC], out_specs=HBM_SPEC,
        out_shape=jax.ShapeDtypeStruct((N_DEV, rows, lanes), shard.dtype),
        scratch_shapes=[pltpu.SemaphoreType.DMA((7,)), pltpu.SemaphoreType.DMA((7,)), pltpu.SemaphoreType.DMA],
    )(shard)


SEM_SPEC = pl.BlockSpec(memory_space=pltpu.SEMAPHORE)
DATAFLOW_EFFECT = pltpu.SideEffectType.DATAFLOW_SIDE_EFFECTING


ALL_PEERS = tuple(range(1, N_DEV))
CHIP_PEERS = (1, 4, 2, 6)
OTHER_CHIPS = (4, 2, 6)


def _spread_copies(src_ref, land_ref, send_sems, recv_sems, per_peer, masks, arriving):
    me = _mesh_place()
    out = []
    for i, mask in enumerate(masks):
        peer = _flip(me, mask)
        data_of = _dev_index(me) if arriving else _dev_index(peer)
        slot = _dev_index(peer) if arriving else _dev_index(me)
        out.append(pltpu.make_async_remote_copy(
            src_ref=src_ref.at[data_of] if per_peer else src_ref, dst_ref=land_ref.at[slot],
            send_sem=send_sems.at[i], recv_sem=recv_sems.at[i],
            device_id=peer, device_id_type=pl.DeviceIdType.MESH))
    return out


def _sem_pair(n):
    return pltpu.SemaphoreType.DMA((n,)), pltpu.SemaphoreType.DMA((n,))


def spread_start(src, land, *, per_peer, name, masks=ALL_PEERS):
    def body(src_ref, land_ref, send_sems, recv_sems, src_thru, land_thru, token):
        for cp in _spread_copies(src_ref, land_ref, send_sems, recv_sems, per_peer, masks, arriving=False):
            cp.start()
        token[...] = jnp.zeros_like(token)

    return pl.pallas_call(
        body, name=name, in_specs=(HBM_SPEC, HBM_SPEC),
        out_shape=(*_sem_pair(len(masks)), pltpu.HBM(src.shape, src.dtype), pltpu.HBM(land.shape, land.dtype),
                   jax.ShapeDtypeStruct((8, 128), F32)),
        out_specs=(SEM_SPEC, SEM_SPEC, HBM_SPEC, HBM_SPEC, pl.BlockSpec(memory_space=pltpu.VMEM)),
        input_output_aliases={0: 2, 1: 3},
        compiler_params=pltpu.CompilerParams(has_side_effects=DATAFLOW_EFFECT),
    )(pltpu.with_memory_space_constraint(src, pltpu.HBM), pltpu.with_memory_space_constraint(land, pltpu.HBM))


def spread_wait(started, after, *, per_peer, name, masks=ALL_PEERS):
    send_sems, recv_sems, src_thru, land_thru, _ = started

    def body(src_ref, land_ref, send_sems, recv_sems, after_ref, src_dead, got_ref):
        for cp in _spread_copies(src_ref, land_ref, send_sems, recv_sems, per_peer, masks, arriving=True):
            cp.wait_send()
            cp.wait_recv()

    return pl.pallas_call(
        body, name=name, in_specs=(HBM_SPEC, HBM_SPEC, SEM_SPEC, SEM_SPEC, pl.BlockSpec(memory_space=pl.ANY)),
        out_shape=(pltpu.HBM(src_thru.shape, src_thru.dtype), pltpu.HBM(land_thru.shape, land_thru.dtype)),
        out_specs=(HBM_SPEC, HBM_SPEC), input_output_aliases={0: 0, 1: 1},
        compiler_params=pltpu.CompilerParams(has_side_effects=DATAFLOW_EFFECT),
    )(src_thru, land_thru, send_sems, recv_sems, after)[1]


def _relay_copies(land_ref, send_sems, recv_sems, arriving):
    me = _mesh_place()
    sibling = _flip(me, 1)
    out = []
    for i, mask in enumerate(OTHER_CHIPS):
        slot = _dev_index(_flip(sibling if arriving else me, mask))
        out.append(pltpu.make_async_remote_copy(
            src_ref=land_ref.at[slot], dst_ref=land_ref.at[slot], send_sem=send_sems.at[i], recv_sem=recv_sems.at[i],
            device_id=sibling, device_id_type=pl.DeviceIdType.MESH))
    return out


def relay_start(land, *, name):
    def body(land_ref, send_sems, recv_sems, land_thru):
        for cp in _relay_copies(land_ref, send_sems, recv_sems, arriving=False):
            cp.start()

    return pl.pallas_call(
        body, name=name, in_specs=(HBM_SPEC,),
        out_shape=(*_sem_pair(len(OTHER_CHIPS)), pltpu.HBM(land.shape, land.dtype)),
        out_specs=(SEM_SPEC, SEM_SPEC, HBM_SPEC), input_output_aliases={0: 2},
        compiler_params=pltpu.CompilerParams(has_side_effects=DATAFLOW_EFFECT),
    )(pltpu.with_memory_space_constraint(land, pltpu.HBM))


def relay_wait(started, *, name):
    send_sems, recv_sems, land_thru = started

    def body(land_ref, send_sems, recv_sems, got_ref):
        for cp in _relay_copies(land_ref, send_sems, recv_sems, arriving=True):
            cp.wait_send()
            cp.wait_recv()

    return pl.pallas_call(
        body, name=name, in_specs=(HBM_SPEC, SEM_SPEC, SEM_SPEC),
        out_shape=pltpu.HBM(land_thru.shape, land_thru.dtype), out_specs=HBM_SPEC, input_output_aliases={0: 0},
        compiler_params=pltpu.CompilerParams(has_side_effects=DATAFLOW_EFFECT),
    )(land_thru, send_sems, recv_sems)


def landing_zone(own_block, my_index):
    zone = lax.empty((N_DEV,) + own_block.shape, own_block.dtype)
    return lax.dynamic_update_slice(zone, own_block[None], (my_index,) + (0,) * own_block.ndim)


def sum_partials(parts, *, name, tr):
    _, rows, lanes = parts.shape
    assert rows % tr == 0

    def body(p_ref, g_ref):
        g = p_ref[0].astype(F32)
        for k in range(1, N_DEV):
            g = g + p_ref[k].astype(F32)
        g_ref[...] = g

    return _pcall(
        body, name=name, grid=(rows // tr,),
        in_specs=[pl.BlockSpec((N_DEV, tr, lanes), lambda i: (0, i, 0))],
        out_specs=pl.BlockSpec((tr, lanes), lambda i: (i, 0)), out_shape=jax.ShapeDtypeStruct((rows, lanes), F32),
        blocks=[((N_DEV, tr, lanes), parts.dtype), ((tr, lanes), F32)],
    )(parts)


def adamw(g, w, m, v, *, name, tr):
    nl, k, n = w.shape
    tr = max(c for c in range(8, min(tr, k) + 1, 8) if k % c == 0)
    bc1 = 1.0 - ADAM_B1 ** ADAM_STEP
    bc2 = 1.0 - ADAM_B2 ** ADAM_STEP

    def body(g_ref, w_ref, m_ref, v_ref, d_ref, mo_ref, vo_ref):
        gv = g_ref[...]
        m_new = ADAM_B1 * m_ref[...] + (1.0 - ADAM_B1) * gv
        v_new = ADAM_B2 * v_ref[...] + (1.0 - ADAM_B2) * (gv * gv)
        mo_ref[...] = m_new
        vo_ref[...] = v_new
        d_ref[...] = -ADAM_LR * ((m_new / bc1) / (jnp.sqrt(v_new / bc2) + ADAM_EPS) + ADAM_WD * w_ref[...])

    spec = pl.BlockSpec((1, tr, n), lambda l, i: (l, i, 0))
    shape = jax.ShapeDtypeStruct(w.shape, F32)
    return _pcall(
        body, name=name, grid=(nl, k // tr), in_specs=[spec] * 4, out_specs=[spec] * 3, out_shape=[shape] * 3,
        blocks=[((1, tr, n), F32)] * 7,
    )(g, w, m, v)


def pack_shards(tensors, layer, part):
    rows = []
    for name, r, by_cols, _ in SUBBLOCKS[part]:
        shard = tensors[name][layer].astype(BF16)
        rows.append((shard.T if by_cols else shard).reshape(r, LANES))
    return jnp.concatenate(rows, axis=0)


def unpack_gathered(gathered, part):
    ws, r0 = {}, 0
    for name, r, by_cols, (k, n) in SUBBLOCKS[part]:
        ws[name] = gathered[:, r0:r0 + r, :].reshape((n, k) if by_cols else (k, n))
        r0 += r
    return ws


def pack_full_grads(grads, part):
    return jnp.concatenate([grads[name].reshape(N_DEV, r, LANES) for name, r, _, _ in SUBBLOCKS[part]], axis=1)


def unpack_summed(summed, part, like):
    out, r0 = {}, 0
    for name, r, by_cols, _ in SUBBLOCKS[part]:
        k, n = like[name].shape[1:]
        out[name] = summed[r0:r0 + r].reshape((n, k) if by_cols else (k, n))
        r0 += r
    return out


def _row(v):
    return v.reshape(1, -1)


def ffn_fwd(x, h, w, pre, tag, next_gain):
    ga, gb, s = swiglu_fwd(h, w[pre + "_w_gate"], w[pre + "_w_up"], name=f"{tag}_gateup")
    out, h_next = matmul_res_norm(s, w[pre + "_w_down"], x, next_gain, scale=0.5, tm=512, name=f"{tag}_down")
    return out, h_next, (x, h, ga, gb, s)


def ffn_bwd_weights(dxb, saved, w, pre, tag):
    x, h, a, b, s = saved
    da, db = swiglu_bwd(dxb, w[pre + "_w_down"], a, b, scale=0.5, name=f"{tag}_dgateup")
    g_down = matmul([(s, dxb)], "tn", tm=1408, tn=1024, tk=2048, out_dtype=BF16, scale=0.5, name=f"{tag}_gdown")
    g_gate = matmul([(da, h)], "tn", tm=1408, tn=1024, tk=2048, out_dtype=BF16, name=f"{tag}_ggate")
    g_up = matmul([(db, h)], "tn", tm=1408, tn=1024, tk=2048, out_dtype=BF16, name=f"{tag}_gup")
    return {pre + "_w_gate": g_gate, pre + "_w_up": g_up, pre + "_w_down": g_down}, (da, db)


def ffn_bwd_input(dx, rest, saved, gain, w, pre, tag):
    da, db = rest
    x = saved[0]
    return matmul_rms_bwd([(da, w[pre + "_w_gate"]), (db, w[pre + "_w_up"])], x, gain, dx, tm=256, name=f"{tag}_dh")


def mixer_fwd(x, h, w, tables, tag, next_gain):
    proj = matmul([(h, w["w_in"])], "nt", tm=512, tn=1280, tk=1024, out_dtype=BF16, name=f"{tag}_in")
    qks, vs = rope_split(proj, tables, name=f"{tag}_rope")
    outs, lses = [], []
    for g in range(N_DIL_GROUPS):
        o, lse = dil_fwd(qks[g], vs[g], name=f"{tag}_dil{g}")
        outs.append(o)
        lses.append(lse)
    odil, lse = dil_merge(outs, lses, name=f"{tag}_merge")
    osb = sb_fwd(proj, name=f"{tag}_sb")
    y, u1, u2 = gate_fwd(odil, osb, w["w_proj_dil"], w["w_proj_sb"], proj, name=f"{tag}_gate")
    out, h_next = matmul_res_norm(y, w["w_out"], x, next_gain, scale=1.0, tm=512, name=f"{tag}_out")
    return out, h_next, (x, h, proj, qks, vs, odil, lse, osb, u1, u2, y)


def mixer_bwd_weights(dxb, saved, w, tables, tag):
    x, h, proj, qks, vs, odil, lse, osb, u1, u2, y = saved
    t = x.shape[0]
    g_out = matmul([(y, dxb)], "tn", tm=1024, tn=1024, tk=2048, out_dtype=BF16, name=f"{tag}_gout")
    du1, du2, dgate = gate_bwd(dxb, w["w_out"], u1, u2, proj, name=f"{tag}_dgate")
    g_pd = matmul([(du1, odil)], "tn", tm=1024, tn=256, tk=2048, out_dtype=BF16, name=f"{tag}_gpd")
    g_ps = matmul([(du2, osb)], "tn", tm=1024, tn=256, tk=2048, out_dtype=BF16, name=f"{tag}_gps")
    dodil = matmul([(du1, w["w_proj_dil"])], "nn", tm=512, tn=256, tk=1024, out_dtype=F32, name=f"{tag}_dodil")
    dosb = matmul([(du2, w["w_proj_sb"])], "nn", tm=512, tn=256, tk=1024, out_dtype=F32, name=f"{tag}_dosb")
    dsum, do_wide, lse_wide, dsum_wide = dil_bwd_prep(dodil, odil, lse, name=f"{tag}_dprep")
    dos = [dodil[None]] + list(do_wide)
    lss = [lse[None]] + list(lse_wide)
    dss = [dsum[None]] + list(dsum_wide)
    dqs, dks, dvs = [], [], []
    for g in range(N_DIL_GROUPS):
        dq, dk, dv = dil_bwd(qks[g], vs[g], dos[g], lss[g], dss[g], name=f"{tag}_ddil{g}")
        dqs.append(dq)
        dks.append(dk)
        dvs.append(dv)
    dqkv = rope_join(dqs, dks, dvs, tables, name=f"{tag}_drope")
    gtot = head_sums(dosb, osb, round_a=True, name=f"{tag}_gsum")
    dq_s, dk_s, dv_s = sb_bwd(proj, dosb, gtot, name=f"{tag}_dsb")
    dproj = jnp.concatenate([dqkv, dq_s.astype(BF16), dk_s.astype(BF16), dv_s.astype(BF16), dgate], axis=1)
    g_in = matmul([(dproj, h)], "tn", tm=1280, tn=1024, tk=2048, out_dtype=BF16, name=f"{tag}_gin")
    return {"w_in": g_in, "w_proj_dil": g_pd, "w_proj_sb": g_ps, "w_out": g_out}, dproj


def mixer_bwd_input(dx, dproj, saved, gain, w, tag):
    x = saved[0]
    return matmul_rms_bwd([(dproj, w["w_in"])], x, gain, dx, tm=256, name=f"{tag}_dh")


def kernel(x, norm_ffn1, ffn1_w_gate, ffn1_w_up, ffn1_w_down, norm_mix, w_in, w_proj_dil, w_proj_sb, w_out, norm_ffn2, ffn2_w_gate, ffn2_w_up, ffn2_w_down, norm_final, loss_target, m_norm_ffn1, m_ffn1_w_gate, m_ffn1_w_up, m_ffn1_w_down, m_norm_mix, m_w_in, m_w_proj_dil, m_w_proj_sb, m_w_out, m_norm_ffn2, m_ffn2_w_gate, m_ffn2_w_up, m_ffn2_w_down, m_norm_final, v_norm_ffn1, v_ffn1_w_gate, v_ffn1_w_up, v_ffn1_w_down, v_norm_mix, v_w_in, v_w_proj_dil, v_w_proj_sb, v_w_out, v_norm_ffn2, v_ffn2_w_gate, v_ffn2_w_up, v_ffn2_w_down, v_norm_final):
    args = dict(locals())
    names = [name for name, _, _, _ in PACK_LAYOUT]
    t = x.shape[1]
    xs = x.reshape(t, D_MODEL)
    target = loss_target.reshape(t, D_MODEL)
    tables = rope_tables(t)

    my_index = 4 * lax.axis_index("x") + 2 * lax.axis_index("y") + lax.axis_index("c")
    parts = [(l, p) for l in range(2) for p in SUBBLOCKS]
    w_shards = {n: args[n] for n in names}
    gains = {n: args[n] for n in NORM_ROWS}

    in_flight, order_token = {}, jnp.zeros((1, 1), F32)
    masks_of = {lp: (CHIP_PEERS if lp == parts[0] else ALL_PEERS) for lp in parts}
    for l, p in parts:
        packed = pack_shards(w_shards, l, p) + order_token.astype(BF16)
        in_flight[(l, p)] = spread_start(packed, landing_zone(packed, my_index), per_peer=False, masks=masks_of[(l, p)],
                                         name=f"gather_start_l{l}_{p}")
        order_token = in_flight[(l, p)][4][0:1, 0:1]

    def weights_of(l, p, after):
        land = spread_wait(in_flight[(l, p)], after, per_peer=False, masks=masks_of[(l, p)], name=f"gather_wait_l{l}_{p}")
        if masks_of[(l, p)] is CHIP_PEERS:
            land = relay_wait(relay_start(land, name=f"gather_relay_l{l}_{p}"), name=f"gather_relayed_l{l}_{p}")
        return unpack_gathered(land, p)

    saved, weights = {}, {}
    act = xs
    h = rms_fwd(xs, _row(gains["norm_ffn1"][0]) + order_token, name="l0_ffn1_norm")
    for i, (l, p) in enumerate(parts):
        weights[(l, p)] = weights_of(l, p, act)
        nl, np_ = parts[i + 1] if i + 1 < len(parts) else (None, None)
        next_gain = _row(gains["norm_" + np_][nl]) if np_ else None
        if p == "mix":
            act, h, saved[(l, p)] = mixer_fwd(act, h, weights[(l, p)], tables, f"l{l}_mix", next_gain)
        else:
            act, h, saved[(l, p)] = ffn_fwd(act, h, weights[(l, p)], p, f"l{l}_{p}", next_gain)
    dx, dxb, g_final, loss_part = final_loss(act, _row(norm_final), target, name="loss_head")

    gain_grads, sent = {}, {}
    order_token = jnp.zeros((1, 1), F32)
    for l, p in reversed(parts):
        w, sv = weights[(l, p)], saved[(l, p)]
        if p == "mix":
            gw, rest = mixer_bwd_weights(dxb, sv, w, tables, f"l{l}_mix")
        else:
            gw, rest = ffn_bwd_weights(dxb, sv, w, p, f"l{l}_{p}")
        slices = pack_full_grads(gw, p)
        own = lax.dynamic_index_in_dim(slices, my_index, 0, keepdims=False)
        sent[(l, p)] = spread_start(slices, landing_zone(own, my_index), per_peer=True, name=f"reduce_start_l{l}_{p}")
        gain = _row(gains["norm_" + p][l]) + sent[(l, p)][4][0:1, 0:1]
        if p == "mix":
            dx, dxb, gain_grads[("norm_mix", l)] = mixer_bwd_input(dx, rest, sv, gain, w, f"l{l}_mix")
        else:
            dx, dxb, gain_grads[("norm_" + p, l)] = ffn_bwd_input(dx, rest, sv, gain, w, p, f"l{l}_{p}")

    loss_row = jnp.pad(loss_part[:, :1], ((0, 0), (0, LANES - 1)))
    small = jnp.concatenate([gain_grads[(n, l)] for n in NORM_ROWS for l in range(2)] + [g_final, loss_row], axis=0)
    small_g = sum_partials(all_gather_rows(small, name="gather_gain_grads"), tr=8, name="sum_gain_grads")
    zero_row = jnp.zeros((1, LANES), F32)
    small_of = lambda pre: jnp.concatenate([args[pre + n] for n in NORM_ROWS] + [_row(args[pre + "norm_final"]), zero_row], axis=0)[None]
    small_out = adamw(small_g[None], small_of(""), small_of("m_"), small_of("v_"), tr=8, name="update_gains")
    small_all = [small_g] + [o[0] for o in small_out]

    grad_of, big_all = {}, [{}, {}, {}, {}]

    def receive(l, p, after):
        partials = spread_wait(sent[(l, p)], after, per_peer=True, name=f"reduce_wait_l{l}_{p}")
        summed = sum_partials(partials, tr=UPDATE_ROWS[p], name=f"sum_l{l}_{p}")
        for n, g in unpack_summed(summed, p, w_shards).items():
            grad_of.setdefault(n, [None, None])[l] = g

    def update(p):
        for n, _, by_cols, _ in SUBBLOCKS[p]:
            view = (lambda a: jnp.swapaxes(a, 1, 2)) if by_cols else (lambda a: a)
            g = jnp.stack(grad_of[n], axis=0)
            outs = adamw(g, view(args[n]), view(args["m_" + n]), view(args["v_" + n]), tr=512, name=f"update_{n}")
            for kind, arr in enumerate([g] + list(outs)):
                big_all[kind][n] = view(arr)
        return outs[0]

    last = parts[0]
    for l, p in reversed(parts[1:]):
        receive(l, p, dx)
    update("ffn2")
    done = update("mix")
    receive(*last, done)
    update("ffn1")

    def gains_of(s):
        out = {n: s[2 * i:2 * i + 2] for i, n in enumerate(NORM_ROWS)}
        out["norm_final"] = s[6]
        return out

    order = ["norm_ffn1", "ffn1_w_gate", "ffn1_w_up", "ffn1_w_down", "norm_mix", "w_in", "w_proj_dil", "w_proj_sb", "w_out",
             "norm_ffn2", "ffn2_w_gate", "ffn2_w_up", "ffn2_w_down", "norm_final"]
    results = []
    for kind in range(4):
        both = {**big_all[kind], **gains_of(small_all[kind])}
        results += [both[n] for n in order]
    loss = small_g[7, 0]
    return (loss, dx.reshape(1, t, D_MODEL), *results)
```

```python
import functools

import jax
import jax.numpy as jnp
from jax import lax
from jax.experimental import pallas as pl
from jax.experimental.pallas import tpu as pltpu

F32 = jnp.float32
BF16 = jnp.bfloat16

D_MODEL = 1024
HEAD_DIM = 64
GROUP_W = 256
D_IN = 5120
N_DIL_GROUPS = 3
DIL_SPAN = 128
DILATIONS = (1, 4, 16)
ROPE_THETA = 500000.0
ROPE_DIM = 16
RMS_EPS = 1e-6
ATT_SCALE = HEAD_DIM ** -0.5
QS_BLK, KS_BLK, VS_BLK = 9, 10, 11
GATE_DIL_BLK, GATE_SB_BLK = 3, 4

ADAM_LR, ADAM_B1, ADAM_B2, ADAM_EPS, ADAM_WD, ADAM_STEP = 0.001, 0.9, 0.999, 1e-08, 0.01, 10

N_DEV = 8
LANES = 1024
VMEM_PHYSICAL_V7X = 64 << 20
VMEM_TEMP_HEADROOM = 20 << 20

PACK_LAYOUT = (
    ("ffn1_w_gate", 352, True, (1024, 2816)),
    ("ffn1_w_up", 352, True, (1024, 2816)),
    ("ffn1_w_down", 352, False, (2816, 1024)),
    ("w_in", 640, True, (1024, 5120)),
    ("w_proj_dil", 32, True, (256, 1024)),
    ("w_proj_sb", 32, True, (256, 1024)),
    ("w_out", 128, False, (1024, 1024)),
    ("ffn2_w_gate", 352, True, (1024, 2816)),
    ("ffn2_w_up", 352, True, (1024, 2816)),
    ("ffn2_w_down", 352, False, (2816, 1024)),
)
SUBBLOCKS = {"ffn1": PACK_LAYOUT[0:3], "mix": PACK_LAYOUT[3:7], "ffn2": PACK_LAYOUT[7:10]}
NORM_ROWS = ("norm_ffn1", "norm_mix", "norm_ffn2")


def _nbytes(shape, dtype):
    n = 1
    for s in shape:
        n *= s
    return n * jnp.dtype(dtype).itemsize


def _pcall(body, *, name, grid, in_specs, out_specs, out_shape, blocks, scratch_shapes=(), scratch_bytes=0):
    need = 2 * sum(_nbytes(s, d) for s, d in blocks) + scratch_bytes + VMEM_TEMP_HEADROOM
    limit = min(need, VMEM_PHYSICAL_V7X - (4 << 20))
    in_hbm = lambda s: pltpu.HBM(s.shape, s.dtype)
    out_shape = [in_hbm(s) for s in out_shape] if isinstance(out_shape, (list, tuple)) else in_hbm(out_shape)
    call = pl.pallas_call(
        body, name=name, grid=grid, in_specs=in_specs, out_specs=out_specs, out_shape=out_shape,
        scratch_shapes=scratch_shapes,
        compiler_params=pltpu.CompilerParams(vmem_limit_bytes=limit),
    )
    return lambda *args: call(*[pltpu.with_memory_space_constraint(a, pltpu.HBM) for a in args])


def _dot(a, b, form):
    dn = {"nn": (((1,), (0,)), ((), ())), "nt": (((1,), (1,)), ((), ())), "tn": (((0,), (0,)), ((), ()))}[form]
    return lax.dot_general(a.astype(BF16), b.astype(BF16), dn, preferred_element_type=F32)


def _sigmoid(x):
    return 1.0 / (1.0 + jnp.exp(-x))


def matmul(pairs, form, *, tm, tn, tk, out_dtype, name, scale=1.0, res=None):
    a0, b0 = pairs[0]
    if form == "tn":
        kdim, m = a0.shape
        n = b0.shape[1]
    else:
        m, kdim = a0.shape
        n = b0.shape[1] if form == "nn" else b0.shape[0]
    tm, tn, tk = min(tm, m), min(tn, n), min(tk, kdim)
    assert m % tm == 0 and n % tn == 0 and kdim % tk == 0, (name, m, n, kdim, tm, tn, tk)
    nk = kdim // tk
    npairs = len(pairs)

    if form == "tn":
        a_blk, a_map = (tk, tm), (lambda j, i, k: (k, i))
    else:
        a_blk, a_map = (tm, tk), (lambda j, i, k: (i, k))
    if form == "nt":
        b_blk, b_map = (tn, tk), (lambda j, i, k: (j, k))
    else:
        b_blk, b_map = (tk, tn), (lambda j, i, k: (k, j))
    o_map = lambda j, i, k: (i, j)

    def body(*refs):
        ab = refs[:2 * npairs]
        rest = refs[2 * npairs:]
        if res is not None:
            r_ref, o_ref = rest[0], rest[1]
            rest = rest[2:]
        else:
            r_ref, o_ref = None, rest[0]
            rest = rest[1:]

        def partial_sum():
            p = _dot(ab[0][...], ab[1][...], form)
            for q in range(1, npairs):
                p = p + _dot(ab[2 * q][...], ab[2 * q + 1][...], form)
            return p

        def finish(acc):
            out = acc * scale if scale != 1.0 else acc
            if r_ref is not None:
                out = r_ref[...] + out
            o_ref[...] = out.astype(out_dtype)

        if nk == 1:
            finish(partial_sum())
        else:
            acc_ref = rest[0]
            k = pl.program_id(2)

            @pl.when(k == 0)
            def _():
                acc_ref[...] = partial_sum()

            @pl.when(k > 0)
            def _():
                acc_ref[...] += partial_sum()

            @pl.when(k == nk - 1)
            def _():
                finish(acc_ref[...])

    in_specs, args, blocks = [], [], []
    for a, b in pairs:
        in_specs += [pl.BlockSpec(a_blk, a_map), pl.BlockSpec(b_blk, b_map)]
        args += [a, b]
        blocks += [(a_blk, a.dtype), (b_blk, b.dtype)]
    if res is not None:
        in_specs.append(pl.BlockSpec((tm, tn), o_map))
        args.append(res)
        blocks.append(((tm, tn), res.dtype))
    blocks.append(((tm, tn), out_dtype))
    scratch = [pltpu.VMEM((tm, tn), F32)] if nk > 1 else []
    return _pcall(
        body, name=name, grid=(n // tn, m // tm, nk), in_specs=in_specs,
        out_specs=pl.BlockSpec((tm, tn), o_map), out_shape=jax.ShapeDtypeStruct((m, n), out_dtype),
        blocks=blocks, scratch_shapes=scratch, scratch_bytes=(tm * tn * 4 if nk > 1 else 0),
    )(*args)


def swiglu_fwd(h, wg_t, wu_t, *, name, tm=512, tn=1408):
    t, d = h.shape
    f = wg_t.shape[0]
    tm, tn = min(tm, t), min(tn, f)

    def body(h_ref, wg_ref, wu_ref, ga_ref, gb_ref, s_ref):
        hh = h_ref[...]
        a = _dot(hh, wg_ref[...], "nt")
        b = _dot(hh, wu_ref[...], "nt")
        sg = _sigmoid(a)
        silu = a * sg
        ga_ref[...] = (b * (sg * (1.0 + a * (1.0 - sg)))).astype(BF16)
        gb_ref[...] = silu.astype(BF16)
        s_ref[...] = (silu * b).astype(BF16)

    w_spec = pl.BlockSpec((tn, d), lambda j, i: (j, 0))
    o_spec = pl.BlockSpec((tm, tn), lambda j, i: (i, j))
    o_shape = jax.ShapeDtypeStruct((t, f), BF16)
    return _pcall(
        body, name=name, grid=(f // tn, t // tm),
        in_specs=[pl.BlockSpec((tm, d), lambda j, i: (i, 0)), w_spec, w_spec],
        out_specs=[o_spec, o_spec, o_spec], out_shape=[o_shape, o_shape, o_shape],
        blocks=[((tm, d), BF16), ((tn, d), BF16), ((tn, d), BF16)] + [((tm, tn), BF16)] * 3,
    )(h, wg_t, wu_t)


def swiglu_bwd(dyb, wd, ga, gb, *, name, scale, tm=512, tn=1408):
    t, d = dyb.shape
    f = wd.shape[0]
    tm, tn = min(tm, t), min(tn, f)

    def body(dy_ref, wd_ref, ga_ref, gb_ref, da_ref, db_ref):
        ds = _dot(dy_ref[...], wd_ref[...], "nt") * scale
        da_ref[...] = (ds * ga_ref[...].astype(F32)).astype(BF16)
        db_ref[...] = (ds * gb_ref[...].astype(F32)).astype(BF16)

    o_spec = pl.BlockSpec((tm, tn), lambda j, i: (i, j))
    o_shape = jax.ShapeDtypeStruct((t, f), BF16)
    return _pcall(
        body, name=name, grid=(f // tn, t // tm),
        in_specs=[pl.BlockSpec((tm, d), lambda j, i: (i, 0)), pl.BlockSpec((tn, d), lambda j, i: (j, 0)), o_spec, o_spec],
        out_specs=[o_spec, o_spec], out_shape=[o_shape, o_shape],
        blocks=[((tm, d), BF16), ((tn, d), BF16)] + [((tm, tn), BF16)] * 4,
    )(dyb, wd, ga, gb)


def gate_fwd(odil, osb, wpd_t, wps_t, proj, *, name, tm=512):
    t = odil.shape[0]
    tm = min(tm, t)

    def body(od_ref, os_ref, wpd_ref, wps_ref, g1_ref, g2_ref, y_ref, u1_ref, u2_ref):
        u1 = _dot(od_ref[...], wpd_ref[...], "nt")
        u2 = _dot(os_ref[...], wps_ref[...], "nt")
        y = _sigmoid(g1_ref[...].astype(F32)) * u1 + _sigmoid(g2_ref[...].astype(F32)) * u2
        y_ref[...] = y.astype(BF16)
        u1_ref[...] = u1.astype(BF16)
        u2_ref[...] = u2.astype(BF16)

    o_spec = pl.BlockSpec((tm, D_MODEL), lambda i: (i, 0))
    w_spec = pl.BlockSpec((D_MODEL, GROUP_W), lambda i: (0, 0))
    a_spec = pl.BlockSpec((tm, GROUP_W), lambda i: (i, 0))
    o_shape = jax.ShapeDtypeStruct((t, D_MODEL), BF16)
    return _pcall(
        body, name=name, grid=(t // tm,),
        in_specs=[a_spec, a_spec, w_spec, w_spec,
                  pl.BlockSpec((tm, D_MODEL), lambda i: (i, GATE_DIL_BLK)),
                  pl.BlockSpec((tm, D_MODEL), lambda i: (i, GATE_SB_BLK))],
        out_specs=[o_spec, o_spec, o_spec], out_shape=[o_shape, o_shape, o_shape],
        blocks=[((tm, GROUP_W), F32)] * 2 + [((D_MODEL, GROUP_W), BF16)] * 2 + [((tm, D_MODEL), BF16)] * 5,
    )(odil, osb, wpd_t, wps_t, proj, proj)


def gate_bwd(dxb, wout, u1, u2, proj, *, name, tm=512):
    t = dxb.shape[0]
    tm = min(tm, t)

    def body(dx_ref, w_ref, u1_ref, u2_ref, g1_ref, g2_ref, du1_ref, du2_ref, dg_ref):
        dy = _dot(dx_ref[...], w_ref[...], "nt")
        s1 = _sigmoid(g1_ref[...].astype(F32))
        s2 = _sigmoid(g2_ref[...].astype(F32))
        du1_ref[...] = (dy * s1).astype(BF16)
        du2_ref[...] = (dy * s2).astype(BF16)
        dg_ref[:, :D_MODEL] = (dy * u1_ref[...].astype(F32) * s1 * (1.0 - s1)).astype(BF16)
        dg_ref[:, D_MODEL:] = (dy * u2_ref[...].astype(F32) * s2 * (1.0 - s2)).astype(BF16)

    o_spec = pl.BlockSpec((tm, D_MODEL), lambda i: (i, 0))
    o_shape = jax.ShapeDtypeStruct((t, D_MODEL), BF16)
    return _pcall(
        body, name=name, grid=(t // tm,),
        in_specs=[o_spec, pl.BlockSpec((D_MODEL, D_MODEL), lambda i: (0, 0)), o_spec, o_spec,
                  pl.BlockSpec((tm, D_MODEL), lambda i: (i, GATE_DIL_BLK)),
                  pl.BlockSpec((tm, D_MODEL), lambda i: (i, GATE_SB_BLK))],
        out_specs=[o_spec, o_spec, pl.BlockSpec((tm, 2 * D_MODEL), lambda i: (i, 0))],
        out_shape=[o_shape, o_shape, jax.ShapeDtypeStruct((t, 2 * D_MODEL), BF16)],
        blocks=[((tm, D_MODEL), BF16)] * 9 + [((D_MODEL, D_MODEL), BF16)],
    )(dxb, wout, u1, u2, proj, proj)


def rms_fwd(x, gain, *, name, tm=512):
    t, d = x.shape
    tm = min(tm, t)

    def body(x_ref, g_ref, h_ref):
        xv = x_ref[...]
        rstd = lax.rsqrt(jnp.mean(xv * xv, axis=1, keepdims=True) + RMS_EPS)
        h_ref[...] = (xv * rstd * g_ref[...]).astype(BF16)

    return _pcall(
        body, name=name, grid=(t // tm,),
        in_specs=[pl.BlockSpec((tm, d), lambda i: (i, 0)), pl.BlockSpec((1, d), lambda i: (0, 0))],
        out_specs=pl.BlockSpec((tm, d), lambda i: (i, 0)), out_shape=jax.ShapeDtypeStruct((t, d), BF16),
        blocks=[((tm, d), F32), ((tm, d), BF16)],
    )(x, gain)


def matmul_res_norm(a, b, res, next_gain, *, scale, tm, name):
    t, k = a.shape
    d = b.shape[1]
    tm = min(tm, t)
    with_norm = next_gain is not None

    def body(a_ref, b_ref, r_ref, *rest):
        out = r_ref[...] + _dot(a_ref[...], b_ref[...], "nn") * scale
        if with_norm:
            g_ref, o_ref, h_ref = rest
            rstd = lax.rsqrt(jnp.mean(out * out, axis=1, keepdims=True) + RMS_EPS)
            h_ref[...] = (out * rstd * g_ref[...]).astype(BF16)
        else:
            o_ref, = rest
        o_ref[...] = out

    row = pl.BlockSpec((tm, d), lambda i: (i, 0))
    in_specs = [pl.BlockSpec((tm, k), lambda i: (i, 0)), pl.BlockSpec((k, d), lambda i: (0, 0)), row]
    args = [a, b, res]
    out_specs, out_shape = [row], [jax.ShapeDtypeStruct((t, d), F32)]
    if with_norm:
        in_specs.append(pl.BlockSpec((1, d), lambda i: (0, 0)))
        args.append(next_gain)
        out_specs.append(row)
        out_shape.append(jax.ShapeDtypeStruct((t, d), BF16))
    outs = _pcall(
        body, name=name, grid=(t // tm,), in_specs=in_specs, out_specs=out_specs, out_shape=out_shape,
        blocks=[((tm, k), a.dtype), ((k, d), b.dtype), ((tm, d), F32), ((tm, d), F32), ((tm, d), BF16)],
    )(*args)
    return (outs[0], outs[1]) if with_norm else (outs[0], None)


def _rms_bwd_rows(dhv, xv, g, drv):
    rstd = lax.rsqrt(jnp.mean(xv * xv, axis=1, keepdims=True) + RMS_EPS)
    xh = xv * rstd
    dxh = dhv * g
    dx = drv + rstd * (dxh - xh * jnp.mean(dxh * xh, axis=1, keepdims=True))
    return dx, jnp.sum(dhv * xh, axis=0, keepdims=True)


def matmul_rms_bwd(pairs, x, gain, dres, *, tm, name):
    t, d = x.shape
    tm = min(tm, t)
    npairs = len(pairs)

    def body(*refs):
        ab = refs[:2 * npairs]
        x_ref, g_ref, dr_ref, dx_ref, dxb_ref, dg_ref = refs[2 * npairs:]
        dh = _dot(ab[0][...], ab[1][...], "nn")
        for q in range(1, npairs):
            dh = dh + _dot(ab[2 * q][...], ab[2 * q + 1][...], "nn")
        dx, part = _rms_bwd_rows(dh, x_ref[...], g_ref[...], dr_ref[...])
        dx_ref[...] = dx
        dxb_ref[...] = dx.astype(BF16)

        @pl.when(pl.program_id(0) == 0)
        def _():
            dg_ref[...] = part

        @pl.when(pl.program_id(0) > 0)
        def _():
            dg_ref[...] += part

    in_specs, args, blocks = [], [], []
    for a, b in pairs:
        k = a.shape[1]
        in_specs += [pl.BlockSpec((tm, k), lambda i: (i, 0)), pl.BlockSpec((k, d), lambda i: (0, 0))]
        args += [a, b]
        blocks += [((tm, k), a.dtype), ((k, d), b.dtype)]
    row = pl.BlockSpec((tm, d), lambda i: (i, 0))
    vec = pl.BlockSpec((1, d), lambda i: (0, 0))
    return _pcall(
        body, name=name, grid=(t // tm,), in_specs=in_specs + [row, vec, row], out_specs=[row, row, vec],
        out_shape=[jax.ShapeDtypeStruct((t, d), F32), jax.ShapeDtypeStruct((t, d), BF16), jax.ShapeDtypeStruct((1, d), F32)],
        blocks=blocks + [((tm, d), F32)] * 3 + [((tm, d), BF16)],
    )(*args, x, gain, dres)


def final_loss(x, gain, target, *, name, tm=512):
    t, d = x.shape
    tm = min(tm, t)

    def body(x_ref, g_ref, t_ref, dx_ref, dxb_ref, dg_ref, loss_ref):
        xv = x_ref[...]
        g = g_ref[...]
        rstd = lax.rsqrt(jnp.mean(xv * xv, axis=1, keepdims=True) + RMS_EPS)
        xh = xv * rstd
        err = xh * g - t_ref[...]
        dy = err * (1.0 / d)
        dxh = dy * g
        dx = rstd * (dxh - xh * jnp.mean(dxh * xh, axis=1, keepdims=True))
        dx_ref[...] = dx
        dxb_ref[...] = dx.astype(BF16)
        part = jnp.sum(dy * xh, axis=0, keepdims=True)
        sq = jnp.sum(jnp.sum(err * err, axis=1, keepdims=True), axis=0, keepdims=True) * (0.5 / d)
        lpart = jnp.broadcast_to(sq, (1, 128))

        @pl.when(pl.program_id(0) == 0)
        def _():
            dg_ref[...] = part
            loss_ref[...] = lpart

        @pl.when(pl.program_id(0) > 0)
        def _():
            dg_ref[...] += part
            loss_ref[...] += lpart

    row = pl.BlockSpec((tm, d), lambda i: (i, 0))
    vec = pl.BlockSpec((1, d), lambda i: (0, 0))
    return _pcall(
        body, name=name, grid=(t // tm,), in_specs=[row, vec, row],
        out_specs=[row, row, vec, pl.BlockSpec((1, 128), lambda i: (0, 0))],
        out_shape=[jax.ShapeDtypeStruct((t, d), F32), jax.ShapeDtypeStruct((t, d), BF16),
                   jax.ShapeDtypeStruct((1, d), F32), jax.ShapeDtypeStruct((1, 128), F32)],
        blocks=[((tm, d), F32)] * 3 + [((tm, d), BF16)],
    )(x, gain, target)


def rope_tables(t):
    pos = jnp.arange(t, dtype=F32)
    inv_freq = ROPE_THETA ** (-jnp.arange(0, ROPE_DIM, 2, dtype=F32) / ROPE_DIM)
    ang = pos[:, None] * inv_freq[None, :]
    cos, sin = jnp.cos(ang), jnp.sin(ang)
    half = ROPE_DIM // 2
    pad = HEAD_DIM - ROPE_DIM
    one_head = lambda lo, hi, fill: jnp.concatenate([lo, hi, jnp.full((t, pad), fill, F32)], axis=1)
    zeros = jnp.zeros((t, half), F32)
    c = one_head(cos, cos, 1.0)
    sa = one_head(-sin, zeros, 0.0)
    sb = one_head(zeros, sin, 0.0)
    two = lambda a: jnp.concatenate([a, a], axis=1)
    return two(c), two(sa), two(sb)


def _rotate(xv, cv, sav, sbv):
    halves = []
    for half in range(2):
        x = xv[:, 128 * half:128 * (half + 1)]
        halves.append(x * cv + pltpu.roll(x, 120, 1) * sav + pltpu.roll(x, 8, 1) * sbv)
    return jnp.concatenate(halves, axis=1)


STAGE_CHUNKS = 4


def _stage(tm):
    return dict(scratch_shapes=[pltpu.VMEM((STAGE_CHUNKS, tm, 128), F32)], scratch_bytes=STAGE_CHUNKS * tm * 128 * 4)


def _split_residues(stage_ref, val, out_ref, d, col, dtype):
    rows, width = val.shape
    if d == 1:
        out_ref[0, :, col:col + width] = val.astype(dtype)
        return
    chunks = width // 128
    for c in range(chunks):
        stage_ref[c] = val[:, 128 * c:128 * (c + 1)]
    for r in range(d):
        for c in range(chunks):
            out_ref[r, :, col + 128 * c:col + 128 * (c + 1)] = stage_ref[c, pl.ds(r, rows // d, stride=d), :].astype(dtype)


def _join_residues(stage_ref, in_ref, d, col=0, width=GROUP_W):
    if d == 1:
        return in_ref[0, :, col:col + width].astype(F32)
    rows = in_ref.shape[1] * d
    chunks = width // 128
    for r in range(d):
        for c in range(chunks):
            stage_ref[c, pl.ds(r, rows // d, stride=d), :] = in_ref[r, :, col + 128 * c:col + 128 * (c + 1)].astype(F32)
    return jnp.concatenate([stage_ref[c] for c in range(chunks)], axis=1)


def rope_split(proj, tables, *, name, tm=512):
    c, sa, sb = tables
    t = c.shape[0]
    tm = min(tm, t)

    def body(*refs):
        pieces = refs[0:9]
        c_ref, sa_ref, sb_ref = refs[9:12]
        qk_out, v_out = refs[12:15], refs[15:18]
        stage = refs[18]
        cv, sav, sbv = c_ref[...], sa_ref[...], sb_ref[...]
        for g, d in enumerate(DILATIONS):
            for kind in range(3):
                xv = pieces[3 * kind + g][...].astype(F32)
                if kind < 2:
                    _split_residues(stage, _rotate(xv, cv, sav, sbv), qk_out[g], d, GROUP_W * kind, BF16)
                else:
                    _split_residues(stage, xv, v_out[g], d, 0, BF16)

    tab = pl.BlockSpec((tm, 128), lambda i: (i, 0))
    in_specs = [pl.BlockSpec((tm, GROUP_W), functools.partial(lambda i, cb: (i, cb), cb=cb)) for cb in range(9)]
    out_specs = ([pl.BlockSpec((d, tm // d, 2 * GROUP_W), lambda i: (0, i, 0)) for d in DILATIONS]
                 + [pl.BlockSpec((d, tm // d, GROUP_W), lambda i: (0, i, 0)) for d in DILATIONS])
    out_shape = ([jax.ShapeDtypeStruct((d, t // d, 2 * GROUP_W), BF16) for d in DILATIONS]
                 + [jax.ShapeDtypeStruct((d, t // d, GROUP_W), BF16) for d in DILATIONS])
    outs = _pcall(
        body, name=name, grid=(t // tm,), in_specs=in_specs + [tab, tab, tab], out_specs=out_specs, out_shape=out_shape,
        blocks=[((tm, GROUP_W), BF16)] * 18 + [((tm, 128), F32)] * 3,
        **_stage(tm),
    )(*([proj] * 9), c, sa, sb)
    return outs[0:3], outs[3:6]


def rope_join(dqs, dks, dvs, tables, *, name, tm=512):
    c, sa, sb = tables
    t = c.shape[0]
    tm = min(tm, t)

    def body(*refs):
        pieces = refs[0:9]
        c_ref, sa_ref, sb_ref = refs[9:12]
        o_ref, stage = refs[12], refs[13]
        cv, sav, sbv = c_ref[...], -sa_ref[...], -sb_ref[...]
        for kind in range(3):
            for g, d in enumerate(DILATIONS):
                xv = _join_residues(stage, pieces[3 * kind + g], d)
                if kind < 2:
                    xv = _rotate(xv, cv, sav, sbv)
                col = GROUP_W * (3 * kind + g)
                o_ref[:, col:col + GROUP_W] = xv.astype(BF16)

    tab = pl.BlockSpec((tm, 128), lambda i: (i, 0))
    in_specs = [pl.BlockSpec((d, tm // d, GROUP_W), lambda i: (0, i, 0)) for _ in range(3) for d in DILATIONS]
    return _pcall(
        body, name=name, grid=(t // tm,), in_specs=in_specs + [tab, tab, tab],
        out_specs=pl.BlockSpec((tm, 9 * GROUP_W), lambda i: (i, 0)), out_shape=jax.ShapeDtypeStruct((t, 9 * GROUP_W), BF16),
        blocks=[((tm, GROUP_W), F32)] * 9 + [((tm, 128), F32)] * 3 + [((tm, 9 * GROUP_W), BF16)],
        **_stage(tm),
    )(*dqs, *dks, *dvs, c, sa, sb)


def _head_mask(h):
    lane = lax.broadcasted_iota(jnp.int32, (1, GROUP_W), 1)
    return (lane // HEAD_DIM) == h


def _band_masks(heads):
    ri = lax.broadcasted_iota(jnp.int32, (heads * DIL_SPAN, DIL_SPAN), 0) % DIL_SPAN
    ci = lax.broadcasted_iota(jnp.int32, (heads * DIL_SPAN, DIL_SPAN), 1)
    return ci <= ri, ci >= ri


def dil_fwd(qk, v, *, name):
    d, nsub, _ = qk.shape
    nblk = nsub // DIL_SPAN

    def body(q_ref, kc_ref, kp_ref, vc_ref, vp_ref, o_ref, lse_ref):
        nb = pl.program_id(1)
        own, prev = _band_masks(1)
        prev = prev & (nb > 0)
        q, kc, kp, vc, vp = q_ref[0] * ATT_SCALE, kc_ref[0], kp_ref[0], vc_ref[0], vp_ref[0]
        o_acc = jnp.zeros((DIL_SPAN, GROUP_W), F32)
        for h in range(4):
            hm = _head_mask(h)
            qh = jnp.where(hm, q, jnp.zeros_like(q))
            sc = jnp.where(own, _dot(qh, kc, "nt"), -jnp.inf)
            sp = jnp.where(prev, _dot(qh, kp, "nt"), -jnp.inf)
            m = jnp.maximum(jnp.max(sc, axis=1, keepdims=True), jnp.max(sp, axis=1, keepdims=True))
            pc = jnp.exp(sc - m)
            pp = jnp.exp(sp - m)
            den = jnp.sum(pc, axis=1, keepdims=True) + jnp.sum(pp, axis=1, keepdims=True)
            oh = (_dot(pc, vc, "nn") + _dot(pp, vp, "nn")) / den
            o_acc = jnp.where(hm, oh, o_acc)
            lse_ref[0, :, 128 * h:128 * (h + 1)] = jnp.broadcast_to(m + jnp.log(den), (DIL_SPAN, 128))
        o_ref[0] = o_acc

    blk = (1, DIL_SPAN, GROUP_W)
    sblk = (1, DIL_SPAN, 512)
    prv = lambda nb: jnp.maximum(nb - 1, 0)
    return _pcall(
        body, name=name, grid=(d, nblk),
        in_specs=[pl.BlockSpec(blk, lambda r, nb: (r, nb, 0)),
                  pl.BlockSpec(blk, lambda r, nb: (r, nb, 1)),
                  pl.BlockSpec(blk, lambda r, nb: (r, prv(nb), 1)),
                  pl.BlockSpec(blk, lambda r, nb: (r, nb, 0)),
                  pl.BlockSpec(blk, lambda r, nb: (r, prv(nb), 0))],
        out_specs=[pl.BlockSpec(blk, lambda r, nb: (r, nb, 0)), pl.BlockSpec(sblk, lambda r, nb: (r, nb, 0))],
        out_shape=[jax.ShapeDtypeStruct((d, nsub, GROUP_W), F32), jax.ShapeDtypeStruct((d, nsub, 512), F32)],
        blocks=[(blk, BF16)] * 5 + [(blk, F32), (sblk, F32)],
    )(qk, qk, qk, v, v)


def dil_merge(outs, lses, *, name, tm=512):
    t = outs[0].shape[0] * outs[0].shape[1]
    tm = min(tm, t)

    def body(o0, o1, o2, l0, l1, l2, o_ref, lse_ref, stage):
        ls = [_join_residues(stage, l, d, 0, 512) for l, d in zip((l0, l1, l2), DILATIONS)]
        m = jnp.maximum(jnp.maximum(ls[0], ls[1]), ls[2])
        tot = m + jnp.log(jnp.exp(ls[0] - m) + jnp.exp(ls[1] - m) + jnp.exp(ls[2] - m))
        lse_ref[...] = tot
        lane = lax.broadcasted_iota(jnp.int32, (1, 128), 1)
        first = lane < HEAD_DIM
        acc = jnp.zeros((tm, GROUP_W), F32)
        for og, lg, d in zip((o0, o1, o2), ls, DILATIONS):
            w = jnp.exp(lg - tot)
            wide = jnp.concatenate([jnp.where(first, w[:, 0:128], w[:, 128:256]),
                                    jnp.where(first, w[:, 256:384], w[:, 384:512])], axis=1)
            acc = acc + wide * _join_residues(stage, og, d)
        o_ref[...] = acc

    o_in = [pl.BlockSpec((d, tm // d, GROUP_W), lambda i: (0, i, 0)) for d in DILATIONS]
    l_in = [pl.BlockSpec((d, tm // d, 512), lambda i: (0, i, 0)) for d in DILATIONS]
    return _pcall(
        body, name=name, grid=(t // tm,), in_specs=o_in + l_in,
        out_specs=[pl.BlockSpec((tm, GROUP_W), lambda i: (i, 0)), pl.BlockSpec((tm, 512), lambda i: (i, 0))],
        out_shape=[jax.ShapeDtypeStruct((t, GROUP_W), F32), jax.ShapeDtypeStruct((t, 512), F32)],
        blocks=[((tm, GROUP_W), F32)] * 4 + [((tm, 512), F32)] * 4,
        **_stage(tm),
    )(*outs, *lses)


def dil_bwd_prep(do, o, lse, *, name, tm=512):
    t = do.shape[0]
    tm = min(tm, t)
    wide = DILATIONS[1:]

    def body(do_ref, o_ref, lse_ref, ds_ref, *rest):
        do_out, lse_out, ds_out = rest[0:2], rest[2:4], rest[4:6]
        stage = rest[6]
        dov = do_ref[...]
        prod = dov * o_ref[...]
        for h in range(4):
            s = jnp.sum(jnp.where(_head_mask(h), prod, 0.0), axis=1, keepdims=True)
            ds_ref[:, 128 * h:128 * (h + 1)] = jnp.broadcast_to(s, (tm, 128))
        for i, d in enumerate(wide):
            _split_residues(stage, dov, do_out[i], d, 0, BF16)
            _split_residues(stage, lse_ref[...], lse_out[i], d, 0, F32)
            _split_residues(stage, ds_ref[...], ds_out[i], d, 0, F32)

    nat = lambda w: pl.BlockSpec((tm, w), lambda i: (i, 0))
    res = lambda d, w: pl.BlockSpec((d, tm // d, w), lambda i: (0, i, 0))
    shape = lambda d, w, dt: jax.ShapeDtypeStruct((d, t // d, w), dt)
    outs = _pcall(
        body, name=name, grid=(t // tm,), in_specs=[nat(GROUP_W), nat(GROUP_W), nat(512)],
        out_specs=[nat(512)] + [res(d, GROUP_W) for d in wide] + [res(d, 512) for d in wide] * 2,
        out_shape=([jax.ShapeDtypeStruct((t, 512), F32)] + [shape(d, GROUP_W, BF16) for d in wide]
                   + [shape(d, 512, F32) for d in wide] * 2),
        blocks=[((tm, GROUP_W), F32)] * 3 + [((tm, 512), F32)] * 6,
        **_stage(tm),
    )(do, o, lse)
    return outs[0], outs[1:3], outs[3:5], outs[5:7]


def head_sums(a, b, *, name, round_a=False, tm=512):
    t = a.shape[0]
    tm = min(tm, t)

    def body(a_ref, b_ref, o_ref):
        av = a_ref[...]
        if round_a:
            av = av.astype(BF16).astype(F32)
        prod = av * b_ref[...]
        for h in range(4):
            s = jnp.sum(jnp.where(_head_mask(h), prod, 0.0), axis=1, keepdims=True)
            o_ref[:, 128 * h:128 * (h + 1)] = jnp.broadcast_to(s, (tm, 128))

    spec = pl.BlockSpec((tm, GROUP_W), lambda i: (i, 0))
    return _pcall(
        body, name=name, grid=(t // tm,), in_specs=[spec, spec],
        out_specs=pl.BlockSpec((tm, 512), lambda i: (i, 0)), out_shape=jax.ShapeDtypeStruct((t, 512), F32),
        blocks=[((tm, GROUP_W), F32)] * 2 + [((tm, 512), F32)],
    )(a, b)


def dil_bwd(qk, v, do, lse, dsum, *, name):
    d, nsub, _ = qk.shape
    nblk = nsub // DIL_SPAN

    def body(qa_ref, qb_ref, kc_ref, kp_ref, vc_ref, vp_ref, doa_ref, dob_ref, la_ref, lb_ref, sa_ref, sb_ref,
             dq_ref, dk_ref, dv_ref):
        nb = pl.program_id(1)
        own, band = _band_masks(4)
        prev = band & (nb > 0)
        nxt = band & (nb < nblk - 1)
        kc, kp, vc, vp = kc_ref[0], kp_ref[0], vc_ref[0], vp_ref[0]
        qas, qbs = _stack_heads(qa_ref[0] * ATT_SCALE), _stack_heads(qb_ref[0] * ATT_SCALE)
        das, dbs = _stack_heads(doa_ref[0].astype(BF16)), _stack_heads(dob_ref[0].astype(BF16))
        stat = lambda ref: jnp.concatenate([ref[0, :, 128 * h:128 * (h + 1)] for h in range(4)], axis=0)
        la, lb, sa, sb = stat(la_ref), stat(lb_ref), stat(sa_ref), stat(sb_ref)

        def probs(qs, ds_, k, v, mask, l, s):
            p = jnp.where(mask, jnp.exp(_dot(qs, k, "nt") - l), 0.0)
            dsc = p * (_dot(ds_, v, "nt") - s)
            return p.astype(BF16), dsc.astype(BF16)

        p_cc, ds_cc = probs(qas, das, kc, vc, own, la, sa)
        _, ds_cp = probs(qas, das, kp, vp, prev, la, sa)
        p_nc, ds_nc = probs(qbs, dbs, kc, vc, nxt, lb, sb)
        dq_ref[0] = _unstack_heads(_dot(ds_cc, kc, "nn") + _dot(ds_cp, kp, "nn"), DIL_SPAN) * ATT_SCALE
        dk_ref[0] = _dot(ds_cc, qas, "tn") + _dot(ds_nc, qbs, "tn")
        dv_ref[0] = _dot(p_cc, das, "tn") + _dot(p_nc, dbs, "tn")

    blk = (1, DIL_SPAN, GROUP_W)
    sblk = (1, DIL_SPAN, 512)
    prv = lambda nb: jnp.maximum(nb - 1, 0)
    nxt_ = lambda nb: jnp.minimum(nb + 1, nblk - 1)
    cur_at = lambda c: pl.BlockSpec(blk, functools.partial(lambda r, nb, c: (r, nb, c), c=c))
    prv_at = lambda c: pl.BlockSpec(blk, functools.partial(lambda r, nb, c: (r, prv(nb), c), c=c))
    nxt_at = lambda c: pl.BlockSpec(blk, functools.partial(lambda r, nb, c: (r, nxt_(nb), c), c=c))
    s_cur = pl.BlockSpec(sblk, lambda r, nb: (r, nb, 0))
    s_nxt = pl.BlockSpec(sblk, lambda r, nb: (r, nxt_(nb), 0))
    o_spec = pl.BlockSpec(blk, lambda r, nb: (r, nb, 0))
    o_shape = jax.ShapeDtypeStruct((d, nsub, GROUP_W), F32)
    return _pcall(
        body, name=name, grid=(d, nblk),
        in_specs=[cur_at(0), nxt_at(0), cur_at(1), prv_at(1), cur_at(0), prv_at(0), cur_at(0), nxt_at(0),
                  s_cur, s_nxt, s_cur, s_nxt],
        out_specs=[o_spec, o_spec, o_spec], out_shape=[o_shape, o_shape, o_shape],
        blocks=[(blk, BF16)] * 6 + [(blk, F32)] * 5 + [(sblk, F32)] * 4,
    )(qk, qk, qk, qk, v, v, do, do, lse, lse, dsum, dsum)


def _tri_dot(x, b):
    hi = x.astype(BF16)
    lo = (x - hi.astype(F32)).astype(BF16)
    return _dot(jnp.concatenate([hi, lo], axis=1), jnp.concatenate([b, b], axis=0), "nn")


SB_TILE = 256


def _stack_heads(a):
    return jnp.concatenate([jnp.where(_head_mask(h), a, jnp.zeros_like(a)) for h in range(4)], axis=0)


def _unstack_heads(acc, rows):
    out = acc[0:rows]
    for h in range(1, 4):
        out = jnp.where(_head_mask(h), acc[h * rows:(h + 1) * rows], out)
    return out


def _tri_masks(n):
    ri = lax.broadcasted_iota(jnp.int32, (n, n), 0)
    ci = lax.broadcasted_iota(jnp.int32, (n, n), 1)
    return (ri > ci).astype(BF16), (ri >= ci).astype(BF16)


def _sb_weights(qs, kt, after, c_keep, diagonal):
    z = _dot(qs, kt, "nt")
    lbeta = jnp.minimum(z, 0.0) - jnp.log(1.0 + jnp.exp(-jnp.abs(z)))
    lkeep = lbeta - z
    past = None
    if diagonal:
        n = SB_TILE
        past = lax.broadcasted_iota(jnp.int32, z.shape, 1) < lax.broadcasted_iota(jnp.int32, z.shape, 0) % n
        lkeep = jnp.where(past, lkeep, 0.0)
    w = jnp.exp(lbeta + _tri_dot(lkeep, after) + c_keep)
    if diagonal:
        w = jnp.where(past, w, 0.0)
    return z, past, lbeta, lkeep, w


def sb_fwd(proj, *, name):
    t = proj.shape[0]
    n = SB_TILE
    assert t % n == 0

    def body(q_ref, k_ref, v_ref, o_ref, acc_ref):
        qb = pl.program_id(0)
        qs = _stack_heads(q_ref[...] * ATT_SCALE)
        after, _ = _tri_masks(n)

        def tile(off, diagonal, c_keep):
            kt = k_ref[pl.ds(off, n), :]
            vt = v_ref[pl.ds(off, n), :]
            _, _, _, lkeep, w = _sb_weights(qs, kt, after, c_keep, diagonal)
            pv = _tri_dot(w, vt)
            if diagonal:
                acc_ref[...] = pv
            else:
                acc_ref[...] += pv
            return c_keep + jnp.sum(lkeep, axis=1, keepdims=True)

        c0 = tile(pl.multiple_of(qb * n, n), True, jnp.zeros((4 * n, 1), F32))
        lax.fori_loop(0, qb, lambda it, c: tile(pl.multiple_of((qb - 1 - it) * n, n), False, c), c0)
        o_ref[...] = _unstack_heads(acc_ref[...], n)

    full = lambda cb: pl.BlockSpec((t, GROUP_W), functools.partial(lambda i, cb: (0, cb), cb=cb))
    return _pcall(
        body, name=name, grid=(t // n,),
        in_specs=[pl.BlockSpec((n, GROUP_W), lambda i: (i, QS_BLK)), full(KS_BLK), full(VS_BLK)],
        out_specs=pl.BlockSpec((n, GROUP_W), lambda i: (i, 0)), out_shape=jax.ShapeDtypeStruct((t, GROUP_W), F32),
        blocks=[((n, GROUP_W), BF16), ((t, GROUP_W), BF16), ((t, GROUP_W), BF16), ((n, GROUP_W), F32)],
        scratch_shapes=[pltpu.VMEM((4 * n, GROUP_W), F32)], scratch_bytes=4 * n * GROUP_W * 4,
    )(proj, proj, proj)


def sb_bwd(proj, do, gtot, *, name):
    t = proj.shape[0]
    n = SB_TILE
    assert t % n == 0

    def body(q_ref, k_ref, v_ref, do_ref, gt_ref, dq_ref, dk_ref, dv_ref, acc_ref):
        qb = pl.program_id(0)

        @pl.when(qb == 0)
        def _():
            dk_ref[...] = jnp.zeros_like(dk_ref)
            dv_ref[...] = jnp.zeros_like(dv_ref)

        qs = _stack_heads(q_ref[...] * ATT_SCALE)
        dos = _stack_heads(do_ref[...].astype(BF16))
        gt = jnp.concatenate([jnp.max(gt_ref[:, 128 * h:128 * (h + 1)], axis=1, keepdims=True) for h in range(4)], axis=0)
        after, from_on = _tri_masks(n)

        def tile(off, diagonal, carry):
            c_keep, c_g = carry
            kt = k_ref[pl.ds(off, n), :]
            vt = v_ref[pl.ds(off, n), :]
            z, past, lbeta, lkeep, w = _sb_weights(qs, kt, after, c_keep, diagonal)
            gw = w * _dot(dos, vt, "nt")
            big_g = gt - (_tri_dot(gw, from_on) + c_g)
            dz = gw * jnp.exp(lbeta - z) - big_g * jnp.exp(lbeta)
            if diagonal:
                dz = jnp.where(past, dz, 0.0)
            dz = dz.astype(BF16)
            dk_ref[pl.ds(off, n), :] += _dot(dz, qs, "tn")
            dv_ref[pl.ds(off, n), :] += _dot(w, dos, "tn")
            dq = _dot(dz, kt, "nn")
            if diagonal:
                acc_ref[...] = dq
            else:
                acc_ref[...] += dq
            return c_keep + jnp.sum(lkeep, axis=1, keepdims=True), c_g + jnp.sum(gw, axis=1, keepdims=True)

        zero_col = jnp.zeros((4 * n, 1), F32)
        c0 = tile(pl.multiple_of(qb * n, n), True, (zero_col, zero_col))
        lax.fori_loop(0, qb, lambda it, c: tile(pl.multiple_of((qb - 1 - it) * n, n), False, c), c0)
        dq_ref[...] = _unstack_heads(acc_ref[...], n) * ATT_SCALE

    full = lambda cb: pl.BlockSpec((t, GROUP_W), functools.partial(lambda i, cb: (0, cb), cb=cb))
    whole = pl.BlockSpec((t, GROUP_W), lambda i: (0, 0))
    rowblk = pl.BlockSpec((n, GROUP_W), lambda i: (i, 0))
    shape = jax.ShapeDtypeStruct((t, GROUP_W), F32)
    return _pcall(
        body, name=name, grid=(t // n,),
        in_specs=[pl.BlockSpec((n, GROUP_W), lambda i: (i, QS_BLK)), full(KS_BLK), full(VS_BLK), rowblk,
                  pl.BlockSpec((n, 512), lambda i: (i, 0))],
        out_specs=[rowblk, whole, whole], out_shape=[shape, shape, shape],
        blocks=[((n, GROUP_W), BF16), ((t, GROUP_W), BF16), ((t, GROUP_W), BF16), ((n, GROUP_W), F32),
                ((n, 512), F32), ((n, GROUP_W), F32), ((t, GROUP_W), F32), ((t, GROUP_W), F32)],
        scratch_shapes=[pltpu.VMEM((4 * n, GROUP_W), F32)], scratch_bytes=4 * n * GROUP_W * 4,
    )(proj, proj, proj, do, gtot)


def _mesh_place():
    return lax.axis_index("x"), lax.axis_index("y"), lax.axis_index("c")


def _flip(place, mask):
    x, y, c = place
    return ((1 - x) if mask & 4 else x, (1 - y) if mask & 2 else y, (1 - c) if mask & 1 else c)


def _dev_index(place):
    x, y, c = place
    return 4 * x + 2 * y + c


HBM_SPEC = pl.BlockSpec(memory_space=pltpu.HBM)


def all_gather_rows(shard, after, *, name):
    rows, lanes = shard.shape

    def body(x_ref, after_ref, out_ref, send_sems, recv_sems, local_sem):
        me = _mesh_place()
        x, y, c = me
        sibling = _flip(me, 1)
        chips = [_flip(me, 4), _flip(me, 2), _flip(me, 6)]

        def copy(k, block, to, src=None):
            dst = out_ref.at[_dev_index(block)]
            return pltpu.make_async_remote_copy(
                src_ref=dst if src is None else src, dst_ref=dst, send_sem=send_sems.at[k], recv_sem=recv_sems.at[k],
                device_id=to, device_id_type=pl.DeviceIdType.MESH)

        mine = pltpu.make_async_copy(x_ref, out_ref.at[_dev_index(me)], local_sem)
        mine.start()
        first = [copy(0, me, sibling, src=x_ref)] + [copy(1 + j, me, chip, src=x_ref) for j, chip in enumerate(chips)]
        for cp in first:
            cp.start()
        passed = [copy(4 + j, chip, sibling) for j, chip in enumerate(chips)]
        for j, chip in enumerate(chips):
            copy(1 + j, chip, me).wait_recv()
            passed[j].start()
        copy(0, sibling, me).wait_recv()
        for j, chip in enumerate(chips):
            copy(4 + j, _flip(chip, 1), me).wait_recv()
        for cp in first + passed:
            cp.wait_send()
        mine.wait()

    return pl.pallas_call(
        body, name=name, in_specs=[HBM_SPEC, pl.BlockSpec(memory_space=pl.ANY)], out_specs=HBM_SPEC,
        out_shape=jax.ShapeDtypeStruct((N_DEV, rows, lanes), shard.dtype),
        scratch_shapes=[pltpu.SemaphoreType.DMA((7,)), pltpu.SemaphoreType.DMA((7,)), pltpu.SemaphoreType.DMA],
    )(shard, after)


SEM_SPEC = pl.BlockSpec(memory_space=pltpu.SEMAPHORE)
DATAFLOW_EFFECT = pltpu.SideEffectType.DATAFLOW_SIDE_EFFECTING


ALL_PEERS = tuple(range(1, N_DEV))
CHIP_PEERS = (1, 4, 2, 6)
OTHER_CHIPS = (4, 2, 6)


def _spread_copies(src_refs, land_refs, send_sems, recv_sems, per_peer, masks, arriving):
    me = _mesh_place()
    my = _dev_index(me)
    remote, local = [], []
    for t, (src_ref, land_ref) in enumerate(zip(src_refs, land_refs)):
        for i, mask in enumerate(masks):
            peer = _flip(me, mask)
            data_of = my if arriving else _dev_index(peer)
            slot = _dev_index(peer) if arriving else my
            k = t * len(masks) + i
            remote.append(pltpu.make_async_remote_copy(
                src_ref=src_ref.at[data_of] if per_peer else src_ref, dst_ref=land_ref.at[slot],
                send_sem=send_sems.at[k], recv_sem=recv_sems.at[k],
                device_id=peer, device_id_type=pl.DeviceIdType.MESH))
        local.append(pltpu.make_async_copy(src_ref.at[my] if per_peer else src_ref, land_ref.at[my],
                                           send_sems.at[len(src_refs) * len(masks) + t]))
    return remote, local


def spread_start(srcs, *, per_peer, name, masks=ALL_PEERS):
    nt = len(srcs)
    zones = [pltpu.HBM((N_DEV,) + (s.shape[1:] if per_peer else s.shape), s.dtype) for s in srcs]

    def body(*refs):
        src_refs, (send_sems, recv_sems) = refs[:nt], refs[nt:nt + 2]
        land_refs, token = refs[2 * nt + 2:3 * nt + 2], refs[3 * nt + 2]
        remote, local = _spread_copies(src_refs, land_refs, send_sems, recv_sems, per_peer, masks, arriving=False)
        for cp in remote + local:
            cp.start()
        token[...] = jnp.zeros_like(token)

    return pl.pallas_call(
        body, name=name, in_specs=(HBM_SPEC,) * nt,
        out_shape=(pltpu.SemaphoreType.DMA((nt * len(masks) + nt,)), pltpu.SemaphoreType.DMA((nt * len(masks),)),
                   *[pltpu.HBM(s.shape, s.dtype) for s in srcs], *zones, jax.ShapeDtypeStruct((8, 128), F32)),
        out_specs=(SEM_SPEC, SEM_SPEC) + (HBM_SPEC,) * (2 * nt) + (pl.BlockSpec(memory_space=pltpu.VMEM),),
        input_output_aliases={t: 2 + t for t in range(nt)},
        compiler_params=pltpu.CompilerParams(has_side_effects=DATAFLOW_EFFECT),
    )(*[pltpu.with_memory_space_constraint(s, pltpu.HBM) for s in srcs])


def spread_wait(started, after, *, per_peer, name, masks=ALL_PEERS):
    nt = (len(started) - 3) // 2
    send_sems, recv_sems = started[0:2]
    srcs_thru, lands_thru = started[2:2 + nt], started[2 + nt:2 + 2 * nt]

    def body(*refs):
        src_refs, land_refs = refs[:nt], refs[nt:2 * nt]
        send_sems, recv_sems = refs[2 * nt:2 * nt + 2]
        remote, local = _spread_copies(src_refs, land_refs, send_sems, recv_sems, per_peer, masks, arriving=True)
        for cp in remote:
            cp.wait_send()
            cp.wait_recv()
        for cp in local:
            cp.wait()

    outs = pl.pallas_call(
        body, name=name, in_specs=(HBM_SPEC,) * (2 * nt) + (SEM_SPEC, SEM_SPEC, pl.BlockSpec(memory_space=pl.ANY)),
        out_shape=tuple(pltpu.HBM(a.shape, a.dtype) for a in (*srcs_thru, *lands_thru)),
        out_specs=(HBM_SPEC,) * (2 * nt), input_output_aliases={t: t for t in range(2 * nt)},
        compiler_params=pltpu.CompilerParams(has_side_effects=DATAFLOW_EFFECT),
    )(*srcs_thru, *lands_thru, send_sems, recv_sems, after)
    return list(outs[nt:])


def _relay_copies(land_refs, send_sems, recv_sems, arriving):
    me = _mesh_place()
    sibling = _flip(me, 1)
    out = []
    for t, land_ref in enumerate(land_refs):
        for i, mask in enumerate(OTHER_CHIPS):
            slot = _dev_index(_flip(sibling if arriving else me, mask))
            k = t * len(OTHER_CHIPS) + i
            out.append(pltpu.make_async_remote_copy(
                src_ref=land_ref.at[slot], dst_ref=land_ref.at[slot], send_sem=send_sems.at[k], recv_sem=recv_sems.at[k],
                device_id=sibling, device_id_type=pl.DeviceIdType.MESH))
    return out


def relay_start(lands, *, name):
    nt = len(lands)
    n_sem = nt * len(OTHER_CHIPS)

    def body(*refs):
        for cp in _relay_copies(refs[:nt], refs[nt], refs[nt + 1], arriving=False):
            cp.start()

    return pl.pallas_call(
        body, name=name, in_specs=(HBM_SPEC,) * nt,
        out_shape=(pltpu.SemaphoreType.DMA((n_sem,)), pltpu.SemaphoreType.DMA((n_sem,)),
                   *[pltpu.HBM(a.shape, a.dtype) for a in lands]),
        out_specs=(SEM_SPEC, SEM_SPEC) + (HBM_SPEC,) * nt, input_output_aliases={t: 2 + t for t in range(nt)},
        compiler_params=pltpu.CompilerParams(has_side_effects=DATAFLOW_EFFECT),
    )(*[pltpu.with_memory_space_constraint(a, pltpu.HBM) for a in lands])


def relay_wait(started, *, name):
    send_sems, recv_sems = started[0:2]
    lands_thru = started[2:]
    nt = len(lands_thru)

    def body(*refs):
        for cp in _relay_copies(refs[:nt], refs[nt], refs[nt + 1], arriving=True):
            cp.wait_send()
            cp.wait_recv()

    return list(pl.pallas_call(
        body, name=name, in_specs=(HBM_SPEC,) * nt + (SEM_SPEC, SEM_SPEC),
        out_shape=tuple(pltpu.HBM(a.shape, a.dtype) for a in lands_thru), out_specs=(HBM_SPEC,) * nt,
        input_output_aliases={t: t for t in range(nt)},
        compiler_params=pltpu.CompilerParams(has_side_effects=DATAFLOW_EFFECT),
    )(*lands_thru, send_sems, recv_sems))


def sum_partials(parts, *, name, tr):
    _, rows, lanes = parts.shape
    assert rows % tr == 0

    def body(p_ref, g_ref):
        g = p_ref[0].astype(F32)
        for k in range(1, N_DEV):
            g = g + p_ref[k].astype(F32)
        g_ref[...] = g

    return _pcall(
        body, name=name, grid=(rows // tr,),
        in_specs=[pl.BlockSpec((N_DEV, tr, lanes), lambda i: (0, i, 0))],
        out_specs=pl.BlockSpec((tr, lanes), lambda i: (i, 0)), out_shape=jax.ShapeDtypeStruct((rows, lanes), F32),
        blocks=[((N_DEV, tr, lanes), parts.dtype), ((tr, lanes), F32)],
    )(parts)


def adamw(g, w, m, v, *, name, tr):
    nl, k, n = w.shape
    tr = max(c for c in range(8, min(tr, k) + 1, 8) if k % c == 0)
    bc1 = 1.0 - ADAM_B1 ** ADAM_STEP
    bc2 = 1.0 - ADAM_B2 ** ADAM_STEP

    def body(g_ref, w_ref, m_ref, v_ref, d_ref, mo_ref, vo_ref):
        gv = g_ref[...]
        m_new = ADAM_B1 * m_ref[...] + (1.0 - ADAM_B1) * gv
        v_new = ADAM_B2 * v_ref[...] + (1.0 - ADAM_B2) * (gv * gv)
        mo_ref[...] = m_new
        vo_ref[...] = v_new
        d_ref[...] = -ADAM_LR * ((m_new / bc1) / (jnp.sqrt(v_new / bc2) + ADAM_EPS) + ADAM_WD * w_ref[...])

    spec = pl.BlockSpec((1, tr, n), lambda l, i: (l, i, 0))
    shape = jax.ShapeDtypeStruct(w.shape, F32)
    return _pcall(
        body, name=name, grid=(nl, k // tr), in_specs=[spec] * 4, out_specs=[spec] * 3, out_shape=[shape] * 3,
        blocks=[((1, tr, n), F32)] * 7,
    )(g, w, m, v)


def sum_adamw(partials, w, m, v, *, name, tr):
    nl, k, n = w.shape
    assert nl == len(partials) == 2
    tr = max(c for c in range(8, min(tr, k) + 1, 8) if k % c == 0)
    bc1 = 1.0 - ADAM_B1 ** ADAM_STEP
    bc2 = 1.0 - ADAM_B2 ** ADAM_STEP

    def body(p0_ref, p1_ref, w_ref, m_ref, v_ref, g_ref, d_ref, mo_ref, vo_ref):
        first = pl.program_id(0) == 0
        gv = jnp.where(first, p0_ref[0], p1_ref[0]).astype(F32)
        for s in range(1, N_DEV):
            gv = gv + jnp.where(first, p0_ref[s], p1_ref[s]).astype(F32)
        m_new = ADAM_B1 * m_ref[0] + (1.0 - ADAM_B1) * gv
        v_new = ADAM_B2 * v_ref[0] + (1.0 - ADAM_B2) * (gv * gv)
        g_ref[0] = gv
        mo_ref[0] = m_new
        vo_ref[0] = v_new
        d_ref[0] = -ADAM_LR * ((m_new / bc1) / (jnp.sqrt(v_new / bc2) + ADAM_EPS) + ADAM_WD * w_ref[0])

    spec = pl.BlockSpec((1, tr, n), lambda l, i: (l, i, 0))
    pspec = pl.BlockSpec((N_DEV, tr, n), lambda l, i: (0, i, 0))
    shape = jax.ShapeDtypeStruct(w.shape, F32)
    return _pcall(
        body, name=name, grid=(nl, k // tr), in_specs=[pspec, pspec, spec, spec, spec], out_specs=[spec] * 4,
        out_shape=[shape] * 4, blocks=[((N_DEV, tr, n), BF16)] * 2 + [((1, tr, n), F32)] * 7,
    )(partials[0], partials[1], w, m, v)


def travelling(a, by_cols):
    return jnp.swapaxes(a, -1, -2) if by_cols else a


def _row(v):
    return v.reshape(1, -1)


def ffn_fwd(x, h, w, pre, tag, next_gain):
    ga, gb, s = swiglu_fwd(h, w[pre + "_w_gate"], w[pre + "_w_up"], name=f"{tag}_gateup")
    out, h_next = matmul_res_norm(s, w[pre + "_w_down"], x, next_gain, scale=0.5, tm=512, name=f"{tag}_down")
    return out, h_next, (x, h, ga, gb, s)


def ffn_bwd_weights(dxb, saved, w, pre, tag):
    x, h, a, b, s = saved
    da, db = swiglu_bwd(dxb, w[pre + "_w_down"], a, b, scale=0.5, name=f"{tag}_dgateup")
    g_down = matmul([(s, dxb)], "tn", tm=1408, tn=1024, tk=2048, out_dtype=BF16, scale=0.5, name=f"{tag}_gdown")
    g_gate = matmul([(da, h)], "tn", tm=1408, tn=1024, tk=2048, out_dtype=BF16, name=f"{tag}_ggate")
    g_up = matmul([(db, h)], "tn", tm=1408, tn=1024, tk=2048, out_dtype=BF16, name=f"{tag}_gup")
    return {pre + "_w_gate": g_gate, pre + "_w_up": g_up, pre + "_w_down": g_down}, (da, db)


def ffn_bwd_input(dx, rest, saved, gain, w, pre, tag):
    da, db = rest
    x = saved[0]
    return matmul_rms_bwd([(da, w[pre + "_w_gate"]), (db, w[pre + "_w_up"])], x, gain, dx, tm=256, name=f"{tag}_dh")


def mixer_fwd(x, h, w, tables, tag, next_gain):
    proj = matmul([(h, w["w_in"])], "nt", tm=512, tn=1280, tk=1024, out_dtype=BF16, name=f"{tag}_in")
    qks, vs = rope_split(proj, tables, name=f"{tag}_rope")
    outs, lses = [], []
    for g in range(N_DIL_GROUPS):
        o, lse = dil_fwd(qks[g], vs[g], name=f"{tag}_dil{g}")
        outs.append(o)
        lses.append(lse)
    odil, lse = dil_merge(outs, lses, name=f"{tag}_merge")
    osb = sb_fwd(proj, name=f"{tag}_sb")
    y, u1, u2 = gate_fwd(odil, osb, w["w_proj_dil"], w["w_proj_sb"], proj, name=f"{tag}_gate")
    out, h_next = matmul_res_norm(y, w["w_out"], x, next_gain, scale=1.0, tm=512, name=f"{tag}_out")
    return out, h_next, (x, h, proj, qks, vs, odil, lse, osb, u1, u2, y)


def mixer_bwd_weights(dxb, saved, w, tables, tag):
    x, h, proj, qks, vs, odil, lse, osb, u1, u2, y = saved
    t = x.shape[0]
    g_out = matmul([(y, dxb)], "tn", tm=1024, tn=1024, tk=2048, out_dtype=BF16, name=f"{tag}_gout")
    du1, du2, dgate = gate_bwd(dxb, w["w_out"], u1, u2, proj, name=f"{tag}_dgate")
    g_pd = matmul([(du1, odil)], "tn", tm=1024, tn=256, tk=2048, out_dtype=BF16, name=f"{tag}_gpd")
    g_ps = matmul([(du2, osb)], "tn", tm=1024, tn=256, tk=2048, out_dtype=BF16, name=f"{tag}_gps")
    dodil = matmul([(du1, w["w_proj_dil"])], "nn", tm=512, tn=256, tk=1024, out_dtype=F32, name=f"{tag}_dodil")
    dosb = matmul([(du2, w["w_proj_sb"])], "nn", tm=512, tn=256, tk=1024, out_dtype=F32, name=f"{tag}_dosb")
    dsum, do_wide, lse_wide, dsum_wide = dil_bwd_prep(dodil, odil, lse, name=f"{tag}_dprep")
    dos = [dodil[None]] + list(do_wide)
    lss = [lse[None]] + list(lse_wide)
    dss = [dsum[None]] + list(dsum_wide)
    dqs, dks, dvs = [], [], []
    for g in range(N_DIL_GROUPS):
        dq, dk, dv = dil_bwd(qks[g], vs[g], dos[g], lss[g], dss[g], name=f"{tag}_ddil{g}")
        dqs.append(dq)
        dks.append(dk)
        dvs.append(dv)
    dqkv = rope_join(dqs, dks, dvs, tables, name=f"{tag}_drope")
    gtot = head_sums(dosb, osb, round_a=True, name=f"{tag}_gsum")
    dq_s, dk_s, dv_s = sb_bwd(proj, dosb, gtot, name=f"{tag}_dsb")
    dproj = jnp.concatenate([dqkv, dq_s.astype(BF16), dk_s.astype(BF16), dv_s.astype(BF16), dgate], axis=1)
    g_in = matmul([(dproj, h)], "tn", tm=1280, tn=1024, tk=2048, out_dtype=BF16, name=f"{tag}_gin")
    return {"w_in": g_in, "w_proj_dil": g_pd, "w_proj_sb": g_ps, "w_out": g_out}, dproj


def mixer_bwd_input(dx, dproj, saved, gain, w, tag):
    x = saved[0]
    return matmul_rms_bwd([(dproj, w["w_in"])], x, gain, dx, tm=256, name=f"{tag}_dh")


def kernel(x, norm_ffn1, ffn1_w_gate, ffn1_w_up, ffn1_w_down, norm_mix, w_in, w_proj_dil, w_proj_sb, w_out, norm_ffn2, ffn2_w_gate, ffn2_w_up, ffn2_w_down, norm_final, loss_target, m_norm_ffn1, m_ffn1_w_gate, m_ffn1_w_up, m_ffn1_w_down, m_norm_mix, m_w_in, m_w_proj_dil, m_w_proj_sb, m_w_out, m_norm_ffn2, m_ffn2_w_gate, m_ffn2_w_up, m_ffn2_w_down, m_norm_final, v_norm_ffn1, v_ffn1_w_gate, v_ffn1_w_up, v_ffn1_w_down, v_norm_mix, v_w_in, v_w_proj_dil, v_w_proj_sb, v_w_out, v_norm_ffn2, v_ffn2_w_gate, v_ffn2_w_up, v_ffn2_w_down, v_norm_final):
    args = dict(locals())
    t = x.shape[1]
    xs = x.reshape(t, D_MODEL)
    target = loss_target.reshape(t, D_MODEL)
    tables = rope_tables(t)

    parts = [(l, p) for l in range(2) for p in SUBBLOCKS]
    gains = {n: args[n] for n in NORM_ROWS}

    in_flight, order_token = {}, jnp.zeros((1, 1), F32)
    masks_of = {lp: (CHIP_PEERS if lp == parts[0] else ALL_PEERS) for lp in parts}
    for l, p in parts:
        shards = [travelling(args[n][l], by_cols).astype(BF16) for n, _, by_cols, _ in SUBBLOCKS[p]]
        shards[0] = shards[0] + order_token.astype(BF16)
        in_flight[(l, p)] = spread_start(shards, per_peer=False, masks=masks_of[(l, p)], name=f"gather_start_l{l}_{p}")
        order_token = in_flight[(l, p)][-1][0:1, 0:1]

    def weights_of(l, p, after):
        lands = spread_wait(in_flight[(l, p)], after, per_peer=False, masks=masks_of[(l, p)], name=f"gather_wait_l{l}_{p}")
        if masks_of[(l, p)] is CHIP_PEERS:
            lands = relay_wait(relay_start(lands, name=f"gather_relay_l{l}_{p}"), name=f"gather_relayed_l{l}_{p}")
        return {n: land.reshape(-1, land.shape[-1]) for (n, _, _, _), land in zip(SUBBLOCKS[p], lands)}

    saved, weights = {}, {}
    act = xs
    h = rms_fwd(xs, _row(gains["norm_ffn1"][0]) + order_token, name="l0_ffn1_norm")
    for i, (l, p) in enumerate(parts):
        weights[(l, p)] = weights_of(l, p, act)
        nl, np_ = parts[i + 1] if i + 1 < len(parts) else (None, None)
        next_gain = _row(gains["norm_" + np_][nl]) if np_ else None
        if p == "mix":
            act, h, saved[(l, p)] = mixer_fwd(act, h, weights[(l, p)], tables, f"l{l}_mix", next_gain)
        else:
            act, h, saved[(l, p)] = ffn_fwd(act, h, weights[(l, p)], p, f"l{l}_{p}", next_gain)
    dx, dxb, g_final, loss_part = final_loss(act, _row(norm_final), target, name="loss_head")

    gain_grads, sent = {}, {}
    order_token = jnp.zeros((1, 1), F32)
    for l, p in reversed(parts):
        w, sv = weights[(l, p)], saved[(l, p)]
        if p == "mix":
            gw, rest = mixer_bwd_weights(dxb, sv, w, tables, f"l{l}_mix")
        else:
            gw, rest = ffn_bwd_weights(dxb, sv, w, p, f"l{l}_{p}")
        slices = [gw[n].reshape(N_DEV, -1, gw[n].shape[-1]) for n, _, _, _ in SUBBLOCKS[p]]
        sent[(l, p)] = spread_start(slices, per_peer=True, name=f"reduce_start_l{l}_{p}")
        gain = _row(gains["norm_" + p][l]) + sent[(l, p)][-1][0:1, 0:1]
        if p == "mix":
            dx, dxb, gain_grads[("norm_mix", l)] = mixer_bwd_input(dx, rest, sv, gain, w, f"l{l}_mix")
        else:
            dx, dxb, gain_grads[("norm_" + p, l)] = ffn_bwd_input(dx, rest, sv, gain, w, p, f"l{l}_{p}")

    partials, big_all = {}, [{}, {}, {}, {}]

    def receive(l, p, after):
        lands = spread_wait(sent[(l, p)], after, per_peer=True, name=f"reduce_wait_l{l}_{p}")
        for (n, _, _, _), land in zip(SUBBLOCKS[p], lands):
            partials.setdefault(n, [None, None])[l] = land

    def update(p):
        for n, _, by_cols, _ in SUBBLOCKS[p]:
            outs = sum_adamw(partials[n], travelling(args[n], by_cols), travelling(args["m_" + n], by_cols),
                             travelling(args["v_" + n], by_cols), tr=256, name=f"update_{n}")
            for kind, arr in enumerate(outs):
                big_all[kind][n] = travelling(arr, by_cols)
        return outs[1]

    for l, p in reversed(parts[1:]):
        receive(l, p, dx)
    update("ffn2")
    done = update("mix")
    receive(*parts[0], done)
    done = update("ffn1")

    loss_row = jnp.pad(loss_part[:, :1], ((0, 0), (0, LANES - 1)))
    small = jnp.concatenate([gain_grads[(n, l)] for n in NORM_ROWS for l in range(2)] + [g_final, loss_row], axis=0)
    small_g = sum_partials(all_gather_rows(small, done, name="gather_gain_grads"), tr=8, name="sum_gain_grads")
    zero_row = jnp.zeros((1, LANES), F32)
    small_of = lambda pre: jnp.concatenate([args[pre + n] for n in NORM_ROWS] + [_row(args[pre + "norm_final"]), zero_row], axis=0)[None]
    small_out = adamw(small_g[None], small_of(""), small_of("m_"), small_of("v_"), tr=8, name="update_gains")
    small_all = [small_g] + [o[0] for o in small_out]

    def gains_of(s):
        out = {n: s[2 * i:2 * i + 2] for i, n in enumerate(NORM_ROWS)}
        out["norm_final"] = s[6]
        return out

    order = ["norm_ffn1", "ffn1_w_gate", "ffn1_w_up", "ffn1_w_down", "norm_mix", "w_in", "w_proj_dil", "w_proj_sb", "w_out",
             "norm_ffn2", "ffn2_w_gate", "ffn2_w_up", "ffn2_w_down", "norm_final"]
    results = []
    for kind in range(4):
        both = {**big_all[kind], **gains_of(small_all[kind])}
        results += [both[n] for n in order]
    loss = small_g[7, 0]
    return (loss, dx.reshape(1, t, D_MODEL), *results)
```

```python
import functools

import jax
import jax.numpy as jnp
from jax import lax
from jax.experimental import pallas as pl
from jax.experimental.pallas import tpu as pltpu

F32 = jnp.float32
BF16 = jnp.bfloat16

D_MODEL = 1024
HEAD_DIM = 64
GROUP_W = 256
D_IN = 5120
N_DIL_GROUPS = 3
DIL_SPAN = 128
DILATIONS = (1, 4, 16)
ROPE_THETA = 500000.0
ROPE_DIM = 16
RMS_EPS = 1e-6
ATT_SCALE = HEAD_DIM ** -0.5
QS_BLK, KS_BLK, VS_BLK = 9, 10, 11
GATE_DIL_BLK, GATE_SB_BLK = 3, 4

ADAM_LR, ADAM_B1, ADAM_B2, ADAM_EPS, ADAM_WD, ADAM_STEP = 0.001, 0.9, 0.999, 1e-08, 0.01, 10

N_DEV = 8
LANES = 1024
VMEM_PHYSICAL_V7X = 64 << 20
VMEM_TEMP_HEADROOM = 20 << 20

PACK_LAYOUT = (
    ("ffn1_w_gate", 352, True, (1024, 2816)),
    ("ffn1_w_up", 352, True, (1024, 2816)),
    ("ffn1_w_down", 352, False, (2816, 1024)),
    ("w_in", 640, True, (1024, 5120)),
    ("w_proj_dil", 32, True, (256, 1024)),
    ("w_proj_sb", 32, True, (256, 1024)),
    ("w_out", 128, False, (1024, 1024)),
    ("ffn2_w_gate", 352, True, (1024, 2816)),
    ("ffn2_w_up", 352, True, (1024, 2816)),
    ("ffn2_w_down", 352, False, (2816, 1024)),
)
SUBBLOCKS = {"ffn1": PACK_LAYOUT[0:3], "mix": PACK_LAYOUT[3:7], "ffn2": PACK_LAYOUT[7:10]}
NORM_ROWS = ("norm_ffn1", "norm_mix", "norm_ffn2")


def _nbytes(shape, dtype):
    n = 1
    for s in shape:
        n *= s
    return n * jnp.dtype(dtype).itemsize


def _pcall(body, *, name, grid, in_specs, out_specs, out_shape, blocks, scratch_shapes=(), scratch_bytes=0):
    need = 2 * sum(_nbytes(s, d) for s, d in blocks) + scratch_bytes + VMEM_TEMP_HEADROOM
    limit = min(need, VMEM_PHYSICAL_V7X - (4 << 20))
    in_hbm = lambda s: pltpu.HBM(s.shape, s.dtype)
    out_shape = [in_hbm(s) for s in out_shape] if isinstance(out_shape, (list, tuple)) else in_hbm(out_shape)
    call = pl.pallas_call(
        body, name=name, grid=grid, in_specs=in_specs, out_specs=out_specs, out_shape=out_shape,
        scratch_shapes=scratch_shapes,
        compiler_params=pltpu.CompilerParams(vmem_limit_bytes=limit),
    )
    return lambda *args: call(*[pltpu.with_memory_space_constraint(a, pltpu.HBM) for a in args])


def _dot(a, b, form):
    dn = {"nn": (((1,), (0,)), ((), ())), "nt": (((1,), (1,)), ((), ())), "tn": (((0,), (0,)), ((), ()))}[form]
    return lax.dot_general(a.astype(BF16), b.astype(BF16), dn, preferred_element_type=F32)


def _sigmoid(x):
    return 1.0 / (1.0 + jnp.exp(-x))


def matmul(pairs, form, *, tm, tn, tk, out_dtype, name, scale=1.0, res=None):
    a0, b0 = pairs[0]
    if form == "tn":
        kdim, m = a0.shape
        n = b0.shape[1]
    else:
        m, kdim = a0.shape
        n = b0.shape[1] if form == "nn" else b0.shape[0]
    tm, tn, tk = min(tm, m), min(tn, n), min(tk, kdim)
    assert m % tm == 0 and n % tn == 0 and kdim % tk == 0, (name, m, n, kdim, tm, tn, tk)
    nk = kdim // tk
    npairs = len(pairs)

    if form == "tn":
        a_blk, a_map = (tk, tm), (lambda j, i, k: (k, i))
    else:
        a_blk, a_map = (tm, tk), (lambda j, i, k: (i, k))
    if form == "nt":
        b_blk, b_map = (tn, tk), (lambda j, i, k: (j, k))
    else:
        b_blk, b_map = (tk, tn), (lambda j, i, k: (k, j))
    o_map = lambda j, i, k: (i, j)

    def body(*refs):
        ab = refs[:2 * npairs]
        rest = refs[2 * npairs:]
        if res is not None:
            r_ref, o_ref = rest[0], rest[1]
            rest = rest[2:]
        else:
            r_ref, o_ref = None, rest[0]
            rest = rest[1:]

        def partial_sum():
            p = _dot(ab[0][...], ab[1][...], form)
            for q in range(1, npairs):
                p = p + _dot(ab[2 * q][...], ab[2 * q + 1][...], form)
            return p

        def finish(acc):
            out = acc * scale if scale != 1.0 else acc
            if r_ref is not None:
                out = r_ref[...] + out
            o_ref[...] = out.astype(out_dtype)

        if nk == 1:
            finish(partial_sum())
        else:
            acc_ref = rest[0]
            k = pl.program_id(2)

            @pl.when(k == 0)
            def _():
                acc_ref[...] = partial_sum()

            @pl.when(k > 0)
            def _():
                acc_ref[...] += partial_sum()

            @pl.when(k == nk - 1)
            def _():
                finish(acc_ref[...])

    in_specs, args, blocks = [], [], []
    for a, b in pairs:
        in_specs += [pl.BlockSpec(a_blk, a_map), pl.BlockSpec(b_blk, b_map)]
        args += [a, b]
        blocks += [(a_blk, a.dtype), (b_blk, b.dtype)]
    if res is not None:
        in_specs.append(pl.BlockSpec((tm, tn), o_map))
        args.append(res)
        blocks.append(((tm, tn), res.dtype))
    blocks.append(((tm, tn), out_dtype))
    scratch = [pltpu.VMEM((tm, tn), F32)] if nk > 1 else []
    return _pcall(
        body, name=name, grid=(n // tn, m // tm, nk), in_specs=in_specs,
        out_specs=pl.BlockSpec((tm, tn), o_map), out_shape=jax.ShapeDtypeStruct((m, n), out_dtype),
        blocks=blocks, scratch_shapes=scratch, scratch_bytes=(tm * tn * 4 if nk > 1 else 0),
    )(*args)


def swiglu_fwd(h, wg_t, wu_t, *, name, tm=512, tn=1408):
    t, d = h.shape
    f = wg_t.shape[0]
    tm, tn = min(tm, t), min(tn, f)

    def body(h_ref, wg_ref, wu_ref, ga_ref, gb_ref, s_ref):
        hh = h_ref[...]
        a = _dot(hh, wg_ref[...], "nt")
        b = _dot(hh, wu_ref[...], "nt")
        sg = _sigmoid(a)
        silu = a * sg
        ga_ref[...] = (b * (sg * (1.0 + a * (1.0 - sg)))).astype(BF16)
        gb_ref[...] = silu.astype(BF16)
        s_ref[...] = (silu * b).astype(BF16)

    w_spec = pl.BlockSpec((tn, d), lambda j, i: (j, 0))
    o_spec = pl.BlockSpec((tm, tn), lambda j, i: (i, j))
    o_shape = jax.ShapeDtypeStruct((t, f), BF16)
    return _pcall(
        body, name=name, grid=(f // tn, t // tm),
        in_specs=[pl.BlockSpec((tm, d), lambda j, i: (i, 0)), w_spec, w_spec],
        out_specs=[o_spec, o_spec, o_spec], out_shape=[o_shape, o_shape, o_shape],
        blocks=[((tm, d), BF16), ((tn, d), BF16), ((tn, d), BF16)] + [((tm, tn), BF16)] * 3,
    )(h, wg_t, wu_t)


def swiglu_bwd(dyb, wd, ga, gb, *, name, scale, tm=512, tn=1408):
    t, d = dyb.shape
    f = wd.shape[0]
    tm, tn = min(tm, t), min(tn, f)

    def body(dy_ref, wd_ref, ga_ref, gb_ref, da_ref, db_ref):
        ds = _dot(dy_ref[...], wd_ref[...], "nt") * scale
        da_ref[...] = (ds * ga_ref[...].astype(F32)).astype(BF16)
        db_ref[...] = (ds * gb_ref[...].astype(F32)).astype(BF16)

    o_spec = pl.BlockSpec((tm, tn), lambda j, i: (i, j))
    o_shape = jax.ShapeDtypeStruct((t, f), BF16)
    return _pcall(
        body, name=name, grid=(f // tn, t // tm),
        in_specs=[pl.BlockSpec((tm, d), lambda j, i: (i, 0)), pl.BlockSpec((tn, d), lambda j, i: (j, 0)), o_spec, o_spec],
        out_specs=[o_spec, o_spec], out_shape=[o_shape, o_shape],
        blocks=[((tm, d), BF16), ((tn, d), BF16)] + [((tm, tn), BF16)] * 4,
    )(dyb, wd, ga, gb)


def gate_fwd(odil, osb, wpd_t, wps_t, proj, *, name, tm=512):
    t = odil.shape[0]
    tm = min(tm, t)

    def body(od_ref, os_ref, wpd_ref, wps_ref, g1_ref, g2_ref, y_ref, u1_ref, u2_ref):
        u1 = _dot(od_ref[...], wpd_ref[...], "nt")
        u2 = _dot(os_ref[...], wps_ref[...], "nt")
        y = _sigmoid(g1_ref[...].astype(F32)) * u1 + _sigmoid(g2_ref[...].astype(F32)) * u2
        y_ref[...] = y.astype(BF16)
        u1_ref[...] = u1.astype(BF16)
        u2_ref[...] = u2.astype(BF16)

    o_spec = pl.BlockSpec((tm, D_MODEL), lambda i: (i, 0))
    w_spec = pl.BlockSpec((D_MODEL, GROUP_W), lambda i: (0, 0))
    a_spec = pl.BlockSpec((tm, GROUP_W), lambda i: (i, 0))
    o_shape = jax.ShapeDtypeStruct((t, D_MODEL), BF16)
    return _pcall(
        body, name=name, grid=(t // tm,),
        in_specs=[a_spec, a_spec, w_spec, w_spec,
                  pl.BlockSpec((tm, D_MODEL), lambda i: (i, GATE_DIL_BLK)),
                  pl.BlockSpec((tm, D_MODEL), lambda i: (i, GATE_SB_BLK))],
        out_specs=[o_spec, o_spec, o_spec], out_shape=[o_shape, o_shape, o_shape],
        blocks=[((tm, GROUP_W), F32)] * 2 + [((D_MODEL, GROUP_W), BF16)] * 2 + [((tm, D_MODEL), BF16)] * 5,
    )(odil, osb, wpd_t, wps_t, proj, proj)


def gate_bwd(dxb, wout, u1, u2, proj, *, name, tm=512):
    t = dxb.shape[0]
    tm = min(tm, t)

    def body(dx_ref, w_ref, u1_ref, u2_ref, g1_ref, g2_ref, du1_ref, du2_ref, dg_ref):
        dy = _dot(dx_ref[...], w_ref[...], "nt")
        s1 = _sigmoid(g1_ref[...].astype(F32))
        s2 = _sigmoid(g2_ref[...].astype(F32))
        du1_ref[...] = (dy * s1).astype(BF16)
        du2_ref[...] = (dy * s2).astype(BF16)
        dg_ref[:, :D_MODEL] = (dy * u1_ref[...].astype(F32) * s1 * (1.0 - s1)).astype(BF16)
        dg_ref[:, D_MODEL:] = (dy * u2_ref[...].astype(F32) * s2 * (1.0 - s2)).astype(BF16)

    o_spec = pl.BlockSpec((tm, D_MODEL), lambda i: (i, 0))
    o_shape = jax.ShapeDtypeStruct((t, D_MODEL), BF16)
    return _pcall(
        body, name=name, grid=(t // tm,),
        in_specs=[o_spec, pl.BlockSpec((D_MODEL, D_MODEL), lambda i: (0, 0)), o_spec, o_spec,
                  pl.BlockSpec((tm, D_MODEL), lambda i: (i, GATE_DIL_BLK)),
                  pl.BlockSpec((tm, D_MODEL), lambda i: (i, GATE_SB_BLK))],
        out_specs=[o_spec, o_spec, pl.BlockSpec((tm, 2 * D_MODEL), lambda i: (i, 0))],
        out_shape=[o_shape, o_shape, jax.ShapeDtypeStruct((t, 2 * D_MODEL), BF16)],
        blocks=[((tm, D_MODEL), BF16)] * 9 + [((D_MODEL, D_MODEL), BF16)],
    )(dxb, wout, u1, u2, proj, proj)


def rms_fwd(x, gain, *, name, tm=512):
    t, d = x.shape
    tm = min(tm, t)

    def body(x_ref, g_ref, h_ref):
        xv = x_ref[...]
        rstd = lax.rsqrt(jnp.mean(xv * xv, axis=1, keepdims=True) + RMS_EPS)
        h_ref[...] = (xv * rstd * g_ref[...]).astype(BF16)

    return _pcall(
        body, name=name, grid=(t // tm,),
        in_specs=[pl.BlockSpec((tm, d), lambda i: (i, 0)), pl.BlockSpec((1, d), lambda i: (0, 0))],
        out_specs=pl.BlockSpec((tm, d), lambda i: (i, 0)), out_shape=jax.ShapeDtypeStruct((t, d), BF16),
        blocks=[((tm, d), F32), ((tm, d), BF16)],
    )(x, gain)


def matmul_res_norm(a, b, res, next_gain, *, scale, tm, name):
    t, k = a.shape
    d = b.shape[1]
    tm = min(tm, t)
    with_norm = next_gain is not None

    def body(a_ref, b_ref, r_ref, *rest):
        out = r_ref[...] + _dot(a_ref[...], b_ref[...], "nn") * scale
        if with_norm:
            g_ref, o_ref, h_ref = rest
            rstd = lax.rsqrt(jnp.mean(out * out, axis=1, keepdims=True) + RMS_EPS)
            h_ref[...] = (out * rstd * g_ref[...]).astype(BF16)
        else:
            o_ref, = rest
        o_ref[...] = out

    row = pl.BlockSpec((tm, d), lambda i: (i, 0))
    in_specs = [pl.BlockSpec((tm, k), lambda i: (i, 0)), pl.BlockSpec((k, d), lambda i: (0, 0)), row]
    args = [a, b, res]
    out_specs, out_shape = [row], [jax.ShapeDtypeStruct((t, d), F32)]
    if with_norm:
        in_specs.append(pl.BlockSpec((1, d), lambda i: (0, 0)))
        args.append(next_gain)
        out_specs.append(row)
        out_shape.append(jax.ShapeDtypeStruct((t, d), BF16))
    outs = _pcall(
        body, name=name, grid=(t // tm,), in_specs=in_specs, out_specs=out_specs, out_shape=out_shape,
        blocks=[((tm, k), a.dtype), ((k, d), b.dtype), ((tm, d), F32), ((tm, d), F32), ((tm, d), BF16)],
    )(*args)
    return (outs[0], outs[1]) if with_norm else (outs[0], None)


def _rms_bwd_rows(dhv, xv, g, drv):
    rstd = lax.rsqrt(jnp.mean(xv * xv, axis=1, keepdims=True) + RMS_EPS)
    xh = xv * rstd
    dxh = dhv * g
    dx = drv + rstd * (dxh - xh * jnp.mean(dxh * xh, axis=1, keepdims=True))
    return dx, jnp.sum(dhv * xh, axis=0, keepdims=True)


def matmul_rms_bwd(pairs, x, gain, dres, *, tm, name):
    t, d = x.shape
    tm = min(tm, t)
    npairs = len(pairs)

    def body(*refs):
        ab = refs[:2 * npairs]
        x_ref, g_ref, dr_ref, dx_ref, dxb_ref, dg_ref = refs[2 * npairs:]
        dh = _dot(ab[0][...], ab[1][...], "nn")
        for q in range(1, npairs):
            dh = dh + _dot(ab[2 * q][...], ab[2 * q + 1][...], "nn")
        dx, part = _rms_bwd_rows(dh, x_ref[...], g_ref[...], dr_ref[...])
        dx_ref[...] = dx
        dxb_ref[...] = dx.astype(BF16)

        @pl.when(pl.program_id(0) == 0)
        def _():
            dg_ref[...] = part

        @pl.when(pl.program_id(0) > 0)
        def _():
            dg_ref[...] += part

    in_specs, args, blocks = [], [], []
    for a, b in pairs:
        k = a.shape[1]
        in_specs += [pl.BlockSpec((tm, k), lambda i: (i, 0)), pl.BlockSpec((k, d), lambda i: (0, 0))]
        args += [a, b]
        blocks += [((tm, k), a.dtype), ((k, d), b.dtype)]
    row = pl.BlockSpec((tm, d), lambda i: (i, 0))
    vec = pl.BlockSpec((1, d), lambda i: (0, 0))
    return _pcall(
        body, name=name, grid=(t // tm,), in_specs=in_specs + [row, vec, row], out_specs=[row, row, vec],
        out_shape=[jax.ShapeDtypeStruct((t, d), F32), jax.ShapeDtypeStruct((t, d), BF16), jax.ShapeDtypeStruct((1, d), F32)],
        blocks=blocks + [((tm, d), F32)] * 3 + [((tm, d), BF16)],
    )(*args, x, gain, dres)


def final_loss(x, gain, target, *, name, tm=512):
    t, d = x.shape
    tm = min(tm, t)

    def body(x_ref, g_ref, t_ref, dx_ref, dxb_ref, dg_ref, loss_ref):
        xv = x_ref[...]
        g = g_ref[...]
        rstd = lax.rsqrt(jnp.mean(xv * xv, axis=1, keepdims=True) + RMS_EPS)
        xh = xv * rstd
        err = xh * g - t_ref[...]
        dy = err * (1.0 / d)
        dxh = dy * g
        dx = rstd * (dxh - xh * jnp.mean(dxh * xh, axis=1, keepdims=True))
        dx_ref[...] = dx
        dxb_ref[...] = dx.astype(BF16)
        part = jnp.sum(dy * xh, axis=0, keepdims=True)
        sq = jnp.sum(jnp.sum(err * err, axis=1, keepdims=True), axis=0, keepdims=True) * (0.5 / d)
        lpart = jnp.broadcast_to(sq, (1, 128))

        @pl.when(pl.program_id(0) == 0)
        def _():
            dg_ref[...] = part
            loss_ref[...] = lpart

        @pl.when(pl.program_id(0) > 0)
        def _():
            dg_ref[...] += part
            loss_ref[...] += lpart

    row = pl.BlockSpec((tm, d), lambda i: (i, 0))
    vec = pl.BlockSpec((1, d), lambda i: (0, 0))
    return _pcall(
        body, name=name, grid=(t // tm,), in_specs=[row, vec, row],
        out_specs=[row, row, vec, pl.BlockSpec((1, 128), lambda i: (0, 0))],
        out_shape=[jax.ShapeDtypeStruct((t, d), F32), jax.ShapeDtypeStruct((t, d), BF16),
                   jax.ShapeDtypeStruct((1, d), F32), jax.ShapeDtypeStruct((1, 128), F32)],
        blocks=[((tm, d), F32)] * 3 + [((tm, d), BF16)],
    )(x, gain, target)


def rope_tables(t):
    pos = jnp.arange(t, dtype=F32)
    inv_freq = ROPE_THETA ** (-jnp.arange(0, ROPE_DIM, 2, dtype=F32) / ROPE_DIM)
    ang = pos[:, None] * inv_freq[None, :]
    cos, sin = jnp.cos(ang), jnp.sin(ang)
    half = ROPE_DIM // 2
    pad = HEAD_DIM - ROPE_DIM
    one_head = lambda lo, hi, fill: jnp.concatenate([lo, hi, jnp.full((t, pad), fill, F32)], axis=1)
    zeros = jnp.zeros((t, half), F32)
    c = one_head(cos, cos, 1.0)
    sa = one_head(-sin, zeros, 0.0)
    sb = one_head(zeros, sin, 0.0)
    two = lambda a: jnp.concatenate([a, a], axis=1)
    return two(c), two(sa), two(sb)


def _rotate(xv, cv, sav, sbv):
    halves = []
    for half in range(2):
        x = xv[:, 128 * half:128 * (half + 1)]
        halves.append(x * cv + pltpu.roll(x, 120, 1) * sav + pltpu.roll(x, 8, 1) * sbv)
    return jnp.concatenate(halves, axis=1)


STAGE_CHUNKS = 4


def _stage(tm):
    return dict(scratch_shapes=[pltpu.VMEM((STAGE_CHUNKS, tm, 128), F32)], scratch_bytes=STAGE_CHUNKS * tm * 128 * 4)


def _split_residues(stage_ref, val, out_ref, d, col, dtype):
    rows, width = val.shape
    if d == 1:
        out_ref[0, :, col:col + width] = val.astype(dtype)
        return
    chunks = width // 128
    for c in range(chunks):
        stage_ref[c] = val[:, 128 * c:128 * (c + 1)]
    for r in range(d):
        for c in range(chunks):
            out_ref[r, :, col + 128 * c:col + 128 * (c + 1)] = stage_ref[c, pl.ds(r, rows // d, stride=d), :].astype(dtype)


def _join_residues(stage_ref, in_ref, d, col=0, width=GROUP_W):
    if d == 1:
        return in_ref[0, :, col:col + width].astype(F32)
    rows = in_ref.shape[1] * d
    chunks = width // 128
    for r in range(d):
        for c in range(chunks):
            stage_ref[c, pl.ds(r, rows // d, stride=d), :] = in_ref[r, :, col + 128 * c:col + 128 * (c + 1)].astype(F32)
    return jnp.concatenate([stage_ref[c] for c in range(chunks)], axis=1)


def rope_split(proj, tables, *, name, tm=512):
    c, sa, sb = tables
    t = c.shape[0]
    tm = min(tm, t)

    def body(*refs):
        pieces = refs[0:9]
        c_ref, sa_ref, sb_ref = refs[9:12]
        qk_out, v_out = refs[12:15], refs[15:18]
        stage = refs[18]
        cv, sav, sbv = c_ref[...], sa_ref[...], sb_ref[...]
        for g, d in enumerate(DILATIONS):
            for kind in range(3):
                xv = pieces[3 * kind + g][...].astype(F32)
                if kind < 2:
                    _split_residues(stage, _rotate(xv, cv, sav, sbv), qk_out[g], d, GROUP_W * kind, BF16)
                else:
                    _split_residues(stage, xv, v_out[g], d, 0, BF16)

    tab = pl.BlockSpec((tm, 128), lambda i: (i, 0))
    in_specs = [pl.BlockSpec((tm, GROUP_W), functools.partial(lambda i, cb: (i, cb), cb=cb)) for cb in range(9)]
    out_specs = ([pl.BlockSpec((d, tm // d, 2 * GROUP_W), lambda i: (0, i, 0)) for d in DILATIONS]
                 + [pl.BlockSpec((d, tm // d, GROUP_W), lambda i: (0, i, 0)) for d in DILATIONS])
    out_shape = ([jax.ShapeDtypeStruct((d, t // d, 2 * GROUP_W), BF16) for d in DILATIONS]
                 + [jax.ShapeDtypeStruct((d, t // d, GROUP_W), BF16) for d in DILATIONS])
    outs = _pcall(
        body, name=name, grid=(t // tm,), in_specs=in_specs + [tab, tab, tab], out_specs=out_specs, out_shape=out_shape,
        blocks=[((tm, GROUP_W), BF16)] * 18 + [((tm, 128), F32)] * 3,
        **_stage(tm),
    )(*([proj] * 9), c, sa, sb)
    return outs[0:3], outs[3:6]


def rope_join(dqs, dks, dvs, sb_grads, dgate, tables, *, name, tm=512):
    c, sa, sb = tables
    t = c.shape[0]
    tm = min(tm, t)

    def body(*refs):
        pieces, sb_refs, dgate_ref = refs[0:9], refs[9:12], refs[12]
        c_ref, sa_ref, sb_ref = refs[13:16]
        o_ref, stage = refs[16], refs[17]
        cv, sav, sbv = c_ref[...], -sa_ref[...], -sb_ref[...]
        for kind in range(3):
            for g, d in enumerate(DILATIONS):
                xv = _join_residues(stage, pieces[3 * kind + g], d)
                if kind < 2:
                    xv = _rotate(xv, cv, sav, sbv)
                col = GROUP_W * (3 * kind + g)
                o_ref[:, col:col + GROUP_W] = xv.astype(BF16)
        for j in range(3):
            o_ref[:, GROUP_W * (QS_BLK + j):GROUP_W * (QS_BLK + j + 1)] = sb_refs[j][...].astype(BF16)
        o_ref[:, D_MODEL * GATE_DIL_BLK:] = dgate_ref[...]

    tab = pl.BlockSpec((tm, 128), lambda i: (i, 0))
    nat = lambda w: pl.BlockSpec((tm, w), lambda i: (i, 0))
    in_specs = [pl.BlockSpec((d, tm // d, GROUP_W), lambda i: (0, i, 0)) for _ in range(3) for d in DILATIONS]
    in_specs += [nat(GROUP_W)] * 3 + [nat(2 * D_MODEL)]
    return _pcall(
        body, name=name, grid=(t // tm,), in_specs=in_specs + [tab, tab, tab],
        out_specs=nat(D_IN), out_shape=jax.ShapeDtypeStruct((t, D_IN), BF16),
        blocks=[((tm, GROUP_W), F32)] * 12 + [((tm, 128), F32)] * 3 + [((tm, 2 * D_MODEL), BF16), ((tm, D_IN), BF16)],
        **_stage(tm),
    )(*dqs, *dks, *dvs, *sb_grads, dgate, c, sa, sb)


def _head_mask(h):
    lane = lax.broadcasted_iota(jnp.int32, (1, GROUP_W), 1)
    return (lane // HEAD_DIM) == h


def _band_masks(heads):
    ri = lax.broadcasted_iota(jnp.int32, (heads * DIL_SPAN, DIL_SPAN), 0) % DIL_SPAN
    ci = lax.broadcasted_iota(jnp.int32, (heads * DIL_SPAN, DIL_SPAN), 1)
    return ci <= ri, ci >= ri


def dil_fwd(qk, v, *, name):
    d, nsub, _ = qk.shape
    nblk = nsub // DIL_SPAN

    def body(q_ref, kc_ref, kp_ref, vc_ref, vp_ref, o_ref, lse_ref):
        nb = pl.program_id(1)
        own, prev = _band_masks(1)
        prev = prev & (nb > 0)
        q, kc, kp, vc, vp = q_ref[0] * ATT_SCALE, kc_ref[0], kp_ref[0], vc_ref[0], vp_ref[0]
        o_acc = jnp.zeros((DIL_SPAN, GROUP_W), F32)
        for h in range(4):
            hm = _head_mask(h)
            qh = jnp.where(hm, q, jnp.zeros_like(q))
            sc = jnp.where(own, _dot(qh, kc, "nt"), -jnp.inf)
            sp = jnp.where(prev, _dot(qh, kp, "nt"), -jnp.inf)
            m = jnp.maximum(jnp.max(sc, axis=1, keepdims=True), jnp.max(sp, axis=1, keepdims=True))
            pc = jnp.exp(sc - m)
            pp = jnp.exp(sp - m)
            den = jnp.sum(pc, axis=1, keepdims=True) + jnp.sum(pp, axis=1, keepdims=True)
            oh = (_dot(pc, vc, "nn") + _dot(pp, vp, "nn")) / den
            o_acc = jnp.where(hm, oh, o_acc)
            lse_ref[0, :, 128 * h:128 * (h + 1)] = jnp.broadcast_to(m + jnp.log(den), (DIL_SPAN, 128))
        o_ref[0] = o_acc

    blk = (1, DIL_SPAN, GROUP_W)
    sblk = (1, DIL_SPAN, 512)
    prv = lambda nb: jnp.maximum(nb - 1, 0)
    return _pcall(
        body, name=name, grid=(d, nblk),
        in_specs=[pl.BlockSpec(blk, lambda r, nb: (r, nb, 0)),
                  pl.BlockSpec(blk, lambda r, nb: (r, nb, 1)),
                  pl.BlockSpec(blk, lambda r, nb: (r, prv(nb), 1)),
                  pl.BlockSpec(blk, lambda r, nb: (r, nb, 0)),
                  pl.BlockSpec(blk, lambda r, nb: (r, prv(nb), 0))],
        out_specs=[pl.BlockSpec(blk, lambda r, nb: (r, nb, 0)), pl.BlockSpec(sblk, lambda r, nb: (r, nb, 0))],
        out_shape=[jax.ShapeDtypeStruct((d, nsub, GROUP_W), F32), jax.ShapeDtypeStruct((d, nsub, 512), F32)],
        blocks=[(blk, BF16)] * 5 + [(blk, F32), (sblk, F32)],
    )(qk, qk, qk, v, v)


def dil_merge(outs, lses, *, name, tm=512):
    t = outs[0].shape[0] * outs[0].shape[1]
    tm = min(tm, t)

    def body(o0, o1, o2, l0, l1, l2, o_ref, lse_ref, stage):
        ls = [_join_residues(stage, l, d, 0, 512) for l, d in zip((l0, l1, l2), DILATIONS)]
        m = jnp.maximum(jnp.maximum(ls[0], ls[1]), ls[2])
        tot = m + jnp.log(jnp.exp(ls[0] - m) + jnp.exp(ls[1] - m) + jnp.exp(ls[2] - m))
        lse_ref[...] = tot
        lane = lax.broadcasted_iota(jnp.int32, (1, 128), 1)
        first = lane < HEAD_DIM
        acc = jnp.zeros((tm, GROUP_W), F32)
        for og, lg, d in zip((o0, o1, o2), ls, DILATIONS):
            w = jnp.exp(lg - tot)
            wide = jnp.concatenate([jnp.where(first, w[:, 0:128], w[:, 128:256]),
                                    jnp.where(first, w[:, 256:384], w[:, 384:512])], axis=1)
            acc = acc + wide * _join_residues(stage, og, d)
        o_ref[...] = acc

    o_in = [pl.BlockSpec((d, tm // d, GROUP_W), lambda i: (0, i, 0)) for d in DILATIONS]
    l_in = [pl.BlockSpec((d, tm // d, 512), lambda i: (0, i, 0)) for d in DILATIONS]
    return _pcall(
        body, name=name, grid=(t // tm,), in_specs=o_in + l_in,
        out_specs=[pl.BlockSpec((tm, GROUP_W), lambda i: (i, 0)), pl.BlockSpec((tm, 512), lambda i: (i, 0))],
        out_shape=[jax.ShapeDtypeStruct((t, GROUP_W), F32), jax.ShapeDtypeStruct((t, 512), F32)],
        blocks=[((tm, GROUP_W), F32)] * 4 + [((tm, 512), F32)] * 4,
        **_stage(tm),
    )(*outs, *lses)


def dil_bwd_prep(do, o, lse, *, name, tm=512):
    t = do.shape[0]
    tm = min(tm, t)
    wide = DILATIONS[1:]

    def body(do_ref, o_ref, lse_ref, ds_ref, *rest):
        do_out, lse_out, ds_out = rest[0:2], rest[2:4], rest[4:6]
        stage = rest[6]
        dov = do_ref[...]
        prod = dov * o_ref[...]
        for h in range(4):
            s = jnp.sum(jnp.where(_head_mask(h), prod, 0.0), axis=1, keepdims=True)
            ds_ref[:, 128 * h:128 * (h + 1)] = jnp.broadcast_to(s, (tm, 128))
        for i, d in enumerate(wide):
            _split_residues(stage, dov, do_out[i], d, 0, BF16)
            _split_residues(stage, lse_ref[...], lse_out[i], d, 0, F32)
            _split_residues(stage, ds_ref[...], ds_out[i], d, 0, F32)

    nat = lambda w: pl.BlockSpec((tm, w), lambda i: (i, 0))
    res = lambda d, w: pl.BlockSpec((d, tm // d, w), lambda i: (0, i, 0))
    shape = lambda d, w, dt: jax.ShapeDtypeStruct((d, t // d, w), dt)
    outs = _pcall(
        body, name=name, grid=(t // tm,), in_specs=[nat(GROUP_W), nat(GROUP_W), nat(512)],
        out_specs=[nat(512)] + [res(d, GROUP_W) for d in wide] + [res(d, 512) for d in wide] * 2,
        out_shape=([jax.ShapeDtypeStruct((t, 512), F32)] + [shape(d, GROUP_W, BF16) for d in wide]
                   + [shape(d, 512, F32) for d in wide] * 2),
        blocks=[((tm, GROUP_W), F32)] * 3 + [((tm, 512), F32)] * 6,
        **_stage(tm),
    )(do, o, lse)
    return outs[0], outs[1:3], outs[3:5], outs[5:7]


def head_sums(a, b, *, name, round_a=False, tm=512):
    t = a.shape[0]
    tm = min(tm, t)

    def body(a_ref, b_ref, o_ref):
        av = a_ref[...]
        if round_a:
            av = av.astype(BF16).astype(F32)
        prod = av * b_ref[...]
        for h in range(4):
            s = jnp.sum(jnp.where(_head_mask(h), prod, 0.0), axis=1, keepdims=True)
            o_ref[:, 128 * h:128 * (h + 1)] = jnp.broadcast_to(s, (tm, 128))

    spec = pl.BlockSpec((tm, GROUP_W), lambda i: (i, 0))
    return _pcall(
        body, name=name, grid=(t // tm,), in_specs=[spec, spec],
        out_specs=pl.BlockSpec((tm, 512), lambda i: (i, 0)), out_shape=jax.ShapeDtypeStruct((t, 512), F32),
        blocks=[((tm, GROUP_W), F32)] * 2 + [((tm, 512), F32)],
    )(a, b)


def dil_bwd(qk, v, do, lse, dsum, *, name):
    d, nsub, _ = qk.shape
    nblk = nsub // DIL_SPAN

    def body(qa_ref, qb_ref, kc_ref, kp_ref, vc_ref, vp_ref, doa_ref, dob_ref, la_ref, lb_ref, sa_ref, sb_ref,
             dq_ref, dk_ref, dv_ref):
        nb = pl.program_id(1)
        own, band = _band_masks(4)
        prev = band & (nb > 0)
        nxt = band & (nb < nblk - 1)
        kc, kp, vc, vp = kc_ref[0], kp_ref[0], vc_ref[0], vp_ref[0]
        qas, qbs = _stack_heads(qa_ref[0] * ATT_SCALE), _stack_heads(qb_ref[0] * ATT_SCALE)
        das, dbs = _stack_heads(doa_ref[0].astype(BF16)), _stack_heads(dob_ref[0].astype(BF16))
        stat = lambda ref: jnp.concatenate([ref[0, :, 128 * h:128 * (h + 1)] for h in range(4)], axis=0)
        la, lb, sa, sb = stat(la_ref), stat(lb_ref), stat(sa_ref), stat(sb_ref)

        def probs(qs, ds_, k, v, mask, l, s):
            p = jnp.where(mask, jnp.exp(_dot(qs, k, "nt") - l), 0.0)
            dsc = p * (_dot(ds_, v, "nt") - s)
            return p.astype(BF16), dsc.astype(BF16)

        p_cc, ds_cc = probs(qas, das, kc, vc, own, la, sa)
        _, ds_cp = probs(qas, das, kp, vp, prev, la, sa)
        p_nc, ds_nc = probs(qbs, dbs, kc, vc, nxt, lb, sb)
        dq_ref[0] = _unstack_heads(_dot(ds_cc, kc, "nn") + _dot(ds_cp, kp, "nn"), DIL_SPAN) * ATT_SCALE
        dk_ref[0] = _dot(ds_cc, qas, "tn") + _dot(ds_nc, qbs, "tn")
        dv_ref[0] = _dot(p_cc, das, "tn") + _dot(p_nc, dbs, "tn")

    blk = (1, DIL_SPAN, GROUP_W)
    sblk = (1, DIL_SPAN, 512)
    prv = lambda nb: jnp.maximum(nb - 1, 0)
    nxt_ = lambda nb: jnp.minimum(nb + 1, nblk - 1)
    cur_at = lambda c: pl.BlockSpec(blk, functools.partial(lambda r, nb, c: (r, nb, c), c=c))
    prv_at = lambda c: pl.BlockSpec(blk, functools.partial(lambda r, nb, c: (r, prv(nb), c), c=c))
    nxt_at = lambda c: pl.BlockSpec(blk, functools.partial(lambda r, nb, c: (r, nxt_(nb), c), c=c))
    s_cur = pl.BlockSpec(sblk, lambda r, nb: (r, nb, 0))
    s_nxt = pl.BlockSpec(sblk, lambda r, nb: (r, nxt_(nb), 0))
    o_spec = pl.BlockSpec(blk, lambda r, nb: (r, nb, 0))
    o_shape = jax.ShapeDtypeStruct((d, nsub, GROUP_W), F32)
    return _pcall(
        body, name=name, grid=(d, nblk),
        in_specs=[cur_at(0), nxt_at(0), cur_at(1), prv_at(1), cur_at(0), prv_at(0), cur_at(0), nxt_at(0),
                  s_cur, s_nxt, s_cur, s_nxt],
        out_specs=[o_spec, o_spec, o_spec], out_shape=[o_shape, o_shape, o_shape],
        blocks=[(blk, BF16)] * 6 + [(blk, F32)] * 5 + [(sblk, F32)] * 4,
    )(qk, qk, qk, qk, v, v, do, do, lse, lse, dsum, dsum)


def _tri_dot(x, b):
    hi = x.astype(BF16)
    lo = (x - hi.astype(F32)).astype(BF16)
    return _dot(jnp.concatenate([hi, lo], axis=1), jnp.concatenate([b, b], axis=0), "nn")


SB_TILE = 256


def _stack_heads(a):
    return jnp.concatenate([jnp.where(_head_mask(h), a, jnp.zeros_like(a)) for h in range(4)], axis=0)


def _unstack_heads(acc, rows):
    out = acc[0:rows]
    for h in range(1, 4):
        out = jnp.where(_head_mask(h), acc[h * rows:(h + 1) * rows], out)
    return out


def _tri_masks(n):
    ri = lax.broadcasted_iota(jnp.int32, (n, n), 0)
    ci = lax.broadcasted_iota(jnp.int32, (n, n), 1)
    return (ri > ci).astype(BF16), (ri >= ci).astype(BF16)


def _sb_weights(qs, kt, after, c_keep, diagonal):
    z = _dot(qs, kt, "nt")
    lbeta = jnp.minimum(z, 0.0) - jnp.log(1.0 + jnp.exp(-jnp.abs(z)))
    lkeep = lbeta - z
    past = None
    if diagonal:
        n = SB_TILE
        past = lax.broadcasted_iota(jnp.int32, z.shape, 1) < lax.broadcasted_iota(jnp.int32, z.shape, 0) % n
        lkeep = jnp.where(past, lkeep, 0.0)
    w = jnp.exp(lbeta + _tri_dot(lkeep, after) + c_keep)
    if diagonal:
        w = jnp.where(past, w, 0.0)
    return z, past, lbeta, lkeep, w


def sb_fwd(proj, *, name):
    t = proj.shape[0]
    n = SB_TILE
    assert t % n == 0

    def body(q_ref, k_ref, v_ref, o_ref, acc_ref):
        qb = pl.program_id(0)
        qs = _stack_heads(q_ref[...] * ATT_SCALE)
        after, _ = _tri_masks(n)

        def tile(off, diagonal, c_keep):
            kt = k_ref[pl.ds(off, n), :]
            vt = v_ref[pl.ds(off, n), :]
            _, _, _, lkeep, w = _sb_weights(qs, kt, after, c_keep, diagonal)
            pv = _tri_dot(w, vt)
            if diagonal:
                acc_ref[...] = pv
            else:
                acc_ref[...] += pv
            return c_keep + jnp.sum(lkeep, axis=1, keepdims=True)

        c0 = tile(pl.multiple_of(qb * n, n), True, jnp.zeros((4 * n, 1), F32))
        lax.fori_loop(0, qb, lambda it, c: tile(pl.multiple_of((qb - 1 - it) * n, n), False, c), c0)
        o_ref[...] = _unstack_heads(acc_ref[...], n)

    full = lambda cb: pl.BlockSpec((t, GROUP_W), functools.partial(lambda i, cb: (0, cb), cb=cb))
    return _pcall(
        body, name=name, grid=(t // n,),
        in_specs=[pl.BlockSpec((n, GROUP_W), lambda i: (i, QS_BLK)), full(KS_BLK), full(VS_BLK)],
        out_specs=pl.BlockSpec((n, GROUP_W), lambda i: (i, 0)), out_shape=jax.ShapeDtypeStruct((t, GROUP_W), F32),
        blocks=[((n, GROUP_W), BF16), ((t, GROUP_W), BF16), ((t, GROUP_W), BF16), ((n, GROUP_W), F32)],
        scratch_shapes=[pltpu.VMEM((4 * n, GROUP_W), F32)], scratch_bytes=4 * n * GROUP_W * 4,
    )(proj, proj, proj)


def sb_bwd(proj, do, gtot, *, name):
    t = proj.shape[0]
    n = SB_TILE
    assert t % n == 0

    def body(q_ref, k_ref, v_ref, do_ref, gt_ref, dq_ref, dk_ref, dv_ref, acc_ref):
        qb = pl.program_id(0)

        @pl.when(qb == 0)
        def _():
            dk_ref[...] = jnp.zeros_like(dk_ref)
            dv_ref[...] = jnp.zeros_like(dv_ref)

        qs = _stack_heads(q_ref[...] * ATT_SCALE)
        dos = _stack_heads(do_ref[...].astype(BF16))
        gt = jnp.concatenate([jnp.max(gt_ref[:, 128 * h:128 * (h + 1)], axis=1, keepdims=True) for h in range(4)], axis=0)
        after, from_on = _tri_masks(n)

        def tile(off, diagonal, carry):
            c_keep, c_g = carry
            kt = k_ref[pl.ds(off, n), :]
            vt = v_ref[pl.ds(off, n), :]
            z, past, lbeta, lkeep, w = _sb_weights(qs, kt, after, c_keep, diagonal)
            gw = w * _dot(dos, vt, "nt")
            big_g = gt - (_tri_dot(gw, from_on) + c_g)
            dz = gw * jnp.exp(lbeta - z) - big_g * jnp.exp(lbeta)
            if diagonal:
                dz = jnp.where(past, dz, 0.0)
            dz = dz.astype(BF16)
            dk_ref[pl.ds(off, n), :] += _dot(dz, qs, "tn")
            dv_ref[pl.ds(off, n), :] += _dot(w, dos, "tn")
            dq = _dot(dz, kt, "nn")
            if diagonal:
                acc_ref[...] = dq
            else:
                acc_ref[...] += dq
            return c_keep + jnp.sum(lkeep, axis=1, keepdims=True), c_g + jnp.sum(gw, axis=1, keepdims=True)

        zero_col = jnp.zeros((4 * n, 1), F32)
        c0 = tile(pl.multiple_of(qb * n, n), True, (zero_col, zero_col))
        lax.fori_loop(0, qb, lambda it, c: tile(pl.multiple_of((qb - 1 - it) * n, n), False, c), c0)
        dq_ref[...] = _unstack_heads(acc_ref[...], n) * ATT_SCALE

    full = lambda cb: pl.BlockSpec((t, GROUP_W), functools.partial(lambda i, cb: (0, cb), cb=cb))
    whole = pl.BlockSpec((t, GROUP_W), lambda i: (0, 0))
    rowblk = pl.BlockSpec((n, GROUP_W), lambda i: (i, 0))
    shape = jax.ShapeDtypeStruct((t, GROUP_W), F32)
    return _pcall(
        body, name=name, grid=(t // n,),
        in_specs=[pl.BlockSpec((n, GROUP_W), lambda i: (i, QS_BLK)), full(KS_BLK), full(VS_BLK), rowblk,
                  pl.BlockSpec((n, 512), lambda i: (i, 0))],
        out_specs=[rowblk, whole, whole], out_shape=[shape, shape, shape],
        blocks=[((n, GROUP_W), BF16), ((t, GROUP_W), BF16), ((t, GROUP_W), BF16), ((n, GROUP_W), F32),
                ((n, 512), F32), ((n, GROUP_W), F32), ((t, GROUP_W), F32), ((t, GROUP_W), F32)],
        scratch_shapes=[pltpu.VMEM((4 * n, GROUP_W), F32)], scratch_bytes=4 * n * GROUP_W * 4,
    )(proj, proj, proj, do, gtot)


def _mesh_place():
    return lax.axis_index("x"), lax.axis_index("y"), lax.axis_index("c")


def _flip(place, mask):
    x, y, c = place
    return ((1 - x) if mask & 4 else x, (1 - y) if mask & 2 else y, (1 - c) if mask & 1 else c)


def _dev_index(place):
    x, y, c = place
    return 4 * x + 2 * y + c


HBM_SPEC = pl.BlockSpec(memory_space=pltpu.HBM)


def all_gather_rows(shard, after, *, name):
    rows, lanes = shard.shape

    def body(x_ref, after_ref, out_ref, send_sems, recv_sems, local_sem):
        me = _mesh_place()
        x, y, c = me
        sibling = _flip(me, 1)
        chips = [_flip(me, 4), _flip(me, 2), _flip(me, 6)]

        def copy(k, block, to, src=None):
            dst = out_ref.at[_dev_index(block)]
            return pltpu.make_async_remote_copy(
                src_ref=dst if src is None else src, dst_ref=dst, send_sem=send_sems.at[k], recv_sem=recv_sems.at[k],
                device_id=to, device_id_type=pl.DeviceIdType.MESH)

        mine = pltpu.make_async_copy(x_ref, out_ref.at[_dev_index(me)], local_sem)
        mine.start()
        first = [copy(0, me, sibling, src=x_ref)] + [copy(1 + j, me, chip, src=x_ref) for j, chip in enumerate(chips)]
        for cp in first:
            cp.start()
        passed = [copy(4 + j, chip, sibling) for j, chip in enumerate(chips)]
        for j, chip in enumerate(chips):
            copy(1 + j, chip, me).wait_recv()
            passed[j].start()
        copy(0, sibling, me).wait_recv()
        for j, chip in enumerate(chips):
            copy(4 + j, _flip(chip, 1), me).wait_recv()
        for cp in first + passed:
            cp.wait_send()
        mine.wait()

    return pl.pallas_call(
        body, name=name, in_specs=[HBM_SPEC, pl.BlockSpec(memory_space=pl.ANY)], out_specs=HBM_SPEC,
        out_shape=jax.ShapeDtypeStruct((N_DEV, rows, lanes), shard.dtype),
        scratch_shapes=[pltpu.SemaphoreType.DMA((7,)), pltpu.SemaphoreType.DMA((7,)), pltpu.SemaphoreType.DMA],
    )(shard, after)


SEM_SPEC = pl.BlockSpec(memory_space=pltpu.SEMAPHORE)
DATAFLOW_EFFECT = pltpu.SideEffectType.DATAFLOW_SIDE_EFFECTING


ALL_PEERS = tuple(range(1, N_DEV))
CHIP_PEERS = (1, 4, 2, 6)
OTHER_CHIPS = (4, 2, 6)


def _spread_copies(src_refs, land_refs, send_sems, recv_sems, per_peer, masks, arriving):
    me = _mesh_place()
    my = _dev_index(me)
    remote, local = [], []
    for t, (src_ref, land_ref) in enumerate(zip(src_refs, land_refs)):
        for i, mask in enumerate(masks):
            peer = _flip(me, mask)
            data_of = my if arriving else _dev_index(peer)
            slot = _dev_index(peer) if arriving else my
            k = t * len(masks) + i
            remote.append(pltpu.make_async_remote_copy(
                src_ref=src_ref.at[data_of] if per_peer else src_ref, dst_ref=land_ref.at[slot],
                send_sem=send_sems.at[k], recv_sem=recv_sems.at[k],
                device_id=peer, device_id_type=pl.DeviceIdType.MESH))
        local.append(pltpu.make_async_copy(src_ref.at[my] if per_peer else src_ref, land_ref.at[my],
                                           send_sems.at[len(src_refs) * len(masks) + t]))
    return remote, local


def spread_start(srcs, *, per_peer, name, masks=ALL_PEERS):
    nt = len(srcs)
    zones = [pltpu.HBM((N_DEV,) + (s.shape[1:] if per_peer else s.shape), s.dtype) for s in srcs]

    def body(*refs):
        src_refs, (send_sems, recv_sems) = refs[:nt], refs[nt:nt + 2]
        land_refs, token = refs[2 * nt + 2:3 * nt + 2], refs[3 * nt + 2]
        remote, local = _spread_copies(src_refs, land_refs, send_sems, recv_sems, per_peer, masks, arriving=False)
        for cp in remote + local:
            cp.start()
        token[...] = jnp.zeros_like(token)

    return pl.pallas_call(
        body, name=name, in_specs=(HBM_SPEC,) * nt,
        out_shape=(pltpu.SemaphoreType.DMA((nt * len(masks) + nt,)), pltpu.SemaphoreType.DMA((nt * len(masks),)),
                   *[pltpu.HBM(s.shape, s.dtype) for s in srcs], *zones, jax.ShapeDtypeStruct((8, 128), F32)),
        out_specs=(SEM_SPEC, SEM_SPEC) + (HBM_SPEC,) * (2 * nt) + (pl.BlockSpec(memory_space=pltpu.VMEM),),
        input_output_aliases={t: 2 + t for t in range(nt)},
        compiler_params=pltpu.CompilerParams(has_side_effects=DATAFLOW_EFFECT),
    )(*[pltpu.with_memory_space_constraint(s, pltpu.HBM) for s in srcs])


def spread_wait(started, after, *, per_peer, name, masks=ALL_PEERS):
    nt = (len(started) - 3) // 2
    send_sems, recv_sems = started[0:2]
    srcs_thru, lands_thru = started[2:2 + nt], started[2 + nt:2 + 2 * nt]

    def body(*refs):
        src_refs, land_refs = refs[:nt], refs[nt:2 * nt]
        send_sems, recv_sems = refs[2 * nt:2 * nt + 2]
        remote, local = _spread_copies(src_refs, land_refs, send_sems, recv_sems, per_peer, masks, arriving=True)
        for cp in remote:
            cp.wait_send()
            cp.wait_recv()
        for cp in local:
            cp.wait()

    outs = pl.pallas_call(
        body, name=name, in_specs=(HBM_SPEC,) * (2 * nt) + (SEM_SPEC, SEM_SPEC, pl.BlockSpec(memory_space=pl.ANY)),
        out_shape=tuple(pltpu.HBM(a.shape, a.dtype) for a in (*srcs_thru, *lands_thru)),
        out_specs=(HBM_SPEC,) * (2 * nt), input_output_aliases={t: t for t in range(2 * nt)},
        compiler_params=pltpu.CompilerParams(has_side_effects=DATAFLOW_EFFECT),
    )(*srcs_thru, *lands_thru, send_sems, recv_sems, after)
    return list(outs[nt:])


def _relay_copies(land_refs, send_sems, recv_sems, arriving):
    me = _mesh_place()
    sibling = _flip(me, 1)
    out = []
    for t, land_ref in enumerate(land_refs):
        for i, mask in enumerate(OTHER_CHIPS):
            slot = _dev_index(_flip(sibling if arriving else me, mask))
            k = t * len(OTHER_CHIPS) + i
            out.append(pltpu.make_async_remote_copy(
                src_ref=land_ref.at[slot], dst_ref=land_ref.at[slot], send_sem=send_sems.at[k], recv_sem=recv_sems.at[k],
                device_id=sibling, device_id_type=pl.DeviceIdType.MESH))
    return out


def relay_start(lands, *, name):
    nt = len(lands)
    n_sem = nt * len(OTHER_CHIPS)

    def body(*refs):
        for cp in _relay_copies(refs[:nt], refs[nt], refs[nt + 1], arriving=False):
            cp.start()

    return pl.pallas_call(
        body, name=name, in_specs=(HBM_SPEC,) * nt,
        out_shape=(pltpu.SemaphoreType.DMA((n_sem,)), pltpu.SemaphoreType.DMA((n_sem,)),
                   *[pltpu.HBM(a.shape, a.dtype) for a in lands]),
        out_specs=(SEM_SPEC, SEM_SPEC) + (HBM_SPEC,) * nt, input_output_aliases={t: 2 + t for t in range(nt)},
        compiler_params=pltpu.CompilerParams(has_side_effects=DATAFLOW_EFFECT),
    )(*[pltpu.with_memory_space_constraint(a, pltpu.HBM) for a in lands])


def relay_wait(started, *, name):
    send_sems, recv_sems = started[0:2]
    lands_thru = started[2:]
    nt = len(lands_thru)

    def body(*refs):
        for cp in _relay_copies(refs[:nt], refs[nt], refs[nt + 1], arriving=True):
            cp.wait_send()
            cp.wait_recv()

    return list(pl.pallas_call(
        body, name=name, in_specs=(HBM_SPEC,) * nt + (SEM_SPEC, SEM_SPEC),
        out_shape=tuple(pltpu.HBM(a.shape, a.dtype) for a in lands_thru), out_specs=(HBM_SPEC,) * nt,
        input_output_aliases={t: t for t in range(nt)},
        compiler_params=pltpu.CompilerParams(has_side_effects=DATAFLOW_EFFECT),
    )(*lands_thru, send_sems, recv_sems))


def sum_partials(parts, *, name, tr):
    _, rows, lanes = parts.shape
    assert rows % tr == 0

    def body(p_ref, g_ref):
        g = p_ref[0].astype(F32)
        for k in range(1, N_DEV):
            g = g + p_ref[k].astype(F32)
        g_ref[...] = g

    return _pcall(
        body, name=name, grid=(rows // tr,),
        in_specs=[pl.BlockSpec((N_DEV, tr, lanes), lambda i: (0, i, 0))],
        out_specs=pl.BlockSpec((tr, lanes), lambda i: (i, 0)), out_shape=jax.ShapeDtypeStruct((rows, lanes), F32),
        blocks=[((N_DEV, tr, lanes), parts.dtype), ((tr, lanes), F32)],
    )(parts)


def adamw(g, w, m, v, *, name, tr):
    nl, k, n = w.shape
    tr = max(c for c in range(8, min(tr, k) + 1, 8) if k % c == 0)
    bc1 = 1.0 - ADAM_B1 ** ADAM_STEP
    bc2 = 1.0 - ADAM_B2 ** ADAM_STEP

    def body(g_ref, w_ref, m_ref, v_ref, d_ref, mo_ref, vo_ref):
        gv = g_ref[...]
        m_new = ADAM_B1 * m_ref[...] + (1.0 - ADAM_B1) * gv
        v_new = ADAM_B2 * v_ref[...] + (1.0 - ADAM_B2) * (gv * gv)
        mo_ref[...] = m_new
        vo_ref[...] = v_new
        d_ref[...] = -ADAM_LR * ((m_new / bc1) / (jnp.sqrt(v_new / bc2) + ADAM_EPS) + ADAM_WD * w_ref[...])

    spec = pl.BlockSpec((1, tr, n), lambda l, i: (l, i, 0))
    shape = jax.ShapeDtypeStruct(w.shape, F32)
    return _pcall(
        body, name=name, grid=(nl, k // tr), in_specs=[spec] * 4, out_specs=[spec] * 3, out_shape=[shape] * 3,
        blocks=[((1, tr, n), F32)] * 7,
    )(g, w, m, v)


def sum_adamw(partials, w, m, v, *, name, tr):
    nl, k, n = w.shape
    assert nl == len(partials) == 2
    tr = max(c for c in range(8, min(tr, k) + 1, 8) if k % c == 0)
    bc1 = 1.0 - ADAM_B1 ** ADAM_STEP
    bc2 = 1.0 - ADAM_B2 ** ADAM_STEP

    def body(p0_ref, p1_ref, w_ref, m_ref, v_ref, g_ref, d_ref, mo_ref, vo_ref):
        first = pl.program_id(0) == 0
        gv = jnp.where(first, p0_ref[0], p1_ref[0]).astype(F32)
        for s in range(1, N_DEV):
            gv = gv + jnp.where(first, p0_ref[s], p1_ref[s]).astype(F32)
        m_new = ADAM_B1 * m_ref[0] + (1.0 - ADAM_B1) * gv
        v_new = ADAM_B2 * v_ref[0] + (1.0 - ADAM_B2) * (gv * gv)
        g_ref[0] = gv
        mo_ref[0] = m_new
        vo_ref[0] = v_new
        d_ref[0] = -ADAM_LR * ((m_new / bc1) / (jnp.sqrt(v_new / bc2) + ADAM_EPS) + ADAM_WD * w_ref[0])

    spec = pl.BlockSpec((1, tr, n), lambda l, i: (l, i, 0))
    p0spec = pl.BlockSpec((N_DEV, tr, n), lambda l, i: (0, i * (1 - l), 0))
    p1spec = pl.BlockSpec((N_DEV, tr, n), lambda l, i: (0, i * l, 0))
    shape = jax.ShapeDtypeStruct(w.shape, F32)
    return _pcall(
        body, name=name, grid=(nl, k // tr), in_specs=[p0spec, p1spec, spec, spec, spec], out_specs=[spec] * 4,
        out_shape=[shape] * 4, blocks=[((N_DEV, tr, n), BF16)] * 2 + [((1, tr, n), F32)] * 7,
    )(partials[0], partials[1], w, m, v)


def travelling(a, by_cols):
    return jnp.swapaxes(a, -1, -2) if by_cols else a


def _row(v):
    return v.reshape(1, -1)


def ffn_fwd(x, h, w, pre, tag, next_gain):
    ga, gb, s = swiglu_fwd(h, w[pre + "_w_gate"], w[pre + "_w_up"], name=f"{tag}_gateup")
    if callable(w[pre + "_w_down"]):
        w[pre + "_w_down"] = w[pre + "_w_down"](s)
    out, h_next = matmul_res_norm(s, w[pre + "_w_down"], x, next_gain, scale=0.5, tm=512, name=f"{tag}_down")
    return out, h_next, (x, h, ga, gb, s)


def ffn_bwd_weights(dxb, saved, w, pre, tag):
    x, h, a, b, s = saved
    da, db = swiglu_bwd(dxb, w[pre + "_w_down"], a, b, scale=0.5, name=f"{tag}_dgateup")
    g_down = matmul([(s, dxb)], "tn", tm=1408, tn=1024, tk=2048, out_dtype=BF16, scale=0.5, name=f"{tag}_gdown")
    g_gate = matmul([(da, h)], "tn", tm=1408, tn=1024, tk=2048, out_dtype=BF16, name=f"{tag}_ggate")
    g_up = matmul([(db, h)], "tn", tm=1408, tn=1024, tk=2048, out_dtype=BF16, name=f"{tag}_gup")
    return {pre + "_w_gate": g_gate, pre + "_w_up": g_up, pre + "_w_down": g_down}, (da, db)


def ffn_bwd_input(dx, rest, saved, gain, w, pre, tag):
    da, db = rest
    x = saved[0]
    return matmul_rms_bwd([(da, w[pre + "_w_gate"]), (db, w[pre + "_w_up"])], x, gain, dx, tm=256, name=f"{tag}_dh")


def mixer_fwd(x, h, w, tables, tag, next_gain):
    proj = matmul([(h, w["w_in"])], "nt", tm=512, tn=1280, tk=1024, out_dtype=BF16, name=f"{tag}_in")
    qks, vs = rope_split(proj, tables, name=f"{tag}_rope")
    outs, lses = [], []
    for g in range(N_DIL_GROUPS):
        o, lse = dil_fwd(qks[g], vs[g], name=f"{tag}_dil{g}")
        outs.append(o)
        lses.append(lse)
    odil, lse = dil_merge(outs, lses, name=f"{tag}_merge")
    osb = sb_fwd(proj, name=f"{tag}_sb")
    y, u1, u2 = gate_fwd(odil, osb, w["w_proj_dil"], w["w_proj_sb"], proj, name=f"{tag}_gate")
    out, h_next = matmul_res_norm(y, w["w_out"], x, next_gain, scale=1.0, tm=512, name=f"{tag}_out")
    return out, h_next, (x, h, proj, qks, vs, odil, lse, osb, u1, u2, y)


def mixer_bwd_weights(dxb, saved, w, tables, tag):
    x, h, proj, qks, vs, odil, lse, osb, u1, u2, y = saved
    t = x.shape[0]
    g_out = matmul([(y, dxb)], "tn", tm=1024, tn=1024, tk=2048, out_dtype=BF16, name=f"{tag}_gout")
    du1, du2, dgate = gate_bwd(dxb, w["w_out"], u1, u2, proj, name=f"{tag}_dgate")
    g_pd = matmul([(du1, odil)], "tn", tm=1024, tn=256, tk=2048, out_dtype=BF16, name=f"{tag}_gpd")
    g_ps = matmul([(du2, osb)], "tn", tm=1024, tn=256, tk=2048, out_dtype=BF16, name=f"{tag}_gps")
    dodil = matmul([(du1, w["w_proj_dil"])], "nn", tm=512, tn=256, tk=1024, out_dtype=F32, name=f"{tag}_dodil")
    dosb = matmul([(du2, w["w_proj_sb"])], "nn", tm=512, tn=256, tk=1024, out_dtype=F32, name=f"{tag}_dosb")
    dsum, do_wide, lse_wide, dsum_wide = dil_bwd_prep(dodil, odil, lse, name=f"{tag}_dprep")
    dos = [dodil[None]] + list(do_wide)
    lss = [lse[None]] + list(lse_wide)
    dss = [dsum[None]] + list(dsum_wide)
    dqs, dks, dvs = [], [], []
    for g in range(N_DIL_GROUPS):
        dq, dk, dv = dil_bwd(qks[g], vs[g], dos[g], lss[g], dss[g], name=f"{tag}_ddil{g}")
        dqs.append(dq)
        dks.append(dk)
        dvs.append(dv)
    gtot = head_sums(dosb, osb, round_a=True, name=f"{tag}_gsum")
    sb_grads = sb_bwd(proj, dosb, gtot, name=f"{tag}_dsb")
    dproj = rope_join(dqs, dks, dvs, sb_grads, dgate, tables, name=f"{tag}_drope")
    g_in = matmul([(dproj, h)], "tn", tm=1280, tn=1024, tk=2048, out_dtype=BF16, name=f"{tag}_gin")
    return {"w_in": g_in, "w_proj_dil": g_pd, "w_proj_sb": g_ps, "w_out": g_out}, dproj


def mixer_bwd_input(dx, dproj, saved, gain, w, tag):
    x = saved[0]
    return matmul_rms_bwd([(dproj, w["w_in"])], x, gain, dx, tm=256, name=f"{tag}_dh")


def kernel(x, norm_ffn1, ffn1_w_gate, ffn1_w_up, ffn1_w_down, norm_mix, w_in, w_proj_dil, w_proj_sb, w_out, norm_ffn2, ffn2_w_gate, ffn2_w_up, ffn2_w_down, norm_final, loss_target, m_norm_ffn1, m_ffn1_w_gate, m_ffn1_w_up, m_ffn1_w_down, m_norm_mix, m_w_in, m_w_proj_dil, m_w_proj_sb, m_w_out, m_norm_ffn2, m_ffn2_w_gate, m_ffn2_w_up, m_ffn2_w_down, m_norm_final, v_norm_ffn1, v_ffn1_w_gate, v_ffn1_w_up, v_ffn1_w_down, v_norm_mix, v_w_in, v_w_proj_dil, v_w_proj_sb, v_w_out, v_norm_ffn2, v_ffn2_w_gate, v_ffn2_w_up, v_ffn2_w_down, v_norm_final):
    args = dict(locals())
    t = x.shape[1]
    xs = x.reshape(t, D_MODEL)
    target = loss_target.reshape(t, D_MODEL)
    tables = rope_tables(t)

    parts = [(l, p) for l in range(2) for p in SUBBLOCKS]
    gains = {n: args[n] for n in NORM_ROWS}

    shipments = []
    for l, p in parts:
        if (l, p) == parts[0]:
            shipments += [(l, p, SUBBLOCKS[p][:2], CHIP_PEERS), (l, p, SUBBLOCKS[p][2:], ALL_PEERS)]
        else:
            shipments.append((l, p, SUBBLOCKS[p], ALL_PEERS))
    in_flight, order_token = [], jnp.zeros((1, 1), F32)
    for l, p, tensors, masks in shipments:
        shards = [travelling(args[n][l], by_cols).astype(BF16) for n, _, by_cols, _ in tensors]
        shards[0] = shards[0] + order_token.astype(BF16)
        in_flight.append(spread_start(shards, per_peer=False, masks=masks, name=f"gather_start_l{l}_{tensors[0][0]}"))
        order_token = in_flight[-1][-1][0:1, 0:1]

    def arrived(i, after):
        l, p, tensors, masks = shipments[i]
        tag = f"l{l}_{tensors[0][0]}"
        lands = spread_wait(in_flight[i], after, per_peer=False, masks=masks, name=f"gather_wait_{tag}")
        if masks is CHIP_PEERS:
            lands = relay_wait(relay_start(lands, name=f"gather_relay_{tag}"), name=f"gather_relayed_{tag}")
        return {n: land.reshape(-1, land.shape[-1]) for (n, _, _, _), land in zip(tensors, lands)}

    def weights_of(l, p, after):
        mine = [i for i, s in enumerate(shipments) if s[0:2] == (l, p)]
        w = arrived(mine[0], after)
        for i in mine[1:]:
            for n, _, _, _ in shipments[i][2]:
                w[n] = functools.partial(lambda after, i, n: arrived(i, after)[n], i=i, n=n)
        return w

    saved, weights = {}, {}
    act = xs
    h = rms_fwd(xs, _row(gains["norm_ffn1"][0]) + order_token, name="l0_ffn1_norm")
    for i, (l, p) in enumerate(parts):
        weights[(l, p)] = weights_of(l, p, act)
        nl, np_ = parts[i + 1] if i + 1 < len(parts) else (None, None)
        next_gain = _row(gains["norm_" + np_][nl]) if np_ else None
        if p == "mix":
            act, h, saved[(l, p)] = mixer_fwd(act, h, weights[(l, p)], tables, f"l{l}_mix", next_gain)
        else:
            act, h, saved[(l, p)] = ffn_fwd(act, h, weights[(l, p)], p, f"l{l}_{p}", next_gain)
    dx, dxb, g_final, loss_part = final_loss(act, _row(norm_final), target, name="loss_head")

    gain_grads, sent = {}, {}
    order_token = jnp.zeros((1, 1), F32)
    for l, p in reversed(parts):
        w, sv = weights[(l, p)], saved[(l, p)]
        if p == "mix":
            gw, rest = mixer_bwd_weights(dxb, sv, w, tables, f"l{l}_mix")
        else:
            gw, rest = ffn_bwd_weights(dxb, sv, w, p, f"l{l}_{p}")
        slices = [gw[n].reshape(N_DEV, -1, gw[n].shape[-1]) for n, _, _, _ in SUBBLOCKS[p]]
        sent[(l, p)] = spread_start(slices, per_peer=True, name=f"reduce_start_l{l}_{p}")
        gain = _row(gains["norm_" + p][l]) + sent[(l, p)][-1][0:1, 0:1]
        if p == "mix":
            dx, dxb, gain_grads[("norm_mix", l)] = mixer_bwd_input(dx, rest, sv, gain, w, f"l{l}_mix")
        else:
            dx, dxb, gain_grads[("norm_" + p, l)] = ffn_bwd_input(dx, rest, sv, gain, w, p, f"l{l}_{p}")

    partials, big_all = {}, [{}, {}, {}, {}]

    def receive(l, p, after):
        lands = spread_wait(sent[(l, p)], after, per_peer=True, name=f"reduce_wait_l{l}_{p}")
        for (n, _, _, _), land in zip(SUBBLOCKS[p], lands):
            partials.setdefault(n, [None, None])[l] = land

    def update(p):
        for n, _, by_cols, _ in SUBBLOCKS[p]:
            outs = sum_adamw(partials[n], travelling(args[n], by_cols), travelling(args["m_" + n], by_cols),
                             travelling(args["v_" + n], by_cols), tr=256, name=f"update_{n}")
            for kind, arr in enumerate(outs):
                big_all[kind][n] = travelling(arr, by_cols)
        return outs[1]

    for l, p in reversed(parts[1:]):
        receive(l, p, dx)
    update("ffn2")
    done = update("mix")
    receive(*parts[0], done)
    done = update("ffn1")

    loss_row = jnp.pad(loss_part[:, :1], ((0, 0), (0, LANES - 1)))
    small = jnp.concatenate([gain_grads[(n, l)] for n in NORM_ROWS for l in range(2)] + [g_final, loss_row], axis=0)
    small_g = sum_partials(all_gather_rows(small, done, name="gather_gain_grads"), tr=8, name="sum_gain_grads")
    zero_row = jnp.zeros((1, LANES), F32)
    small_of = lambda pre: jnp.concatenate([args[pre + n] for n in NORM_ROWS] + [_row(args[pre + "norm_final"]), zero_row], axis=0)[None]
    small_out = adamw(small_g[None], small_of(""), small_of("m_"), small_of("v_"), tr=8, name="update_gains")
    small_all = [small_g] + [o[0] for o in small_out]

    def gains_of(s):
        out = {n: s[2 * i:2 * i + 2] for i, n in enumerate(NORM_ROWS)}
        out["norm_final"] = s[6]
        return out

    order = ["norm_ffn1", "ffn1_w_gate", "ffn1_w_up", "ffn1_w_down", "norm_mix", "w_in", "w_proj_dil", "w_proj_sb", "w_out",
             "norm_ffn2", "ffn2_w_gate", "ffn2_w_up", "ffn2_w_down", "norm_final"]
    results = []
    for kind in range(4):
        both = {**big_all[kind], **gains_of(small_all[kind])}
        results += [both[n] for n in order]
    loss = small_g[7, 0]
    return (loss, dx.reshape(1, t, D_MODEL), *results)
```

```python
import functools

import jax
import jax.numpy as jnp
from jax import lax
from jax.experimental import pallas as pl
from jax.experimental.pallas import tpu as pltpu

F32 = jnp.float32
BF16 = jnp.bfloat16

D_MODEL = 1024
HEAD_DIM = 64
GROUP_W = 256
D_IN = 5120
N_DIL_GROUPS = 3
DIL_SPAN = 128
DILATIONS = (1, 4, 16)
ROPE_THETA = 500000.0
ROPE_DIM = 16
RMS_EPS = 1e-6
ATT_SCALE = HEAD_DIM ** -0.5
QS_BLK, KS_BLK, VS_BLK = 9, 10, 11
GATE_DIL_BLK, GATE_SB_BLK = 3, 4

ADAM_LR, ADAM_B1, ADAM_B2, ADAM_EPS, ADAM_WD, ADAM_STEP = 0.001, 0.9, 0.999, 1e-08, 0.01, 10

N_DEV = 8
LANES = 1024
VMEM_PHYSICAL_V7X = 64 << 20
VMEM_TEMP_HEADROOM = 20 << 20

PACK_LAYOUT = (
    ("ffn1_w_gate", 352, True, (1024, 2816)),
    ("ffn1_w_up", 352, True, (1024, 2816)),
    ("ffn1_w_down", 352, False, (2816, 1024)),
    ("w_in", 640, True, (1024, 5120)),
    ("w_proj_dil", 32, True, (256, 1024)),
    ("w_proj_sb", 32, True, (256, 1024)),
    ("w_out", 128, False, (1024, 1024)),
    ("ffn2_w_gate", 352, True, (1024, 2816)),
    ("ffn2_w_up", 352, True, (1024, 2816)),
    ("ffn2_w_down", 352, False, (2816, 1024)),
)
SUBBLOCKS = {"ffn1": PACK_LAYOUT[0:3], "mix": PACK_LAYOUT[3:7], "ffn2": PACK_LAYOUT[7:10]}
NORM_ROWS = ("norm_ffn1", "norm_mix", "norm_ffn2")


def _nbytes(shape, dtype):
    n = 1
    for s in shape:
        n *= s
    return n * jnp.dtype(dtype).itemsize


def _pcall(body, *, name, grid, in_specs, out_specs, out_shape, blocks, scratch_shapes=(), scratch_bytes=0):
    need = 2 * sum(_nbytes(s, d) for s, d in blocks) + scratch_bytes + VMEM_TEMP_HEADROOM
    limit = min(need, VMEM_PHYSICAL_V7X - (4 << 20))
    in_hbm = lambda s: pltpu.HBM(s.shape, s.dtype)
    out_shape = [in_hbm(s) for s in out_shape] if isinstance(out_shape, (list, tuple)) else in_hbm(out_shape)
    call = pl.pallas_call(
        body, name=name, grid=grid, in_specs=in_specs, out_specs=out_specs, out_shape=out_shape,
        scratch_shapes=scratch_shapes,
        compiler_params=pltpu.CompilerParams(vmem_limit_bytes=limit),
    )
    return lambda *args: call(*[pltpu.with_memory_space_constraint(a, pltpu.HBM) for a in args])


def _dot(a, b, form):
    dn = {"nn": (((1,), (0,)), ((), ())), "nt": (((1,), (1,)), ((), ())), "tn": (((0,), (0,)), ((), ()))}[form]
    return lax.dot_general(a.astype(BF16), b.astype(BF16), dn, preferred_element_type=F32)


def _sigmoid(x):
    return 1.0 / (1.0 + jnp.exp(-x))


def matmul(pairs, form, *, tm, tn, tk, out_dtype, name, scale=1.0, res=None):
    a0, b0 = pairs[0]
    if form == "tn":
        kdim, m = a0.shape
        n = b0.shape[1]
    else:
        m, kdim = a0.shape
        n = b0.shape[1] if form == "nn" else b0.shape[0]
    tm, tn, tk = min(tm, m), min(tn, n), min(tk, kdim)
    assert m % tm == 0 and n % tn == 0 and kdim % tk == 0, (name, m, n, kdim, tm, tn, tk)
    nk = kdim // tk
    npairs = len(pairs)

    if form == "tn":
        a_blk, a_map = (tk, tm), (lambda j, i, k: (k, i))
    else:
        a_blk, a_map = (tm, tk), (lambda j, i, k: (i, k))
    if form == "nt":
        b_blk, b_map = (tn, tk), (lambda j, i, k: (j, k))
    else:
        b_blk, b_map = (tk, tn), (lambda j, i, k: (k, j))
    o_map = lambda j, i, k: (i, j)

    def body(*refs):
        ab = refs[:2 * npairs]
        rest = refs[2 * npairs:]
        if res is not None:
            r_ref, o_ref = rest[0], rest[1]
            rest = rest[2:]
        else:
            r_ref, o_ref = None, rest[0]
            rest = rest[1:]

        def partial_sum():
            p = _dot(ab[0][...], ab[1][...], form)
            for q in range(1, npairs):
                p = p + _dot(ab[2 * q][...], ab[2 * q + 1][...], form)
            return p

        def finish(acc):
            out = acc * scale if scale != 1.0 else acc
            if r_ref is not None:
                out = r_ref[...] + out
            o_ref[...] = out.astype(out_dtype)

        if nk == 1:
            finish(partial_sum())
        else:
            acc_ref = rest[0]
            k = pl.program_id(2)

            @pl.when(k == 0)
            def _():
                acc_ref[...] = partial_sum()

            @pl.when(k > 0)
            def _():
                acc_ref[...] += partial_sum()

            @pl.when(k == nk - 1)
            def _():
                finish(acc_ref[...])

    in_specs, args, blocks = [], [], []
    for a, b in pairs:
        in_specs += [pl.BlockSpec(a_blk, a_map), pl.BlockSpec(b_blk, b_map)]
        args += [a, b]
        blocks += [(a_blk, a.dtype), (b_blk, b.dtype)]
    if res is not None:
        in_specs.append(pl.BlockSpec((tm, tn), o_map))
        args.append(res)
        blocks.append(((tm, tn), res.dtype))
    blocks.append(((tm, tn), out_dtype))
    scratch = [pltpu.VMEM((tm, tn), F32)] if nk > 1 else []
    return _pcall(
        body, name=name, grid=(n // tn, m // tm, nk), in_specs=in_specs,
        out_specs=pl.BlockSpec((tm, tn), o_map), out_shape=jax.ShapeDtypeStruct((m, n), out_dtype),
        blocks=blocks, scratch_shapes=scratch, scratch_bytes=(tm * tn * 4 if nk > 1 else 0),
    )(*args)


def swiglu_fwd(h, wg_t, wu_t, *, name, tm=512, tn=1408):
    t, d = h.shape
    f = wg_t.shape[0]
    tm, tn = min(tm, t), min(tn, f)

    def body(h_ref, wg_ref, wu_ref, ga_ref, gb_ref, s_ref):
        hh = h_ref[...]
        a = _dot(hh, wg_ref[...], "nt")
        b = _dot(hh, wu_ref[...], "nt")
        sg = _sigmoid(a)
        silu = a * sg
        ga_ref[...] = (b * (sg * (1.0 + a * (1.0 - sg)))).astype(BF16)
        gb_ref[...] = silu.astype(BF16)
        s_ref[...] = (silu * b).astype(BF16)

    w_spec = pl.BlockSpec((tn, d), lambda j, i: (j, 0))
    o_spec = pl.BlockSpec((tm, tn), lambda j, i: (i, j))
    o_shape = jax.ShapeDtypeStruct((t, f), BF16)
    return _pcall(
        body, name=name, grid=(f // tn, t // tm),
        in_specs=[pl.BlockSpec((tm, d), lambda j, i: (i, 0)), w_spec, w_spec],
        out_specs=[o_spec, o_spec, o_spec], out_shape=[o_shape, o_shape, o_shape],
        blocks=[((tm, d), BF16), ((tn, d), BF16), ((tn, d), BF16)] + [((tm, tn), BF16)] * 3,
    )(h, wg_t, wu_t)


def swiglu_bwd(dyb, wd, ga, gb, *, name, scale, tm=512, tn=1408):
    t, d = dyb.shape
    f = wd.shape[0]
    tm, tn = min(tm, t), min(tn, f)

    def body(dy_ref, wd_ref, ga_ref, gb_ref, da_ref, db_ref):
        ds = _dot(dy_ref[...], wd_ref[...], "nt") * scale
        da_ref[...] = (ds * ga_ref[...].astype(F32)).astype(BF16)
        db_ref[...] = (ds * gb_ref[...].astype(F32)).astype(BF16)

    o_spec = pl.BlockSpec((tm, tn), lambda j, i: (i, j))
    o_shape = jax.ShapeDtypeStruct((t, f), BF16)
    return _pcall(
        body, name=name, grid=(f // tn, t // tm),
        in_specs=[pl.BlockSpec((tm, d), lambda j, i: (i, 0)), pl.BlockSpec((tn, d), lambda j, i: (j, 0)), o_spec, o_spec],
        out_specs=[o_spec, o_spec], out_shape=[o_shape, o_shape],
        blocks=[((tm, d), BF16), ((tn, d), BF16)] + [((tm, tn), BF16)] * 4,
    )(dyb, wd, ga, gb)


def gate_fwd(odil, osb, wpd_t, wps_t, proj, *, name, tm=512):
    t = odil.shape[0]
    tm = min(tm, t)

    def body(od_ref, os_ref, wpd_ref, wps_ref, g1_ref, g2_ref, y_ref, u1_ref, u2_ref):
        u1 = _dot(od_ref[...], wpd_ref[...], "nt")
        u2 = _dot(os_ref[...], wps_ref[...], "nt")
        y = _sigmoid(g1_ref[...].astype(F32)) * u1 + _sigmoid(g2_ref[...].astype(F32)) * u2
        y_ref[...] = y.astype(BF16)
        u1_ref[...] = u1.astype(BF16)
        u2_ref[...] = u2.astype(BF16)

    o_spec = pl.BlockSpec((tm, D_MODEL), lambda i: (i, 0))
    w_spec = pl.BlockSpec((D_MODEL, GROUP_W), lambda i: (0, 0))
    a_spec = pl.BlockSpec((tm, GROUP_W), lambda i: (i, 0))
    o_shape = jax.ShapeDtypeStruct((t, D_MODEL), BF16)
    return _pcall(
        body, name=name, grid=(t // tm,),
        in_specs=[a_spec, a_spec, w_spec, w_spec,
                  pl.BlockSpec((tm, D_MODEL), lambda i: (i, GATE_DIL_BLK)),
                  pl.BlockSpec((tm, D_MODEL), lambda i: (i, GATE_SB_BLK))],
        out_specs=[o_spec, o_spec, o_spec], out_shape=[o_shape, o_shape, o_shape],
        blocks=[((tm, GROUP_W), F32)] * 2 + [((D_MODEL, GROUP_W), BF16)] * 2 + [((tm, D_MODEL), BF16)] * 5,
    )(odil, osb, wpd_t, wps_t, proj, proj)


def gate_bwd(dxb, wout, u1, u2, proj, *, name, tm=512):
    t = dxb.shape[0]
    tm = min(tm, t)

    def body(dx_ref, w_ref, u1_ref, u2_ref, g1_ref, g2_ref, du1_ref, du2_ref, dg_ref):
        dy = _dot(dx_ref[...], w_ref[...], "nt")
        s1 = _sigmoid(g1_ref[...].astype(F32))
        s2 = _sigmoid(g2_ref[...].astype(F32))
        du1_ref[...] = (dy * s1).astype(BF16)
        du2_ref[...] = (dy * s2).astype(BF16)
        dg_ref[:, :D_MODEL] = (dy * u1_ref[...].astype(F32) * s1 * (1.0 - s1)).astype(BF16)
        dg_ref[:, D_MODEL:] = (dy * u2_ref[...].astype(F32) * s2 * (1.0 - s2)).astype(BF16)

    o_spec = pl.BlockSpec((tm, D_MODEL), lambda i: (i, 0))
    o_shape = jax.ShapeDtypeStruct((t, D_MODEL), BF16)
    return _pcall(
        body, name=name, grid=(t // tm,),
        in_specs=[o_spec, pl.BlockSpec((D_MODEL, D_MODEL), lambda i: (0, 0)), o_spec, o_spec,
                  pl.BlockSpec((tm, D_MODEL), lambda i: (i, GATE_DIL_BLK)),
                  pl.BlockSpec((tm, D_MODEL), lambda i: (i, GATE_SB_BLK))],
        out_specs=[o_spec, o_spec, pl.BlockSpec((tm, 2 * D_MODEL), lambda i: (i, 0))],
        out_shape=[o_shape, o_shape, jax.ShapeDtypeStruct((t, 2 * D_MODEL), BF16)],
        blocks=[((tm, D_MODEL), BF16)] * 9 + [((D_MODEL, D_MODEL), BF16)],
    )(dxb, wout, u1, u2, proj, proj)


def rms_fwd(x, gain, *, name, tm=512):
    t, d = x.shape
    tm = min(tm, t)

    def body(x_ref, g_ref, h_ref):
        xv = x_ref[...]
        rstd = lax.rsqrt(jnp.mean(xv * xv, axis=1, keepdims=True) + RMS_EPS)
        h_ref[...] = (xv * rstd * g_ref[...]).astype(BF16)

    return _pcall(
        body, name=name, grid=(t // tm,),
        in_specs=[pl.BlockSpec((tm, d), lambda i: (i, 0)), pl.BlockSpec((1, d), lambda i: (0, 0))],
        out_specs=pl.BlockSpec((tm, d), lambda i: (i, 0)), out_shape=jax.ShapeDtypeStruct((t, d), BF16),
        blocks=[((tm, d), F32), ((tm, d), BF16)],
    )(x, gain)


def matmul_res_norm(a, b, res, next_gain, *, scale, tm, name):
    t, k = a.shape
    d = b.shape[1]
    tm = min(tm, t)
    with_norm = next_gain is not None

    def body(a_ref, b_ref, r_ref, *rest):
        out = r_ref[...] + _dot(a_ref[...], b_ref[...], "nn") * scale
        if with_norm:
            g_ref, o_ref, h_ref = rest
            rstd = lax.rsqrt(jnp.mean(out * out, axis=1, keepdims=True) + RMS_EPS)
            h_ref[...] = (out * rstd * g_ref[...]).astype(BF16)
        else:
            o_ref, = rest
        o_ref[...] = out

    row = pl.BlockSpec((tm, d), lambda i: (i, 0))
    in_specs = [pl.BlockSpec((tm, k), lambda i: (i, 0)), pl.BlockSpec((k, d), lambda i: (0, 0)), row]
    args = [a, b, res]
    out_specs, out_shape = [row], [jax.ShapeDtypeStruct((t, d), F32)]
    if with_norm:
        in_specs.append(pl.BlockSpec((1, d), lambda i: (0, 0)))
        args.append(next_gain)
        out_specs.append(row)
        out_shape.append(jax.ShapeDtypeStruct((t, d), BF16))
    outs = _pcall(
        body, name=name, grid=(t // tm,), in_specs=in_specs, out_specs=out_specs, out_shape=out_shape,
        blocks=[((tm, k), a.dtype), ((k, d), b.dtype), ((tm, d), F32), ((tm, d), F32), ((tm, d), BF16)],
    )(*args)
    return (outs[0], outs[1]) if with_norm else (outs[0], None)


def _rms_bwd_rows(dhv, xv, g, drv):
    rstd = lax.rsqrt(jnp.mean(xv * xv, axis=1, keepdims=True) + RMS_EPS)
    xh = xv * rstd
    dxh = dhv * g
    dx = drv + rstd * (dxh - xh * jnp.mean(dxh * xh, axis=1, keepdims=True))
    return dx, jnp.sum(dhv * xh, axis=0, keepdims=True)


def matmul_rms_bwd(pairs, x, gain, dres, *, tm, name):
    t, d = x.shape
    tm = min(tm, t)
    npairs = len(pairs)

    def body(*refs):
        ab = refs[:2 * npairs]
        x_ref, g_ref, dr_ref, dx_ref, dxb_ref, dg_ref = refs[2 * npairs:]
        dh = _dot(ab[0][...], ab[1][...], "nn")
        for q in range(1, npairs):
            dh = dh + _dot(ab[2 * q][...], ab[2 * q + 1][...], "nn")
        dx, part = _rms_bwd_rows(dh, x_ref[...], g_ref[...], dr_ref[...])
        dx_ref[...] = dx
        dxb_ref[...] = dx.astype(BF16)

        @pl.when(pl.program_id(0) == 0)
        def _():
            dg_ref[...] = part

        @pl.when(pl.program_id(0) > 0)
        def _():
            dg_ref[...] += part

    in_specs, args, blocks = [], [], []
    for a, b in pairs:
        k = a.shape[1]
        in_specs += [pl.BlockSpec((tm, k), lambda i: (i, 0)), pl.BlockSpec((k, d), lambda i: (0, 0))]
        args += [a, b]
        blocks += [((tm, k), a.dtype), ((k, d), b.dtype)]
    row = pl.BlockSpec((tm, d), lambda i: (i, 0))
    vec = pl.BlockSpec((1, d), lambda i: (0, 0))
    return _pcall(
        body, name=name, grid=(t // tm,), in_specs=in_specs + [row, vec, row], out_specs=[row, row, vec],
        out_shape=[jax.ShapeDtypeStruct((t, d), F32), jax.ShapeDtypeStruct((t, d), BF16), jax.ShapeDtypeStruct((1, d), F32)],
        blocks=blocks + [((tm, d), F32)] * 3 + [((tm, d), BF16)],
    )(*args, x, gain, dres)


def final_loss(x, gain, target, *, name, tm=512):
    t, d = x.shape
    tm = min(tm, t)

    def body(x_ref, g_ref, t_ref, dx_ref, dxb_ref, dg_ref, loss_ref):
        xv = x_ref[...]
        g = g_ref[...]
        rstd = lax.rsqrt(jnp.mean(xv * xv, axis=1, keepdims=True) + RMS_EPS)
        xh = xv * rstd
        err = xh * g - t_ref[...]
        dy = err * (1.0 / d)
        dxh = dy * g
        dx = rstd * (dxh - xh * jnp.mean(dxh * xh, axis=1, keepdims=True))
        dx_ref[...] = dx
        dxb_ref[...] = dx.astype(BF16)
        part = jnp.sum(dy * xh, axis=0, keepdims=True)
        sq = jnp.sum(jnp.sum(err * err, axis=1, keepdims=True), axis=0, keepdims=True) * (0.5 / d)
        lpart = jnp.broadcast_to(sq, (1, 128))

        @pl.when(pl.program_id(0) == 0)
        def _():
            dg_ref[...] = part
            loss_ref[...] = lpart

        @pl.when(pl.program_id(0) > 0)
        def _():
            dg_ref[...] += part
            loss_ref[...] += lpart

    row = pl.BlockSpec((tm, d), lambda i: (i, 0))
    vec = pl.BlockSpec((1, d), lambda i: (0, 0))
    return _pcall(
        body, name=name, grid=(t // tm,), in_specs=[row, vec, row],
        out_specs=[row, row, vec, pl.BlockSpec((1, 128), lambda i: (0, 0))],
        out_shape=[jax.ShapeDtypeStruct((t, d), F32), jax.ShapeDtypeStruct((t, d), BF16),
                   jax.ShapeDtypeStruct((1, d), F32), jax.ShapeDtypeStruct((1, 128), F32)],
        blocks=[((tm, d), F32)] * 3 + [((tm, d), BF16)],
    )(x, gain, target)


def rope_tables(t):
    pos = jnp.arange(t, dtype=F32)
    inv_freq = ROPE_THETA ** (-jnp.arange(0, ROPE_DIM, 2, dtype=F32) / ROPE_DIM)
    ang = pos[:, None] * inv_freq[None, :]
    cos, sin = jnp.cos(ang), jnp.sin(ang)
    half = ROPE_DIM // 2
    pad = HEAD_DIM - ROPE_DIM
    zeros, ones, rest = jnp.zeros((t, half), F32), jnp.ones((t, pad), F32), jnp.zeros((t, pad), F32)
    c = [cos, cos, ones] * 2
    sa = [-sin, zeros, rest] * 2
    sb = [zeros, sin, rest] * 2
    return jnp.concatenate(c + sa + sb, axis=1)


def _rotate(xv, cv, sav, sbv):
    halves = []
    for half in range(2):
        x = xv[:, 128 * half:128 * (half + 1)]
        halves.append(x * cv + pltpu.roll(x, 120, 1) * sav + pltpu.roll(x, 8, 1) * sbv)
    return jnp.concatenate(halves, axis=1)


STAGE_CHUNKS = 4


def _stage(tm):
    return dict(scratch_shapes=[pltpu.VMEM((STAGE_CHUNKS, tm, 128), F32)], scratch_bytes=STAGE_CHUNKS * tm * 128 * 4)


def _split_residues(stage_ref, val, out_ref, d, col, dtype):
    rows, width = val.shape
    if d == 1:
        out_ref[0, :, col:col + width] = val.astype(dtype)
        return
    chunks = width // 128
    for c in range(chunks):
        stage_ref[c] = val[:, 128 * c:128 * (c + 1)]
    for r in range(d):
        for c in range(chunks):
            out_ref[r, :, col + 128 * c:col + 128 * (c + 1)] = stage_ref[c, pl.ds(r, rows // d, stride=d), :].astype(dtype)


def _join_residues(stage_ref, in_ref, d, col=0, width=GROUP_W):
    if d == 1:
        return in_ref[0, :, col:col + width].astype(F32)
    rows = in_ref.shape[1] * d
    chunks = width // 128
    for r in range(d):
        for c in range(chunks):
            stage_ref[c, pl.ds(r, rows // d, stride=d), :] = in_ref[r, :, col + 128 * c:col + 128 * (c + 1)].astype(F32)
    return jnp.concatenate([stage_ref[c] for c in range(chunks)], axis=1)


def rope_split(proj, tables, *, name, tm=512):
    c = sa = sb = tables
    t = tables.shape[0]
    tm = min(tm, t)

    def body(*refs):
        pieces = refs[0:9]
        c_ref, sa_ref, sb_ref = refs[9:12]
        qk_out, v_out = refs[12:15], refs[15:18]
        stage = refs[18]
        cv, sav, sbv = c_ref[...], sa_ref[...], sb_ref[...]
        for g, d in enumerate(DILATIONS):
            for kind in range(3):
                xv = pieces[3 * kind + g][...].astype(F32)
                if kind < 2:
                    _split_residues(stage, _rotate(xv, cv, sav, sbv), qk_out[g], d, GROUP_W * kind, BF16)
                else:
                    _split_residues(stage, xv, v_out[g], d, 0, BF16)

    tabs = [pl.BlockSpec((tm, 128), functools.partial(lambda i, cb: (i, cb), cb=cb)) for cb in range(3)]
    in_specs = [pl.BlockSpec((tm, GROUP_W), functools.partial(lambda i, cb: (i, cb), cb=cb)) for cb in range(9)]
    out_specs = ([pl.BlockSpec((d, tm // d, 2 * GROUP_W), lambda i: (0, i, 0)) for d in DILATIONS]
                 + [pl.BlockSpec((d, tm // d, GROUP_W), lambda i: (0, i, 0)) for d in DILATIONS])
    out_shape = ([jax.ShapeDtypeStruct((d, t // d, 2 * GROUP_W), BF16) for d in DILATIONS]
                 + [jax.ShapeDtypeStruct((d, t // d, GROUP_W), BF16) for d in DILATIONS])
    outs = _pcall(
        body, name=name, grid=(t // tm,), in_specs=in_specs + tabs, out_specs=out_specs, out_shape=out_shape,
        blocks=[((tm, GROUP_W), BF16)] * 18 + [((tm, 128), F32)] * 3,
        **_stage(tm),
    )(*([proj] * 9), c, sa, sb)
    return outs[0:3], outs[3:6]


def rope_join(dqs, dks, dvs, sb_grads, dgate, tables, *, name, tm=512):
    c = sa = sb = tables
    t = tables.shape[0]
    tm = min(tm, t)

    def body(*refs):
        pieces, sb_refs, dgate_ref = refs[0:9], refs[9:12], refs[12]
        c_ref, sa_ref, sb_ref = refs[13:16]
        o_ref, stage = refs[16], refs[17]
        cv, sav, sbv = c_ref[...], -sa_ref[...], -sb_ref[...]
        for kind in range(3):
            for g, d in enumerate(DILATIONS):
                xv = _join_residues(stage, pieces[3 * kind + g], d)
                if kind < 2:
                    xv = _rotate(xv, cv, sav, sbv)
                col = GROUP_W * (3 * kind + g)
                o_ref[:, col:col + GROUP_W] = xv.astype(BF16)
        for j in range(3):
            o_ref[:, GROUP_W * (QS_BLK + j):GROUP_W * (QS_BLK + j + 1)] = sb_refs[j][...].astype(BF16)
        o_ref[:, D_MODEL * GATE_DIL_BLK:] = dgate_ref[...]

    tabs = [pl.BlockSpec((tm, 128), functools.partial(lambda i, cb: (i, cb), cb=cb)) for cb in range(3)]
    nat = lambda w: pl.BlockSpec((tm, w), lambda i: (i, 0))
    in_specs = [pl.BlockSpec((d, tm // d, GROUP_W), lambda i: (0, i, 0)) for _ in range(3) for d in DILATIONS]
    in_specs += [nat(GROUP_W)] * 3 + [nat(2 * D_MODEL)]
    return _pcall(
        body, name=name, grid=(t // tm,), in_specs=in_specs + tabs,
        out_specs=nat(D_IN), out_shape=jax.ShapeDtypeStruct((t, D_IN), BF16),
        blocks=[((tm, GROUP_W), F32)] * 12 + [((tm, 128), F32)] * 3 + [((tm, 2 * D_MODEL), BF16), ((tm, D_IN), BF16)],
        **_stage(tm),
    )(*dqs, *dks, *dvs, *sb_grads, dgate, c, sa, sb)


def _head_mask(h):
    lane = lax.broadcasted_iota(jnp.int32, (1, GROUP_W), 1)
    return (lane // HEAD_DIM) == h


def _band_masks(heads):
    ri = lax.broadcasted_iota(jnp.int32, (heads * DIL_SPAN, DIL_SPAN), 0) % DIL_SPAN
    ci = lax.broadcasted_iota(jnp.int32, (heads * DIL_SPAN, DIL_SPAN), 1)
    return ci <= ri, ci >= ri


def dil_fwd(qk, v, *, name):
    d, nsub, _ = qk.shape
    nblk = nsub // DIL_SPAN

    def body(q_ref, kc_ref, kp_ref, vc_ref, vp_ref, o_ref, lse_ref):
        nb = pl.program_id(1)
        kk = jnp.concatenate([kp_ref[0], kc_ref[0]], axis=0)
        vv = jnp.concatenate([vp_ref[0], vc_ref[0]], axis=0)
        s = _dot(_stack_heads(q_ref[0] * ATT_SCALE), kk, "nt")
        ri = lax.broadcasted_iota(jnp.int32, s.shape, 0) % DIL_SPAN
        ci = lax.broadcasted_iota(jnp.int32, s.shape, 1)
        valid = ((ci < DIL_SPAN) & (ci >= ri) & (nb > 0)) | ((ci >= DIL_SPAN) & (ci - DIL_SPAN <= ri))
        s = jnp.where(valid, s, -jnp.inf)
        m = jnp.max(s, axis=1, keepdims=True)
        p = jnp.exp(s - m)
        den = jnp.sum(p, axis=1, keepdims=True)
        o_ref[0] = _unstack_heads(_dot(p, vv, "nn") / den, DIL_SPAN)
        lse = m + jnp.log(den)
        for h in range(4):
            lse_ref[0, :, 128 * h:128 * (h + 1)] = jnp.broadcast_to(lse[DIL_SPAN * h:DIL_SPAN * (h + 1)], (DIL_SPAN, 128))

    blk = (1, DIL_SPAN, GROUP_W)
    sblk = (1, DIL_SPAN, 512)
    prv = lambda nb: jnp.maximum(nb - 1, 0)
    return _pcall(
        body, name=name, grid=(d, nblk),
        in_specs=[pl.BlockSpec(blk, lambda r, nb: (r, nb, 0)),
                  pl.BlockSpec(blk, lambda r, nb: (r, nb, 1)),
                  pl.BlockSpec(blk, lambda r, nb: (r, prv(nb), 1)),
                  pl.BlockSpec(blk, lambda r, nb: (r, nb, 0)),
                  pl.BlockSpec(blk, lambda r, nb: (r, prv(nb), 0))],
        out_specs=[pl.BlockSpec(blk, lambda r, nb: (r, nb, 0)), pl.BlockSpec(sblk, lambda r, nb: (r, nb, 0))],
        out_shape=[jax.ShapeDtypeStruct((d, nsub, GROUP_W), F32), jax.ShapeDtypeStruct((d, nsub, 512), F32)],
        blocks=[(blk, BF16)] * 5 + [(blk, F32), (sblk, F32)],
    )(qk, qk, qk, v, v)


def dil_merge(outs, lses, *, name, tm=512):
    t = outs[0].shape[0] * outs[0].shape[1]
    tm = min(tm, t)

    def body(o0, o1, o2, l0, l1, l2, o_ref, lse_ref, stage):
        ls = [_join_residues(stage, l, d, 0, 512) for l, d in zip((l0, l1, l2), DILATIONS)]
        m = jnp.maximum(jnp.maximum(ls[0], ls[1]), ls[2])
        tot = m + jnp.log(jnp.exp(ls[0] - m) + jnp.exp(ls[1] - m) + jnp.exp(ls[2] - m))
        lse_ref[...] = tot
        lane = lax.broadcasted_iota(jnp.int32, (1, 128), 1)
        first = lane < HEAD_DIM
        acc = jnp.zeros((tm, GROUP_W), F32)
        for og, lg, d in zip((o0, o1, o2), ls, DILATIONS):
            w = jnp.exp(lg - tot)
            wide = jnp.concatenate([jnp.where(first, w[:, 0:128], w[:, 128:256]),
                                    jnp.where(first, w[:, 256:384], w[:, 384:512])], axis=1)
            acc = acc + wide * _join_residues(stage, og, d)
        o_ref[...] = acc

    o_in = [pl.BlockSpec((d, tm // d, GROUP_W), lambda i: (0, i, 0)) for d in DILATIONS]
    l_in = [pl.BlockSpec((d, tm // d, 512), lambda i: (0, i, 0)) for d in DILATIONS]
    return _pcall(
        body, name=name, grid=(t // tm,), in_specs=o_in + l_in,
        out_specs=[pl.BlockSpec((tm, GROUP_W), lambda i: (i, 0)), pl.BlockSpec((tm, 512), lambda i: (i, 0))],
        out_shape=[jax.ShapeDtypeStruct((t, GROUP_W), F32), jax.ShapeDtypeStruct((t, 512), F32)],
        blocks=[((tm, GROUP_W), F32)] * 4 + [((tm, 512), F32)] * 4,
        **_stage(tm),
    )(*outs, *lses)


def dil_bwd_prep(do, o, lse, *, name, tm=512):
    t = do.shape[0]
    tm = min(tm, t)
    wide = DILATIONS[1:]

    def body(do_ref, o_ref, lse_ref, ds_ref, *rest):
        do_out, lse_out, ds_out = rest[0:2], rest[2:4], rest[4:6]
        stage = rest[6]
        dov = do_ref[...]
        prod = dov * o_ref[...]
        for h in range(4):
            s = jnp.sum(jnp.where(_head_mask(h), prod, 0.0), axis=1, keepdims=True)
            ds_ref[:, 128 * h:128 * (h + 1)] = jnp.broadcast_to(s, (tm, 128))
        for i, d in enumerate(wide):
            _split_residues(stage, dov, do_out[i], d, 0, BF16)
            _split_residues(stage, lse_ref[...], lse_out[i], d, 0, F32)
            _split_residues(stage, ds_ref[...], ds_out[i], d, 0, F32)

    nat = lambda w: pl.BlockSpec((tm, w), lambda i: (i, 0))
    res = lambda d, w: pl.BlockSpec((d, tm // d, w), lambda i: (0, i, 0))
    shape = lambda d, w, dt: jax.ShapeDtypeStruct((d, t // d, w), dt)
    outs = _pcall(
        body, name=name, grid=(t // tm,), in_specs=[nat(GROUP_W), nat(GROUP_W), nat(512)],
        out_specs=[nat(512)] + [res(d, GROUP_W) for d in wide] + [res(d, 512) for d in wide] * 2,
        out_shape=([jax.ShapeDtypeStruct((t, 512), F32)] + [shape(d, GROUP_W, BF16) for d in wide]
                   + [shape(d, 512, F32) for d in wide] * 2),
        blocks=[((tm, GROUP_W), F32)] * 3 + [((tm, 512), F32)] * 6,
        **_stage(tm),
    )(do, o, lse)
    return outs[0], outs[1:3], outs[3:5], outs[5:7]


def head_sums(a, b, *, name, round_a=False, tm=512):
    t = a.shape[0]
    tm = min(tm, t)

    def body(a_ref, b_ref, o_ref):
        av = a_ref[...]
        if round_a:
            av = av.astype(BF16).astype(F32)
        prod = av * b_ref[...]
        for h in range(4):
            s = jnp.sum(jnp.where(_head_mask(h), prod, 0.0), axis=1, keepdims=True)
            o_ref[:, 128 * h:128 * (h + 1)] = jnp.broadcast_to(s, (tm, 128))

    spec = pl.BlockSpec((tm, GROUP_W), lambda i: (i, 0))
    return _pcall(
        body, name=name, grid=(t // tm,), in_specs=[spec, spec],
        out_specs=pl.BlockSpec((tm, 512), lambda i: (i, 0)), out_shape=jax.ShapeDtypeStruct((t, 512), F32),
        blocks=[((tm, GROUP_W), F32)] * 2 + [((tm, 512), F32)],
    )(a, b)


def dil_bwd(qk, v, do, lse, dsum, *, name):
    d, nsub, _ = qk.shape
    nblk = nsub // DIL_SPAN

    def body(qa_ref, qb_ref, kc_ref, kp_ref, vc_ref, vp_ref, doa_ref, dob_ref, la_ref, lb_ref, sa_ref, sb_ref,
             dq_ref, dk_ref, dv_ref):
        nb = pl.program_id(1)
        own, band = _band_masks(4)
        prev = band & (nb > 0)
        nxt = band & (nb < nblk - 1)
        kc, kp, vc, vp = kc_ref[0], kp_ref[0], vc_ref[0], vp_ref[0]
        qas, qbs = _stack_heads(qa_ref[0] * ATT_SCALE), _stack_heads(qb_ref[0] * ATT_SCALE)
        das, dbs = _stack_heads(doa_ref[0].astype(BF16)), _stack_heads(dob_ref[0].astype(BF16))
        stat = lambda ref: jnp.concatenate([ref[0, :, 128 * h:128 * (h + 1)] for h in range(4)], axis=0)
        la, lb, sa, sb = stat(la_ref), stat(lb_ref), stat(sa_ref), stat(sb_ref)

        def probs(qs, ds_, k, v, mask, l, s):
            p = jnp.where(mask, jnp.exp(_dot(qs, k, "nt") - l), 0.0)
            dsc = p * (_dot(ds_, v, "nt") - s)
            return p.astype(BF16), dsc.astype(BF16)

        p_cc, ds_cc = probs(qas, das, kc, vc, own, la, sa)
        _, ds_cp = probs(qas, das, kp, vp, prev, la, sa)
        p_nc, ds_nc = probs(qbs, dbs, kc, vc, nxt, lb, sb)
        dq_ref[0] = _unstack_heads(_dot(ds_cc, kc, "nn") + _dot(ds_cp, kp, "nn"), DIL_SPAN) * ATT_SCALE
        dk_ref[0] = _dot(ds_cc, qas, "tn") + _dot(ds_nc, qbs, "tn")
        dv_ref[0] = _dot(p_cc, das, "tn") + _dot(p_nc, dbs, "tn")

    blk = (1, DIL_SPAN, GROUP_W)
    sblk = (1, DIL_SPAN, 512)
    prv = lambda nb: jnp.maximum(nb - 1, 0)
    nxt_ = lambda nb: jnp.minimum(nb + 1, nblk - 1)
    cur_at = lambda c: pl.BlockSpec(blk, functools.partial(lambda r, nb, c: (r, nb, c), c=c))
    prv_at = lambda c: pl.BlockSpec(blk, functools.partial(lambda r, nb, c: (r, prv(nb), c), c=c))
    nxt_at = lambda c: pl.BlockSpec(blk, functools.partial(lambda r, nb, c: (r, nxt_(nb), c), c=c))
    s_cur = pl.BlockSpec(sblk, lambda r, nb: (r, nb, 0))
    s_nxt = pl.BlockSpec(sblk, lambda r, nb: (r, nxt_(nb), 0))
    o_spec = pl.BlockSpec(blk, lambda r, nb: (r, nb, 0))
    o_shape = jax.ShapeDtypeStruct((d, nsub, GROUP_W), F32)
    return _pcall(
        body, name=name, grid=(d, nblk),
        in_specs=[cur_at(0), nxt_at(0), cur_at(1), prv_at(1), cur_at(0), prv_at(0), cur_at(0), nxt_at(0),
                  s_cur, s_nxt, s_cur, s_nxt],
        out_specs=[o_spec, o_spec, o_spec], out_shape=[o_shape, o_shape, o_shape],
        blocks=[(blk, BF16)] * 6 + [(blk, F32)] * 5 + [(sblk, F32)] * 4,
    )(qk, qk, qk, qk, v, v, do, do, lse, lse, dsum, dsum)


def _tri_dot(x, b):
    hi = x.astype(BF16)
    lo = (x - hi.astype(F32)).astype(BF16)
    return _dot(jnp.concatenate([hi, lo], axis=1), jnp.concatenate([b, b], axis=0), "nn")


SB_TILE = 256


def _stack_heads(a):
    return jnp.concatenate([jnp.where(_head_mask(h), a, jnp.zeros_like(a)) for h in range(4)], axis=0)


def _unstack_heads(acc, rows):
    out = acc[0:rows]
    for h in range(1, 4):
        out = jnp.where(_head_mask(h), acc[h * rows:(h + 1) * rows], out)
    return out


def _tri_masks(n):
    ri = lax.broadcasted_iota(jnp.int32, (n, n), 0)
    ci = lax.broadcasted_iota(jnp.int32, (n, n), 1)
    return (ri > ci).astype(BF16), (ri >= ci).astype(BF16)


def _sb_weights(qs, kt, after, c_keep, diagonal):
    z = _dot(qs, kt, "nt")
    lbeta = jnp.minimum(z, 0.0) - jnp.log(1.0 + jnp.exp(-jnp.abs(z)))
    lkeep = lbeta - z
    past = None
    if diagonal:
        n = SB_TILE
        past = lax.broadcasted_iota(jnp.int32, z.shape, 1) < lax.broadcasted_iota(jnp.int32, z.shape, 0) % n
        lkeep = jnp.where(past, lkeep, 0.0)
    w = jnp.exp(lbeta + _tri_dot(lkeep, after) + c_keep)
    if diagonal:
        w = jnp.where(past, w, 0.0)
    return z, past, lbeta, lkeep, w


def sb_fwd(proj, *, name):
    t = proj.shape[0]
    n = SB_TILE
    assert t % n == 0

    def body(q_ref, k_ref, v_ref, o_ref, acc_ref):
        qb = pl.program_id(0)
        qs = _stack_heads(q_ref[...] * ATT_SCALE)
        after, _ = _tri_masks(n)

        def tile(off, diagonal, c_keep):
            kt = k_ref[pl.ds(off, n), :]
            vt = v_ref[pl.ds(off, n), :]
            _, _, _, lkeep, w = _sb_weights(qs, kt, after, c_keep, diagonal)
            pv = _tri_dot(w, vt)
            if diagonal:
                acc_ref[...] = pv
            else:
                acc_ref[...] += pv
            return c_keep + jnp.sum(lkeep, axis=1, keepdims=True)

        c0 = tile(pl.multiple_of(qb * n, n), True, jnp.zeros((4 * n, 1), F32))
        lax.fori_loop(0, qb, lambda it, c: tile(pl.multiple_of((qb - 1 - it) * n, n), False, c), c0)
        o_ref[...] = _unstack_heads(acc_ref[...], n)

    full = lambda cb: pl.BlockSpec((t, GROUP_W), functools.partial(lambda i, cb: (0, cb), cb=cb))
    return _pcall(
        body, name=name, grid=(t // n,),
        in_specs=[pl.BlockSpec((n, GROUP_W), lambda i: (i, QS_BLK)), full(KS_BLK), full(VS_BLK)],
        out_specs=pl.BlockSpec((n, GROUP_W), lambda i: (i, 0)), out_shape=jax.ShapeDtypeStruct((t, GROUP_W), F32),
        blocks=[((n, GROUP_W), BF16), ((t, GROUP_W), BF16), ((t, GROUP_W), BF16), ((n, GROUP_W), F32)],
        scratch_shapes=[pltpu.VMEM((4 * n, GROUP_W), F32)], scratch_bytes=4 * n * GROUP_W * 4,
    )(proj, proj, proj)


def sb_bwd(proj, do, gtot, *, name):
    t = proj.shape[0]
    n = SB_TILE
    assert t % n == 0

    def body(q_ref, k_ref, v_ref, do_ref, gt_ref, dq_ref, dk_ref, dv_ref, acc_ref):
        qb = pl.program_id(0)

        @pl.when(qb == 0)
        def _():
            dk_ref[...] = jnp.zeros_like(dk_ref)
            dv_ref[...] = jnp.zeros_like(dv_ref)

        qs = _stack_heads(q_ref[...] * ATT_SCALE)
        dos = _stack_heads(do_ref[...].astype(BF16))
        gt = jnp.concatenate([jnp.max(gt_ref[:, 128 * h:128 * (h + 1)], axis=1, keepdims=True) for h in range(4)], axis=0)
        after, from_on = _tri_masks(n)

        def tile(off, diagonal, carry):
            c_keep, c_g = carry
            kt = k_ref[pl.ds(off, n), :]
            vt = v_ref[pl.ds(off, n), :]
            z, past, lbeta, lkeep, w = _sb_weights(qs, kt, after, c_keep, diagonal)
            gw = w * _dot(dos, vt, "nt")
            big_g = gt - (_tri_dot(gw, from_on) + c_g)
            dz = gw * jnp.exp(lbeta - z) - big_g * jnp.exp(lbeta)
            if diagonal:
                dz = jnp.where(past, dz, 0.0)
            dz = dz.astype(BF16)
            dk_ref[pl.ds(off, n), :] += _dot(dz, qs, "tn")
            dv_ref[pl.ds(off, n), :] += _dot(w, dos, "tn")
            dq = _dot(dz, kt, "nn")
            if diagonal:
                acc_ref[...] = dq
            else:
                acc_ref[...] += dq
            return c_keep + jnp.sum(lkeep, axis=1, keepdims=True), c_g + jnp.sum(gw, axis=1, keepdims=True)

        zero_col = jnp.zeros((4 * n, 1), F32)
        c0 = tile(pl.multiple_of(qb * n, n), True, (zero_col, zero_col))
        lax.fori_loop(0, qb, lambda it, c: tile(pl.multiple_of((qb - 1 - it) * n, n), False, c), c0)
        dq_ref[...] = _unstack_heads(acc_ref[...], n) * ATT_SCALE

    full = lambda cb: pl.BlockSpec((t, GROUP_W), functools.partial(lambda i, cb: (0, cb), cb=cb))
    whole = pl.BlockSpec((t, GROUP_W), lambda i: (0, 0))
    rowblk = pl.BlockSpec((n, GROUP_W), lambda i: (i, 0))
    shape = jax.ShapeDtypeStruct((t, GROUP_W), F32)
    return _pcall(
        body, name=name, grid=(t // n,),
        in_specs=[pl.BlockSpec((n, GROUP_W), lambda i: (i, QS_BLK)), full(KS_BLK), full(VS_BLK), rowblk,
                  pl.BlockSpec((n, 512), lambda i: (i, 0))],
        out_specs=[rowblk, whole, whole], out_shape=[shape, shape, shape],
        blocks=[((n, GROUP_W), BF16), ((t, GROUP_W), BF16), ((t, GROUP_W), BF16), ((n, GROUP_W), F32),
                ((n, 512), F32), ((n, GROUP_W), F32), ((t, GROUP_W), F32), ((t, GROUP_W), F32)],
        scratch_shapes=[pltpu.VMEM((4 * n, GROUP_W), F32)], scratch_bytes=4 * n * GROUP_W * 4,
    )(proj, proj, proj, do, gtot)


def _mesh_place():
    return lax.axis_index("x"), lax.axis_index("y"), lax.axis_index("c")


def _flip(place, mask):
    x, y, c = place
    return ((1 - x) if mask & 4 else x, (1 - y) if mask & 2 else y, (1 - c) if mask & 1 else c)


def _dev_index(place):
    x, y, c = place
    return 4 * x + 2 * y + c


HBM_SPEC = pl.BlockSpec(memory_space=pltpu.HBM)


def all_gather_rows(shard, after, *, name):
    rows, lanes = shard.shape

    def body(x_ref, after_ref, out_ref, send_sems, recv_sems, local_sem):
        me = _mesh_place()
        x, y, c = me
        sibling = _flip(me, 1)
        chips = [_flip(me, 4), _flip(me, 2), _flip(me, 6)]

        def copy(k, block, to, src=None):
            dst = out_ref.at[_dev_index(block)]
            return pltpu.make_async_remote_copy(
                src_ref=dst if src is None else src, dst_ref=dst, send_sem=send_sems.at[k], recv_sem=recv_sems.at[k],
                device_id=to, device_id_type=pl.DeviceIdType.MESH)

        mine = pltpu.make_async_copy(x_ref, out_ref.at[_dev_index(me)], local_sem)
        mine.start()
        first = [copy(0, me, sibling, src=x_ref)] + [copy(1 + j, me, chip, src=x_ref) for j, chip in enumerate(chips)]
        for cp in first:
            cp.start()
        passed = [copy(4 + j, chip, sibling) for j, chip in enumerate(chips)]
        for j, chip in enumerate(chips):
            copy(1 + j, chip, me).wait_recv()
            passed[j].start()
        copy(0, sibling, me).wait_recv()
        for j, chip in enumerate(chips):
            copy(4 + j, _flip(chip, 1), me).wait_recv()
        for cp in first + passed:
            cp.wait_send()
        mine.wait()

    return pl.pallas_call(
        body, name=name, in_specs=[HBM_SPEC, pl.BlockSpec(memory_space=pl.ANY)], out_specs=HBM_SPEC,
        out_shape=jax.ShapeDtypeStruct((N_DEV, rows, lanes), shard.dtype),
        scratch_shapes=[pltpu.SemaphoreType.DMA((7,)), pltpu.SemaphoreType.DMA((7,)), pltpu.SemaphoreType.DMA],
    )(shard, after)


SEM_SPEC = pl.BlockSpec(memory_space=pltpu.SEMAPHORE)
DATAFLOW_EFFECT = pltpu.SideEffectType.DATAFLOW_SIDE_EFFECTING


ALL_PEERS = tuple(range(1, N_DEV))
CHIP_PEERS = (1, 4, 2, 6)
OTHER_CHIPS = (4, 2, 6)


def _spread_copies(src_refs, land_refs, send_sems, recv_sems, per_peer, masks, arriving):
    me = _mesh_place()
    my = _dev_index(me)
    remote, local = [], []
    for t, (src_ref, land_ref) in enumerate(zip(src_refs, land_refs)):
        for i, mask in enumerate(masks):
            peer = _flip(me, mask)
            data_of = my if arriving else _dev_index(peer)
            slot = _dev_index(peer) if arriving else my
            k = t * len(masks) + i
            remote.append(pltpu.make_async_remote_copy(
                src_ref=src_ref.at[data_of] if per_peer else src_ref, dst_ref=land_ref.at[slot],
                send_sem=send_sems.at[k], recv_sem=recv_sems.at[k],
                device_id=peer, device_id_type=pl.DeviceIdType.MESH))
        local.append(pltpu.make_async_copy(src_ref.at[my] if per_peer else src_ref, land_ref.at[my],
                                           send_sems.at[len(src_refs) * len(masks) + t]))
    return remote, local


def spread_start(srcs, *, per_peer, name, masks=ALL_PEERS):
    nt = len(srcs)
    zones = [pltpu.HBM((N_DEV,) + (s.shape[1:] if per_peer else s.shape), s.dtype) for s in srcs]

    def body(*refs):
        src_refs, (send_sems, recv_sems) = refs[:nt], refs[nt:nt + 2]
        land_refs, token = refs[2 * nt + 2:3 * nt + 2], refs[3 * nt + 2]
        remote, local = _spread_copies(src_refs, land_refs, send_sems, recv_sems, per_peer, masks, arriving=False)
        for cp in remote + local:
            cp.start()
        token[...] = jnp.zeros_like(token)

    return pl.pallas_call(
        body, name=name, in_specs=(HBM_SPEC,) * nt,
        out_shape=(pltpu.SemaphoreType.DMA((nt * len(masks) + nt,)), pltpu.SemaphoreType.DMA((nt * len(masks),)),
                   *[pltpu.HBM(s.shape, s.dtype) for s in srcs], *zones, jax.ShapeDtypeStruct((8, 128), F32)),
        out_specs=(SEM_SPEC, SEM_SPEC) + (HBM_SPEC,) * (2 * nt) + (pl.BlockSpec(memory_space=pltpu.VMEM),),
        input_output_aliases={t: 2 + t for t in range(nt)},
        compiler_params=pltpu.CompilerParams(has_side_effects=DATAFLOW_EFFECT),
    )(*[pltpu.with_memory_space_constraint(s, pltpu.HBM) for s in srcs])


def spread_wait(started, after, *, per_peer, name, masks=ALL_PEERS):
    nt = (len(started) - 3) // 2
    send_sems, recv_sems = started[0:2]
    srcs_thru, lands_thru = started[2:2 + nt], started[2 + nt:2 + 2 * nt]

    def body(*refs):
        src_refs, land_refs = refs[:nt], refs[nt:2 * nt]
        send_sems, recv_sems = refs[2 * nt:2 * nt + 2]
        remote, local = _spread_copies(src_refs, land_refs, send_sems, recv_sems, per_peer, masks, arriving=True)
        for cp in remote:
            cp.wait_send()
            cp.wait_recv()
        for cp in local:
            cp.wait()

    outs = pl.pallas_call(
        body, name=name, in_specs=(HBM_SPEC,) * (2 * nt) + (SEM_SPEC, SEM_SPEC, pl.BlockSpec(memory_space=pl.ANY)),
        out_shape=tuple(pltpu.HBM(a.shape, a.dtype) for a in (*srcs_thru, *lands_thru)),
        out_specs=(HBM_SPEC,) * (2 * nt), input_output_aliases={t: t for t in range(2 * nt)},
        compiler_params=pltpu.CompilerParams(has_side_effects=DATAFLOW_EFFECT),
    )(*srcs_thru, *lands_thru, send_sems, recv_sems, after)
    return list(outs[nt:])


def _relay_copies(land_refs, send_sems, recv_sems, arriving):
    me = _mesh_place()
    sibling = _flip(me, 1)
    out = []
    for t, land_ref in enumerate(land_refs):
        for i, mask in enumerate(OTHER_CHIPS):
            slot = _dev_index(_flip(sibling if arriving else me, mask))
            k = t * len(OTHER_CHIPS) + i
            out.append(pltpu.make_async_remote_copy(
                src_ref=land_ref.at[slot], dst_ref=land_ref.at[slot], send_sem=send_sems.at[k], recv_sem=recv_sems.at[k],
                device_id=sibling, device_id_type=pl.DeviceIdType.MESH))
    return out


def relay_start(lands, *, name):
    nt = len(lands)
    n_sem = nt * len(OTHER_CHIPS)

    def body(*refs):
        for cp in _relay_copies(refs[:nt], refs[nt], refs[nt + 1], arriving=False):
            cp.start()

    return pl.pallas_call(
        body, name=name, in_specs=(HBM_SPEC,) * nt,
        out_shape=(pltpu.SemaphoreType.DMA((n_sem,)), pltpu.SemaphoreType.DMA((n_sem,)),
                   *[pltpu.HBM(a.shape, a.dtype) for a in lands]),
        out_specs=(SEM_SPEC, SEM_SPEC) + (HBM_SPEC,) * nt, input_output_aliases={t: 2 + t for t in range(nt)},
        compiler_params=pltpu.CompilerParams(has_side_effects=DATAFLOW_EFFECT),
    )(*[pltpu.with_memory_space_constraint(a, pltpu.HBM) for a in lands])


def relay_wait(started, *, name):
    send_sems, recv_sems = started[0:2]
    lands_thru = started[2:]
    nt = len(lands_thru)

    def body(*refs):
        for cp in _relay_copies(refs[:nt], refs[nt], refs[nt + 1], arriving=True):
            cp.wait_send()
            cp.wait_recv()

    return list(pl.pallas_call(
        body, name=name, in_specs=(HBM_SPEC,) * nt + (SEM_SPEC, SEM_SPEC),
        out_shape=tuple(pltpu.HBM(a.shape, a.dtype) for a in lands_thru), out_specs=(HBM_SPEC,) * nt,
        input_output_aliases={t: t for t in range(nt)},
        compiler_params=pltpu.CompilerParams(has_side_effects=DATAFLOW_EFFECT),
    )(*lands_thru, send_sems, recv_sems))


def sum_partials(parts, *, name, tr):
    _, rows, lanes = parts.shape
    assert rows % tr == 0

    def body(p_ref, g_ref):
        g = p_ref[0].astype(F32)
        for k in range(1, N_DEV):
            g = g + p_ref[k].astype(F32)
        g_ref[...] = g

    return _pcall(
        body, name=name, grid=(rows // tr,),
        in_specs=[pl.BlockSpec((N_DEV, tr, lanes), lambda i: (0, i, 0))],
        out_specs=pl.BlockSpec((tr, lanes), lambda i: (i, 0)), out_shape=jax.ShapeDtypeStruct((rows, lanes), F32),
        blocks=[((N_DEV, tr, lanes), parts.dtype), ((tr, lanes), F32)],
    )(parts)


def adamw(g, w, m, v, *, name, tr):
    nl, k, n = w.shape
    tr = max(c for c in range(8, min(tr, k) + 1, 8) if k % c == 0)
    bc1 = 1.0 - ADAM_B1 ** ADAM_STEP
    bc2 = 1.0 - ADAM_B2 ** ADAM_STEP

    def body(g_ref, w_ref, m_ref, v_ref, d_ref, mo_ref, vo_ref):
        gv = g_ref[...]
        m_new = ADAM_B1 * m_ref[...] + (1.0 - ADAM_B1) * gv
        v_new = ADAM_B2 * v_ref[...] + (1.0 - ADAM_B2) * (gv * gv)
        mo_ref[...] = m_new
        vo_ref[...] = v_new
        d_ref[...] = -ADAM_LR * ((m_new / bc1) / (jnp.sqrt(v_new / bc2) + ADAM_EPS) + ADAM_WD * w_ref[...])

    spec = pl.BlockSpec((1, tr, n), lambda l, i: (l, i, 0))
    shape = jax.ShapeDtypeStruct(w.shape, F32)
    return _pcall(
        body, name=name, grid=(nl, k // tr), in_specs=[spec] * 4, out_specs=[spec] * 3, out_shape=[shape] * 3,
        blocks=[((1, tr, n), F32)] * 7,
    )(g, w, m, v)


def sum_adamw(partials, w, m, v, *, name, tr):
    nl, k, n = w.shape
    assert nl == len(partials) == 2
    tr = max(c for c in range(8, min(tr, k) + 1, 8) if k % c == 0)
    bc1 = 1.0 - ADAM_B1 ** ADAM_STEP
    bc2 = 1.0 - ADAM_B2 ** ADAM_STEP

    def body(p0_ref, p1_ref, w_ref, m_ref, v_ref, g_ref, d_ref, mo_ref, vo_ref):
        first = pl.program_id(0) == 0
        gv = jnp.where(first, p0_ref[0], p1_ref[0]).astype(F32)
        for s in range(1, N_DEV):
            gv = gv + jnp.where(first, p0_ref[s], p1_ref[s]).astype(F32)
        m_new = ADAM_B1 * m_ref[0] + (1.0 - ADAM_B1) * gv
        v_new = ADAM_B2 * v_ref[0] + (1.0 - ADAM_B2) * (gv * gv)
        g_ref[0] = gv
        mo_ref[0] = m_new
        vo_ref[0] = v_new
        d_ref[0] = -ADAM_LR * ((m_new / bc1) / (jnp.sqrt(v_new / bc2) + ADAM_EPS) + ADAM_WD * w_ref[0])

    spec = pl.BlockSpec((1, tr, n), lambda l, i: (l, i, 0))
    p0spec = pl.BlockSpec((N_DEV, tr, n), lambda l, i: (0, i * (1 - l), 0))
    p1spec = pl.BlockSpec((N_DEV, tr, n), lambda l, i: (0, i * l, 0))
    shape = jax.ShapeDtypeStruct(w.shape, F32)
    return _pcall(
        body, name=name, grid=(nl, k // tr), in_specs=[p0spec, p1spec, spec, spec, spec], out_specs=[spec] * 4,
        out_shape=[shape] * 4, blocks=[((N_DEV, tr, n), BF16)] * 2 + [((1, tr, n), F32)] * 7,
    )(partials[0], partials[1], w, m, v)


def travelling(a, by_cols):
    return jnp.swapaxes(a, -1, -2) if by_cols else a


def _row(v):
    return v.reshape(1, -1)


def ffn_fwd(x, h, w, pre, tag, next_gain):
    ga, gb, s = swiglu_fwd(h, w[pre + "_w_gate"], w[pre + "_w_up"], name=f"{tag}_gateup")
    if callable(w[pre + "_w_down"]):
        w[pre + "_w_down"] = w[pre + "_w_down"](s)
    out, h_next = matmul_res_norm(s, w[pre + "_w_down"], x, next_gain, scale=0.5, tm=512, name=f"{tag}_down")
    return out, h_next, (x, h, ga, gb, s)


def ffn_bwd_weights(dxb, saved, w, pre, tag):
    x, h, a, b, s = saved
    da, db = swiglu_bwd(dxb, w[pre + "_w_down"], a, b, scale=0.5, name=f"{tag}_dgateup")
    g_down = matmul([(s, dxb)], "tn", tm=1408, tn=1024, tk=2048, out_dtype=BF16, scale=0.5, name=f"{tag}_gdown")
    g_gate = matmul([(da, h)], "tn", tm=1408, tn=1024, tk=2048, out_dtype=BF16, name=f"{tag}_ggate")
    g_up = matmul([(db, h)], "tn", tm=1408, tn=1024, tk=2048, out_dtype=BF16, name=f"{tag}_gup")
    return {pre + "_w_gate": g_gate, pre + "_w_up": g_up, pre + "_w_down": g_down}, (da, db)


def ffn_bwd_input(dx, rest, saved, gain, w, pre, tag):
    da, db = rest
    x = saved[0]
    return matmul_rms_bwd([(da, w[pre + "_w_gate"]), (db, w[pre + "_w_up"])], x, gain, dx, tm=256, name=f"{tag}_dh")


def mixer_fwd(x, h, w, tables, tag, next_gain):
    proj = matmul([(h, w["w_in"])], "nt", tm=512, tn=1280, tk=1024, out_dtype=BF16, name=f"{tag}_in")
    qks, vs = rope_split(proj, tables, name=f"{tag}_rope")
    outs, lses = [], []
    for g in range(N_DIL_GROUPS):
        o, lse = dil_fwd(qks[g], vs[g], name=f"{tag}_dil{g}")
        outs.append(o)
        lses.append(lse)
    odil, lse = dil_merge(outs, lses, name=f"{tag}_merge")
    osb = sb_fwd(proj, name=f"{tag}_sb")
    y, u1, u2 = gate_fwd(odil, osb, w["w_proj_dil"], w["w_proj_sb"], proj, name=f"{tag}_gate")
    out, h_next = matmul_res_norm(y, w["w_out"], x, next_gain, scale=1.0, tm=512, name=f"{tag}_out")
    return out, h_next, (x, h, proj, qks, vs, odil, lse, osb, u1, u2, y)


def mixer_bwd_weights(dxb, saved, w, tables, tag):
    x, h, proj, qks, vs, odil, lse, osb, u1, u2, y = saved
    t = x.shape[0]
    g_out = matmul([(y, dxb)], "tn", tm=1024, tn=1024, tk=2048, out_dtype=BF16, name=f"{tag}_gout")
    du1, du2, dgate = gate_bwd(dxb, w["w_out"], u1, u2, proj, name=f"{tag}_dgate")
    g_pd = matmul([(du1, odil)], "tn", tm=1024, tn=256, tk=2048, out_dtype=BF16, name=f"{tag}_gpd")
    g_ps = matmul([(du2, osb)], "tn", tm=1024, tn=256, tk=2048, out_dtype=BF16, name=f"{tag}_gps")
    dodil = matmul([(du1, w["w_proj_dil"])], "nn", tm=512, tn=256, tk=1024, out_dtype=F32, name=f"{tag}_dodil")
    dosb = matmul([(du2, w["w_proj_sb"])], "nn", tm=512, tn=256, tk=1024, out_dtype=F32, name=f"{tag}_dosb")
    dsum, do_wide, lse_wide, dsum_wide = dil_bwd_prep(dodil, odil, lse, name=f"{tag}_dprep")
    dos = [dodil[None]] + list(do_wide)
    lss = [lse[None]] + list(lse_wide)
    dss = [dsum[None]] + list(dsum_wide)
    dqs, dks, dvs = [], [], []
    for g in range(N_DIL_GROUPS):
        dq, dk, dv = dil_bwd(qks[g], vs[g], dos[g], lss[g], dss[g], name=f"{tag}_ddil{g}")
        dqs.append(dq)
        dks.append(dk)
        dvs.append(dv)
    gtot = head_sums(dosb, osb, round_a=True, name=f"{tag}_gsum")
    sb_grads = sb_bwd(proj, dosb, gtot, name=f"{tag}_dsb")
    dproj = rope_join(dqs, dks, dvs, sb_grads, dgate, tables, name=f"{tag}_drope")
    g_in = matmul([(dproj, h)], "tn", tm=1280, tn=1024, tk=2048, out_dtype=BF16, name=f"{tag}_gin")
    return {"w_in": g_in, "w_proj_dil": g_pd, "w_proj_sb": g_ps, "w_out": g_out}, dproj


def mixer_bwd_input(dx, dproj, saved, gain, w, tag):
    x = saved[0]
    return matmul_rms_bwd([(dproj, w["w_in"])], x, gain, dx, tm=256, name=f"{tag}_dh")


def kernel(x, norm_ffn1, ffn1_w_gate, ffn1_w_up, ffn1_w_down, norm_mix, w_in, w_proj_dil, w_proj_sb, w_out, norm_ffn2, ffn2_w_gate, ffn2_w_up, ffn2_w_down, norm_final, loss_target, m_norm_ffn1, m_ffn1_w_gate, m_ffn1_w_up, m_ffn1_w_down, m_norm_mix, m_w_in, m_w_proj_dil, m_w_proj_sb, m_w_out, m_norm_ffn2, m_ffn2_w_gate, m_ffn2_w_up, m_ffn2_w_down, m_norm_final, v_norm_ffn1, v_ffn1_w_gate, v_ffn1_w_up, v_ffn1_w_down, v_norm_mix, v_w_in, v_w_proj_dil, v_w_proj_sb, v_w_out, v_norm_ffn2, v_ffn2_w_gate, v_ffn2_w_up, v_ffn2_w_down, v_norm_final):
    args = dict(locals())
    t = x.shape[1]
    xs = x.reshape(t, D_MODEL)
    target = loss_target.reshape(t, D_MODEL)
    tables = rope_tables(t)

    parts = [(l, p) for l in range(2) for p in SUBBLOCKS]
    gains = {n: args[n] for n in NORM_ROWS}

    shipments = []
    for l, p in parts:
        if (l, p) == parts[0]:
            shipments += [(l, p, SUBBLOCKS[p][:2], CHIP_PEERS), (l, p, SUBBLOCKS[p][2:], ALL_PEERS)]
        else:
            shipments.append((l, p, SUBBLOCKS[p], ALL_PEERS))
    in_flight, order_token = [], jnp.zeros((1, 1), F32)
    for l, p, tensors, masks in shipments:
        shards = [travelling(args[n][l], by_cols).astype(BF16) for n, _, by_cols, _ in tensors]
        shards[0] = shards[0] + order_token.astype(BF16)
        in_flight.append(spread_start(shards, per_peer=False, masks=masks, name=f"gather_start_l{l}_{tensors[0][0]}"))
        order_token = in_flight[-1][-1][0:1, 0:1]

    def arrived(i, after):
        l, p, tensors, masks = shipments[i]
        tag = f"l{l}_{tensors[0][0]}"
        lands = spread_wait(in_flight[i], after, per_peer=False, masks=masks, name=f"gather_wait_{tag}")
        if masks is CHIP_PEERS:
            lands = relay_wait(relay_start(lands, name=f"gather_relay_{tag}"), name=f"gather_relayed_{tag}")
        return {n: land.reshape(-1, land.shape[-1]) for (n, _, _, _), land in zip(tensors, lands)}

    def weights_of(l, p, after):
        mine = [i for i, s in enumerate(shipments) if s[0:2] == (l, p)]
        w = arrived(mine[0], after)
        for i in mine[1:]:
            for n, _, _, _ in shipments[i][2]:
                w[n] = functools.partial(lambda after, i, n: arrived(i, after)[n], i=i, n=n)
        return w

    saved, weights = {}, {}
    act = xs
    h = rms_fwd(xs, _row(gains["norm_ffn1"][0]) + order_token, name="l0_ffn1_norm")
    for i, (l, p) in enumerate(parts):
        weights[(l, p)] = weights_of(l, p, act)
        nl, np_ = parts[i + 1] if i + 1 < len(parts) else (None, None)
        next_gain = _row(gains["norm_" + np_][nl]) if np_ else None
        if p == "mix":
            act, h, saved[(l, p)] = mixer_fwd(act, h, weights[(l, p)], tables, f"l{l}_mix", next_gain)
        else:
            act, h, saved[(l, p)] = ffn_fwd(act, h, weights[(l, p)], p, f"l{l}_{p}", next_gain)
    dx, dxb, g_final, loss_part = final_loss(act, _row(norm_final), target, name="loss_head")

    gain_grads, sent = {}, {}
    order_token = jnp.zeros((1, 1), F32)
    for l, p in reversed(parts):
        w, sv = weights[(l, p)], saved[(l, p)]
        if p == "mix":
            gw, rest = mixer_bwd_weights(dxb, sv, w, tables, f"l{l}_mix")
        else:
            gw, rest = ffn_bwd_weights(dxb, sv, w, p, f"l{l}_{p}")
        slices = [gw[n].reshape(N_DEV, -1, gw[n].shape[-1]) for n, _, _, _ in SUBBLOCKS[p]]
        sent[(l, p)] = spread_start(slices, per_peer=True, name=f"reduce_start_l{l}_{p}")
        gain = _row(gains["norm_" + p][l]) + sent[(l, p)][-1][0:1, 0:1]
        if p == "mix":
            dx, dxb, gain_grads[("norm_mix", l)] = mixer_bwd_input(dx, rest, sv, gain, w, f"l{l}_mix")
        else:
            dx, dxb, gain_grads[("norm_" + p, l)] = ffn_bwd_input(dx, rest, sv, gain, w, p, f"l{l}_{p}")

    partials, big_all = {}, [{}, {}, {}, {}]

    def receive(l, p, after):
        lands = spread_wait(sent[(l, p)], after, per_peer=True, name=f"reduce_wait_l{l}_{p}")
        for (n, _, _, _), land in zip(SUBBLOCKS[p], lands):
            partials.setdefault(n, [None, None])[l] = land

    def update(p):
        for n, _, by_cols, _ in SUBBLOCKS[p]:
            outs = sum_adamw(partials[n], travelling(args[n], by_cols), travelling(args["m_" + n], by_cols),
                             travelling(args["v_" + n], by_cols), tr=256, name=f"update_{n}")
            for kind, arr in enumerate(outs):
                big_all[kind][n] = travelling(arr, by_cols)
        return outs[1]

    for l, p in reversed(parts[1:]):
        receive(l, p, dx)
    update("ffn2")
    done = update("mix")
    receive(*parts[0], done)
    done = update("ffn1")

    loss_row = jnp.pad(loss_part[:, :1], ((0, 0), (0, LANES - 1)))
    small = jnp.concatenate([gain_grads[(n, l)] for n in NORM_ROWS for l in range(2)] + [g_final, loss_row], axis=0)
    small_g = sum_partials(all_gather_rows(small, done, name="gather_gain_grads"), tr=8, name="sum_gain_grads")
    zero_row = jnp.zeros((1, LANES), F32)
    small_of = lambda pre: jnp.concatenate([args[pre + n] for n in NORM_ROWS] + [_row(args[pre + "norm_final"]), zero_row], axis=0)[None]
    small_out = adamw(small_g[None], small_of(""), small_of("m_"), small_of("v_"), tr=8, name="update_gains")
    small_all = [small_g] + [o[0] for o in small_out]

    def gains_of(s):
        out = {n: s[2 * i:2 * i + 2] for i, n in enumerate(NORM_ROWS)}
        out["norm_final"] = s[6]
        return out

    order = ["norm_ffn1", "ffn1_w_gate", "ffn1_w_up", "ffn1_w_down", "norm_mix", "w_in", "w_proj_dil", "w_proj_sb", "w_out",
             "norm_ffn2", "ffn2_w_gate", "ffn2_w_up", "ffn2_w_down", "norm_final"]
    results = []
    for kind in range(4):
        both = {**big_all[kind], **gains_of(small_all[kind])}
        results += [both[n] for n in order]
    loss = small_g[7, 0]
    return (loss, dx.reshape(1, t, D_MODEL), *results)
```

```python
import functools

import jax
import jax.numpy as jnp
from jax import lax
from jax.experimental import pallas as pl
from jax.experimental.pallas import tpu as pltpu

F32 = jnp.float32
BF16 = jnp.bfloat16

D_MODEL = 1024
HEAD_DIM = 64
GROUP_W = 256
D_IN = 5120
N_DIL_GROUPS = 3
DIL_SPAN = 128
DILATIONS = (1, 4, 16)
ROPE_THETA = 500000.0
ROPE_DIM = 16
RMS_EPS = 1e-6
ATT_SCALE = HEAD_DIM ** -0.5
QS_BLK, KS_BLK, VS_BLK = 9, 10, 11
GATE_DIL_BLK, GATE_SB_BLK = 3, 4

ADAM_LR, ADAM_B1, ADAM_B2, ADAM_EPS, ADAM_WD, ADAM_STEP = 0.001, 0.9, 0.999, 1e-08, 0.01, 10

N_DEV = 8
LANES = 1024
VMEM_PHYSICAL_V7X = 64 << 20
VMEM_TEMP_HEADROOM = 20 << 20

PACK_LAYOUT = (
    ("ffn1_w_gate", 352, True, (1024, 2816)),
    ("ffn1_w_up", 352, True, (1024, 2816)),
    ("ffn1_w_down", 352, False, (2816, 1024)),
    ("w_in", 640, True, (1024, 5120)),
    ("w_proj_dil", 32, True, (256, 1024)),
    ("w_proj_sb", 32, True, (256, 1024)),
    ("w_out", 128, False, (1024, 1024)),
    ("ffn2_w_gate", 352, True, (1024, 2816)),
    ("ffn2_w_up", 352, True, (1024, 2816)),
    ("ffn2_w_down", 352, False, (2816, 1024)),
)
SUBBLOCKS = {"ffn1": PACK_LAYOUT[0:3], "mix": PACK_LAYOUT[3:7], "ffn2": PACK_LAYOUT[7:10]}
NORM_ROWS = ("norm_ffn1", "norm_mix", "norm_ffn2")


def _nbytes(shape, dtype):
    n = 1
    for s in shape:
        n *= s
    return n * jnp.dtype(dtype).itemsize


def _pcall(body, *, name, grid, in_specs, out_specs, out_shape, blocks, scratch_shapes=(), scratch_bytes=0):
    need = 2 * sum(_nbytes(s, d) for s, d in blocks) + scratch_bytes + VMEM_TEMP_HEADROOM
    limit = min(need, VMEM_PHYSICAL_V7X - (4 << 20))
    in_hbm = lambda s: pltpu.HBM(s.shape, s.dtype)
    out_shape = [in_hbm(s) for s in out_shape] if isinstance(out_shape, (list, tuple)) else in_hbm(out_shape)
    call = pl.pallas_call(
        body, name=name, grid=grid, in_specs=in_specs, out_specs=out_specs, out_shape=out_shape,
        scratch_shapes=scratch_shapes,
        compiler_params=pltpu.CompilerParams(vmem_limit_bytes=limit),
    )
    return lambda *args: call(*[pltpu.with_memory_space_constraint(a, pltpu.HBM) for a in args])


def _dot(a, b, form):
    dn = {"nn": (((1,), (0,)), ((), ())), "nt": (((1,), (1,)), ((), ())), "tn": (((0,), (0,)), ((), ()))}[form]
    return lax.dot_general(a.astype(BF16), b.astype(BF16), dn, preferred_element_type=F32)


def _sigmoid(x):
    return 1.0 / (1.0 + jnp.exp(-x))


def matmul(pairs, form, *, tm, tn, tk, out_dtype, name, scale=1.0, res=None):
    a0, b0 = pairs[0]
    if form == "tn":
        kdim, m = a0.shape
        n = b0.shape[1]
    else:
        m, kdim = a0.shape
        n = b0.shape[1] if form == "nn" else b0.shape[0]
    tm, tn, tk = min(tm, m), min(tn, n), min(tk, kdim)
    assert m % tm == 0 and n % tn == 0 and kdim % tk == 0, (name, m, n, kdim, tm, tn, tk)
    nk = kdim // tk
    npairs = len(pairs)

    if form == "tn":
        a_blk, a_map = (tk, tm), (lambda j, i, k: (k, i))
    else:
        a_blk, a_map = (tm, tk), (lambda j, i, k: (i, k))
    if form == "nt":
        b_blk, b_map = (tn, tk), (lambda j, i, k: (j, k))
    else:
        b_blk, b_map = (tk, tn), (lambda j, i, k: (k, j))
    o_map = lambda j, i, k: (i, j)

    def body(*refs):
        ab = refs[:2 * npairs]
        rest = refs[2 * npairs:]
        if res is not None:
            r_ref, o_ref = rest[0], rest[1]
            rest = rest[2:]
        else:
            r_ref, o_ref = None, rest[0]
            rest = rest[1:]

        def partial_sum():
            p = _dot(ab[0][...], ab[1][...], form)
            for q in range(1, npairs):
                p = p + _dot(ab[2 * q][...], ab[2 * q + 1][...], form)
            return p

        def finish(acc):
            out = acc * scale if scale != 1.0 else acc
            if r_ref is not None:
                out = r_ref[...] + out
            o_ref[...] = out.astype(out_dtype)

        if nk == 1:
            finish(partial_sum())
        else:
            acc_ref = rest[0]
            k = pl.program_id(2)

            @pl.when(k == 0)
            def _():
                acc_ref[...] = partial_sum()

            @pl.when(k > 0)
            def _():
                acc_ref[...] += partial_sum()

            @pl.when(k == nk - 1)
            def _():
                finish(acc_ref[...])

    in_specs, args, blocks = [], [], []
    for a, b in pairs:
        in_specs += [pl.BlockSpec(a_blk, a_map), pl.BlockSpec(b_blk, b_map)]
        args += [a, b]
        blocks += [(a_blk, a.dtype), (b_blk, b.dtype)]
    if res is not None:
        in_specs.append(pl.BlockSpec((tm, tn), o_map))
        args.append(res)
        blocks.append(((tm, tn), res.dtype))
    blocks.append(((tm, tn), out_dtype))
    scratch = [pltpu.VMEM((tm, tn), F32)] if nk > 1 else []
    return _pcall(
        body, name=name, grid=(n // tn, m // tm, nk), in_specs=in_specs,
        out_specs=pl.BlockSpec((tm, tn), o_map), out_shape=jax.ShapeDtypeStruct((m, n), out_dtype),
        blocks=blocks, scratch_shapes=scratch, scratch_bytes=(tm * tn * 4 if nk > 1 else 0),
    )(*args)


def swiglu_fwd(h, wg_t, wu_t, *, name, tm=512, tn=1408):
    t, d = h.shape
    f = wg_t.shape[0]
    tm, tn = min(tm, t), min(tn, f)

    def body(h_ref, wg_ref, wu_ref, ga_ref, gb_ref, s_ref):
        hh = h_ref[...]
        a = _dot(hh, wg_ref[...], "nt")
        b = _dot(hh, wu_ref[...], "nt")
        sg = _sigmoid(a)
        silu = a * sg
        ga_ref[...] = (b * (sg * (1.0 + a * (1.0 - sg)))).astype(BF16)
        gb_ref[...] = silu.astype(BF16)
        s_ref[...] = (silu * b).astype(BF16)

    w_spec = pl.BlockSpec((tn, d), lambda j, i: (j, 0))
    o_spec = pl.BlockSpec((tm, tn), lambda j, i: (i, j))
    o_shape = jax.ShapeDtypeStruct((t, f), BF16)
    return _pcall(
        body, name=name, grid=(f // tn, t // tm),
        in_specs=[pl.BlockSpec((tm, d), lambda j, i: (i, 0)), w_spec, w_spec],
        out_specs=[o_spec, o_spec, o_spec], out_shape=[o_shape, o_shape, o_shape],
        blocks=[((tm, d), BF16), ((tn, d), BF16), ((tn, d), BF16)] + [((tm, tn), BF16)] * 3,
    )(h, wg_t, wu_t)


def swiglu_bwd(dyb, wd, ga, gb, *, name, scale, tm=512, tn=1408):
    t, d = dyb.shape
    f = wd.shape[0]
    tm, tn = min(tm, t), min(tn, f)

    def body(dy_ref, wd_ref, ga_ref, gb_ref, da_ref, db_ref):
        ds = _dot(dy_ref[...], wd_ref[...], "nt") * scale
        da_ref[...] = (ds * ga_ref[...].astype(F32)).astype(BF16)
        db_ref[...] = (ds * gb_ref[...].astype(F32)).astype(BF16)

    o_spec = pl.BlockSpec((tm, tn), lambda j, i: (i, j))
    o_shape = jax.ShapeDtypeStruct((t, f), BF16)
    return _pcall(
        body, name=name, grid=(f // tn, t // tm),
        in_specs=[pl.BlockSpec((tm, d), lambda j, i: (i, 0)), pl.BlockSpec((tn, d), lambda j, i: (j, 0)), o_spec, o_spec],
        out_specs=[o_spec, o_spec], out_shape=[o_shape, o_shape],
        blocks=[((tm, d), BF16), ((tn, d), BF16)] + [((tm, tn), BF16)] * 4,
    )(dyb, wd, ga, gb)


def gate_fwd(odil, osb, wpd_t, wps_t, proj, *, name, tm=512):
    t = odil.shape[0]
    tm = min(tm, t)

    def body(od_ref, os_ref, wpd_ref, wps_ref, g1_ref, g2_ref, y_ref, u1_ref, u2_ref):
        u1 = _dot(od_ref[...], wpd_ref[...], "nt")
        u2 = _dot(os_ref[...], wps_ref[...], "nt")
        y = _sigmoid(g1_ref[...].astype(F32)) * u1 + _sigmoid(g2_ref[...].astype(F32)) * u2
        y_ref[...] = y.astype(BF16)
        u1_ref[...] = u1.astype(BF16)
        u2_ref[...] = u2.astype(BF16)

    o_spec = pl.BlockSpec((tm, D_MODEL), lambda i: (i, 0))
    w_spec = pl.BlockSpec((D_MODEL, GROUP_W), lambda i: (0, 0))
    a_spec = pl.BlockSpec((tm, GROUP_W), lambda i: (i, 0))
    o_shape = jax.ShapeDtypeStruct((t, D_MODEL), BF16)
    return _pcall(
        body, name=name, grid=(t // tm,),
        in_specs=[a_spec, a_spec, w_spec, w_spec,
                  pl.BlockSpec((tm, D_MODEL), lambda i: (i, GATE_DIL_BLK)),
                  pl.BlockSpec((tm, D_MODEL), lambda i: (i, GATE_SB_BLK))],
        out_specs=[o_spec, o_spec, o_spec], out_shape=[o_shape, o_shape, o_shape],
        blocks=[((tm, GROUP_W), F32)] * 2 + [((D_MODEL, GROUP_W), BF16)] * 2 + [((tm, D_MODEL), BF16)] * 5,
    )(odil, osb, wpd_t, wps_t, proj, proj)


def gate_bwd(dxb, wout, u1, u2, proj, *, name, tm=512):
    t = dxb.shape[0]
    tm = min(tm, t)

    def body(dx_ref, w_ref, u1_ref, u2_ref, g1_ref, g2_ref, du1_ref, du2_ref, dg_ref):
        dy = _dot(dx_ref[...], w_ref[...], "nt")
        s1 = _sigmoid(g1_ref[...].astype(F32))
        s2 = _sigmoid(g2_ref[...].astype(F32))
        du1_ref[...] = (dy * s1).astype(BF16)
        du2_ref[...] = (dy * s2).astype(BF16)
        dg_ref[:, :D_MODEL] = (dy * u1_ref[...].astype(F32) * s1 * (1.0 - s1)).astype(BF16)
        dg_ref[:, D_MODEL:] = (dy * u2_ref[...].astype(F32) * s2 * (1.0 - s2)).astype(BF16)

    o_spec = pl.BlockSpec((tm, D_MODEL), lambda i: (i, 0))
    o_shape = jax.ShapeDtypeStruct((t, D_MODEL), BF16)
    return _pcall(
        body, name=name, grid=(t // tm,),
        in_specs=[o_spec, pl.BlockSpec((D_MODEL, D_MODEL), lambda i: (0, 0)), o_spec, o_spec,
                  pl.BlockSpec((tm, D_MODEL), lambda i: (i, GATE_DIL_BLK)),
                  pl.BlockSpec((tm, D_MODEL), lambda i: (i, GATE_SB_BLK))],
        out_specs=[o_spec, o_spec, pl.BlockSpec((tm, 2 * D_MODEL), lambda i: (i, 0))],
        out_shape=[o_shape, o_shape, jax.ShapeDtypeStruct((t, 2 * D_MODEL), BF16)],
        blocks=[((tm, D_MODEL), BF16)] * 9 + [((D_MODEL, D_MODEL), BF16)],
    )(dxb, wout, u1, u2, proj, proj)


def rms_fwd(x, gain, *, name, tm=512):
    t, d = x.shape
    tm = min(tm, t)

    def body(x_ref, g_ref, h_ref):
        xv = x_ref[...]
        rstd = lax.rsqrt(jnp.mean(xv * xv, axis=1, keepdims=True) + RMS_EPS)
        h_ref[...] = (xv * rstd * g_ref[...]).astype(BF16)

    return _pcall(
        body, name=name, grid=(t // tm,),
        in_specs=[pl.BlockSpec((tm, d), lambda i: (i, 0)), pl.BlockSpec((1, d), lambda i: (0, 0))],
        out_specs=pl.BlockSpec((tm, d), lambda i: (i, 0)), out_shape=jax.ShapeDtypeStruct((t, d), BF16),
        blocks=[((tm, d), F32), ((tm, d), BF16)],
    )(x, gain)


def matmul_res_norm(a, b, res, next_gain, *, scale, tm, name):
    t, k = a.shape
    d = b.shape[1]
    tm = min(tm, t)
    with_norm = next_gain is not None

    def body(a_ref, b_ref, r_ref, *rest):
        out = r_ref[...] + _dot(a_ref[...], b_ref[...], "nn") * scale
        if with_norm:
            g_ref, o_ref, h_ref = rest
            rstd = lax.rsqrt(jnp.mean(out * out, axis=1, keepdims=True) + RMS_EPS)
            h_ref[...] = (out * rstd * g_ref[...]).astype(BF16)
        else:
            o_ref, = rest
        o_ref[...] = out

    row = pl.BlockSpec((tm, d), lambda i: (i, 0))
    in_specs = [pl.BlockSpec((tm, k), lambda i: (i, 0)), pl.BlockSpec((k, d), lambda i: (0, 0)), row]
    args = [a, b, res]
    out_specs, out_shape = [row], [jax.ShapeDtypeStruct((t, d), F32)]
    if with_norm:
        in_specs.append(pl.BlockSpec((1, d), lambda i: (0, 0)))
        args.append(next_gain)
        out_specs.append(row)
        out_shape.append(jax.ShapeDtypeStruct((t, d), BF16))
    outs = _pcall(
        body, name=name, grid=(t // tm,), in_specs=in_specs, out_specs=out_specs, out_shape=out_shape,
        blocks=[((tm, k), a.dtype), ((k, d), b.dtype), ((tm, d), F32), ((tm, d), F32), ((tm, d), BF16)],
    )(*args)
    return (outs[0], outs[1]) if with_norm else (outs[0], None)


def _rms_bwd_rows(dhv, xv, g, drv):
    rstd = lax.rsqrt(jnp.mean(xv * xv, axis=1, keepdims=True) + RMS_EPS)
    xh = xv * rstd
    dxh = dhv * g
    dx = drv + rstd * (dxh - xh * jnp.mean(dxh * xh, axis=1, keepdims=True))
    return dx, jnp.sum(dhv * xh, axis=0, keepdims=True)


def matmul_rms_bwd(pairs, x, gain, dres, *, tm, name):
    t, d = x.shape
    tm = min(tm, t)
    npairs = len(pairs)

    def body(*refs):
        ab = refs[:2 * npairs]
        x_ref, g_ref, dr_ref, dx_ref, dxb_ref, dg_ref = refs[2 * npairs:]
        dh = _dot(ab[0][...], ab[1][...], "nn")
        for q in range(1, npairs):
            dh = dh + _dot(ab[2 * q][...], ab[2 * q + 1][...], "nn")
        dx, part = _rms_bwd_rows(dh, x_ref[...], g_ref[...], dr_ref[...])
        dx_ref[...] = dx
        dxb_ref[...] = dx.astype(BF16)

        @pl.when(pl.program_id(0) == 0)
        def _():
            dg_ref[...] = part

        @pl.when(pl.program_id(0) > 0)
        def _():
            dg_ref[...] += part

    in_specs, args, blocks = [], [], []
    for a, b in pairs:
        k = a.shape[1]
        in_specs += [pl.BlockSpec((tm, k), lambda i: (i, 0)), pl.BlockSpec((k, d), lambda i: (0, 0))]
        args += [a, b]
        blocks += [((tm, k), a.dtype), ((k, d), b.dtype)]
    row = pl.BlockSpec((tm, d), lambda i: (i, 0))
    vec = pl.BlockSpec((1, d), lambda i: (0, 0))
    return _pcall(
        body, name=name, grid=(t // tm,), in_specs=in_specs + [row, vec, row], out_specs=[row, row, vec],
        out_shape=[jax.ShapeDtypeStruct((t, d), F32), jax.ShapeDtypeStruct((t, d), BF16), jax.ShapeDtypeStruct((1, d), F32)],
        blocks=blocks + [((tm, d), F32)] * 3 + [((tm, d), BF16)],
    )(*args, x, gain, dres)


def final_loss(x, gain, target, *, name, tm=512):
    t, d = x.shape
    tm = min(tm, t)

    def body(x_ref, g_ref, t_ref, dx_ref, dxb_ref, dg_ref, loss_ref):
        xv = x_ref[...]
        g = g_ref[...]
        rstd = lax.rsqrt(jnp.mean(xv * xv, axis=1, keepdims=True) + RMS_EPS)
        xh = xv * rstd
        err = xh * g - t_ref[...]
        dy = err * (1.0 / d)
        dxh = dy * g
        dx = rstd * (dxh - xh * jnp.mean(dxh * xh, axis=1, keepdims=True))
        dx_ref[...] = dx
        dxb_ref[...] = dx.astype(BF16)
        part = jnp.sum(dy * xh, axis=0, keepdims=True)
        sq = jnp.sum(jnp.sum(err * err, axis=1, keepdims=True), axis=0, keepdims=True) * (0.5 / d)
        lpart = jnp.broadcast_to(sq, (1, 128))

        @pl.when(pl.program_id(0) == 0)
        def _():
            dg_ref[...] = part
            loss_ref[...] = lpart

        @pl.when(pl.program_id(0) > 0)
        def _():
            dg_ref[...] += part
            loss_ref[...] += lpart

    row = pl.BlockSpec((tm, d), lambda i: (i, 0))
    vec = pl.BlockSpec((1, d), lambda i: (0, 0))
    return _pcall(
        body, name=name, grid=(t // tm,), in_specs=[row, vec, row],
        out_specs=[row, row, vec, pl.BlockSpec((1, 128), lambda i: (0, 0))],
        out_shape=[jax.ShapeDtypeStruct((t, d), F32), jax.ShapeDtypeStruct((t, d), BF16),
                   jax.ShapeDtypeStruct((1, d), F32), jax.ShapeDtypeStruct((1, 128), F32)],
        blocks=[((tm, d), F32)] * 3 + [((tm, d), BF16)],
    )(x, gain, target)


def rope_tables(t):
    pos = jnp.arange(t, dtype=F32)
    inv_freq = ROPE_THETA ** (-jnp.arange(0, ROPE_DIM, 2, dtype=F32) / ROPE_DIM)
    ang = pos[:, None] * inv_freq[None, :]
    cos, sin = jnp.cos(ang), jnp.sin(ang)
    half = ROPE_DIM // 2
    pad = HEAD_DIM - ROPE_DIM
    zeros, ones, rest = jnp.zeros((t, half), F32), jnp.ones((t, pad), F32), jnp.zeros((t, pad), F32)
    c = [cos, cos, ones] * 2
    sa = [-sin, zeros, rest] * 2
    sb = [zeros, sin, rest] * 2
    return jnp.concatenate(c + sa + sb, axis=1)


def _rotate(xv, cv, sav, sbv):
    halves = []
    for half in range(2):
        x = xv[:, 128 * half:128 * (half + 1)]
        halves.append(x * cv + pltpu.roll(x, 120, 1) * sav + pltpu.roll(x, 8, 1) * sbv)
    return jnp.concatenate(halves, axis=1)


STAGE_CHUNKS = 4


def _stage(tm):
    return dict(scratch_shapes=[pltpu.VMEM((STAGE_CHUNKS, tm, 128), F32)], scratch_bytes=STAGE_CHUNKS * tm * 128 * 4)


def _split_residues(stage_ref, val, out_ref, d, col, dtype):
    rows, width = val.shape
    if d == 1:
        out_ref[0, :, col:col + width] = val.astype(dtype)
        return
    chunks = width // 128
    for c in range(chunks):
        stage_ref[c] = val[:, 128 * c:128 * (c + 1)]
    for r in range(d):
        for c in range(chunks):
            out_ref[r, :, col + 128 * c:col + 128 * (c + 1)] = stage_ref[c, pl.ds(r, rows // d, stride=d), :].astype(dtype)


def _join_residues(stage_ref, in_ref, d, col=0, width=GROUP_W):
    if d == 1:
        return in_ref[0, :, col:col + width].astype(F32)
    rows = in_ref.shape[1] * d
    chunks = width // 128
    for r in range(d):
        for c in range(chunks):
            stage_ref[c, pl.ds(r, rows // d, stride=d), :] = in_ref[r, :, col + 128 * c:col + 128 * (c + 1)].astype(F32)
    return jnp.concatenate([stage_ref[c] for c in range(chunks)], axis=1)


def rope_split(proj, tables, *, name, tm=512):
    c = sa = sb = tables
    t = tables.shape[0]
    tm = min(tm, t)

    def body(*refs):
        pieces = refs[0:9]
        c_ref, sa_ref, sb_ref = refs[9:12]
        qk_out, v_out = refs[12:15], refs[15:18]
        stage = refs[18]
        cv, sav, sbv = c_ref[...], sa_ref[...], sb_ref[...]
        for g, d in enumerate(DILATIONS):
            for kind in range(3):
                xv = pieces[3 * kind + g][...].astype(F32)
                if kind < 2:
                    _split_residues(stage, _rotate(xv, cv, sav, sbv), qk_out[g], d, GROUP_W * kind, BF16)
                else:
                    _split_residues(stage, xv, v_out[g], d, 0, BF16)

    tabs = [pl.BlockSpec((tm, 128), functools.partial(lambda i, cb: (i, cb), cb=cb)) for cb in range(3)]
    in_specs = [pl.BlockSpec((tm, GROUP_W), functools.partial(lambda i, cb: (i, cb), cb=cb)) for cb in range(9)]
    out_specs = ([pl.BlockSpec((d, tm // d, 2 * GROUP_W), lambda i: (0, i, 0)) for d in DILATIONS]
                 + [pl.BlockSpec((d, tm // d, GROUP_W), lambda i: (0, i, 0)) for d in DILATIONS])
    out_shape = ([jax.ShapeDtypeStruct((d, t // d, 2 * GROUP_W), BF16) for d in DILATIONS]
                 + [jax.ShapeDtypeStruct((d, t // d, GROUP_W), BF16) for d in DILATIONS])
    outs = _pcall(
        body, name=name, grid=(t // tm,), in_specs=in_specs + tabs, out_specs=out_specs, out_shape=out_shape,
        blocks=[((tm, GROUP_W), BF16)] * 18 + [((tm, 128), F32)] * 3,
        **_stage(tm),
    )(*([proj] * 9), c, sa, sb)
    return outs[0:3], outs[3:6]


def rope_join(dqs, dks, dvs, sb_grads, dgate, tables, *, name, tm=512):
    c = sa = sb = tables
    t = tables.shape[0]
    tm = min(tm, t)

    def body(*refs):
        pieces, sb_refs, dgate_ref = refs[0:9], refs[9:12], refs[12]
        c_ref, sa_ref, sb_ref = refs[13:16]
        o_ref, stage = refs[16], refs[17]
        cv, sav, sbv = c_ref[...], -sa_ref[...], -sb_ref[...]
        for kind in range(3):
            for g, d in enumerate(DILATIONS):
                xv = _join_residues(stage, pieces[3 * kind + g], d)
                if kind < 2:
                    xv = _rotate(xv, cv, sav, sbv)
                col = GROUP_W * (3 * kind + g)
                o_ref[:, col:col + GROUP_W] = xv.astype(BF16)
        for j in range(3):
            o_ref[:, GROUP_W * (QS_BLK + j):GROUP_W * (QS_BLK + j + 1)] = sb_refs[j][...].astype(BF16)
        o_ref[:, D_MODEL * GATE_DIL_BLK:] = dgate_ref[...]

    tabs = [pl.BlockSpec((tm, 128), functools.partial(lambda i, cb: (i, cb), cb=cb)) for cb in range(3)]
    nat = lambda w: pl.BlockSpec((tm, w), lambda i: (i, 0))
    in_specs = [pl.BlockSpec((d, tm // d, GROUP_W), lambda i: (0, i, 0)) for _ in range(3) for d in DILATIONS]
    in_specs += [nat(GROUP_W)] * 3 + [nat(2 * D_MODEL)]
    return _pcall(
        body, name=name, grid=(t // tm,), in_specs=in_specs + tabs,
        out_specs=nat(D_IN), out_shape=jax.ShapeDtypeStruct((t, D_IN), BF16),
        blocks=[((tm, GROUP_W), F32)] * 12 + [((tm, 128), F32)] * 3 + [((tm, 2 * D_MODEL), BF16), ((tm, D_IN), BF16)],
        **_stage(tm),
    )(*dqs, *dks, *dvs, *sb_grads, dgate, c, sa, sb)


def _head_mask(h):
    lane = lax.broadcasted_iota(jnp.int32, (1, GROUP_W), 1)
    return (lane // HEAD_DIM) == h


def _band_masks(heads):
    ri = lax.broadcasted_iota(jnp.int32, (heads * DIL_SPAN, DIL_SPAN), 0) % DIL_SPAN
    ci = lax.broadcasted_iota(jnp.int32, (heads * DIL_SPAN, DIL_SPAN), 1)
    return ci <= ri, ci >= ri


def dil_fwd(qk, v, *, name):
    d, nsub, _ = qk.shape
    nblk = nsub // DIL_SPAN

    def body(q_ref, kc_ref, kp_ref, vc_ref, vp_ref, o_ref, lse_ref):
        nb = pl.program_id(1)
        kk = jnp.concatenate([kp_ref[0], kc_ref[0]], axis=0)
        vv = jnp.concatenate([vp_ref[0], vc_ref[0]], axis=0)
        s = _dot(_stack_heads(q_ref[0] * ATT_SCALE), kk, "nt")
        ri = lax.broadcasted_iota(jnp.int32, s.shape, 0) % DIL_SPAN
        ci = lax.broadcasted_iota(jnp.int32, s.shape, 1)
        valid = ((ci < DIL_SPAN) & (ci >= ri) & (nb > 0)) | ((ci >= DIL_SPAN) & (ci - DIL_SPAN <= ri))
        s = jnp.where(valid, s, -jnp.inf)
        m = jnp.max(s, axis=1, keepdims=True)
        p = jnp.exp(s - m)
        den = jnp.sum(p, axis=1, keepdims=True)
        o_ref[0] = _unstack_heads(_dot(p, vv, "nn") / den, DIL_SPAN)
        lse = m + jnp.log(den)
        for h in range(4):
            lse_ref[0, :, 128 * h:128 * (h + 1)] = jnp.broadcast_to(lse[DIL_SPAN * h:DIL_SPAN * (h + 1)], (DIL_SPAN, 128))

    blk = (1, DIL_SPAN, GROUP_W)
    sblk = (1, DIL_SPAN, 512)
    prv = lambda nb: jnp.maximum(nb - 1, 0)
    return _pcall(
        body, name=name, grid=(d, nblk),
        in_specs=[pl.BlockSpec(blk, lambda r, nb: (r, nb, 0)),
                  pl.BlockSpec(blk, lambda r, nb: (r, nb, 1)),
                  pl.BlockSpec(blk, lambda r, nb: (r, prv(nb), 1)),
                  pl.BlockSpec(blk, lambda r, nb: (r, nb, 0)),
                  pl.BlockSpec(blk, lambda r, nb: (r, prv(nb), 0))],
        out_specs=[pl.BlockSpec(blk, lambda r, nb: (r, nb, 0)), pl.BlockSpec(sblk, lambda r, nb: (r, nb, 0))],
        out_shape=[jax.ShapeDtypeStruct((d, nsub, GROUP_W), F32), jax.ShapeDtypeStruct((d, nsub, 512), F32)],
        blocks=[(blk, BF16)] * 5 + [(blk, F32), (sblk, F32)],
    )(qk, qk, qk, v, v)


def dil_merge(outs, lses, *, name, tm=512):
    t = outs[0].shape[0] * outs[0].shape[1]
    tm = min(tm, t)

    def body(o0, o1, o2, l0, l1, l2, o_ref, lse_ref, stage):
        ls = [_join_residues(stage, l, d, 0, 512) for l, d in zip((l0, l1, l2), DILATIONS)]
        m = jnp.maximum(jnp.maximum(ls[0], ls[1]), ls[2])
        tot = m + jnp.log(jnp.exp(ls[0] - m) + jnp.exp(ls[1] - m) + jnp.exp(ls[2] - m))
        lse_ref[...] = tot
        lane = lax.broadcasted_iota(jnp.int32, (1, 128), 1)
        first = lane < HEAD_DIM
        acc = jnp.zeros((tm, GROUP_W), F32)
        for og, lg, d in zip((o0, o1, o2), ls, DILATIONS):
            w = jnp.exp(lg - tot)
            wide = jnp.concatenate([jnp.where(first, w[:, 0:128], w[:, 128:256]),
                                    jnp.where(first, w[:, 256:384], w[:, 384:512])], axis=1)
            acc = acc + wide * _join_residues(stage, og, d)
        o_ref[...] = acc

    o_in = [pl.BlockSpec((d, tm // d, GROUP_W), lambda i: (0, i, 0)) for d in DILATIONS]
    l_in = [pl.BlockSpec((d, tm // d, 512), lambda i: (0, i, 0)) for d in DILATIONS]
    return _pcall(
        body, name=name, grid=(t // tm,), in_specs=o_in + l_in,
        out_specs=[pl.BlockSpec((tm, GROUP_W), lambda i: (i, 0)), pl.BlockSpec((tm, 512), lambda i: (i, 0))],
        out_shape=[jax.ShapeDtypeStruct((t, GROUP_W), F32), jax.ShapeDtypeStruct((t, 512), F32)],
        blocks=[((tm, GROUP_W), F32)] * 4 + [((tm, 512), F32)] * 4,
        **_stage(tm),
    )(*outs, *lses)


def dil_bwd_prep(do, o, lse, *, name, tm=512):
    t = do.shape[0]
    tm = min(tm, t)
    wide = DILATIONS[1:]

    def body(do_ref, o_ref, lse_ref, ds_ref, *rest):
        do_out, lse_out, ds_out = rest[0:2], rest[2:4], rest[4:6]
        stage = rest[6]
        dov = do_ref[...]
        prod = dov * o_ref[...]
        for h in range(4):
            s = jnp.sum(jnp.where(_head_mask(h), prod, 0.0), axis=1, keepdims=True)
            ds_ref[:, 128 * h:128 * (h + 1)] = jnp.broadcast_to(s, (tm, 128))
        for i, d in enumerate(wide):
            _split_residues(stage, dov, do_out[i], d, 0, BF16)
            _split_residues(stage, lse_ref[...], lse_out[i], d, 0, F32)
            _split_residues(stage, ds_ref[...], ds_out[i], d, 0, F32)

    nat = lambda w: pl.BlockSpec((tm, w), lambda i: (i, 0))
    res = lambda d, w: pl.BlockSpec((d, tm // d, w), lambda i: (0, i, 0))
    shape = lambda d, w, dt: jax.ShapeDtypeStruct((d, t // d, w), dt)
    outs = _pcall(
        body, name=name, grid=(t // tm,), in_specs=[nat(GROUP_W), nat(GROUP_W), nat(512)],
        out_specs=[nat(512)] + [res(d, GROUP_W) for d in wide] + [res(d, 512) for d in wide] * 2,
        out_shape=([jax.ShapeDtypeStruct((t, 512), F32)] + [shape(d, GROUP_W, BF16) for d in wide]
                   + [shape(d, 512, F32) for d in wide] * 2),
        blocks=[((tm, GROUP_W), F32)] * 3 + [((tm, 512), F32)] * 6,
        **_stage(tm),
    )(do, o, lse)
    return outs[0], outs[1:3], outs[3:5], outs[5:7]


def head_sums(a, b, *, name, round_a=False, tm=512):
    t = a.shape[0]
    tm = min(tm, t)

    def body(a_ref, b_ref, o_ref):
        av = a_ref[...]
        if round_a:
            av = av.astype(BF16).astype(F32)
        prod = av * b_ref[...]
        for h in range(4):
            s = jnp.sum(jnp.where(_head_mask(h), prod, 0.0), axis=1, keepdims=True)
            o_ref[:, 128 * h:128 * (h + 1)] = jnp.broadcast_to(s, (tm, 128))

    spec = pl.BlockSpec((tm, GROUP_W), lambda i: (i, 0))
    return _pcall(
        body, name=name, grid=(t // tm,), in_specs=[spec, spec],
        out_specs=pl.BlockSpec((tm, 512), lambda i: (i, 0)), out_shape=jax.ShapeDtypeStruct((t, 512), F32),
        blocks=[((tm, GROUP_W), F32)] * 2 + [((tm, 512), F32)],
    )(a, b)


def dil_bwd(qk, v, do, lse, dsum, *, name):
    d, nsub, _ = qk.shape
    nblk = nsub // DIL_SPAN

    def body(qa_ref, qb_ref, kc_ref, kp_ref, vc_ref, vp_ref, doa_ref, dob_ref, la_ref, lb_ref, sa_ref, sb_ref,
             dq_ref, dk_ref, dv_ref):
        nb = pl.program_id(1)
        own, band = _band_masks(4)
        prev = band & (nb > 0)
        nxt = band & (nb < nblk - 1)
        kc, kp, vc, vp = kc_ref[0], kp_ref[0], vc_ref[0], vp_ref[0]
        qas, qbs = _stack_heads(qa_ref[0] * ATT_SCALE), _stack_heads(qb_ref[0] * ATT_SCALE)
        das, dbs = _stack_heads(doa_ref[0].astype(BF16)), _stack_heads(dob_ref[0].astype(BF16))
        stat = lambda ref: jnp.concatenate([ref[0, :, 128 * h:128 * (h + 1)] for h in range(4)], axis=0)
        la, lb, sa, sb = stat(la_ref), stat(lb_ref), stat(sa_ref), stat(sb_ref)

        def probs(qs, ds_, k, v, mask, l, s):
            p = jnp.where(mask, jnp.exp(_dot(qs, k, "nt") - l), 0.0)
            dsc = p * (_dot(ds_, v, "nt") - s)
            return p.astype(BF16), dsc.astype(BF16)

        p_cc, ds_cc = probs(qas, das, kc, vc, own, la, sa)
        _, ds_cp = probs(qas, das, kp, vp, prev, la, sa)
        p_nc, ds_nc = probs(qbs, dbs, kc, vc, nxt, lb, sb)
        dq_ref[0] = _unstack_heads(_dot(ds_cc, kc, "nn") + _dot(ds_cp, kp, "nn"), DIL_SPAN) * ATT_SCALE
        dk_ref[0] = _dot(ds_cc, qas, "tn") + _dot(ds_nc, qbs, "tn")
        dv_ref[0] = _dot(p_cc, das, "tn") + _dot(p_nc, dbs, "tn")

    blk = (1, DIL_SPAN, GROUP_W)
    sblk = (1, DIL_SPAN, 512)
    prv = lambda nb: jnp.maximum(nb - 1, 0)
    nxt_ = lambda nb: jnp.minimum(nb + 1, nblk - 1)
    cur_at = lambda c: pl.BlockSpec(blk, functools.partial(lambda r, nb, c: (r, nb, c), c=c))
    prv_at = lambda c: pl.BlockSpec(blk, functools.partial(lambda r, nb, c: (r, prv(nb), c), c=c))
    nxt_at = lambda c: pl.BlockSpec(blk, functools.partial(lambda r, nb, c: (r, nxt_(nb), c), c=c))
    s_cur = pl.BlockSpec(sblk, lambda r, nb: (r, nb, 0))
    s_nxt = pl.BlockSpec(sblk, lambda r, nb: (r, nxt_(nb), 0))
    o_spec = pl.BlockSpec(blk, lambda r, nb: (r, nb, 0))
    o_shape = jax.ShapeDtypeStruct((d, nsub, GROUP_W), F32)
    return _pcall(
        body, name=name, grid=(d, nblk),
        in_specs=[cur_at(0), nxt_at(0), cur_at(1), prv_at(1), cur_at(0), prv_at(0), cur_at(0), nxt_at(0),
                  s_cur, s_nxt, s_cur, s_nxt],
        out_specs=[o_spec, o_spec, o_spec], out_shape=[o_shape, o_shape, o_shape],
        blocks=[(blk, BF16)] * 6 + [(blk, F32)] * 5 + [(sblk, F32)] * 4,
    )(qk, qk, qk, qk, v, v, do, do, lse, lse, dsum, dsum)


def _tri_dot(x, b):
    hi = x.astype(BF16)
    lo = (x - hi.astype(F32)).astype(BF16)
    return _dot(jnp.concatenate([hi, lo], axis=1), jnp.concatenate([b, b], axis=0), "nn")


SB_TILE = 256
SB_ROWS = 512


def _stack_heads(a):
    return jnp.concatenate([jnp.where(_head_mask(h), a, jnp.zeros_like(a)) for h in range(4)], axis=0)


def _unstack_heads(acc, rows):
    out = acc[0:rows]
    for h in range(1, 4):
        out = jnp.where(_head_mask(h), acc[h * rows:(h + 1) * rows], out)
    return out


def _tri_masks(n):
    ri = lax.broadcasted_iota(jnp.int32, (n, n), 0)
    ci = lax.broadcasted_iota(jnp.int32, (n, n), 1)
    return (ri > ci).astype(BF16), (ri >= ci).astype(BF16)


def _sb_weights(qs, kt, after, c_keep, lead):
    z = _dot(qs, kt, "nt")
    lbeta = jnp.minimum(z, 0.0) - jnp.log(1.0 + jnp.exp(-jnp.abs(z)))
    lkeep = lbeta - z
    past = None
    if lead is not None:
        query = lax.broadcasted_iota(jnp.int32, z.shape, 0) % SB_ROWS
        past = lax.broadcasted_iota(jnp.int32, z.shape, 1) + lead < query
        lkeep = jnp.where(past, lkeep, 0.0)
    w = jnp.exp(lbeta + _tri_dot(lkeep, after) + c_keep)
    if lead is not None:
        w = jnp.where(past, w, 0.0)
    return z, past, lbeta, lkeep, w


def _sb_walk(qb, tile, carry):
    per = SB_ROWS // SB_TILE
    for i in reversed(range(per)):
        carry = tile(pl.multiple_of(qb * SB_ROWS + i * SB_TILE, SB_TILE), i * SB_TILE, i == per - 1, carry)
    past_tiles = qb * per
    return lax.fori_loop(0, past_tiles,
                         lambda it, c: tile(pl.multiple_of((past_tiles - 1 - it) * SB_TILE, SB_TILE), None, False, c), carry)


def sb_fwd(proj, *, name):
    t = proj.shape[0]
    n, m = SB_TILE, SB_ROWS
    assert t % m == 0

    def body(q_ref, k_ref, v_ref, o_ref, acc_ref):
        qb = pl.program_id(0)
        qs = _stack_heads(q_ref[...] * ATT_SCALE)
        after, _ = _tri_masks(n)

        def tile(off, lead, first, c_keep):
            kt = k_ref[pl.ds(off, n), :]
            vt = v_ref[pl.ds(off, n), :]
            _, _, _, lkeep, w = _sb_weights(qs, kt, after, c_keep, lead)
            pv = _tri_dot(w, vt)
            if first:
                acc_ref[...] = pv
            else:
                acc_ref[...] += pv
            return c_keep + jnp.sum(lkeep, axis=1, keepdims=True)

        _sb_walk(qb, tile, jnp.zeros((4 * m, 1), F32))
        o_ref[...] = _unstack_heads(acc_ref[...], m)

    full = lambda cb: pl.BlockSpec((t, GROUP_W), functools.partial(lambda i, cb: (0, cb), cb=cb))
    return _pcall(
        body, name=name, grid=(t // m,),
        in_specs=[pl.BlockSpec((m, GROUP_W), lambda i: (i, QS_BLK)), full(KS_BLK), full(VS_BLK)],
        out_specs=pl.BlockSpec((m, GROUP_W), lambda i: (i, 0)), out_shape=jax.ShapeDtypeStruct((t, GROUP_W), F32),
        blocks=[((m, GROUP_W), BF16), ((t, GROUP_W), BF16), ((t, GROUP_W), BF16), ((m, GROUP_W), F32)],
        scratch_shapes=[pltpu.VMEM((4 * m, GROUP_W), F32)], scratch_bytes=4 * m * GROUP_W * 4,
    )(proj, proj, proj)


def sb_bwd(proj, do, gtot, *, name):
    t = proj.shape[0]
    n, m = SB_TILE, SB_ROWS
    assert t % m == 0

    def body(q_ref, k_ref, v_ref, do_ref, gt_ref, dq_ref, dk_ref, dv_ref, acc_ref):
        qb = pl.program_id(0)

        @pl.when(qb == 0)
        def _():
            dk_ref[...] = jnp.zeros_like(dk_ref)
            dv_ref[...] = jnp.zeros_like(dv_ref)

        qs = _stack_heads(q_ref[...] * ATT_SCALE)
        dos = _stack_heads(do_ref[...].astype(BF16))
        gt = jnp.concatenate([jnp.max(gt_ref[:, 128 * h:128 * (h + 1)], axis=1, keepdims=True) for h in range(4)], axis=0)
        after, from_on = _tri_masks(n)

        def tile(off, lead, first, carry):
            c_keep, c_g = carry
            kt = k_ref[pl.ds(off, n), :]
            vt = v_ref[pl.ds(off, n), :]
            z, past, lbeta, lkeep, w = _sb_weights(qs, kt, after, c_keep, lead)
            gw = w * _dot(dos, vt, "nt")
            big_g = gt - (_tri_dot(gw, from_on) + c_g)
            dz = gw * jnp.exp(lbeta - z) - big_g * jnp.exp(lbeta)
            if lead is not None:
                dz = jnp.where(past, dz, 0.0)
            dz = dz.astype(BF16)
            dk_ref[pl.ds(off, n), :] += _dot(dz, qs, "tn")
            dv_ref[pl.ds(off, n), :] += _dot(w, dos, "tn")
            dq = _dot(dz, kt, "nn")
            if first:
                acc_ref[...] = dq
            else:
                acc_ref[...] += dq
            return c_keep + jnp.sum(lkeep, axis=1, keepdims=True), c_g + jnp.sum(gw, axis=1, keepdims=True)

        zero_col = jnp.zeros((4 * m, 1), F32)
        _sb_walk(qb, tile, (zero_col, zero_col))
        dq_ref[...] = _unstack_heads(acc_ref[...], m) * ATT_SCALE

    full = lambda cb: pl.BlockSpec((t, GROUP_W), functools.partial(lambda i, cb: (0, cb), cb=cb))
    whole = pl.BlockSpec((t, GROUP_W), lambda i: (0, 0))
    rowblk = pl.BlockSpec((m, GROUP_W), lambda i: (i, 0))
    shape = jax.ShapeDtypeStruct((t, GROUP_W), F32)
    return _pcall(
        body, name=name, grid=(t // m,),
        in_specs=[pl.BlockSpec((m, GROUP_W), lambda i: (i, QS_BLK)), full(KS_BLK), full(VS_BLK), rowblk,
                  pl.BlockSpec((m, 512), lambda i: (i, 0))],
        out_specs=[rowblk, whole, whole], out_shape=[shape, shape, shape],
        blocks=[((m, GROUP_W), BF16), ((t, GROUP_W), BF16), ((t, GROUP_W), BF16), ((m, GROUP_W), F32),
                ((m, 512), F32), ((m, GROUP_W), F32), ((t, GROUP_W), F32), ((t, GROUP_W), F32)],
        scratch_shapes=[pltpu.VMEM((4 * m, GROUP_W), F32)], scratch_bytes=4 * m * GROUP_W * 4,
    )(proj, proj, proj, do, gtot)


def _mesh_place():
    return lax.axis_index("x"), lax.axis_index("y"), lax.axis_index("c")


def _flip(place, mask):
    x, y, c = place
    return ((1 - x) if mask & 4 else x, (1 - y) if mask & 2 else y, (1 - c) if mask & 1 else c)


def _dev_index(place):
    x, y, c = place
    return 4 * x + 2 * y + c


HBM_SPEC = pl.BlockSpec(memory_space=pltpu.HBM)


def all_gather_rows(shard, after, *, name):
    rows, lanes = shard.shape

    def body(x_ref, after_ref, out_ref, send_sems, recv_sems, local_sem):
        me = _mesh_place()
        x, y, c = me
        sibling = _flip(me, 1)
        chips = [_flip(me, 4), _flip(me, 2), _flip(me, 6)]

        def copy(k, block, to, src=None):
            dst = out_ref.at[_dev_index(block)]
            return pltpu.make_async_remote_copy(
                src_ref=dst if src is None else src, dst_ref=dst, send_sem=send_sems.at[k], recv_sem=recv_sems.at[k],
                device_id=to, device_id_type=pl.DeviceIdType.MESH)

        mine = pltpu.make_async_copy(x_ref, out_ref.at[_dev_index(me)], local_sem)
        mine.start()
        first = [copy(0, me, sibling, src=x_ref)] + [copy(1 + j, me, chip, src=x_ref) for j, chip in enumerate(chips)]
        for cp in first:
            cp.start()
        passed = [copy(4 + j, chip, sibling) for j, chip in enumerate(chips)]
        for j, chip in enumerate(chips):
            copy(1 + j, chip, me).wait_recv()
            passed[j].start()
        copy(0, sibling, me).wait_recv()
        for j, chip in enumerate(chips):
            copy(4 + j, _flip(chip, 1), me).wait_recv()
        for cp in first + passed:
            cp.wait_send()
        mine.wait()

    return pl.pallas_call(
        body, name=name, in_specs=[HBM_SPEC, pl.BlockSpec(memory_space=pl.ANY)], out_specs=HBM_SPEC,
        out_shape=jax.ShapeDtypeStruct((N_DEV, rows, lanes), shard.dtype),
        scratch_shapes=[pltpu.SemaphoreType.DMA((7,)), pltpu.SemaphoreType.DMA((7,)), pltpu.SemaphoreType.DMA],
    )(shard, after)


SEM_SPEC = pl.BlockSpec(memory_space=pltpu.SEMAPHORE)
DATAFLOW_EFFECT = pltpu.SideEffectType.DATAFLOW_SIDE_EFFECTING


ALL_PEERS = tuple(range(1, N_DEV))
CHIP_PEERS = (1, 4, 2, 6)
OTHER_CHIPS = (4, 2, 6)


def _spread_copies(src_refs, land_refs, send_sems, recv_sems, per_peer, masks, arriving):
    me = _mesh_place()
    my = _dev_index(me)
    remote, local = [], []
    for t, (src_ref, land_ref) in enumerate(zip(src_refs, land_refs)):
        for i, mask in enumerate(masks):
            peer = _flip(me, mask)
            data_of = my if arriving else _dev_index(peer)
            slot = _dev_index(peer) if arriving else my
            k = t * len(masks) + i
            remote.append(pltpu.make_async_remote_copy(
                src_ref=src_ref.at[data_of] if per_peer else src_ref, dst_ref=land_ref.at[slot],
                send_sem=send_sems.at[k], recv_sem=recv_sems.at[k],
                device_id=peer, device_id_type=pl.DeviceIdType.MESH))
        local.append(pltpu.make_async_copy(src_ref.at[my] if per_peer else src_ref, land_ref.at[my],
                                           send_sems.at[len(src_refs) * len(masks) + t]))
    return remote, local


def spread_start(srcs, *, per_peer, name, masks=ALL_PEERS):
    nt = len(srcs)
    zones = [pltpu.HBM((N_DEV,) + (s.shape[1:] if per_peer else s.shape), s.dtype) for s in srcs]

    def body(*refs):
        src_refs, (send_sems, recv_sems) = refs[:nt], refs[nt:nt + 2]
        land_refs, token = refs[2 * nt + 2:3 * nt + 2], refs[3 * nt + 2]
        remote, local = _spread_copies(src_refs, land_refs, send_sems, recv_sems, per_peer, masks, arriving=False)
        for cp in remote + local:
            cp.start()
        token[...] = jnp.zeros_like(token)

    return pl.pallas_call(
        body, name=name, in_specs=(HBM_SPEC,) * nt,
        out_shape=(pltpu.SemaphoreType.DMA((nt * len(masks) + nt,)), pltpu.SemaphoreType.DMA((nt * len(masks),)),
                   *[pltpu.HBM(s.shape, s.dtype) for s in srcs], *zones, jax.ShapeDtypeStruct((8, 128), F32)),
        out_specs=(SEM_SPEC, SEM_SPEC) + (HBM_SPEC,) * (2 * nt) + (pl.BlockSpec(memory_space=pltpu.VMEM),),
        input_output_aliases={t: 2 + t for t in range(nt)},
        compiler_params=pltpu.CompilerParams(has_side_effects=DATAFLOW_EFFECT),
    )(*[pltpu.with_memory_space_constraint(s, pltpu.HBM) for s in srcs])


def spread_wait(started, after, *, per_peer, name, masks=ALL_PEERS):
    nt = (len(started) - 3) // 2
    send_sems, recv_sems = started[0:2]
    srcs_thru, lands_thru = started[2:2 + nt], started[2 + nt:2 + 2 * nt]

    def body(*refs):
        src_refs, land_refs = refs[:nt], refs[nt:2 * nt]
        send_sems, recv_sems = refs[2 * nt:2 * nt + 2]
        remote, local = _spread_copies(src_refs, land_refs, send_sems, recv_sems, per_peer, masks, arriving=True)
        for cp in remote:
            cp.wait_send()
            cp.wait_recv()
        for cp in local:
            cp.wait()

    outs = pl.pallas_call(
        body, name=name, in_specs=(HBM_SPEC,) * (2 * nt) + (SEM_SPEC, SEM_SPEC, pl.BlockSpec(memory_space=pl.ANY)),
        out_shape=tuple(pltpu.HBM(a.shape, a.dtype) for a in (*srcs_thru, *lands_thru)),
        out_specs=(HBM_SPEC,) * (2 * nt), input_output_aliases={t: t for t in range(2 * nt)},
        compiler_params=pltpu.CompilerParams(has_side_effects=DATAFLOW_EFFECT),
    )(*srcs_thru, *lands_thru, send_sems, recv_sems, after)
    return list(outs[nt:])


def _relay_copies(land_refs, send_sems, recv_sems, arriving):
    me = _mesh_place()
    sibling = _flip(me, 1)
    out = []
    for t, land_ref in enumerate(land_refs):
        for i, mask in enumerate(OTHER_CHIPS):
            slot = _dev_index(_flip(sibling if arriving else me, mask))
            k = t * len(OTHER_CHIPS) + i
            out.append(pltpu.make_async_remote_copy(
                src_ref=land_ref.at[slot], dst_ref=land_ref.at[slot], send_sem=send_sems.at[k], recv_sem=recv_sems.at[k],
                device_id=sibling, device_id_type=pl.DeviceIdType.MESH))
    return out


def relay_start(lands, *, name):
    nt = len(lands)
    n_sem = nt * len(OTHER_CHIPS)

    def body(*refs):
        for cp in _relay_copies(refs[:nt], refs[nt], refs[nt + 1], arriving=False):
            cp.start()

    return pl.pallas_call(
        body, name=name, in_specs=(HBM_SPEC,) * nt,
        out_shape=(pltpu.SemaphoreType.DMA((n_sem,)), pltpu.SemaphoreType.DMA((n_sem,)),
                   *[pltpu.HBM(a.shape, a.dtype) for a in lands]),
        out_specs=(SEM_SPEC, SEM_SPEC) + (HBM_SPEC,) * nt, input_output_aliases={t: 2 + t for t in range(nt)},
        compiler_params=pltpu.CompilerParams(has_side_effects=DATAFLOW_EFFECT),
    )(*[pltpu.with_memory_space_constraint(a, pltpu.HBM) for a in lands])


def relay_wait(started, *, name):
    send_sems, recv_sems = started[0:2]
    lands_thru = started[2:]
    nt = len(lands_thru)

    def body(*refs):
        for cp in _relay_copies(refs[:nt], refs[nt], refs[nt + 1], arriving=True):
            cp.wait_send()
            cp.wait_recv()

    return list(pl.pallas_call(
        body, name=name, in_specs=(HBM_SPEC,) * nt + (SEM_SPEC, SEM_SPEC),
        out_shape=tuple(pltpu.HBM(a.shape, a.dtype) for a in lands_thru), out_specs=(HBM_SPEC,) * nt,
        input_output_aliases={t: t for t in range(nt)},
        compiler_params=pltpu.CompilerParams(has_side_effects=DATAFLOW_EFFECT),
    )(*lands_thru, send_sems, recv_sems))


def sum_partials(parts, *, name, tr):
    _, rows, lanes = parts.shape
    assert rows % tr == 0

    def body(p_ref, g_ref):
        g = p_ref[0].astype(F32)
        for k in range(1, N_DEV):
            g = g + p_ref[k].astype(F32)
        g_ref[...] = g

    return _pcall(
        body, name=name, grid=(rows // tr,),
        in_specs=[pl.BlockSpec((N_DEV, tr, lanes), lambda i: (0, i, 0))],
        out_specs=pl.BlockSpec((tr, lanes), lambda i: (i, 0)), out_shape=jax.ShapeDtypeStruct((rows, lanes), F32),
        blocks=[((N_DEV, tr, lanes), parts.dtype), ((tr, lanes), F32)],
    )(parts)


def adamw(g, w, m, v, *, name, tr):
    nl, k, n = w.shape
    tr = max(c for c in range(8, min(tr, k) + 1, 8) if k % c == 0)
    bc1 = 1.0 - ADAM_B1 ** ADAM_STEP
    bc2 = 1.0 - ADAM_B2 ** ADAM_STEP

    def body(g_ref, w_ref, m_ref, v_ref, d_ref, mo_ref, vo_ref):
        gv = g_ref[...]
        m_new = ADAM_B1 * m_ref[...] + (1.0 - ADAM_B1) * gv
        v_new = ADAM_B2 * v_ref[...] + (1.0 - ADAM_B2) * (gv * gv)
        mo_ref[...] = m_new
        vo_ref[...] = v_new
        d_ref[...] = -ADAM_LR * ((m_new / bc1) / (jnp.sqrt(v_new / bc2) + ADAM_EPS) + ADAM_WD * w_ref[...])

    spec = pl.BlockSpec((1, tr, n), lambda l, i: (l, i, 0))
    shape = jax.ShapeDtypeStruct(w.shape, F32)
    return _pcall(
        body, name=name, grid=(nl, k // tr), in_specs=[spec] * 4, out_specs=[spec] * 3, out_shape=[shape] * 3,
        blocks=[((1, tr, n), F32)] * 7,
    )(g, w, m, v)


def sum_adamw(partials, w, m, v, *, name, tr):
    nl, k, n = w.shape
    assert nl == len(partials) == 2
    tr = max(c for c in range(8, min(tr, k) + 1, 8) if k % c == 0)
    bc1 = 1.0 - ADAM_B1 ** ADAM_STEP
    bc2 = 1.0 - ADAM_B2 ** ADAM_STEP

    def body(p0_ref, p1_ref, w_ref, m_ref, v_ref, g_ref, d_ref, mo_ref, vo_ref):
        first = pl.program_id(0) == 0
        gv = jnp.where(first, p0_ref[0], p1_ref[0]).astype(F32)
        for s in range(1, N_DEV):
            gv = gv + jnp.where(first, p0_ref[s], p1_ref[s]).astype(F32)
        m_new = ADAM_B1 * m_ref[0] + (1.0 - ADAM_B1) * gv
        v_new = ADAM_B2 * v_ref[0] + (1.0 - ADAM_B2) * (gv * gv)
        g_ref[0] = gv
        mo_ref[0] = m_new
        vo_ref[0] = v_new
        d_ref[0] = -ADAM_LR * ((m_new / bc1) / (jnp.sqrt(v_new / bc2) + ADAM_EPS) + ADAM_WD * w_ref[0])

    spec = pl.BlockSpec((1, tr, n), lambda l, i: (l, i, 0))
    p0spec = pl.BlockSpec((N_DEV, tr, n), lambda l, i: (0, i * (1 - l), 0))
    p1spec = pl.BlockSpec((N_DEV, tr, n), lambda l, i: (0, i * l, 0))
    shape = jax.ShapeDtypeStruct(w.shape, F32)
    return _pcall(
        body, name=name, grid=(nl, k // tr), in_specs=[p0spec, p1spec, spec, spec, spec], out_specs=[spec] * 4,
        out_shape=[shape] * 4, blocks=[((N_DEV, tr, n), BF16)] * 2 + [((1, tr, n), F32)] * 7,
    )(partials[0], partials[1], w, m, v)


def travelling(a, by_cols):
    return jnp.swapaxes(a, -1, -2) if by_cols else a


def _row(v):
    return v.reshape(1, -1)


def ffn_fwd(x, h, w, pre, tag, next_gain):
    ga, gb, s = swiglu_fwd(h, w[pre + "_w_gate"], w[pre + "_w_up"], name=f"{tag}_gateup")
    if callable(w[pre + "_w_down"]):
        w[pre + "_w_down"] = w[pre + "_w_down"](s)
    out, h_next = matmul_res_norm(s, w[pre + "_w_down"], x, next_gain, scale=0.5, tm=512, name=f"{tag}_down")
    return out, h_next, (x, h, ga, gb, s)


def ffn_bwd_weights(dxb, saved, w, pre, tag):
    x, h, a, b, s = saved
    da, db = swiglu_bwd(dxb, w[pre + "_w_down"], a, b, scale=0.5, name=f"{tag}_dgateup")
    g_down = matmul([(s, dxb)], "tn", tm=1408, tn=1024, tk=2048, out_dtype=BF16, scale=0.5, name=f"{tag}_gdown")
    g_gate = matmul([(da, h)], "tn", tm=1408, tn=1024, tk=2048, out_dtype=BF16, name=f"{tag}_ggate")
    g_up = matmul([(db, h)], "tn", tm=1408, tn=1024, tk=2048, out_dtype=BF16, name=f"{tag}_gup")
    return {pre + "_w_gate": g_gate, pre + "_w_up": g_up, pre + "_w_down": g_down}, (da, db)


def ffn_bwd_input(dx, rest, saved, gain, w, pre, tag):
    da, db = rest
    x = saved[0]
    return matmul_rms_bwd([(da, w[pre + "_w_gate"]), (db, w[pre + "_w_up"])], x, gain, dx, tm=256, name=f"{tag}_dh")


def mixer_fwd(x, h, w, tables, tag, next_gain):
    proj = matmul([(h, w["w_in"])], "nt", tm=512, tn=1280, tk=1024, out_dtype=BF16, name=f"{tag}_in")
    qks, vs = rope_split(proj, tables, name=f"{tag}_rope")
    outs, lses = [], []
    for g in range(N_DIL_GROUPS):
        o, lse = dil_fwd(qks[g], vs[g], name=f"{tag}_dil{g}")
        outs.append(o)
        lses.append(lse)
    odil, lse = dil_merge(outs, lses, name=f"{tag}_merge")
    osb = sb_fwd(proj, name=f"{tag}_sb")
    y, u1, u2 = gate_fwd(odil, osb, w["w_proj_dil"], w["w_proj_sb"], proj, name=f"{tag}_gate")
    out, h_next = matmul_res_norm(y, w["w_out"], x, next_gain, scale=1.0, tm=512, name=f"{tag}_out")
    return out, h_next, (x, h, proj, qks, vs, odil, lse, osb, u1, u2, y)


def mixer_bwd_weights(dxb, saved, w, tables, tag):
    x, h, proj, qks, vs, odil, lse, osb, u1, u2, y = saved
    t = x.shape[0]
    g_out = matmul([(y, dxb)], "tn", tm=1024, tn=1024, tk=2048, out_dtype=BF16, name=f"{tag}_gout")
    du1, du2, dgate = gate_bwd(dxb, w["w_out"], u1, u2, proj, name=f"{tag}_dgate")
    g_pd = matmul([(du1, odil)], "tn", tm=1024, tn=256, tk=2048, out_dtype=BF16, name=f"{tag}_gpd")
    g_ps = matmul([(du2, osb)], "tn", tm=1024, tn=256, tk=2048, out_dtype=BF16, name=f"{tag}_gps")
    dodil = matmul([(du1, w["w_proj_dil"])], "nn", tm=512, tn=256, tk=1024, out_dtype=F32, name=f"{tag}_dodil")
    dosb = matmul([(du2, w["w_proj_sb"])], "nn", tm=512, tn=256, tk=1024, out_dtype=F32, name=f"{tag}_dosb")
    dsum, do_wide, lse_wide, dsum_wide = dil_bwd_prep(dodil, odil, lse, name=f"{tag}_dprep")
    dos = [dodil[None]] + list(do_wide)
    lss = [lse[None]] + list(lse_wide)
    dss = [dsum[None]] + list(dsum_wide)
    dqs, dks, dvs = [], [], []
    for g in range(N_DIL_GROUPS):
        dq, dk, dv = dil_bwd(qks[g], vs[g], dos[g], lss[g], dss[g], name=f"{tag}_ddil{g}")
        dqs.append(dq)
        dks.append(dk)
        dvs.append(dv)
    gtot = head_sums(dosb, osb, round_a=True, name=f"{tag}_gsum")
    sb_grads = sb_bwd(proj, dosb, gtot, name=f"{tag}_dsb")
    dproj = rope_join(dqs, dks, dvs, sb_grads, dgate, tables, name=f"{tag}_drope")
    g_in = matmul([(dproj, h)], "tn", tm=1280, tn=1024, tk=2048, out_dtype=BF16, name=f"{tag}_gin")
    return {"w_in": g_in, "w_proj_dil": g_pd, "w_proj_sb": g_ps, "w_out": g_out}, dproj


def mixer_bwd_input(dx, dproj, saved, gain, w, tag):
    x = saved[0]
    return matmul_rms_bwd([(dproj, w["w_in"])], x, gain, dx, tm=256, name=f"{tag}_dh")


def kernel(x, norm_ffn1, ffn1_w_gate, ffn1_w_up, ffn1_w_down, norm_mix, w_in, w_proj_dil, w_proj_sb, w_out, norm_ffn2, ffn2_w_gate, ffn2_w_up, ffn2_w_down, norm_final, loss_target, m_norm_ffn1, m_ffn1_w_gate, m_ffn1_w_up, m_ffn1_w_down, m_norm_mix, m_w_in, m_w_proj_dil, m_w_proj_sb, m_w_out, m_norm_ffn2, m_ffn2_w_gate, m_ffn2_w_up, m_ffn2_w_down, m_norm_final, v_norm_ffn1, v_ffn1_w_gate, v_ffn1_w_up, v_ffn1_w_down, v_norm_mix, v_w_in, v_w_proj_dil, v_w_proj_sb, v_w_out, v_norm_ffn2, v_ffn2_w_gate, v_ffn2_w_up, v_ffn2_w_down, v_norm_final):
    args = dict(locals())
    t = x.shape[1]
    xs = x.reshape(t, D_MODEL)
    target = loss_target.reshape(t, D_MODEL)
    tables = rope_tables(t)

    parts = [(l, p) for l in range(2) for p in SUBBLOCKS]
    gains = {n: args[n] for n in NORM_ROWS}

    shipments = []
    for l, p in parts:
        if (l, p) == parts[0]:
            shipments += [(l, p, SUBBLOCKS[p][:2], CHIP_PEERS), (l, p, SUBBLOCKS[p][2:], ALL_PEERS)]
        else:
            shipments.append((l, p, SUBBLOCKS[p], ALL_PEERS))
    in_flight, order_token = [], jnp.zeros((1, 1), F32)
    for l, p, tensors, masks in shipments:
        shards = [travelling(args[n][l], by_cols).astype(BF16) for n, _, by_cols, _ in tensors]
        shards[0] = shards[0] + order_token.astype(BF16)
        in_flight.append(spread_start(shards, per_peer=False, masks=masks, name=f"gather_start_l{l}_{tensors[0][0]}"))
        order_token = in_flight[-1][-1][0:1, 0:1]

    def arrived(i, after):
        l, p, tensors, masks = shipments[i]
        tag = f"l{l}_{tensors[0][0]}"
        lands = spread_wait(in_flight[i], after, per_peer=False, masks=masks, name=f"gather_wait_{tag}")
        if masks is CHIP_PEERS:
            lands = relay_wait(relay_start(lands, name=f"gather_relay_{tag}"), name=f"gather_relayed_{tag}")
        return {n: land.reshape(-1, land.shape[-1]) for (n, _, _, _), land in zip(tensors, lands)}

    def weights_of(l, p, after):
        mine = [i for i, s in enumerate(shipments) if s[0:2] == (l, p)]
        w = arrived(mine[0], after)
        for i in mine[1:]:
            for n, _, _, _ in shipments[i][2]:
                w[n] = functools.partial(lambda after, i, n: arrived(i, after)[n], i=i, n=n)
        return w

    saved, weights = {}, {}
    act = xs
    h = rms_fwd(xs, _row(gains["norm_ffn1"][0]) + order_token, name="l0_ffn1_norm")
    for i, (l, p) in enumerate(parts):
        weights[(l, p)] = weights_of(l, p, h if i == 0 else act)
        nl, np_ = parts[i + 1] if i + 1 < len(parts) else (None, None)
        next_gain = _row(gains["norm_" + np_][nl]) if np_ else None
        if p == "mix":
            act, h, saved[(l, p)] = mixer_fwd(act, h, weights[(l, p)], tables, f"l{l}_mix", next_gain)
        else:
            act, h, saved[(l, p)] = ffn_fwd(act, h, weights[(l, p)], p, f"l{l}_{p}", next_gain)
    dx, dxb, g_final, loss_part = final_loss(act, _row(norm_final), target, name="loss_head")

    gain_grads, sent = {}, {}
    order_token = jnp.zeros((1, 1), F32)
    for l, p in reversed(parts):
        w, sv = weights[(l, p)], saved[(l, p)]
        if p == "mix":
            gw, rest = mixer_bwd_weights(dxb, sv, w, tables, f"l{l}_mix")
        else:
            gw, rest = ffn_bwd_weights(dxb, sv, w, p, f"l{l}_{p}")
        slices = [gw[n].reshape(N_DEV, -1, gw[n].shape[-1]) for n, _, _, _ in SUBBLOCKS[p]]
        sent[(l, p)] = spread_start(slices, per_peer=True, name=f"reduce_start_l{l}_{p}")
        gain = _row(gains["norm_" + p][l]) + sent[(l, p)][-1][0:1, 0:1]
        if p == "mix":
            dx, dxb, gain_grads[("norm_mix", l)] = mixer_bwd_input(dx, rest, sv, gain, w, f"l{l}_mix")
        else:
            dx, dxb, gain_grads[("norm_" + p, l)] = ffn_bwd_input(dx, rest, sv, gain, w, p, f"l{l}_{p}")

    partials, big_all = {}, [{}, {}, {}, {}]

    def receive(l, p, after):
        lands = spread_wait(sent[(l, p)], after, per_peer=True, name=f"reduce_wait_l{l}_{p}")
        for (n, _, _, _), land in zip(SUBBLOCKS[p], lands):
            partials.setdefault(n, [None, None])[l] = land

    def update(p):
        for n, _, by_cols, _ in SUBBLOCKS[p]:
            outs = sum_adamw(partials[n], travelling(args[n], by_cols), travelling(args["m_" + n], by_cols),
                             travelling(args["v_" + n], by_cols), tr=256, name=f"update_{n}")
            for kind, arr in enumerate(outs):
                big_all[kind][n] = travelling(arr, by_cols)
        return outs[1]

    for l, p in reversed(parts[1:]):
        receive(l, p, dx)
    update("ffn2")
    done = update("mix")
    receive(*parts[0], done)
    done = update("ffn1")

    loss_row = jnp.pad(loss_part[:, :1], ((0, 0), (0, LANES - 1)))
    small = jnp.concatenate([gain_grads[(n, l)] for n in NORM_ROWS for l in range(2)] + [g_final, loss_row], axis=0)
    small_g = sum_partials(all_gather_rows(small, done, name="gather_gain_grads"), tr=8, name="sum_gain_grads")
    zero_row = jnp.zeros((1, LANES), F32)
    small_of = lambda pre: jnp.concatenate([args[pre + n] for n in NORM_ROWS] + [_row(args[pre + "norm_final"]), zero_row], axis=0)[None]
    small_out = adamw(small_g[None], small_of(""), small_of("m_"), small_of("v_"), tr=8, name="update_gains")
    small_all = [small_g] + [o[0] for o in small_out]

    def gains_of(s):
        out = {n: s[2 * i:2 * i + 2] for i, n in enumerate(NORM_ROWS)}
        out["norm_final"] = s[6]
        return out

    order = ["norm_ffn1", "ffn1_w_gate", "ffn1_w_up", "ffn1_w_down", "norm_mix", "w_in", "w_proj_dil", "w_proj_sb", "w_out",
             "norm_ffn2", "ffn2_w_gate", "ffn2_w_up", "ffn2_w_down", "norm_final"]
    results = []
    for kind in range(4):
        both = {**big_all[kind], **gains_of(small_all[kind])}
        results += [both[n] for n in order]
    loss = small_g[7, 0]
    return (loss, dx.reshape(1, t, D_MODEL), *results)
```

```python
import functools

import jax
import jax.numpy as jnp
from jax import lax
from jax.experimental import pallas as pl
from jax.experimental.pallas import tpu as pltpu

F32 = jnp.float32
BF16 = jnp.bfloat16

D_MODEL = 1024
HEAD_DIM = 64
GROUP_W = 256
D_IN = 5120
N_DIL_GROUPS = 3
DIL_SPAN = 128
DILATIONS = (1, 4, 16)
ROPE_THETA = 500000.0
ROPE_DIM = 16
RMS_EPS = 1e-6
ATT_SCALE = HEAD_DIM ** -0.5
QS_BLK, KS_BLK, VS_BLK = 9, 10, 11
GATE_DIL_BLK, GATE_SB_BLK = 3, 4

ADAM_LR, ADAM_B1, ADAM_B2, ADAM_EPS, ADAM_WD, ADAM_STEP = 0.001, 0.9, 0.999, 1e-08, 0.01, 10

N_DEV = 8
LANES = 1024
VMEM_PHYSICAL_V7X = 64 << 20
VMEM_TEMP_HEADROOM = 20 << 20

PACK_LAYOUT = (
    ("ffn1_w_gate", 352, True, (1024, 2816)),
    ("ffn1_w_up", 352, True, (1024, 2816)),
    ("ffn1_w_down", 352, False, (2816, 1024)),
    ("w_in", 640, True, (1024, 5120)),
    ("w_proj_dil", 32, True, (256, 1024)),
    ("w_proj_sb", 32, True, (256, 1024)),
    ("w_out", 128, False, (1024, 1024)),
    ("ffn2_w_gate", 352, True, (1024, 2816)),
    ("ffn2_w_up", 352, True, (1024, 2816)),
    ("ffn2_w_down", 352, False, (2816, 1024)),
)
SUBBLOCKS = {"ffn1": PACK_LAYOUT[0:3], "mix": PACK_LAYOUT[3:7], "ffn2": PACK_LAYOUT[7:10]}
NORM_ROWS = ("norm_ffn1", "norm_mix", "norm_ffn2")


def _nbytes(shape, dtype):
    n = 1
    for s in shape:
        n *= s
    return n * jnp.dtype(dtype).itemsize


def _pcall(body, *, name, grid, in_specs, out_specs, out_shape, blocks, scratch_shapes=(), scratch_bytes=0):
    need = 2 * sum(_nbytes(s, d) for s, d in blocks) + scratch_bytes + VMEM_TEMP_HEADROOM
    limit = min(need, VMEM_PHYSICAL_V7X - (4 << 20))
    in_hbm = lambda s: pltpu.HBM(s.shape, s.dtype)
    out_shape = [in_hbm(s) for s in out_shape] if isinstance(out_shape, (list, tuple)) else in_hbm(out_shape)
    call = pl.pallas_call(
        body, name=name, grid=grid, in_specs=in_specs, out_specs=out_specs, out_shape=out_shape,
        scratch_shapes=scratch_shapes,
        compiler_params=pltpu.CompilerParams(vmem_limit_bytes=limit),
    )
    return lambda *args: call(*[pltpu.with_memory_space_constraint(a, pltpu.HBM) for a in args])


def _dot(a, b, form):
    dn = {"nn": (((1,), (0,)), ((), ())), "nt": (((1,), (1,)), ((), ())), "tn": (((0,), (0,)), ((), ()))}[form]
    return lax.dot_general(a.astype(BF16), b.astype(BF16), dn, preferred_element_type=F32)


def _sigmoid(x):
    return 1.0 / (1.0 + jnp.exp(-x))


def matmul(pairs, form, *, tm, tn, tk, out_dtype, name, scale=1.0, res=None):
    a0, b0 = pairs[0]
    if form == "tn":
        kdim, m = a0.shape
        n = b0.shape[1]
    else:
        m, kdim = a0.shape
        n = b0.shape[1] if form == "nn" else b0.shape[0]
    tm, tn, tk = min(tm, m), min(tn, n), min(tk, kdim)
    assert m % tm == 0 and n % tn == 0 and kdim % tk == 0, (name, m, n, kdim, tm, tn, tk)
    nk = kdim // tk
    npairs = len(pairs)

    if form == "tn":
        a_blk, a_map = (tk, tm), (lambda j, i, k: (k, i))
    else:
        a_blk, a_map = (tm, tk), (lambda j, i, k: (i, k))
    if form == "nt":
        b_blk, b_map = (tn, tk), (lambda j, i, k: (j, k))
    else:
        b_blk, b_map = (tk, tn), (lambda j, i, k: (k, j))
    o_map = lambda j, i, k: (i, j)

    def body(*refs):
        ab = refs[:2 * npairs]
        rest = refs[2 * npairs:]
        if res is not None:
            r_ref, o_ref = rest[0], rest[1]
            rest = rest[2:]
        else:
            r_ref, o_ref = None, rest[0]
            rest = rest[1:]

        def partial_sum():
            p = _dot(ab[0][...], ab[1][...], form)
            for q in range(1, npairs):
                p = p + _dot(ab[2 * q][...], ab[2 * q + 1][...], form)
            return p

        def finish(acc):
            out = acc * scale if scale != 1.0 else acc
            if r_ref is not None:
                out = r_ref[...] + out
            o_ref[...] = out.astype(out_dtype)

        if nk == 1:
            finish(partial_sum())
        else:
            acc_ref = rest[0]
            k = pl.program_id(2)

            @pl.when(k == 0)
            def _():
                acc_ref[...] = partial_sum()

            @pl.when(k > 0)
            def _():
                acc_ref[...] += partial_sum()

            @pl.when(k == nk - 1)
            def _():
                finish(acc_ref[...])

    in_specs, args, blocks = [], [], []
    for a, b in pairs:
        in_specs += [pl.BlockSpec(a_blk, a_map), pl.BlockSpec(b_blk, b_map)]
        args += [a, b]
        blocks += [(a_blk, a.dtype), (b_blk, b.dtype)]
    if res is not None:
        in_specs.append(pl.BlockSpec((tm, tn), o_map))
        args.append(res)
        blocks.append(((tm, tn), res.dtype))
    blocks.append(((tm, tn), out_dtype))
    scratch = [pltpu.VMEM((tm, tn), F32)] if nk > 1 else []
    return _pcall(
        body, name=name, grid=(n // tn, m // tm, nk), in_specs=in_specs,
        out_specs=pl.BlockSpec((tm, tn), o_map), out_shape=jax.ShapeDtypeStruct((m, n), out_dtype),
        blocks=blocks, scratch_shapes=scratch, scratch_bytes=(tm * tn * 4 if nk > 1 else 0),
    )(*args)


def swiglu_fwd(h, wg_t, wu_t, *, name, tm=512, tn=1408):
    t, d = h.shape
    f = wg_t.shape[0]
    tm, tn = min(tm, t), min(tn, f)

    def body(h_ref, wg_ref, wu_ref, ga_ref, gb_ref, s_ref):
        hh = h_ref[...]
        a = _dot(hh, wg_ref[...], "nt")
        b = _dot(hh, wu_ref[...], "nt")
        sg = _sigmoid(a)
        silu = a * sg
        ga_ref[...] = (b * (sg * (1.0 + a * (1.0 - sg)))).astype(BF16)
        gb_ref[...] = silu.astype(BF16)
        s_ref[...] = (silu * b).astype(BF16)

    w_spec = pl.BlockSpec((tn, d), lambda j, i: (j, 0))
    o_spec = pl.BlockSpec((tm, tn), lambda j, i: (i, j))
    o_shape = jax.ShapeDtypeStruct((t, f), BF16)
    return _pcall(
        body, name=name, grid=(f // tn, t // tm),
        in_specs=[pl.BlockSpec((tm, d), lambda j, i: (i, 0)), w_spec, w_spec],
        out_specs=[o_spec, o_spec, o_spec], out_shape=[o_shape, o_shape, o_shape],
        blocks=[((tm, d), BF16), ((tn, d), BF16), ((tn, d), BF16)] + [((tm, tn), BF16)] * 3,
    )(h, wg_t, wu_t)


def swiglu_bwd(dyb, wd, ga, gb, *, name, scale, tm=512, tn=1408):
    t, d = dyb.shape
    f = wd.shape[0]
    tm, tn = min(tm, t), min(tn, f)

    def body(dy_ref, wd_ref, ga_ref, gb_ref, da_ref, db_ref):
        ds = _dot(dy_ref[...], wd_ref[...], "nt") * scale
        da_ref[...] = (ds * ga_ref[...].astype(F32)).astype(BF16)
        db_ref[...] = (ds * gb_ref[...].astype(F32)).astype(BF16)

    o_spec = pl.BlockSpec((tm, tn), lambda j, i: (i, j))
    o_shape = jax.ShapeDtypeStruct((t, f), BF16)
    return _pcall(
        body, name=name, grid=(f // tn, t // tm),
        in_specs=[pl.BlockSpec((tm, d), lambda j, i: (i, 0)), pl.BlockSpec((tn, d), lambda j, i: (j, 0)), o_spec, o_spec],
        out_specs=[o_spec, o_spec], out_shape=[o_shape, o_shape],
        blocks=[((tm, d), BF16), ((tn, d), BF16)] + [((tm, tn), BF16)] * 4,
    )(dyb, wd, ga, gb)


def gate_fwd(odil, osb, wpd_t, wps_t, proj, *, name, tm=512):
    t = odil.shape[0]
    tm = min(tm, t)

    def body(od_ref, os_ref, wpd_ref, wps_ref, g1_ref, g2_ref, y_ref, u1_ref, u2_ref):
        u1 = _dot(od_ref[...], wpd_ref[...], "nt")
        u2 = _dot(os_ref[...], wps_ref[...], "nt")
        y = _sigmoid(g1_ref[...].astype(F32)) * u1 + _sigmoid(g2_ref[...].astype(F32)) * u2
        y_ref[...] = y.astype(BF16)
        u1_ref[...] = u1.astype(BF16)
        u2_ref[...] = u2.astype(BF16)

    o_spec = pl.BlockSpec((tm, D_MODEL), lambda i: (i, 0))
    w_spec = pl.BlockSpec((D_MODEL, GROUP_W), lambda i: (0, 0))
    a_spec = pl.BlockSpec((tm, GROUP_W), lambda i: (i, 0))
    o_shape = jax.ShapeDtypeStruct((t, D_MODEL), BF16)
    return _pcall(
        body, name=name, grid=(t // tm,),
        in_specs=[a_spec, a_spec, w_spec, w_spec,
                  pl.BlockSpec((tm, D_MODEL), lambda i: (i, GATE_DIL_BLK)),
                  pl.BlockSpec((tm, D_MODEL), lambda i: (i, GATE_SB_BLK))],
        out_specs=[o_spec, o_spec, o_spec], out_shape=[o_shape, o_shape, o_shape],
        blocks=[((tm, GROUP_W), F32)] * 2 + [((D_MODEL, GROUP_W), BF16)] * 2 + [((tm, D_MODEL), BF16)] * 5,
    )(odil, osb, wpd_t, wps_t, proj, proj)


def gate_bwd(dxb, wout, u1, u2, proj, *, name, tm=512):
    t = dxb.shape[0]
    tm = min(tm, t)

    def body(dx_ref, w_ref, u1_ref, u2_ref, g1_ref, g2_ref, du1_ref, du2_ref, dg_ref):
        dy = _dot(dx_ref[...], w_ref[...], "nt")
        s1 = _sigmoid(g1_ref[...].astype(F32))
        s2 = _sigmoid(g2_ref[...].astype(F32))
        du1_ref[...] = (dy * s1).astype(BF16)
        du2_ref[...] = (dy * s2).astype(BF16)
        dg_ref[:, :D_MODEL] = (dy * u1_ref[...].astype(F32) * s1 * (1.0 - s1)).astype(BF16)
        dg_ref[:, D_MODEL:] = (dy * u2_ref[...].astype(F32) * s2 * (1.0 - s2)).astype(BF16)

    o_spec = pl.BlockSpec((tm, D_MODEL), lambda i: (i, 0))
    o_shape = jax.ShapeDtypeStruct((t, D_MODEL), BF16)
    return _pcall(
        body, name=name, grid=(t // tm,),
        in_specs=[o_spec, pl.BlockSpec((D_MODEL, D_MODEL), lambda i: (0, 0)), o_spec, o_spec,
                  pl.BlockSpec((tm, D_MODEL), lambda i: (i, GATE_DIL_BLK)),
                  pl.BlockSpec((tm, D_MODEL), lambda i: (i, GATE_SB_BLK))],
        out_specs=[o_spec, o_spec, pl.BlockSpec((tm, 2 * D_MODEL), lambda i: (i, 0))],
        out_shape=[o_shape, o_shape, jax.ShapeDtypeStruct((t, 2 * D_MODEL), BF16)],
        blocks=[((tm, D_MODEL), BF16)] * 9 + [((D_MODEL, D_MODEL), BF16)],
    )(dxb, wout, u1, u2, proj, proj)


def rms_fwd(x, gain, *, name, tm=512):
    t, d = x.shape
    tm = min(tm, t)

    def body(x_ref, g_ref, h_ref):
        xv = x_ref[...]
        rstd = lax.rsqrt(jnp.mean(xv * xv, axis=1, keepdims=True) + RMS_EPS)
        h_ref[...] = (xv * rstd * g_ref[...]).astype(BF16)

    return _pcall(
        body, name=name, grid=(t // tm,),
        in_specs=[pl.BlockSpec((tm, d), lambda i: (i, 0)), pl.BlockSpec((1, d), lambda i: (0, 0))],
        out_specs=pl.BlockSpec((tm, d), lambda i: (i, 0)), out_shape=jax.ShapeDtypeStruct((t, d), BF16),
        blocks=[((tm, d), F32), ((tm, d), BF16)],
    )(x, gain)


def matmul_res_norm(a, b, res, next_gain, *, scale, tm, name):
    t, k = a.shape
    d = b.shape[1]
    tm = min(tm, t)
    with_norm = next_gain is not None

    def body(a_ref, b_ref, r_ref, *rest):
        out = r_ref[...] + _dot(a_ref[...], b_ref[...], "nn") * scale
        if with_norm:
            g_ref, o_ref, h_ref = rest
            rstd = lax.rsqrt(jnp.mean(out * out, axis=1, keepdims=True) + RMS_EPS)
            h_ref[...] = (out * rstd * g_ref[...]).astype(BF16)
        else:
            o_ref, = rest
        o_ref[...] = out

    row = pl.BlockSpec((tm, d), lambda i: (i, 0))
    in_specs = [pl.BlockSpec((tm, k), lambda i: (i, 0)), pl.BlockSpec((k, d), lambda i: (0, 0)), row]
    args = [a, b, res]
    out_specs, out_shape = [row], [jax.ShapeDtypeStruct((t, d), F32)]
    if with_norm:
        in_specs.append(pl.BlockSpec((1, d), lambda i: (0, 0)))
        args.append(next_gain)
        out_specs.append(row)
        out_shape.append(jax.ShapeDtypeStruct((t, d), BF16))
    outs = _pcall(
        body, name=name, grid=(t // tm,), in_specs=in_specs, out_specs=out_specs, out_shape=out_shape,
        blocks=[((tm, k), a.dtype), ((k, d), b.dtype), ((tm, d), F32), ((tm, d), F32), ((tm, d), BF16)],
    )(*args)
    return (outs[0], outs[1]) if with_norm else (outs[0], None)


def _rms_bwd_rows(dhv, xv, g, drv):
    rstd = lax.rsqrt(jnp.mean(xv * xv, axis=1, keepdims=True) + RMS_EPS)
    xh = xv * rstd
    dxh = dhv * g
    dx = drv + rstd * (dxh - xh * jnp.mean(dxh * xh, axis=1, keepdims=True))
    return dx, jnp.sum(dhv * xh, axis=0, keepdims=True)


def matmul_rms_bwd(pairs, x, gain, dres, *, tm, name):
    t, d = x.shape
    tm = min(tm, t)
    npairs = len(pairs)

    def body(*refs):
        ab = refs[:2 * npairs]
        x_ref, g_ref, dr_ref, dx_ref, dxb_ref, dg_ref = refs[2 * npairs:]
        dh = _dot(ab[0][...], ab[1][...], "nn")
        for q in range(1, npairs):
            dh = dh + _dot(ab[2 * q][...], ab[2 * q + 1][...], "nn")
        dx, part = _rms_bwd_rows(dh, x_ref[...], g_ref[...], dr_ref[...])
        dx_ref[...] = dx
        dxb_ref[...] = dx.astype(BF16)

        @pl.when(pl.program_id(0) == 0)
        def _():
            dg_ref[...] = part

        @pl.when(pl.program_id(0) > 0)
        def _():
            dg_ref[...] += part

    in_specs, args, blocks = [], [], []
    for a, b in pairs:
        k = a.shape[1]
        in_specs += [pl.BlockSpec((tm, k), lambda i: (i, 0)), pl.BlockSpec((k, d), lambda i: (0, 0))]
        args += [a, b]
        blocks += [((tm, k), a.dtype), ((k, d), b.dtype)]
    row = pl.BlockSpec((tm, d), lambda i: (i, 0))
    vec = pl.BlockSpec((1, d), lambda i: (0, 0))
    return _pcall(
        body, name=name, grid=(t // tm,), in_specs=in_specs + [row, vec, row], out_specs=[row, row, vec],
        out_shape=[jax.ShapeDtypeStruct((t, d), F32), jax.ShapeDtypeStruct((t, d), BF16), jax.ShapeDtypeStruct((1, d), F32)],
        blocks=blocks + [((tm, d), F32)] * 3 + [((tm, d), BF16)],
    )(*args, x, gain, dres)


def final_loss(x, gain, target, *, name, tm=512):
    t, d = x.shape
    tm = min(tm, t)

    def body(x_ref, g_ref, t_ref, dx_ref, dxb_ref, dg_ref, loss_ref):
        xv = x_ref[...]
        g = g_ref[...]
        rstd = lax.rsqrt(jnp.mean(xv * xv, axis=1, keepdims=True) + RMS_EPS)
        xh = xv * rstd
        err = xh * g - t_ref[...]
        dy = err * (1.0 / d)
        dxh = dy * g
        dx = rstd * (dxh - xh * jnp.mean(dxh * xh, axis=1, keepdims=True))
        dx_ref[...] = dx
        dxb_ref[...] = dx.astype(BF16)
        part = jnp.sum(dy * xh, axis=0, keepdims=True)
        sq = jnp.sum(jnp.sum(err * err, axis=1, keepdims=True), axis=0, keepdims=True) * (0.5 / d)
        lpart = jnp.broadcast_to(sq, (1, 128))

        @pl.when(pl.program_id(0) == 0)
        def _():
            dg_ref[...] = part
            loss_ref[...] = lpart

        @pl.when(pl.program_id(0) > 0)
        def _():
            dg_ref[...] += part
            loss_ref[...] += lpart

    row = pl.BlockSpec((tm, d), lambda i: (i, 0))
    vec = pl.BlockSpec((1, d), lambda i: (0, 0))
    return _pcall(
        body, name=name, grid=(t // tm,), in_specs=[row, vec, row],
        out_specs=[row, row, vec, pl.BlockSpec((1, 128), lambda i: (0, 0))],
        out_shape=[jax.ShapeDtypeStruct((t, d), F32), jax.ShapeDtypeStruct((t, d), BF16),
                   jax.ShapeDtypeStruct((1, d), F32), jax.ShapeDtypeStruct((1, 128), F32)],
        blocks=[((tm, d), F32)] * 3 + [((tm, d), BF16)],
    )(x, gain, target)


def rope_tables(t):
    pos = jnp.arange(t, dtype=F32)
    inv_freq = ROPE_THETA ** (-jnp.arange(0, ROPE_DIM, 2, dtype=F32) / ROPE_DIM)
    ang = pos[:, None] * inv_freq[None, :]
    cos, sin = jnp.cos(ang), jnp.sin(ang)
    half = ROPE_DIM // 2
    pad = HEAD_DIM - ROPE_DIM
    zeros, ones, rest = jnp.zeros((t, half), F32), jnp.ones((t, pad), F32), jnp.zeros((t, pad), F32)
    c = [cos, cos, ones] * 2
    sa = [-sin, zeros, rest] * 2
    sb = [zeros, sin, rest] * 2
    return jnp.concatenate(c + sa + sb, axis=1)


def _rotate(xv, cv, sav, sbv):
    halves = []
    for half in range(2):
        x = xv[:, 128 * half:128 * (half + 1)]
        halves.append(x * cv + pltpu.roll(x, 120, 1) * sav + pltpu.roll(x, 8, 1) * sbv)
    return jnp.concatenate(halves, axis=1)


STAGE_CHUNKS = 4


def _stage(tm):
    return dict(scratch_shapes=[pltpu.VMEM((STAGE_CHUNKS, tm, 128), F32)], scratch_bytes=STAGE_CHUNKS * tm * 128 * 4)


def _split_residues(stage_ref, val, out_ref, d, col, dtype):
    rows, width = val.shape
    if d == 1:
        out_ref[0, :, col:col + width] = val.astype(dtype)
        return
    chunks = width // 128
    for c in range(chunks):
        stage_ref[c] = val[:, 128 * c:128 * (c + 1)]
    for r in range(d):
        for c in range(chunks):
            out_ref[r, :, col + 128 * c:col + 128 * (c + 1)] = stage_ref[c, pl.ds(r, rows // d, stride=d), :].astype(dtype)


def _join_residues(stage_ref, in_ref, d, col=0, width=GROUP_W):
    if d == 1:
        return in_ref[0, :, col:col + width].astype(F32)
    rows = in_ref.shape[1] * d
    chunks = width // 128
    for r in range(d):
        for c in range(chunks):
            stage_ref[c, pl.ds(r, rows // d, stride=d), :] = in_ref[r, :, col + 128 * c:col + 128 * (c + 1)].astype(F32)
    return jnp.concatenate([stage_ref[c] for c in range(chunks)], axis=1)


def rope_split(proj, tables, *, name, tm=512):
    c = sa = sb = tables
    t = tables.shape[0]
    tm = min(tm, t)

    def body(*refs):
        pieces = refs[0:9]
        c_ref, sa_ref, sb_ref = refs[9:12]
        qk_out, v_out = refs[12:15], refs[15:18]
        stage = refs[18]
        cv, sav, sbv = c_ref[...], sa_ref[...], sb_ref[...]
        for g, d in enumerate(DILATIONS):
            for kind in range(3):
                xv = pieces[3 * kind + g][...].astype(F32)
                if kind < 2:
                    _split_residues(stage, _rotate(xv, cv, sav, sbv), qk_out[g], d, GROUP_W * kind, BF16)
                else:
                    _split_residues(stage, xv, v_out[g], d, 0, BF16)

    tabs = [pl.BlockSpec((tm, 128), functools.partial(lambda i, cb: (i, cb), cb=cb)) for cb in range(3)]
    in_specs = [pl.BlockSpec((tm, GROUP_W), functools.partial(lambda i, cb: (i, cb), cb=cb)) for cb in range(9)]
    out_specs = ([pl.BlockSpec((d, tm // d, 2 * GROUP_W), lambda i: (0, i, 0)) for d in DILATIONS]
                 + [pl.BlockSpec((d, tm // d, GROUP_W), lambda i: (0, i, 0)) for d in DILATIONS])
    out_shape = ([jax.ShapeDtypeStruct((d, t // d, 2 * GROUP_W), BF16) for d in DILATIONS]
                 + [jax.ShapeDtypeStruct((d, t // d, GROUP_W), BF16) for d in DILATIONS])
    outs = _pcall(
        body, name=name, grid=(t // tm,), in_specs=in_specs + tabs, out_specs=out_specs, out_shape=out_shape,
        blocks=[((tm, GROUP_W), BF16)] * 18 + [((tm, 128), F32)] * 3,
        **_stage(tm),
    )(*([proj] * 9), c, sa, sb)
    return outs[0:3], outs[3:6]


def rope_join(dqs, dks, dvs, sb_grads, dgate, tables, *, name, tm=512):
    c = sa = sb = tables
    t = tables.shape[0]
    tm = min(tm, t)

    def body(*refs):
        pieces, sb_refs, dgate_ref = refs[0:9], refs[9:12], refs[12]
        c_ref, sa_ref, sb_ref = refs[13:16]
        o_ref, stage = refs[16], refs[17]
        cv, sav, sbv = c_ref[...], -sa_ref[...], -sb_ref[...]
        for kind in range(3):
            for g, d in enumerate(DILATIONS):
                xv = _join_residues(stage, pieces[3 * kind + g], d)
                if kind < 2:
                    xv = _rotate(xv, cv, sav, sbv)
                col = GROUP_W * (3 * kind + g)
                o_ref[:, col:col + GROUP_W] = xv.astype(BF16)
        for j in range(3):
            o_ref[:, GROUP_W * (QS_BLK + j):GROUP_W * (QS_BLK + j + 1)] = sb_refs[j][...].astype(BF16)
        o_ref[:, D_MODEL * GATE_DIL_BLK:] = dgate_ref[...]

    tabs = [pl.BlockSpec((tm, 128), functools.partial(lambda i, cb: (i, cb), cb=cb)) for cb in range(3)]
    nat = lambda w: pl.BlockSpec((tm, w), lambda i: (i, 0))
    in_specs = [pl.BlockSpec((d, tm // d, GROUP_W), lambda i: (0, i, 0)) for _ in range(3) for d in DILATIONS]
    in_specs += [nat(GROUP_W)] * 3 + [nat(2 * D_MODEL)]
    return _pcall(
        body, name=name, grid=(t // tm,), in_specs=in_specs + tabs,
        out_specs=nat(D_IN), out_shape=jax.ShapeDtypeStruct((t, D_IN), BF16),
        blocks=[((tm, GROUP_W), F32)] * 12 + [((tm, 128), F32)] * 3 + [((tm, 2 * D_MODEL), BF16), ((tm, D_IN), BF16)],
        **_stage(tm),
    )(*dqs, *dks, *dvs, *sb_grads, dgate, c, sa, sb)


def _head_mask(h):
    lane = lax.broadcasted_iota(jnp.int32, (1, GROUP_W), 1)
    return (lane // HEAD_DIM) == h


def _band_masks(heads):
    ri = lax.broadcasted_iota(jnp.int32, (heads * DIL_SPAN, DIL_SPAN), 0) % DIL_SPAN
    ci = lax.broadcasted_iota(jnp.int32, (heads * DIL_SPAN, DIL_SPAN), 1)
    return ci <= ri, ci >= ri


def dil_fwd(qk, v, *, name):
    d, nsub, _ = qk.shape
    nblk = nsub // DIL_SPAN

    def body(q_ref, kc_ref, kp_ref, vc_ref, vp_ref, o_ref, lse_ref):
        nb = pl.program_id(1)
        kk = jnp.concatenate([kp_ref[0], kc_ref[0]], axis=0)
        vv = jnp.concatenate([vp_ref[0], vc_ref[0]], axis=0)
        s = _dot(_stack_heads(q_ref[0] * ATT_SCALE), kk, "nt")
        ri = lax.broadcasted_iota(jnp.int32, s.shape, 0) % DIL_SPAN
        ci = lax.broadcasted_iota(jnp.int32, s.shape, 1)
        valid = ((ci < DIL_SPAN) & (ci >= ri) & (nb > 0)) | ((ci >= DIL_SPAN) & (ci - DIL_SPAN <= ri))
        s = jnp.where(valid, s, -jnp.inf)
        m = jnp.max(s, axis=1, keepdims=True)
        p = jnp.exp(s - m)
        den = jnp.sum(p, axis=1, keepdims=True)
        o_ref[0] = _unstack_heads(_dot(p, vv, "nn") / den, DIL_SPAN)
        lse = m + jnp.log(den)
        for h in range(4):
            lse_ref[0, :, 128 * h:128 * (h + 1)] = jnp.broadcast_to(lse[DIL_SPAN * h:DIL_SPAN * (h + 1)], (DIL_SPAN, 128))

    blk = (1, DIL_SPAN, GROUP_W)
    sblk = (1, DIL_SPAN, 512)
    prv = lambda nb: jnp.maximum(nb - 1, 0)
    return _pcall(
        body, name=name, grid=(d, nblk),
        in_specs=[pl.BlockSpec(blk, lambda r, nb: (r, nb, 0)),
                  pl.BlockSpec(blk, lambda r, nb: (r, nb, 1)),
                  pl.BlockSpec(blk, lambda r, nb: (r, prv(nb), 1)),
                  pl.BlockSpec(blk, lambda r, nb: (r, nb, 0)),
                  pl.BlockSpec(blk, lambda r, nb: (r, prv(nb), 0))],
        out_specs=[pl.BlockSpec(blk, lambda r, nb: (r, nb, 0)), pl.BlockSpec(sblk, lambda r, nb: (r, nb, 0))],
        out_shape=[jax.ShapeDtypeStruct((d, nsub, GROUP_W), F32), jax.ShapeDtypeStruct((d, nsub, 512), F32)],
        blocks=[(blk, BF16)] * 5 + [(blk, F32), (sblk, F32)],
    )(qk, qk, qk, v, v)


def dil_merge(outs, lses, *, name, tm=512):
    t = outs[0].shape[0] * outs[0].shape[1]
    tm = min(tm, t)

    def body(o0, o1, o2, l0, l1, l2, o_ref, lse_ref, stage):
        ls = [_join_residues(stage, l, d, 0, 512) for l, d in zip((l0, l1, l2), DILATIONS)]
        m = jnp.maximum(jnp.maximum(ls[0], ls[1]), ls[2])
        tot = m + jnp.log(jnp.exp(ls[0] - m) + jnp.exp(ls[1] - m) + jnp.exp(ls[2] - m))
        lse_ref[...] = tot
        lane = lax.broadcasted_iota(jnp.int32, (1, 128), 1)
        first = lane < HEAD_DIM
        acc = jnp.zeros((tm, GROUP_W), F32)
        for og, lg, d in zip((o0, o1, o2), ls, DILATIONS):
            w = jnp.exp(lg - tot)
            wide = jnp.concatenate([jnp.where(first, w[:, 0:128], w[:, 128:256]),
                                    jnp.where(first, w[:, 256:384], w[:, 384:512])], axis=1)
            acc = acc + wide * _join_residues(stage, og, d)
        o_ref[...] = acc

    o_in = [pl.BlockSpec((d, tm // d, GROUP_W), lambda i: (0, i, 0)) for d in DILATIONS]
    l_in = [pl.BlockSpec((d, tm // d, 512), lambda i: (0, i, 0)) for d in DILATIONS]
    return _pcall(
        body, name=name, grid=(t // tm,), in_specs=o_in + l_in,
        out_specs=[pl.BlockSpec((tm, GROUP_W), lambda i: (i, 0)), pl.BlockSpec((tm, 512), lambda i: (i, 0))],
        out_shape=[jax.ShapeDtypeStruct((t, GROUP_W), F32), jax.ShapeDtypeStruct((t, 512), F32)],
        blocks=[((tm, GROUP_W), F32)] * 4 + [((tm, 512), F32)] * 4,
        **_stage(tm),
    )(*outs, *lses)


def dil_bwd_prep(do, o, lse, *, name, tm=512):
    t = do.shape[0]
    tm = min(tm, t)
    wide = DILATIONS[1:]

    def body(do_ref, o_ref, lse_ref, ds_ref, *rest):
        do_out, lse_out, ds_out = rest[0:2], rest[2:4], rest[4:6]
        stage = rest[6]
        dov = do_ref[...]
        prod = dov * o_ref[...]
        for h in range(4):
            s = jnp.sum(jnp.where(_head_mask(h), prod, 0.0), axis=1, keepdims=True)
            ds_ref[:, 128 * h:128 * (h + 1)] = jnp.broadcast_to(s, (tm, 128))
        for i, d in enumerate(wide):
            _split_residues(stage, dov, do_out[i], d, 0, BF16)
            _split_residues(stage, lse_ref[...], lse_out[i], d, 0, F32)
            _split_residues(stage, ds_ref[...], ds_out[i], d, 0, F32)

    nat = lambda w: pl.BlockSpec((tm, w), lambda i: (i, 0))
    res = lambda d, w: pl.BlockSpec((d, tm // d, w), lambda i: (0, i, 0))
    shape = lambda d, w, dt: jax.ShapeDtypeStruct((d, t // d, w), dt)
    outs = _pcall(
        body, name=name, grid=(t // tm,), in_specs=[nat(GROUP_W), nat(GROUP_W), nat(512)],
        out_specs=[nat(512)] + [res(d, GROUP_W) for d in wide] + [res(d, 512) for d in wide] * 2,
        out_shape=([jax.ShapeDtypeStruct((t, 512), F32)] + [shape(d, GROUP_W, BF16) for d in wide]
                   + [shape(d, 512, F32) for d in wide] * 2),
        blocks=[((tm, GROUP_W), F32)] * 3 + [((tm, 512), F32)] * 6,
        **_stage(tm),
    )(do, o, lse)
    return outs[0], outs[1:3], outs[3:5], outs[5:7]


def head_sums(a, b, *, name, round_a=False, tm=512):
    t = a.shape[0]
    tm = min(tm, t)

    def body(a_ref, b_ref, o_ref):
        av = a_ref[...]
        if round_a:
            av = av.astype(BF16).astype(F32)
        prod = av * b_ref[...]
        for h in range(4):
            s = jnp.sum(jnp.where(_head_mask(h), prod, 0.0), axis=1, keepdims=True)
            o_ref[:, 128 * h:128 * (h + 1)] = jnp.broadcast_to(s, (tm, 128))

    spec = pl.BlockSpec((tm, GROUP_W), lambda i: (i, 0))
    return _pcall(
        body, name=name, grid=(t // tm,), in_specs=[spec, spec],
        out_specs=pl.BlockSpec((tm, 512), lambda i: (i, 0)), out_shape=jax.ShapeDtypeStruct((t, 512), F32),
        blocks=[((tm, GROUP_W), F32)] * 2 + [((tm, 512), F32)],
    )(a, b)


def dil_bwd(qk, v, do, lse, dsum, *, name):
    d, nsub, _ = qk.shape
    nblk = nsub // DIL_SPAN

    def body(qa_ref, qb_ref, kc_ref, kp_ref, vc_ref, vp_ref, doa_ref, dob_ref, la_ref, lb_ref, sa_ref, sb_ref,
             dq_ref, dk_ref, dv_ref):
        nb = pl.program_id(1)
        _, band = _band_masks(4)
        nxt = band & (nb < nblk - 1)
        kc, kp, vc, vp = kc_ref[0], kp_ref[0], vc_ref[0], vp_ref[0]
        qas, qbs = _stack_heads(qa_ref[0] * ATT_SCALE), _stack_heads(qb_ref[0] * ATT_SCALE)
        das, dbs = _stack_heads(doa_ref[0].astype(BF16)), _stack_heads(dob_ref[0].astype(BF16))
        stat = lambda ref: jnp.concatenate([ref[0, :, 128 * h:128 * (h + 1)] for h in range(4)], axis=0)
        la, lb, sa, sb = stat(la_ref), stat(lb_ref), stat(sa_ref), stat(sb_ref)

        def probs(qs, ds_, k, v, mask, l, s):
            p = jnp.where(mask, jnp.exp(_dot(qs, k, "nt") - l), 0.0)
            dsc = p * (_dot(ds_, v, "nt") - s)
            return p.astype(BF16), dsc.astype(BF16)

        wide = lambda a: jnp.concatenate([a, a], axis=1)
        ri = lax.broadcasted_iota(jnp.int32, (4 * DIL_SPAN, 2 * DIL_SPAN), 0) % DIL_SPAN
        ci = lax.broadcasted_iota(jnp.int32, (4 * DIL_SPAN, 2 * DIL_SPAN), 1)
        valid = ((ci < DIL_SPAN) & (ci >= ri) & (nb > 0)) | ((ci >= DIL_SPAN) & (ci - DIL_SPAN <= ri))
        p_a, ds_a = probs(qas, das, jnp.concatenate([kp, kc], axis=0), jnp.concatenate([vp, vc], axis=0),
                          valid, wide(la), wide(sa))
        p_nc, ds_nc = probs(qbs, dbs, kc, vc, nxt, lb, sb)
        dq_ref[0] = _unstack_heads(_dot(ds_a, jnp.concatenate([kp, kc], axis=0), "nn"), DIL_SPAN) * ATT_SCALE
        dk_ref[0] = _dot(ds_a[:, DIL_SPAN:], qas, "tn") + _dot(ds_nc, qbs, "tn")
        dv_ref[0] = _dot(p_a[:, DIL_SPAN:], das, "tn") + _dot(p_nc, dbs, "tn")

    blk = (1, DIL_SPAN, GROUP_W)
    sblk = (1, DIL_SPAN, 512)
    prv = lambda nb: jnp.maximum(nb - 1, 0)
    nxt_ = lambda nb: jnp.minimum(nb + 1, nblk - 1)
    cur_at = lambda c: pl.BlockSpec(blk, functools.partial(lambda r, nb, c: (r, nb, c), c=c))
    prv_at = lambda c: pl.BlockSpec(blk, functools.partial(lambda r, nb, c: (r, prv(nb), c), c=c))
    nxt_at = lambda c: pl.BlockSpec(blk, functools.partial(lambda r, nb, c: (r, nxt_(nb), c), c=c))
    s_cur = pl.BlockSpec(sblk, lambda r, nb: (r, nb, 0))
    s_nxt = pl.BlockSpec(sblk, lambda r, nb: (r, nxt_(nb), 0))
    o_spec = pl.BlockSpec(blk, lambda r, nb: (r, nb, 0))
    o_shape = jax.ShapeDtypeStruct((d, nsub, GROUP_W), F32)
    return _pcall(
        body, name=name, grid=(d, nblk),
        in_specs=[cur_at(0), nxt_at(0), cur_at(1), prv_at(1), cur_at(0), prv_at(0), cur_at(0), nxt_at(0),
                  s_cur, s_nxt, s_cur, s_nxt],
        out_specs=[o_spec, o_spec, o_spec], out_shape=[o_shape, o_shape, o_shape],
        blocks=[(blk, BF16)] * 6 + [(blk, F32)] * 5 + [(sblk, F32)] * 4,
    )(qk, qk, qk, qk, v, v, do, do, lse, lse, dsum, dsum)


def _tri_dot(x, b):
    hi = x.astype(BF16)
    lo = (x - hi.astype(F32)).astype(BF16)
    return _dot(jnp.concatenate([hi, lo], axis=1), jnp.concatenate([b, b], axis=0), "nn")


SB_TILE = 256
SB_ROWS = 512


def _stack_heads(a):
    return jnp.concatenate([jnp.where(_head_mask(h), a, jnp.zeros_like(a)) for h in range(4)], axis=0)


def _unstack_heads(acc, rows):
    out = acc[0:rows]
    for h in range(1, 4):
        out = jnp.where(_head_mask(h), acc[h * rows:(h + 1) * rows], out)
    return out


def _tri_masks(n):
    ri = lax.broadcasted_iota(jnp.int32, (n, n), 0)
    ci = lax.broadcasted_iota(jnp.int32, (n, n), 1)
    return (ri > ci).astype(BF16), (ri >= ci).astype(BF16)


def _sb_weights(qs, kt, after, c_keep, lead):
    z = _dot(qs, kt, "nt")
    lbeta = jnp.minimum(z, 0.0) - jnp.log(1.0 + jnp.exp(-jnp.abs(z)))
    lkeep = lbeta - z
    past = None
    if lead is not None:
        query = lax.broadcasted_iota(jnp.int32, z.shape, 0) % SB_ROWS
        past = lax.broadcasted_iota(jnp.int32, z.shape, 1) + lead < query
        lkeep = jnp.where(past, lkeep, 0.0)
    w = jnp.exp(lbeta + _tri_dot(lkeep, after) + c_keep)
    if lead is not None:
        w = jnp.where(past, w, 0.0)
    return z, past, lbeta, lkeep, w


def _sb_walk(qb, tile, carry):
    per = SB_ROWS // SB_TILE
    for i in reversed(range(per)):
        carry = tile(pl.multiple_of(qb * SB_ROWS + i * SB_TILE, SB_TILE), i * SB_TILE, i == per - 1, carry)
    past_tiles = qb * per
    return lax.fori_loop(0, past_tiles,
                         lambda it, c: tile(pl.multiple_of((past_tiles - 1 - it) * SB_TILE, SB_TILE), None, False, c), carry)


def sb_fwd(proj, *, name):
    t = proj.shape[0]
    n, m = SB_TILE, SB_ROWS
    assert t % m == 0

    def body(q_ref, k_ref, v_ref, o_ref, acc_ref):
        qb = pl.program_id(0)
        qs = _stack_heads(q_ref[...] * ATT_SCALE)
        after, _ = _tri_masks(n)

        def tile(off, lead, first, c_keep):
            kt = k_ref[pl.ds(off, n), :]
            vt = v_ref[pl.ds(off, n), :]
            _, _, _, lkeep, w = _sb_weights(qs, kt, after, c_keep, lead)
            pv = _tri_dot(w, vt)
            if first:
                acc_ref[...] = pv
            else:
                acc_ref[...] += pv
            return c_keep + jnp.sum(lkeep, axis=1, keepdims=True)

        _sb_walk(qb, tile, jnp.zeros((4 * m, 1), F32))
        o_ref[...] = _unstack_heads(acc_ref[...], m)

    full = lambda cb: pl.BlockSpec((t, GROUP_W), functools.partial(lambda i, cb: (0, cb), cb=cb))
    return _pcall(
        body, name=name, grid=(t // m,),
        in_specs=[pl.BlockSpec((m, GROUP_W), lambda i: (i, QS_BLK)), full(KS_BLK), full(VS_BLK)],
        out_specs=pl.BlockSpec((m, GROUP_W), lambda i: (i, 0)), out_shape=jax.ShapeDtypeStruct((t, GROUP_W), F32),
        blocks=[((m, GROUP_W), BF16), ((t, GROUP_W), BF16), ((t, GROUP_W), BF16), ((m, GROUP_W), F32)],
        scratch_shapes=[pltpu.VMEM((4 * m, GROUP_W), F32)], scratch_bytes=4 * m * GROUP_W * 4,
    )(proj, proj, proj)


def sb_bwd(proj, do, gtot, *, name):
    t = proj.shape[0]
    n, m = SB_TILE, SB_ROWS
    assert t % m == 0

    def body(q_ref, k_ref, v_ref, do_ref, gt_ref, dq_ref, dk_ref, dv_ref, acc_ref):
        qb = pl.program_id(0)

        @pl.when(qb == 0)
        def _():
            dk_ref[...] = jnp.zeros_like(dk_ref)
            dv_ref[...] = jnp.zeros_like(dv_ref)

        qs = _stack_heads(q_ref[...] * ATT_SCALE)
        dos = _stack_heads(do_ref[...].astype(BF16))
        gt = jnp.concatenate([jnp.max(gt_ref[:, 128 * h:128 * (h + 1)], axis=1, keepdims=True) for h in range(4)], axis=0)
        after, from_on = _tri_masks(n)

        def tile(off, lead, first, carry):
            c_keep, c_g = carry
            kt = k_ref[pl.ds(off, n), :]
            vt = v_ref[pl.ds(off, n), :]
            z, past, lbeta, lkeep, w = _sb_weights(qs, kt, after, c_keep, lead)
            gw = w * _dot(dos, vt, "nt")
            big_g = gt - (_tri_dot(gw, from_on) + c_g)
            dz = gw * jnp.exp(lbeta - z) - big_g * jnp.exp(lbeta)
            if lead is not None:
                dz = jnp.where(past, dz, 0.0)
            dz = dz.astype(BF16)
            dk_ref[pl.ds(off, n), :] += _dot(dz, qs, "tn")
            dv_ref[pl.ds(off, n), :] += _dot(w, dos, "tn")
            dq = _dot(dz, kt, "nn")
            if first:
                acc_ref[...] = dq
            else:
                acc_ref[...] += dq
            return c_keep + jnp.sum(lkeep, axis=1, keepdims=True), c_g + jnp.sum(gw, axis=1, keepdims=True)

        zero_col = jnp.zeros((4 * m, 1), F32)
        _sb_walk(qb, tile, (zero_col, zero_col))
        dq_ref[...] = _unstack_heads(acc_ref[...], m) * ATT_SCALE

    full = lambda cb: pl.BlockSpec((t, GROUP_W), functools.partial(lambda i, cb: (0, cb), cb=cb))
    whole = pl.BlockSpec((t, GROUP_W), lambda i: (0, 0))
    rowblk = pl.BlockSpec((m, GROUP_W), lambda i: (i, 0))
    shape = jax.ShapeDtypeStruct((t, GROUP_W), F32)
    return _pcall(
        body, name=name, grid=(t // m,),
        in_specs=[pl.BlockSpec((m, GROUP_W), lambda i: (i, QS_BLK)), full(KS_BLK), full(VS_BLK), rowblk,
                  pl.BlockSpec((m, 512), lambda i: (i, 0))],
        out_specs=[rowblk, whole, whole], out_shape=[shape, shape, shape],
        blocks=[((m, GROUP_W), BF16), ((t, GROUP_W), BF16), ((t, GROUP_W), BF16), ((m, GROUP_W), F32),
                ((m, 512), F32), ((m, GROUP_W), F32), ((t, GROUP_W), F32), ((t, GROUP_W), F32)],
        scratch_shapes=[pltpu.VMEM((4 * m, GROUP_W), F32)], scratch_bytes=4 * m * GROUP_W * 4,
    )(proj, proj, proj, do, gtot)


def _mesh_place():
    return lax.axis_index("x"), lax.axis_index("y"), lax.axis_index("c")


def _flip(place, mask):
    x, y, c = place
    return ((1 - x) if mask & 4 else x, (1 - y) if mask & 2 else y, (1 - c) if mask & 1 else c)


def _dev_index(place):
    x, y, c = place
    return 4 * x + 2 * y + c


HBM_SPEC = pl.BlockSpec(memory_space=pltpu.HBM)


def all_gather_rows(shard, after, *, name):
    rows, lanes = shard.shape

    def body(x_ref, after_ref, out_ref, send_sems, recv_sems, local_sem):
        me = _mesh_place()
        x, y, c = me
        sibling = _flip(me, 1)
        chips = [_flip(me, 4), _flip(me, 2), _flip(me, 6)]

        def copy(k, block, to, src=None):
            dst = out_ref.at[_dev_index(block)]
            return pltpu.make_async_remote_copy(
                src_ref=dst if src is None else src, dst_ref=dst, send_sem=send_sems.at[k], recv_sem=recv_sems.at[k],
                device_id=to, device_id_type=pl.DeviceIdType.MESH)

        mine = pltpu.make_async_copy(x_ref, out_ref.at[_dev_index(me)], local_sem)
        mine.start()
        first = [copy(0, me, sibling, src=x_ref)] + [copy(1 + j, me, chip, src=x_ref) for j, chip in enumerate(chips)]
        for cp in first:
            cp.start()
        passed = [copy(4 + j, chip, sibling) for j, chip in enumerate(chips)]
        for j, chip in enumerate(chips):
            copy(1 + j, chip, me).wait_recv()
            passed[j].start()
        copy(0, sibling, me).wait_recv()
        for j, chip in enumerate(chips):
            copy(4 + j, _flip(chip, 1), me).wait_recv()
        for cp in first + passed:
            cp.wait_send()
        mine.wait()

    return pl.pallas_call(
        body, name=name, in_specs=[HBM_SPEC, pl.BlockSpec(memory_space=pl.ANY)], out_specs=HBM_SPEC,
        out_shape=jax.ShapeDtypeStruct((N_DEV, rows, lanes), shard.dtype),
        scratch_shapes=[pltpu.SemaphoreType.DMA((7,)), pltpu.SemaphoreType.DMA((7,)), pltpu.SemaphoreType.DMA],
    )(shard, after)


SEM_SPEC = pl.BlockSpec(memory_space=pltpu.SEMAPHORE)
DATAFLOW_EFFECT = pltpu.SideEffectType.DATAFLOW_SIDE_EFFECTING


ALL_PEERS = tuple(range(1, N_DEV))
CHIP_PEERS = (1, 4, 2, 6)
OTHER_CHIPS = (4, 2, 6)


def _spread_copies(src_refs, land_refs, send_sems, recv_sems, per_peer, masks, arriving):
    me = _mesh_place()
    my = _dev_index(me)
    remote, local = [], []
    for t, (src_ref, land_ref) in enumerate(zip(src_refs, land_refs)):
        for i, mask in enumerate(masks):
            peer = _flip(me, mask)
            data_of = my if arriving else _dev_index(peer)
            slot = _dev_index(peer) if arriving else my
            k = t * len(masks) + i
            remote.append(pltpu.make_async_remote_copy(
                src_ref=src_ref.at[data_of] if per_peer else src_ref, dst_ref=land_ref.at[slot],
                send_sem=send_sems.at[k], recv_sem=recv_sems.at[k],
                device_id=peer, device_id_type=pl.DeviceIdType.MESH))
        local.append(pltpu.make_async_copy(src_ref.at[my] if per_peer else src_ref, land_ref.at[my],
                                           send_sems.at[len(src_refs) * len(masks) + t]))
    return remote, local


def spread_start(srcs, *, per_peer, name, masks=ALL_PEERS):
    nt = len(srcs)
    zones = [pltpu.HBM((N_DEV,) + (s.shape[1:] if per_peer else s.shape), s.dtype) for s in srcs]

    def body(*refs):
        src_refs, (send_sems, recv_sems) = refs[:nt], refs[nt:nt + 2]
        land_refs, token = refs[2 * nt + 2:3 * nt + 2], refs[3 * nt + 2]
        remote, local = _spread_copies(src_refs, land_refs, send_sems, recv_sems, per_peer, masks, arriving=False)
        for cp in remote + local:
            cp.start()
        token[...] = jnp.zeros_like(token)

    return pl.pallas_call(
        body, name=name, in_specs=(HBM_SPEC,) * nt,
        out_shape=(pltpu.SemaphoreType.DMA((nt * len(masks) + nt,)), pltpu.SemaphoreType.DMA((nt * len(masks),)),
                   *[pltpu.HBM(s.shape, s.dtype) for s in srcs], *zones, jax.ShapeDtypeStruct((8, 128), F32)),
        out_specs=(SEM_SPEC, SEM_SPEC) + (HBM_SPEC,) * (2 * nt) + (pl.BlockSpec(memory_space=pltpu.VMEM),),
        input_output_aliases={t: 2 + t for t in range(nt)},
        compiler_params=pltpu.CompilerParams(has_side_effects=DATAFLOW_EFFECT),
    )(*[pltpu.with_memory_space_constraint(s, pltpu.HBM) for s in srcs])


def spread_wait(started, after, *, per_peer, name, masks=ALL_PEERS):
    nt = (len(started) - 3) // 2
    send_sems, recv_sems = started[0:2]
    srcs_thru, lands_thru = started[2:2 + nt], started[2 + nt:2 + 2 * nt]

    def body(*refs):
        src_refs, land_refs = refs[:nt], refs[nt:2 * nt]
        send_sems, recv_sems = refs[2 * nt:2 * nt + 2]
        remote, local = _spread_copies(src_refs, land_refs, send_sems, recv_sems, per_peer, masks, arriving=True)
        for cp in remote:
            cp.wait_send()
            cp.wait_recv()
        for cp in local:
            cp.wait()

    outs = pl.pallas_call(
        body, name=name, in_specs=(HBM_SPEC,) * (2 * nt) + (SEM_SPEC, SEM_SPEC, pl.BlockSpec(memory_space=pl.ANY)),
        out_shape=tuple(pltpu.HBM(a.shape, a.dtype) for a in (*srcs_thru, *lands_thru)),
        out_specs=(HBM_SPEC,) * (2 * nt), input_output_aliases={t: t for t in range(2 * nt)},
        compiler_params=pltpu.CompilerParams(has_side_effects=DATAFLOW_EFFECT),
    )(*srcs_thru, *lands_thru, send_sems, recv_sems, after)
    return list(outs[nt:])


def _relay_copies(land_refs, send_sems, recv_sems, arriving):
    me = _mesh_place()
    sibling = _flip(me, 1)
    out = []
    for t, land_ref in enumerate(land_refs):
        for i, mask in enumerate(OTHER_CHIPS):
            slot = _dev_index(_flip(sibling if arriving else me, mask))
            k = t * len(OTHER_CHIPS) + i
            out.append(pltpu.make_async_remote_copy(
                src_ref=land_ref.at[slot], dst_ref=land_ref.at[slot], send_sem=send_sems.at[k], recv_sem=recv_sems.at[k],
                device_id=sibling, device_id_type=pl.DeviceIdType.MESH))
    return out


def relay_start(lands, *, name):
    nt = len(lands)
    n_sem = nt * len(OTHER_CHIPS)

    def body(*refs):
        for cp in _relay_copies(refs[:nt], refs[nt], refs[nt + 1], arriving=False):
            cp.start()

    return pl.pallas_call(
        body, name=name, in_specs=(HBM_SPEC,) * nt,
        out_shape=(pltpu.SemaphoreType.DMA((n_sem,)), pltpu.SemaphoreType.DMA((n_sem,)),
                   *[pltpu.HBM(a.shape, a.dtype) for a in lands]),
        out_specs=(SEM_SPEC, SEM_SPEC) + (HBM_SPEC,) * nt, input_output_aliases={t: 2 + t for t in range(nt)},
        compiler_params=pltpu.CompilerParams(has_side_effects=DATAFLOW_EFFECT),
    )(*[pltpu.with_memory_space_constraint(a, pltpu.HBM) for a in lands])


def relay_wait(started, *, name):
    send_sems, recv_sems = started[0:2]
    lands_thru = started[2:]
    nt = len(lands_thru)

    def body(*refs):
        for cp in _relay_copies(refs[:nt], refs[nt], refs[nt + 1], arriving=True):
            cp.wait_send()
            cp.wait_recv()

    return list(pl.pallas_call(
        body, name=name, in_specs=(HBM_SPEC,) * nt + (SEM_SPEC, SEM_SPEC),
        out_shape=tuple(pltpu.HBM(a.shape, a.dtype) for a in lands_thru), out_specs=(HBM_SPEC,) * nt,
        input_output_aliases={t: t for t in range(nt)},
        compiler_params=pltpu.CompilerParams(has_side_effects=DATAFLOW_EFFECT),
    )(*lands_thru, send_sems, recv_sems))


def sum_partials(parts, *, name, tr):
    _, rows, lanes = parts.shape
    assert rows % tr == 0

    def body(p_ref, g_ref):
        g = p_ref[0].astype(F32)
        for k in range(1, N_DEV):
            g = g + p_ref[k].astype(F32)
        g_ref[...] = g

    return _pcall(
        body, name=name, grid=(rows // tr,),
        in_specs=[pl.BlockSpec((N_DEV, tr, lanes), lambda i: (0, i, 0))],
        out_specs=pl.BlockSpec((tr, lanes), lambda i: (i, 0)), out_shape=jax.ShapeDtypeStruct((rows, lanes), F32),
        blocks=[((N_DEV, tr, lanes), parts.dtype), ((tr, lanes), F32)],
    )(parts)


def adamw(g, w, m, v, *, name, tr):
    nl, k, n = w.shape
    tr = max(c for c in range(8, min(tr, k) + 1, 8) if k % c == 0)
    bc1 = 1.0 - ADAM_B1 ** ADAM_STEP
    bc2 = 1.0 - ADAM_B2 ** ADAM_STEP

    def body(g_ref, w_ref, m_ref, v_ref, d_ref, mo_ref, vo_ref):
        gv = g_ref[...]
        m_new = ADAM_B1 * m_ref[...] + (1.0 - ADAM_B1) * gv
        v_new = ADAM_B2 * v_ref[...] + (1.0 - ADAM_B2) * (gv * gv)
        mo_ref[...] = m_new
        vo_ref[...] = v_new
        d_ref[...] = -ADAM_LR * ((m_new / bc1) / (jnp.sqrt(v_new / bc2) + ADAM_EPS) + ADAM_WD * w_ref[...])

    spec = pl.BlockSpec((1, tr, n), lambda l, i: (l, i, 0))
    shape = jax.ShapeDtypeStruct(w.shape, F32)
    return _pcall(
        body, name=name, grid=(nl, k // tr), in_specs=[spec] * 4, out_specs=[spec] * 3, out_shape=[shape] * 3,
        blocks=[((1, tr, n), F32)] * 7,
    )(g, w, m, v)


def sum_adamw(partials, w, m, v, *, name, tr):
    nl, k, n = w.shape
    assert nl == len(partials) == 2
    tr = max(c for c in range(8, min(tr, k) + 1, 8) if k % c == 0)
    bc1 = 1.0 - ADAM_B1 ** ADAM_STEP
    bc2 = 1.0 - ADAM_B2 ** ADAM_STEP

    def body(p0_ref, p1_ref, w_ref, m_ref, v_ref, g_ref, d_ref, mo_ref, vo_ref):
        first = pl.program_id(0) == 0
        gv = jnp.where(first, p0_ref[0], p1_ref[0]).astype(F32)
        for s in range(1, N_DEV):
            gv = gv + jnp.where(first, p0_ref[s], p1_ref[s]).astype(F32)
        m_new = ADAM_B1 * m_ref[0] + (1.0 - ADAM_B1) * gv
        v_new = ADAM_B2 * v_ref[0] + (1.0 - ADAM_B2) * (gv * gv)
        g_ref[0] = gv
        mo_ref[0] = m_new
        vo_ref[0] = v_new
        d_ref[0] = -ADAM_LR * ((m_new / bc1) / (jnp.sqrt(v_new / bc2) + ADAM_EPS) + ADAM_WD * w_ref[0])

    spec = pl.BlockSpec((1, tr, n), lambda l, i: (l, i, 0))
    p0spec = pl.BlockSpec((N_DEV, tr, n), lambda l, i: (0, i * (1 - l), 0))
    p1spec = pl.BlockSpec((N_DEV, tr, n), lambda l, i: (0, i * l, 0))
    shape = jax.ShapeDtypeStruct(w.shape, F32)
    return _pcall(
        body, name=name, grid=(nl, k // tr), in_specs=[p0spec, p1spec, spec, spec, spec], out_specs=[spec] * 4,
        out_shape=[shape] * 4, blocks=[((N_DEV, tr, n), BF16)] * 2 + [((1, tr, n), F32)] * 7,
    )(partials[0], partials[1], w, m, v)


def travelling(a, by_cols):
    return jnp.swapaxes(a, -1, -2) if by_cols else a


def _row(v):
    return v.reshape(1, -1)


def ffn_fwd(x, h, w, pre, tag, next_gain):
    ga, gb, s = swiglu_fwd(h, w[pre + "_w_gate"], w[pre + "_w_up"], name=f"{tag}_gateup")
    if callable(w[pre + "_w_down"]):
        w[pre + "_w_down"] = w[pre + "_w_down"](s)
    out, h_next = matmul_res_norm(s, w[pre + "_w_down"], x, next_gain, scale=0.5, tm=512, name=f"{tag}_down")
    return out, h_next, (x, h, ga, gb, s)


def ffn_bwd_weights(dxb, saved, w, pre, tag):
    x, h, a, b, s = saved
    da, db = swiglu_bwd(dxb, w[pre + "_w_down"], a, b, scale=0.5, name=f"{tag}_dgateup")
    g_down = matmul([(s, dxb)], "tn", tm=1408, tn=1024, tk=2048, out_dtype=BF16, scale=0.5, name=f"{tag}_gdown")
    g_gate = matmul([(da, h)], "tn", tm=1408, tn=1024, tk=2048, out_dtype=BF16, name=f"{tag}_ggate")
    g_up = matmul([(db, h)], "tn", tm=1408, tn=1024, tk=2048, out_dtype=BF16, name=f"{tag}_gup")
    return {pre + "_w_gate": g_gate, pre + "_w_up": g_up, pre + "_w_down": g_down}, (da, db)


def ffn_bwd_input(dx, rest, saved, gain, w, pre, tag):
    da, db = rest
    x = saved[0]
    return matmul_rms_bwd([(da, w[pre + "_w_gate"]), (db, w[pre + "_w_up"])], x, gain, dx, tm=256, name=f"{tag}_dh")


def mixer_fwd(x, h, w, tables, tag, next_gain):
    proj = matmul([(h, w["w_in"])], "nt", tm=512, tn=1280, tk=1024, out_dtype=BF16, name=f"{tag}_in")
    qks, vs = rope_split(proj, tables, name=f"{tag}_rope")
    outs, lses = [], []
    for g in range(N_DIL_GROUPS):
        o, lse = dil_fwd(qks[g], vs[g], name=f"{tag}_dil{g}")
        outs.append(o)
        lses.append(lse)
    odil, lse = dil_merge(outs, lses, name=f"{tag}_merge")
    osb = sb_fwd(proj, name=f"{tag}_sb")
    y, u1, u2 = gate_fwd(odil, osb, w["w_proj_dil"], w["w_proj_sb"], proj, name=f"{tag}_gate")
    out, h_next = matmul_res_norm(y, w["w_out"], x, next_gain, scale=1.0, tm=512, name=f"{tag}_out")
    return out, h_next, (x, h, proj, qks, vs, odil, lse, osb, u1, u2, y)


def mixer_bwd_weights(dxb, saved, w, tables, tag):
    x, h, proj, qks, vs, odil, lse, osb, u1, u2, y = saved
    t = x.shape[0]
    g_out = matmul([(y, dxb)], "tn", tm=1024, tn=1024, tk=2048, out_dtype=BF16, name=f"{tag}_gout")
    du1, du2, dgate = gate_bwd(dxb, w["w_out"], u1, u2, proj, name=f"{tag}_dgate")
    g_pd = matmul([(du1, odil)], "tn", tm=1024, tn=256, tk=2048, out_dtype=BF16, name=f"{tag}_gpd")
    g_ps = matmul([(du2, osb)], "tn", tm=1024, tn=256, tk=2048, out_dtype=BF16, name=f"{tag}_gps")
    dodil = matmul([(du1, w["w_proj_dil"])], "nn", tm=512, tn=256, tk=1024, out_dtype=F32, name=f"{tag}_dodil")
    dosb = matmul([(du2, w["w_proj_sb"])], "nn", tm=512, tn=256, tk=1024, out_dtype=F32, name=f"{tag}_dosb")
    dsum, do_wide, lse_wide, dsum_wide = dil_bwd_prep(dodil, odil, lse, name=f"{tag}_dprep")
    dos = [dodil[None]] + list(do_wide)
    lss = [lse[None]] + list(lse_wide)
    dss = [dsum[None]] + list(dsum_wide)
    dqs, dks, dvs = [], [], []
    for g in range(N_DIL_GROUPS):
        dq, dk, dv = dil_bwd(qks[g], vs[g], dos[g], lss[g], dss[g], name=f"{tag}_ddil{g}")
        dqs.append(dq)
        dks.append(dk)
        dvs.append(dv)
    gtot = head_sums(dosb, osb, round_a=True, name=f"{tag}_gsum")
    sb_grads = sb_bwd(proj, dosb, gtot, name=f"{tag}_dsb")
    dproj = rope_join(dqs, dks, dvs, sb_grads, dgate, tables, name=f"{tag}_drope")
    g_in = matmul([(dproj, h)], "tn", tm=1280, tn=1024, tk=2048, out_dtype=BF16, name=f"{tag}_gin")
    return {"w_in": g_in, "w_proj_dil": g_pd, "w_proj_sb": g_ps, "w_out": g_out}, dproj


def mixer_bwd_input(dx, dproj, saved, gain, w, tag):
    x = saved[0]
    return matmul_rms_bwd([(dproj, w["w_in"])], x, gain, dx, tm=256, name=f"{tag}_dh")


def kernel(x, norm_ffn1, ffn1_w_gate, ffn1_w_up, ffn1_w_down, norm_mix, w_in, w_proj_dil, w_proj_sb, w_out, norm_ffn2, ffn2_w_gate, ffn2_w_up, ffn2_w_down, norm_final, loss_target, m_norm_ffn1, m_ffn1_w_gate, m_ffn1_w_up, m_ffn1_w_down, m_norm_mix, m_w_in, m_w_proj_dil, m_w_proj_sb, m_w_out, m_norm_ffn2, m_ffn2_w_gate, m_ffn2_w_up, m_ffn2_w_down, m_norm_final, v_norm_ffn1, v_ffn1_w_gate, v_ffn1_w_up, v_ffn1_w_down, v_norm_mix, v_w_in, v_w_proj_dil, v_w_proj_sb, v_w_out, v_norm_ffn2, v_ffn2_w_gate, v_ffn2_w_up, v_ffn2_w_down, v_norm_final):
    args = dict(locals())
    t = x.shape[1]
    xs = x.reshape(t, D_MODEL)
    target = loss_target.reshape(t, D_MODEL)
    tables = rope_tables(t)

    parts = [(l, p) for l in range(2) for p in SUBBLOCKS]
    gains = {n: args[n] for n in NORM_ROWS}

    shipments = []
    for l, p in parts:
        if (l, p) == parts[0]:
            shipments += [(l, p, SUBBLOCKS[p][:2], CHIP_PEERS), (l, p, SUBBLOCKS[p][2:], ALL_PEERS)]
        else:
            shipments.append((l, p, SUBBLOCKS[p], ALL_PEERS))
    in_flight, order_token = [], jnp.zeros((1, 1), F32)
    for l, p, tensors, masks in shipments:
        shards = [travelling(args[n][l], by_cols).astype(BF16) for n, _, by_cols, _ in tensors]
        shards[0] = shards[0] + order_token.astype(BF16)
        in_flight.append(spread_start(shards, per_peer=False, masks=masks, name=f"gather_start_l{l}_{tensors[0][0]}"))
        order_token = in_flight[-1][-1][0:1, 0:1]

    def arrived(i, after):
        l, p, tensors, masks = shipments[i]
        tag = f"l{l}_{tensors[0][0]}"
        lands = spread_wait(in_flight[i], after, per_peer=False, masks=masks, name=f"gather_wait_{tag}")
        if masks is CHIP_PEERS:
            lands = relay_wait(relay_start(lands, name=f"gather_relay_{tag}"), name=f"gather_relayed_{tag}")
        return {n: land.reshape(-1, land.shape[-1]) for (n, _, _, _), land in zip(tensors, lands)}

    def weights_of(l, p, after):
        mine = [i for i, s in enumerate(shipments) if s[0:2] == (l, p)]
        w = arrived(mine[0], after)
        for i in mine[1:]:
            for n, _, _, _ in shipments[i][2]:
                w[n] = functools.partial(lambda after, i, n: arrived(i, after)[n], i=i, n=n)
        return w

    saved, weights = {}, {}
    act = xs
    h = rms_fwd(xs, _row(gains["norm_ffn1"][0]) + order_token, name="l0_ffn1_norm")
    for i, (l, p) in enumerate(parts):
        weights[(l, p)] = weights_of(l, p, h if i == 0 else act)
        nl, np_ = parts[i + 1] if i + 1 < len(parts) else (None, None)
        next_gain = _row(gains["norm_" + np_][nl]) if np_ else None
        if p == "mix":
            act, h, saved[(l, p)] = mixer_fwd(act, h, weights[(l, p)], tables, f"l{l}_mix", next_gain)
        else:
            act, h, saved[(l, p)] = ffn_fwd(act, h, weights[(l, p)], p, f"l{l}_{p}", next_gain)
    dx, dxb, g_final, loss_part = final_loss(act, _row(norm_final), target, name="loss_head")

    gain_grads, sent = {}, {}
    order_token = jnp.zeros((1, 1), F32)
    for l, p in reversed(parts):
        w, sv = weights[(l, p)], saved[(l, p)]
        if p == "mix":
            gw, rest = mixer_bwd_weights(dxb, sv, w, tables, f"l{l}_mix")
        else:
            gw, rest = ffn_bwd_weights(dxb, sv, w, p, f"l{l}_{p}")
        slices = [gw[n].reshape(N_DEV, -1, gw[n].shape[-1]) for n, _, _, _ in SUBBLOCKS[p]]
        sent[(l, p)] = spread_start(slices, per_peer=True, name=f"reduce_start_l{l}_{p}")
        gain = _row(gains["norm_" + p][l]) + sent[(l, p)][-1][0:1, 0:1]
        if p == "mix":
            dx, dxb, gain_grads[("norm_mix", l)] = mixer_bwd_input(dx, rest, sv, gain, w, f"l{l}_mix")
        else:
            dx, dxb, gain_grads[("norm_" + p, l)] = ffn_bwd_input(dx, rest, sv, gain, w, p, f"l{l}_{p}")

    partials, big_all = {}, [{}, {}, {}, {}]

    def receive(l, p, after):
        lands = spread_wait(sent[(l, p)], after, per_peer=True, name=f"reduce_wait_l{l}_{p}")
        for (n, _, _, _), land in zip(SUBBLOCKS[p], lands):
            partials.setdefault(n, [None, None])[l] = land

    def update(p):
        for n, _, by_cols, _ in SUBBLOCKS[p]:
            outs = sum_adamw(partials[n], travelling(args[n], by_cols), travelling(args["m_" + n], by_cols),
                             travelling(args["v_" + n], by_cols), tr=256, name=f"update_{n}")
            for kind, arr in enumerate(outs):
                big_all[kind][n] = travelling(arr, by_cols)
        return outs[1]

    for l, p in reversed(parts[1:]):
        receive(l, p, dx)
    update("ffn2")
    done = update("mix")
    receive(*parts[0], done)
    done = update("ffn1")

    loss_row = jnp.pad(loss_part[:, :1], ((0, 0), (0, LANES - 1)))
    small = jnp.concatenate([gain_grads[(n, l)] for n in NORM_ROWS for l in range(2)] + [g_final, loss_row], axis=0)
    small_g = sum_partials(all_gather_rows(small, done, name="gather_gain_grads"), tr=8, name="sum_gain_grads")
    zero_row = jnp.zeros((1, LANES), F32)
    small_of = lambda pre: jnp.concatenate([args[pre + n] for n in NORM_ROWS] + [_row(args[pre + "norm_final"]), zero_row], axis=0)[None]
    small_out = adamw(small_g[None], small_of(""), small_of("m_"), small_of("v_"), tr=8, name="update_gains")
    small_all = [small_g] + [o[0] for o in small_out]

    def gains_of(s):
        out = {n: s[2 * i:2 * i + 2] for i, n in enumerate(NORM_ROWS)}
        out["norm_final"] = s[6]
        return out

    order = ["norm_ffn1", "ffn1_w_gate", "ffn1_w_up", "ffn1_w_down", "norm_mix", "w_in", "w_proj_dil", "w_proj_sb", "w_out",
             "norm_ffn2", "ffn2_w_gate", "ffn2_w_up", "ffn2_w_down", "norm_final"]
    results = []
    for kind in range(4):
        both = {**big_all[kind], **gains_of(small_all[kind])}
        results += [both[n] for n in order]
    loss = small_g[7, 0]
    return (loss, dx.reshape(1, t, D_MODEL), *results)
```

```python
import functools

import jax
import jax.numpy as jnp
from jax import lax
from jax.experimental import pallas as pl
from jax.experimental.pallas import tpu as pltpu

F32 = jnp.float32
BF16 = jnp.bfloat16

D_MODEL = 1024
HEAD_DIM = 64
GROUP_W = 256
D_IN = 5120
N_DIL_GROUPS = 3
DIL_SPAN = 128
DILATIONS = (1, 4, 16)
ROPE_THETA = 500000.0
ROPE_DIM = 16
RMS_EPS = 1e-6
ATT_SCALE = HEAD_DIM ** -0.5
QS_BLK, KS_BLK, VS_BLK = 9, 10, 11
GATE_DIL_BLK, GATE_SB_BLK = 3, 4

ADAM_LR, ADAM_B1, ADAM_B2, ADAM_EPS, ADAM_WD, ADAM_STEP = 0.001, 0.9, 0.999, 1e-08, 0.01, 10

N_DEV = 8
VMEM_PHYSICAL_V7X = 64 << 20
VMEM_TEMP_HEADROOM = 20 << 20

SUBBLOCKS = {
    "ffn1": (("ffn1_w_gate", True), ("ffn1_w_up", True), ("ffn1_w_down", False)),
    "mix": (("w_in", True), ("w_proj_dil", True), ("w_proj_sb", True), ("w_out", False)),
    "ffn2": (("ffn2_w_gate", True), ("ffn2_w_up", True), ("ffn2_w_down", False)),
}
NORM_ROWS = ("norm_ffn1", "norm_mix", "norm_ffn2")


def _nbytes(shape, dtype):
    n = 1
    for s in shape:
        n *= s
    return n * jnp.dtype(dtype).itemsize


def _pcall(body, *, name, grid, in_specs, out_specs, out_shape, blocks, scratch_shapes=(), scratch_bytes=0):
    need = 2 * sum(_nbytes(s, d) for s, d in blocks) + scratch_bytes + VMEM_TEMP_HEADROOM
    limit = min(need, VMEM_PHYSICAL_V7X - (4 << 20))
    in_hbm = lambda s: pltpu.HBM(s.shape, s.dtype)
    out_shape = [in_hbm(s) for s in out_shape] if isinstance(out_shape, (list, tuple)) else in_hbm(out_shape)
    call = pl.pallas_call(
        body, name=name, grid=grid, in_specs=in_specs, out_specs=out_specs, out_shape=out_shape,
        scratch_shapes=scratch_shapes,
        compiler_params=pltpu.CompilerParams(vmem_limit_bytes=limit),
    )
    return lambda *args: call(*[pltpu.with_memory_space_constraint(a, pltpu.HBM) for a in args])


def _dot(a, b, form):
    dn = {"nn": (((1,), (0,)), ((), ())), "nt": (((1,), (1,)), ((), ())), "tn": (((0,), (0,)), ((), ()))}[form]
    return lax.dot_general(a.astype(BF16), b.astype(BF16), dn, preferred_element_type=F32)


def _sigmoid(x):
    return 1.0 / (1.0 + jnp.exp(-x))


def matmul(a, b, form, *, tm, tn, tk, out_dtype, name, scale=1.0):
    if form == "tn":
        kdim, m = a.shape
        n = b.shape[1]
    else:
        m, kdim = a.shape
        n = b.shape[1] if form == "nn" else b.shape[0]
    tm, tn, tk = min(tm, m), min(tn, n), min(tk, kdim)
    assert m % tm == 0 and n % tn == 0 and kdim % tk == 0, (name, m, n, kdim, tm, tn, tk)
    nk = kdim // tk

    if form == "tn":
        a_blk, a_map = (tk, tm), (lambda j, i, k: (k, i))
    else:
        a_blk, a_map = (tm, tk), (lambda j, i, k: (i, k))
    if form == "nt":
        b_blk, b_map = (tn, tk), (lambda j, i, k: (j, k))
    else:
        b_blk, b_map = (tk, tn), (lambda j, i, k: (k, j))
    o_map = lambda j, i, k: (i, j)

    def body(a_ref, b_ref, o_ref, *acc):
        def finish(total):
            o_ref[...] = (total * scale if scale != 1.0 else total).astype(out_dtype)

        if nk == 1:
            finish(_dot(a_ref[...], b_ref[...], form))
        else:
            acc_ref, = acc
            k = pl.program_id(2)

            @pl.when(k == 0)
            def _():
                acc_ref[...] = _dot(a_ref[...], b_ref[...], form)

            @pl.when(k > 0)
            def _():
                acc_ref[...] += _dot(a_ref[...], b_ref[...], form)

            @pl.when(k == nk - 1)
            def _():
                finish(acc_ref[...])

    scratch = [pltpu.VMEM((tm, tn), F32)] if nk > 1 else []
    return _pcall(
        body, name=name, grid=(n // tn, m // tm, nk), in_specs=[pl.BlockSpec(a_blk, a_map), pl.BlockSpec(b_blk, b_map)],
        out_specs=pl.BlockSpec((tm, tn), o_map), out_shape=jax.ShapeDtypeStruct((m, n), out_dtype),
        blocks=[(a_blk, a.dtype), (b_blk, b.dtype), ((tm, tn), out_dtype)],
        scratch_shapes=scratch, scratch_bytes=(tm * tn * 4 if nk > 1 else 0),
    )(a, b)


def swiglu_fwd(h, wg_t, wu_t, *, name, tm=512, tn=1408):
    t, d = h.shape
    f = wg_t.shape[0]
    tm, tn = min(tm, t), min(tn, f)

    def body(h_ref, wg_ref, wu_ref, ga_ref, gb_ref, s_ref):
        hh = h_ref[...]
        a = _dot(hh, wg_ref[...], "nt")
        b = _dot(hh, wu_ref[...], "nt")
        sg = _sigmoid(a)
        silu = a * sg
        ga_ref[...] = (b * (sg * (1.0 + a * (1.0 - sg)))).astype(BF16)
        gb_ref[...] = silu.astype(BF16)
        s_ref[...] = (silu * b).astype(BF16)

    w_spec = pl.BlockSpec((tn, d), lambda j, i: (j, 0))
    o_spec = pl.BlockSpec((tm, tn), lambda j, i: (i, j))
    o_shape = jax.ShapeDtypeStruct((t, f), BF16)
    return _pcall(
        body, name=name, grid=(f // tn, t // tm),
        in_specs=[pl.BlockSpec((tm, d), lambda j, i: (i, 0)), w_spec, w_spec],
        out_specs=[o_spec, o_spec, o_spec], out_shape=[o_shape, o_shape, o_shape],
        blocks=[((tm, d), BF16), ((tn, d), BF16), ((tn, d), BF16)] + [((tm, tn), BF16)] * 3,
    )(h, wg_t, wu_t)


def swiglu_bwd(dyb, wd, ga, gb, *, name, scale, tm=512, tn=1408):
    t, d = dyb.shape
    f = wd.shape[0]
    tm, tn = min(tm, t), min(tn, f)

    def body(dy_ref, wd_ref, ga_ref, gb_ref, da_ref, db_ref):
        ds = _dot(dy_ref[...], wd_ref[...], "nt") * scale
        da_ref[...] = (ds * ga_ref[...].astype(F32)).astype(BF16)
        db_ref[...] = (ds * gb_ref[...].astype(F32)).astype(BF16)

    o_spec = pl.BlockSpec((tm, tn), lambda j, i: (i, j))
    o_shape = jax.ShapeDtypeStruct((t, f), BF16)
    return _pcall(
        body, name=name, grid=(f // tn, t // tm),
        in_specs=[pl.BlockSpec((tm, d), lambda j, i: (i, 0)), pl.BlockSpec((tn, d), lambda j, i: (j, 0)), o_spec, o_spec],
        out_specs=[o_spec, o_spec], out_shape=[o_shape, o_shape],
        blocks=[((tm, d), BF16), ((tn, d), BF16)] + [((tm, tn), BF16)] * 4,
    )(dyb, wd, ga, gb)


def gate_fwd(odil, osb, wpd_t, wps_t, proj, *, name, tm=512):
    t = odil.shape[0]
    tm = min(tm, t)

    def body(od_ref, os_ref, wpd_ref, wps_ref, g1_ref, g2_ref, y_ref, u1_ref, u2_ref):
        u1 = _dot(od_ref[...], wpd_ref[...], "nt")
        u2 = _dot(os_ref[...], wps_ref[...], "nt")
        y = _sigmoid(g1_ref[...].astype(F32)) * u1 + _sigmoid(g2_ref[...].astype(F32)) * u2
        y_ref[...] = y.astype(BF16)
        u1_ref[...] = u1.astype(BF16)
        u2_ref[...] = u2.astype(BF16)

    o_spec = pl.BlockSpec((tm, D_MODEL), lambda i: (i, 0))
    w_spec = pl.BlockSpec((D_MODEL, GROUP_W), lambda i: (0, 0))
    a_spec = pl.BlockSpec((tm, GROUP_W), lambda i: (i, 0))
    o_shape = jax.ShapeDtypeStruct((t, D_MODEL), BF16)
    return _pcall(
        body, name=name, grid=(t // tm,),
        in_specs=[a_spec, a_spec, w_spec, w_spec,
                  pl.BlockSpec((tm, D_MODEL), lambda i: (i, GATE_DIL_BLK)),
                  pl.BlockSpec((tm, D_MODEL), lambda i: (i, GATE_SB_BLK))],
        out_specs=[o_spec, o_spec, o_spec], out_shape=[o_shape, o_shape, o_shape],
        blocks=[((tm, GROUP_W), F32)] * 2 + [((D_MODEL, GROUP_W), BF16)] * 2 + [((tm, D_MODEL), BF16)] * 5,
    )(odil, osb, wpd_t, wps_t, proj, proj)


def gate_bwd(dxb, wout, u1, u2, proj, *, name, tm=512):
    t = dxb.shape[0]
    tm = min(tm, t)

    def body(dx_ref, w_ref, u1_ref, u2_ref, g1_ref, g2_ref, du1_ref, du2_ref, dg_ref):
        dy = _dot(dx_ref[...], w_ref[...], "nt")
        s1 = _sigmoid(g1_ref[...].astype(F32))
        s2 = _sigmoid(g2_ref[...].astype(F32))
        du1_ref[...] = (dy * s1).astype(BF16)
        du2_ref[...] = (dy * s2).astype(BF16)
        dg_ref[:, :D_MODEL] = (dy * u1_ref[...].astype(F32) * s1 * (1.0 - s1)).astype(BF16)
        dg_ref[:, D_MODEL:] = (dy * u2_ref[...].astype(F32) * s2 * (1.0 - s2)).astype(BF16)

    o_spec = pl.BlockSpec((tm, D_MODEL), lambda i: (i, 0))
    o_shape = jax.ShapeDtypeStruct((t, D_MODEL), BF16)
    return _pcall(
        body, name=name, grid=(t // tm,),
        in_specs=[o_spec, pl.BlockSpec((D_MODEL, D_MODEL), lambda i: (0, 0)), o_spec, o_spec,
                  pl.BlockSpec((tm, D_MODEL), lambda i: (i, GATE_DIL_BLK)),
                  pl.BlockSpec((tm, D_MODEL), lambda i: (i, GATE_SB_BLK))],
        out_specs=[o_spec, o_spec, pl.BlockSpec((tm, 2 * D_MODEL), lambda i: (i, 0))],
        out_shape=[o_shape, o_shape, jax.ShapeDtypeStruct((t, 2 * D_MODEL), BF16)],
        blocks=[((tm, D_MODEL), BF16)] * 9 + [((D_MODEL, D_MODEL), BF16)],
    )(dxb, wout, u1, u2, proj, proj)


def rms_fwd(x, gain, *, name, tm=512):
    t, d = x.shape
    tm = min(tm, t)

    def body(x_ref, g_ref, h_ref):
        xv = x_ref[...]
        rstd = lax.rsqrt(jnp.mean(xv * xv, axis=1, keepdims=True) + RMS_EPS)
        h_ref[...] = (xv * rstd * g_ref[...]).astype(BF16)

    return _pcall(
        body, name=name, grid=(t // tm,),
        in_specs=[pl.BlockSpec((tm, d), lambda i: (i, 0)), pl.BlockSpec((1, d), lambda i: (0, 0))],
        out_specs=pl.BlockSpec((tm, d), lambda i: (i, 0)), out_shape=jax.ShapeDtypeStruct((t, d), BF16),
        blocks=[((tm, d), F32), ((tm, d), BF16)],
    )(x, gain)


def matmul_res_norm(a, b, res, next_gain, *, scale, tm, name):
    t, k = a.shape
    d = b.shape[1]
    tm = min(tm, t)
    with_norm = next_gain is not None

    def body(a_ref, b_ref, r_ref, *rest):
        out = r_ref[...] + _dot(a_ref[...], b_ref[...], "nn") * scale
        if with_norm:
            g_ref, o_ref, h_ref = rest
            rstd = lax.rsqrt(jnp.mean(out * out, axis=1, keepdims=True) + RMS_EPS)
            h_ref[...] = (out * rstd * g_ref[...]).astype(BF16)
        else:
            o_ref, = rest
        o_ref[...] = out

    row = pl.BlockSpec((tm, d), lambda i: (i, 0))
    in_specs = [pl.BlockSpec((tm, k), lambda i: (i, 0)), pl.BlockSpec((k, d), lambda i: (0, 0)), row]
    args = [a, b, res]
    out_specs, out_shape = [row], [jax.ShapeDtypeStruct((t, d), F32)]
    if with_norm:
        in_specs.append(pl.BlockSpec((1, d), lambda i: (0, 0)))
        args.append(next_gain)
        out_specs.append(row)
        out_shape.append(jax.ShapeDtypeStruct((t, d), BF16))
    outs = _pcall(
        body, name=name, grid=(t // tm,), in_specs=in_specs, out_specs=out_specs, out_shape=out_shape,
        blocks=[((tm, k), a.dtype), ((k, d), b.dtype), ((tm, d), F32), ((tm, d), F32), ((tm, d), BF16)],
    )(*args)
    return (outs[0], outs[1]) if with_norm else (outs[0], None)


def _rms_bwd_rows(dhv, xv, g, drv):
    rstd = lax.rsqrt(jnp.mean(xv * xv, axis=1, keepdims=True) + RMS_EPS)
    xh = xv * rstd
    dxh = dhv * g
    dx = drv + rstd * (dxh - xh * jnp.mean(dxh * xh, axis=1, keepdims=True))
    return dx, jnp.sum(dhv * xh, axis=0, keepdims=True)


def matmul_rms_bwd(pairs, x, gain, dres, *, tm, name):
    t, d = x.shape
    tm = min(tm, t)
    npairs = len(pairs)

    def body(*refs):
        ab = refs[:2 * npairs]
        x_ref, g_ref, dr_ref, dx_ref, dxb_ref, dg_ref = refs[2 * npairs:]
        dh = _dot(ab[0][...], ab[1][...], "nn")
        for q in range(1, npairs):
            dh = dh + _dot(ab[2 * q][...], ab[2 * q + 1][...], "nn")
        dx, part = _rms_bwd_rows(dh, x_ref[...], g_ref[...], dr_ref[...])
        dx_ref[...] = dx
        dxb_ref[...] = dx.astype(BF16)

        @pl.when(pl.program_id(0) == 0)
        def _():
            dg_ref[...] = part

        @pl.when(pl.program_id(0) > 0)
        def _():
            dg_ref[...] += part

    in_specs, args, blocks = [], [], []
    for a, b in pairs:
        k = a.shape[1]
        in_specs += [pl.BlockSpec((tm, k), lambda i: (i, 0)), pl.BlockSpec((k, d), lambda i: (0, 0))]
        args += [a, b]
        blocks += [((tm, k), a.dtype), ((k, d), b.dtype)]
    row = pl.BlockSpec((tm, d), lambda i: (i, 0))
    vec = pl.BlockSpec((1, d), lambda i: (0, 0))
    return _pcall(
        body, name=name, grid=(t // tm,), in_specs=in_specs + [row, vec, row], out_specs=[row, row, vec],
        out_shape=[jax.ShapeDtypeStruct((t, d), F32), jax.ShapeDtypeStruct((t, d), BF16), jax.ShapeDtypeStruct((1, d), F32)],
        blocks=blocks + [((tm, d), F32)] * 3 + [((tm, d), BF16)],
    )(*args, x, gain, dres)


def final_loss(x, gain, target, *, name, tm=512):
    t, d = x.shape
    tm = min(tm, t)

    def body(x_ref, g_ref, t_ref, dx_ref, dxb_ref, dg_ref, loss_ref):
        xv = x_ref[...]
        g = g_ref[...]
        rstd = lax.rsqrt(jnp.mean(xv * xv, axis=1, keepdims=True) + RMS_EPS)
        xh = xv * rstd
        err = xh * g - t_ref[...]
        dy = err * (1.0 / d)
        dxh = dy * g
        dx = rstd * (dxh - xh * jnp.mean(dxh * xh, axis=1, keepdims=True))
        dx_ref[...] = dx
        dxb_ref[...] = dx.astype(BF16)
        part = jnp.sum(dy * xh, axis=0, keepdims=True)
        sq = jnp.sum(jnp.sum(err * err, axis=1, keepdims=True), axis=0, keepdims=True) * (0.5 / d)
        lpart = jnp.broadcast_to(sq, (1, 128))

        @pl.when(pl.program_id(0) == 0)
        def _():
            dg_ref[...] = part
            loss_ref[...] = lpart

        @pl.when(pl.program_id(0) > 0)
        def _():
            dg_ref[...] += part
            loss_ref[...] += lpart

    row = pl.BlockSpec((tm, d), lambda i: (i, 0))
    vec = pl.BlockSpec((1, d), lambda i: (0, 0))
    return _pcall(
        body, name=name, grid=(t // tm,), in_specs=[row, vec, row],
        out_specs=[row, row, vec, pl.BlockSpec((1, 128), lambda i: (0, 0))],
        out_shape=[jax.ShapeDtypeStruct((t, d), F32), jax.ShapeDtypeStruct((t, d), BF16),
                   jax.ShapeDtypeStruct((1, d), F32), jax.ShapeDtypeStruct((1, 128), F32)],
        blocks=[((tm, d), F32)] * 3 + [((tm, d), BF16)],
    )(x, gain, target)


def rope_tables(t):
    pos = jnp.arange(t, dtype=F32)
    inv_freq = ROPE_THETA ** (-jnp.arange(0, ROPE_DIM, 2, dtype=F32) / ROPE_DIM)
    ang = pos[:, None] * inv_freq[None, :]
    cos, sin = jnp.cos(ang), jnp.sin(ang)
    half = ROPE_DIM // 2
    in_head = jnp.arange(128) % HEAD_DIM
    cosw, sinw = jnp.tile(cos, (1, 128 // half)), jnp.tile(sin, (1, 128 // half))
    c = jnp.where(in_head < ROPE_DIM, cosw, 1.0)
    sa = jnp.where(in_head < half, -sinw, 0.0)
    sb = jnp.where((in_head >= half) & (in_head < ROPE_DIM), sinw, 0.0)
    return jnp.concatenate([c, sa, sb], axis=1)


def _rotate(xv, cv, sav, sbv):
    halves = []
    for half in range(2):
        x = xv[:, 128 * half:128 * (half + 1)]
        halves.append(x * cv + pltpu.roll(x, 120, 1) * sav + pltpu.roll(x, 8, 1) * sbv)
    return jnp.concatenate(halves, axis=1)


STAGE_CHUNKS = 4


def _stage(tm):
    return dict(scratch_shapes=[pltpu.VMEM((STAGE_CHUNKS, tm, 128), F32)], scratch_bytes=STAGE_CHUNKS * tm * 128 * 4)


def _split_residues(stage_ref, val, out_ref, d, col, dtype):
    rows, width = val.shape
    if d == 1:
        out_ref[0, :, col:col + width] = val.astype(dtype)
        return
    chunks = width // 128
    for c in range(chunks):
        stage_ref[c] = val[:, 128 * c:128 * (c + 1)]
    for r in range(d):
        for c in range(chunks):
            out_ref[r, :, col + 128 * c:col + 128 * (c + 1)] = stage_ref[c, pl.ds(r, rows // d, stride=d), :].astype(dtype)


def _join_residues(stage_ref, in_ref, d, col=0, width=GROUP_W):
    if d == 1:
        return in_ref[0, :, col:col + width].astype(F32)
    rows = in_ref.shape[1] * d
    chunks = width // 128
    for r in range(d):
        for c in range(chunks):
            stage_ref[c, pl.ds(r, rows // d, stride=d), :] = in_ref[r, :, col + 128 * c:col + 128 * (c + 1)].astype(F32)
    return jnp.concatenate([stage_ref[c] for c in range(chunks)], axis=1)


def rope_split(proj, tables, *, name, tm=512):
    c = sa = sb = tables
    t = tables.shape[0]
    tm = min(tm, t)

    def body(*refs):
        pieces = refs[0:9]
        c_ref, sa_ref, sb_ref = refs[9:12]
        qk_out, v_out = refs[12:15], refs[15:18]
        stage = refs[18]
        cv, sav, sbv = c_ref[...], sa_ref[...], sb_ref[...]
        for g, d in enumerate(DILATIONS):
            for kind in range(3):
                xv = pieces[3 * kind + g][...].astype(F32)
                if kind < 2:
                    _split_residues(stage, _rotate(xv, cv, sav, sbv), qk_out[g], d, GROUP_W * kind, BF16)
                else:
                    _split_residues(stage, xv, v_out[g], d, 0, BF16)

    tabs = [pl.BlockSpec((tm, 128), functools.partial(lambda i, cb: (i, cb), cb=cb)) for cb in range(3)]
    in_specs = [pl.BlockSpec((tm, GROUP_W), functools.partial(lambda i, cb: (i, cb), cb=cb)) for cb in range(9)]
    out_specs = ([pl.BlockSpec((d, tm // d, 2 * GROUP_W), lambda i: (0, i, 0)) for d in DILATIONS]
                 + [pl.BlockSpec((d, tm // d, GROUP_W), lambda i: (0, i, 0)) for d in DILATIONS])
    out_shape = ([jax.ShapeDtypeStruct((d, t // d, 2 * GROUP_W), BF16) for d in DILATIONS]
                 + [jax.ShapeDtypeStruct((d, t // d, GROUP_W), BF16) for d in DILATIONS])
    outs = _pcall(
        body, name=name, grid=(t // tm,), in_specs=in_specs + tabs, out_specs=out_specs, out_shape=out_shape,
        blocks=[((tm, GROUP_W), BF16)] * 18 + [((tm, 128), F32)] * 3,
        **_stage(tm),
    )(*([proj] * 9), c, sa, sb)
    return outs[0:3], outs[3:6]


def rope_join(dqs, dks, dvs, sb_grads, dgate, tables, *, name, tm=512):
    c = sa = sb = tables
    t = tables.shape[0]
    tm = min(tm, t)

    def body(*refs):
        pieces, sb_refs, dgate_ref = refs[0:9], refs[9:12], refs[12]
        c_ref, sa_ref, sb_ref = refs[13:16]
        o_ref, stage = refs[16], refs[17]
        cv, sav, sbv = c_ref[...], -sa_ref[...], -sb_ref[...]
        for kind in range(3):
            for g, d in enumerate(DILATIONS):
                xv = _join_residues(stage, pieces[3 * kind + g], d)
                if kind < 2:
                    xv = _rotate(xv, cv, sav, sbv)
                col = GROUP_W * (3 * kind + g)
                o_ref[:, col:col + GROUP_W] = xv.astype(BF16)
        for j in range(3):
            o_ref[:, GROUP_W * (QS_BLK + j):GROUP_W * (QS_BLK + j + 1)] = sb_refs[j][...].astype(BF16)
        o_ref[:, D_MODEL * GATE_DIL_BLK:] = dgate_ref[...]

    tabs = [pl.BlockSpec((tm, 128), functools.partial(lambda i, cb: (i, cb), cb=cb)) for cb in range(3)]
    nat = lambda w: pl.BlockSpec((tm, w), lambda i: (i, 0))
    in_specs = [pl.BlockSpec((d, tm // d, GROUP_W), lambda i: (0, i, 0)) for _ in range(3) for d in DILATIONS]
    in_specs += [nat(GROUP_W)] * 3 + [nat(2 * D_MODEL)]
    return _pcall(
        body, name=name, grid=(t // tm,), in_specs=in_specs + tabs,
        out_specs=nat(D_IN), out_shape=jax.ShapeDtypeStruct((t, D_IN), BF16),
        blocks=[((tm, GROUP_W), F32)] * 12 + [((tm, 128), F32)] * 3 + [((tm, 2 * D_MODEL), BF16), ((tm, D_IN), BF16)],
        **_stage(tm),
    )(*dqs, *dks, *dvs, *sb_grads, dgate, c, sa, sb)


def _head_mask(h):
    lane = lax.broadcasted_iota(jnp.int32, (1, GROUP_W), 1)
    return (lane // HEAD_DIM) == h


def _band_mask_before():
    ri = lax.broadcasted_iota(jnp.int32, (4 * DIL_SPAN, DIL_SPAN), 0) % DIL_SPAN
    ci = lax.broadcasted_iota(jnp.int32, (4 * DIL_SPAN, DIL_SPAN), 1)
    return ci >= ri


def dil_fwd(qk, v, *, name):
    d, nsub, _ = qk.shape
    nblk = nsub // DIL_SPAN

    def body(q_ref, kc_ref, kp_ref, vc_ref, vp_ref, o_ref, lse_ref):
        nb = pl.program_id(1)
        kk = jnp.concatenate([kp_ref[0], kc_ref[0]], axis=0)
        vv = jnp.concatenate([vp_ref[0], vc_ref[0]], axis=0)
        s = _dot(_stack_heads(q_ref[0] * ATT_SCALE), kk, "nt")
        ri = lax.broadcasted_iota(jnp.int32, s.shape, 0) % DIL_SPAN
        ci = lax.broadcasted_iota(jnp.int32, s.shape, 1)
        valid = ((ci < DIL_SPAN) & (ci >= ri) & (nb > 0)) | ((ci >= DIL_SPAN) & (ci - DIL_SPAN <= ri))
        s = jnp.where(valid, s, -jnp.inf)
        m = jnp.max(s, axis=1, keepdims=True)
        p = jnp.exp(s - m)
        den = jnp.sum(p, axis=1, keepdims=True)
        o_ref[0] = _unstack_heads(_dot(p, vv, "nn") / den, DIL_SPAN)
        lse = m + jnp.log(den)
        for h in range(4):
            lse_ref[0, :, 128 * h:128 * (h + 1)] = jnp.broadcast_to(lse[DIL_SPAN * h:DIL_SPAN * (h + 1)], (DIL_SPAN, 128))

    blk = (1, DIL_SPAN, GROUP_W)
    sblk = (1, DIL_SPAN, 512)
    prv = lambda nb: jnp.maximum(nb - 1, 0)
    return _pcall(
        body, name=name, grid=(d, nblk),
        in_specs=[pl.BlockSpec(blk, lambda r, nb: (r, nb, 0)),
                  pl.BlockSpec(blk, lambda r, nb: (r, nb, 1)),
                  pl.BlockSpec(blk, lambda r, nb: (r, prv(nb), 1)),
                  pl.BlockSpec(blk, lambda r, nb: (r, nb, 0)),
                  pl.BlockSpec(blk, lambda r, nb: (r, prv(nb), 0))],
        out_specs=[pl.BlockSpec(blk, lambda r, nb: (r, nb, 0)), pl.BlockSpec(sblk, lambda r, nb: (r, nb, 0))],
        out_shape=[jax.ShapeDtypeStruct((d, nsub, GROUP_W), F32), jax.ShapeDtypeStruct((d, nsub, 512), F32)],
        blocks=[(blk, BF16)] * 5 + [(blk, F32), (sblk, F32)],
    )(qk, qk, qk, v, v)


def dil_merge(outs, lses, *, name, tm=512):
    t = outs[0].shape[0] * outs[0].shape[1]
    tm = min(tm, t)

    def body(o0, o1, o2, l0, l1, l2, o_ref, lse_ref, stage):
        ls = [_join_residues(stage, l, d, 0, 512) for l, d in zip((l0, l1, l2), DILATIONS)]
        m = jnp.maximum(jnp.maximum(ls[0], ls[1]), ls[2])
        tot = m + jnp.log(jnp.exp(ls[0] - m) + jnp.exp(ls[1] - m) + jnp.exp(ls[2] - m))
        lse_ref[...] = tot
        lane = lax.broadcasted_iota(jnp.int32, (1, 128), 1)
        first = lane < HEAD_DIM
        acc = jnp.zeros((tm, GROUP_W), F32)
        for og, lg, d in zip((o0, o1, o2), ls, DILATIONS):
            w = jnp.exp(lg - tot)
            wide = jnp.concatenate([jnp.where(first, w[:, 0:128], w[:, 128:256]),
                                    jnp.where(first, w[:, 256:384], w[:, 384:512])], axis=1)
            acc = acc + wide * _join_residues(stage, og, d)
        o_ref[...] = acc

    o_in = [pl.BlockSpec((d, tm // d, GROUP_W), lambda i: (0, i, 0)) for d in DILATIONS]
    l_in = [pl.BlockSpec((d, tm // d, 512), lambda i: (0, i, 0)) for d in DILATIONS]
    return _pcall(
        body, name=name, grid=(t // tm,), in_specs=o_in + l_in,
        out_specs=[pl.BlockSpec((tm, GROUP_W), lambda i: (i, 0)), pl.BlockSpec((tm, 512), lambda i: (i, 0))],
        out_shape=[jax.ShapeDtypeStruct((t, GROUP_W), F32), jax.ShapeDtypeStruct((t, 512), F32)],
        blocks=[((tm, GROUP_W), F32)] * 4 + [((tm, 512), F32)] * 4,
        **_stage(tm),
    )(*outs, *lses)


def dil_bwd_prep(do, o, lse, *, name, tm=512):
    t = do.shape[0]
    tm = min(tm, t)
    wide = DILATIONS[1:]

    def body(do_ref, o_ref, lse_ref, ds_ref, *rest):
        do_out, lse_out, ds_out = rest[0:2], rest[2:4], rest[4:6]
        stage = rest[6]
        dov = do_ref[...]
        prod = dov * o_ref[...]
        for h in range(4):
            s = jnp.sum(jnp.where(_head_mask(h), prod, 0.0), axis=1, keepdims=True)
            ds_ref[:, 128 * h:128 * (h + 1)] = jnp.broadcast_to(s, (tm, 128))
        for i, d in enumerate(wide):
            _split_residues(stage, dov, do_out[i], d, 0, BF16)
            _split_residues(stage, lse_ref[...], lse_out[i], d, 0, F32)
            _split_residues(stage, ds_ref[...], ds_out[i], d, 0, F32)

    nat = lambda w: pl.BlockSpec((tm, w), lambda i: (i, 0))
    res = lambda d, w: pl.BlockSpec((d, tm // d, w), lambda i: (0, i, 0))
    shape = lambda d, w, dt: jax.ShapeDtypeStruct((d, t // d, w), dt)
    outs = _pcall(
        body, name=name, grid=(t // tm,), in_specs=[nat(GROUP_W), nat(GROUP_W), nat(512)],
        out_specs=[nat(512)] + [res(d, GROUP_W) for d in wide] + [res(d, 512) for d in wide] * 2,
        out_shape=([jax.ShapeDtypeStruct((t, 512), F32)] + [shape(d, GROUP_W, BF16) for d in wide]
                   + [shape(d, 512, F32) for d in wide] * 2),
        blocks=[((tm, GROUP_W), F32)] * 3 + [((tm, 512), F32)] * 6,
        **_stage(tm),
    )(do, o, lse)
    return outs[0], outs[1:3], outs[3:5], outs[5:7]


def head_sums(a, b, *, name, tm=512):
    t = a.shape[0]
    tm = min(tm, t)

    def body(a_ref, b_ref, o_ref):
        prod = a_ref[...].astype(BF16).astype(F32) * b_ref[...]
        for h in range(4):
            s = jnp.sum(jnp.where(_head_mask(h), prod, 0.0), axis=1, keepdims=True)
            o_ref[:, 128 * h:128 * (h + 1)] = jnp.broadcast_to(s, (tm, 128))

    spec = pl.BlockSpec((tm, GROUP_W), lambda i: (i, 0))
    return _pcall(
        body, name=name, grid=(t // tm,), in_specs=[spec, spec],
        out_specs=pl.BlockSpec((tm, 512), lambda i: (i, 0)), out_shape=jax.ShapeDtypeStruct((t, 512), F32),
        blocks=[((tm, GROUP_W), F32)] * 2 + [((tm, 512), F32)],
    )(a, b)


def dil_bwd(qk, v, do, lse, dsum, *, name):
    d, nsub, _ = qk.shape
    nblk = nsub // DIL_SPAN

    def body(qa_ref, qb_ref, kc_ref, kp_ref, vc_ref, vp_ref, doa_ref, dob_ref, la_ref, lb_ref, sa_ref, sb_ref,
             dq_ref, dk_ref, dv_ref):
        nb = pl.program_id(1)
        nxt = _band_mask_before() & (nb < nblk - 1)
        kc, kp, vc, vp = kc_ref[0], kp_ref[0], vc_ref[0], vp_ref[0]
        qas, qbs = _stack_heads(qa_ref[0] * ATT_SCALE), _stack_heads(qb_ref[0] * ATT_SCALE)
        das, dbs = _stack_heads(doa_ref[0].astype(BF16)), _stack_heads(dob_ref[0].astype(BF16))
        stat = lambda ref: jnp.concatenate([ref[0, :, 128 * h:128 * (h + 1)] for h in range(4)], axis=0)
        la, lb, sa, sb = stat(la_ref), stat(lb_ref), stat(sa_ref), stat(sb_ref)

        def probs(qs, ds_, k, v, mask, l, s):
            p = jnp.where(mask, jnp.exp(_dot(qs, k, "nt") - l), 0.0)
            dsc = p * (_dot(ds_, v, "nt") - s)
            return p.astype(BF16), dsc.astype(BF16)

        wide = lambda a: jnp.concatenate([a, a], axis=1)
        ri = lax.broadcasted_iota(jnp.int32, (4 * DIL_SPAN, 2 * DIL_SPAN), 0) % DIL_SPAN
        ci = lax.broadcasted_iota(jnp.int32, (4 * DIL_SPAN, 2 * DIL_SPAN), 1)
        valid = ((ci < DIL_SPAN) & (ci >= ri) & (nb > 0)) | ((ci >= DIL_SPAN) & (ci - DIL_SPAN <= ri))
        p_a, ds_a = probs(qas, das, jnp.concatenate([kp, kc], axis=0), jnp.concatenate([vp, vc], axis=0),
                          valid, wide(la), wide(sa))
        p_nc, ds_nc = probs(qbs, dbs, kc, vc, nxt, lb, sb)
        dq_ref[0] = _unstack_heads(_dot(ds_a, jnp.concatenate([kp, kc], axis=0), "nn"), DIL_SPAN) * ATT_SCALE
        dk_ref[0] = _dot(ds_a[:, DIL_SPAN:], qas, "tn") + _dot(ds_nc, qbs, "tn")
        dv_ref[0] = _dot(p_a[:, DIL_SPAN:], das, "tn") + _dot(p_nc, dbs, "tn")

    blk = (1, DIL_SPAN, GROUP_W)
    sblk = (1, DIL_SPAN, 512)
    prv = lambda nb: jnp.maximum(nb - 1, 0)
    nxt_ = lambda nb: jnp.minimum(nb + 1, nblk - 1)
    cur_at = lambda c: pl.BlockSpec(blk, functools.partial(lambda r, nb, c: (r, nb, c), c=c))
    prv_at = lambda c: pl.BlockSpec(blk, functools.partial(lambda r, nb, c: (r, prv(nb), c), c=c))
    nxt_at = lambda c: pl.BlockSpec(blk, functools.partial(lambda r, nb, c: (r, nxt_(nb), c), c=c))
    s_cur = pl.BlockSpec(sblk, lambda r, nb: (r, nb, 0))
    s_nxt = pl.BlockSpec(sblk, lambda r, nb: (r, nxt_(nb), 0))
    o_spec = pl.BlockSpec(blk, lambda r, nb: (r, nb, 0))
    o_shape = jax.ShapeDtypeStruct((d, nsub, GROUP_W), F32)
    return _pcall(
        body, name=name, grid=(d, nblk),
        in_specs=[cur_at(0), nxt_at(0), cur_at(1), prv_at(1), cur_at(0), prv_at(0), cur_at(0), nxt_at(0),
                  s_cur, s_nxt, s_cur, s_nxt],
        out_specs=[o_spec, o_spec, o_spec], out_shape=[o_shape, o_shape, o_shape],
        blocks=[(blk, BF16)] * 6 + [(blk, F32)] * 5 + [(sblk, F32)] * 4,
    )(qk, qk, qk, qk, v, v, do, do, lse, lse, dsum, dsum)


def _tri_dot(x, b):
    hi = x.astype(BF16)
    lo = (x - hi.astype(F32)).astype(BF16)
    return _dot(jnp.concatenate([hi, lo], axis=1), jnp.concatenate([b, b], axis=0), "nn")


SB_TILE = 256
SB_ROWS = 512


def _stack_heads(a):
    return jnp.concatenate([jnp.where(_head_mask(h), a, jnp.zeros_like(a)) for h in range(4)], axis=0)


def _unstack_heads(acc, rows):
    out = acc[0:rows]
    for h in range(1, 4):
        out = jnp.where(_head_mask(h), acc[h * rows:(h + 1) * rows], out)
    return out


def _tri_masks(n):
    ri = lax.broadcasted_iota(jnp.int32, (n, n), 0)
    ci = lax.broadcasted_iota(jnp.int32, (n, n), 1)
    return (ri > ci).astype(BF16), (ri >= ci).astype(BF16)


def _sb_weights(qs, kt, after, c_keep, lead):
    z = _dot(qs, kt, "nt")
    lbeta = jnp.minimum(z, 0.0) - jnp.log(1.0 + jnp.exp(-jnp.abs(z)))
    lkeep = lbeta - z
    past = None
    if lead is not None:
        query = lax.broadcasted_iota(jnp.int32, z.shape, 0) % SB_ROWS
        past = lax.broadcasted_iota(jnp.int32, z.shape, 1) + lead < query
        lkeep = jnp.where(past, lkeep, 0.0)
    w = jnp.exp(lbeta + _tri_dot(lkeep, after) + c_keep)
    if lead is not None:
        w = jnp.where(past, w, 0.0)
    return z, past, lbeta, lkeep, w


def _sb_walk(qb, tile, carry):
    per = SB_ROWS // SB_TILE
    for i in reversed(range(per)):
        carry = tile(pl.multiple_of(qb * SB_ROWS + i * SB_TILE, SB_TILE), i * SB_TILE, i == per - 1, carry)
    past_tiles = qb * per
    return lax.fori_loop(0, past_tiles,
                         lambda it, c: tile(pl.multiple_of((past_tiles - 1 - it) * SB_TILE, SB_TILE), None, False, c), carry)


def sb_fwd(proj, *, name):
    t = proj.shape[0]
    n, m = SB_TILE, SB_ROWS
    assert t % m == 0

    def body(q_ref, k_ref, v_ref, o_ref, acc_ref):
        qb = pl.program_id(0)
        qs = _stack_heads(q_ref[...] * ATT_SCALE)
        after, _ = _tri_masks(n)

        def tile(off, lead, first, c_keep):
            kt = k_ref[pl.ds(off, n), :]
            vt = v_ref[pl.ds(off, n), :]
            _, _, _, lkeep, w = _sb_weights(qs, kt, after, c_keep, lead)
            pv = _tri_dot(w, vt)
            if first:
                acc_ref[...] = pv
            else:
                acc_ref[...] += pv
            return c_keep + jnp.sum(lkeep, axis=1, keepdims=True)

        _sb_walk(qb, tile, jnp.zeros((4 * m, 1), F32))
        o_ref[...] = _unstack_heads(acc_ref[...], m)

    full = lambda cb: pl.BlockSpec((t, GROUP_W), functools.partial(lambda i, cb: (0, cb), cb=cb))
    return _pcall(
        body, name=name, grid=(t // m,),
        in_specs=[pl.BlockSpec((m, GROUP_W), lambda i: (i, QS_BLK)), full(KS_BLK), full(VS_BLK)],
        out_specs=pl.BlockSpec((m, GROUP_W), lambda i: (i, 0)), out_shape=jax.ShapeDtypeStruct((t, GROUP_W), F32),
        blocks=[((m, GROUP_W), BF16), ((t, GROUP_W), BF16), ((t, GROUP_W), BF16), ((m, GROUP_W), F32)],
        scratch_shapes=[pltpu.VMEM((4 * m, GROUP_W), F32)], scratch_bytes=4 * m * GROUP_W * 4,
    )(proj, proj, proj)


def sb_bwd(proj, do, gtot, *, name):
    t = proj.shape[0]
    n, m = SB_TILE, SB_ROWS
    assert t % m == 0

    def body(q_ref, k_ref, v_ref, do_ref, gt_ref, dq_ref, dk_ref, dv_ref, acc_ref):
        qb = pl.program_id(0)

        @pl.when(qb == 0)
        def _():
            dk_ref[...] = jnp.zeros_like(dk_ref)
            dv_ref[...] = jnp.zeros_like(dv_ref)

        qs = _stack_heads(q_ref[...] * ATT_SCALE)
        dos = _stack_heads(do_ref[...].astype(BF16))
        gt = jnp.concatenate([jnp.max(gt_ref[:, 128 * h:128 * (h + 1)], axis=1, keepdims=True) for h in range(4)], axis=0)
        after, from_on = _tri_masks(n)

        def tile(off, lead, first, carry):
            c_keep, c_g = carry
            kt = k_ref[pl.ds(off, n), :]
            vt = v_ref[pl.ds(off, n), :]
            z, past, lbeta, lkeep, w = _sb_weights(qs, kt, after, c_keep, lead)
            gw = w * _dot(dos, vt, "nt")
            big_g = gt - (_tri_dot(gw, from_on) + c_g)
            dz = gw * jnp.exp(lbeta - z) - big_g * jnp.exp(lbeta)
            if lead is not None:
                dz = jnp.where(past, dz, 0.0)
            dz = dz.astype(BF16)
            dk_ref[pl.ds(off, n), :] += _dot(dz, qs, "tn")
            dv_ref[pl.ds(off, n), :] += _dot(w, dos, "tn")
            dq = _dot(dz, kt, "nn")
            if first:
                acc_ref[...] = dq
            else:
                acc_ref[...] += dq
            return c_keep + jnp.sum(lkeep, axis=1, keepdims=True), c_g + jnp.sum(gw, axis=1, keepdims=True)

        zero_col = jnp.zeros((4 * m, 1), F32)
        _sb_walk(qb, tile, (zero_col, zero_col))
        dq_ref[...] = _unstack_heads(acc_ref[...], m) * ATT_SCALE

    full = lambda cb: pl.BlockSpec((t, GROUP_W), functools.partial(lambda i, cb: (0, cb), cb=cb))
    whole = pl.BlockSpec((t, GROUP_W), lambda i: (0, 0))
    rowblk = pl.BlockSpec((m, GROUP_W), lambda i: (i, 0))
    shape = jax.ShapeDtypeStruct((t, GROUP_W), F32)
    return _pcall(
        body, name=name, grid=(t // m,),
        in_specs=[pl.BlockSpec((m, GROUP_W), lambda i: (i, QS_BLK)), full(KS_BLK), full(VS_BLK), rowblk,
                  pl.BlockSpec((m, 512), lambda i: (i, 0))],
        out_specs=[rowblk, whole, whole], out_shape=[shape, shape, shape],
        blocks=[((m, GROUP_W), BF16), ((t, GROUP_W), BF16), ((t, GROUP_W), BF16), ((m, GROUP_W), F32),
                ((m, 512), F32), ((m, GROUP_W), F32), ((t, GROUP_W), F32), ((t, GROUP_W), F32)],
        scratch_shapes=[pltpu.VMEM((4 * m, GROUP_W), F32)], scratch_bytes=4 * m * GROUP_W * 4,
    )(proj, proj, proj, do, gtot)


def _mesh_place():
    return lax.axis_index("x"), lax.axis_index("y"), lax.axis_index("c")


def _flip(place, mask):
    x, y, c = place
    return ((1 - x) if mask & 4 else x, (1 - y) if mask & 2 else y, (1 - c) if mask & 1 else c)


def _dev_index(place):
    x, y, c = place
    return 4 * x + 2 * y + c


HBM_SPEC = pl.BlockSpec(memory_space=pltpu.HBM)


def all_gather_rows(shard, after, *, name):
    rows, lanes = shard.shape

    def body(x_ref, after_ref, out_ref, send_sems, recv_sems, local_sem):
        me = _mesh_place()
        x, y, c = me
        sibling = _flip(me, 1)
        chips = [_flip(me, 4), _flip(me, 2), _flip(me, 6)]

        def copy(k, block, to, src=None):
            dst = out_ref.at[_dev_index(block)]
            return pltpu.make_async_remote_copy(
                src_ref=dst if src is None else src, dst_ref=dst, send_sem=send_sems.at[k], recv_sem=recv_sems.at[k],
                device_id=to, device_id_type=pl.DeviceIdType.MESH)

        mine = pltpu.make_async_copy(x_ref, out_ref.at[_dev_index(me)], local_sem)
        mine.start()
        first = [copy(0, me, sibling, src=x_ref)] + [copy(1 + j, me, chip, src=x_ref) for j, chip in enumerate(chips)]
        for cp in first:
            cp.start()
        passed = [copy(4 + j, chip, sibling) for j, chip in enumerate(chips)]
        for j, chip in enumerate(chips):
            copy(1 + j, chip, me).wait_recv()
            passed[j].start()
        copy(0, sibling, me).wait_recv()
        for j, chip in enumerate(chips):
            copy(4 + j, _flip(chip, 1), me).wait_recv()
        for cp in first + passed:
            cp.wait_send()
        mine.wait()

    return pl.pallas_call(
        body, name=name, in_specs=[HBM_SPEC, pl.BlockSpec(memory_space=pl.ANY)], out_specs=HBM_SPEC,
        out_shape=jax.ShapeDtypeStruct((N_DEV, rows, lanes), shard.dtype),
        scratch_shapes=[pltpu.SemaphoreType.DMA((7,)), pltpu.SemaphoreType.DMA((7,)), pltpu.SemaphoreType.DMA],
    )(shard, after)


SEM_SPEC = pl.BlockSpec(memory_space=pltpu.SEMAPHORE)
DATAFLOW_EFFECT = pltpu.SideEffectType.DATAFLOW_SIDE_EFFECTING


ALL_PEERS = tuple(range(1, N_DEV))
CHIP_PEERS = (1, 4, 2, 6)
OTHER_CHIPS = (4, 2, 6)


def _spread_copies(src_refs, land_refs, send_sems, recv_sems, per_peer, masks, arriving):
    me = _mesh_place()
    my = _dev_index(me)
    remote, local = [], []
    for t, (src_ref, land_ref) in enumerate(zip(src_refs, land_refs)):
        for i, mask in enumerate(masks):
            peer = _flip(me, mask)
            data_of = my if arriving else _dev_index(peer)
            slot = _dev_index(peer) if arriving else my
            k = t * len(masks) + i
            remote.append(pltpu.make_async_remote_copy(
                src_ref=src_ref.at[data_of] if per_peer else src_ref, dst_ref=land_ref.at[slot],
                send_sem=send_sems.at[k], recv_sem=recv_sems.at[k],
                device_id=peer, device_id_type=pl.DeviceIdType.MESH))
        local.append(pltpu.make_async_copy(src_ref.at[my] if per_peer else src_ref, land_ref.at[my],
                                           send_sems.at[len(src_refs) * len(masks) + t]))
    return remote, local


def spread_start(srcs, *, per_peer, name, masks=ALL_PEERS):
    nt = len(srcs)
    zones = [pltpu.HBM((N_DEV,) + (s.shape[1:] if per_peer else s.shape), s.dtype) for s in srcs]

    def body(*refs):
        src_refs, (send_sems, recv_sems) = refs[:nt], refs[nt:nt + 2]
        land_refs, token = refs[2 * nt + 2:3 * nt + 2], refs[3 * nt + 2]
        remote, local = _spread_copies(src_refs, land_refs, send_sems, recv_sems, per_peer, masks, arriving=False)
        for cp in remote + local:
            cp.start()
        token[...] = jnp.zeros_like(token)

    return pl.pallas_call(
        body, name=name, in_specs=(HBM_SPEC,) * nt,
        out_shape=(pltpu.SemaphoreType.DMA((nt * len(masks) + nt,)), pltpu.SemaphoreType.DMA((nt * len(masks),)),
                   *[pltpu.HBM(s.shape, s.dtype) for s in srcs], *zones, jax.ShapeDtypeStruct((8, 128), F32)),
        out_specs=(SEM_SPEC, SEM_SPEC) + (HBM_SPEC,) * (2 * nt) + (pl.BlockSpec(memory_space=pltpu.VMEM),),
        input_output_aliases={t: 2 + t for t in range(nt)},
        compiler_params=pltpu.CompilerParams(has_side_effects=DATAFLOW_EFFECT),
    )(*[pltpu.with_memory_space_constraint(s, pltpu.HBM) for s in srcs])


def spread_wait(started, after, *, per_peer, name, masks=ALL_PEERS):
    nt = (len(started) - 3) // 2
    send_sems, recv_sems = started[0:2]
    srcs_thru, lands_thru = started[2:2 + nt], started[2 + nt:2 + 2 * nt]

    def body(*refs):
        src_refs, land_refs = refs[:nt], refs[nt:2 * nt]
        send_sems, recv_sems = refs[2 * nt:2 * nt + 2]
        remote, local = _spread_copies(src_refs, land_refs, send_sems, recv_sems, per_peer, masks, arriving=True)
        for cp in remote:
            cp.wait_send()
            cp.wait_recv()
        for cp in local:
            cp.wait()

    outs = pl.pallas_call(
        body, name=name, in_specs=(HBM_SPEC,) * (2 * nt) + (SEM_SPEC, SEM_SPEC, pl.BlockSpec(memory_space=pl.ANY)),
        out_shape=tuple(pltpu.HBM(a.shape, a.dtype) for a in (*srcs_thru, *lands_thru)),
        out_specs=(HBM_SPEC,) * (2 * nt), input_output_aliases={t: t for t in range(2 * nt)},
        compiler_params=pltpu.CompilerParams(has_side_effects=DATAFLOW_EFFECT),
    )(*srcs_thru, *lands_thru, send_sems, recv_sems, after)
    return list(outs[nt:])


def _relay_copies(land_refs, send_sems, recv_sems, arriving):
    me = _mesh_place()
    sibling = _flip(me, 1)
    out = []
    for t, land_ref in enumerate(land_refs):
        for i, mask in enumerate(OTHER_CHIPS):
            slot = _dev_index(_flip(sibling if arriving else me, mask))
            k = t * len(OTHER_CHIPS) + i
            out.append(pltpu.make_async_remote_copy(
                src_ref=land_ref.at[slot], dst_ref=land_ref.at[slot], send_sem=send_sems.at[k], recv_sem=recv_sems.at[k],
                device_id=sibling, device_id_type=pl.DeviceIdType.MESH))
    return out


def relay_start(lands, *, name):
    nt = len(lands)
    n_sem = nt * len(OTHER_CHIPS)

    def body(*refs):
        for cp in _relay_copies(refs[:nt], refs[nt], refs[nt + 1], arriving=False):
            cp.start()

    return pl.pallas_call(
        body, name=name, in_specs=(HBM_SPEC,) * nt,
        out_shape=(pltpu.SemaphoreType.DMA((n_sem,)), pltpu.SemaphoreType.DMA((n_sem,)),
                   *[pltpu.HBM(a.shape, a.dtype) for a in lands]),
        out_specs=(SEM_SPEC, SEM_SPEC) + (HBM_SPEC,) * nt, input_output_aliases={t: 2 + t for t in range(nt)},
        compiler_params=pltpu.CompilerParams(has_side_effects=DATAFLOW_EFFECT),
    )(*[pltpu.with_memory_space_constraint(a, pltpu.HBM) for a in lands])


def relay_wait(started, *, name):
    send_sems, recv_sems = started[0:2]
    lands_thru = started[2:]
    nt = len(lands_thru)

    def body(*refs):
        for cp in _relay_copies(refs[:nt], refs[nt], refs[nt + 1], arriving=True):
            cp.wait_send()
            cp.wait_recv()

    return list(pl.pallas_call(
        body, name=name, in_specs=(HBM_SPEC,) * nt + (SEM_SPEC, SEM_SPEC),
        out_shape=tuple(pltpu.HBM(a.shape, a.dtype) for a in lands_thru), out_specs=(HBM_SPEC,) * nt,
        input_output_aliases={t: t for t in range(nt)},
        compiler_params=pltpu.CompilerParams(has_side_effects=DATAFLOW_EFFECT),
    )(*lands_thru, send_sems, recv_sems))


def sum_partials(parts, *, name, tr):
    _, rows, lanes = parts.shape
    assert rows % tr == 0

    def body(p_ref, g_ref):
        g = p_ref[0].astype(F32)
        for k in range(1, N_DEV):
            g = g + p_ref[k].astype(F32)
        g_ref[...] = g

    return _pcall(
        body, name=name, grid=(rows // tr,),
        in_specs=[pl.BlockSpec((N_DEV, tr, lanes), lambda i: (0, i, 0))],
        out_specs=pl.BlockSpec((tr, lanes), lambda i: (i, 0)), out_shape=jax.ShapeDtypeStruct((rows, lanes), F32),
        blocks=[((N_DEV, tr, lanes), parts.dtype), ((tr, lanes), F32)],
    )(parts)


def adamw(g, w, m, v, *, name, tr):
    nl, k, n = w.shape
    tr = max(c for c in range(8, min(tr, k) + 1, 8) if k % c == 0)
    bc1 = 1.0 - ADAM_B1 ** ADAM_STEP
    bc2 = 1.0 - ADAM_B2 ** ADAM_STEP

    def body(g_ref, w_ref, m_ref, v_ref, d_ref, mo_ref, vo_ref):
        gv = g_ref[...]
        m_new = ADAM_B1 * m_ref[...] + (1.0 - ADAM_B1) * gv
        v_new = ADAM_B2 * v_ref[...] + (1.0 - ADAM_B2) * (gv * gv)
        mo_ref[...] = m_new
        vo_ref[...] = v_new
        d_ref[...] = -ADAM_LR * ((m_new / bc1) / (jnp.sqrt(v_new / bc2) + ADAM_EPS) + ADAM_WD * w_ref[...])

    spec = pl.BlockSpec((1, tr, n), lambda l, i: (l, i, 0))
    shape = jax.ShapeDtypeStruct(w.shape, F32)
    return _pcall(
        body, name=name, grid=(nl, k // tr), in_specs=[spec] * 4, out_specs=[spec] * 3, out_shape=[shape] * 3,
        blocks=[((1, tr, n), F32)] * 7,
    )(g, w, m, v)


def sum_adamw(partials, w, m, v, *, name, tr):
    nl, k, n = w.shape
    assert nl == len(partials) == 2
    tr = max(c for c in range(8, min(tr, k) + 1, 8) if k % c == 0)
    bc1 = 1.0 - ADAM_B1 ** ADAM_STEP
    bc2 = 1.0 - ADAM_B2 ** ADAM_STEP

    def body(p0_ref, p1_ref, w_ref, m_ref, v_ref, g_ref, d_ref, mo_ref, vo_ref):
        first = pl.program_id(0) == 0
        gv = jnp.where(first, p0_ref[0], p1_ref[0]).astype(F32)
        for s in range(1, N_DEV):
            gv = gv + jnp.where(first, p0_ref[s], p1_ref[s]).astype(F32)
        m_new = ADAM_B1 * m_ref[0] + (1.0 - ADAM_B1) * gv
        v_new = ADAM_B2 * v_ref[0] + (1.0 - ADAM_B2) * (gv * gv)
        g_ref[0] = gv
        mo_ref[0] = m_new
        vo_ref[0] = v_new
        d_ref[0] = -ADAM_LR * ((m_new / bc1) / (jnp.sqrt(v_new / bc2) + ADAM_EPS) + ADAM_WD * w_ref[0])

    spec = pl.BlockSpec((1, tr, n), lambda l, i: (l, i, 0))
    p0spec = pl.BlockSpec((N_DEV, tr, n), lambda l, i: (0, i * (1 - l), 0))
    p1spec = pl.BlockSpec((N_DEV, tr, n), lambda l, i: (0, i * l, 0))
    shape = jax.ShapeDtypeStruct(w.shape, F32)
    return _pcall(
        body, name=name, grid=(nl, k // tr), in_specs=[p0spec, p1spec, spec, spec, spec], out_specs=[spec] * 4,
        out_shape=[shape] * 4, blocks=[((N_DEV, tr, n), BF16)] * 2 + [((1, tr, n), F32)] * 7,
    )(partials[0], partials[1], w, m, v)


def travelling(a, by_cols):
    return jnp.swapaxes(a, -1, -2) if by_cols else a


def _row(v):
    return v.reshape(1, -1)


def ffn_fwd(x, h, w, pre, tag, next_gain):
    ga, gb, s = swiglu_fwd(h, w[pre + "_w_gate"], w[pre + "_w_up"], name=f"{tag}_gateup")
    if callable(w[pre + "_w_down"]):
        w[pre + "_w_down"] = w[pre + "_w_down"](s)
    out, h_next = matmul_res_norm(s, w[pre + "_w_down"], x, next_gain, scale=0.5, tm=512, name=f"{tag}_down")
    return out, h_next, (x, h, ga, gb, s)


def ffn_bwd_weights(dxb, saved, w, pre, tag):
    x, h, a, b, s = saved
    da, db = swiglu_bwd(dxb, w[pre + "_w_down"], a, b, scale=0.5, name=f"{tag}_dgateup")
    g_down = matmul(s, dxb, "tn", tm=1408, tn=1024, tk=2048, out_dtype=BF16, scale=0.5, name=f"{tag}_gdown")
    g_gate = matmul(da, h, "tn", tm=1408, tn=1024, tk=2048, out_dtype=BF16, name=f"{tag}_ggate")
    g_up = matmul(db, h, "tn", tm=1408, tn=1024, tk=2048, out_dtype=BF16, name=f"{tag}_gup")
    return {pre + "_w_gate": g_gate, pre + "_w_up": g_up, pre + "_w_down": g_down}, (da, db)


def ffn_bwd_input(dx, rest, saved, gain, w, pre, tag):
    da, db = rest
    x = saved[0]
    return matmul_rms_bwd([(da, w[pre + "_w_gate"]), (db, w[pre + "_w_up"])], x, gain, dx, tm=256, name=f"{tag}_dh")


def mixer_fwd(x, h, w, tables, tag, next_gain):
    proj = matmul(h, w["w_in"], "nt", tm=512, tn=1280, tk=1024, out_dtype=BF16, name=f"{tag}_in")
    qks, vs = rope_split(proj, tables, name=f"{tag}_rope")
    outs, lses = [], []
    for g in range(N_DIL_GROUPS):
        o, lse = dil_fwd(qks[g], vs[g], name=f"{tag}_dil{g}")
        outs.append(o)
        lses.append(lse)
    odil, lse = dil_merge(outs, lses, name=f"{tag}_merge")
    osb = sb_fwd(proj, name=f"{tag}_sb")
    y, u1, u2 = gate_fwd(odil, osb, w["w_proj_dil"], w["w_proj_sb"], proj, name=f"{tag}_gate")
    out, h_next = matmul_res_norm(y, w["w_out"], x, next_gain, scale=1.0, tm=512, name=f"{tag}_out")
    return out, h_next, (x, h, proj, qks, vs, odil, lse, osb, u1, u2, y)


def mixer_bwd_weights(dxb, saved, w, tables, tag):
    x, h, proj, qks, vs, odil, lse, osb, u1, u2, y = saved
    t = x.shape[0]
    g_out = matmul(y, dxb, "tn", tm=1024, tn=1024, tk=2048, out_dtype=BF16, name=f"{tag}_gout")
    du1, du2, dgate = gate_bwd(dxb, w["w_out"], u1, u2, proj, name=f"{tag}_dgate")
    g_pd = matmul(du1, odil, "tn", tm=1024, tn=256, tk=2048, out_dtype=BF16, name=f"{tag}_gpd")
    g_ps = matmul(du2, osb, "tn", tm=1024, tn=256, tk=2048, out_dtype=BF16, name=f"{tag}_gps")
    dodil = matmul(du1, w["w_proj_dil"], "nn", tm=512, tn=256, tk=1024, out_dtype=F32, name=f"{tag}_dodil")
    dosb = matmul(du2, w["w_proj_sb"], "nn", tm=512, tn=256, tk=1024, out_dtype=F32, name=f"{tag}_dosb")
    dsum, do_wide, lse_wide, dsum_wide = dil_bwd_prep(dodil, odil, lse, name=f"{tag}_dprep")
    dos = [dodil[None]] + list(do_wide)
    lss = [lse[None]] + list(lse_wide)
    dss = [dsum[None]] + list(dsum_wide)
    dqs, dks, dvs = [], [], []
    for g in range(N_DIL_GROUPS):
        dq, dk, dv = dil_bwd(qks[g], vs[g], dos[g], lss[g], dss[g], name=f"{tag}_ddil{g}")
        dqs.append(dq)
        dks.append(dk)
        dvs.append(dv)
    gtot = head_sums(dosb, osb, name=f"{tag}_gsum")
    sb_grads = sb_bwd(proj, dosb, gtot, name=f"{tag}_dsb")
    dproj = rope_join(dqs, dks, dvs, sb_grads, dgate, tables, name=f"{tag}_drope")
    g_in = matmul(dproj, h, "tn", tm=1280, tn=1024, tk=2048, out_dtype=BF16, name=f"{tag}_gin")
    return {"w_in": g_in, "w_proj_dil": g_pd, "w_proj_sb": g_ps, "w_out": g_out}, dproj


def mixer_bwd_input(dx, dproj, saved, gain, w, tag):
    x = saved[0]
    return matmul_rms_bwd([(dproj, w["w_in"])], x, gain, dx, tm=256, name=f"{tag}_dh")


def kernel(x, norm_ffn1, ffn1_w_gate, ffn1_w_up, ffn1_w_down, norm_mix, w_in, w_proj_dil, w_proj_sb, w_out, norm_ffn2, ffn2_w_gate, ffn2_w_up, ffn2_w_down, norm_final, loss_target, m_norm_ffn1, m_ffn1_w_gate, m_ffn1_w_up, m_ffn1_w_down, m_norm_mix, m_w_in, m_w_proj_dil, m_w_proj_sb, m_w_out, m_norm_ffn2, m_ffn2_w_gate, m_ffn2_w_up, m_ffn2_w_down, m_norm_final, v_norm_ffn1, v_ffn1_w_gate, v_ffn1_w_up, v_ffn1_w_down, v_norm_mix, v_w_in, v_w_proj_dil, v_w_proj_sb, v_w_out, v_norm_ffn2, v_ffn2_w_gate, v_ffn2_w_up, v_ffn2_w_down, v_norm_final):
    args = dict(locals())
    t = x.shape[1]
    xs = x.reshape(t, D_MODEL)
    target = loss_target.reshape(t, D_MODEL)
    tables = rope_tables(t)

    parts = [(l, p) for l in range(2) for p in SUBBLOCKS]
    gains = {n: args[n] for n in NORM_ROWS}

    shipments = []
    for l, p in parts:
        if (l, p) == parts[0]:
            shipments += [(l, p, SUBBLOCKS[p][:2], CHIP_PEERS), (l, p, SUBBLOCKS[p][2:], ALL_PEERS)]
        else:
            shipments.append((l, p, SUBBLOCKS[p], ALL_PEERS))
    in_flight, order_token = [], jnp.zeros((1, 1), F32)
    for l, p, tensors, masks in shipments:
        shards = [travelling(args[n][l], by_cols).astype(BF16) for n, by_cols in tensors]
        shards[0] = shards[0] + order_token.astype(BF16)
        in_flight.append(spread_start(shards, per_peer=False, masks=masks, name=f"gather_start_l{l}_{tensors[0][0]}"))
        order_token = in_flight[-1][-1][0:1, 0:1]

    def arrived(i, after):
        l, p, tensors, masks = shipments[i]
        tag = f"l{l}_{tensors[0][0]}"
        lands = spread_wait(in_flight[i], after, per_peer=False, masks=masks, name=f"gather_wait_{tag}")
        if masks is CHIP_PEERS:
            lands = relay_wait(relay_start(lands, name=f"gather_relay_{tag}"), name=f"gather_relayed_{tag}")
        return {n: land.reshape(-1, land.shape[-1]) for (n, _), land in zip(tensors, lands)}

    def weights_of(l, p, after):
        mine = [i for i, s in enumerate(shipments) if s[0:2] == (l, p)]
        w = arrived(mine[0], after)
        for i in mine[1:]:
            for n, _ in shipments[i][2]:
                w[n] = functools.partial(lambda after, i, n: arrived(i, after)[n], i=i, n=n)
        return w

    saved, weights = {}, {}
    act = xs
    h = rms_fwd(xs, _row(gains["norm_ffn1"][0]) + order_token, name="l0_ffn1_norm")
    for i, (l, p) in enumerate(parts):
        weights[(l, p)] = weights_of(l, p, h if i == 0 else act)
        nl, np_ = parts[i + 1] if i + 1 < len(parts) else (None, None)
        next_gain = _row(gains["norm_" + np_][nl]) if np_ else None
        if p == "mix":
            act, h, saved[(l, p)] = mixer_fwd(act, h, weights[(l, p)], tables, f"l{l}_mix", next_gain)
        else:
            act, h, saved[(l, p)] = ffn_fwd(act, h, weights[(l, p)], p, f"l{l}_{p}", next_gain)
    dx, dxb, g_final, loss_part = final_loss(act, _row(norm_final), target, name="loss_head")

    gain_grads, sent = {}, {}
    order_token = jnp.zeros((1, 1), F32)
    for l, p in reversed(parts):
        w, sv = weights[(l, p)], saved[(l, p)]
        if p == "mix":
            gw, rest = mixer_bwd_weights(dxb, sv, w, tables, f"l{l}_mix")
        else:
            gw, rest = ffn_bwd_weights(dxb, sv, w, p, f"l{l}_{p}")
        slices = [gw[n].reshape(N_DEV, -1, gw[n].shape[-1]) for n, _ in SUBBLOCKS[p]]
        sent[(l, p)] = spread_start(slices, per_peer=True, name=f"reduce_start_l{l}_{p}")
        gain = _row(gains["norm_" + p][l]) + sent[(l, p)][-1][0:1, 0:1]
        if p == "mix":
            dx, dxb, gain_grads[("norm_mix", l)] = mixer_bwd_input(dx, rest, sv, gain, w, f"l{l}_mix")
        else:
            dx, dxb, gain_grads[("norm_" + p, l)] = ffn_bwd_input(dx, rest, sv, gain, w, p, f"l{l}_{p}")

    partials, big_all = {}, [{}, {}, {}, {}]

    def receive(l, p, after):
        lands = spread_wait(sent[(l, p)], after, per_peer=True, name=f"reduce_wait_l{l}_{p}")
        for (n, _), land in zip(SUBBLOCKS[p], lands):
            partials.setdefault(n, [None, None])[l] = land

    def update(p):
        for n, by_cols in SUBBLOCKS[p]:
            outs = sum_adamw(partials[n], travelling(args[n], by_cols), travelling(args["m_" + n], by_cols),
                             travelling(args["v_" + n], by_cols), tr=256, name=f"update_{n}")
            for kind, arr in enumerate(outs):
                big_all[kind][n] = travelling(arr, by_cols)
        return outs[1]

    for l, p in reversed(parts[1:]):
        receive(l, p, dx)
    update("ffn2")
    done = update("mix")
    receive(*parts[0], done)
    done = update("ffn1")

    loss_row = jnp.pad(loss_part[:, :1], ((0, 0), (0, D_MODEL - 1)))
    small = jnp.concatenate([gain_grads[(n, l)] for n in NORM_ROWS for l in range(2)] + [g_final, loss_row], axis=0)
    small_g = sum_partials(all_gather_rows(small, done, name="gather_gain_grads"), tr=8, name="sum_gain_grads")
    zero_row = jnp.zeros((1, D_MODEL), F32)
    small_of = lambda pre: jnp.concatenate([args[pre + n] for n in NORM_ROWS] + [_row(args[pre + "norm_final"]), zero_row], axis=0)[None]
    small_out = adamw(small_g[None], small_of(""), small_of("m_"), small_of("v_"), tr=8, name="update_gains")
    small_all = [small_g] + [o[0] for o in small_out]

    def gains_of(s):
        out = {n: s[2 * i:2 * i + 2] for i, n in enumerate(NORM_ROWS)}
        out["norm_final"] = s[6]
        return out

    order = ["norm_ffn1", "ffn1_w_gate", "ffn1_w_up", "ffn1_w_down", "norm_mix", "w_in", "w_proj_dil", "w_proj_sb", "w_out",
             "norm_ffn2", "ffn2_w_gate", "ffn2_w_up", "ffn2_w_down", "norm_final"]
    results = []
    for kind in range(4):
        both = {**big_all[kind], **gains_of(small_all[kind])}
        results += [both[n] for n in order]
    loss = small_g[7, 0]
    return (loss, dx.reshape(1, t, D_MODEL), *results)
```

```python
import functools

import jax
import jax.numpy as jnp
from jax import lax
from jax.experimental import pallas as pl
from jax.experimental.pallas import tpu as pltpu

F32 = jnp.float32
BF16 = jnp.bfloat16

D_MODEL = 1024
HEAD_DIM = 64
GROUP_W = 256
D_IN = 5120
N_DIL_GROUPS = 3
DIL_SPAN = 128
DILATIONS = (1, 4, 16)
ROPE_THETA = 500000.0
ROPE_DIM = 16
RMS_EPS = 1e-6
ATT_SCALE = HEAD_DIM ** -0.5
QS_BLK, KS_BLK, VS_BLK = 9, 10, 11
GATE_DIL_BLK, GATE_SB_BLK = 3, 4

ADAM_LR, ADAM_B1, ADAM_B2, ADAM_EPS, ADAM_WD, ADAM_STEP = 0.001, 0.9, 0.999, 1e-08, 0.01, 10

N_DEV = 8
VMEM_PHYSICAL_V7X = 64 << 20
VMEM_TEMP_HEADROOM = 20 << 20

SUBBLOCKS = {
    "ffn1": (("ffn1_w_gate", True), ("ffn1_w_up", True), ("ffn1_w_down", False)),
    "mix": (("w_in", True), ("w_proj_dil", True), ("w_proj_sb", True), ("w_out", False)),
    "ffn2": (("ffn2_w_gate", True), ("ffn2_w_up", True), ("ffn2_w_down", False)),
}
NORM_ROWS = ("norm_ffn1", "norm_mix", "norm_ffn2")


def _nbytes(shape, dtype):
    n = 1
    for s in shape:
        n *= s
    return n * jnp.dtype(dtype).itemsize


def _pcall(body, *, name, grid, in_specs, out_specs, out_shape, blocks, scratch_shapes=(), scratch_bytes=0):
    need = 2 * sum(_nbytes(s, d) for s, d in blocks) + scratch_bytes + VMEM_TEMP_HEADROOM
    limit = min(need, VMEM_PHYSICAL_V7X - (4 << 20))
    in_hbm = lambda s: pltpu.HBM(s.shape, s.dtype)
    out_shape = [in_hbm(s) for s in out_shape] if isinstance(out_shape, (list, tuple)) else in_hbm(out_shape)
    call = pl.pallas_call(
        body, name=name, grid=grid, in_specs=in_specs, out_specs=out_specs, out_shape=out_shape,
        scratch_shapes=scratch_shapes,
        compiler_params=pltpu.CompilerParams(vmem_limit_bytes=limit),
    )
    return lambda *args: call(*[pltpu.with_memory_space_constraint(a, pltpu.HBM) for a in args])


def _dot(a, b, form):
    dn = {"nn": (((1,), (0,)), ((), ())), "nt": (((1,), (1,)), ((), ())), "tn": (((0,), (0,)), ((), ()))}[form]
    return lax.dot_general(a.astype(BF16), b.astype(BF16), dn, preferred_element_type=F32)


def _sigmoid(x):
    return 1.0 / (1.0 + jnp.exp(-x))


def matmul(a, b, form, *, tm, tn, tk, out_dtype, name, scale=1.0):
    if form == "tn":
        kdim, m = a.shape
        n = b.shape[1]
    else:
        m, kdim = a.shape
        n = b.shape[1] if form == "nn" else b.shape[0]
    tm, tn, tk = min(tm, m), min(tn, n), min(tk, kdim)
    assert m % tm == 0 and n % tn == 0 and kdim % tk == 0, (name, m, n, kdim, tm, tn, tk)
    nk = kdim // tk

    if form == "tn":
        a_blk, a_map = (tk, tm), (lambda j, i, k: (k, i))
    else:
        a_blk, a_map = (tm, tk), (lambda j, i, k: (i, k))
    if form == "nt":
        b_blk, b_map = (tn, tk), (lambda j, i, k: (j, k))
    else:
        b_blk, b_map = (tk, tn), (lambda j, i, k: (k, j))
    o_map = lambda j, i, k: (i, j)

    def body(a_ref, b_ref, o_ref, *acc):
        def finish(total):
            o_ref[...] = (total * scale if scale != 1.0 else total).astype(out_dtype)

        if nk == 1:
            finish(_dot(a_ref[...], b_ref[...], form))
        else:
            acc_ref, = acc
            k = pl.program_id(2)

            @pl.when(k == 0)
            def _():
                acc_ref[...] = _dot(a_ref[...], b_ref[...], form)

            @pl.when(k > 0)
            def _():
                acc_ref[...] += _dot(a_ref[...], b_ref[...], form)

            @pl.when(k == nk - 1)
            def _():
                finish(acc_ref[...])

    scratch = [pltpu.VMEM((tm, tn), F32)] if nk > 1 else []
    return _pcall(
        body, name=name, grid=(n // tn, m // tm, nk), in_specs=[pl.BlockSpec(a_blk, a_map), pl.BlockSpec(b_blk, b_map)],
        out_specs=pl.BlockSpec((tm, tn), o_map), out_shape=jax.ShapeDtypeStruct((m, n), out_dtype),
        blocks=[(a_blk, a.dtype), (b_blk, b.dtype), ((tm, tn), out_dtype)],
        scratch_shapes=scratch, scratch_bytes=(tm * tn * 4 if nk > 1 else 0),
    )(a, b)


def swiglu_fwd(h, wg_t, wu_t, *, name, tm=512, tn=1408):
    t, d = h.shape
    f = wg_t.shape[0]
    tm, tn = min(tm, t), min(tn, f)

    def body(h_ref, wg_ref, wu_ref, ga_ref, gb_ref, s_ref):
        hh = h_ref[...]
        a = _dot(hh, wg_ref[...], "nt")
        b = _dot(hh, wu_ref[...], "nt")
        sg = _sigmoid(a)
        silu = a * sg
        ga_ref[...] = (b * (sg * (1.0 + a * (1.0 - sg)))).astype(BF16)
        gb_ref[...] = silu.astype(BF16)
        s_ref[...] = (silu * b).astype(BF16)

    w_spec = pl.BlockSpec((tn, d), lambda j, i: (j, 0))
    o_spec = pl.BlockSpec((tm, tn), lambda j, i: (i, j))
    o_shape = jax.ShapeDtypeStruct((t, f), BF16)
    return _pcall(
        body, name=name, grid=(f // tn, t // tm),
        in_specs=[pl.BlockSpec((tm, d), lambda j, i: (i, 0)), w_spec, w_spec],
        out_specs=[o_spec, o_spec, o_spec], out_shape=[o_shape, o_shape, o_shape],
        blocks=[((tm, d), BF16), ((tn, d), BF16), ((tn, d), BF16)] + [((tm, tn), BF16)] * 3,
    )(h, wg_t, wu_t)


def swiglu_bwd(dyb, wd, ga, gb, *, name, scale, tm=512, tn=1408):
    t, d = dyb.shape
    f = wd.shape[0]
    tm, tn = min(tm, t), min(tn, f)

    def body(dy_ref, wd_ref, ga_ref, gb_ref, da_ref, db_ref):
        ds = _dot(dy_ref[...], wd_ref[...], "nt") * scale
        da_ref[...] = (ds * ga_ref[...].astype(F32)).astype(BF16)
        db_ref[...] = (ds * gb_ref[...].astype(F32)).astype(BF16)

    o_spec = pl.BlockSpec((tm, tn), lambda j, i: (i, j))
    o_shape = jax.ShapeDtypeStruct((t, f), BF16)
    return _pcall(
        body, name=name, grid=(f // tn, t // tm),
        in_specs=[pl.BlockSpec((tm, d), lambda j, i: (i, 0)), pl.BlockSpec((tn, d), lambda j, i: (j, 0)), o_spec, o_spec],
        out_specs=[o_spec, o_spec], out_shape=[o_shape, o_shape],
        blocks=[((tm, d), BF16), ((tn, d), BF16)] + [((tm, tn), BF16)] * 4,
    )(dyb, wd, ga, gb)


def gate_fwd(odil, osb, wpd_t, wps_t, proj, *, name, tm=512):
    t = odil.shape[0]
    tm = min(tm, t)

    def body(od_ref, os_ref, wpd_ref, wps_ref, g1_ref, g2_ref, y_ref, u1_ref, u2_ref):
        u1 = _dot(od_ref[...], wpd_ref[...], "nt")
        u2 = _dot(os_ref[...], wps_ref[...], "nt")
        y = _sigmoid(g1_ref[...].astype(F32)) * u1 + _sigmoid(g2_ref[...].astype(F32)) * u2
        y_ref[...] = y.astype(BF16)
        u1_ref[...] = u1.astype(BF16)
        u2_ref[...] = u2.astype(BF16)

    o_spec = pl.BlockSpec((tm, D_MODEL), lambda i: (i, 0))
    w_spec = pl.BlockSpec((D_MODEL, GROUP_W), lambda i: (0, 0))
    a_spec = pl.BlockSpec((tm, GROUP_W), lambda i: (i, 0))
    o_shape = jax.ShapeDtypeStruct((t, D_MODEL), BF16)
    return _pcall(
        body, name=name, grid=(t // tm,),
        in_specs=[a_spec, a_spec, w_spec, w_spec,
                  pl.BlockSpec((tm, D_MODEL), lambda i: (i, GATE_DIL_BLK)),
                  pl.BlockSpec((tm, D_MODEL), lambda i: (i, GATE_SB_BLK))],
        out_specs=[o_spec, o_spec, o_spec], out_shape=[o_shape, o_shape, o_shape],
        blocks=[((tm, GROUP_W), F32)] * 2 + [((D_MODEL, GROUP_W), BF16)] * 2 + [((tm, D_MODEL), BF16)] * 5,
    )(odil, osb, wpd_t, wps_t, proj, proj)


def gate_bwd(dxb, wout, u1, u2, proj, *, name, tm=512):
    t = dxb.shape[0]
    tm = min(tm, t)

    def body(dx_ref, w_ref, u1_ref, u2_ref, g1_ref, g2_ref, du1_ref, du2_ref, dg_ref):
        dy = _dot(dx_ref[...], w_ref[...], "nt")
        s1 = _sigmoid(g1_ref[...].astype(F32))
        s2 = _sigmoid(g2_ref[...].astype(F32))
        du1_ref[...] = (dy * s1).astype(BF16)
        du2_ref[...] = (dy * s2).astype(BF16)
        dg_ref[:, :D_MODEL] = (dy * u1_ref[...].astype(F32) * s1 * (1.0 - s1)).astype(BF16)
        dg_ref[:, D_MODEL:] = (dy * u2_ref[...].astype(F32) * s2 * (1.0 - s2)).astype(BF16)

    o_spec = pl.BlockSpec((tm, D_MODEL), lambda i: (i, 0))
    o_shape = jax.ShapeDtypeStruct((t, D_MODEL), BF16)
    return _pcall(
        body, name=name, grid=(t // tm,),
        in_specs=[o_spec, pl.BlockSpec((D_MODEL, D_MODEL), lambda i: (0, 0)), o_spec, o_spec,
                  pl.BlockSpec((tm, D_MODEL), lambda i: (i, GATE_DIL_BLK)),
                  pl.BlockSpec((tm, D_MODEL), lambda i: (i, GATE_SB_BLK))],
        out_specs=[o_spec, o_spec, pl.BlockSpec((tm, 2 * D_MODEL), lambda i: (i, 0))],
        out_shape=[o_shape, o_shape, jax.ShapeDtypeStruct((t, 2 * D_MODEL), BF16)],
        blocks=[((tm, D_MODEL), BF16)] * 9 + [((D_MODEL, D_MODEL), BF16)],
    )(dxb, wout, u1, u2, proj, proj)


def rms_fwd(x, gain, *, name, tm=512):
    t, d = x.shape
    tm = min(tm, t)

    def body(x_ref, g_ref, h_ref):
        xv = x_ref[...]
        rstd = lax.rsqrt(jnp.mean(xv * xv, axis=1, keepdims=True) + RMS_EPS)
        h_ref[...] = (xv * rstd * g_ref[...]).astype(BF16)

    return _pcall(
        body, name=name, grid=(t // tm,),
        in_specs=[pl.BlockSpec((tm, d), lambda i: (i, 0)), pl.BlockSpec((1, d), lambda i: (0, 0))],
        out_specs=pl.BlockSpec((tm, d), lambda i: (i, 0)), out_shape=jax.ShapeDtypeStruct((t, d), BF16),
        blocks=[((tm, d), F32), ((tm, d), BF16)],
    )(x, gain)


def matmul_res_norm(a, b, res, next_gain, *, scale, tm, name):
    t, k = a.shape
    d = b.shape[1]
    tm = min(tm, t)
    with_norm = next_gain is not None

    def body(a_ref, b_ref, r_ref, *rest):
        out = r_ref[...] + _dot(a_ref[...], b_ref[...], "nn") * scale
        if with_norm:
            g_ref, o_ref, h_ref = rest
            rstd = lax.rsqrt(jnp.mean(out * out, axis=1, keepdims=True) + RMS_EPS)
            h_ref[...] = (out * rstd * g_ref[...]).astype(BF16)
        else:
            o_ref, = rest
        o_ref[...] = out

    row = pl.BlockSpec((tm, d), lambda i: (i, 0))
    in_specs = [pl.BlockSpec((tm, k), lambda i: (i, 0)), pl.BlockSpec((k, d), lambda i: (0, 0)), row]
    args = [a, b, res]
    out_specs, out_shape = [row], [jax.ShapeDtypeStruct((t, d), F32)]
    if with_norm:
        in_specs.append(pl.BlockSpec((1, d), lambda i: (0, 0)))
        args.append(next_gain)
        out_specs.append(row)
        out_shape.append(jax.ShapeDtypeStruct((t, d), BF16))
    outs = _pcall(
        body, name=name, grid=(t // tm,), in_specs=in_specs, out_specs=out_specs, out_shape=out_shape,
        blocks=[((tm, k), a.dtype), ((k, d), b.dtype), ((tm, d), F32), ((tm, d), F32), ((tm, d), BF16)],
    )(*args)
    return (outs[0], outs[1]) if with_norm else (outs[0], None)


def _rms_bwd_rows(dhv, xv, g, drv):
    rstd = lax.rsqrt(jnp.mean(xv * xv, axis=1, keepdims=True) + RMS_EPS)
    xh = xv * rstd
    dxh = dhv * g
    dx = drv + rstd * (dxh - xh * jnp.mean(dxh * xh, axis=1, keepdims=True))
    return dx, jnp.sum(dhv * xh, axis=0, keepdims=True)


def matmul_rms_bwd(pairs, x, gain, dres, *, tm, name):
    t, d = x.shape
    tm = min(tm, t)
    npairs = len(pairs)

    def body(*refs):
        ab = refs[:2 * npairs]
        x_ref, g_ref, dr_ref, dx_ref, dxb_ref, dg_ref = refs[2 * npairs:]
        dh = _dot(ab[0][...], ab[1][...], "nn")
        for q in range(1, npairs):
            dh = dh + _dot(ab[2 * q][...], ab[2 * q + 1][...], "nn")
        dx, part = _rms_bwd_rows(dh, x_ref[...], g_ref[...], dr_ref[...])
        dx_ref[...] = dx
        dxb_ref[...] = dx.astype(BF16)

        @pl.when(pl.program_id(0) == 0)
        def _():
            dg_ref[...] = part

        @pl.when(pl.program_id(0) > 0)
        def _():
            dg_ref[...] += part

    in_specs, args, blocks = [], [], []
    for a, b in pairs:
        k = a.shape[1]
        in_specs += [pl.BlockSpec((tm, k), lambda i: (i, 0)), pl.BlockSpec((k, d), lambda i: (0, 0))]
        args += [a, b]
        blocks += [((tm, k), a.dtype), ((k, d), b.dtype)]
    row = pl.BlockSpec((tm, d), lambda i: (i, 0))
    vec = pl.BlockSpec((1, d), lambda i: (0, 0))
    return _pcall(
        body, name=name, grid=(t // tm,), in_specs=in_specs + [row, vec, row], out_specs=[row, row, vec],
        out_shape=[jax.ShapeDtypeStruct((t, d), F32), jax.ShapeDtypeStruct((t, d), BF16), jax.ShapeDtypeStruct((1, d), F32)],
        blocks=blocks + [((tm, d), F32)] * 3 + [((tm, d), BF16)],
    )(*args, x, gain, dres)


def final_loss(x, gain, target, *, name, tm=512):
    t, d = x.shape
    tm = min(tm, t)

    def body(x_ref, g_ref, t_ref, dx_ref, dxb_ref, dg_ref, loss_ref):
        xv = x_ref[...]
        g = g_ref[...]
        rstd = lax.rsqrt(jnp.mean(xv * xv, axis=1, keepdims=True) + RMS_EPS)
        xh = xv * rstd
        err = xh * g - t_ref[...]
        dy = err * (1.0 / d)
        dxh = dy * g
        dx = rstd * (dxh - xh * jnp.mean(dxh * xh, axis=1, keepdims=True))
        dx_ref[...] = dx
        dxb_ref[...] = dx.astype(BF16)
        part = jnp.sum(dy * xh, axis=0, keepdims=True)
        sq = jnp.sum(jnp.sum(err * err, axis=1, keepdims=True), axis=0, keepdims=True) * (0.5 / d)
        lpart = jnp.broadcast_to(sq, (1, 128))

        @pl.when(pl.program_id(0) == 0)
        def _():
            dg_ref[...] = part
            loss_ref[...] = lpart

        @pl.when(pl.program_id(0) > 0)
        def _():
            dg_ref[...] += part
            loss_ref[...] += lpart

    row = pl.BlockSpec((tm, d), lambda i: (i, 0))
    vec = pl.BlockSpec((1, d), lambda i: (0, 0))
    return _pcall(
        body, name=name, grid=(t // tm,), in_specs=[row, vec, row],
        out_specs=[row, row, vec, pl.BlockSpec((1, 128), lambda i: (0, 0))],
        out_shape=[jax.ShapeDtypeStruct((t, d), F32), jax.ShapeDtypeStruct((t, d), BF16),
                   jax.ShapeDtypeStruct((1, d), F32), jax.ShapeDtypeStruct((1, 128), F32)],
        blocks=[((tm, d), F32)] * 3 + [((tm, d), BF16)],
    )(x, gain, target)


def rope_tables(t):
    pos = jnp.arange(t, dtype=F32)
    inv_freq = ROPE_THETA ** (-jnp.arange(0, ROPE_DIM, 2, dtype=F32) / ROPE_DIM)
    ang = pos[:, None] * inv_freq[None, :]
    cos, sin = jnp.cos(ang), jnp.sin(ang)
    half = ROPE_DIM // 2
    in_head = jnp.arange(128) % HEAD_DIM
    cosw, sinw = jnp.tile(cos, (1, 128 // half)), jnp.tile(sin, (1, 128 // half))
    c = jnp.where(in_head < ROPE_DIM, cosw, 1.0)
    sa = jnp.where(in_head < half, -sinw, 0.0)
    sb = jnp.where((in_head >= half) & (in_head < ROPE_DIM), sinw, 0.0)
    return jnp.concatenate([c, sa, sb], axis=1)


def _rotate(xv, cv, sav, sbv):
    halves = []
    for half in range(2):
        x = xv[:, 128 * half:128 * (half + 1)]
        halves.append(x * cv + pltpu.roll(x, 120, 1) * sav + pltpu.roll(x, 8, 1) * sbv)
    return jnp.concatenate(halves, axis=1)


STAGE_CHUNKS = 4


def _stage(tm):
    return dict(scratch_shapes=[pltpu.VMEM((STAGE_CHUNKS, tm, 128), F32)], scratch_bytes=STAGE_CHUNKS * tm * 128 * 4)


def _split_residues(stage_ref, val, out_ref, d, col, dtype):
    rows, width = val.shape
    if d == 1:
        out_ref[0, :, col:col + width] = val.astype(dtype)
        return
    chunks = width // 128
    for c in range(chunks):
        stage_ref[c] = val[:, 128 * c:128 * (c + 1)]
    for r in range(d):
        for c in range(chunks):
            out_ref[r, :, col + 128 * c:col + 128 * (c + 1)] = stage_ref[c, pl.ds(r, rows // d, stride=d), :].astype(dtype)


def _join_residues(stage_ref, in_ref, d, col=0, width=GROUP_W):
    if d == 1:
        return in_ref[0, :, col:col + width].astype(F32)
    rows = in_ref.shape[1] * d
    chunks = width // 128
    for r in range(d):
        for c in range(chunks):
            stage_ref[c, pl.ds(r, rows // d, stride=d), :] = in_ref[r, :, col + 128 * c:col + 128 * (c + 1)].astype(F32)
    return jnp.concatenate([stage_ref[c] for c in range(chunks)], axis=1)


def rope_split(proj, tables, *, name, tm=512):
    c = sa = sb = tables
    t = tables.shape[0]
    tm = min(tm, t)

    def body(*refs):
        pieces = refs[0:9]
        c_ref, sa_ref, sb_ref = refs[9:12]
        qk_out, v_out = refs[12:15], refs[15:18]
        stage = refs[18]
        cv, sav, sbv = c_ref[...], sa_ref[...], sb_ref[...]
        for g, d in enumerate(DILATIONS):
            for kind in range(3):
                xv = pieces[3 * kind + g][...].astype(F32)
                if kind < 2:
                    _split_residues(stage, _rotate(xv, cv, sav, sbv), qk_out[g], d, GROUP_W * kind, BF16)
                else:
                    _split_residues(stage, xv, v_out[g], d, 0, BF16)

    tabs = [pl.BlockSpec((tm, 128), functools.partial(lambda i, cb: (i, cb), cb=cb)) for cb in range(3)]
    in_specs = [pl.BlockSpec((tm, GROUP_W), functools.partial(lambda i, cb: (i, cb), cb=cb)) for cb in range(9)]
    out_specs = ([pl.BlockSpec((d, tm // d, 2 * GROUP_W), lambda i: (0, i, 0)) for d in DILATIONS]
                 + [pl.BlockSpec((d, tm // d, GROUP_W), lambda i: (0, i, 0)) for d in DILATIONS])
    out_shape = ([jax.ShapeDtypeStruct((d, t // d, 2 * GROUP_W), BF16) for d in DILATIONS]
                 + [jax.ShapeDtypeStruct((d, t // d, GROUP_W), BF16) for d in DILATIONS])
    outs = _pcall(
        body, name=name, grid=(t // tm,), in_specs=in_specs + tabs, out_specs=out_specs, out_shape=out_shape,
        blocks=[((tm, GROUP_W), BF16)] * 18 + [((tm, 128), F32)] * 3,
        **_stage(tm),
    )(*([proj] * 9), c, sa, sb)
    return outs[0:3], outs[3:6]


def rope_join(dqs, dks, dvs, sb_grads, dgate, tables, *, name, tm=512):
    c = sa = sb = tables
    t = tables.shape[0]
    tm = min(tm, t)

    def body(*refs):
        pieces, sb_refs, dgate_ref = refs[0:9], refs[9:12], refs[12]
        c_ref, sa_ref, sb_ref = refs[13:16]
        o_ref, stage = refs[16], refs[17]
        cv, sav, sbv = c_ref[...], -sa_ref[...], -sb_ref[...]
        for kind in range(3):
            for g, d in enumerate(DILATIONS):
                xv = _join_residues(stage, pieces[3 * kind + g], d)
                if kind < 2:
                    xv = _rotate(xv, cv, sav, sbv)
                col = GROUP_W * (3 * kind + g)
                o_ref[:, col:col + GROUP_W] = xv.astype(BF16)
        for j in range(3):
            o_ref[:, GROUP_W * (QS_BLK + j):GROUP_W * (QS_BLK + j + 1)] = sb_refs[j][...].astype(BF16)
        o_ref[:, D_MODEL * GATE_DIL_BLK:] = dgate_ref[...]

    tabs = [pl.BlockSpec((tm, 128), functools.partial(lambda i, cb: (i, cb), cb=cb)) for cb in range(3)]
    nat = lambda w: pl.BlockSpec((tm, w), lambda i: (i, 0))
    in_specs = [pl.BlockSpec((d, tm // d, GROUP_W), lambda i: (0, i, 0)) for _ in range(3) for d in DILATIONS]
    in_specs += [nat(GROUP_W)] * 3 + [nat(2 * D_MODEL)]
    return _pcall(
        body, name=name, grid=(t // tm,), in_specs=in_specs + tabs,
        out_specs=nat(D_IN), out_shape=jax.ShapeDtypeStruct((t, D_IN), BF16),
        blocks=[((tm, GROUP_W), F32)] * 12 + [((tm, 128), F32)] * 3 + [((tm, 2 * D_MODEL), BF16), ((tm, D_IN), BF16)],
        **_stage(tm),
    )(*dqs, *dks, *dvs, *sb_grads, dgate, c, sa, sb)


def _head_mask(h):
    lane = lax.broadcasted_iota(jnp.int32, (1, GROUP_W), 1)
    return (lane // HEAD_DIM) == h


def _band_mask_before():
    ri = lax.broadcasted_iota(jnp.int32, (4 * DIL_SPAN, DIL_SPAN), 0) % DIL_SPAN
    ci = lax.broadcasted_iota(jnp.int32, (4 * DIL_SPAN, DIL_SPAN), 1)
    return ci >= ri


def dil_fwd(qk, v, *, name):
    d, nsub, _ = qk.shape
    nblk = nsub // DIL_SPAN

    def body(q_ref, kc_ref, kp_ref, vc_ref, vp_ref, o_ref, lse_ref):
        nb = pl.program_id(1)
        kk = jnp.concatenate([kp_ref[0], kc_ref[0]], axis=0)
        vv = jnp.concatenate([vp_ref[0], vc_ref[0]], axis=0)
        s = _dot(_stack_heads(q_ref[0] * ATT_SCALE), kk, "nt")
        ri = lax.broadcasted_iota(jnp.int32, s.shape, 0) % DIL_SPAN
        ci = lax.broadcasted_iota(jnp.int32, s.shape, 1)
        valid = ((ci < DIL_SPAN) & (ci >= ri) & (nb > 0)) | ((ci >= DIL_SPAN) & (ci - DIL_SPAN <= ri))
        s = jnp.where(valid, s, -jnp.inf)
        m = jnp.max(s, axis=1, keepdims=True)
        p = jnp.exp(s - m)
        den = jnp.sum(p, axis=1, keepdims=True)
        o_ref[0] = _unstack_heads(_dot(p, vv, "nn") / den, DIL_SPAN)
        lse = m + jnp.log(den)
        for h in range(4):
            lse_ref[0, :, 128 * h:128 * (h + 1)] = jnp.broadcast_to(lse[DIL_SPAN * h:DIL_SPAN * (h + 1)], (DIL_SPAN, 128))

    blk = (1, DIL_SPAN, GROUP_W)
    sblk = (1, DIL_SPAN, 512)
    prv = lambda nb: jnp.maximum(nb - 1, 0)
    return _pcall(
        body, name=name, grid=(d, nblk),
        in_specs=[pl.BlockSpec(blk, lambda r, nb: (r, nb, 0)),
                  pl.BlockSpec(blk, lambda r, nb: (r, nb, 1)),
                  pl.BlockSpec(blk, lambda r, nb: (r, prv(nb), 1)),
                  pl.BlockSpec(blk, lambda r, nb: (r, nb, 0)),
                  pl.BlockSpec(blk, lambda r, nb: (r, prv(nb), 0))],
        out_specs=[pl.BlockSpec(blk, lambda r, nb: (r, nb, 0)), pl.BlockSpec(sblk, lambda r, nb: (r, nb, 0))],
        out_shape=[jax.ShapeDtypeStruct((d, nsub, GROUP_W), F32), jax.ShapeDtypeStruct((d, nsub, 512), F32)],
        blocks=[(blk, BF16)] * 5 + [(blk, F32), (sblk, F32)],
    )(qk, qk, qk, v, v)


def dil_merge(outs, lses, *, name, tm=512):
    t = outs[0].shape[0] * outs[0].shape[1]
    tm = min(tm, t)

    def body(o0, o1, o2, l0, l1, l2, o_ref, lse_ref, stage):
        ls = [_join_residues(stage, l, d, 0, 512) for l, d in zip((l0, l1, l2), DILATIONS)]
        m = jnp.maximum(jnp.maximum(ls[0], ls[1]), ls[2])
        tot = m + jnp.log(jnp.exp(ls[0] - m) + jnp.exp(ls[1] - m) + jnp.exp(ls[2] - m))
        lse_ref[...] = tot
        lane = lax.broadcasted_iota(jnp.int32, (1, 128), 1)
        first = lane < HEAD_DIM
        acc = jnp.zeros((tm, GROUP_W), F32)
        for og, lg, d in zip((o0, o1, o2), ls, DILATIONS):
            w = jnp.exp(lg - tot)
            wide = jnp.concatenate([jnp.where(first, w[:, 0:128], w[:, 128:256]),
                                    jnp.where(first, w[:, 256:384], w[:, 384:512])], axis=1)
            acc = acc + wide * _join_residues(stage, og, d)
        o_ref[...] = acc

    o_in = [pl.BlockSpec((d, tm // d, GROUP_W), lambda i: (0, i, 0)) for d in DILATIONS]
    l_in = [pl.BlockSpec((d, tm // d, 512), lambda i: (0, i, 0)) for d in DILATIONS]
    return _pcall(
        body, name=name, grid=(t // tm,), in_specs=o_in + l_in,
        out_specs=[pl.BlockSpec((tm, GROUP_W), lambda i: (i, 0)), pl.BlockSpec((tm, 512), lambda i: (i, 0))],
        out_shape=[jax.ShapeDtypeStruct((t, GROUP_W), F32), jax.ShapeDtypeStruct((t, 512), F32)],
        blocks=[((tm, GROUP_W), F32)] * 4 + [((tm, 512), F32)] * 4,
        **_stage(tm),
    )(*outs, *lses)


def dil_bwd_prep(do, o, lse, *, name, tm=512):
    t = do.shape[0]
    tm = min(tm, t)
    wide = DILATIONS[1:]

    def body(do_ref, o_ref, lse_ref, ds_ref, *rest):
        do_out, lse_out, ds_out = rest[0:2], rest[2:4], rest[4:6]
        stage = rest[6]
        dov = do_ref[...]
        prod = dov * o_ref[...]
        for h in range(4):
            s = jnp.sum(jnp.where(_head_mask(h), prod, 0.0), axis=1, keepdims=True)
            ds_ref[:, 128 * h:128 * (h + 1)] = jnp.broadcast_to(s, (tm, 128))
        for i, d in enumerate(wide):
            _split_residues(stage, dov, do_out[i], d, 0, BF16)
            _split_residues(stage, lse_ref[...], lse_out[i], d, 0, F32)
            _split_residues(stage, ds_ref[...], ds_out[i], d, 0, F32)

    nat = lambda w: pl.BlockSpec((tm, w), lambda i: (i, 0))
    res = lambda d, w: pl.BlockSpec((d, tm // d, w), lambda i: (0, i, 0))
    shape = lambda d, w, dt: jax.ShapeDtypeStruct((d, t // d, w), dt)
    outs = _pcall(
        body, name=name, grid=(t // tm,), in_specs=[nat(GROUP_W), nat(GROUP_W), nat(512)],
        out_specs=[nat(512)] + [res(d, GROUP_W) for d in wide] + [res(d, 512) for d in wide] * 2,
        out_shape=([jax.ShapeDtypeStruct((t, 512), F32)] + [shape(d, GROUP_W, BF16) for d in wide]
                   + [shape(d, 512, F32) for d in wide] * 2),
        blocks=[((tm, GROUP_W), F32)] * 3 + [((tm, 512), F32)] * 6,
        **_stage(tm),
    )(do, o, lse)
    return outs[0], outs[1:3], outs[3:5], outs[5:7]


def head_sums(a, b, *, name, tm=512):
    t = a.shape[0]
    tm = min(tm, t)

    def body(a_ref, b_ref, o_ref):
        prod = a_ref[...].astype(BF16).astype(F32) * b_ref[...]
        for h in range(4):
            s = jnp.sum(jnp.where(_head_mask(h), prod, 0.0), axis=1, keepdims=True)
            o_ref[:, 128 * h:128 * (h + 1)] = jnp.broadcast_to(s, (tm, 128))

    spec = pl.BlockSpec((tm, GROUP_W), lambda i: (i, 0))
    return _pcall(
        body, name=name, grid=(t // tm,), in_specs=[spec, spec],
        out_specs=pl.BlockSpec((tm, 512), lambda i: (i, 0)), out_shape=jax.ShapeDtypeStruct((t, 512), F32),
        blocks=[((tm, GROUP_W), F32)] * 2 + [((tm, 512), F32)],
    )(a, b)


def dil_bwd(qk, v, do, lse, dsum, *, name):
    d, nsub, _ = qk.shape
    nblk = nsub // DIL_SPAN

    def body(qa_ref, qb_ref, kc_ref, kp_ref, vc_ref, vp_ref, doa_ref, dob_ref, la_ref, lb_ref, sa_ref, sb_ref,
             dq_ref, dk_ref, dv_ref):
        nb = pl.program_id(1)
        nxt = _band_mask_before() & (nb < nblk - 1)
        kc, kp, vc, vp = kc_ref[0], kp_ref[0], vc_ref[0], vp_ref[0]
        qas, qbs = _stack_heads(qa_ref[0] * ATT_SCALE), _stack_heads(qb_ref[0] * ATT_SCALE)
        das, dbs = _stack_heads(doa_ref[0].astype(BF16)), _stack_heads(dob_ref[0].astype(BF16))
        stat = lambda ref: jnp.concatenate([ref[0, :, 128 * h:128 * (h + 1)] for h in range(4)], axis=0)
        la, lb, sa, sb = stat(la_ref), stat(lb_ref), stat(sa_ref), stat(sb_ref)

        def probs(qs, ds_, k, v, mask, l, s):
            p = jnp.where(mask, jnp.exp(_dot(qs, k, "nt") - l), 0.0)
            dsc = p * (_dot(ds_, v, "nt") - s)
            return p.astype(BF16), dsc.astype(BF16)

        wide = lambda a: jnp.concatenate([a, a], axis=1)
        ri = lax.broadcasted_iota(jnp.int32, (4 * DIL_SPAN, 2 * DIL_SPAN), 0) % DIL_SPAN
        ci = lax.broadcasted_iota(jnp.int32, (4 * DIL_SPAN, 2 * DIL_SPAN), 1)
        valid = ((ci < DIL_SPAN) & (ci >= ri) & (nb > 0)) | ((ci >= DIL_SPAN) & (ci - DIL_SPAN <= ri))
        p_a, ds_a = probs(qas, das, jnp.concatenate([kp, kc], axis=0), jnp.concatenate([vp, vc], axis=0),
                          valid, wide(la), wide(sa))
        p_nc, ds_nc = probs(qbs, dbs, kc, vc, nxt, lb, sb)
        dq_ref[0] = _unstack_heads(_dot(ds_a, jnp.concatenate([kp, kc], axis=0), "nn"), DIL_SPAN) * ATT_SCALE
        dk_ref[0] = _dot(ds_a[:, DIL_SPAN:], qas, "tn") + _dot(ds_nc, qbs, "tn")
        dv_ref[0] = _dot(p_a[:, DIL_SPAN:], das, "tn") + _dot(p_nc, dbs, "tn")

    blk = (1, DIL_SPAN, GROUP_W)
    sblk = (1, DIL_SPAN, 512)
    prv = lambda nb: jnp.maximum(nb - 1, 0)
    nxt_ = lambda nb: jnp.minimum(nb + 1, nblk - 1)
    cur_at = lambda c: pl.BlockSpec(blk, functools.partial(lambda r, nb, c: (r, nb, c), c=c))
    prv_at = lambda c: pl.BlockSpec(blk, functools.partial(lambda r, nb, c: (r, prv(nb), c), c=c))
    nxt_at = lambda c: pl.BlockSpec(blk, functools.partial(lambda r, nb, c: (r, nxt_(nb), c), c=c))
    s_cur = pl.BlockSpec(sblk, lambda r, nb: (r, nb, 0))
    s_nxt = pl.BlockSpec(sblk, lambda r, nb: (r, nxt_(nb), 0))
    o_spec = pl.BlockSpec(blk, lambda r, nb: (r, nb, 0))
    o_shape = jax.ShapeDtypeStruct((d, nsub, GROUP_W), F32)
    return _pcall(
        body, name=name, grid=(d, nblk),
        in_specs=[cur_at(0), nxt_at(0), cur_at(1), prv_at(1), cur_at(0), prv_at(0), cur_at(0), nxt_at(0),
                  s_cur, s_nxt, s_cur, s_nxt],
        out_specs=[o_spec, o_spec, o_spec], out_shape=[o_shape, o_shape, o_shape],
        blocks=[(blk, BF16)] * 6 + [(blk, F32)] * 5 + [(sblk, F32)] * 4,
    )(qk, qk, qk, qk, v, v, do, do, lse, lse, dsum, dsum)


def _tri_dot(x, b):
    hi = x.astype(BF16)
    lo = (x - hi.astype(F32)).astype(BF16)
    return _dot(jnp.concatenate([hi, lo], axis=1), jnp.concatenate([b, b], axis=0), "nn")


SB_TILE = 256
SB_ROWS = 512


def _stack_heads(a):
    return jnp.concatenate([jnp.where(_head_mask(h), a, jnp.zeros_like(a)) for h in range(4)], axis=0)


def _unstack_heads(acc, rows):
    out = acc[0:rows]
    for h in range(1, 4):
        out = jnp.where(_head_mask(h), acc[h * rows:(h + 1) * rows], out)
    return out


def _tri_masks(n):
    ri = lax.broadcasted_iota(jnp.int32, (n, n), 0)
    ci = lax.broadcasted_iota(jnp.int32, (n, n), 1)
    return (ri > ci).astype(BF16), (ri >= ci).astype(BF16)


def _sb_weights(qs, kt, after, c_keep, lead):
    z = _dot(qs, kt, "nt")
    lbeta = jnp.minimum(z, 0.0) - jnp.log(1.0 + jnp.exp(-jnp.abs(z)))
    lkeep = lbeta - z
    past = None
    if lead is not None:
        query = lax.broadcasted_iota(jnp.int32, z.shape, 0) % SB_ROWS
        past = lax.broadcasted_iota(jnp.int32, z.shape, 1) + lead < query
        lkeep = jnp.where(past, lkeep, 0.0)
    w = jnp.exp(lbeta + _tri_dot(lkeep, after) + c_keep)
    if lead is not None:
        w = jnp.where(past, w, 0.0)
    return z, past, lbeta, lkeep, w


def _sb_walk(qb, tile, carry):
    per = SB_ROWS // SB_TILE
    for i in reversed(range(per)):
        carry = tile(pl.multiple_of(qb * SB_ROWS + i * SB_TILE, SB_TILE), i * SB_TILE, i == per - 1, carry)
    past_tiles = qb * per
    return lax.fori_loop(0, past_tiles,
                         lambda it, c: tile(pl.multiple_of((past_tiles - 1 - it) * SB_TILE, SB_TILE), None, False, c), carry)


def sb_fwd(proj, *, name):
    t = proj.shape[0]
    n, m = SB_TILE, SB_ROWS
    assert t % m == 0

    def body(q_ref, k_ref, v_ref, o_ref, acc_ref):
        qb = pl.program_id(0)
        qs = _stack_heads(q_ref[...] * ATT_SCALE)
        after, _ = _tri_masks(n)

        def tile(off, lead, first, c_keep):
            kt = k_ref[pl.ds(off, n), :]
            vt = v_ref[pl.ds(off, n), :]
            _, _, _, lkeep, w = _sb_weights(qs, kt, after, c_keep, lead)
            pv = _tri_dot(w, vt)
            if first:
                acc_ref[...] = pv
            else:
                acc_ref[...] += pv
            return c_keep + jnp.sum(lkeep, axis=1, keepdims=True)

        _sb_walk(qb, tile, jnp.zeros((4 * m, 1), F32))
        o_ref[...] = _unstack_heads(acc_ref[...], m)

    full = lambda cb: pl.BlockSpec((t, GROUP_W), functools.partial(lambda i, cb: (0, cb), cb=cb))
    return _pcall(
        body, name=name, grid=(t // m,),
        in_specs=[pl.BlockSpec((m, GROUP_W), lambda i: (i, QS_BLK)), full(KS_BLK), full(VS_BLK)],
        out_specs=pl.BlockSpec((m, GROUP_W), lambda i: (i, 0)), out_shape=jax.ShapeDtypeStruct((t, GROUP_W), F32),
        blocks=[((m, GROUP_W), BF16), ((t, GROUP_W), BF16), ((t, GROUP_W), BF16), ((m, GROUP_W), F32)],
        scratch_shapes=[pltpu.VMEM((4 * m, GROUP_W), F32)], scratch_bytes=4 * m * GROUP_W * 4,
    )(proj, proj, proj)


def sb_bwd(proj, do, gtot, *, name):
    t = proj.shape[0]
    n, m = SB_TILE, SB_ROWS
    assert t % m == 0

    def body(q_ref, k_ref, v_ref, do_ref, gt_ref, dq_ref, dk_ref, dv_ref, acc_ref):
        qb = pl.program_id(0)

        @pl.when(qb == 0)
        def _():
            dk_ref[...] = jnp.zeros_like(dk_ref)
            dv_ref[...] = jnp.zeros_like(dv_ref)

        qs = _stack_heads(q_ref[...] * ATT_SCALE)
        dos = _stack_heads(do_ref[...].astype(BF16))
        gt = jnp.concatenate([jnp.max(gt_ref[:, 128 * h:128 * (h + 1)], axis=1, keepdims=True) for h in range(4)], axis=0)
        after, from_on = _tri_masks(n)

        def tile(off, lead, first, carry):
            c_keep, c_g = carry
            kt = k_ref[pl.ds(off, n), :]
            vt = v_ref[pl.ds(off, n), :]
            z, past, lbeta, lkeep, w = _sb_weights(qs, kt, after, c_keep, lead)
            gw = w * _dot(dos, vt, "nt")
            big_g = gt - (_tri_dot(gw, from_on) + c_g)
            dz = gw * jnp.exp(lbeta - z) - big_g * jnp.exp(lbeta)
            if lead is not None:
                dz = jnp.where(past, dz, 0.0)
            dz = dz.astype(BF16)
            dk_ref[pl.ds(off, n), :] += _dot(dz, qs, "tn")
            dv_ref[pl.ds(off, n), :] += _dot(w, dos, "tn")
            dq = _dot(dz, kt, "nn")
            if first:
                acc_ref[...] = dq
            else:
                acc_ref[...] += dq
            return c_keep + jnp.sum(lkeep, axis=1, keepdims=True), c_g + jnp.sum(gw, axis=1, keepdims=True)

        zero_col = jnp.zeros((4 * m, 1), F32)
        _sb_walk(qb, tile, (zero_col, zero_col))
        dq_ref[...] = _unstack_heads(acc_ref[...], m) * ATT_SCALE

    full = lambda cb: pl.BlockSpec((t, GROUP_W), functools.partial(lambda i, cb: (0, cb), cb=cb))
    whole = pl.BlockSpec((t, GROUP_W), lambda i: (0, 0))
    rowblk = pl.BlockSpec((m, GROUP_W), lambda i: (i, 0))
    shape = jax.ShapeDtypeStruct((t, GROUP_W), F32)
    return _pcall(
        body, name=name, grid=(t // m,),
        in_specs=[pl.BlockSpec((m, GROUP_W), lambda i: (i, QS_BLK)), full(KS_BLK), full(VS_BLK), rowblk,
                  pl.BlockSpec((m, 512), lambda i: (i, 0))],
        out_specs=[rowblk, whole, whole], out_shape=[shape, shape, shape],
        blocks=[((m, GROUP_W), BF16), ((t, GROUP_W), BF16), ((t, GROUP_W), BF16), ((m, GROUP_W), F32),
                ((m, 512), F32), ((m, GROUP_W), F32), ((t, GROUP_W), F32), ((t, GROUP_W), F32)],
        scratch_shapes=[pltpu.VMEM((4 * m, GROUP_W), F32)], scratch_bytes=4 * m * GROUP_W * 4,
    )(proj, proj, proj, do, gtot)


def _mesh_place():
    return lax.axis_index("x"), lax.axis_index("y"), lax.axis_index("c")


def _flip(place, mask):
    x, y, c = place
    return ((1 - x) if mask & 4 else x, (1 - y) if mask & 2 else y, (1 - c) if mask & 1 else c)


def _dev_index(place):
    x, y, c = place
    return 4 * x + 2 * y + c


HBM_SPEC = pl.BlockSpec(memory_space=pltpu.HBM)


def all_gather_rows(shard, after, *, name):
    rows, lanes = shard.shape

    def body(x_ref, after_ref, out_ref, send_sems, recv_sems, local_sem):
        me = _mesh_place()
        x, y, c = me
        sibling = _flip(me, 1)
        chips = [_flip(me, 4), _flip(me, 2), _flip(me, 6)]

        def copy(k, block, to, src=None):
            dst = out_ref.at[_dev_index(block)]
            return pltpu.make_async_remote_copy(
                src_ref=dst if src is None else src, dst_ref=dst, send_sem=send_sems.at[k], recv_sem=recv_sems.at[k],
                device_id=to, device_id_type=pl.DeviceIdType.MESH)

        mine = pltpu.make_async_copy(x_ref, out_ref.at[_dev_index(me)], local_sem)
        mine.start()
        first = [copy(0, me, sibling, src=x_ref)] + [copy(1 + j, me, chip, src=x_ref) for j, chip in enumerate(chips)]
        for cp in first:
            cp.start()
        passed = [copy(4 + j, chip, sibling) for j, chip in enumerate(chips)]
        for j, chip in enumerate(chips):
            copy(1 + j, chip, me).wait_recv()
            passed[j].start()
        copy(0, sibling, me).wait_recv()
        for j, chip in enumerate(chips):
            copy(4 + j, _flip(chip, 1), me).wait_recv()
        for cp in first + passed:
            cp.wait_send()
        mine.wait()

    return pl.pallas_call(
        body, name=name, in_specs=[HBM_SPEC, pl.BlockSpec(memory_space=pl.ANY)], out_specs=HBM_SPEC,
        out_shape=jax.ShapeDtypeStruct((N_DEV, rows, lanes), shard.dtype),
        scratch_shapes=[pltpu.SemaphoreType.DMA((7,)), pltpu.SemaphoreType.DMA((7,)), pltpu.SemaphoreType.DMA],
    )(shard, after)


SEM_SPEC = pl.BlockSpec(memory_space=pltpu.SEMAPHORE)
DATAFLOW_EFFECT = pltpu.SideEffectType.DATAFLOW_SIDE_EFFECTING


ALL_PEERS = tuple(range(1, N_DEV))
CHIP_PEERS = (1, 4, 2, 6)
OTHER_CHIPS = (4, 2, 6)


def _spread_copies(src_refs, land_refs, send_sems, recv_sems, per_peer, masks, arriving):
    me = _mesh_place()
    my = _dev_index(me)
    remote, local = [], []
    for t, (src_ref, land_ref) in enumerate(zip(src_refs, land_refs)):
        for i, mask in enumerate(masks):
            peer = _flip(me, mask)
            data_of = my if arriving else _dev_index(peer)
            slot = _dev_index(peer) if arriving else my
            k = t * len(masks) + i
            remote.append(pltpu.make_async_remote_copy(
                src_ref=src_ref.at[data_of] if per_peer else src_ref, dst_ref=land_ref.at[slot],
                send_sem=send_sems.at[k], recv_sem=recv_sems.at[k],
                device_id=peer, device_id_type=pl.DeviceIdType.MESH))
        local.append(pltpu.make_async_copy(src_ref.at[my] if per_peer else src_ref, land_ref.at[my],
                                           send_sems.at[len(src_refs) * len(masks) + t]))
    return remote, local


def spread_start(srcs, *, per_peer, name, masks=ALL_PEERS):
    nt = len(srcs)
    zones = [pltpu.HBM((N_DEV,) + (s.shape[1:] if per_peer else s.shape), s.dtype) for s in srcs]

    def body(*refs):
        src_refs, (send_sems, recv_sems) = refs[:nt], refs[nt:nt + 2]
        land_refs, token = refs[2 * nt + 2:3 * nt + 2], refs[3 * nt + 2]
        remote, local = _spread_copies(src_refs, land_refs, send_sems, recv_sems, per_peer, masks, arriving=False)
        for cp in remote + local:
            cp.start()
        token[...] = jnp.zeros_like(token)

    return pl.pallas_call(
        body, name=name, in_specs=(HBM_SPEC,) * nt,
        out_shape=(pltpu.SemaphoreType.DMA((nt * len(masks) + nt,)), pltpu.SemaphoreType.DMA((nt * len(masks),)),
                   *[pltpu.HBM(s.shape, s.dtype) for s in srcs], *zones, jax.ShapeDtypeStruct((8, 128), F32)),
        out_specs=(SEM_SPEC, SEM_SPEC) + (HBM_SPEC,) * (2 * nt) + (pl.BlockSpec(memory_space=pltpu.VMEM),),
        input_output_aliases={t: 2 + t for t in range(nt)},
        compiler_params=pltpu.CompilerParams(has_side_effects=DATAFLOW_EFFECT),
    )(*[pltpu.with_memory_space_constraint(s, pltpu.HBM) for s in srcs])


def spread_wait(started, after, *, per_peer, name, masks=ALL_PEERS):
    nt = (len(started) - 3) // 2
    send_sems, recv_sems = started[0:2]
    srcs_thru, lands_thru = started[2:2 + nt], started[2 + nt:2 + 2 * nt]

    def body(*refs):
        src_refs, land_refs = refs[:nt], refs[nt:2 * nt]
        send_sems, recv_sems = refs[2 * nt:2 * nt + 2]
        remote, local = _spread_copies(src_refs, land_refs, send_sems, recv_sems, per_peer, masks, arriving=True)
        for cp in remote:
            cp.wait_send()
            cp.wait_recv()
        for cp in local:
            cp.wait()

    outs = pl.pallas_call(
        body, name=name, in_specs=(HBM_SPEC,) * (2 * nt) + (SEM_SPEC, SEM_SPEC, pl.BlockSpec(memory_space=pl.ANY)),
        out_shape=tuple(pltpu.HBM(a.shape, a.dtype) for a in (*srcs_thru, *lands_thru)),
        out_specs=(HBM_SPEC,) * (2 * nt), input_output_aliases={t: t for t in range(2 * nt)},
        compiler_params=pltpu.CompilerParams(has_side_effects=DATAFLOW_EFFECT),
    )(*srcs_thru, *lands_thru, send_sems, recv_sems, after)
    return list(outs[nt:])


def _relay_copies(land_refs, send_sems, recv_sems, arriving):
    me = _mesh_place()
    sibling = _flip(me, 1)
    out = []
    for t, land_ref in enumerate(land_refs):
        for i, mask in enumerate(OTHER_CHIPS):
            slot = _dev_index(_flip(sibling if arriving else me, mask))
            k = t * len(OTHER_CHIPS) + i
            out.append(pltpu.make_async_remote_copy(
                src_ref=land_ref.at[slot], dst_ref=land_ref.at[slot], send_sem=send_sems.at[k], recv_sem=recv_sems.at[k],
                device_id=sibling, device_id_type=pl.DeviceIdType.MESH))
    return out


def relay_start(lands, *, name):
    nt = len(lands)
    n_sem = nt * len(OTHER_CHIPS)

    def body(*refs):
        for cp in _relay_copies(refs[:nt], refs[nt], refs[nt + 1], arriving=False):
            cp.start()

    return pl.pallas_call(
        body, name=name, in_specs=(HBM_SPEC,) * nt,
        out_shape=(pltpu.SemaphoreType.DMA((n_sem,)), pltpu.SemaphoreType.DMA((n_sem,)),
                   *[pltpu.HBM(a.shape, a.dtype) for a in lands]),
        out_specs=(SEM_SPEC, SEM_SPEC) + (HBM_SPEC,) * nt, input_output_aliases={t: 2 + t for t in range(nt)},
        compiler_params=pltpu.CompilerParams(has_side_effects=DATAFLOW_EFFECT),
    )(*[pltpu.with_memory_space_constraint(a, pltpu.HBM) for a in lands])


def relay_wait(started, *, name):
    send_sems, recv_sems = started[0:2]
    lands_thru = started[2:]
    nt = len(lands_thru)

    def body(*refs):
        for cp in _relay_copies(refs[:nt], refs[nt], refs[nt + 1], arriving=True):
            cp.wait_send()
            cp.wait_recv()

    return list(pl.pallas_call(
        body, name=name, in_specs=(HBM_SPEC,) * nt + (SEM_SPEC, SEM_SPEC),
        out_shape=tuple(pltpu.HBM(a.shape, a.dtype) for a in lands_thru), out_specs=(HBM_SPEC,) * nt,
        input_output_aliases={t: t for t in range(nt)},
        compiler_params=pltpu.CompilerParams(has_side_effects=DATAFLOW_EFFECT),
    )(*lands_thru, send_sems, recv_sems))


def sum_partials(parts, *, name, tr):
    _, rows, lanes = parts.shape
    assert rows % tr == 0

    def body(p_ref, g_ref):
        g = p_ref[0].astype(F32)
        for k in range(1, N_DEV):
            g = g + p_ref[k].astype(F32)
        g_ref[...] = g

    return _pcall(
        body, name=name, grid=(rows // tr,),
        in_specs=[pl.BlockSpec((N_DEV, tr, lanes), lambda i: (0, i, 0))],
        out_specs=pl.BlockSpec((tr, lanes), lambda i: (i, 0)), out_shape=jax.ShapeDtypeStruct((rows, lanes), F32),
        blocks=[((N_DEV, tr, lanes), parts.dtype), ((tr, lanes), F32)],
    )(parts)


def adamw(g, w, m, v, *, name, tr):
    nl, k, n = w.shape
    tr = max(c for c in range(8, min(tr, k) + 1, 8) if k % c == 0)
    bc1 = 1.0 - ADAM_B1 ** ADAM_STEP
    bc2 = 1.0 - ADAM_B2 ** ADAM_STEP

    def body(g_ref, w_ref, m_ref, v_ref, d_ref, mo_ref, vo_ref):
        gv = g_ref[...]
        m_new = ADAM_B1 * m_ref[...] + (1.0 - ADAM_B1) * gv
        v_new = ADAM_B2 * v_ref[...] + (1.0 - ADAM_B2) * (gv * gv)
        mo_ref[...] = m_new
        vo_ref[...] = v_new
        d_ref[...] = -ADAM_LR * ((m_new / bc1) / (jnp.sqrt(v_new / bc2) + ADAM_EPS) + ADAM_WD * w_ref[...])

    spec = pl.BlockSpec((1, tr, n), lambda l, i: (l, i, 0))
    shape = jax.ShapeDtypeStruct(w.shape, F32)
    return _pcall(
        body, name=name, grid=(nl, k // tr), in_specs=[spec] * 4, out_specs=[spec] * 3, out_shape=[shape] * 3,
        blocks=[((1, tr, n), F32)] * 7,
    )(g, w, m, v)


def sum_adamw(partials, w, m, v, *, name, tr, transposed=False):
    nl, k, n = w.shape
    assert nl == len(partials) == 2
    step = 128 if transposed else 8
    tr = max(c for c in range(step, min(tr, k) + 1, step) if k % c == 0)
    bc1 = 1.0 - ADAM_B1 ** ADAM_STEP
    bc2 = 1.0 - ADAM_B2 ** ADAM_STEP

    def body(p0_ref, p1_ref, w_ref, m_ref, v_ref, g_ref, d_ref, mo_ref, vo_ref):
        first = pl.program_id(0) == 0
        gv = jnp.where(first, p0_ref[0], p1_ref[0]).astype(F32)
        for s in range(1, N_DEV):
            gv = gv + jnp.where(first, p0_ref[s], p1_ref[s]).astype(F32)
        if transposed:
            gv = gv.T
        m_new = ADAM_B1 * m_ref[0] + (1.0 - ADAM_B1) * gv
        v_new = ADAM_B2 * v_ref[0] + (1.0 - ADAM_B2) * (gv * gv)
        g_ref[0] = gv
        mo_ref[0] = m_new
        vo_ref[0] = v_new
        d_ref[0] = -ADAM_LR * ((m_new / bc1) / (jnp.sqrt(v_new / bc2) + ADAM_EPS) + ADAM_WD * w_ref[0])

    spec = pl.BlockSpec((1, tr, n), lambda l, i: (l, i, 0))
    if transposed:
        p0spec = pl.BlockSpec((N_DEV, n, tr), lambda l, i: (0, 0, i * (1 - l)))
        p1spec = pl.BlockSpec((N_DEV, n, tr), lambda l, i: (0, 0, i * l))
    else:
        p0spec = pl.BlockSpec((N_DEV, tr, n), lambda l, i: (0, i * (1 - l), 0))
        p1spec = pl.BlockSpec((N_DEV, tr, n), lambda l, i: (0, i * l, 0))
    shape = jax.ShapeDtypeStruct(w.shape, F32)
    return _pcall(
        body, name=name, grid=(nl, k // tr), in_specs=[p0spec, p1spec, spec, spec, spec], out_specs=[spec] * 4,
        out_shape=[shape] * 4, blocks=[((N_DEV, tr, n), BF16)] * 2 + [((1, tr, n), F32)] * 7,
    )(partials[0], partials[1], w, m, v)


def travelling(a, by_cols):
    return jnp.swapaxes(a, -1, -2) if by_cols else a


def _row(v):
    return v.reshape(1, -1)


def ffn_fwd(x, h, w, pre, tag, next_gain):
    ga, gb, s = swiglu_fwd(h, w[pre + "_w_gate"], w[pre + "_w_up"], name=f"{tag}_gateup")
    if callable(w[pre + "_w_down"]):
        w[pre + "_w_down"] = w[pre + "_w_down"](s)
    out, h_next = matmul_res_norm(s, w[pre + "_w_down"], x, next_gain, scale=0.5, tm=512, name=f"{tag}_down")
    return out, h_next, (x, h, ga, gb, s)


def ffn_bwd_weights(dxb, saved, w, pre, tag):
    x, h, a, b, s = saved
    da, db = swiglu_bwd(dxb, w[pre + "_w_down"], a, b, scale=0.5, name=f"{tag}_dgateup")
    g_down = matmul(s, dxb, "tn", tm=1408, tn=1024, tk=2048, out_dtype=BF16, scale=0.5, name=f"{tag}_gdown")
    g_gate = matmul(da, h, "tn", tm=1408, tn=1024, tk=2048, out_dtype=BF16, name=f"{tag}_ggate")
    g_up = matmul(db, h, "tn", tm=1408, tn=1024, tk=2048, out_dtype=BF16, name=f"{tag}_gup")
    return {pre + "_w_gate": g_gate, pre + "_w_up": g_up, pre + "_w_down": g_down}, (da, db)


def ffn_bwd_input(dx, rest, saved, gain, w, pre, tag):
    da, db = rest
    x = saved[0]
    return matmul_rms_bwd([(da, w[pre + "_w_gate"]), (db, w[pre + "_w_up"])], x, gain, dx, tm=256, name=f"{tag}_dh")


def mixer_fwd(x, h, w, tables, tag, next_gain):
    proj = matmul(h, w["w_in"], "nt", tm=512, tn=1280, tk=1024, out_dtype=BF16, name=f"{tag}_in")
    qks, vs = rope_split(proj, tables, name=f"{tag}_rope")
    outs, lses = [], []
    for g in range(N_DIL_GROUPS):
        o, lse = dil_fwd(qks[g], vs[g], name=f"{tag}_dil{g}")
        outs.append(o)
        lses.append(lse)
    odil, lse = dil_merge(outs, lses, name=f"{tag}_merge")
    osb = sb_fwd(proj, name=f"{tag}_sb")
    y, u1, u2 = gate_fwd(odil, osb, w["w_proj_dil"], w["w_proj_sb"], proj, name=f"{tag}_gate")
    out, h_next = matmul_res_norm(y, w["w_out"], x, next_gain, scale=1.0, tm=512, name=f"{tag}_out")
    return out, h_next, (x, h, proj, qks, vs, odil, lse, osb, u1, u2, y)


def mixer_bwd_weights(dxb, saved, w, tables, tag):
    x, h, proj, qks, vs, odil, lse, osb, u1, u2, y = saved
    t = x.shape[0]
    g_out = matmul(y, dxb, "tn", tm=1024, tn=1024, tk=2048, out_dtype=BF16, name=f"{tag}_gout")
    du1, du2, dgate = gate_bwd(dxb, w["w_out"], u1, u2, proj, name=f"{tag}_dgate")
    g_pd = matmul(du1, odil, "tn", tm=1024, tn=256, tk=2048, out_dtype=BF16, name=f"{tag}_gpd")
    g_ps = matmul(du2, osb, "tn", tm=1024, tn=256, tk=2048, out_dtype=BF16, name=f"{tag}_gps")
    dodil = matmul(du1, w["w_proj_dil"], "nn", tm=512, tn=256, tk=1024, out_dtype=F32, name=f"{tag}_dodil")
    dosb = matmul(du2, w["w_proj_sb"], "nn", tm=512, tn=256, tk=1024, out_dtype=F32, name=f"{tag}_dosb")
    dsum, do_wide, lse_wide, dsum_wide = dil_bwd_prep(dodil, odil, lse, name=f"{tag}_dprep")
    dos = [dodil[None]] + list(do_wide)
    lss = [lse[None]] + list(lse_wide)
    dss = [dsum[None]] + list(dsum_wide)
    dqs, dks, dvs = [], [], []
    for g in range(N_DIL_GROUPS):
        dq, dk, dv = dil_bwd(qks[g], vs[g], dos[g], lss[g], dss[g], name=f"{tag}_ddil{g}")
        dqs.append(dq)
        dks.append(dk)
        dvs.append(dv)
    gtot = head_sums(dosb, osb, name=f"{tag}_gsum")
    sb_grads = sb_bwd(proj, dosb, gtot, name=f"{tag}_dsb")
    dproj = rope_join(dqs, dks, dvs, sb_grads, dgate, tables, name=f"{tag}_drope")
    g_in = matmul(dproj, h, "tn", tm=1280, tn=1024, tk=2048, out_dtype=BF16, name=f"{tag}_gin")
    return {"w_in": g_in, "w_proj_dil": g_pd, "w_proj_sb": g_ps, "w_out": g_out}, dproj


def mixer_bwd_input(dx, dproj, saved, gain, w, tag):
    x = saved[0]
    return matmul_rms_bwd([(dproj, w["w_in"])], x, gain, dx, tm=256, name=f"{tag}_dh")


def kernel(x, norm_ffn1, ffn1_w_gate, ffn1_w_up, ffn1_w_down, norm_mix, w_in, w_proj_dil, w_proj_sb, w_out, norm_ffn2, ffn2_w_gate, ffn2_w_up, ffn2_w_down, norm_final, loss_target, m_norm_ffn1, m_ffn1_w_gate, m_ffn1_w_up, m_ffn1_w_down, m_norm_mix, m_w_in, m_w_proj_dil, m_w_proj_sb, m_w_out, m_norm_ffn2, m_ffn2_w_gate, m_ffn2_w_up, m_ffn2_w_down, m_norm_final, v_norm_ffn1, v_ffn1_w_gate, v_ffn1_w_up, v_ffn1_w_down, v_norm_mix, v_w_in, v_w_proj_dil, v_w_proj_sb, v_w_out, v_norm_ffn2, v_ffn2_w_gate, v_ffn2_w_up, v_ffn2_w_down, v_norm_final):
    args = dict(locals())
    t = x.shape[1]
    xs = x.reshape(t, D_MODEL)
    target = loss_target.reshape(t, D_MODEL)
    tables = rope_tables(t)

    parts = [(l, p) for l in range(2) for p in SUBBLOCKS]
    gains = {n: args[n] for n in NORM_ROWS}

    shipments = []
    for l, p in parts:
        if (l, p) == parts[0]:
            shipments += [(l, p, SUBBLOCKS[p][:2], CHIP_PEERS), (l, p, SUBBLOCKS[p][2:], ALL_PEERS)]
        else:
            shipments.append((l, p, SUBBLOCKS[p], ALL_PEERS))
    in_flight, order_token = [], jnp.zeros((1, 1), F32)
    for l, p, tensors, masks in shipments:
        shards = [travelling(args[n][l], by_cols).astype(BF16) for n, by_cols in tensors]
        shards[0] = shards[0] + order_token.astype(BF16)
        in_flight.append(spread_start(shards, per_peer=False, masks=masks, name=f"gather_start_l{l}_{tensors[0][0]}"))
        order_token = in_flight[-1][-1][0:1, 0:1]

    def arrived(i, after):
        l, p, tensors, masks = shipments[i]
        tag = f"l{l}_{tensors[0][0]}"
        lands = spread_wait(in_flight[i], after, per_peer=False, masks=masks, name=f"gather_wait_{tag}")
        if masks is CHIP_PEERS:
            lands = relay_wait(relay_start(lands, name=f"gather_relay_{tag}"), name=f"gather_relayed_{tag}")
        return {n: land.reshape(-1, land.shape[-1]) for (n, _), land in zip(tensors, lands)}

    def weights_of(l, p, after):
        mine = [i for i, s in enumerate(shipments) if s[0:2] == (l, p)]
        w = arrived(mine[0], after)
        for i in mine[1:]:
            for n, _ in shipments[i][2]:
                w[n] = functools.partial(lambda after, i, n: arrived(i, after)[n], i=i, n=n)
        return w

    saved, weights = {}, {}
    act = xs
    h = rms_fwd(xs, _row(gains["norm_ffn1"][0]) + order_token, name="l0_ffn1_norm")
    for i, (l, p) in enumerate(parts):
        weights[(l, p)] = weights_of(l, p, h if i == 0 else act)
        nl, np_ = parts[i + 1] if i + 1 < len(parts) else (None, None)
        next_gain = _row(gains["norm_" + np_][nl]) if np_ else None
        if p == "mix":
            act, h, saved[(l, p)] = mixer_fwd(act, h, weights[(l, p)], tables, f"l{l}_mix", next_gain)
        else:
            act, h, saved[(l, p)] = ffn_fwd(act, h, weights[(l, p)], p, f"l{l}_{p}", next_gain)
    dx, dxb, g_final, loss_part = final_loss(act, _row(norm_final), target, name="loss_head")

    gain_grads, sent = {}, {}
    order_token = jnp.zeros((1, 1), F32)
    for l, p in reversed(parts):
        w, sv = weights[(l, p)], saved[(l, p)]
        if p == "mix":
            gw, rest = mixer_bwd_weights(dxb, sv, w, tables, f"l{l}_mix")
        else:
            gw, rest = ffn_bwd_weights(dxb, sv, w, p, f"l{l}_{p}")
        slices = [gw[n].reshape(N_DEV, -1, gw[n].shape[-1]) for n, _ in SUBBLOCKS[p]]
        sent[(l, p)] = spread_start(slices, per_peer=True, name=f"reduce_start_l{l}_{p}")
        gain = _row(gains["norm_" + p][l]) + sent[(l, p)][-1][0:1, 0:1]
        if p == "mix":
            dx, dxb, gain_grads[("norm_mix", l)] = mixer_bwd_input(dx, rest, sv, gain, w, f"l{l}_mix")
        else:
            dx, dxb, gain_grads[("norm_" + p, l)] = ffn_bwd_input(dx, rest, sv, gain, w, p, f"l{l}_{p}")

    partials, big_all = {}, [{}, {}, {}, {}]

    def receive(l, p, after):
        lands = spread_wait(sent[(l, p)], after, per_peer=True, name=f"reduce_wait_l{l}_{p}")
        for (n, _), land in zip(SUBBLOCKS[p], lands):
            partials.setdefault(n, [None, None])[l] = land

    def update(p):
        for n, by_cols in SUBBLOCKS[p]:
            if by_cols and args[n].shape[-1] % 128 == 0:
                outs = sum_adamw(partials[n], args[n], args["m_" + n], args["v_" + n], tr=256, transposed=True,
                                 name=f"update_{n}")
                for kind, arr in enumerate(outs):
                    big_all[kind][n] = arr
            else:
                outs = sum_adamw(partials[n], travelling(args[n], by_cols), travelling(args["m_" + n], by_cols),
                                 travelling(args["v_" + n], by_cols), tr=256, name=f"update_{n}")
                for kind, arr in enumerate(outs):
                    big_all[kind][n] = travelling(arr, by_cols)
        return outs[1]

    for l, p in reversed(parts[1:]):
        receive(l, p, dx)
    update("ffn2")
    done = update("mix")
    receive(*parts[0], done)
    done = update("ffn1")

    loss_row = jnp.pad(loss_part[:, :1], ((0, 0), (0, D_MODEL - 1)))
    small = jnp.concatenate([gain_grads[(n, l)] for n in NORM_ROWS for l in range(2)] + [g_final, loss_row], axis=0)
    small_g = sum_partials(all_gather_rows(small, done, name="gather_gain_grads"), tr=8, name="sum_gain_grads")
    zero_row = jnp.zeros((1, D_MODEL), F32)
    small_of = lambda pre: jnp.concatenate([args[pre + n] for n in NORM_ROWS] + [_row(args[pre + "norm_final"]), zero_row], axis=0)[None]
    small_out = adamw(small_g[None], small_of(""), small_of("m_"), small_of("v_"), tr=8, name="update_gains")
    small_all = [small_g] + [o[0] for o in small_out]

    def gains_of(s):
        out = {n: s[2 * i:2 * i + 2] for i, n in enumerate(NORM_ROWS)}
        out["norm_final"] = s[6]
        return out

    order = ["norm_ffn1", "ffn1_w_gate", "ffn1_w_up", "ffn1_w_down", "norm_mix", "w_in", "w_proj_dil", "w_proj_sb", "w_out",
             "norm_ffn2", "ffn2_w_gate", "ffn2_w_up", "ffn2_w_down", "norm_final"]
    results = []
    for kind in range(4):
        both = {**big_all[kind], **gains_of(small_all[kind])}
        results += [both[n] for n in order]
    loss = small_g[7, 0]
    return (loss, dx.reshape(1, t, D_MODEL), *results)
```

```python
import functools

import jax
import jax.numpy as jnp
from jax import lax
from jax.experimental import pallas as pl
from jax.experimental.pallas import tpu as pltpu

F32 = jnp.float32
BF16 = jnp.bfloat16

D_MODEL = 1024
HEAD_DIM = 64
GROUP_W = 256
D_IN = 5120
N_DIL_GROUPS = 3
DIL_SPAN = 128
DILATIONS = (1, 4, 16)
ROPE_THETA = 500000.0
ROPE_DIM = 16
RMS_EPS = 1e-6
ATT_SCALE = HEAD_DIM ** -0.5
QS_BLK, KS_BLK, VS_BLK = 9, 10, 11
GATE_DIL_BLK, GATE_SB_BLK = 3, 4

ADAM_LR, ADAM_B1, ADAM_B2, ADAM_EPS, ADAM_WD, ADAM_STEP = 0.001, 0.9, 0.999, 1e-08, 0.01, 10

N_DEV = 8
VMEM_PHYSICAL_V7X = 64 << 20
VMEM_TEMP_HEADROOM = 20 << 20

SUBBLOCKS = {
    "ffn1": (("ffn1_w_gate", True), ("ffn1_w_up", True), ("ffn1_w_down", False)),
    "mix": (("w_in", True), ("w_proj_dil", True), ("w_proj_sb", True), ("w_out", False)),
    "ffn2": (("ffn2_w_gate", True), ("ffn2_w_up", True), ("ffn2_w_down", False)),
}
NORM_ROWS = ("norm_ffn1", "norm_mix", "norm_ffn2")


def _nbytes(shape, dtype):
    n = 1
    for s in shape:
        n *= s
    return n * jnp.dtype(dtype).itemsize


def _pcall(body, *, name, grid, in_specs, out_specs, out_shape, blocks, scratch_shapes=(), scratch_bytes=0):
    need = 2 * sum(_nbytes(s, d) for s, d in blocks) + scratch_bytes + VMEM_TEMP_HEADROOM
    limit = min(need, VMEM_PHYSICAL_V7X - (4 << 20))
    in_hbm = lambda s: pltpu.HBM(s.shape, s.dtype)
    out_shape = [in_hbm(s) for s in out_shape] if isinstance(out_shape, (list, tuple)) else in_hbm(out_shape)
    call = pl.pallas_call(
        body, name=name, grid=grid, in_specs=in_specs, out_specs=out_specs, out_shape=out_shape,
        scratch_shapes=scratch_shapes,
        compiler_params=pltpu.CompilerParams(vmem_limit_bytes=limit),
    )
    return lambda *args: call(*[pltpu.with_memory_space_constraint(a, pltpu.HBM) for a in args])


def _dot(a, b, form):
    dn = {"nn": (((1,), (0,)), ((), ())), "nt": (((1,), (1,)), ((), ())), "tn": (((0,), (0,)), ((), ()))}[form]
    return lax.dot_general(a.astype(BF16), b.astype(BF16), dn, preferred_element_type=F32)


def _sigmoid(x):
    return 1.0 / (1.0 + jnp.exp(-x))


def matmul(a, b, form, *, tm, tn, tk, out_dtype, name, scale=1.0, after=None):
    if form == "tn":
        kdim, m = a.shape
        n = b.shape[1]
    else:
        m, kdim = a.shape
        n = b.shape[1] if form == "nn" else b.shape[0]
    tm, tn, tk = min(tm, m), min(tn, n), min(tk, kdim)
    assert m % tm == 0 and n % tn == 0 and kdim % tk == 0, (name, m, n, kdim, tm, tn, tk)
    nk = kdim // tk

    if form == "tn":
        a_blk, a_map = (tk, tm), (lambda j, i, k: (k, i))
    else:
        a_blk, a_map = (tm, tk), (lambda j, i, k: (i, k))
    if form == "nt":
        b_blk, b_map = (tn, tk), (lambda j, i, k: (j, k))
    else:
        b_blk, b_map = (tk, tn), (lambda j, i, k: (k, j))
    o_map = lambda j, i, k: (i, j)

    def body(a_ref, b_ref, *rest):
        o_ref, acc = (rest[1], rest[2:]) if after is not None else (rest[0], rest[1:])

        def finish(total):
            o_ref[...] = (total * scale if scale != 1.0 else total).astype(out_dtype)

        if nk == 1:
            finish(_dot(a_ref[...], b_ref[...], form))
        else:
            acc_ref, = acc
            k = pl.program_id(2)

            @pl.when(k == 0)
            def _():
                acc_ref[...] = _dot(a_ref[...], b_ref[...], form)

            @pl.when(k > 0)
            def _():
                acc_ref[...] += _dot(a_ref[...], b_ref[...], form)

            @pl.when(k == nk - 1)
            def _():
                finish(acc_ref[...])

    scratch = [pltpu.VMEM((tm, tn), F32)] if nk > 1 else []
    in_specs = [pl.BlockSpec(a_blk, a_map), pl.BlockSpec(b_blk, b_map)]
    args = [a, b]
    if after is not None:
        in_specs.append(pl.BlockSpec(memory_space=pl.ANY))
        args.append(after)
    return _pcall(
        body, name=name, grid=(n // tn, m // tm, nk), in_specs=in_specs,
        out_specs=pl.BlockSpec((tm, tn), o_map), out_shape=jax.ShapeDtypeStruct((m, n), out_dtype),
        blocks=[(a_blk, a.dtype), (b_blk, b.dtype), ((tm, tn), out_dtype)],
        scratch_shapes=scratch, scratch_bytes=(tm * tn * 4 if nk > 1 else 0),
    )(*args)


def swiglu_fwd(h, wg_t, wu_t, *, name, tm=512, tn=1408):
    t, d = h.shape
    f = wg_t.shape[0]
    tm, tn = min(tm, t), min(tn, f)

    def body(h_ref, wg_ref, wu_ref, ga_ref, gb_ref, s_ref):
        hh = h_ref[...]
        a = _dot(hh, wg_ref[...], "nt")
        b = _dot(hh, wu_ref[...], "nt")
        sg = _sigmoid(a)
        silu = a * sg
        ga_ref[...] = (b * (sg * (1.0 + a * (1.0 - sg)))).astype(BF16)
        gb_ref[...] = silu.astype(BF16)
        s_ref[...] = (silu * b).astype(BF16)

    w_spec = pl.BlockSpec((tn, d), lambda j, i: (j, 0))
    o_spec = pl.BlockSpec((tm, tn), lambda j, i: (i, j))
    o_shape = jax.ShapeDtypeStruct((t, f), BF16)
    return _pcall(
        body, name=name, grid=(f // tn, t // tm),
        in_specs=[pl.BlockSpec((tm, d), lambda j, i: (i, 0)), w_spec, w_spec],
        out_specs=[o_spec, o_spec, o_spec], out_shape=[o_shape, o_shape, o_shape],
        blocks=[((tm, d), BF16), ((tn, d), BF16), ((tn, d), BF16)] + [((tm, tn), BF16)] * 3,
    )(h, wg_t, wu_t)


def swiglu_bwd(dyb, wd, ga, gb, *, name, scale, tm=512, tn=1408):
    t, d = dyb.shape
    f = wd.shape[0]
    tm, tn = min(tm, t), min(tn, f)

    def body(dy_ref, wd_ref, ga_ref, gb_ref, da_ref, db_ref):
        ds = _dot(dy_ref[...], wd_ref[...], "nt") * scale
        da_ref[...] = (ds * ga_ref[...].astype(F32)).astype(BF16)
        db_ref[...] = (ds * gb_ref[...].astype(F32)).astype(BF16)

    o_spec = pl.BlockSpec((tm, tn), lambda j, i: (i, j))
    o_shape = jax.ShapeDtypeStruct((t, f), BF16)
    return _pcall(
        body, name=name, grid=(f // tn, t // tm),
        in_specs=[pl.BlockSpec((tm, d), lambda j, i: (i, 0)), pl.BlockSpec((tn, d), lambda j, i: (j, 0)), o_spec, o_spec],
        out_specs=[o_spec, o_spec], out_shape=[o_shape, o_shape],
        blocks=[((tm, d), BF16), ((tn, d), BF16)] + [((tm, tn), BF16)] * 4,
    )(dyb, wd, ga, gb)


def gate_fwd(odil, osb, wpd_t, wps_t, proj, *, name, tm=512):
    t = odil.shape[0]
    tm = min(tm, t)

    def body(od_ref, os_ref, wpd_ref, wps_ref, g1_ref, g2_ref, y_ref, u1_ref, u2_ref):
        u1 = _dot(od_ref[...], wpd_ref[...], "nt")
        u2 = _dot(os_ref[...], wps_ref[...], "nt")
        y = _sigmoid(g1_ref[...].astype(F32)) * u1 + _sigmoid(g2_ref[...].astype(F32)) * u2
        y_ref[...] = y.astype(BF16)
        u1_ref[...] = u1.astype(BF16)
        u2_ref[...] = u2.astype(BF16)

    o_spec = pl.BlockSpec((tm, D_MODEL), lambda i: (i, 0))
    w_spec = pl.BlockSpec((D_MODEL, GROUP_W), lambda i: (0, 0))
    a_spec = pl.BlockSpec((tm, GROUP_W), lambda i: (i, 0))
    o_shape = jax.ShapeDtypeStruct((t, D_MODEL), BF16)
    return _pcall(
        body, name=name, grid=(t // tm,),
        in_specs=[a_spec, a_spec, w_spec, w_spec,
                  pl.BlockSpec((tm, D_MODEL), lambda i: (i, GATE_DIL_BLK)),
                  pl.BlockSpec((tm, D_MODEL), lambda i: (i, GATE_SB_BLK))],
        out_specs=[o_spec, o_spec, o_spec], out_shape=[o_shape, o_shape, o_shape],
        blocks=[((tm, GROUP_W), F32)] * 2 + [((D_MODEL, GROUP_W), BF16)] * 2 + [((tm, D_MODEL), BF16)] * 5,
    )(odil, osb, wpd_t, wps_t, proj, proj)


def gate_bwd(dxb, wout, u1, u2, proj, *, name, tm=512):
    t = dxb.shape[0]
    tm = min(tm, t)

    def body(dx_ref, w_ref, u1_ref, u2_ref, g1_ref, g2_ref, du1_ref, du2_ref, dg_ref):
        dy = _dot(dx_ref[...], w_ref[...], "nt")
        s1 = _sigmoid(g1_ref[...].astype(F32))
        s2 = _sigmoid(g2_ref[...].astype(F32))
        du1_ref[...] = (dy * s1).astype(BF16)
        du2_ref[...] = (dy * s2).astype(BF16)
        dg_ref[:, :D_MODEL] = (dy * u1_ref[...].astype(F32) * s1 * (1.0 - s1)).astype(BF16)
        dg_ref[:, D_MODEL:] = (dy * u2_ref[...].astype(F32) * s2 * (1.0 - s2)).astype(BF16)

    o_spec = pl.BlockSpec((tm, D_MODEL), lambda i: (i, 0))
    o_shape = jax.ShapeDtypeStruct((t, D_MODEL), BF16)
    return _pcall(
        body, name=name, grid=(t // tm,),
        in_specs=[o_spec, pl.BlockSpec((D_MODEL, D_MODEL), lambda i: (0, 0)), o_spec, o_spec,
                  pl.BlockSpec((tm, D_MODEL), lambda i: (i, GATE_DIL_BLK)),
                  pl.BlockSpec((tm, D_MODEL), lambda i: (i, GATE_SB_BLK))],
        out_specs=[o_spec, o_spec, pl.BlockSpec((tm, 2 * D_MODEL), lambda i: (i, 0))],
        out_shape=[o_shape, o_shape, jax.ShapeDtypeStruct((t, 2 * D_MODEL), BF16)],
        blocks=[((tm, D_MODEL), BF16)] * 9 + [((D_MODEL, D_MODEL), BF16)],
    )(dxb, wout, u1, u2, proj, proj)


def rms_fwd(x, gain, *, name, tm=512):
    t, d = x.shape
    tm = min(tm, t)

    def body(x_ref, g_ref, h_ref):
        xv = x_ref[...]
        rstd = lax.rsqrt(jnp.mean(xv * xv, axis=1, keepdims=True) + RMS_EPS)
        h_ref[...] = (xv * rstd * g_ref[...]).astype(BF16)

    return _pcall(
        body, name=name, grid=(t // tm,),
        in_specs=[pl.BlockSpec((tm, d), lambda i: (i, 0)), pl.BlockSpec((1, d), lambda i: (0, 0))],
        out_specs=pl.BlockSpec((tm, d), lambda i: (i, 0)), out_shape=jax.ShapeDtypeStruct((t, d), BF16),
        blocks=[((tm, d), F32), ((tm, d), BF16)],
    )(x, gain)


def matmul_res_norm(a, b, res, next_gain, *, scale, tm, name):
    t, k = a.shape
    d = b.shape[1]
    tm = min(tm, t)
    with_norm = next_gain is not None

    def body(a_ref, b_ref, r_ref, *rest):
        out = r_ref[...] + _dot(a_ref[...], b_ref[...], "nn") * scale
        if with_norm:
            g_ref, o_ref, h_ref = rest
            rstd = lax.rsqrt(jnp.mean(out * out, axis=1, keepdims=True) + RMS_EPS)
            h_ref[...] = (out * rstd * g_ref[...]).astype(BF16)
        else:
            o_ref, = rest
        o_ref[...] = out

    row = pl.BlockSpec((tm, d), lambda i: (i, 0))
    in_specs = [pl.BlockSpec((tm, k), lambda i: (i, 0)), pl.BlockSpec((k, d), lambda i: (0, 0)), row]
    args = [a, b, res]
    out_specs, out_shape = [row], [jax.ShapeDtypeStruct((t, d), F32)]
    if with_norm:
        in_specs.append(pl.BlockSpec((1, d), lambda i: (0, 0)))
        args.append(next_gain)
        out_specs.append(row)
        out_shape.append(jax.ShapeDtypeStruct((t, d), BF16))
    outs = _pcall(
        body, name=name, grid=(t // tm,), in_specs=in_specs, out_specs=out_specs, out_shape=out_shape,
        blocks=[((tm, k), a.dtype), ((k, d), b.dtype), ((tm, d), F32), ((tm, d), F32), ((tm, d), BF16)],
    )(*args)
    return (outs[0], outs[1]) if with_norm else (outs[0], None)


def _rms_bwd_rows(dhv, xv, g, drv):
    rstd = lax.rsqrt(jnp.mean(xv * xv, axis=1, keepdims=True) + RMS_EPS)
    xh = xv * rstd
    dxh = dhv * g
    dx = drv + rstd * (dxh - xh * jnp.mean(dxh * xh, axis=1, keepdims=True))
    return dx, jnp.sum(dhv * xh, axis=0, keepdims=True)


def matmul_rms_bwd(pairs, x, gain, dres, *, tm, name):
    t, d = x.shape
    tm = min(tm, t)
    npairs = len(pairs)

    def body(*refs):
        ab = refs[:2 * npairs]
        x_ref, g_ref, dr_ref, dx_ref, dxb_ref, dg_ref = refs[2 * npairs:]
        dh = _dot(ab[0][...], ab[1][...], "nn")
        for q in range(1, npairs):
            dh = dh + _dot(ab[2 * q][...], ab[2 * q + 1][...], "nn")
        dx, part = _rms_bwd_rows(dh, x_ref[...], g_ref[...], dr_ref[...])
        dx_ref[...] = dx
        dxb_ref[...] = dx.astype(BF16)

        @pl.when(pl.program_id(0) == 0)
        def _():
            dg_ref[...] = part

        @pl.when(pl.program_id(0) > 0)
        def _():
            dg_ref[...] += part

    in_specs, args, blocks = [], [], []
    for a, b in pairs:
        k = a.shape[1]
        in_specs += [pl.BlockSpec((tm, k), lambda i: (i, 0)), pl.BlockSpec((k, d), lambda i: (0, 0))]
        args += [a, b]
        blocks += [((tm, k), a.dtype), ((k, d), b.dtype)]
    row = pl.BlockSpec((tm, d), lambda i: (i, 0))
    vec = pl.BlockSpec((1, d), lambda i: (0, 0))
    return _pcall(
        body, name=name, grid=(t // tm,), in_specs=in_specs + [row, vec, row], out_specs=[row, row, vec],
        out_shape=[jax.ShapeDtypeStruct((t, d), F32), jax.ShapeDtypeStruct((t, d), BF16), jax.ShapeDtypeStruct((1, d), F32)],
        blocks=blocks + [((tm, d), F32)] * 3 + [((tm, d), BF16)],
    )(*args, x, gain, dres)


def final_loss(x, gain, target, *, name, tm=512):
    t, d = x.shape
    tm = min(tm, t)

    def body(x_ref, g_ref, t_ref, dx_ref, dxb_ref, dg_ref, loss_ref):
        xv = x_ref[...]
        g = g_ref[...]
        rstd = lax.rsqrt(jnp.mean(xv * xv, axis=1, keepdims=True) + RMS_EPS)
        xh = xv * rstd
        err = xh * g - t_ref[...]
        dy = err * (1.0 / d)
        dxh = dy * g
        dx = rstd * (dxh - xh * jnp.mean(dxh * xh, axis=1, keepdims=True))
        dx_ref[...] = dx
        dxb_ref[...] = dx.astype(BF16)
        part = jnp.sum(dy * xh, axis=0, keepdims=True)
        sq = jnp.sum(jnp.sum(err * err, axis=1, keepdims=True), axis=0, keepdims=True) * (0.5 / d)
        lpart = jnp.broadcast_to(sq, (1, 128))

        @pl.when(pl.program_id(0) == 0)
        def _():
            dg_ref[...] = part
            loss_ref[...] = lpart

        @pl.when(pl.program_id(0) > 0)
        def _():
            dg_ref[...] += part
            loss_ref[...] += lpart

    row = pl.BlockSpec((tm, d), lambda i: (i, 0))
    vec = pl.BlockSpec((1, d), lambda i: (0, 0))
    return _pcall(
        body, name=name, grid=(t // tm,), in_specs=[row, vec, row],
        out_specs=[row, row, vec, pl.BlockSpec((1, 128), lambda i: (0, 0))],
        out_shape=[jax.ShapeDtypeStruct((t, d), F32), jax.ShapeDtypeStruct((t, d), BF16),
                   jax.ShapeDtypeStruct((1, d), F32), jax.ShapeDtypeStruct((1, 128), F32)],
        blocks=[((tm, d), F32)] * 3 + [((tm, d), BF16)],
    )(x, gain, target)


def rope_tables(t):
    pos = jnp.arange(t, dtype=F32)
    inv_freq = ROPE_THETA ** (-jnp.arange(0, ROPE_DIM, 2, dtype=F32) / ROPE_DIM)
    ang = pos[:, None] * inv_freq[None, :]
    cos, sin = jnp.cos(ang), jnp.sin(ang)
    half = ROPE_DIM // 2
    in_head = jnp.arange(128) % HEAD_DIM
    cosw, sinw = jnp.tile(cos, (1, 128 // half)), jnp.tile(sin, (1, 128 // half))
    c = jnp.where(in_head < ROPE_DIM, cosw, 1.0)
    sa = jnp.where(in_head < half, -sinw, 0.0)
    sb = jnp.where((in_head >= half) & (in_head < ROPE_DIM), sinw, 0.0)
    return jnp.concatenate([c, sa, sb], axis=1)


def _rotate(xv, cv, sav, sbv):
    halves = []
    for half in range(2):
        x = xv[:, 128 * half:128 * (half + 1)]
        halves.append(x * cv + pltpu.roll(x, 120, 1) * sav + pltpu.roll(x, 8, 1) * sbv)
    return jnp.concatenate(halves, axis=1)


STAGE_CHUNKS = 4


def _stage(tm):
    return dict(scratch_shapes=[pltpu.VMEM((STAGE_CHUNKS, tm, 128), F32)], scratch_bytes=STAGE_CHUNKS * tm * 128 * 4)


def _split_residues(stage_ref, val, out_ref, d, col, dtype):
    rows, width = val.shape
    if d == 1:
        out_ref[0, :, col:col + width] = val.astype(dtype)
        return
    chunks = width // 128
    for c in range(chunks):
        stage_ref[c] = val[:, 128 * c:128 * (c + 1)]
    for r in range(d):
        for c in range(chunks):
            out_ref[r, :, col + 128 * c:col + 128 * (c + 1)] = stage_ref[c, pl.ds(r, rows // d, stride=d), :].astype(dtype)


def _join_residues(stage_ref, in_ref, d, col=0, width=GROUP_W):
    if d == 1:
        return in_ref[0, :, col:col + width].astype(F32)
    rows = in_ref.shape[1] * d
    chunks = width // 128
    for r in range(d):
        for c in range(chunks):
            stage_ref[c, pl.ds(r, rows // d, stride=d), :] = in_ref[r, :, col + 128 * c:col + 128 * (c + 1)].astype(F32)
    return jnp.concatenate([stage_ref[c] for c in range(chunks)], axis=1)


def rope_split(proj, tables, *, name, tm=512):
    c = sa = sb = tables
    t = tables.shape[0]
    tm = min(tm, t)

    def body(*refs):
        pieces = refs[0:9]
        c_ref, sa_ref, sb_ref = refs[9:12]
        qk_out, v_out = refs[12:15], refs[15:18]
        stage = refs[18]
        cv, sav, sbv = c_ref[...], sa_ref[...], sb_ref[...]
        for g, d in enumerate(DILATIONS):
            for kind in range(3):
                xv = pieces[3 * kind + g][...].astype(F32)
                if kind < 2:
                    _split_residues(stage, _rotate(xv, cv, sav, sbv), qk_out[g], d, GROUP_W * kind, BF16)
                else:
                    _split_residues(stage, xv, v_out[g], d, 0, BF16)

    tabs = [pl.BlockSpec((tm, 128), functools.partial(lambda i, cb: (i, cb), cb=cb)) for cb in range(3)]
    in_specs = [pl.BlockSpec((tm, GROUP_W), functools.partial(lambda i, cb: (i, cb), cb=cb)) for cb in range(9)]
    out_specs = ([pl.BlockSpec((d, tm // d, 2 * GROUP_W), lambda i: (0, i, 0)) for d in DILATIONS]
                 + [pl.BlockSpec((d, tm // d, GROUP_W), lambda i: (0, i, 0)) for d in DILATIONS])
    out_shape = ([jax.ShapeDtypeStruct((d, t // d, 2 * GROUP_W), BF16) for d in DILATIONS]
                 + [jax.ShapeDtypeStruct((d, t // d, GROUP_W), BF16) for d in DILATIONS])
    outs = _pcall(
        body, name=name, grid=(t // tm,), in_specs=in_specs + tabs, out_specs=out_specs, out_shape=out_shape,
        blocks=[((tm, GROUP_W), BF16)] * 18 + [((tm, 128), F32)] * 3,
        **_stage(tm),
    )(*([proj] * 9), c, sa, sb)
    return outs[0:3], outs[3:6]


def rope_join(dqs, dks, dvs, sb_grads, dgate, tables, *, name, tm=512):
    c = sa = sb = tables
    t = tables.shape[0]
    tm = min(tm, t)

    def body(*refs):
        pieces, sb_refs, dgate_ref = refs[0:9], refs[9:12], refs[12]
        c_ref, sa_ref, sb_ref = refs[13:16]
        o_ref, stage = refs[16], refs[17]
        cv, sav, sbv = c_ref[...], -sa_ref[...], -sb_ref[...]
        for kind in range(3):
            for g, d in enumerate(DILATIONS):
                xv = _join_residues(stage, pieces[3 * kind + g], d)
                if kind < 2:
                    xv = _rotate(xv, cv, sav, sbv)
                col = GROUP_W * (3 * kind + g)
                o_ref[:, col:col + GROUP_W] = xv.astype(BF16)
        for j in range(3):
            o_ref[:, GROUP_W * (QS_BLK + j):GROUP_W * (QS_BLK + j + 1)] = sb_refs[j][...].astype(BF16)
        o_ref[:, D_MODEL * GATE_DIL_BLK:] = dgate_ref[...]

    tabs = [pl.BlockSpec((tm, 128), functools.partial(lambda i, cb: (i, cb), cb=cb)) for cb in range(3)]
    nat = lambda w: pl.BlockSpec((tm, w), lambda i: (i, 0))
    in_specs = [pl.BlockSpec((d, tm // d, GROUP_W), lambda i: (0, i, 0)) for _ in range(3) for d in DILATIONS]
    in_specs += [nat(GROUP_W)] * 3 + [nat(2 * D_MODEL)]
    return _pcall(
        body, name=name, grid=(t // tm,), in_specs=in_specs + tabs,
        out_specs=nat(D_IN), out_shape=jax.ShapeDtypeStruct((t, D_IN), BF16),
        blocks=[((tm, GROUP_W), F32)] * 12 + [((tm, 128), F32)] * 3 + [((tm, 2 * D_MODEL), BF16), ((tm, D_IN), BF16)],
        **_stage(tm),
    )(*dqs, *dks, *dvs, *sb_grads, dgate, c, sa, sb)


def _head_mask(h):
    lane = lax.broadcasted_iota(jnp.int32, (1, GROUP_W), 1)
    return (lane // HEAD_DIM) == h


def _band_mask_before():
    ri = lax.broadcasted_iota(jnp.int32, (4 * DIL_SPAN, DIL_SPAN), 0) % DIL_SPAN
    ci = lax.broadcasted_iota(jnp.int32, (4 * DIL_SPAN, DIL_SPAN), 1)
    return ci >= ri


def dil_fwd(qk, v, *, name):
    d, nsub, _ = qk.shape
    nblk = nsub // DIL_SPAN

    def body(q_ref, kc_ref, kp_ref, vc_ref, vp_ref, o_ref, lse_ref):
        nb = pl.program_id(1)
        kk = jnp.concatenate([kp_ref[0], kc_ref[0]], axis=0)
        vv = jnp.concatenate([vp_ref[0], vc_ref[0]], axis=0)
        s = _dot(_stack_heads(q_ref[0] * ATT_SCALE), kk, "nt")
        ri = lax.broadcasted_iota(jnp.int32, s.shape, 0) % DIL_SPAN
        ci = lax.broadcasted_iota(jnp.int32, s.shape, 1)
        valid = ((ci < DIL_SPAN) & (ci >= ri) & (nb > 0)) | ((ci >= DIL_SPAN) & (ci - DIL_SPAN <= ri))
        s = jnp.where(valid, s, -jnp.inf)
        m = jnp.max(s, axis=1, keepdims=True)
        p = jnp.exp(s - m)
        den = jnp.sum(p, axis=1, keepdims=True)
        o_ref[0] = _unstack_heads(_dot(p, vv, "nn") / den, DIL_SPAN)
        lse = m + jnp.log(den)
        for h in range(4):
            lse_ref[0, :, 128 * h:128 * (h + 1)] = jnp.broadcast_to(lse[DIL_SPAN * h:DIL_SPAN * (h + 1)], (DIL_SPAN, 128))

    blk = (1, DIL_SPAN, GROUP_W)
    sblk = (1, DIL_SPAN, 512)
    prv = lambda nb: jnp.maximum(nb - 1, 0)
    return _pcall(
        body, name=name, grid=(d, nblk),
        in_specs=[pl.BlockSpec(blk, lambda r, nb: (r, nb, 0)),
                  pl.BlockSpec(blk, lambda r, nb: (r, nb, 1)),
                  pl.BlockSpec(blk, lambda r, nb: (r, prv(nb), 1)),
                  pl.BlockSpec(blk, lambda r, nb: (r, nb, 0)),
                  pl.BlockSpec(blk, lambda r, nb: (r, prv(nb), 0))],
        out_specs=[pl.BlockSpec(blk, lambda r, nb: (r, nb, 0)), pl.BlockSpec(sblk, lambda r, nb: (r, nb, 0))],
        out_shape=[jax.ShapeDtypeStruct((d, nsub, GROUP_W), F32), jax.ShapeDtypeStruct((d, nsub, 512), F32)],
        blocks=[(blk, BF16)] * 5 + [(blk, F32), (sblk, F32)],
    )(qk, qk, qk, v, v)


def dil_merge(outs, lses, *, name, tm=512):
    t = outs[0].shape[0] * outs[0].shape[1]
    tm = min(tm, t)

    def body(o0, o1, o2, l0, l1, l2, o_ref, lse_ref, stage):
        ls = [_join_residues(stage, l, d, 0, 512) for l, d in zip((l0, l1, l2), DILATIONS)]
        m = jnp.maximum(jnp.maximum(ls[0], ls[1]), ls[2])
        tot = m + jnp.log(jnp.exp(ls[0] - m) + jnp.exp(ls[1] - m) + jnp.exp(ls[2] - m))
        lse_ref[...] = tot
        lane = lax.broadcasted_iota(jnp.int32, (1, 128), 1)
        first = lane < HEAD_DIM
        acc = jnp.zeros((tm, GROUP_W), F32)
        for og, lg, d in zip((o0, o1, o2), ls, DILATIONS):
            w = jnp.exp(lg - tot)
            wide = jnp.concatenate([jnp.where(first, w[:, 0:128], w[:, 128:256]),
                                    jnp.where(first, w[:, 256:384], w[:, 384:512])], axis=1)
            acc = acc + wide * _join_residues(stage, og, d)
        o_ref[...] = acc

    o_in = [pl.BlockSpec((d, tm // d, GROUP_W), lambda i: (0, i, 0)) for d in DILATIONS]
    l_in = [pl.BlockSpec((d, tm // d, 512), lambda i: (0, i, 0)) for d in DILATIONS]
    return _pcall(
        body, name=name, grid=(t // tm,), in_specs=o_in + l_in,
        out_specs=[pl.BlockSpec((tm, GROUP_W), lambda i: (i, 0)), pl.BlockSpec((tm, 512), lambda i: (i, 0))],
        out_shape=[jax.ShapeDtypeStruct((t, GROUP_W), F32), jax.ShapeDtypeStruct((t, 512), F32)],
        blocks=[((tm, GROUP_W), F32)] * 4 + [((tm, 512), F32)] * 4,
        **_stage(tm),
    )(*outs, *lses)


def dil_bwd_prep(do, o, lse, *, name, tm=512):
    t = do.shape[0]
    tm = min(tm, t)
    wide = DILATIONS[1:]

    def body(do_ref, o_ref, lse_ref, ds_ref, *rest):
        do_out, lse_out, ds_out = rest[0:2], rest[2:4], rest[4:6]
        stage = rest[6]
        dov = do_ref[...]
        prod = dov * o_ref[...]
        for h in range(4):
            s = jnp.sum(jnp.where(_head_mask(h), prod, 0.0), axis=1, keepdims=True)
            ds_ref[:, 128 * h:128 * (h + 1)] = jnp.broadcast_to(s, (tm, 128))
        for i, d in enumerate(wide):
            _split_residues(stage, dov, do_out[i], d, 0, BF16)
            _split_residues(stage, lse_ref[...], lse_out[i], d, 0, F32)
            _split_residues(stage, ds_ref[...], ds_out[i], d, 0, F32)

    nat = lambda w: pl.BlockSpec((tm, w), lambda i: (i, 0))
    res = lambda d, w: pl.BlockSpec((d, tm // d, w), lambda i: (0, i, 0))
    shape = lambda d, w, dt: jax.ShapeDtypeStruct((d, t // d, w), dt)
    outs = _pcall(
        body, name=name, grid=(t // tm,), in_specs=[nat(GROUP_W), nat(GROUP_W), nat(512)],
        out_specs=[nat(512)] + [res(d, GROUP_W) for d in wide] + [res(d, 512) for d in wide] * 2,
        out_shape=([jax.ShapeDtypeStruct((t, 512), F32)] + [shape(d, GROUP_W, BF16) for d in wide]
                   + [shape(d, 512, F32) for d in wide] * 2),
        blocks=[((tm, GROUP_W), F32)] * 3 + [((tm, 512), F32)] * 6,
        **_stage(tm),
    )(do, o, lse)
    return outs[0], outs[1:3], outs[3:5], outs[5:7]


def head_sums(a, b, *, name, tm=512):
    t = a.shape[0]
    tm = min(tm, t)

    def body(a_ref, b_ref, o_ref):
        prod = a_ref[...].astype(BF16).astype(F32) * b_ref[...]
        for h in range(4):
            s = jnp.sum(jnp.where(_head_mask(h), prod, 0.0), axis=1, keepdims=True)
            o_ref[:, 128 * h:128 * (h + 1)] = jnp.broadcast_to(s, (tm, 128))

    spec = pl.BlockSpec((tm, GROUP_W), lambda i: (i, 0))
    return _pcall(
        body, name=name, grid=(t // tm,), in_specs=[spec, spec],
        out_specs=pl.BlockSpec((tm, 512), lambda i: (i, 0)), out_shape=jax.ShapeDtypeStruct((t, 512), F32),
        blocks=[((tm, GROUP_W), F32)] * 2 + [((tm, 512), F32)],
    )(a, b)


def dil_bwd(qk, v, do, lse, dsum, *, name):
    d, nsub, _ = qk.shape
    nblk = nsub // DIL_SPAN

    def body(qa_ref, qb_ref, kc_ref, kp_ref, vc_ref, vp_ref, doa_ref, dob_ref, la_ref, lb_ref, sa_ref, sb_ref,
             dq_ref, dk_ref, dv_ref):
        nb = pl.program_id(1)
        nxt = _band_mask_before() & (nb < nblk - 1)
        kc, kp, vc, vp = kc_ref[0], kp_ref[0], vc_ref[0], vp_ref[0]
        qas, qbs = _stack_heads(qa_ref[0] * ATT_SCALE), _stack_heads(qb_ref[0] * ATT_SCALE)
        das, dbs = _stack_heads(doa_ref[0].astype(BF16)), _stack_heads(dob_ref[0].astype(BF16))
        stat = lambda ref: jnp.concatenate([ref[0, :, 128 * h:128 * (h + 1)] for h in range(4)], axis=0)
        la, lb, sa, sb = stat(la_ref), stat(lb_ref), stat(sa_ref), stat(sb_ref)

        def probs(qs, ds_, k, v, mask, l, s):
            p = jnp.where(mask, jnp.exp(_dot(qs, k, "nt") - l), 0.0)
            dsc = p * (_dot(ds_, v, "nt") - s)
            return p.astype(BF16), dsc.astype(BF16)

        wide = lambda a: jnp.concatenate([a, a], axis=1)
        ri = lax.broadcasted_iota(jnp.int32, (4 * DIL_SPAN, 2 * DIL_SPAN), 0) % DIL_SPAN
        ci = lax.broadcasted_iota(jnp.int32, (4 * DIL_SPAN, 2 * DIL_SPAN), 1)
        valid = ((ci < DIL_SPAN) & (ci >= ri) & (nb > 0)) | ((ci >= DIL_SPAN) & (ci - DIL_SPAN <= ri))
        p_a, ds_a = probs(qas, das, jnp.concatenate([kp, kc], axis=0), jnp.concatenate([vp, vc], axis=0),
                          valid, wide(la), wide(sa))
        p_nc, ds_nc = probs(qbs, dbs, kc, vc, nxt, lb, sb)
        dq_ref[0] = _unstack_heads(_dot(ds_a, jnp.concatenate([kp, kc], axis=0), "nn"), DIL_SPAN) * ATT_SCALE
        dk_ref[0] = _dot(ds_a[:, DIL_SPAN:], qas, "tn") + _dot(ds_nc, qbs, "tn")
        dv_ref[0] = _dot(p_a[:, DIL_SPAN:], das, "tn") + _dot(p_nc, dbs, "tn")

    blk = (1, DIL_SPAN, GROUP_W)
    sblk = (1, DIL_SPAN, 512)
    prv = lambda nb: jnp.maximum(nb - 1, 0)
    nxt_ = lambda nb: jnp.minimum(nb + 1, nblk - 1)
    cur_at = lambda c: pl.BlockSpec(blk, functools.partial(lambda r, nb, c: (r, nb, c), c=c))
    prv_at = lambda c: pl.BlockSpec(blk, functools.partial(lambda r, nb, c: (r, prv(nb), c), c=c))
    nxt_at = lambda c: pl.BlockSpec(blk, functools.partial(lambda r, nb, c: (r, nxt_(nb), c), c=c))
    s_cur = pl.BlockSpec(sblk, lambda r, nb: (r, nb, 0))
    s_nxt = pl.BlockSpec(sblk, lambda r, nb: (r, nxt_(nb), 0))
    o_spec = pl.BlockSpec(blk, lambda r, nb: (r, nb, 0))
    o_shape = jax.ShapeDtypeStruct((d, nsub, GROUP_W), F32)
    return _pcall(
        body, name=name, grid=(d, nblk),
        in_specs=[cur_at(0), nxt_at(0), cur_at(1), prv_at(1), cur_at(0), prv_at(0), cur_at(0), nxt_at(0),
                  s_cur, s_nxt, s_cur, s_nxt],
        out_specs=[o_spec, o_spec, o_spec], out_shape=[o_shape, o_shape, o_shape],
        blocks=[(blk, BF16)] * 6 + [(blk, F32)] * 5 + [(sblk, F32)] * 4,
    )(qk, qk, qk, qk, v, v, do, do, lse, lse, dsum, dsum)


def _tri_dot(x, b):
    hi = x.astype(BF16)
    lo = (x - hi.astype(F32)).astype(BF16)
    return _dot(jnp.concatenate([hi, lo], axis=1), jnp.concatenate([b, b], axis=0), "nn")


SB_TILE = 256
SB_ROWS = 512


def _stack_heads(a):
    return jnp.concatenate([jnp.where(_head_mask(h), a, jnp.zeros_like(a)) for h in range(4)], axis=0)


def _unstack_heads(acc, rows):
    out = acc[0:rows]
    for h in range(1, 4):
        out = jnp.where(_head_mask(h), acc[h * rows:(h + 1) * rows], out)
    return out


def _tri_masks(n):
    ri = lax.broadcasted_iota(jnp.int32, (n, n), 0)
    ci = lax.broadcasted_iota(jnp.int32, (n, n), 1)
    return (ri > ci).astype(BF16), (ri >= ci).astype(BF16)


def _sb_weights(qs, kt, after, c_keep, lead):
    z = _dot(qs, kt, "nt")
    lbeta = jnp.minimum(z, 0.0) - jnp.log(1.0 + jnp.exp(-jnp.abs(z)))
    lkeep = lbeta - z
    past = None
    if lead is not None:
        query = lax.broadcasted_iota(jnp.int32, z.shape, 0) % SB_ROWS
        past = lax.broadcasted_iota(jnp.int32, z.shape, 1) + lead < query
        lkeep = jnp.where(past, lkeep, 0.0)
    w = jnp.exp(lbeta + _tri_dot(lkeep, after) + c_keep)
    if lead is not None:
        w = jnp.where(past, w, 0.0)
    return z, past, lbeta, lkeep, w


def _sb_walk(qb, tile, carry):
    per = SB_ROWS // SB_TILE
    for i in reversed(range(per)):
        carry = tile(pl.multiple_of(qb * SB_ROWS + i * SB_TILE, SB_TILE), i * SB_TILE, i == per - 1, carry)
    past_tiles = qb * per
    return lax.fori_loop(0, past_tiles,
                         lambda it, c: tile(pl.multiple_of((past_tiles - 1 - it) * SB_TILE, SB_TILE), None, False, c), carry)


def sb_fwd(proj, *, name):
    t = proj.shape[0]
    n, m = SB_TILE, SB_ROWS
    assert t % m == 0

    def body(q_ref, k_ref, v_ref, o_ref, acc_ref):
        qb = pl.program_id(0)
        qs = _stack_heads(q_ref[...] * ATT_SCALE)
        after, _ = _tri_masks(n)

        def tile(off, lead, first, c_keep):
            kt = k_ref[pl.ds(off, n), :]
            vt = v_ref[pl.ds(off, n), :]
            _, _, _, lkeep, w = _sb_weights(qs, kt, after, c_keep, lead)
            pv = _tri_dot(w, vt)
            if first:
                acc_ref[...] = pv
            else:
                acc_ref[...] += pv
            return c_keep + jnp.sum(lkeep, axis=1, keepdims=True)

        _sb_walk(qb, tile, jnp.zeros((4 * m, 1), F32))
        o_ref[...] = _unstack_heads(acc_ref[...], m)

    full = lambda cb: pl.BlockSpec((t, GROUP_W), functools.partial(lambda i, cb: (0, cb), cb=cb))
    return _pcall(
        body, name=name, grid=(t // m,),
        in_specs=[pl.BlockSpec((m, GROUP_W), lambda i: (i, QS_BLK)), full(KS_BLK), full(VS_BLK)],
        out_specs=pl.BlockSpec((m, GROUP_W), lambda i: (i, 0)), out_shape=jax.ShapeDtypeStruct((t, GROUP_W), F32),
        blocks=[((m, GROUP_W), BF16), ((t, GROUP_W), BF16), ((t, GROUP_W), BF16), ((m, GROUP_W), F32)],
        scratch_shapes=[pltpu.VMEM((4 * m, GROUP_W), F32)], scratch_bytes=4 * m * GROUP_W * 4,
    )(proj, proj, proj)


def sb_bwd(proj, do, gtot, *, name):
    t = proj.shape[0]
    n, m = SB_TILE, SB_ROWS
    assert t % m == 0

    def body(q_ref, k_ref, v_ref, do_ref, gt_ref, dq_ref, dk_ref, dv_ref, acc_ref):
        qb = pl.program_id(0)

        @pl.when(qb == 0)
        def _():
            dk_ref[...] = jnp.zeros_like(dk_ref)
            dv_ref[...] = jnp.zeros_like(dv_ref)

        qs = _stack_heads(q_ref[...] * ATT_SCALE)
        dos = _stack_heads(do_ref[...].astype(BF16))
        gt = jnp.concatenate([jnp.max(gt_ref[:, 128 * h:128 * (h + 1)], axis=1, keepdims=True) for h in range(4)], axis=0)
        after, from_on = _tri_masks(n)

        def tile(off, lead, first, carry):
            c_keep, c_g = carry
            kt = k_ref[pl.ds(off, n), :]
            vt = v_ref[pl.ds(off, n), :]
            z, past, lbeta, lkeep, w = _sb_weights(qs, kt, after, c_keep, lead)
            gw = w * _dot(dos, vt, "nt")
            big_g = gt - (_tri_dot(gw, from_on) + c_g)
            dz = gw * jnp.exp(lbeta - z) - big_g * jnp.exp(lbeta)
            if lead is not None:
                dz = jnp.where(past, dz, 0.0)
            dz = dz.astype(BF16)
            dk_ref[pl.ds(off, n), :] += _dot(dz, qs, "tn")
            dv_ref[pl.ds(off, n), :] += _dot(w, dos, "tn")
            dq = _dot(dz, kt, "nn")
            if first:
                acc_ref[...] = dq
            else:
                acc_ref[...] += dq
            return c_keep + jnp.sum(lkeep, axis=1, keepdims=True), c_g + jnp.sum(gw, axis=1, keepdims=True)

        zero_col = jnp.zeros((4 * m, 1), F32)
        _sb_walk(qb, tile, (zero_col, zero_col))
        dq_ref[...] = _unstack_heads(acc_ref[...], m) * ATT_SCALE

    full = lambda cb: pl.BlockSpec((t, GROUP_W), functools.partial(lambda i, cb: (0, cb), cb=cb))
    whole = pl.BlockSpec((t, GROUP_W), lambda i: (0, 0))
    rowblk = pl.BlockSpec((m, GROUP_W), lambda i: (i, 0))
    shape = jax.ShapeDtypeStruct((t, GROUP_W), F32)
    return _pcall(
        body, name=name, grid=(t // m,),
        in_specs=[pl.BlockSpec((m, GROUP_W), lambda i: (i, QS_BLK)), full(KS_BLK), full(VS_BLK), rowblk,
                  pl.BlockSpec((m, 512), lambda i: (i, 0))],
        out_specs=[rowblk, whole, whole], out_shape=[shape, shape, shape],
        blocks=[((m, GROUP_W), BF16), ((t, GROUP_W), BF16), ((t, GROUP_W), BF16), ((m, GROUP_W), F32),
                ((m, 512), F32), ((m, GROUP_W), F32), ((t, GROUP_W), F32), ((t, GROUP_W), F32)],
        scratch_shapes=[pltpu.VMEM((4 * m, GROUP_W), F32)], scratch_bytes=4 * m * GROUP_W * 4,
    )(proj, proj, proj, do, gtot)


def _mesh_place():
    return lax.axis_index("x"), lax.axis_index("y"), lax.axis_index("c")


def _flip(place, mask):
    x, y, c = place
    return ((1 - x) if mask & 4 else x, (1 - y) if mask & 2 else y, (1 - c) if mask & 1 else c)


def _dev_index(place):
    x, y, c = place
    return 4 * x + 2 * y + c


HBM_SPEC = pl.BlockSpec(memory_space=pltpu.HBM)


def all_gather_rows(shard, after, *, name):
    rows, lanes = shard.shape

    def body(x_ref, after_ref, out_ref, send_sems, recv_sems, local_sem):
        me = _mesh_place()
        x, y, c = me
        sibling = _flip(me, 1)
        chips = [_flip(me, 4), _flip(me, 2), _flip(me, 6)]

        def copy(k, block, to, src=None):
            dst = out_ref.at[_dev_index(block)]
            return pltpu.make_async_remote_copy(
                src_ref=dst if src is None else src, dst_ref=dst, send_sem=send_sems.at[k], recv_sem=recv_sems.at[k],
                device_id=to, device_id_type=pl.DeviceIdType.MESH)

        mine = pltpu.make_async_copy(x_ref, out_ref.at[_dev_index(me)], local_sem)
        mine.start()
        first = [copy(0, me, sibling, src=x_ref)] + [copy(1 + j, me, chip, src=x_ref) for j, chip in enumerate(chips)]
        for cp in first:
            cp.start()
        passed = [copy(4 + j, chip, sibling) for j, chip in enumerate(chips)]
        for j, chip in enumerate(chips):
            copy(1 + j, chip, me).wait_recv()
            passed[j].start()
        copy(0, sibling, me).wait_recv()
        for j, chip in enumerate(chips):
            copy(4 + j, _flip(chip, 1), me).wait_recv()
        for cp in first + passed:
            cp.wait_send()
        mine.wait()

    return pl.pallas_call(
        body, name=name, in_specs=[HBM_SPEC, pl.BlockSpec(memory_space=pl.ANY)], out_specs=HBM_SPEC,
        out_shape=jax.ShapeDtypeStruct((N_DEV, rows, lanes), shard.dtype),
        scratch_shapes=[pltpu.SemaphoreType.DMA((7,)), pltpu.SemaphoreType.DMA((7,)), pltpu.SemaphoreType.DMA],
    )(shard, after)


SEM_SPEC = pl.BlockSpec(memory_space=pltpu.SEMAPHORE)
DATAFLOW_EFFECT = pltpu.SideEffectType.DATAFLOW_SIDE_EFFECTING


ALL_PEERS = tuple(range(1, N_DEV))
CHIP_PEERS = (1, 4, 2, 6)
OTHER_CHIPS = (4, 2, 6)


def _spread_copies(src_refs, land_refs, send_sems, recv_sems, per_peer, masks, arriving):
    me = _mesh_place()
    my = _dev_index(me)
    remote, local = [], []
    for t, (src_ref, land_ref) in enumerate(zip(src_refs, land_refs)):
        for i, mask in enumerate(masks):
            peer = _flip(me, mask)
            data_of = my if arriving else _dev_index(peer)
            slot = _dev_index(peer) if arriving else my
            k = t * len(masks) + i
            remote.append(pltpu.make_async_remote_copy(
                src_ref=src_ref.at[data_of] if per_peer else src_ref, dst_ref=land_ref.at[slot],
                send_sem=send_sems.at[k], recv_sem=recv_sems.at[k],
                device_id=peer, device_id_type=pl.DeviceIdType.MESH))
        local.append(pltpu.make_async_copy(src_ref.at[my] if per_peer else src_ref, land_ref.at[my],
                                           send_sems.at[len(src_refs) * len(masks) + t]))
    return remote, local


def spread_start(srcs, *, per_peer, name, masks=ALL_PEERS):
    nt = len(srcs)
    zones = [pltpu.HBM((N_DEV,) + (s.shape[1:] if per_peer else s.shape), s.dtype) for s in srcs]

    def body(*refs):
        src_refs, (send_sems, recv_sems) = refs[:nt], refs[nt:nt + 2]
        land_refs, token = refs[2 * nt + 2:3 * nt + 2], refs[3 * nt + 2]
        remote, local = _spread_copies(src_refs, land_refs, send_sems, recv_sems, per_peer, masks, arriving=False)
        for cp in remote + local:
            cp.start()
        token[...] = jnp.zeros_like(token)

    return pl.pallas_call(
        body, name=name, in_specs=(HBM_SPEC,) * nt,
        out_shape=(pltpu.SemaphoreType.DMA((nt * len(masks) + nt,)), pltpu.SemaphoreType.DMA((nt * len(masks),)),
                   *[pltpu.HBM(s.shape, s.dtype) for s in srcs], *zones, jax.ShapeDtypeStruct((8, 128), F32)),
        out_specs=(SEM_SPEC, SEM_SPEC) + (HBM_SPEC,) * (2 * nt) + (pl.BlockSpec(memory_space=pltpu.VMEM),),
        input_output_aliases={t: 2 + t for t in range(nt)},
        compiler_params=pltpu.CompilerParams(has_side_effects=DATAFLOW_EFFECT),
    )(*[pltpu.with_memory_space_constraint(s, pltpu.HBM) for s in srcs])


def spread_wait(started, after, *, per_peer, name, masks=ALL_PEERS):
    nt = (len(started) - 3) // 2
    send_sems, recv_sems = started[0:2]
    srcs_thru, lands_thru = started[2:2 + nt], started[2 + nt:2 + 2 * nt]

    def body(*refs):
        src_refs, land_refs = refs[:nt], refs[nt:2 * nt]
        send_sems, recv_sems = refs[2 * nt:2 * nt + 2]
        remote, local = _spread_copies(src_refs, land_refs, send_sems, recv_sems, per_peer, masks, arriving=True)
        for cp in remote:
            cp.wait_send()
            cp.wait_recv()
        for cp in local:
            cp.wait()

    outs = pl.pallas_call(
        body, name=name, in_specs=(HBM_SPEC,) * (2 * nt) + (SEM_SPEC, SEM_SPEC, pl.BlockSpec(memory_space=pl.ANY)),
        out_shape=tuple(pltpu.HBM(a.shape, a.dtype) for a in (*srcs_thru, *lands_thru)),
        out_specs=(HBM_SPEC,) * (2 * nt), input_output_aliases={t: t for t in range(2 * nt)},
        compiler_params=pltpu.CompilerParams(has_side_effects=DATAFLOW_EFFECT),
    )(*srcs_thru, *lands_thru, send_sems, recv_sems, after)
    return list(outs[nt:])


def _relay_copies(land_refs, send_sems, recv_sems, arriving):
    me = _mesh_place()
    sibling = _flip(me, 1)
    out = []
    for t, land_ref in enumerate(land_refs):
        for i, mask in enumerate(OTHER_CHIPS):
            slot = _dev_index(_flip(sibling if arriving else me, mask))
            k = t * len(OTHER_CHIPS) + i
            out.append(pltpu.make_async_remote_copy(
                src_ref=land_ref.at[slot], dst_ref=land_ref.at[slot], send_sem=send_sems.at[k], recv_sem=recv_sems.at[k],
                device_id=sibling, device_id_type=pl.DeviceIdType.MESH))
    return out


def relay_start(lands, *, name):
    nt = len(lands)
    n_sem = nt * len(OTHER_CHIPS)

    def body(*refs):
        for cp in _relay_copies(refs[:nt], refs[nt], refs[nt + 1], arriving=False):
            cp.start()

    return pl.pallas_call(
        body, name=name, in_specs=(HBM_SPEC,) * nt,
        out_shape=(pltpu.SemaphoreType.DMA((n_sem,)), pltpu.SemaphoreType.DMA((n_sem,)),
                   *[pltpu.HBM(a.shape, a.dtype) for a in lands]),
        out_specs=(SEM_SPEC, SEM_SPEC) + (HBM_SPEC,) * nt, input_output_aliases={t: 2 + t for t in range(nt)},
        compiler_params=pltpu.CompilerParams(has_side_effects=DATAFLOW_EFFECT),
    )(*[pltpu.with_memory_space_constraint(a, pltpu.HBM) for a in lands])


def relay_wait(started, *, name):
    send_sems, recv_sems = started[0:2]
    lands_thru = started[2:]
    nt = len(lands_thru)

    def body(*refs):
        for cp in _relay_copies(refs[:nt], refs[nt], refs[nt + 1], arriving=True):
            cp.wait_send()
            cp.wait_recv()

    return list(pl.pallas_call(
        body, name=name, in_specs=(HBM_SPEC,) * nt + (SEM_SPEC, SEM_SPEC),
        out_shape=tuple(pltpu.HBM(a.shape, a.dtype) for a in lands_thru), out_specs=(HBM_SPEC,) * nt,
        input_output_aliases={t: t for t in range(nt)},
        compiler_params=pltpu.CompilerParams(has_side_effects=DATAFLOW_EFFECT),
    )(*lands_thru, send_sems, recv_sems))


def sum_partials(parts, *, name, tr):
    _, rows, lanes = parts.shape
    assert rows % tr == 0

    def body(p_ref, g_ref):
        g = p_ref[0].astype(F32)
        for k in range(1, N_DEV):
            g = g + p_ref[k].astype(F32)
        g_ref[...] = g

    return _pcall(
        body, name=name, grid=(rows // tr,),
        in_specs=[pl.BlockSpec((N_DEV, tr, lanes), lambda i: (0, i, 0))],
        out_specs=pl.BlockSpec((tr, lanes), lambda i: (i, 0)), out_shape=jax.ShapeDtypeStruct((rows, lanes), F32),
        blocks=[((N_DEV, tr, lanes), parts.dtype), ((tr, lanes), F32)],
    )(parts)


def adamw(g, w, m, v, *, name, tr):
    nl, k, n = w.shape
    tr = max(c for c in range(8, min(tr, k) + 1, 8) if k % c == 0)
    bc1 = 1.0 - ADAM_B1 ** ADAM_STEP
    bc2 = 1.0 - ADAM_B2 ** ADAM_STEP

    def body(g_ref, w_ref, m_ref, v_ref, d_ref, mo_ref, vo_ref):
        gv = g_ref[...]
        m_new = ADAM_B1 * m_ref[...] + (1.0 - ADAM_B1) * gv
        v_new = ADAM_B2 * v_ref[...] + (1.0 - ADAM_B2) * (gv * gv)
        mo_ref[...] = m_new
        vo_ref[...] = v_new
        d_ref[...] = -ADAM_LR * ((m_new / bc1) / (jnp.sqrt(v_new / bc2) + ADAM_EPS) + ADAM_WD * w_ref[...])

    spec = pl.BlockSpec((1, tr, n), lambda l, i: (l, i, 0))
    shape = jax.ShapeDtypeStruct(w.shape, F32)
    return _pcall(
        body, name=name, grid=(nl, k // tr), in_specs=[spec] * 4, out_specs=[spec] * 3, out_shape=[shape] * 3,
        blocks=[((1, tr, n), F32)] * 7,
    )(g, w, m, v)


def sum_adamw(partials, w, m, v, *, name, tr, transposed=False):
    nl, k, n = w.shape
    assert nl == len(partials) == 2
    step = 128 if transposed else 8
    tr = max(c for c in range(step, min(tr, k) + 1, step) if k % c == 0)
    bc1 = 1.0 - ADAM_B1 ** ADAM_STEP
    bc2 = 1.0 - ADAM_B2 ** ADAM_STEP

    def body(p0_ref, p1_ref, w_ref, m_ref, v_ref, g_ref, d_ref, mo_ref, vo_ref):
        first = pl.program_id(0) == 0
        gv = jnp.where(first, p0_ref[0], p1_ref[0]).astype(F32)
        for s in range(1, N_DEV):
            gv = gv + jnp.where(first, p0_ref[s], p1_ref[s]).astype(F32)
        if transposed:
            gv = gv.T
        m_new = ADAM_B1 * m_ref[0] + (1.0 - ADAM_B1) * gv
        v_new = ADAM_B2 * v_ref[0] + (1.0 - ADAM_B2) * (gv * gv)
        g_ref[0] = gv
        mo_ref[0] = m_new
        vo_ref[0] = v_new
        d_ref[0] = -ADAM_LR * ((m_new / bc1) / (jnp.sqrt(v_new / bc2) + ADAM_EPS) + ADAM_WD * w_ref[0])

    spec = pl.BlockSpec((1, tr, n), lambda l, i: (l, i, 0))
    if transposed:
        p0spec = pl.BlockSpec((N_DEV, n, tr), lambda l, i: (0, 0, i * (1 - l)))
        p1spec = pl.BlockSpec((N_DEV, n, tr), lambda l, i: (0, 0, i * l))
    else:
        p0spec = pl.BlockSpec((N_DEV, tr, n), lambda l, i: (0, i * (1 - l), 0))
        p1spec = pl.BlockSpec((N_DEV, tr, n), lambda l, i: (0, i * l, 0))
    shape = jax.ShapeDtypeStruct(w.shape, F32)
    return _pcall(
        body, name=name, grid=(nl, k // tr), in_specs=[p0spec, p1spec, spec, spec, spec], out_specs=[spec] * 4,
        out_shape=[shape] * 4, blocks=[((N_DEV, tr, n), BF16)] * 2 + [((1, tr, n), F32)] * 7,
    )(partials[0], partials[1], w, m, v)


def travelling(a, by_cols):
    return jnp.swapaxes(a, -1, -2) if by_cols else a


def _row(v):
    return v.reshape(1, -1)


def ffn_fwd(x, h, w, pre, tag, next_gain):
    ga, gb, s = swiglu_fwd(h, w[pre + "_w_gate"], w[pre + "_w_up"], name=f"{tag}_gateup")
    if callable(w[pre + "_w_down"]):
        w[pre + "_w_down"] = w[pre + "_w_down"](s)
    out, h_next = matmul_res_norm(s, w[pre + "_w_down"], x, next_gain, scale=0.5, tm=512, name=f"{tag}_down")
    return out, h_next, (x, h, ga, gb, s)


def ffn_bwd_weights(dxb, saved, w, pre, tag, ship=None):
    x, h, a, b, s = saved
    token = None
    grads = {}

    def made(name, g):
        grads[name] = g
        return ship(name, g) if ship else None

    da, db = swiglu_bwd(dxb, w[pre + "_w_down"], a, b, scale=0.5, name=f"{tag}_dgateup")
    token = made(pre + "_w_down", matmul(s, dxb, "tn", tm=1408, tn=1024, tk=2048, out_dtype=BF16, scale=0.5, name=f"{tag}_gdown"))
    token = made(pre + "_w_gate", matmul(da, h, "tn", tm=1408, tn=1024, tk=2048, out_dtype=BF16, after=token, name=f"{tag}_ggate"))
    token = made(pre + "_w_up", matmul(db, h, "tn", tm=1408, tn=1024, tk=2048, out_dtype=BF16, after=token, name=f"{tag}_gup"))
    return grads, (da, db), token


def ffn_bwd_input(dx, rest, saved, gain, w, pre, tag):
    da, db = rest
    x = saved[0]
    return matmul_rms_bwd([(da, w[pre + "_w_gate"]), (db, w[pre + "_w_up"])], x, gain, dx, tm=256, name=f"{tag}_dh")


def mixer_fwd(x, h, w, tables, tag, next_gain):
    proj = matmul(h, w["w_in"], "nt", tm=512, tn=1280, tk=1024, out_dtype=BF16, name=f"{tag}_in")
    qks, vs = rope_split(proj, tables, name=f"{tag}_rope")
    outs, lses = [], []
    for g in range(N_DIL_GROUPS):
        o, lse = dil_fwd(qks[g], vs[g], name=f"{tag}_dil{g}")
        outs.append(o)
        lses.append(lse)
    odil, lse = dil_merge(outs, lses, name=f"{tag}_merge")
    osb = sb_fwd(proj, name=f"{tag}_sb")
    y, u1, u2 = gate_fwd(odil, osb, w["w_proj_dil"], w["w_proj_sb"], proj, name=f"{tag}_gate")
    out, h_next = matmul_res_norm(y, w["w_out"], x, next_gain, scale=1.0, tm=512, name=f"{tag}_out")
    return out, h_next, (x, h, proj, qks, vs, odil, lse, osb, u1, u2, y)


def mixer_bwd_weights(dxb, saved, w, tables, tag):
    x, h, proj, qks, vs, odil, lse, osb, u1, u2, y = saved
    t = x.shape[0]
    g_out = matmul(y, dxb, "tn", tm=1024, tn=1024, tk=2048, out_dtype=BF16, name=f"{tag}_gout")
    du1, du2, dgate = gate_bwd(dxb, w["w_out"], u1, u2, proj, name=f"{tag}_dgate")
    g_pd = matmul(du1, odil, "tn", tm=1024, tn=256, tk=2048, out_dtype=BF16, name=f"{tag}_gpd")
    g_ps = matmul(du2, osb, "tn", tm=1024, tn=256, tk=2048, out_dtype=BF16, name=f"{tag}_gps")
    dodil = matmul(du1, w["w_proj_dil"], "nn", tm=512, tn=256, tk=1024, out_dtype=F32, name=f"{tag}_dodil")
    dosb = matmul(du2, w["w_proj_sb"], "nn", tm=512, tn=256, tk=1024, out_dtype=F32, name=f"{tag}_dosb")
    dsum, do_wide, lse_wide, dsum_wide = dil_bwd_prep(dodil, odil, lse, name=f"{tag}_dprep")
    dos = [dodil[None]] + list(do_wide)
    lss = [lse[None]] + list(lse_wide)
    dss = [dsum[None]] + list(dsum_wide)
    dqs, dks, dvs = [], [], []
    for g in range(N_DIL_GROUPS):
        dq, dk, dv = dil_bwd(qks[g], vs[g], dos[g], lss[g], dss[g], name=f"{tag}_ddil{g}")
        dqs.append(dq)
        dks.append(dk)
        dvs.append(dv)
    gtot = head_sums(dosb, osb, name=f"{tag}_gsum")
    sb_grads = sb_bwd(proj, dosb, gtot, name=f"{tag}_dsb")
    dproj = rope_join(dqs, dks, dvs, sb_grads, dgate, tables, name=f"{tag}_drope")
    g_in = matmul(dproj, h, "tn", tm=1280, tn=1024, tk=2048, out_dtype=BF16, name=f"{tag}_gin")
    return {"w_in": g_in, "w_proj_dil": g_pd, "w_proj_sb": g_ps, "w_out": g_out}, dproj


def mixer_bwd_input(dx, dproj, saved, gain, w, tag):
    x = saved[0]
    return matmul_rms_bwd([(dproj, w["w_in"])], x, gain, dx, tm=256, name=f"{tag}_dh")


def kernel(x, norm_ffn1, ffn1_w_gate, ffn1_w_up, ffn1_w_down, norm_mix, w_in, w_proj_dil, w_proj_sb, w_out, norm_ffn2, ffn2_w_gate, ffn2_w_up, ffn2_w_down, norm_final, loss_target, m_norm_ffn1, m_ffn1_w_gate, m_ffn1_w_up, m_ffn1_w_down, m_norm_mix, m_w_in, m_w_proj_dil, m_w_proj_sb, m_w_out, m_norm_ffn2, m_ffn2_w_gate, m_ffn2_w_up, m_ffn2_w_down, m_norm_final, v_norm_ffn1, v_ffn1_w_gate, v_ffn1_w_up, v_ffn1_w_down, v_norm_mix, v_w_in, v_w_proj_dil, v_w_proj_sb, v_w_out, v_norm_ffn2, v_ffn2_w_gate, v_ffn2_w_up, v_ffn2_w_down, v_norm_final):
    args = dict(locals())
    t = x.shape[1]
    xs = x.reshape(t, D_MODEL)
    target = loss_target.reshape(t, D_MODEL)
    tables = rope_tables(t)

    parts = [(l, p) for l in range(2) for p in SUBBLOCKS]
    gains = {n: args[n] for n in NORM_ROWS}

    shipments = []
    for l, p in parts:
        if (l, p) == parts[0]:
            shipments += [(l, p, SUBBLOCKS[p][:2], CHIP_PEERS), (l, p, SUBBLOCKS[p][2:], ALL_PEERS)]
        else:
            shipments.append((l, p, SUBBLOCKS[p], ALL_PEERS))
    in_flight, order_token = [], jnp.zeros((1, 1), F32)
    for l, p, tensors, masks in shipments:
        shards = [travelling(args[n][l], by_cols).astype(BF16) for n, by_cols in tensors]
        shards[0] = shards[0] + order_token.astype(BF16)
        in_flight.append(spread_start(shards, per_peer=False, masks=masks, name=f"gather_start_l{l}_{tensors[0][0]}"))
        order_token = in_flight[-1][-1][0:1, 0:1]

    def arrived(i, after):
        l, p, tensors, masks = shipments[i]
        tag = f"l{l}_{tensors[0][0]}"
        lands = spread_wait(in_flight[i], after, per_peer=False, masks=masks, name=f"gather_wait_{tag}")
        if masks is CHIP_PEERS:
            lands = relay_wait(relay_start(lands, name=f"gather_relay_{tag}"), name=f"gather_relayed_{tag}")
        return {n: land.reshape(-1, land.shape[-1]) for (n, _), land in zip(tensors, lands)}

    def weights_of(l, p, after):
        mine = [i for i, s in enumerate(shipments) if s[0:2] == (l, p)]
        w = arrived(mine[0], after)
        for i in mine[1:]:
            for n, _ in shipments[i][2]:
                w[n] = functools.partial(lambda after, i, n: arrived(i, after)[n], i=i, n=n)
        return w

    saved, weights = {}, {}
    act = xs
    h = rms_fwd(xs, _row(gains["norm_ffn1"][0]) + order_token, name="l0_ffn1_norm")
    for i, (l, p) in enumerate(parts):
        weights[(l, p)] = weights_of(l, p, h if i == 0 else act)
        nl, np_ = parts[i + 1] if i + 1 < len(parts) else (None, None)
        next_gain = _row(gains["norm_" + np_][nl]) if np_ else None
        if p == "mix":
            act, h, saved[(l, p)] = mixer_fwd(act, h, weights[(l, p)], tables, f"l{l}_mix", next_gain)
        else:
            act, h, saved[(l, p)] = ffn_fwd(act, h, weights[(l, p)], p, f"l{l}_{p}", next_gain)
    dx, dxb, g_final, loss_part = final_loss(act, _row(norm_final), target, name="loss_head")

    gain_grads, sent, sent_last = {}, {}, {}
    per_device = lambda g: g.reshape(N_DEV, -1, g.shape[-1])

    def ship_last(name, g):
        sent_last[name] = spread_start([per_device(g)], per_peer=True, name=f"reduce_start_{name}")
        return sent_last[name][-1]

    for l, p in reversed(parts):
        w, sv = weights[(l, p)], saved[(l, p)]
        if p == "mix":
            gw, rest = mixer_bwd_weights(dxb, sv, w, tables, f"l{l}_mix")
        elif (l, p) == parts[0]:
            gw, rest, token = ffn_bwd_weights(dxb, sv, w, p, f"l{l}_{p}", ship=ship_last)
        else:
            gw, rest, _ = ffn_bwd_weights(dxb, sv, w, p, f"l{l}_{p}")
        if (l, p) != parts[0]:
            sent[(l, p)] = spread_start([per_device(gw[n]) for n, _ in SUBBLOCKS[p]], per_peer=True, name=f"reduce_start_l{l}_{p}")
            token = sent[(l, p)][-1]
        gain = _row(gains["norm_" + p][l]) + token[0:1, 0:1]
        if p == "mix":
            dx, dxb, gain_grads[("norm_mix", l)] = mixer_bwd_input(dx, rest, sv, gain, w, f"l{l}_mix")
        else:
            dx, dxb, gain_grads[("norm_" + p, l)] = ffn_bwd_input(dx, rest, sv, gain, w, p, f"l{l}_{p}")

    partials, big_all = {}, [{}, {}, {}, {}]

    def receive(l, p, after):
        if (l, p) == parts[0]:
            for n, started in sent_last.items():
                partials.setdefault(n, [None, None])[l] = spread_wait(started, after, per_peer=True, name=f"reduce_wait_{n}")[0]
            return
        lands = spread_wait(sent[(l, p)], after, per_peer=True, name=f"reduce_wait_l{l}_{p}")
        for (n, _), land in zip(SUBBLOCKS[p], lands):
            partials.setdefault(n, [None, None])[l] = land

    def update(p):
        for n, by_cols in SUBBLOCKS[p]:
            if by_cols and args[n].shape[-1] % 128 == 0:
                outs = sum_adamw(partials[n], args[n], args["m_" + n], args["v_" + n], tr=256, transposed=True,
                                 name=f"update_{n}")
                for kind, arr in enumerate(outs):
                    big_all[kind][n] = arr
            else:
                outs = sum_adamw(partials[n], travelling(args[n], by_cols), travelling(args["m_" + n], by_cols),
                                 travelling(args["v_" + n], by_cols), tr=256, name=f"update_{n}")
                for kind, arr in enumerate(outs):
                    big_all[kind][n] = travelling(arr, by_cols)
        return outs[1]

    for l, p in reversed(parts[1:]):
        receive(l, p, dx)
    update("ffn2")
    done = update("mix")
    receive(*parts[0], done)
    done = update("ffn1")

    loss_row = jnp.pad(loss_part[:, :1], ((0, 0), (0, D_MODEL - 1)))
    small = jnp.concatenate([gain_grads[(n, l)] for n in NORM_ROWS for l in range(2)] + [g_final, loss_row], axis=0)
    small_g = sum_partials(all_gather_rows(small, done, name="gather_gain_grads"), tr=8, name="sum_gain_grads")
    zero_row = jnp.zeros((1, D_MODEL), F32)
    small_of = lambda pre: jnp.concatenate([args[pre + n] for n in NORM_ROWS] + [_row(args[pre + "norm_final"]), zero_row], axis=0)[None]
    small_out = adamw(small_g[None], small_of(""), small_of("m_"), small_of("v_"), tr=8, name="update_gains")
    small_all = [small_g] + [o[0] for o in small_out]

    def gains_of(s):
        out = {n: s[2 * i:2 * i + 2] for i, n in enumerate(NORM_ROWS)}
        out["norm_final"] = s[6]
        return out

    order = ["norm_ffn1", "ffn1_w_gate", "ffn1_w_up", "ffn1_w_down", "norm_mix", "w_in", "w_proj_dil", "w_proj_sb", "w_out",
             "norm_ffn2", "ffn2_w_gate", "ffn2_w_up", "ffn2_w_down", "norm_final"]
    results = []
    for kind in range(4):
        both = {**big_all[kind], **gains_of(small_all[kind])}
        results += [both[n] for n in order]
    loss = small_g[7, 0]
    return (loss, dx.reshape(1, t, D_MODEL), *results)
```

```python
import functools

import jax
import jax.numpy as jnp
from jax import lax
from jax.experimental import pallas as pl
from jax.experimental.pallas import tpu as pltpu

F32 = jnp.float32
BF16 = jnp.bfloat16

D_MODEL = 1024
HEAD_DIM = 64
GROUP_W = 256
D_IN = 5120
N_DIL_GROUPS = 3
DIL_SPAN = 128
DILATIONS = (1, 4, 16)
ROPE_THETA = 500000.0
ROPE_DIM = 16
RMS_EPS = 1e-6
ATT_SCALE = HEAD_DIM ** -0.5
QS_BLK, KS_BLK, VS_BLK = 9, 10, 11
GATE_DIL_BLK, GATE_SB_BLK = 3, 4

ADAM_LR, ADAM_B1, ADAM_B2, ADAM_EPS, ADAM_WD, ADAM_STEP = 0.001, 0.9, 0.999, 1e-08, 0.01, 10

N_DEV = 8
VMEM_PHYSICAL_V7X = 64 << 20
VMEM_TEMP_HEADROOM = 20 << 20

SUBBLOCKS = {
    "ffn1": (("ffn1_w_gate", True), ("ffn1_w_up", True), ("ffn1_w_down", False)),
    "mix": (("w_in", True), ("w_proj_dil", True), ("w_proj_sb", True), ("w_out", False)),
    "ffn2": (("ffn2_w_gate", True), ("ffn2_w_up", True), ("ffn2_w_down", False)),
}
NORM_ROWS = ("norm_ffn1", "norm_mix", "norm_ffn2")


def _nbytes(shape, dtype):
    n = 1
    for s in shape:
        n *= s
    return n * jnp.dtype(dtype).itemsize


def _pcall(body, *, name, grid, in_specs, out_specs, out_shape, blocks, scratch_shapes=(), scratch_bytes=0):
    need = 2 * sum(_nbytes(s, d) for s, d in blocks) + scratch_bytes + VMEM_TEMP_HEADROOM
    limit = min(need, VMEM_PHYSICAL_V7X - (4 << 20))
    in_hbm = lambda s: pltpu.HBM(s.shape, s.dtype)
    out_shape = [in_hbm(s) for s in out_shape] if isinstance(out_shape, (list, tuple)) else in_hbm(out_shape)
    call = pl.pallas_call(
        body, name=name, grid=grid, in_specs=in_specs, out_specs=out_specs, out_shape=out_shape,
        scratch_shapes=scratch_shapes,
        compiler_params=pltpu.CompilerParams(vmem_limit_bytes=limit),
    )
    return lambda *args: call(*[pltpu.with_memory_space_constraint(a, pltpu.HBM) for a in args])


def _dot(a, b, form):
    dn = {"nn": (((1,), (0,)), ((), ())), "nt": (((1,), (1,)), ((), ())), "tn": (((0,), (0,)), ((), ()))}[form]
    return lax.dot_general(a.astype(BF16), b.astype(BF16), dn, preferred_element_type=F32)


def _sigmoid(x):
    return 1.0 / (1.0 + jnp.exp(-x))


def matmul(a, b, form, *, tm, tn, tk, out_dtype, name, scale=1.0, after=None):
    if form == "tn":
        kdim, m = a.shape
        n = b.shape[1]
    else:
        m, kdim = a.shape
        n = b.shape[1] if form == "nn" else b.shape[0]
    tm, tn, tk = min(tm, m), min(tn, n), min(tk, kdim)
    assert m % tm == 0 and n % tn == 0 and kdim % tk == 0, (name, m, n, kdim, tm, tn, tk)
    nk = kdim // tk

    if form == "tn":
        a_blk, a_map = (tk, tm), (lambda j, i, k: (k, i))
    else:
        a_blk, a_map = (tm, tk), (lambda j, i, k: (i, k))
    if form == "nt":
        b_blk, b_map = (tn, tk), (lambda j, i, k: (j, k))
    else:
        b_blk, b_map = (tk, tn), (lambda j, i, k: (k, j))
    o_map = lambda j, i, k: (i, j)

    def body(a_ref, b_ref, *rest):
        o_ref, acc = (rest[1], rest[2:]) if after is not None else (rest[0], rest[1:])

        def finish(total):
            o_ref[...] = (total * scale if scale != 1.0 else total).astype(out_dtype)

        if nk == 1:
            finish(_dot(a_ref[...], b_ref[...], form))
        else:
            acc_ref, = acc
            k = pl.program_id(2)

            @pl.when(k == 0)
            def _():
                acc_ref[...] = _dot(a_ref[...], b_ref[...], form)

            @pl.when(k > 0)
            def _():
                acc_ref[...] += _dot(a_ref[...], b_ref[...], form)

            @pl.when(k == nk - 1)
            def _():
                finish(acc_ref[...])

    scratch = [pltpu.VMEM((tm, tn), F32)] if nk > 1 else []
    in_specs = [pl.BlockSpec(a_blk, a_map), pl.BlockSpec(b_blk, b_map)]
    args = [a, b]
    if after is not None:
        in_specs.append(pl.BlockSpec(memory_space=pl.ANY))
        args.append(after)
    return _pcall(
        body, name=name, grid=(n // tn, m // tm, nk), in_specs=in_specs,
        out_specs=pl.BlockSpec((tm, tn), o_map), out_shape=jax.ShapeDtypeStruct((m, n), out_dtype),
        blocks=[(a_blk, a.dtype), (b_blk, b.dtype), ((tm, tn), out_dtype)],
        scratch_shapes=scratch, scratch_bytes=(tm * tn * 4 if nk > 1 else 0),
    )(*args)


def swiglu_fwd(h, wg_t, wu_t, *, name, tm=512, tn=1408):
    t, d = h.shape
    f = wg_t.shape[0]
    tm, tn = min(tm, t), min(tn, f)

    def body(h_ref, wg_ref, wu_ref, ga_ref, gb_ref, s_ref):
        hh = h_ref[...]
        a = _dot(hh, wg_ref[...], "nt")
        b = _dot(hh, wu_ref[...], "nt")
        sg = _sigmoid(a)
        silu = a * sg
        ga_ref[...] = (b * (sg * (1.0 + a * (1.0 - sg)))).astype(BF16)
        gb_ref[...] = silu.astype(BF16)
        s_ref[...] = (silu * b).astype(BF16)

    w_spec = pl.BlockSpec((tn, d), lambda j, i: (j, 0))
    o_spec = pl.BlockSpec((tm, tn), lambda j, i: (i, j))
    o_shape = jax.ShapeDtypeStruct((t, f), BF16)
    return _pcall(
        body, name=name, grid=(f // tn, t // tm),
        in_specs=[pl.BlockSpec((tm, d), lambda j, i: (i, 0)), w_spec, w_spec],
        out_specs=[o_spec, o_spec, o_spec], out_shape=[o_shape, o_shape, o_shape],
        blocks=[((tm, d), BF16), ((tn, d), BF16), ((tn, d), BF16)] + [((tm, tn), BF16)] * 3,
    )(h, wg_t, wu_t)


def swiglu_bwd(dyb, wd, ga, gb, *, name, scale, tm=512, tn=1408):
    t, d = dyb.shape
    f = wd.shape[0]
    tm, tn = min(tm, t), min(tn, f)

    def body(dy_ref, wd_ref, ga_ref, gb_ref, da_ref, db_ref):
        ds = _dot(dy_ref[...], wd_ref[...], "nt") * scale
        da_ref[...] = (ds * ga_ref[...].astype(F32)).astype(BF16)
        db_ref[...] = (ds * gb_ref[...].astype(F32)).astype(BF16)

    o_spec = pl.BlockSpec((tm, tn), lambda j, i: (i, j))
    o_shape = jax.ShapeDtypeStruct((t, f), BF16)
    return _pcall(
        body, name=name, grid=(f // tn, t // tm),
        in_specs=[pl.BlockSpec((tm, d), lambda j, i: (i, 0)), pl.BlockSpec((tn, d), lambda j, i: (j, 0)), o_spec, o_spec],
        out_specs=[o_spec, o_spec], out_shape=[o_shape, o_shape],
        blocks=[((tm, d), BF16), ((tn, d), BF16)] + [((tm, tn), BF16)] * 4,
    )(dyb, wd, ga, gb)


def gate_fwd(odil, osb, wpd_t, wps_t, proj, *, name, tm=512):
    t = odil.shape[0]
    tm = min(tm, t)

    def body(od_ref, os_ref, wpd_ref, wps_ref, g1_ref, g2_ref, y_ref, u1_ref, u2_ref):
        u1 = _dot(od_ref[...], wpd_ref[...], "nt")
        u2 = _dot(os_ref[...], wps_ref[...], "nt")
        y = _sigmoid(g1_ref[...].astype(F32)) * u1 + _sigmoid(g2_ref[...].astype(F32)) * u2
        y_ref[...] = y.astype(BF16)
        u1_ref[...] = u1.astype(BF16)
        u2_ref[...] = u2.astype(BF16)

    o_spec = pl.BlockSpec((tm, D_MODEL), lambda i: (i, 0))
    w_spec = pl.BlockSpec((D_MODEL, GROUP_W), lambda i: (0, 0))
    a_spec = pl.BlockSpec((tm, GROUP_W), lambda i: (i, 0))
    o_shape = jax.ShapeDtypeStruct((t, D_MODEL), BF16)
    return _pcall(
        body, name=name, grid=(t // tm,),
        in_specs=[a_spec, a_spec, w_spec, w_spec,
                  pl.BlockSpec((tm, D_MODEL), lambda i: (i, GATE_DIL_BLK)),
                  pl.BlockSpec((tm, D_MODEL), lambda i: (i, GATE_SB_BLK))],
        out_specs=[o_spec, o_spec, o_spec], out_shape=[o_shape, o_shape, o_shape],
        blocks=[((tm, GROUP_W), F32)] * 2 + [((D_MODEL, GROUP_W), BF16)] * 2 + [((tm, D_MODEL), BF16)] * 5,
    )(odil, osb, wpd_t, wps_t, proj, proj)


def gate_bwd(dxb, wout, u1, u2, proj, *, name, tm=512):
    t = dxb.shape[0]
    tm = min(tm, t)

    def body(dx_ref, w_ref, u1_ref, u2_ref, g1_ref, g2_ref, du1_ref, du2_ref, dg_ref):
        dy = _dot(dx_ref[...], w_ref[...], "nt")
        s1 = _sigmoid(g1_ref[...].astype(F32))
        s2 = _sigmoid(g2_ref[...].astype(F32))
        du1_ref[...] = (dy * s1).astype(BF16)
        du2_ref[...] = (dy * s2).astype(BF16)
        dg_ref[:, :D_MODEL] = (dy * u1_ref[...].astype(F32) * s1 * (1.0 - s1)).astype(BF16)
        dg_ref[:, D_MODEL:] = (dy * u2_ref[...].astype(F32) * s2 * (1.0 - s2)).astype(BF16)

    o_spec = pl.BlockSpec((tm, D_MODEL), lambda i: (i, 0))
    o_shape = jax.ShapeDtypeStruct((t, D_MODEL), BF16)
    return _pcall(
        body, name=name, grid=(t // tm,),
        in_specs=[o_spec, pl.BlockSpec((D_MODEL, D_MODEL), lambda i: (0, 0)), o_spec, o_spec,
                  pl.BlockSpec((tm, D_MODEL), lambda i: (i, GATE_DIL_BLK)),
                  pl.BlockSpec((tm, D_MODEL), lambda i: (i, GATE_SB_BLK))],
        out_specs=[o_spec, o_spec, pl.BlockSpec((tm, 2 * D_MODEL), lambda i: (i, 0))],
        out_shape=[o_shape, o_shape, jax.ShapeDtypeStruct((t, 2 * D_MODEL), BF16)],
        blocks=[((tm, D_MODEL), BF16)] * 9 + [((D_MODEL, D_MODEL), BF16)],
    )(dxb, wout, u1, u2, proj, proj)


def rms_fwd(x, gain, *, name, tm=512):
    t, d = x.shape
    tm = min(tm, t)

    def body(x_ref, g_ref, h_ref):
        xv = x_ref[...]
        rstd = lax.rsqrt(jnp.mean(xv * xv, axis=1, keepdims=True) + RMS_EPS)
        h_ref[...] = (xv * rstd * g_ref[...]).astype(BF16)

    return _pcall(
        body, name=name, grid=(t // tm,),
        in_specs=[pl.BlockSpec((tm, d), lambda i: (i, 0)), pl.BlockSpec((1, d), lambda i: (0, 0))],
        out_specs=pl.BlockSpec((tm, d), lambda i: (i, 0)), out_shape=jax.ShapeDtypeStruct((t, d), BF16),
        blocks=[((tm, d), F32), ((tm, d), BF16)],
    )(x, gain)


def matmul_res_norm(a, b, res, next_gain, *, scale, tm, name):
    t, k = a.shape
    d = b.shape[1]
    tm = min(tm, t)
    with_norm = next_gain is not None

    def body(a_ref, b_ref, r_ref, *rest):
        out = r_ref[...] + _dot(a_ref[...], b_ref[...], "nn") * scale
        if with_norm:
            g_ref, o_ref, h_ref = rest
            rstd = lax.rsqrt(jnp.mean(out * out, axis=1, keepdims=True) + RMS_EPS)
            h_ref[...] = (out * rstd * g_ref[...]).astype(BF16)
        else:
            o_ref, = rest
        o_ref[...] = out

    row = pl.BlockSpec((tm, d), lambda i: (i, 0))
    in_specs = [pl.BlockSpec((tm, k), lambda i: (i, 0)), pl.BlockSpec((k, d), lambda i: (0, 0)), row]
    args = [a, b, res]
    out_specs, out_shape = [row], [jax.ShapeDtypeStruct((t, d), F32)]
    if with_norm:
        in_specs.append(pl.BlockSpec((1, d), lambda i: (0, 0)))
        args.append(next_gain)
        out_specs.append(row)
        out_shape.append(jax.ShapeDtypeStruct((t, d), BF16))
    outs = _pcall(
        body, name=name, grid=(t // tm,), in_specs=in_specs, out_specs=out_specs, out_shape=out_shape,
        blocks=[((tm, k), a.dtype), ((k, d), b.dtype), ((tm, d), F32), ((tm, d), F32), ((tm, d), BF16)],
    )(*args)
    return (outs[0], outs[1]) if with_norm else (outs[0], None)


def _rms_bwd_rows(dhv, xv, g, drv):
    rstd = lax.rsqrt(jnp.mean(xv * xv, axis=1, keepdims=True) + RMS_EPS)
    xh = xv * rstd
    dxh = dhv * g
    dx = drv + rstd * (dxh - xh * jnp.mean(dxh * xh, axis=1, keepdims=True))
    return dx, jnp.sum(dhv * xh, axis=0, keepdims=True)


def matmul_rms_bwd(pairs, x, gain, dres, *, tm, name):
    t, d = x.shape
    tm = min(tm, t)
    npairs = len(pairs)

    def body(*refs):
        ab = refs[:2 * npairs]
        x_ref, g_ref, dr_ref, dx_ref, dxb_ref, dg_ref = refs[2 * npairs:]
        dh = _dot(ab[0][...], ab[1][...], "nn")
        for q in range(1, npairs):
            dh = dh + _dot(ab[2 * q][...], ab[2 * q + 1][...], "nn")
        dx, part = _rms_bwd_rows(dh, x_ref[...], g_ref[...], dr_ref[...])
        dx_ref[...] = dx
        dxb_ref[...] = dx.astype(BF16)

        @pl.when(pl.program_id(0) == 0)
        def _():
            dg_ref[...] = part

        @pl.when(pl.program_id(0) > 0)
        def _():
            dg_ref[...] += part

    in_specs, args, blocks = [], [], []
    for a, b in pairs:
        k = a.shape[1]
        in_specs += [pl.BlockSpec((tm, k), lambda i: (i, 0)), pl.BlockSpec((k, d), lambda i: (0, 0))]
        args += [a, b]
        blocks += [((tm, k), a.dtype), ((k, d), b.dtype)]
    row = pl.BlockSpec((tm, d), lambda i: (i, 0))
    vec = pl.BlockSpec((1, d), lambda i: (0, 0))
    return _pcall(
        body, name=name, grid=(t // tm,), in_specs=in_specs + [row, vec, row], out_specs=[row, row, vec],
        out_shape=[jax.ShapeDtypeStruct((t, d), F32), jax.ShapeDtypeStruct((t, d), BF16), jax.ShapeDtypeStruct((1, d), F32)],
        blocks=blocks + [((tm, d), F32)] * 3 + [((tm, d), BF16)],
    )(*args, x, gain, dres)


def final_loss(x, gain, target, *, name, tm=512):
    t, d = x.shape
    tm = min(tm, t)

    def body(x_ref, g_ref, t_ref, dx_ref, dxb_ref, dg_ref, loss_ref):
        xv = x_ref[...]
        g = g_ref[...]
        rstd = lax.rsqrt(jnp.mean(xv * xv, axis=1, keepdims=True) + RMS_EPS)
        xh = xv * rstd
        err = xh * g - t_ref[...]
        dy = err * (1.0 / d)
        dxh = dy * g
        dx = rstd * (dxh - xh * jnp.mean(dxh * xh, axis=1, keepdims=True))
        dx_ref[...] = dx
        dxb_ref[...] = dx.astype(BF16)
        part = jnp.sum(dy * xh, axis=0, keepdims=True)
        sq = jnp.sum(jnp.sum(err * err, axis=1, keepdims=True), axis=0, keepdims=True) * (0.5 / d)
        lpart = jnp.broadcast_to(sq, (1, 128))

        @pl.when(pl.program_id(0) == 0)
        def _():
            dg_ref[...] = part
            loss_ref[...] = lpart

        @pl.when(pl.program_id(0) > 0)
        def _():
            dg_ref[...] += part
            loss_ref[...] += lpart

    row = pl.BlockSpec((tm, d), lambda i: (i, 0))
    vec = pl.BlockSpec((1, d), lambda i: (0, 0))
    return _pcall(
        body, name=name, grid=(t // tm,), in_specs=[row, vec, row],
        out_specs=[row, row, vec, pl.BlockSpec((1, 128), lambda i: (0, 0))],
        out_shape=[jax.ShapeDtypeStruct((t, d), F32), jax.ShapeDtypeStruct((t, d), BF16),
                   jax.ShapeDtypeStruct((1, d), F32), jax.ShapeDtypeStruct((1, 128), F32)],
        blocks=[((tm, d), F32)] * 3 + [((tm, d), BF16)],
    )(x, gain, target)


def rope_tables(t):
    pos = jnp.arange(t, dtype=F32)
    inv_freq = ROPE_THETA ** (-jnp.arange(0, ROPE_DIM, 2, dtype=F32) / ROPE_DIM)
    ang = pos[:, None] * inv_freq[None, :]
    cos, sin = jnp.cos(ang), jnp.sin(ang)
    half = ROPE_DIM // 2
    in_head = jnp.arange(128) % HEAD_DIM
    cosw, sinw = jnp.tile(cos, (1, 128 // half)), jnp.tile(sin, (1, 128 // half))
    c = jnp.where(in_head < ROPE_DIM, cosw, 1.0)
    sa = jnp.where(in_head < half, -sinw, 0.0)
    sb = jnp.where((in_head >= half) & (in_head < ROPE_DIM), sinw, 0.0)
    return jnp.concatenate([c, sa, sb], axis=1)


def _rotate(xv, cv, sav, sbv):
    halves = []
    for half in range(2):
        x = xv[:, 128 * half:128 * (half + 1)]
        halves.append(x * cv + pltpu.roll(x, 120, 1) * sav + pltpu.roll(x, 8, 1) * sbv)
    return jnp.concatenate(halves, axis=1)


STAGE_CHUNKS = 4


def _stage(tm):
    return dict(scratch_shapes=[pltpu.VMEM((STAGE_CHUNKS, tm, 128), F32)], scratch_bytes=STAGE_CHUNKS * tm * 128 * 4)


def _split_residues(stage_ref, val, out_ref, d, col, dtype):
    rows, width = val.shape
    if d == 1:
        out_ref[0, :, col:col + width] = val.astype(dtype)
        return
    chunks = width // 128
    for c in range(chunks):
        stage_ref[c] = val[:, 128 * c:128 * (c + 1)]
    for r in range(d):
        for c in range(chunks):
            out_ref[r, :, col + 128 * c:col + 128 * (c + 1)] = stage_ref[c, pl.ds(r, rows // d, stride=d), :].astype(dtype)


def _join_residues(stage_ref, in_ref, d, col=0, width=GROUP_W):
    if d == 1:
        return in_ref[0, :, col:col + width].astype(F32)
    rows = in_ref.shape[1] * d
    chunks = width // 128
    for r in range(d):
        for c in range(chunks):
            stage_ref[c, pl.ds(r, rows // d, stride=d), :] = in_ref[r, :, col + 128 * c:col + 128 * (c + 1)].astype(F32)
    return jnp.concatenate([stage_ref[c] for c in range(chunks)], axis=1)


def rope_split(proj, tables, *, name, tm=512):
    c = sa = sb = tables
    t = tables.shape[0]
    tm = min(tm, t)

    def body(*refs):
        pieces = refs[0:9]
        c_ref, sa_ref, sb_ref = refs[9:12]
        qk_out, v_out = refs[12:15], refs[15:18]
        stage = refs[18]
        cv, sav, sbv = c_ref[...], sa_ref[...], sb_ref[...]
        for g, d in enumerate(DILATIONS):
            for kind in range(3):
                xv = pieces[3 * kind + g][...].astype(F32)
                if kind < 2:
                    _split_residues(stage, _rotate(xv, cv, sav, sbv), qk_out[g], d, GROUP_W * kind, BF16)
                else:
                    _split_residues(stage, xv, v_out[g], d, 0, BF16)

    tabs = [pl.BlockSpec((tm, 128), functools.partial(lambda i, cb: (i, cb), cb=cb)) for cb in range(3)]
    in_specs = [pl.BlockSpec((tm, GROUP_W), functools.partial(lambda i, cb: (i, cb), cb=cb)) for cb in range(9)]
    out_specs = ([pl.BlockSpec((d, tm // d, 2 * GROUP_W), lambda i: (0, i, 0)) for d in DILATIONS]
                 + [pl.BlockSpec((d, tm // d, GROUP_W), lambda i: (0, i, 0)) for d in DILATIONS])
    out_shape = ([jax.ShapeDtypeStruct((d, t // d, 2 * GROUP_W), BF16) for d in DILATIONS]
                 + [jax.ShapeDtypeStruct((d, t // d, GROUP_W), BF16) for d in DILATIONS])
    outs = _pcall(
        body, name=name, grid=(t // tm,), in_specs=in_specs + tabs, out_specs=out_specs, out_shape=out_shape,
        blocks=[((tm, GROUP_W), BF16)] * 18 + [((tm, 128), F32)] * 3,
        **_stage(tm),
    )(*([proj] * 9), c, sa, sb)
    return outs[0:3], outs[3:6]


def rope_join(dqs, dks, dvs, sb_grads, dgate, tables, *, name, tm=512):
    c = sa = sb = tables
    t = tables.shape[0]
    tm = min(tm, t)

    def body(*refs):
        pieces, sb_refs, dgate_ref = refs[0:9], refs[9:12], refs[12]
        c_ref, sa_ref, sb_ref = refs[13:16]
        o_ref, stage = refs[16], refs[17]
        cv, sav, sbv = c_ref[...], -sa_ref[...], -sb_ref[...]
        for kind in range(3):
            for g, d in enumerate(DILATIONS):
                xv = _join_residues(stage, pieces[3 * kind + g], d)
                if kind < 2:
                    xv = _rotate(xv, cv, sav, sbv)
                col = GROUP_W * (3 * kind + g)
                o_ref[:, col:col + GROUP_W] = xv.astype(BF16)
        for j in range(3):
            o_ref[:, GROUP_W * (QS_BLK + j):GROUP_W * (QS_BLK + j + 1)] = sb_refs[j][...].astype(BF16)
        o_ref[:, D_MODEL * GATE_DIL_BLK:] = dgate_ref[...]

    tabs = [pl.BlockSpec((tm, 128), functools.partial(lambda i, cb: (i, cb), cb=cb)) for cb in range(3)]
    nat = lambda w: pl.BlockSpec((tm, w), lambda i: (i, 0))
    in_specs = [pl.BlockSpec((d, tm // d, GROUP_W), lambda i: (0, i, 0)) for _ in range(3) for d in DILATIONS]
    in_specs += [nat(GROUP_W)] * 3 + [nat(2 * D_MODEL)]
    return _pcall(
        body, name=name, grid=(t // tm,), in_specs=in_specs + tabs,
        out_specs=nat(D_IN), out_shape=jax.ShapeDtypeStruct((t, D_IN), BF16),
        blocks=[((tm, GROUP_W), F32)] * 12 + [((tm, 128), F32)] * 3 + [((tm, 2 * D_MODEL), BF16), ((tm, D_IN), BF16)],
        **_stage(tm),
    )(*dqs, *dks, *dvs, *sb_grads, dgate, c, sa, sb)


def _head_mask(h):
    lane = lax.broadcasted_iota(jnp.int32, (1, GROUP_W), 1)
    return (lane // HEAD_DIM) == h


def _band_mask_before():
    ri = lax.broadcasted_iota(jnp.int32, (4 * DIL_SPAN, DIL_SPAN), 0) % DIL_SPAN
    ci = lax.broadcasted_iota(jnp.int32, (4 * DIL_SPAN, DIL_SPAN), 1)
    return ci >= ri


def dil_fwd(qk, v, *, name):
    d, nsub, _ = qk.shape
    nblk = nsub // DIL_SPAN

    def body(q_ref, kc_ref, kp_ref, vc_ref, vp_ref, o_ref, lse_ref):
        nb = pl.program_id(1)
        kk = jnp.concatenate([kp_ref[0], kc_ref[0]], axis=0)
        vv = jnp.concatenate([vp_ref[0], vc_ref[0]], axis=0)
        s = _dot(_stack_heads(q_ref[0] * ATT_SCALE), kk, "nt")
        ri = lax.broadcasted_iota(jnp.int32, s.shape, 0) % DIL_SPAN
        ci = lax.broadcasted_iota(jnp.int32, s.shape, 1)
        valid = ((ci < DIL_SPAN) & (ci >= ri) & (nb > 0)) | ((ci >= DIL_SPAN) & (ci - DIL_SPAN <= ri))
        s = jnp.where(valid, s, -jnp.inf)
        m = jnp.max(s, axis=1, keepdims=True)
        p = jnp.exp(s - m)
        den = jnp.sum(p, axis=1, keepdims=True)
        o_ref[0] = _unstack_heads(_dot(p, vv, "nn") / den, DIL_SPAN)
        lse = m + jnp.log(den)
        for h in range(4):
            lse_ref[0, :, 128 * h:128 * (h + 1)] = jnp.broadcast_to(lse[DIL_SPAN * h:DIL_SPAN * (h + 1)], (DIL_SPAN, 128))

    blk = (1, DIL_SPAN, GROUP_W)
    sblk = (1, DIL_SPAN, 512)
    prv = lambda nb: jnp.maximum(nb - 1, 0)
    return _pcall(
        body, name=name, grid=(d, nblk),
        in_specs=[pl.BlockSpec(blk, lambda r, nb: (r, nb, 0)),
                  pl.BlockSpec(blk, lambda r, nb: (r, nb, 1)),
                  pl.BlockSpec(blk, lambda r, nb: (r, prv(nb), 1)),
                  pl.BlockSpec(blk, lambda r, nb: (r, nb, 0)),
                  pl.BlockSpec(blk, lambda r, nb: (r, prv(nb), 0))],
        out_specs=[pl.BlockSpec(blk, lambda r, nb: (r, nb, 0)), pl.BlockSpec(sblk, lambda r, nb: (r, nb, 0))],
        out_shape=[jax.ShapeDtypeStruct((d, nsub, GROUP_W), F32), jax.ShapeDtypeStruct((d, nsub, 512), F32)],
        blocks=[(blk, BF16)] * 5 + [(blk, F32), (sblk, F32)],
    )(qk, qk, qk, v, v)


def dil_merge(outs, lses, *, name, tm=512):
    t = outs[0].shape[0] * outs[0].shape[1]
    tm = min(tm, t)

    def body(o0, o1, o2, l0, l1, l2, o_ref, lse_ref, stage):
        ls = [_join_residues(stage, l, d, 0, 512) for l, d in zip((l0, l1, l2), DILATIONS)]
        m = jnp.maximum(jnp.maximum(ls[0], ls[1]), ls[2])
        tot = m + jnp.log(jnp.exp(ls[0] - m) + jnp.exp(ls[1] - m) + jnp.exp(ls[2] - m))
        lse_ref[...] = tot
        lane = lax.broadcasted_iota(jnp.int32, (1, 128), 1)
        first = lane < HEAD_DIM
        acc = jnp.zeros((tm, GROUP_W), F32)
        for og, lg, d in zip((o0, o1, o2), ls, DILATIONS):
            w = jnp.exp(lg - tot)
            wide = jnp.concatenate([jnp.where(first, w[:, 0:128], w[:, 128:256]),
                                    jnp.where(first, w[:, 256:384], w[:, 384:512])], axis=1)
            acc = acc + wide * _join_residues(stage, og, d)
        o_ref[...] = acc

    o_in = [pl.BlockSpec((d, tm // d, GROUP_W), lambda i: (0, i, 0)) for d in DILATIONS]
    l_in = [pl.BlockSpec((d, tm // d, 512), lambda i: (0, i, 0)) for d in DILATIONS]
    return _pcall(
        body, name=name, grid=(t // tm,), in_specs=o_in + l_in,
        out_specs=[pl.BlockSpec((tm, GROUP_W), lambda i: (i, 0)), pl.BlockSpec((tm, 512), lambda i: (i, 0))],
        out_shape=[jax.ShapeDtypeStruct((t, GROUP_W), F32), jax.ShapeDtypeStruct((t, 512), F32)],
        blocks=[((tm, GROUP_W), F32)] * 4 + [((tm, 512), F32)] * 4,
        **_stage(tm),
    )(*outs, *lses)


def dil_bwd_prep(do, o, lse, *, name, tm=512):
    t = do.shape[0]
    tm = min(tm, t)
    wide = DILATIONS[1:]

    def body(do_ref, o_ref, lse_ref, ds_ref, *rest):
        do_out, lse_out, ds_out = rest[0:2], rest[2:4], rest[4:6]
        stage = rest[6]
        dov = do_ref[...]
        prod = dov * o_ref[...]
        for h in range(4):
            s = jnp.sum(jnp.where(_head_mask(h), prod, 0.0), axis=1, keepdims=True)
            ds_ref[:, 128 * h:128 * (h + 1)] = jnp.broadcast_to(s, (tm, 128))
        for i, d in enumerate(wide):
            _split_residues(stage, dov, do_out[i], d, 0, BF16)
            _split_residues(stage, lse_ref[...], lse_out[i], d, 0, F32)
            _split_residues(stage, ds_ref[...], ds_out[i], d, 0, F32)

    nat = lambda w: pl.BlockSpec((tm, w), lambda i: (i, 0))
    res = lambda d, w: pl.BlockSpec((d, tm // d, w), lambda i: (0, i, 0))
    shape = lambda d, w, dt: jax.ShapeDtypeStruct((d, t // d, w), dt)
    outs = _pcall(
        body, name=name, grid=(t // tm,), in_specs=[nat(GROUP_W), nat(GROUP_W), nat(512)],
        out_specs=[nat(512)] + [res(d, GROUP_W) for d in wide] + [res(d, 512) for d in wide] * 2,
        out_shape=([jax.ShapeDtypeStruct((t, 512), F32)] + [shape(d, GROUP_W, BF16) for d in wide]
                   + [shape(d, 512, F32) for d in wide] * 2),
        blocks=[((tm, GROUP_W), F32)] * 3 + [((tm, 512), F32)] * 6,
        **_stage(tm),
    )(do, o, lse)
    return outs[0], outs[1:3], outs[3:5], outs[5:7]


def head_sums(a, b, *, name, tm=512):
    t = a.shape[0]
    tm = min(tm, t)

    def body(a_ref, b_ref, o_ref):
        prod = a_ref[...].astype(BF16).astype(F32) * b_ref[...]
        for h in range(4):
            s = jnp.sum(jnp.where(_head_mask(h), prod, 0.0), axis=1, keepdims=True)
            o_ref[:, 128 * h:128 * (h + 1)] = jnp.broadcast_to(s, (tm, 128))

    spec = pl.BlockSpec((tm, GROUP_W), lambda i: (i, 0))
    return _pcall(
        body, name=name, grid=(t // tm,), in_specs=[spec, spec],
        out_specs=pl.BlockSpec((tm, 512), lambda i: (i, 0)), out_shape=jax.ShapeDtypeStruct((t, 512), F32),
        blocks=[((tm, GROUP_W), F32)] * 2 + [((tm, 512), F32)],
    )(a, b)


def dil_bwd(qk, v, do, lse, dsum, *, name):
    d, nsub, _ = qk.shape
    nblk = nsub // DIL_SPAN

    def body(qa_ref, qb_ref, kc_ref, kp_ref, vc_ref, vp_ref, doa_ref, dob_ref, la_ref, lb_ref, sa_ref, sb_ref,
             dq_ref, dk_ref, dv_ref):
        nb = pl.program_id(1)
        nxt = _band_mask_before() & (nb < nblk - 1)
        kc, kp, vc, vp = kc_ref[0], kp_ref[0], vc_ref[0], vp_ref[0]
        qas, qbs = _stack_heads(qa_ref[0] * ATT_SCALE), _stack_heads(qb_ref[0] * ATT_SCALE)
        das, dbs = _stack_heads(doa_ref[0].astype(BF16)), _stack_heads(dob_ref[0].astype(BF16))
        stat = lambda ref: jnp.concatenate([ref[0, :, 128 * h:128 * (h + 1)] for h in range(4)], axis=0)
        la, lb, sa, sb = stat(la_ref), stat(lb_ref), stat(sa_ref), stat(sb_ref)

        def probs(qs, ds_, k, v, mask, l, s):
            p = jnp.where(mask, jnp.exp(_dot(qs, k, "nt") - l), 0.0)
            dsc = p * (_dot(ds_, v, "nt") - s)
            return p.astype(BF16), dsc.astype(BF16)

        wide = lambda a: jnp.concatenate([a, a], axis=1)
        ri = lax.broadcasted_iota(jnp.int32, (4 * DIL_SPAN, 2 * DIL_SPAN), 0) % DIL_SPAN
        ci = lax.broadcasted_iota(jnp.int32, (4 * DIL_SPAN, 2 * DIL_SPAN), 1)
        valid = ((ci < DIL_SPAN) & (ci >= ri) & (nb > 0)) | ((ci >= DIL_SPAN) & (ci - DIL_SPAN <= ri))
        p_a, ds_a = probs(qas, das, jnp.concatenate([kp, kc], axis=0), jnp.concatenate([vp, vc], axis=0),
                          valid, wide(la), wide(sa))
        p_nc, ds_nc = probs(qbs, dbs, kc, vc, nxt, lb, sb)
        dq_ref[0] = _unstack_heads(_dot(ds_a, jnp.concatenate([kp, kc], axis=0), "nn"), DIL_SPAN) * ATT_SCALE
        dk_ref[0] = _dot(ds_a[:, DIL_SPAN:], qas, "tn") + _dot(ds_nc, qbs, "tn")
        dv_ref[0] = _dot(p_a[:, DIL_SPAN:], das, "tn") + _dot(p_nc, dbs, "tn")

    blk = (1, DIL_SPAN, GROUP_W)
    sblk = (1, DIL_SPAN, 512)
    prv = lambda nb: jnp.maximum(nb - 1, 0)
    nxt_ = lambda nb: jnp.minimum(nb + 1, nblk - 1)
    cur_at = lambda c: pl.BlockSpec(blk, functools.partial(lambda r, nb, c: (r, nb, c), c=c))
    prv_at = lambda c: pl.BlockSpec(blk, functools.partial(lambda r, nb, c: (r, prv(nb), c), c=c))
    nxt_at = lambda c: pl.BlockSpec(blk, functools.partial(lambda r, nb, c: (r, nxt_(nb), c), c=c))
    s_cur = pl.BlockSpec(sblk, lambda r, nb: (r, nb, 0))
    s_nxt = pl.BlockSpec(sblk, lambda r, nb: (r, nxt_(nb), 0))
    o_spec = pl.BlockSpec(blk, lambda r, nb: (r, nb, 0))
    o_shape = jax.ShapeDtypeStruct((d, nsub, GROUP_W), F32)
    return _pcall(
        body, name=name, grid=(d, nblk),
        in_specs=[cur_at(0), nxt_at(0), cur_at(1), prv_at(1), cur_at(0), prv_at(0), cur_at(0), nxt_at(0),
                  s_cur, s_nxt, s_cur, s_nxt],
        out_specs=[o_spec, o_spec, o_spec], out_shape=[o_shape, o_shape, o_shape],
        blocks=[(blk, BF16)] * 6 + [(blk, F32)] * 5 + [(sblk, F32)] * 4,
    )(qk, qk, qk, qk, v, v, do, do, lse, lse, dsum, dsum)


def _tri_dot(x, b):
    hi = x.astype(BF16)
    lo = (x - hi.astype(F32)).astype(BF16)
    return _dot(jnp.concatenate([hi, lo], axis=1), jnp.concatenate([b, b], axis=0), "nn")


SB_TILE = 256
SB_ROWS = 512


def _stack_heads(a):
    return jnp.concatenate([jnp.where(_head_mask(h), a, jnp.zeros_like(a)) for h in range(4)], axis=0)


def _unstack_heads(acc, rows):
    out = acc[0:rows]
    for h in range(1, 4):
        out = jnp.where(_head_mask(h), acc[h * rows:(h + 1) * rows], out)
    return out


def _tri_masks(n):
    ri = lax.broadcasted_iota(jnp.int32, (n, n), 0)
    ci = lax.broadcasted_iota(jnp.int32, (n, n), 1)
    return (ri > ci).astype(BF16), (ri >= ci).astype(BF16)


def _sb_weights(qs, kt, after, c_keep, lead):
    z = _dot(qs, kt, "nt")
    lbeta = jnp.minimum(z, 0.0) - jnp.log(1.0 + jnp.exp(-jnp.abs(z)))
    lkeep = lbeta - z
    past = None
    if lead is not None:
        query = lax.broadcasted_iota(jnp.int32, z.shape, 0) % SB_ROWS
        past = lax.broadcasted_iota(jnp.int32, z.shape, 1) + lead < query
        lkeep = jnp.where(past, lkeep, 0.0)
    w = jnp.exp(lbeta + _tri_dot(lkeep, after) + c_keep)
    if lead is not None:
        w = jnp.where(past, w, 0.0)
    return z, past, lbeta, lkeep, w


def _sb_walk(qb, tile, carry):
    per = SB_ROWS // SB_TILE
    for i in reversed(range(per)):
        carry = tile(pl.multiple_of(qb * SB_ROWS + i * SB_TILE, SB_TILE), i * SB_TILE, i == per - 1, carry)
    past_tiles = qb * per
    return lax.fori_loop(0, past_tiles,
                         lambda it, c: tile(pl.multiple_of((past_tiles - 1 - it) * SB_TILE, SB_TILE), None, False, c), carry)


def sb_fwd(proj, *, name):
    t = proj.shape[0]
    n, m = SB_TILE, SB_ROWS
    assert t % m == 0

    def body(q_ref, k_ref, v_ref, o_ref, acc_ref):
        qb = pl.program_id(0)
        qs = _stack_heads(q_ref[...] * ATT_SCALE)
        after, _ = _tri_masks(n)

        def tile(off, lead, first, c_keep):
            kt = k_ref[pl.ds(off, n), :]
            vt = v_ref[pl.ds(off, n), :]
            _, _, _, lkeep, w = _sb_weights(qs, kt, after, c_keep, lead)
            pv = _tri_dot(w, vt)
            if first:
                acc_ref[...] = pv
            else:
                acc_ref[...] += pv
            return c_keep + jnp.sum(lkeep, axis=1, keepdims=True)

        _sb_walk(qb, tile, jnp.zeros((4 * m, 1), F32))
        o_ref[...] = _unstack_heads(acc_ref[...], m)

    full = lambda cb: pl.BlockSpec((t, GROUP_W), functools.partial(lambda i, cb: (0, cb), cb=cb))
    return _pcall(
        body, name=name, grid=(t // m,),
        in_specs=[pl.BlockSpec((m, GROUP_W), lambda i: (i, QS_BLK)), full(KS_BLK), full(VS_BLK)],
        out_specs=pl.BlockSpec((m, GROUP_W), lambda i: (i, 0)), out_shape=jax.ShapeDtypeStruct((t, GROUP_W), F32),
        blocks=[((m, GROUP_W), BF16), ((t, GROUP_W), BF16), ((t, GROUP_W), BF16), ((m, GROUP_W), F32)],
        scratch_shapes=[pltpu.VMEM((4 * m, GROUP_W), F32)], scratch_bytes=4 * m * GROUP_W * 4,
    )(proj, proj, proj)


def sb_bwd(proj, do, gtot, *, name):
    t = proj.shape[0]
    n, m = SB_TILE, SB_ROWS
    assert t % m == 0

    def body(q_ref, k_ref, v_ref, do_ref, gt_ref, dq_ref, dk_ref, dv_ref, acc_ref):
        qb = pl.program_id(0)

        @pl.when(qb == 0)
        def _():
            dk_ref[...] = jnp.zeros_like(dk_ref)
            dv_ref[...] = jnp.zeros_like(dv_ref)

        qs = _stack_heads(q_ref[...] * ATT_SCALE)
        dos = _stack_heads(do_ref[...].astype(BF16))
        gt = jnp.concatenate([jnp.max(gt_ref[:, 128 * h:128 * (h + 1)], axis=1, keepdims=True) for h in range(4)], axis=0)
        after, from_on = _tri_masks(n)

        def tile(off, lead, first, carry):
            c_keep, c_g = carry
            kt = k_ref[pl.ds(off, n), :]
            vt = v_ref[pl.ds(off, n), :]
            z, past, lbeta, lkeep, w = _sb_weights(qs, kt, after, c_keep, lead)
            gw = w * _dot(dos, vt, "nt")
            big_g = gt - (_tri_dot(gw, from_on) + c_g)
            dz = gw * jnp.exp(lbeta - z) - big_g * jnp.exp(lbeta)
            if lead is not None:
                dz = jnp.where(past, dz, 0.0)
            dz = dz.astype(BF16)
            dk_ref[pl.ds(off, n), :] += _dot(dz, qs, "tn")
            dv_ref[pl.ds(off, n), :] += _dot(w, dos, "tn")
            dq = _dot(dz, kt, "nn")
            if first:
                acc_ref[...] = dq
            else:
                acc_ref[...] += dq
            return c_keep + jnp.sum(lkeep, axis=1, keepdims=True), c_g + jnp.sum(gw, axis=1, keepdims=True)

        zero_col = jnp.zeros((4 * m, 1), F32)
        _sb_walk(qb, tile, (zero_col, zero_col))
        dq_ref[...] = _unstack_heads(acc_ref[...], m) * ATT_SCALE

    full = lambda cb: pl.BlockSpec((t, GROUP_W), functools.partial(lambda i, cb: (0, cb), cb=cb))
    whole = pl.BlockSpec((t, GROUP_W), lambda i: (0, 0))
    rowblk = pl.BlockSpec((m, GROUP_W), lambda i: (i, 0))
    shape = jax.ShapeDtypeStruct((t, GROUP_W), F32)
    return _pcall(
        body, name=name, grid=(t // m,),
        in_specs=[pl.BlockSpec((m, GROUP_W), lambda i: (i, QS_BLK)), full(KS_BLK), full(VS_BLK), rowblk,
                  pl.BlockSpec((m, 512), lambda i: (i, 0))],
        out_specs=[rowblk, whole, whole], out_shape=[shape, shape, shape],
        blocks=[((m, GROUP_W), BF16), ((t, GROUP_W), BF16), ((t, GROUP_W), BF16), ((m, GROUP_W), F32),
                ((m, 512), F32), ((m, GROUP_W), F32), ((t, GROUP_W), F32), ((t, GROUP_W), F32)],
        scratch_shapes=[pltpu.VMEM((4 * m, GROUP_W), F32)], scratch_bytes=4 * m * GROUP_W * 4,
    )(proj, proj, proj, do, gtot)


def _mesh_place():
    return lax.axis_index("x"), lax.axis_index("y"), lax.axis_index("c")


def _flip(place, mask):
    x, y, c = place
    return ((1 - x) if mask & 4 else x, (1 - y) if mask & 2 else y, (1 - c) if mask & 1 else c)


def _dev_index(place):
    x, y, c = place
    return 4 * x + 2 * y + c


HBM_SPEC = pl.BlockSpec(memory_space=pltpu.HBM)


def all_gather_rows(shard, after, *, name):
    rows, lanes = shard.shape

    def body(x_ref, after_ref, out_ref, send_sems, recv_sems, local_sem):
        me = _mesh_place()
        x, y, c = me
        sibling = _flip(me, 1)
        chips = [_flip(me, 4), _flip(me, 2), _flip(me, 6)]

        def copy(k, block, to, src=None):
            dst = out_ref.at[_dev_index(block)]
            return pltpu.make_async_remote_copy(
                src_ref=dst if src is None else src, dst_ref=dst, send_sem=send_sems.at[k], recv_sem=recv_sems.at[k],
                device_id=to, device_id_type=pl.DeviceIdType.MESH)

        mine = pltpu.make_async_copy(x_ref, out_ref.at[_dev_index(me)], local_sem)
        mine.start()
        first = [copy(0, me, sibling, src=x_ref)] + [copy(1 + j, me, chip, src=x_ref) for j, chip in enumerate(chips)]
        for cp in first:
            cp.start()
        passed = [copy(4 + j, chip, sibling) for j, chip in enumerate(chips)]
        for j, chip in enumerate(chips):
            copy(1 + j, chip, me).wait_recv()
            passed[j].start()
        copy(0, sibling, me).wait_recv()
        for j, chip in enumerate(chips):
            copy(4 + j, _flip(chip, 1), me).wait_recv()
        for cp in first + passed:
            cp.wait_send()
        mine.wait()

    return pl.pallas_call(
        body, name=name, in_specs=[HBM_SPEC, pl.BlockSpec(memory_space=pl.ANY)], out_specs=HBM_SPEC,
        out_shape=jax.ShapeDtypeStruct((N_DEV, rows, lanes), shard.dtype),
        scratch_shapes=[pltpu.SemaphoreType.DMA((7,)), pltpu.SemaphoreType.DMA((7,)), pltpu.SemaphoreType.DMA],
    )(shard, after)


SEM_SPEC = pl.BlockSpec(memory_space=pltpu.SEMAPHORE)
DATAFLOW_EFFECT = pltpu.SideEffectType.DATAFLOW_SIDE_EFFECTING


ALL_PEERS = tuple(range(1, N_DEV))
CHIP_PEERS = (1, 4, 2, 6)
OTHER_CHIPS = (4, 2, 6)


def _spread_copies(src_refs, land_refs, send_sems, recv_sems, per_peer, masks, arriving):
    me = _mesh_place()
    my = _dev_index(me)
    remote, local = [], []
    for t, (src_ref, land_ref) in enumerate(zip(src_refs, land_refs)):
        for i, mask in enumerate(masks):
            peer = _flip(me, mask)
            data_of = my if arriving else _dev_index(peer)
            slot = _dev_index(peer) if arriving else my
            k = t * len(masks) + i
            remote.append(pltpu.make_async_remote_copy(
                src_ref=src_ref.at[data_of] if per_peer else src_ref, dst_ref=land_ref.at[slot],
                send_sem=send_sems.at[k], recv_sem=recv_sems.at[k],
                device_id=peer, device_id_type=pl.DeviceIdType.MESH))
        local.append(pltpu.make_async_copy(src_ref.at[my] if per_peer else src_ref, land_ref.at[my],
                                           send_sems.at[len(src_refs) * len(masks) + t]))
    return remote, local


def spread_start(srcs, *, per_peer, name, masks=ALL_PEERS):
    nt = len(srcs)
    zones = [pltpu.HBM((N_DEV,) + (s.shape[1:] if per_peer else s.shape), s.dtype) for s in srcs]

    def body(*refs):
        src_refs, (send_sems, recv_sems) = refs[:nt], refs[nt:nt + 2]
        land_refs, token = refs[2 * nt + 2:3 * nt + 2], refs[3 * nt + 2]
        remote, local = _spread_copies(src_refs, land_refs, send_sems, recv_sems, per_peer, masks, arriving=False)
        for cp in remote + local:
            cp.start()
        token[...] = jnp.zeros_like(token)

    return pl.pallas_call(
        body, name=name, in_specs=(HBM_SPEC,) * nt,
        out_shape=(pltpu.SemaphoreType.DMA((nt * len(masks) + nt,)), pltpu.SemaphoreType.DMA((nt * len(masks),)),
                   *[pltpu.HBM(s.shape, s.dtype) for s in srcs], *zones, jax.ShapeDtypeStruct((8, 128), F32)),
        out_specs=(SEM_SPEC, SEM_SPEC) + (HBM_SPEC,) * (2 * nt) + (pl.BlockSpec(memory_space=pltpu.VMEM),),
        input_output_aliases={t: 2 + t for t in range(nt)},
        compiler_params=pltpu.CompilerParams(has_side_effects=DATAFLOW_EFFECT),
    )(*[pltpu.with_memory_space_constraint(s, pltpu.HBM) for s in srcs])


def spread_wait(started, after, *, per_peer, name, masks=ALL_PEERS):
    nt = (len(started) - 3) // 2
    send_sems, recv_sems = started[0:2]
    srcs_thru, lands_thru = started[2:2 + nt], started[2 + nt:2 + 2 * nt]

    def body(*refs):
        src_refs, land_refs = refs[:nt], refs[nt:2 * nt]
        send_sems, recv_sems = refs[2 * nt:2 * nt + 2]
        remote, local = _spread_copies(src_refs, land_refs, send_sems, recv_sems, per_peer, masks, arriving=True)
        for cp in remote:
            cp.wait_send()
            cp.wait_recv()
        for cp in local:
            cp.wait()

    outs = pl.pallas_call(
        body, name=name, in_specs=(HBM_SPEC,) * (2 * nt) + (SEM_SPEC, SEM_SPEC, pl.BlockSpec(memory_space=pl.ANY)),
        out_shape=tuple(pltpu.HBM(a.shape, a.dtype) for a in (*srcs_thru, *lands_thru)),
        out_specs=(HBM_SPEC,) * (2 * nt), input_output_aliases={t: t for t in range(2 * nt)},
        compiler_params=pltpu.CompilerParams(has_side_effects=DATAFLOW_EFFECT),
    )(*srcs_thru, *lands_thru, send_sems, recv_sems, after)
    return list(outs[nt:])


def _relay_copies(land_refs, send_sems, recv_sems, arriving):
    me = _mesh_place()
    sibling = _flip(me, 1)
    out = []
    for t, land_ref in enumerate(land_refs):
        for i, mask in enumerate(OTHER_CHIPS):
            slot = _dev_index(_flip(sibling if arriving else me, mask))
            k = t * len(OTHER_CHIPS) + i
            out.append(pltpu.make_async_remote_copy(
                src_ref=land_ref.at[slot], dst_ref=land_ref.at[slot], send_sem=send_sems.at[k], recv_sem=recv_sems.at[k],
                device_id=sibling, device_id_type=pl.DeviceIdType.MESH))
    return out


def relay_start(lands, *, name):
    nt = len(lands)
    n_sem = nt * len(OTHER_CHIPS)

    def body(*refs):
        for cp in _relay_copies(refs[:nt], refs[nt], refs[nt + 1], arriving=False):
            cp.start()

    return pl.pallas_call(
        body, name=name, in_specs=(HBM_SPEC,) * nt,
        out_shape=(pltpu.SemaphoreType.DMA((n_sem,)), pltpu.SemaphoreType.DMA((n_sem,)),
                   *[pltpu.HBM(a.shape, a.dtype) for a in lands]),
        out_specs=(SEM_SPEC, SEM_SPEC) + (HBM_SPEC,) * nt, input_output_aliases={t: 2 + t for t in range(nt)},
        compiler_params=pltpu.CompilerParams(has_side_effects=DATAFLOW_EFFECT),
    )(*[pltpu.with_memory_space_constraint(a, pltpu.HBM) for a in lands])


def relay_wait(started, *, name):
    send_sems, recv_sems = started[0:2]
    lands_thru = started[2:]
    nt = len(lands_thru)

    def body(*refs):
        for cp in _relay_copies(refs[:nt], refs[nt], refs[nt + 1], arriving=True):
            cp.wait_send()
            cp.wait_recv()

    return list(pl.pallas_call(
        body, name=name, in_specs=(HBM_SPEC,) * nt + (SEM_SPEC, SEM_SPEC),
        out_shape=tuple(pltpu.HBM(a.shape, a.dtype) for a in lands_thru), out_specs=(HBM_SPEC,) * nt,
        input_output_aliases={t: t for t in range(nt)},
        compiler_params=pltpu.CompilerParams(has_side_effects=DATAFLOW_EFFECT),
    )(*lands_thru, send_sems, recv_sems))


def sum_partials(parts, *, name, tr):
    _, rows, lanes = parts.shape
    assert rows % tr == 0

    def body(p_ref, g_ref):
        g = p_ref[0].astype(F32)
        for k in range(1, N_DEV):
            g = g + p_ref[k].astype(F32)
        g_ref[...] = g

    return _pcall(
        body, name=name, grid=(rows // tr,),
        in_specs=[pl.BlockSpec((N_DEV, tr, lanes), lambda i: (0, i, 0))],
        out_specs=pl.BlockSpec((tr, lanes), lambda i: (i, 0)), out_shape=jax.ShapeDtypeStruct((rows, lanes), F32),
        blocks=[((N_DEV, tr, lanes), parts.dtype), ((tr, lanes), F32)],
    )(parts)


def adamw(g, w, m, v, *, name, tr):
    nl, k, n = w.shape
    tr = max(c for c in range(8, min(tr, k) + 1, 8) if k % c == 0)
    bc1 = 1.0 - ADAM_B1 ** ADAM_STEP
    bc2 = 1.0 - ADAM_B2 ** ADAM_STEP

    def body(g_ref, w_ref, m_ref, v_ref, d_ref, mo_ref, vo_ref):
        gv = g_ref[...]
        m_new = ADAM_B1 * m_ref[...] + (1.0 - ADAM_B1) * gv
        v_new = ADAM_B2 * v_ref[...] + (1.0 - ADAM_B2) * (gv * gv)
        mo_ref[...] = m_new
        vo_ref[...] = v_new
        d_ref[...] = -ADAM_LR * ((m_new / bc1) / (jnp.sqrt(v_new / bc2) + ADAM_EPS) + ADAM_WD * w_ref[...])

    spec = pl.BlockSpec((1, tr, n), lambda l, i: (l, i, 0))
    shape = jax.ShapeDtypeStruct(w.shape, F32)
    return _pcall(
        body, name=name, grid=(nl, k // tr), in_specs=[spec] * 4, out_specs=[spec] * 3, out_shape=[shape] * 3,
        blocks=[((1, tr, n), F32)] * 7,
    )(g, w, m, v)


def sum_adamw(partials, w, m, v, *, name, tr, transposed=False):
    nl, k, n = w.shape
    assert nl == len(partials) == 2
    step = 128 if transposed else 8
    tr = max(c for c in range(step, min(tr, k) + 1, step) if k % c == 0)
    bc1 = 1.0 - ADAM_B1 ** ADAM_STEP
    bc2 = 1.0 - ADAM_B2 ** ADAM_STEP

    def body(p0_ref, p1_ref, w_ref, m_ref, v_ref, g_ref, d_ref, mo_ref, vo_ref):
        first = pl.program_id(0) == 0
        gv = jnp.where(first, p0_ref[0], p1_ref[0]).astype(F32)
        for s in range(1, N_DEV):
            gv = gv + jnp.where(first, p0_ref[s], p1_ref[s]).astype(F32)
        if transposed:
            gv = gv.T
        m_new = ADAM_B1 * m_ref[0] + (1.0 - ADAM_B1) * gv
        v_new = ADAM_B2 * v_ref[0] + (1.0 - ADAM_B2) * (gv * gv)
        g_ref[0] = gv
        mo_ref[0] = m_new
        vo_ref[0] = v_new
        d_ref[0] = -ADAM_LR * ((m_new / bc1) / (jnp.sqrt(v_new / bc2) + ADAM_EPS) + ADAM_WD * w_ref[0])

    spec = pl.BlockSpec((1, tr, n), lambda l, i: (l, i, 0))
    if transposed:
        p0spec = pl.BlockSpec((N_DEV, n, tr), lambda l, i: (0, 0, i * (1 - l)))
        p1spec = pl.BlockSpec((N_DEV, n, tr), lambda l, i: (0, 0, i * l))
    else:
        p0spec = pl.BlockSpec((N_DEV, tr, n), lambda l, i: (0, i * (1 - l), 0))
        p1spec = pl.BlockSpec((N_DEV, tr, n), lambda l, i: (0, i * l, 0))
    shape = jax.ShapeDtypeStruct(w.shape, F32)
    return _pcall(
        body, name=name, grid=(nl, k // tr), in_specs=[p0spec, p1spec, spec, spec, spec], out_specs=[spec] * 4,
        out_shape=[shape] * 4, blocks=[((N_DEV, tr, n), BF16)] * 2 + [((1, tr, n), F32)] * 7,
    )(partials[0], partials[1], w, m, v)


def travelling(a, by_cols):
    return jnp.swapaxes(a, -1, -2) if by_cols else a


def _row(v):
    return v.reshape(1, -1)


def ffn_fwd(x, h, w, pre, tag, next_gain):
    ga, gb, s = swiglu_fwd(h, w[pre + "_w_gate"], w[pre + "_w_up"], name=f"{tag}_gateup")
    if callable(w[pre + "_w_down"]):
        w[pre + "_w_down"] = w[pre + "_w_down"](s)
    out, h_next = matmul_res_norm(s, w[pre + "_w_down"], x, next_gain, scale=0.5, tm=512, name=f"{tag}_down")
    return out, h_next, (x, h, ga, gb, s)


def ffn_bwd_weights(dxb, saved, w, pre, tag, ship=None):
    x, h, a, b, s = saved
    token = None
    grads = {}

    def made(name, g):
        grads[name] = g
        return ship(name, g) if ship else None

    da, db = swiglu_bwd(dxb, w[pre + "_w_down"], a, b, scale=0.5, name=f"{tag}_dgateup")
    token = made(pre + "_w_down", matmul(s, dxb, "tn", tm=1408, tn=1024, tk=2048, out_dtype=BF16, scale=0.5, name=f"{tag}_gdown"))
    token = made(pre + "_w_gate", matmul(da, h, "tn", tm=1408, tn=1024, tk=2048, out_dtype=BF16, after=token, name=f"{tag}_ggate"))
    token = made(pre + "_w_up", matmul(db, h, "tn", tm=1408, tn=1024, tk=2048, out_dtype=BF16, after=token, name=f"{tag}_gup"))
    return grads, (da, db), token


def ffn_bwd_input(dx, rest, saved, gain, w, pre, tag):
    da, db = rest
    x = saved[0]
    return matmul_rms_bwd([(da, w[pre + "_w_gate"]), (db, w[pre + "_w_up"])], x, gain, dx, tm=256, name=f"{tag}_dh")


def mixer_fwd(x, h, w, tables, tag, next_gain):
    proj = matmul(h, w["w_in"], "nt", tm=512, tn=1280, tk=1024, out_dtype=BF16, name=f"{tag}_in")
    qks, vs = rope_split(proj, tables, name=f"{tag}_rope")
    outs, lses = [], []
    for g in range(N_DIL_GROUPS):
        o, lse = dil_fwd(qks[g], vs[g], name=f"{tag}_dil{g}")
        outs.append(o)
        lses.append(lse)
    odil, lse = dil_merge(outs, lses, name=f"{tag}_merge")
    osb = sb_fwd(proj, name=f"{tag}_sb")
    for n in ("w_proj_dil", "w_proj_sb", "w_out"):
        if callable(w[n]):
            w[n] = w[n](osb)
    y, u1, u2 = gate_fwd(odil, osb, w["w_proj_dil"], w["w_proj_sb"], proj, name=f"{tag}_gate")
    out, h_next = matmul_res_norm(y, w["w_out"], x, next_gain, scale=1.0, tm=512, name=f"{tag}_out")
    return out, h_next, (x, h, proj, qks, vs, odil, lse, osb, u1, u2, y)


def mixer_bwd_weights(dxb, saved, w, tables, tag):
    x, h, proj, qks, vs, odil, lse, osb, u1, u2, y = saved
    t = x.shape[0]
    g_out = matmul(y, dxb, "tn", tm=1024, tn=1024, tk=2048, out_dtype=BF16, name=f"{tag}_gout")
    du1, du2, dgate = gate_bwd(dxb, w["w_out"], u1, u2, proj, name=f"{tag}_dgate")
    g_pd = matmul(du1, odil, "tn", tm=1024, tn=256, tk=2048, out_dtype=BF16, name=f"{tag}_gpd")
    g_ps = matmul(du2, osb, "tn", tm=1024, tn=256, tk=2048, out_dtype=BF16, name=f"{tag}_gps")
    dodil = matmul(du1, w["w_proj_dil"], "nn", tm=512, tn=256, tk=1024, out_dtype=F32, name=f"{tag}_dodil")
    dosb = matmul(du2, w["w_proj_sb"], "nn", tm=512, tn=256, tk=1024, out_dtype=F32, name=f"{tag}_dosb")
    dsum, do_wide, lse_wide, dsum_wide = dil_bwd_prep(dodil, odil, lse, name=f"{tag}_dprep")
    dos = [dodil[None]] + list(do_wide)
    lss = [lse[None]] + list(lse_wide)
    dss = [dsum[None]] + list(dsum_wide)
    dqs, dks, dvs = [], [], []
    for g in range(N_DIL_GROUPS):
        dq, dk, dv = dil_bwd(qks[g], vs[g], dos[g], lss[g], dss[g], name=f"{tag}_ddil{g}")
        dqs.append(dq)
        dks.append(dk)
        dvs.append(dv)
    gtot = head_sums(dosb, osb, name=f"{tag}_gsum")
    sb_grads = sb_bwd(proj, dosb, gtot, name=f"{tag}_dsb")
    dproj = rope_join(dqs, dks, dvs, sb_grads, dgate, tables, name=f"{tag}_drope")
    g_in = matmul(dproj, h, "tn", tm=1280, tn=1024, tk=2048, out_dtype=BF16, name=f"{tag}_gin")
    return {"w_in": g_in, "w_proj_dil": g_pd, "w_proj_sb": g_ps, "w_out": g_out}, dproj


def mixer_bwd_input(dx, dproj, saved, gain, w, tag):
    x = saved[0]
    return matmul_rms_bwd([(dproj, w["w_in"])], x, gain, dx, tm=256, name=f"{tag}_dh")


def kernel(x, norm_ffn1, ffn1_w_gate, ffn1_w_up, ffn1_w_down, norm_mix, w_in, w_proj_dil, w_proj_sb, w_out, norm_ffn2, ffn2_w_gate, ffn2_w_up, ffn2_w_down, norm_final, loss_target, m_norm_ffn1, m_ffn1_w_gate, m_ffn1_w_up, m_ffn1_w_down, m_norm_mix, m_w_in, m_w_proj_dil, m_w_proj_sb, m_w_out, m_norm_ffn2, m_ffn2_w_gate, m_ffn2_w_up, m_ffn2_w_down, m_norm_final, v_norm_ffn1, v_ffn1_w_gate, v_ffn1_w_up, v_ffn1_w_down, v_norm_mix, v_w_in, v_w_proj_dil, v_w_proj_sb, v_w_out, v_norm_ffn2, v_ffn2_w_gate, v_ffn2_w_up, v_ffn2_w_down, v_norm_final):
    args = dict(locals())
    t = x.shape[1]
    xs = x.reshape(t, D_MODEL)
    target = loss_target.reshape(t, D_MODEL)
    tables = rope_tables(t)

    parts = [(l, p) for l in range(2) for p in SUBBLOCKS]
    gains = {n: args[n] for n in NORM_ROWS}

    shipments = []
    for l, p in parts:
        if (l, p) == parts[0]:
            shipments += [(l, p, SUBBLOCKS[p][:2], CHIP_PEERS), (l, p, SUBBLOCKS[p][2:], ALL_PEERS)]
        elif (l, p) == parts[1]:
            shipments += [(l, p, SUBBLOCKS[p][:1], ALL_PEERS), (l, p, SUBBLOCKS[p][1:], ALL_PEERS)]
        else:
            shipments.append((l, p, SUBBLOCKS[p], ALL_PEERS))
    in_flight, order_token = [], jnp.zeros((1, 1), F32)
    for l, p, tensors, masks in shipments:
        shards = [travelling(args[n][l], by_cols).astype(BF16) for n, by_cols in tensors]
        shards[0] = shards[0] + order_token.astype(BF16)
        in_flight.append(spread_start(shards, per_peer=False, masks=masks, name=f"gather_start_l{l}_{tensors[0][0]}"))
        order_token = in_flight[-1][-1][0:1, 0:1]

    landed = {}

    def arrived(i, after):
        if i not in landed:
            landed[i] = wait_for(i, after)
        return landed[i]

    def wait_for(i, after):
        l, p, tensors, masks = shipments[i]
        tag = f"l{l}_{tensors[0][0]}"
        lands = spread_wait(in_flight[i], after, per_peer=False, masks=masks, name=f"gather_wait_{tag}")
        if masks is CHIP_PEERS:
            lands = relay_wait(relay_start(lands, name=f"gather_relay_{tag}"), name=f"gather_relayed_{tag}")
        return {n: land.reshape(-1, land.shape[-1]) for (n, _), land in zip(tensors, lands)}

    def weights_of(l, p, after):
        mine = [i for i, s in enumerate(shipments) if s[0:2] == (l, p)]
        w = arrived(mine[0], after)
        for i in mine[1:]:
            for n, _ in shipments[i][2]:
                w[n] = functools.partial(lambda after, i, n: arrived(i, after)[n], i=i, n=n)
        return w

    saved, weights = {}, {}
    act = xs
    h = rms_fwd(xs, _row(gains["norm_ffn1"][0]) + order_token, name="l0_ffn1_norm")
    for i, (l, p) in enumerate(parts):
        weights[(l, p)] = weights_of(l, p, h if i == 0 else act)
        nl, np_ = parts[i + 1] if i + 1 < len(parts) else (None, None)
        next_gain = _row(gains["norm_" + np_][nl]) if np_ else None
        if p == "mix":
            act, h, saved[(l, p)] = mixer_fwd(act, h, weights[(l, p)], tables, f"l{l}_mix", next_gain)
        else:
            act, h, saved[(l, p)] = ffn_fwd(act, h, weights[(l, p)], p, f"l{l}_{p}", next_gain)
    dx, dxb, g_final, loss_part = final_loss(act, _row(norm_final), target, name="loss_head")

    gain_grads, sent, sent_last = {}, {}, {}
    per_device = lambda g: g.reshape(N_DEV, -1, g.shape[-1])

    def ship_last(name, g):
        sent_last[name] = spread_start([per_device(g)], per_peer=True, name=f"reduce_start_{name}")
        return sent_last[name][-1]

    for l, p in reversed(parts):
        w, sv = weights[(l, p)], saved[(l, p)]
        if p == "mix":
            gw, rest = mixer_bwd_weights(dxb, sv, w, tables, f"l{l}_mix")
        elif (l, p) == parts[0]:
            gw, rest, token = ffn_bwd_weights(dxb, sv, w, p, f"l{l}_{p}", ship=ship_last)
        else:
            gw, rest, _ = ffn_bwd_weights(dxb, sv, w, p, f"l{l}_{p}")
        if (l, p) != parts[0]:
            sent[(l, p)] = spread_start([per_device(gw[n]) for n, _ in SUBBLOCKS[p]], per_peer=True, name=f"reduce_start_l{l}_{p}")
            token = sent[(l, p)][-1]
        gain = _row(gains["norm_" + p][l]) + token[0:1, 0:1]
        if p == "mix":
            dx, dxb, gain_grads[("norm_mix", l)] = mixer_bwd_input(dx, rest, sv, gain, w, f"l{l}_mix")
        else:
            dx, dxb, gain_grads[("norm_" + p, l)] = ffn_bwd_input(dx, rest, sv, gain, w, p, f"l{l}_{p}")

    partials, big_all = {}, [{}, {}, {}, {}]

    def receive(l, p, after):
        if (l, p) == parts[0]:
            for n, started in sent_last.items():
                partials.setdefault(n, [None, None])[l] = spread_wait(started, after, per_peer=True, name=f"reduce_wait_{n}")[0]
            return
        lands = spread_wait(sent[(l, p)], after, per_peer=True, name=f"reduce_wait_l{l}_{p}")
        for (n, _), land in zip(SUBBLOCKS[p], lands):
            partials.setdefault(n, [None, None])[l] = land

    def update(p):
        for n, by_cols in SUBBLOCKS[p]:
            if by_cols and args[n].shape[-1] % 128 == 0:
                outs = sum_adamw(partials[n], args[n], args["m_" + n], args["v_" + n], tr=256, transposed=True,
                                 name=f"update_{n}")
                for kind, arr in enumerate(outs):
                    big_all[kind][n] = arr
            else:
                outs = sum_adamw(partials[n], travelling(args[n], by_cols), travelling(args["m_" + n], by_cols),
                                 travelling(args["v_" + n], by_cols), tr=256, name=f"update_{n}")
                for kind, arr in enumerate(outs):
                    big_all[kind][n] = travelling(arr, by_cols)
        return outs[1]

    for l, p in reversed(parts[1:]):
        receive(l, p, dx)
    update("ffn2")
    done = update("mix")
    receive(*parts[0], done)
    done = update("ffn1")

    loss_row = jnp.pad(loss_part[:, :1], ((0, 0), (0, D_MODEL - 1)))
    small = jnp.concatenate([gain_grads[(n, l)] for n in NORM_ROWS for l in range(2)] + [g_final, loss_row], axis=0)
    small_g = sum_partials(all_gather_rows(small, done, name="gather_gain_grads"), tr=8, name="sum_gain_grads")
    zero_row = jnp.zeros((1, D_MODEL), F32)
    small_of = lambda pre: jnp.concatenate([args[pre + n] for n in NORM_ROWS] + [_row(args[pre + "norm_final"]), zero_row], axis=0)[None]
    small_out = adamw(small_g[None], small_of(""), small_of("m_"), small_of("v_"), tr=8, name="update_gains")
    small_all = [small_g] + [o[0] for o in small_out]

    def gains_of(s):
        out = {n: s[2 * i:2 * i + 2] for i, n in enumerate(NORM_ROWS)}
        out["norm_final"] = s[6]
        return out

    order = ["norm_ffn1", "ffn1_w_gate", "ffn1_w_up", "ffn1_w_down", "norm_mix", "w_in", "w_proj_dil", "w_proj_sb", "w_out",
             "norm_ffn2", "ffn2_w_gate", "ffn2_w_up", "ffn2_w_down", "norm_final"]
    results = []
    for kind in range(4):
        both = {**big_all[kind], **gains_of(small_all[kind])}
        results += [both[n] for n in order]
    loss = small_g[7, 0]
    return (loss, dx.reshape(1, t, D_MODEL), *results)
```

```python
import functools

import jax
import jax.numpy as jnp
from jax import lax
from jax.experimental import pallas as pl
from jax.experimental.pallas import tpu as pltpu

F32 = jnp.float32
BF16 = jnp.bfloat16

D_MODEL = 1024
HEAD_DIM = 64
GROUP_W = 256
D_IN = 5120
N_DIL_GROUPS = 3
DIL_SPAN = 128
DILATIONS = (1, 4, 16)
ROPE_THETA = 500000.0
ROPE_DIM = 16
RMS_EPS = 1e-6
ATT_SCALE = HEAD_DIM ** -0.5
QS_BLK, KS_BLK, VS_BLK = 9, 10, 11
GATE_DIL_BLK, GATE_SB_BLK = 3, 4

ADAM_LR, ADAM_B1, ADAM_B2, ADAM_EPS, ADAM_WD, ADAM_STEP = 0.001, 0.9, 0.999, 1e-08, 0.01, 10

N_DEV = 8
VMEM_PHYSICAL_V7X = 64 << 20
VMEM_TEMP_HEADROOM = 20 << 20

SUBBLOCKS = {
    "ffn1": (("ffn1_w_gate", True), ("ffn1_w_up", True), ("ffn1_w_down", False)),
    "mix": (("w_in", True), ("w_proj_dil", True), ("w_proj_sb", True), ("w_out", False)),
    "ffn2": (("ffn2_w_gate", True), ("ffn2_w_up", True), ("ffn2_w_down", False)),
}
NORM_ROWS = ("norm_ffn1", "norm_mix", "norm_ffn2")


def _nbytes(shape, dtype):
    n = 1
    for s in shape:
        n *= s
    return n * jnp.dtype(dtype).itemsize


def _pcall(body, *, name, grid, in_specs, out_specs, out_shape, blocks, scratch_shapes=(), scratch_bytes=0):
    need = 2 * sum(_nbytes(s, d) for s, d in blocks) + scratch_bytes + VMEM_TEMP_HEADROOM
    limit = min(need, VMEM_PHYSICAL_V7X - (4 << 20))
    in_hbm = lambda s: pltpu.HBM(s.shape, s.dtype)
    out_shape = [in_hbm(s) for s in out_shape] if isinstance(out_shape, (list, tuple)) else in_hbm(out_shape)
    call = pl.pallas_call(
        body, name=name, grid=grid, in_specs=in_specs, out_specs=out_specs, out_shape=out_shape,
        scratch_shapes=scratch_shapes,
        compiler_params=pltpu.CompilerParams(vmem_limit_bytes=limit),
    )
    return lambda *args: call(*[pltpu.with_memory_space_constraint(a, pltpu.HBM) for a in args])


def _dot(a, b, form):
    dn = {"nn": (((1,), (0,)), ((), ())), "nt": (((1,), (1,)), ((), ())), "tn": (((0,), (0,)), ((), ()))}[form]
    return lax.dot_general(a.astype(BF16), b.astype(BF16), dn, preferred_element_type=F32)


def _sigmoid(x):
    return 1.0 / (1.0 + jnp.exp(-x))


def matmul(a, b, form, *, tm, tn, tk, out_dtype, name, scale=1.0, after=None):
    if form == "tn":
        kdim, m = a.shape
        n = b.shape[1]
    else:
        m, kdim = a.shape
        n = b.shape[1] if form == "nn" else b.shape[0]
    tm, tn, tk = min(tm, m), min(tn, n), min(tk, kdim)
    assert m % tm == 0 and n % tn == 0 and kdim % tk == 0, (name, m, n, kdim, tm, tn, tk)
    nk = kdim // tk

    if form == "tn":
        a_blk, a_map = (tk, tm), (lambda j, i, k: (k, i))
    else:
        a_blk, a_map = (tm, tk), (lambda j, i, k: (i, k))
    if form == "nt":
        b_blk, b_map = (tn, tk), (lambda j, i, k: (j, k))
    else:
        b_blk, b_map = (tk, tn), (lambda j, i, k: (k, j))
    o_map = lambda j, i, k: (i, j)

    def body(a_ref, b_ref, *rest):
        o_ref, acc = (rest[1], rest[2:]) if after is not None else (rest[0], rest[1:])

        def finish(total):
            o_ref[...] = (total * scale if scale != 1.0 else total).astype(out_dtype)

        if nk == 1:
            finish(_dot(a_ref[...], b_ref[...], form))
        else:
            acc_ref, = acc
            k = pl.program_id(2)

            @pl.when(k == 0)
            def _():
                acc_ref[...] = _dot(a_ref[...], b_ref[...], form)

            @pl.when(k > 0)
            def _():
                acc_ref[...] += _dot(a_ref[...], b_ref[...], form)

            @pl.when(k == nk - 1)
            def _():
                finish(acc_ref[...])

    scratch = [pltpu.VMEM((tm, tn), F32)] if nk > 1 else []
    in_specs = [pl.BlockSpec(a_blk, a_map), pl.BlockSpec(b_blk, b_map)]
    args = [a, b]
    if after is not None:
        in_specs.append(pl.BlockSpec(memory_space=pl.ANY))
        args.append(after)
    return _pcall(
        body, name=name, grid=(n // tn, m // tm, nk), in_specs=in_specs,
        out_specs=pl.BlockSpec((tm, tn), o_map), out_shape=jax.ShapeDtypeStruct((m, n), out_dtype),
        blocks=[(a_blk, a.dtype), (b_blk, b.dtype), ((tm, tn), out_dtype)],
        scratch_shapes=scratch, scratch_bytes=(tm * tn * 4 if nk > 1 else 0),
    )(*args)


def swiglu_fwd(h, wg_t, wu_t, *, name, tm=512, tn=1408):
    t, d = h.shape
    f = wg_t.shape[0]
    tm, tn = min(tm, t), min(tn, f)

    def body(h_ref, wg_ref, wu_ref, ga_ref, gb_ref, s_ref):
        hh = h_ref[...]
        a = _dot(hh, wg_ref[...], "nt")
        b = _dot(hh, wu_ref[...], "nt")
        sg = _sigmoid(a)
        silu = a * sg
        ga_ref[...] = (b * (sg * (1.0 + a * (1.0 - sg)))).astype(BF16)
        gb_ref[...] = silu.astype(BF16)
        s_ref[...] = (silu * b).astype(BF16)

    w_spec = pl.BlockSpec((tn, d), lambda j, i: (j, 0))
    o_spec = pl.BlockSpec((tm, tn), lambda j, i: (i, j))
    o_shape = jax.ShapeDtypeStruct((t, f), BF16)
    return _pcall(
        body, name=name, grid=(f // tn, t // tm),
        in_specs=[pl.BlockSpec((tm, d), lambda j, i: (i, 0)), w_spec, w_spec],
        out_specs=[o_spec, o_spec, o_spec], out_shape=[o_shape, o_shape, o_shape],
        blocks=[((tm, d), BF16), ((tn, d), BF16), ((tn, d), BF16)] + [((tm, tn), BF16)] * 3,
    )(h, wg_t, wu_t)


def swiglu_bwd(dyb, wd, ga, gb, *, name, scale, tm=512, tn=1408):
    t, d = dyb.shape
    f = wd.shape[0]
    tm, tn = min(tm, t), min(tn, f)

    def body(dy_ref, wd_ref, ga_ref, gb_ref, da_ref, db_ref):
        ds = _dot(dy_ref[...], wd_ref[...], "nt") * scale
        da_ref[...] = (ds * ga_ref[...].astype(F32)).astype(BF16)
        db_ref[...] = (ds * gb_ref[...].astype(F32)).astype(BF16)

    o_spec = pl.BlockSpec((tm, tn), lambda j, i: (i, j))
    o_shape = jax.ShapeDtypeStruct((t, f), BF16)
    return _pcall(
        body, name=name, grid=(f // tn, t // tm),
        in_specs=[pl.BlockSpec((tm, d), lambda j, i: (i, 0)), pl.BlockSpec((tn, d), lambda j, i: (j, 0)), o_spec, o_spec],
        out_specs=[o_spec, o_spec], out_shape=[o_shape, o_shape],
        blocks=[((tm, d), BF16), ((tn, d), BF16)] + [((tm, tn), BF16)] * 4,
    )(dyb, wd, ga, gb)


def gate_fwd(odil, osb, wpd_t, wps_t, proj, *, name, tm=512):
    t = odil.shape[0]
    tm = min(tm, t)

    def body(od_ref, os_ref, wpd_ref, wps_ref, g1_ref, g2_ref, y_ref, u1_ref, u2_ref):
        u1 = _dot(od_ref[...], wpd_ref[...], "nt")
        u2 = _dot(os_ref[...], wps_ref[...], "nt")
        y = _sigmoid(g1_ref[...].astype(F32)) * u1 + _sigmoid(g2_ref[...].astype(F32)) * u2
        y_ref[...] = y.astype(BF16)
        u1_ref[...] = u1.astype(BF16)
        u2_ref[...] = u2.astype(BF16)

    o_spec = pl.BlockSpec((tm, D_MODEL), lambda i: (i, 0))
    w_spec = pl.BlockSpec((D_MODEL, GROUP_W), lambda i: (0, 0))
    a_spec = pl.BlockSpec((tm, GROUP_W), lambda i: (i, 0))
    o_shape = jax.ShapeDtypeStruct((t, D_MODEL), BF16)
    return _pcall(
        body, name=name, grid=(t // tm,),
        in_specs=[a_spec, a_spec, w_spec, w_spec,
                  pl.BlockSpec((tm, D_MODEL), lambda i: (i, GATE_DIL_BLK)),
                  pl.BlockSpec((tm, D_MODEL), lambda i: (i, GATE_SB_BLK))],
        out_specs=[o_spec, o_spec, o_spec], out_shape=[o_shape, o_shape, o_shape],
        blocks=[((tm, GROUP_W), F32)] * 2 + [((D_MODEL, GROUP_W), BF16)] * 2 + [((tm, D_MODEL), BF16)] * 5,
    )(odil, osb, wpd_t, wps_t, proj, proj)


def gate_bwd(dxb, wout, u1, u2, proj, *, name, tm=512):
    t = dxb.shape[0]
    tm = min(tm, t)

    def body(dx_ref, w_ref, u1_ref, u2_ref, g1_ref, g2_ref, du1_ref, du2_ref, dg_ref):
        dy = _dot(dx_ref[...], w_ref[...], "nt")
        s1 = _sigmoid(g1_ref[...].astype(F32))
        s2 = _sigmoid(g2_ref[...].astype(F32))
        du1_ref[...] = (dy * s1).astype(BF16)
        du2_ref[...] = (dy * s2).astype(BF16)
        dg_ref[:, :D_MODEL] = (dy * u1_ref[...].astype(F32) * s1 * (1.0 - s1)).astype(BF16)
        dg_ref[:, D_MODEL:] = (dy * u2_ref[...].astype(F32) * s2 * (1.0 - s2)).astype(BF16)

    o_spec = pl.BlockSpec((tm, D_MODEL), lambda i: (i, 0))
    o_shape = jax.ShapeDtypeStruct((t, D_MODEL), BF16)
    return _pcall(
        body, name=name, grid=(t // tm,),
        in_specs=[o_spec, pl.BlockSpec((D_MODEL, D_MODEL), lambda i: (0, 0)), o_spec, o_spec,
                  pl.BlockSpec((tm, D_MODEL), lambda i: (i, GATE_DIL_BLK)),
                  pl.BlockSpec((tm, D_MODEL), lambda i: (i, GATE_SB_BLK))],
        out_specs=[o_spec, o_spec, pl.BlockSpec((tm, 2 * D_MODEL), lambda i: (i, 0))],
        out_shape=[o_shape, o_shape, jax.ShapeDtypeStruct((t, 2 * D_MODEL), BF16)],
        blocks=[((tm, D_MODEL), BF16)] * 9 + [((D_MODEL, D_MODEL), BF16)],
    )(dxb, wout, u1, u2, proj, proj)


def rms_fwd(x, gain, *, name, tm=512):
    t, d = x.shape
    tm = min(tm, t)

    def body(x_ref, g_ref, h_ref):
        xv = x_ref[...]
        rstd = lax.rsqrt(jnp.mean(xv * xv, axis=1, keepdims=True) + RMS_EPS)
        h_ref[...] = (xv * rstd * g_ref[...]).astype(BF16)

    return _pcall(
        body, name=name, grid=(t // tm,),
        in_specs=[pl.BlockSpec((tm, d), lambda i: (i, 0)), pl.BlockSpec((1, d), lambda i: (0, 0))],
        out_specs=pl.BlockSpec((tm, d), lambda i: (i, 0)), out_shape=jax.ShapeDtypeStruct((t, d), BF16),
        blocks=[((tm, d), F32), ((tm, d), BF16)],
    )(x, gain)


def matmul_res_norm(a, b, res, next_gain, *, scale, tm, name):
    t, k = a.shape
    d = b.shape[1]
    tm = min(tm, t)
    with_norm = next_gain is not None

    def body(a_ref, b_ref, r_ref, *rest):
        out = r_ref[...] + _dot(a_ref[...], b_ref[...], "nn") * scale
        if with_norm:
            g_ref, o_ref, h_ref = rest
            rstd = lax.rsqrt(jnp.mean(out * out, axis=1, keepdims=True) + RMS_EPS)
            h_ref[...] = (out * rstd * g_ref[...]).astype(BF16)
        else:
            o_ref, = rest
        o_ref[...] = out

    row = pl.BlockSpec((tm, d), lambda i: (i, 0))
    in_specs = [pl.BlockSpec((tm, k), lambda i: (i, 0)), pl.BlockSpec((k, d), lambda i: (0, 0)), row]
    args = [a, b, res]
    out_specs, out_shape = [row], [jax.ShapeDtypeStruct((t, d), F32)]
    if with_norm:
        in_specs.append(pl.BlockSpec((1, d), lambda i: (0, 0)))
        args.append(next_gain)
        out_specs.append(row)
        out_shape.append(jax.ShapeDtypeStruct((t, d), BF16))
    outs = _pcall(
        body, name=name, grid=(t // tm,), in_specs=in_specs, out_specs=out_specs, out_shape=out_shape,
        blocks=[((tm, k), a.dtype), ((k, d), b.dtype), ((tm, d), F32), ((tm, d), F32), ((tm, d), BF16)],
    )(*args)
    return (outs[0], outs[1]) if with_norm else (outs[0], None)


def _rms_bwd_rows(dhv, xv, g, drv):
    rstd = lax.rsqrt(jnp.mean(xv * xv, axis=1, keepdims=True) + RMS_EPS)
    xh = xv * rstd
    dxh = dhv * g
    dx = drv + rstd * (dxh - xh * jnp.mean(dxh * xh, axis=1, keepdims=True))
    return dx, jnp.sum(dhv * xh, axis=0, keepdims=True)


def matmul_rms_bwd(pairs, x, gain, dres, *, tm, name):
    t, d = x.shape
    tm = min(tm, t)
    npairs = len(pairs)

    def body(*refs):
        ab = refs[:2 * npairs]
        x_ref, g_ref, dr_ref, dx_ref, dxb_ref, dg_ref = refs[2 * npairs:]
        dh = _dot(ab[0][...], ab[1][...], "nn")
        for q in range(1, npairs):
            dh = dh + _dot(ab[2 * q][...], ab[2 * q + 1][...], "nn")
        dx, part = _rms_bwd_rows(dh, x_ref[...], g_ref[...], dr_ref[...])
        dx_ref[...] = dx
        dxb_ref[...] = dx.astype(BF16)

        @pl.when(pl.program_id(0) == 0)
        def _():
            dg_ref[...] = part

        @pl.when(pl.program_id(0) > 0)
        def _():
            dg_ref[...] += part

    in_specs, args, blocks = [], [], []
    for a, b in pairs:
        k = a.shape[1]
        in_specs += [pl.BlockSpec((tm, k), lambda i: (i, 0)), pl.BlockSpec((k, d), lambda i: (0, 0))]
        args += [a, b]
        blocks += [((tm, k), a.dtype), ((k, d), b.dtype)]
    row = pl.BlockSpec((tm, d), lambda i: (i, 0))
    vec = pl.BlockSpec((1, d), lambda i: (0, 0))
    return _pcall(
        body, name=name, grid=(t // tm,), in_specs=in_specs + [row, vec, row], out_specs=[row, row, vec],
        out_shape=[jax.ShapeDtypeStruct((t, d), F32), jax.ShapeDtypeStruct((t, d), BF16), jax.ShapeDtypeStruct((1, d), F32)],
        blocks=blocks + [((tm, d), F32)] * 3 + [((tm, d), BF16)],
    )(*args, x, gain, dres)


def final_loss(x, gain, target, *, name, tm=512):
    t, d = x.shape
    tm = min(tm, t)

    def body(x_ref, g_ref, t_ref, dx_ref, dxb_ref, dg_ref, loss_ref):
        xv = x_ref[...]
        g = g_ref[...]
        rstd = lax.rsqrt(jnp.mean(xv * xv, axis=1, keepdims=True) + RMS_EPS)
        xh = xv * rstd
        err = xh * g - t_ref[...]
        dy = err * (1.0 / d)
        dxh = dy * g
        dx = rstd * (dxh - xh * jnp.mean(dxh * xh, axis=1, keepdims=True))
        dx_ref[...] = dx
        dxb_ref[...] = dx.astype(BF16)
        part = jnp.sum(dy * xh, axis=0, keepdims=True)
        sq = jnp.sum(jnp.sum(err * err, axis=1, keepdims=True), axis=0, keepdims=True) * (0.5 / d)
        lpart = jnp.broadcast_to(sq, (1, 128))

        @pl.when(pl.program_id(0) == 0)
        def _():
            dg_ref[...] = part
            loss_ref[...] = lpart

        @pl.when(pl.program_id(0) > 0)
        def _():
            dg_ref[...] += part
            loss_ref[...] += lpart

    row = pl.BlockSpec((tm, d), lambda i: (i, 0))
    vec = pl.BlockSpec((1, d), lambda i: (0, 0))
    return _pcall(
        body, name=name, grid=(t // tm,), in_specs=[row, vec, row],
        out_specs=[row, row, vec, pl.BlockSpec((1, 128), lambda i: (0, 0))],
        out_shape=[jax.ShapeDtypeStruct((t, d), F32), jax.ShapeDtypeStruct((t, d), BF16),
                   jax.ShapeDtypeStruct((1, d), F32), jax.ShapeDtypeStruct((1, 128), F32)],
        blocks=[((tm, d), F32)] * 3 + [((tm, d), BF16)],
    )(x, gain, target)


def rope_tables(t):
    pos = jnp.arange(t, dtype=F32)
    inv_freq = ROPE_THETA ** (-jnp.arange(0, ROPE_DIM, 2, dtype=F32) / ROPE_DIM)
    ang = pos[:, None] * inv_freq[None, :]
    cos, sin = jnp.cos(ang), jnp.sin(ang)
    half = ROPE_DIM // 2
    in_head = jnp.arange(128) % HEAD_DIM
    cosw, sinw = jnp.tile(cos, (1, 128 // half)), jnp.tile(sin, (1, 128 // half))
    c = jnp.where(in_head < ROPE_DIM, cosw, 1.0)
    sa = jnp.where(in_head < half, -sinw, 0.0)
    sb = jnp.where((in_head >= half) & (in_head < ROPE_DIM), sinw, 0.0)
    return jnp.concatenate([c, sa, sb], axis=1)


def _rotate(xv, cv, sav, sbv):
    halves = []
    for half in range(2):
        x = xv[:, 128 * half:128 * (half + 1)]
        halves.append(x * cv + pltpu.roll(x, 120, 1) * sav + pltpu.roll(x, 8, 1) * sbv)
    return jnp.concatenate(halves, axis=1)


STAGE_CHUNKS = 4


def _stage(tm):
    return dict(scratch_shapes=[pltpu.VMEM((STAGE_CHUNKS, tm, 128), F32)], scratch_bytes=STAGE_CHUNKS * tm * 128 * 4)


def _split_residues(stage_ref, val, out_ref, d, col, dtype):
    rows, width = val.shape
    if d == 1:
        out_ref[0, :, col:col + width] = val.astype(dtype)
        return
    chunks = width // 128
    for c in range(chunks):
        stage_ref[c] = val[:, 128 * c:128 * (c + 1)]
    for r in range(d):
        for c in range(chunks):
            out_ref[r, :, col + 128 * c:col + 128 * (c + 1)] = stage_ref[c, pl.ds(r, rows // d, stride=d), :].astype(dtype)


def _join_residues(stage_ref, in_ref, d, col=0, width=GROUP_W):
    if d == 1:
        return in_ref[0, :, col:col + width].astype(F32)
    rows = in_ref.shape[1] * d
    chunks = width // 128
    for r in range(d):
        for c in range(chunks):
            stage_ref[c, pl.ds(r, rows // d, stride=d), :] = in_ref[r, :, col + 128 * c:col + 128 * (c + 1)].astype(F32)
    return jnp.concatenate([stage_ref[c] for c in range(chunks)], axis=1)


def rope_split(proj, tables, *, name, tm=512):
    c = sa = sb = tables
    t = tables.shape[0]
    tm = min(tm, t)

    def body(*refs):
        pieces = refs[0:9]
        c_ref, sa_ref, sb_ref = refs[9:12]
        qk_out, v_out = refs[12:15], refs[15:18]
        stage = refs[18]
        cv, sav, sbv = c_ref[...], sa_ref[...], sb_ref[...]
        for g, d in enumerate(DILATIONS):
            for kind in range(3):
                xv = pieces[3 * kind + g][...].astype(F32)
                if kind < 2:
                    _split_residues(stage, _rotate(xv, cv, sav, sbv), qk_out[g], d, GROUP_W * kind, BF16)
                else:
                    _split_residues(stage, xv, v_out[g], d, 0, BF16)

    tabs = [pl.BlockSpec((tm, 128), functools.partial(lambda i, cb: (i, cb), cb=cb)) for cb in range(3)]
    in_specs = [pl.BlockSpec((tm, GROUP_W), functools.partial(lambda i, cb: (i, cb), cb=cb)) for cb in range(9)]
    out_specs = ([pl.BlockSpec((d, tm // d, 2 * GROUP_W), lambda i: (0, i, 0)) for d in DILATIONS]
                 + [pl.BlockSpec((d, tm // d, GROUP_W), lambda i: (0, i, 0)) for d in DILATIONS])
    out_shape = ([jax.ShapeDtypeStruct((d, t // d, 2 * GROUP_W), BF16) for d in DILATIONS]
                 + [jax.ShapeDtypeStruct((d, t // d, GROUP_W), BF16) for d in DILATIONS])
    outs = _pcall(
        body, name=name, grid=(t // tm,), in_specs=in_specs + tabs, out_specs=out_specs, out_shape=out_shape,
        blocks=[((tm, GROUP_W), BF16)] * 18 + [((tm, 128), F32)] * 3,
        **_stage(tm),
    )(*([proj] * 9), c, sa, sb)
    return outs[0:3], outs[3:6]


def rope_join(dqs, dks, dvs, sb_grads, dgate, tables, *, name, tm=512):
    c = sa = sb = tables
    t = tables.shape[0]
    tm = min(tm, t)

    def body(*refs):
        pieces, sb_refs, dgate_ref = refs[0:9], refs[9:12], refs[12]
        c_ref, sa_ref, sb_ref = refs[13:16]
        o_ref, stage = refs[16], refs[17]
        cv, sav, sbv = c_ref[...], -sa_ref[...], -sb_ref[...]
        for kind in range(3):
            for g, d in enumerate(DILATIONS):
                xv = _join_residues(stage, pieces[3 * kind + g], d)
                if kind < 2:
                    xv = _rotate(xv, cv, sav, sbv)
                col = GROUP_W * (3 * kind + g)
                o_ref[:, col:col + GROUP_W] = xv.astype(BF16)
        for j in range(3):
            o_ref[:, GROUP_W * (QS_BLK + j):GROUP_W * (QS_BLK + j + 1)] = sb_refs[j][...].astype(BF16)
        o_ref[:, D_MODEL * GATE_DIL_BLK:] = dgate_ref[...]

    tabs = [pl.BlockSpec((tm, 128), functools.partial(lambda i, cb: (i, cb), cb=cb)) for cb in range(3)]
    nat = lambda w: pl.BlockSpec((tm, w), lambda i: (i, 0))
    in_specs = [pl.BlockSpec((d, tm // d, GROUP_W), lambda i: (0, i, 0)) for _ in range(3) for d in DILATIONS]
    in_specs += [nat(GROUP_W)] * 3 + [nat(2 * D_MODEL)]
    return _pcall(
        body, name=name, grid=(t // tm,), in_specs=in_specs + tabs,
        out_specs=nat(D_IN), out_shape=jax.ShapeDtypeStruct((t, D_IN), BF16),
        blocks=[((tm, GROUP_W), F32)] * 12 + [((tm, 128), F32)] * 3 + [((tm, 2 * D_MODEL), BF16), ((tm, D_IN), BF16)],
        **_stage(tm),
    )(*dqs, *dks, *dvs, *sb_grads, dgate, c, sa, sb)


def _head_mask(h):
    lane = lax.broadcasted_iota(jnp.int32, (1, GROUP_W), 1)
    return (lane // HEAD_DIM) == h


def _band_mask_before():
    ri = lax.broadcasted_iota(jnp.int32, (4 * DIL_SPAN, DIL_SPAN), 0) % DIL_SPAN
    ci = lax.broadcasted_iota(jnp.int32, (4 * DIL_SPAN, DIL_SPAN), 1)
    return ci >= ri


def dil_fwd(qk, v, *, name):
    d, nsub, _ = qk.shape
    nblk = nsub // DIL_SPAN

    def body(q_ref, kc_ref, kp_ref, vc_ref, vp_ref, o_ref, lse_ref):
        nb = pl.program_id(1)
        kk = jnp.concatenate([kp_ref[0], kc_ref[0]], axis=0)
        vv = jnp.concatenate([vp_ref[0], vc_ref[0]], axis=0)
        s = _dot(_stack_heads(q_ref[0] * ATT_SCALE), kk, "nt")
        ri = lax.broadcasted_iota(jnp.int32, s.shape, 0) % DIL_SPAN
        ci = lax.broadcasted_iota(jnp.int32, s.shape, 1)
        valid = ((ci < DIL_SPAN) & (ci >= ri) & (nb > 0)) | ((ci >= DIL_SPAN) & (ci - DIL_SPAN <= ri))
        s = jnp.where(valid, s, -jnp.inf)
        m = jnp.max(s, axis=1, keepdims=True)
        p = jnp.exp(s - m)
        den = jnp.sum(p, axis=1, keepdims=True)
        o_ref[0] = _unstack_heads(_dot(p, vv, "nn") / den, DIL_SPAN)
        lse = m + jnp.log(den)
        for h in range(4):
            lse_ref[0, :, 128 * h:128 * (h + 1)] = jnp.broadcast_to(lse[DIL_SPAN * h:DIL_SPAN * (h + 1)], (DIL_SPAN, 128))

    blk = (1, DIL_SPAN, GROUP_W)
    sblk = (1, DIL_SPAN, 512)
    prv = lambda nb: jnp.maximum(nb - 1, 0)
    return _pcall(
        body, name=name, grid=(d, nblk),
        in_specs=[pl.BlockSpec(blk, lambda r, nb: (r, nb, 0)),
                  pl.BlockSpec(blk, lambda r, nb: (r, nb, 1)),
                  pl.BlockSpec(blk, lambda r, nb: (r, prv(nb), 1)),
                  pl.BlockSpec(blk, lambda r, nb: (r, nb, 0)),
                  pl.BlockSpec(blk, lambda r, nb: (r, prv(nb), 0))],
        out_specs=[pl.BlockSpec(blk, lambda r, nb: (r, nb, 0)), pl.BlockSpec(sblk, lambda r, nb: (r, nb, 0))],
        out_shape=[jax.ShapeDtypeStruct((d, nsub, GROUP_W), F32), jax.ShapeDtypeStruct((d, nsub, 512), F32)],
        blocks=[(blk, BF16)] * 5 + [(blk, F32), (sblk, F32)],
    )(qk, qk, qk, v, v)


def dil_merge(outs, lses, *, name, tm=512):
    t = outs[0].shape[0] * outs[0].shape[1]
    tm = min(tm, t)

    def body(o0, o1, o2, l0, l1, l2, o_ref, lse_ref, stage):
        ls = [_join_residues(stage, l, d, 0, 512) for l, d in zip((l0, l1, l2), DILATIONS)]
        m = jnp.maximum(jnp.maximum(ls[0], ls[1]), ls[2])
        tot = m + jnp.log(jnp.exp(ls[0] - m) + jnp.exp(ls[1] - m) + jnp.exp(ls[2] - m))
        lse_ref[...] = tot
        lane = lax.broadcasted_iota(jnp.int32, (1, 128), 1)
        first = lane < HEAD_DIM
        acc = jnp.zeros((tm, GROUP_W), F32)
        for og, lg, d in zip((o0, o1, o2), ls, DILATIONS):
            w = jnp.exp(lg - tot)
            wide = jnp.concatenate([jnp.where(first, w[:, 0:128], w[:, 128:256]),
                                    jnp.where(first, w[:, 256:384], w[:, 384:512])], axis=1)
            acc = acc + wide * _join_residues(stage, og, d)
        o_ref[...] = acc

    o_in = [pl.BlockSpec((d, tm // d, GROUP_W), lambda i: (0, i, 0)) for d in DILATIONS]
    l_in = [pl.BlockSpec((d, tm // d, 512), lambda i: (0, i, 0)) for d in DILATIONS]
    return _pcall(
        body, name=name, grid=(t // tm,), in_specs=o_in + l_in,
        out_specs=[pl.BlockSpec((tm, GROUP_W), lambda i: (i, 0)), pl.BlockSpec((tm, 512), lambda i: (i, 0))],
        out_shape=[jax.ShapeDtypeStruct((t, GROUP_W), F32), jax.ShapeDtypeStruct((t, 512), F32)],
        blocks=[((tm, GROUP_W), F32)] * 4 + [((tm, 512), F32)] * 4,
        **_stage(tm),
    )(*outs, *lses)


def dil_bwd_prep(do, o, lse, *, name, tm=512):
    t = do.shape[0]
    tm = min(tm, t)
    wide = DILATIONS[1:]

    def body(do_ref, o_ref, lse_ref, ds_ref, *rest):
        do_out, lse_out, ds_out = rest[0:2], rest[2:4], rest[4:6]
        stage = rest[6]
        dov = do_ref[...]
        prod = dov * o_ref[...]
        for h in range(4):
            s = jnp.sum(jnp.where(_head_mask(h), prod, 0.0), axis=1, keepdims=True)
            ds_ref[:, 128 * h:128 * (h + 1)] = jnp.broadcast_to(s, (tm, 128))
        for i, d in enumerate(wide):
            _split_residues(stage, dov, do_out[i], d, 0, BF16)
            _split_residues(stage, lse_ref[...], lse_out[i], d, 0, F32)
            _split_residues(stage, ds_ref[...], ds_out[i], d, 0, F32)

    nat = lambda w: pl.BlockSpec((tm, w), lambda i: (i, 0))
    res = lambda d, w: pl.BlockSpec((d, tm // d, w), lambda i: (0, i, 0))
    shape = lambda d, w, dt: jax.ShapeDtypeStruct((d, t // d, w), dt)
    outs = _pcall(
        body, name=name, grid=(t // tm,), in_specs=[nat(GROUP_W), nat(GROUP_W), nat(512)],
        out_specs=[nat(512)] + [res(d, GROUP_W) for d in wide] + [res(d, 512) for d in wide] * 2,
        out_shape=([jax.ShapeDtypeStruct((t, 512), F32)] + [shape(d, GROUP_W, BF16) for d in wide]
                   + [shape(d, 512, F32) for d in wide] * 2),
        blocks=[((tm, GROUP_W), F32)] * 3 + [((tm, 512), F32)] * 6,
        **_stage(tm),
    )(do, o, lse)
    return outs[0], outs[1:3], outs[3:5], outs[5:7]


def head_sums(a, b, *, name, tm=512):
    t = a.shape[0]
    tm = min(tm, t)

    def body(a_ref, b_ref, o_ref):
        prod = a_ref[...].astype(BF16).astype(F32) * b_ref[...]
        for h in range(4):
            s = jnp.sum(jnp.where(_head_mask(h), prod, 0.0), axis=1, keepdims=True)
            o_ref[:, 128 * h:128 * (h + 1)] = jnp.broadcast_to(s, (tm, 128))

    spec = pl.BlockSpec((tm, GROUP_W), lambda i: (i, 0))
    return _pcall(
        body, name=name, grid=(t // tm,), in_specs=[spec, spec],
        out_specs=pl.BlockSpec((tm, 512), lambda i: (i, 0)), out_shape=jax.ShapeDtypeStruct((t, 512), F32),
        blocks=[((tm, GROUP_W), F32)] * 2 + [((tm, 512), F32)],
    )(a, b)


def dil_bwd(qk, v, do, lse, dsum, *, name):
    d, nsub, _ = qk.shape
    nblk = nsub // DIL_SPAN

    def body(qa_ref, qb_ref, kc_ref, kp_ref, vc_ref, vp_ref, doa_ref, dob_ref, la_ref, lb_ref, sa_ref, sb_ref,
             dq_ref, dk_ref, dv_ref):
        nb = pl.program_id(1)
        nxt = _band_mask_before() & (nb < nblk - 1)
        kc, kp, vc, vp = kc_ref[0], kp_ref[0], vc_ref[0], vp_ref[0]
        qas, qbs = _stack_heads(qa_ref[0] * ATT_SCALE), _stack_heads(qb_ref[0] * ATT_SCALE)
        das, dbs = _stack_heads(doa_ref[0].astype(BF16)), _stack_heads(dob_ref[0].astype(BF16))
        stat = lambda ref: jnp.concatenate([ref[0, :, 128 * h:128 * (h + 1)] for h in range(4)], axis=0)
        la, lb, sa, sb = stat(la_ref), stat(lb_ref), stat(sa_ref), stat(sb_ref)

        def probs(qs, ds_, k, v, mask, l, s):
            p = jnp.where(mask, jnp.exp(_dot(qs, k, "nt") - l), 0.0)
            dsc = p * (_dot(ds_, v, "nt") - s)
            return p.astype(BF16), dsc.astype(BF16)

        wide = lambda a: jnp.concatenate([a, a], axis=1)
        ri = lax.broadcasted_iota(jnp.int32, (4 * DIL_SPAN, 2 * DIL_SPAN), 0) % DIL_SPAN
        ci = lax.broadcasted_iota(jnp.int32, (4 * DIL_SPAN, 2 * DIL_SPAN), 1)
        valid = ((ci < DIL_SPAN) & (ci >= ri) & (nb > 0)) | ((ci >= DIL_SPAN) & (ci - DIL_SPAN <= ri))
        p_a, ds_a = probs(qas, das, jnp.concatenate([kp, kc], axis=0), jnp.concatenate([vp, vc], axis=0),
                          valid, wide(la), wide(sa))
        p_nc, ds_nc = probs(qbs, dbs, kc, vc, nxt, lb, sb)
        dq_ref[0] = _unstack_heads(_dot(ds_a, jnp.concatenate([kp, kc], axis=0), "nn"), DIL_SPAN) * ATT_SCALE
        dk_ref[0] = _dot(ds_a[:, DIL_SPAN:], qas, "tn") + _dot(ds_nc, qbs, "tn")
        dv_ref[0] = _dot(p_a[:, DIL_SPAN:], das, "tn") + _dot(p_nc, dbs, "tn")

    blk = (1, DIL_SPAN, GROUP_W)
    sblk = (1, DIL_SPAN, 512)
    prv = lambda nb: jnp.maximum(nb - 1, 0)
    nxt_ = lambda nb: jnp.minimum(nb + 1, nblk - 1)
    cur_at = lambda c: pl.BlockSpec(blk, functools.partial(lambda r, nb, c: (r, nb, c), c=c))
    prv_at = lambda c: pl.BlockSpec(blk, functools.partial(lambda r, nb, c: (r, prv(nb), c), c=c))
    nxt_at = lambda c: pl.BlockSpec(blk, functools.partial(lambda r, nb, c: (r, nxt_(nb), c), c=c))
    s_cur = pl.BlockSpec(sblk, lambda r, nb: (r, nb, 0))
    s_nxt = pl.BlockSpec(sblk, lambda r, nb: (r, nxt_(nb), 0))
    o_spec = pl.BlockSpec(blk, lambda r, nb: (r, nb, 0))
    o_shape = jax.ShapeDtypeStruct((d, nsub, GROUP_W), F32)
    return _pcall(
        body, name=name, grid=(d, nblk),
        in_specs=[cur_at(0), nxt_at(0), cur_at(1), prv_at(1), cur_at(0), prv_at(0), cur_at(0), nxt_at(0),
                  s_cur, s_nxt, s_cur, s_nxt],
        out_specs=[o_spec, o_spec, o_spec], out_shape=[o_shape, o_shape, o_shape],
        blocks=[(blk, BF16)] * 6 + [(blk, F32)] * 5 + [(sblk, F32)] * 4,
    )(qk, qk, qk, qk, v, v, do, do, lse, lse, dsum, dsum)


def _tri_dot(x, b):
    hi = x.astype(BF16)
    lo = (x - hi.astype(F32)).astype(BF16)
    return _dot(jnp.concatenate([hi, lo], axis=1), jnp.concatenate([b, b], axis=0), "nn")


SB_TILE = 256
SB_ROWS = 512


def _stack_heads(a):
    return jnp.concatenate([jnp.where(_head_mask(h), a, jnp.zeros_like(a)) for h in range(4)], axis=0)


def _unstack_heads(acc, rows):
    out = acc[0:rows]
    for h in range(1, 4):
        out = jnp.where(_head_mask(h), acc[h * rows:(h + 1) * rows], out)
    return out


def _tri_masks(n):
    ri = lax.broadcasted_iota(jnp.int32, (n, n), 0)
    ci = lax.broadcasted_iota(jnp.int32, (n, n), 1)
    return (ri > ci).astype(BF16), (ri >= ci).astype(BF16)


def _sb_weights(qs, kt, after, c_keep, lead):
    z = _dot(qs, kt, "nt")
    lbeta = jnp.minimum(z, 0.0) - jnp.log(1.0 + jnp.exp(-jnp.abs(z)))
    lkeep = lbeta - z
    past = None
    if lead is not None:
        query = lax.broadcasted_iota(jnp.int32, z.shape, 0) % SB_ROWS
        past = lax.broadcasted_iota(jnp.int32, z.shape, 1) + lead < query
        lkeep = jnp.where(past, lkeep, 0.0)
    w = jnp.exp(lbeta + _tri_dot(lkeep, after) + c_keep)
    if lead is not None:
        w = jnp.where(past, w, 0.0)
    return z, past, lbeta, lkeep, w


def _sb_walk(qb, tile, carry):
    per = SB_ROWS // SB_TILE
    for i in reversed(range(per)):
        carry = tile(pl.multiple_of(qb * SB_ROWS + i * SB_TILE, SB_TILE), i * SB_TILE, i == per - 1, carry)
    past_tiles = qb * per
    return lax.fori_loop(0, past_tiles,
                         lambda it, c: tile(pl.multiple_of((past_tiles - 1 - it) * SB_TILE, SB_TILE), None, False, c), carry)


def sb_fwd(proj, *, name):
    t = proj.shape[0]
    n, m = SB_TILE, SB_ROWS
    assert t % m == 0

    def body(q_ref, k_ref, v_ref, o_ref, acc_ref):
        qb = pl.program_id(0)
        qs = _stack_heads(q_ref[...] * ATT_SCALE)
        after, _ = _tri_masks(n)

        def tile(off, lead, first, c_keep):
            kt = k_ref[pl.ds(off, n), :]
            vt = v_ref[pl.ds(off, n), :]
            _, _, _, lkeep, w = _sb_weights(qs, kt, after, c_keep, lead)
            pv = _tri_dot(w, vt)
            if first:
                acc_ref[...] = pv
            else:
                acc_ref[...] += pv
            return c_keep + jnp.sum(lkeep, axis=1, keepdims=True)

        _sb_walk(qb, tile, jnp.zeros((4 * m, 1), F32))
        o_ref[...] = _unstack_heads(acc_ref[...], m)

    full = lambda cb: pl.BlockSpec((t, GROUP_W), functools.partial(lambda i, cb: (0, cb), cb=cb))
    return _pcall(
        body, name=name, grid=(t // m,),
        in_specs=[pl.BlockSpec((m, GROUP_W), lambda i: (i, QS_BLK)), full(KS_BLK), full(VS_BLK)],
        out_specs=pl.BlockSpec((m, GROUP_W), lambda i: (i, 0)), out_shape=jax.ShapeDtypeStruct((t, GROUP_W), F32),
        blocks=[((m, GROUP_W), BF16), ((t, GROUP_W), BF16), ((t, GROUP_W), BF16), ((m, GROUP_W), F32)],
        scratch_shapes=[pltpu.VMEM((4 * m, GROUP_W), F32)], scratch_bytes=4 * m * GROUP_W * 4,
    )(proj, proj, proj)


def sb_bwd(proj, do, gtot, *, name):
    t = proj.shape[0]
    n, m = SB_TILE, SB_ROWS
    assert t % m == 0

    def body(q_ref, k_ref, v_ref, do_ref, gt_ref, dq_ref, dk_ref, dv_ref, acc_ref):
        qb = pl.program_id(0)

        @pl.when(qb == 0)
        def _():
            dk_ref[...] = jnp.zeros_like(dk_ref)
            dv_ref[...] = jnp.zeros_like(dv_ref)

        qs = _stack_heads(q_ref[...] * ATT_SCALE)
        dos = _stack_heads(do_ref[...].astype(BF16))
        gt = jnp.concatenate([jnp.max(gt_ref[:, 128 * h:128 * (h + 1)], axis=1, keepdims=True) for h in range(4)], axis=0)
        after, from_on = _tri_masks(n)

        def tile(off, lead, first, carry):
            c_keep, c_g = carry
            kt = k_ref[pl.ds(off, n), :]
            vt = v_ref[pl.ds(off, n), :]
            z, past, lbeta, lkeep, w = _sb_weights(qs, kt, after, c_keep, lead)
            gw = w * _dot(dos, vt, "nt")
            big_g = gt - (_tri_dot(gw, from_on) + c_g)
            dz = gw * jnp.exp(lbeta - z) - big_g * jnp.exp(lbeta)
            if lead is not None:
                dz = jnp.where(past, dz, 0.0)
            dz = dz.astype(BF16)
            dk_ref[pl.ds(off, n), :] += _dot(dz, qs, "tn")
            dv_ref[pl.ds(off, n), :] += _dot(w, dos, "tn")
            dq = _dot(dz, kt, "nn")
            if first:
                acc_ref[...] = dq
            else:
                acc_ref[...] += dq
            return c_keep + jnp.sum(lkeep, axis=1, keepdims=True), c_g + jnp.sum(gw, axis=1, keepdims=True)

        zero_col = jnp.zeros((4 * m, 1), F32)
        _sb_walk(qb, tile, (zero_col, zero_col))
        dq_ref[...] = _unstack_heads(acc_ref[...], m) * ATT_SCALE

    full = lambda cb: pl.BlockSpec((t, GROUP_W), functools.partial(lambda i, cb: (0, cb), cb=cb))
    whole = pl.BlockSpec((t, GROUP_W), lambda i: (0, 0))
    rowblk = pl.BlockSpec((m, GROUP_W), lambda i: (i, 0))
    shape = jax.ShapeDtypeStruct((t, GROUP_W), F32)
    return _pcall(
        body, name=name, grid=(t // m,),
        in_specs=[pl.BlockSpec((m, GROUP_W), lambda i: (i, QS_BLK)), full(KS_BLK), full(VS_BLK), rowblk,
                  pl.BlockSpec((m, 512), lambda i: (i, 0))],
        out_specs=[rowblk, whole, whole], out_shape=[shape, shape, shape],
        blocks=[((m, GROUP_W), BF16), ((t, GROUP_W), BF16), ((t, GROUP_W), BF16), ((m, GROUP_W), F32),
                ((m, 512), F32), ((m, GROUP_W), F32), ((t, GROUP_W), F32), ((t, GROUP_W), F32)],
        scratch_shapes=[pltpu.VMEM((4 * m, GROUP_W), F32)], scratch_bytes=4 * m * GROUP_W * 4,
    )(proj, proj, proj, do, gtot)


def _mesh_place():
    return lax.axis_index("x"), lax.axis_index("y"), lax.axis_index("c")


def _flip(place, mask):
    x, y, c = place
    return ((1 - x) if mask & 4 else x, (1 - y) if mask & 2 else y, (1 - c) if mask & 1 else c)


def _dev_index(place):
    x, y, c = place
    return 4 * x + 2 * y + c


HBM_SPEC = pl.BlockSpec(memory_space=pltpu.HBM)


def all_gather_rows(shard, after, *, name):
    rows, lanes = shard.shape

    def body(x_ref, after_ref, out_ref, send_sems, recv_sems, local_sem):
        me = _mesh_place()
        x, y, c = me
        sibling = _flip(me, 1)
        chips = [_flip(me, 4), _flip(me, 2), _flip(me, 6)]

        def copy(k, block, to, src=None):
            dst = out_ref.at[_dev_index(block)]
            return pltpu.make_async_remote_copy(
                src_ref=dst if src is None else src, dst_ref=dst, send_sem=send_sems.at[k], recv_sem=recv_sems.at[k],
                device_id=to, device_id_type=pl.DeviceIdType.MESH)

        mine = pltpu.make_async_copy(x_ref, out_ref.at[_dev_index(me)], local_sem)
        mine.start()
        first = [copy(0, me, sibling, src=x_ref)] + [copy(1 + j, me, chip, src=x_ref) for j, chip in enumerate(chips)]
        for cp in first:
            cp.start()
        passed = [copy(4 + j, chip, sibling) for j, chip in enumerate(chips)]
        for j, chip in enumerate(chips):
            copy(1 + j, chip, me).wait_recv()
            passed[j].start()
        copy(0, sibling, me).wait_recv()
        for j, chip in enumerate(chips):
            copy(4 + j, _flip(chip, 1), me).wait_recv()
        for cp in first + passed:
            cp.wait_send()
        mine.wait()

    return pl.pallas_call(
        body, name=name, in_specs=[HBM_SPEC, pl.BlockSpec(memory_space=pl.ANY)], out_specs=HBM_SPEC,
        out_shape=jax.ShapeDtypeStruct((N_DEV, rows, lanes), shard.dtype),
        scratch_shapes=[pltpu.SemaphoreType.DMA((7,)), pltpu.SemaphoreType.DMA((7,)), pltpu.SemaphoreType.DMA],
    )(shard, after)


SEM_SPEC = pl.BlockSpec(memory_space=pltpu.SEMAPHORE)
DATAFLOW_EFFECT = pltpu.SideEffectType.DATAFLOW_SIDE_EFFECTING


ALL_PEERS = tuple(range(1, N_DEV))
CHIP_PEERS = (1, 4, 2, 6)
OTHER_CHIPS = (4, 2, 6)


def _spread_copies(src_refs, land_refs, send_sems, recv_sems, per_peer, masks, arriving):
    me = _mesh_place()
    my = _dev_index(me)
    remote, local = [], []
    for t, (src_ref, land_ref) in enumerate(zip(src_refs, land_refs)):
        for i, mask in enumerate(masks):
            peer = _flip(me, mask)
            data_of = my if arriving else _dev_index(peer)
            slot = _dev_index(peer) if arriving else my
            k = t * len(masks) + i
            remote.append(pltpu.make_async_remote_copy(
                src_ref=src_ref.at[data_of] if per_peer else src_ref, dst_ref=land_ref.at[slot],
                send_sem=send_sems.at[k], recv_sem=recv_sems.at[k],
                device_id=peer, device_id_type=pl.DeviceIdType.MESH))
        local.append(pltpu.make_async_copy(src_ref.at[my] if per_peer else src_ref, land_ref.at[my],
                                           send_sems.at[len(src_refs) * len(masks) + t]))
    return remote, local


def spread_start(srcs, *, per_peer, name, masks=ALL_PEERS):
    nt = len(srcs)
    zones = [pltpu.HBM((N_DEV,) + (s.shape[1:] if per_peer else s.shape), s.dtype) for s in srcs]

    def body(*refs):
        src_refs, (send_sems, recv_sems) = refs[:nt], refs[nt:nt + 2]
        land_refs, token = refs[2 * nt + 2:3 * nt + 2], refs[3 * nt + 2]
        remote, local = _spread_copies(src_refs, land_refs, send_sems, recv_sems, per_peer, masks, arriving=False)
        for cp in remote + local:
            cp.start()
        token[...] = jnp.zeros_like(token)

    return pl.pallas_call(
        body, name=name, in_specs=(HBM_SPEC,) * nt,
        out_shape=(pltpu.SemaphoreType.DMA((nt * len(masks) + nt,)), pltpu.SemaphoreType.DMA((nt * len(masks),)),
                   *[pltpu.HBM(s.shape, s.dtype) for s in srcs], *zones, jax.ShapeDtypeStruct((8, 128), F32)),
        out_specs=(SEM_SPEC, SEM_SPEC) + (HBM_SPEC,) * (2 * nt) + (pl.BlockSpec(memory_space=pltpu.VMEM),),
        input_output_aliases={t: 2 + t for t in range(nt)},
        compiler_params=pltpu.CompilerParams(has_side_effects=DATAFLOW_EFFECT),
    )(*[pltpu.with_memory_space_constraint(s, pltpu.HBM) for s in srcs])


def spread_wait(started, after, *, per_peer, name, masks=ALL_PEERS):
    nt = (len(started) - 3) // 2
    send_sems, recv_sems = started[0:2]
    srcs_thru, lands_thru = started[2:2 + nt], started[2 + nt:2 + 2 * nt]

    def body(*refs):
        src_refs, land_refs = refs[:nt], refs[nt:2 * nt]
        send_sems, recv_sems = refs[2 * nt:2 * nt + 2]
        remote, local = _spread_copies(src_refs, land_refs, send_sems, recv_sems, per_peer, masks, arriving=True)
        for cp in remote:
            cp.wait_send()
            cp.wait_recv()
        for cp in local:
            cp.wait()

    outs = pl.pallas_call(
        body, name=name, in_specs=(HBM_SPEC,) * (2 * nt) + (SEM_SPEC, SEM_SPEC, pl.BlockSpec(memory_space=pl.ANY)),
        out_shape=tuple(pltpu.HBM(a.shape, a.dtype) for a in (*srcs_thru, *lands_thru)),
        out_specs=(HBM_SPEC,) * (2 * nt), input_output_aliases={t: t for t in range(2 * nt)},
        compiler_params=pltpu.CompilerParams(has_side_effects=DATAFLOW_EFFECT),
    )(*srcs_thru, *lands_thru, send_sems, recv_sems, after)
    return list(outs[nt:])


def _relay_copies(land_refs, send_sems, recv_sems, arriving):
    me = _mesh_place()
    sibling = _flip(me, 1)
    out = []
    for t, land_ref in enumerate(land_refs):
        for i, mask in enumerate(OTHER_CHIPS):
            slot = _dev_index(_flip(sibling if arriving else me, mask))
            k = t * len(OTHER_CHIPS) + i
            out.append(pltpu.make_async_remote_copy(
                src_ref=land_ref.at[slot], dst_ref=land_ref.at[slot], send_sem=send_sems.at[k], recv_sem=recv_sems.at[k],
                device_id=sibling, device_id_type=pl.DeviceIdType.MESH))
    return out


def relay_start(lands, *, name):
    nt = len(lands)
    n_sem = nt * len(OTHER_CHIPS)

    def body(*refs):
        for cp in _relay_copies(refs[:nt], refs[nt], refs[nt + 1], arriving=False):
            cp.start()

    return pl.pallas_call(
        body, name=name, in_specs=(HBM_SPEC,) * nt,
        out_shape=(pltpu.SemaphoreType.DMA((n_sem,)), pltpu.SemaphoreType.DMA((n_sem,)),
                   *[pltpu.HBM(a.shape, a.dtype) for a in lands]),
        out_specs=(SEM_SPEC, SEM_SPEC) + (HBM_SPEC,) * nt, input_output_aliases={t: 2 + t for t in range(nt)},
        compiler_params=pltpu.CompilerParams(has_side_effects=DATAFLOW_EFFECT),
    )(*[pltpu.with_memory_space_constraint(a, pltpu.HBM) for a in lands])


def relay_wait(started, *, name):
    send_sems, recv_sems = started[0:2]
    lands_thru = started[2:]
    nt = len(lands_thru)

    def body(*refs):
        for cp in _relay_copies(refs[:nt], refs[nt], refs[nt + 1], arriving=True):
            cp.wait_send()
            cp.wait_recv()

    return list(pl.pallas_call(
        body, name=name, in_specs=(HBM_SPEC,) * nt + (SEM_SPEC, SEM_SPEC),
        out_shape=tuple(pltpu.HBM(a.shape, a.dtype) for a in lands_thru), out_specs=(HBM_SPEC,) * nt,
        input_output_aliases={t: t for t in range(nt)},
        compiler_params=pltpu.CompilerParams(has_side_effects=DATAFLOW_EFFECT),
    )(*lands_thru, send_sems, recv_sems))


def sum_partials(parts, *, name, tr):
    _, rows, lanes = parts.shape
    assert rows % tr == 0

    def body(p_ref, g_ref):
        g = p_ref[0].astype(F32)
        for k in range(1, N_DEV):
            g = g + p_ref[k].astype(F32)
        g_ref[...] = g

    return _pcall(
        body, name=name, grid=(rows // tr,),
        in_specs=[pl.BlockSpec((N_DEV, tr, lanes), lambda i: (0, i, 0))],
        out_specs=pl.BlockSpec((tr, lanes), lambda i: (i, 0)), out_shape=jax.ShapeDtypeStruct((rows, lanes), F32),
        blocks=[((N_DEV, tr, lanes), parts.dtype), ((tr, lanes), F32)],
    )(parts)


def adamw(g, w, m, v, *, name, tr):
    nl, k, n = w.shape
    tr = max(c for c in range(8, min(tr, k) + 1, 8) if k % c == 0)
    bc1 = 1.0 - ADAM_B1 ** ADAM_STEP
    bc2 = 1.0 - ADAM_B2 ** ADAM_STEP

    def body(g_ref, w_ref, m_ref, v_ref, d_ref, mo_ref, vo_ref):
        gv = g_ref[...]
        m_new = ADAM_B1 * m_ref[...] + (1.0 - ADAM_B1) * gv
        v_new = ADAM_B2 * v_ref[...] + (1.0 - ADAM_B2) * (gv * gv)
        mo_ref[...] = m_new
        vo_ref[...] = v_new
        d_ref[...] = -ADAM_LR * ((m_new / bc1) / (jnp.sqrt(v_new / bc2) + ADAM_EPS) + ADAM_WD * w_ref[...])

    spec = pl.BlockSpec((1, tr, n), lambda l, i: (l, i, 0))
    shape = jax.ShapeDtypeStruct(w.shape, F32)
    return _pcall(
        body, name=name, grid=(nl, k // tr), in_specs=[spec] * 4, out_specs=[spec] * 3, out_shape=[shape] * 3,
        blocks=[((1, tr, n), F32)] * 7,
    )(g, w, m, v)


def sum_adamw(partials, w, m, v, *, name, tr, transposed=False):
    nl, k, n = w.shape
    assert nl == len(partials) == 2
    step = 128 if transposed else 8
    tr = max(c for c in range(step, min(tr, k) + 1, step) if k % c == 0)
    bc1 = 1.0 - ADAM_B1 ** ADAM_STEP
    bc2 = 1.0 - ADAM_B2 ** ADAM_STEP

    def body(p0_ref, p1_ref, w_ref, m_ref, v_ref, g_ref, d_ref, mo_ref, vo_ref):
        first = pl.program_id(0) == 0
        gv = jnp.where(first, p0_ref[0], p1_ref[0]).astype(F32)
        for s in range(1, N_DEV):
            gv = gv + jnp.where(first, p0_ref[s], p1_ref[s]).astype(F32)
        if transposed:
            gv = gv.T
        m_new = ADAM_B1 * m_ref[0] + (1.0 - ADAM_B1) * gv
        v_new = ADAM_B2 * v_ref[0] + (1.0 - ADAM_B2) * (gv * gv)
        g_ref[0] = gv
        mo_ref[0] = m_new
        vo_ref[0] = v_new
        d_ref[0] = -ADAM_LR * ((m_new / bc1) / (jnp.sqrt(v_new / bc2) + ADAM_EPS) + ADAM_WD * w_ref[0])

    spec = pl.BlockSpec((1, tr, n), lambda l, i: (l, i, 0))
    if transposed:
        p0spec = pl.BlockSpec((N_DEV, n, tr), lambda l, i: (0, 0, i * (1 - l)))
        p1spec = pl.BlockSpec((N_DEV, n, tr), lambda l, i: (0, 0, i * l))
    else:
        p0spec = pl.BlockSpec((N_DEV, tr, n), lambda l, i: (0, i * (1 - l), 0))
        p1spec = pl.BlockSpec((N_DEV, tr, n), lambda l, i: (0, i * l, 0))
    shape = jax.ShapeDtypeStruct(w.shape, F32)
    return _pcall(
        body, name=name, grid=(nl, k // tr), in_specs=[p0spec, p1spec, spec, spec, spec], out_specs=[spec] * 4,
        out_shape=[shape] * 4, blocks=[((N_DEV, tr, n), BF16)] * 2 + [((1, tr, n), F32)] * 7,
    )(partials[0], partials[1], w, m, v)


def travelling(a, by_cols):
    return jnp.swapaxes(a, -1, -2) if by_cols else a


def _row(v):
    return v.reshape(1, -1)


def ffn_fwd(x, h, w, pre, tag, next_gain):
    ga, gb, s = swiglu_fwd(h, w[pre + "_w_gate"], w[pre + "_w_up"], name=f"{tag}_gateup")
    if callable(w[pre + "_w_down"]):
        w[pre + "_w_down"] = w[pre + "_w_down"](s)
    out, h_next = matmul_res_norm(s, w[pre + "_w_down"], x, next_gain, scale=0.5, tm=512, name=f"{tag}_down")
    return out, h_next, (x, h, ga, gb, s)


def ffn_bwd_weights(dxb, saved, w, pre, tag, ship=None):
    x, h, a, b, s = saved
    token = None
    grads = {}

    def made(name, g):
        grads[name] = g
        return ship(name, g) if ship else None

    da, db = swiglu_bwd(dxb, w[pre + "_w_down"], a, b, scale=0.5, name=f"{tag}_dgateup")
    token = made(pre + "_w_down", matmul(s, dxb, "tn", tm=1408, tn=1024, tk=2048, out_dtype=BF16, scale=0.5, name=f"{tag}_gdown"))
    token = made(pre + "_w_gate", matmul(da, h, "tn", tm=1408, tn=1024, tk=2048, out_dtype=BF16, after=token, name=f"{tag}_ggate"))
    token = made(pre + "_w_up", matmul(db, h, "tn", tm=1408, tn=1024, tk=2048, out_dtype=BF16, after=token, name=f"{tag}_gup"))
    return grads, (da, db), token


def ffn_bwd_input(dx, rest, saved, gain, w, pre, tag):
    da, db = rest
    x = saved[0]
    return matmul_rms_bwd([(da, w[pre + "_w_gate"]), (db, w[pre + "_w_up"])], x, gain, dx, tm=256, name=f"{tag}_dh")


def mixer_fwd(x, h, w, tables, tag, next_gain):
    proj = matmul(h, w["w_in"], "nt", tm=512, tn=1280, tk=1024, out_dtype=BF16, name=f"{tag}_in")
    qks, vs = rope_split(proj, tables, name=f"{tag}_rope")
    outs, lses = [], []
    for g in range(N_DIL_GROUPS):
        o, lse = dil_fwd(qks[g], vs[g], name=f"{tag}_dil{g}")
        outs.append(o)
        lses.append(lse)
    odil, lse = dil_merge(outs, lses, name=f"{tag}_merge")
    osb = sb_fwd(proj, name=f"{tag}_sb")
    for n in ("w_proj_dil", "w_proj_sb", "w_out"):
        if callable(w[n]):
            w[n] = w[n](osb)
    y, u1, u2 = gate_fwd(odil, osb, w["w_proj_dil"], w["w_proj_sb"], proj, name=f"{tag}_gate")
    out, h_next = matmul_res_norm(y, w["w_out"], x, next_gain, scale=1.0, tm=512, name=f"{tag}_out")
    return out, h_next, (x, h, proj, qks, vs, odil, lse, osb, u1, u2, y)


def mixer_bwd_weights(dxb, saved, w, tables, tag):
    x, h, proj, qks, vs, odil, lse, osb, u1, u2, y = saved
    t = x.shape[0]
    g_out = matmul(y, dxb, "tn", tm=1024, tn=1024, tk=2048, out_dtype=BF16, name=f"{tag}_gout")
    du1, du2, dgate = gate_bwd(dxb, w["w_out"], u1, u2, proj, name=f"{tag}_dgate")
    g_pd = matmul(du1, odil, "tn", tm=1024, tn=256, tk=2048, out_dtype=BF16, name=f"{tag}_gpd")
    g_ps = matmul(du2, osb, "tn", tm=1024, tn=256, tk=2048, out_dtype=BF16, name=f"{tag}_gps")
    dodil = matmul(du1, w["w_proj_dil"], "nn", tm=512, tn=256, tk=1024, out_dtype=F32, name=f"{tag}_dodil")
    dosb = matmul(du2, w["w_proj_sb"], "nn", tm=512, tn=256, tk=1024, out_dtype=F32, name=f"{tag}_dosb")
    dsum, do_wide, lse_wide, dsum_wide = dil_bwd_prep(dodil, odil, lse, name=f"{tag}_dprep")
    dos = [dodil[None]] + list(do_wide)
    lss = [lse[None]] + list(lse_wide)
    dss = [dsum[None]] + list(dsum_wide)
    dqs, dks, dvs = [], [], []
    for g in range(N_DIL_GROUPS):
        dq, dk, dv = dil_bwd(qks[g], vs[g], dos[g], lss[g], dss[g], name=f"{tag}_ddil{g}")
        dqs.append(dq)
        dks.append(dk)
        dvs.append(dv)
    gtot = head_sums(dosb, osb, name=f"{tag}_gsum")
    sb_grads = sb_bwd(proj, dosb, gtot, name=f"{tag}_dsb")
    dproj = rope_join(dqs, dks, dvs, sb_grads, dgate, tables, name=f"{tag}_drope")
    g_in = matmul(dproj, h, "tn", tm=1280, tn=1024, tk=2048, out_dtype=BF16, name=f"{tag}_gin")
    return {"w_in": g_in, "w_proj_dil": g_pd, "w_proj_sb": g_ps, "w_out": g_out}, dproj


def mixer_bwd_input(dx, dproj, saved, gain, w, tag):
    x = saved[0]
    return matmul_rms_bwd([(dproj, w["w_in"])], x, gain, dx, tm=256, name=f"{tag}_dh")


def kernel(x, norm_ffn1, ffn1_w_gate, ffn1_w_up, ffn1_w_down, norm_mix, w_in, w_proj_dil, w_proj_sb, w_out, norm_ffn2, ffn2_w_gate, ffn2_w_up, ffn2_w_down, norm_final, loss_target, m_norm_ffn1, m_ffn1_w_gate, m_ffn1_w_up, m_ffn1_w_down, m_norm_mix, m_w_in, m_w_proj_dil, m_w_proj_sb, m_w_out, m_norm_ffn2, m_ffn2_w_gate, m_ffn2_w_up, m_ffn2_w_down, m_norm_final, v_norm_ffn1, v_ffn1_w_gate, v_ffn1_w_up, v_ffn1_w_down, v_norm_mix, v_w_in, v_w_proj_dil, v_w_proj_sb, v_w_out, v_norm_ffn2, v_ffn2_w_gate, v_ffn2_w_up, v_ffn2_w_down, v_norm_final):
    args = dict(locals())
    t = x.shape[1]
    xs = x.reshape(t, D_MODEL)
    target = loss_target.reshape(t, D_MODEL)
    tables = rope_tables(t)

    parts = [(l, p) for l in range(2) for p in SUBBLOCKS]
    gains = {n: args[n] for n in NORM_ROWS}

    shipments = []
    for l, p in parts:
        if (l, p) == parts[0]:
            shipments += [(l, p, SUBBLOCKS[p][:2], CHIP_PEERS), (l, p, SUBBLOCKS[p][2:], ALL_PEERS)]
        elif (l, p) == parts[1]:
            shipments += [(l, p, SUBBLOCKS[p][:1], CHIP_PEERS), (l, p, SUBBLOCKS[p][1:], ALL_PEERS)]
        else:
            shipments.append((l, p, SUBBLOCKS[p], ALL_PEERS))
    in_flight, order_token = [], jnp.zeros((1, 1), F32)
    for l, p, tensors, masks in shipments:
        shards = [travelling(args[n][l], by_cols).astype(BF16) for n, by_cols in tensors]
        shards[0] = shards[0] + order_token.astype(BF16)
        in_flight.append(spread_start(shards, per_peer=False, masks=masks, name=f"gather_start_l{l}_{tensors[0][0]}"))
        order_token = in_flight[-1][-1][0:1, 0:1]

    landed = {}

    def arrived(i, after):
        if i not in landed:
            landed[i] = wait_for(i, after)
        return landed[i]

    def wait_for(i, after):
        l, p, tensors, masks = shipments[i]
        tag = f"l{l}_{tensors[0][0]}"
        lands = spread_wait(in_flight[i], after, per_peer=False, masks=masks, name=f"gather_wait_{tag}")
        if masks is CHIP_PEERS:
            lands = relay_wait(relay_start(lands, name=f"gather_relay_{tag}"), name=f"gather_relayed_{tag}")
        return {n: land.reshape(-1, land.shape[-1]) for (n, _), land in zip(tensors, lands)}

    def weights_of(l, p, after):
        mine = [i for i, s in enumerate(shipments) if s[0:2] == (l, p)]
        w = arrived(mine[0], after)
        for i in mine[1:]:
            for n, _ in shipments[i][2]:
                w[n] = functools.partial(lambda after, i, n: arrived(i, after)[n], i=i, n=n)
        return w

    saved, weights = {}, {}
    act = xs
    h = rms_fwd(xs, _row(gains["norm_ffn1"][0]) + order_token, name="l0_ffn1_norm")
    for i, (l, p) in enumerate(parts):
        weights[(l, p)] = weights_of(l, p, h if i == 0 else act)
        nl, np_ = parts[i + 1] if i + 1 < len(parts) else (None, None)
        next_gain = _row(gains["norm_" + np_][nl]) if np_ else None
        if p == "mix":
            act, h, saved[(l, p)] = mixer_fwd(act, h, weights[(l, p)], tables, f"l{l}_mix", next_gain)
        else:
            act, h, saved[(l, p)] = ffn_fwd(act, h, weights[(l, p)], p, f"l{l}_{p}", next_gain)
    dx, dxb, g_final, loss_part = final_loss(act, _row(norm_final), target, name="loss_head")

    gain_grads, sent, sent_last = {}, {}, {}
    per_device = lambda g: g.reshape(N_DEV, -1, g.shape[-1])

    def ship_last(name, g):
        sent_last[name] = spread_start([per_device(g)], per_peer=True, name=f"reduce_start_{name}")
        return sent_last[name][-1]

    for l, p in reversed(parts):
        w, sv = weights[(l, p)], saved[(l, p)]
        if p == "mix":
            gw, rest = mixer_bwd_weights(dxb, sv, w, tables, f"l{l}_mix")
        elif (l, p) == parts[0]:
            gw, rest, token = ffn_bwd_weights(dxb, sv, w, p, f"l{l}_{p}", ship=ship_last)
        else:
            gw, rest, _ = ffn_bwd_weights(dxb, sv, w, p, f"l{l}_{p}")
        if (l, p) != parts[0]:
            sent[(l, p)] = spread_start([per_device(gw[n]) for n, _ in SUBBLOCKS[p]], per_peer=True, name=f"reduce_start_l{l}_{p}")
            token = sent[(l, p)][-1]
        gain = _row(gains["norm_" + p][l]) + token[0:1, 0:1]
        if p == "mix":
            dx, dxb, gain_grads[("norm_mix", l)] = mixer_bwd_input(dx, rest, sv, gain, w, f"l{l}_mix")
        else:
            dx, dxb, gain_grads[("norm_" + p, l)] = ffn_bwd_input(dx, rest, sv, gain, w, p, f"l{l}_{p}")

    partials, big_all = {}, [{}, {}, {}, {}]

    def receive(l, p, after):
        if (l, p) == parts[0]:
            for n, started in sent_last.items():
                partials.setdefault(n, [None, None])[l] = spread_wait(started, after, per_peer=True, name=f"reduce_wait_{n}")[0]
            return
        lands = spread_wait(sent[(l, p)], after, per_peer=True, name=f"reduce_wait_l{l}_{p}")
        for (n, _), land in zip(SUBBLOCKS[p], lands):
            partials.setdefault(n, [None, None])[l] = land

    def update(p):
        for n, by_cols in SUBBLOCKS[p]:
            if by_cols and args[n].shape[-1] % 128 == 0:
                outs = sum_adamw(partials[n], args[n], args["m_" + n], args["v_" + n], tr=256, transposed=True,
                                 name=f"update_{n}")
                for kind, arr in enumerate(outs):
                    big_all[kind][n] = arr
            else:
                outs = sum_adamw(partials[n], travelling(args[n], by_cols), travelling(args["m_" + n], by_cols),
                                 travelling(args["v_" + n], by_cols), tr=256, name=f"update_{n}")
                for kind, arr in enumerate(outs):
                    big_all[kind][n] = travelling(arr, by_cols)
        return outs[1]

    for l, p in reversed(parts[1:]):
        receive(l, p, dx)
    update("ffn2")
    done = update("mix")
    receive(*parts[0], done)
    done = update("ffn1")

    loss_row = jnp.pad(loss_part[:, :1], ((0, 0), (0, D_MODEL - 1)))
    small = jnp.concatenate([gain_grads[(n, l)] for n in NORM_ROWS for l in range(2)] + [g_final, loss_row], axis=0)
    small_g = sum_partials(all_gather_rows(small, done, name="gather_gain_grads"), tr=8, name="sum_gain_grads")
    zero_row = jnp.zeros((1, D_MODEL), F32)
    small_of = lambda pre: jnp.concatenate([args[pre + n] for n in NORM_ROWS] + [_row(args[pre + "norm_final"]), zero_row], axis=0)[None]
    small_out = adamw(small_g[None], small_of(""), small_of("m_"), small_of("v_"), tr=8, name="update_gains")
    small_all = [small_g] + [o[0] for o in small_out]

    def gains_of(s):
        out = {n: s[2 * i:2 * i + 2] for i, n in enumerate(NORM_ROWS)}
        out["norm_final"] = s[6]
        return out

    order = ["norm_ffn1", "ffn1_w_gate", "ffn1_w_up", "ffn1_w_down", "norm_mix", "w_in", "w_proj_dil", "w_proj_sb", "w_out",
             "norm_ffn2", "ffn2_w_gate", "ffn2_w_up", "ffn2_w_down", "norm_final"]
    results = []
    for kind in range(4):
        both = {**big_all[kind], **gains_of(small_all[kind])}
        results += [both[n] for n in order]
    loss = small_g[7, 0]
    return (loss, dx.reshape(1, t, D_MODEL), *results)
```

```python
import functools

import jax
import jax.numpy as jnp
from jax import lax
from jax.experimental import pallas as pl
from jax.experimental.pallas import tpu as pltpu

F32 = jnp.float32
BF16 = jnp.bfloat16

D_MODEL = 1024
HEAD_DIM = 64
GROUP_W = 256
D_IN = 5120
N_DIL_GROUPS = 3
DIL_SPAN = 128
DILATIONS = (1, 4, 16)
ROPE_THETA = 500000.0
ROPE_DIM = 16
RMS_EPS = 1e-6
ATT_SCALE = HEAD_DIM ** -0.5
QS_BLK, KS_BLK, VS_BLK = 9, 10, 11
GATE_DIL_BLK, GATE_SB_BLK = 3, 4

ADAM_LR, ADAM_B1, ADAM_B2, ADAM_EPS, ADAM_WD, ADAM_STEP = 0.001, 0.9, 0.999, 1e-08, 0.01, 10

N_DEV = 8
VMEM_PHYSICAL_V7X = 64 << 20
VMEM_TEMP_HEADROOM = 20 << 20

SUBBLOCKS = {
    "ffn1": (("ffn1_w_gate", True), ("ffn1_w_up", True), ("ffn1_w_down", False)),
    "mix": (("w_in", True), ("w_proj_dil", True), ("w_proj_sb", True), ("w_out", False)),
    "ffn2": (("ffn2_w_gate", True), ("ffn2_w_up", True), ("ffn2_w_down", False)),
}
NORM_ROWS = ("norm_ffn1", "norm_mix", "norm_ffn2")


def _nbytes(shape, dtype):
    n = 1
    for s in shape:
        n *= s
    return n * jnp.dtype(dtype).itemsize


def _pcall(body, *, name, grid, in_specs, out_specs, out_shape, blocks, scratch_shapes=(), scratch_bytes=0):
    need = 2 * sum(_nbytes(s, d) for s, d in blocks) + scratch_bytes + VMEM_TEMP_HEADROOM
    limit = min(need, VMEM_PHYSICAL_V7X - (4 << 20))
    in_hbm = lambda s: pltpu.HBM(s.shape, s.dtype)
    out_shape = [in_hbm(s) for s in out_shape] if isinstance(out_shape, (list, tuple)) else in_hbm(out_shape)
    call = pl.pallas_call(
        body, name=name, grid=grid, in_specs=in_specs, out_specs=out_specs, out_shape=out_shape,
        scratch_shapes=scratch_shapes,
        compiler_params=pltpu.CompilerParams(vmem_limit_bytes=limit),
    )
    return lambda *args: call(*[pltpu.with_memory_space_constraint(a, pltpu.HBM) for a in args])


def _dot(a, b, form):
    dn = {"nn": (((1,), (0,)), ((), ())), "nt": (((1,), (1,)), ((), ())), "tn": (((0,), (0,)), ((), ()))}[form]
    return lax.dot_general(a.astype(BF16), b.astype(BF16), dn, preferred_element_type=F32)


def _sigmoid(x):
    return 1.0 / (1.0 + jnp.exp(-x))


def matmul(a, b, form, *, tm, tn, tk, out_dtype, name, scale=1.0, after=None):
    if form == "tn":
        kdim, m = a.shape
        n = b.shape[1]
    else:
        m, kdim = a.shape
        n = b.shape[1] if form == "nn" else b.shape[0]
    tm, tn, tk = min(tm, m), min(tn, n), min(tk, kdim)
    assert m % tm == 0 and n % tn == 0 and kdim % tk == 0, (name, m, n, kdim, tm, tn, tk)
    nk = kdim // tk

    if form == "tn":
        a_blk, a_map = (tk, tm), (lambda j, i, k: (k, i))
    else:
        a_blk, a_map = (tm, tk), (lambda j, i, k: (i, k))
    if form == "nt":
        b_blk, b_map = (tn, tk), (lambda j, i, k: (j, k))
    else:
        b_blk, b_map = (tk, tn), (lambda j, i, k: (k, j))
    o_map = lambda j, i, k: (i, j)

    def body(a_ref, b_ref, *rest):
        o_ref, acc = (rest[1], rest[2:]) if after is not None else (rest[0], rest[1:])

        def finish(total):
            o_ref[...] = (total * scale if scale != 1.0 else total).astype(out_dtype)

        if nk == 1:
            finish(_dot(a_ref[...], b_ref[...], form))
        else:
            acc_ref, = acc
            k = pl.program_id(2)

            @pl.when(k == 0)
            def _():
                acc_ref[...] = _dot(a_ref[...], b_ref[...], form)

            @pl.when(k > 0)
            def _():
                acc_ref[...] += _dot(a_ref[...], b_ref[...], form)

            @pl.when(k == nk - 1)
            def _():
                finish(acc_ref[...])

    scratch = [pltpu.VMEM((tm, tn), F32)] if nk > 1 else []
    in_specs = [pl.BlockSpec(a_blk, a_map), pl.BlockSpec(b_blk, b_map)]
    args = [a, b]
    if after is not None:
        in_specs.append(pl.BlockSpec(memory_space=pl.ANY))
        args.append(after)
    return _pcall(
        body, name=name, grid=(n // tn, m // tm, nk), in_specs=in_specs,
        out_specs=pl.BlockSpec((tm, tn), o_map), out_shape=jax.ShapeDtypeStruct((m, n), out_dtype),
        blocks=[(a_blk, a.dtype), (b_blk, b.dtype), ((tm, tn), out_dtype)],
        scratch_shapes=scratch, scratch_bytes=(tm * tn * 4 if nk > 1 else 0),
    )(*args)


def swiglu_fwd(h, wg_t, wu_t, *, name, tm=512, tn=1408):
    t, d = h.shape
    f = wg_t.shape[0]
    tm, tn = min(tm, t), min(tn, f)

    def body(h_ref, wg_ref, wu_ref, ga_ref, gb_ref, s_ref):
        hh = h_ref[...]
        a = _dot(hh, wg_ref[...], "nt")
        b = _dot(hh, wu_ref[...], "nt")
        sg = _sigmoid(a)
        silu = a * sg
        ga_ref[...] = (b * (sg * (1.0 + a * (1.0 - sg)))).astype(BF16)
        gb_ref[...] = silu.astype(BF16)
        s_ref[...] = (silu * b).astype(BF16)

    w_spec = pl.BlockSpec((tn, d), lambda j, i: (j, 0))
    o_spec = pl.BlockSpec((tm, tn), lambda j, i: (i, j))
    o_shape = jax.ShapeDtypeStruct((t, f), BF16)
    return _pcall(
        body, name=name, grid=(f // tn, t // tm),
        in_specs=[pl.BlockSpec((tm, d), lambda j, i: (i, 0)), w_spec, w_spec],
        out_specs=[o_spec, o_spec, o_spec], out_shape=[o_shape, o_shape, o_shape],
        blocks=[((tm, d), BF16), ((tn, d), BF16), ((tn, d), BF16)] + [((tm, tn), BF16)] * 3,
    )(h, wg_t, wu_t)


def swiglu_bwd(dyb, wd, ga, gb, *, name, scale, tm=1024, tn=1408):
    t, d = dyb.shape
    f = wd.shape[0]
    tm, tn = min(tm, t), min(tn, f)

    def body(dy_ref, wd_ref, ga_ref, gb_ref, da_ref, db_ref):
        ds = _dot(dy_ref[...], wd_ref[...], "nt") * scale
        da_ref[...] = (ds * ga_ref[...].astype(F32)).astype(BF16)
        db_ref[...] = (ds * gb_ref[...].astype(F32)).astype(BF16)

    o_spec = pl.BlockSpec((tm, tn), lambda j, i: (i, j))
    o_shape = jax.ShapeDtypeStruct((t, f), BF16)
    return _pcall(
        body, name=name, grid=(f // tn, t // tm),
        in_specs=[pl.BlockSpec((tm, d), lambda j, i: (i, 0)), pl.BlockSpec((tn, d), lambda j, i: (j, 0)), o_spec, o_spec],
        out_specs=[o_spec, o_spec], out_shape=[o_shape, o_shape],
        blocks=[((tm, d), BF16), ((tn, d), BF16)] + [((tm, tn), BF16)] * 4,
    )(dyb, wd, ga, gb)


def gate_fwd(odil, osb, wpd_t, wps_t, proj, *, name, tm=512):
    t = odil.shape[0]
    tm = min(tm, t)

    def body(od_ref, os_ref, wpd_ref, wps_ref, g1_ref, g2_ref, y_ref, u1_ref, u2_ref):
        u1 = _dot(od_ref[...], wpd_ref[...], "nt")
        u2 = _dot(os_ref[...], wps_ref[...], "nt")
        y = _sigmoid(g1_ref[...].astype(F32)) * u1 + _sigmoid(g2_ref[...].astype(F32)) * u2
        y_ref[...] = y.astype(BF16)
        u1_ref[...] = u1.astype(BF16)
        u2_ref[...] = u2.astype(BF16)

    o_spec = pl.BlockSpec((tm, D_MODEL), lambda i: (i, 0))
    w_spec = pl.BlockSpec((D_MODEL, GROUP_W), lambda i: (0, 0))
    a_spec = pl.BlockSpec((tm, GROUP_W), lambda i: (i, 0))
    o_shape = jax.ShapeDtypeStruct((t, D_MODEL), BF16)
    return _pcall(
        body, name=name, grid=(t // tm,),
        in_specs=[a_spec, a_spec, w_spec, w_spec,
                  pl.BlockSpec((tm, D_MODEL), lambda i: (i, GATE_DIL_BLK)),
                  pl.BlockSpec((tm, D_MODEL), lambda i: (i, GATE_SB_BLK))],
        out_specs=[o_spec, o_spec, o_spec], out_shape=[o_shape, o_shape, o_shape],
        blocks=[((tm, GROUP_W), F32)] * 2 + [((D_MODEL, GROUP_W), BF16)] * 2 + [((tm, D_MODEL), BF16)] * 5,
    )(odil, osb, wpd_t, wps_t, proj, proj)


def gate_bwd(dxb, wout, u1, u2, proj, *, name, tm=512):
    t = dxb.shape[0]
    tm = min(tm, t)

    def body(dx_ref, w_ref, u1_ref, u2_ref, g1_ref, g2_ref, du1_ref, du2_ref, dg_ref):
        dy = _dot(dx_ref[...], w_ref[...], "nt")
        s1 = _sigmoid(g1_ref[...].astype(F32))
        s2 = _sigmoid(g2_ref[...].astype(F32))
        du1_ref[...] = (dy * s1).astype(BF16)
        du2_ref[...] = (dy * s2).astype(BF16)
        dg_ref[:, :D_MODEL] = (dy * u1_ref[...].astype(F32) * s1 * (1.0 - s1)).astype(BF16)
        dg_ref[:, D_MODEL:] = (dy * u2_ref[...].astype(F32) * s2 * (1.0 - s2)).astype(BF16)

    o_spec = pl.BlockSpec((tm, D_MODEL), lambda i: (i, 0))
    o_shape = jax.ShapeDtypeStruct((t, D_MODEL), BF16)
    return _pcall(
        body, name=name, grid=(t // tm,),
        in_specs=[o_spec, pl.BlockSpec((D_MODEL, D_MODEL), lambda i: (0, 0)), o_spec, o_spec,
                  pl.BlockSpec((tm, D_MODEL), lambda i: (i, GATE_DIL_BLK)),
                  pl.BlockSpec((tm, D_MODEL), lambda i: (i, GATE_SB_BLK))],
        out_specs=[o_spec, o_spec, pl.BlockSpec((tm, 2 * D_MODEL), lambda i: (i, 0))],
        out_shape=[o_shape, o_shape, jax.ShapeDtypeStruct((t, 2 * D_MODEL), BF16)],
        blocks=[((tm, D_MODEL), BF16)] * 9 + [((D_MODEL, D_MODEL), BF16)],
    )(dxb, wout, u1, u2, proj, proj)


def rms_fwd(x, gain, *, name, tm=512):
    t, d = x.shape
    tm = min(tm, t)

    def body(x_ref, g_ref, h_ref):
        xv = x_ref[...]
        rstd = lax.rsqrt(jnp.mean(xv * xv, axis=1, keepdims=True) + RMS_EPS)
        h_ref[...] = (xv * rstd * g_ref[...]).astype(BF16)

    return _pcall(
        body, name=name, grid=(t // tm,),
        in_specs=[pl.BlockSpec((tm, d), lambda i: (i, 0)), pl.BlockSpec((1, d), lambda i: (0, 0))],
        out_specs=pl.BlockSpec((tm, d), lambda i: (i, 0)), out_shape=jax.ShapeDtypeStruct((t, d), BF16),
        blocks=[((tm, d), F32), ((tm, d), BF16)],
    )(x, gain)


def matmul_res_norm(a, b, res, next_gain, *, scale, tm, name):
    t, k = a.shape
    d = b.shape[1]
    tm = min(tm, t)
    with_norm = next_gain is not None

    def body(a_ref, b_ref, r_ref, *rest):
        out = r_ref[...] + _dot(a_ref[...], b_ref[...], "nn") * scale
        if with_norm:
            g_ref, o_ref, h_ref = rest
            rstd = lax.rsqrt(jnp.mean(out * out, axis=1, keepdims=True) + RMS_EPS)
            h_ref[...] = (out * rstd * g_ref[...]).astype(BF16)
        else:
            o_ref, = rest
        o_ref[...] = out

    row = pl.BlockSpec((tm, d), lambda i: (i, 0))
    in_specs = [pl.BlockSpec((tm, k), lambda i: (i, 0)), pl.BlockSpec((k, d), lambda i: (0, 0)), row]
    args = [a, b, res]
    out_specs, out_shape = [row], [jax.ShapeDtypeStruct((t, d), F32)]
    if with_norm:
        in_specs.append(pl.BlockSpec((1, d), lambda i: (0, 0)))
        args.append(next_gain)
        out_specs.append(row)
        out_shape.append(jax.ShapeDtypeStruct((t, d), BF16))
    outs = _pcall(
        body, name=name, grid=(t // tm,), in_specs=in_specs, out_specs=out_specs, out_shape=out_shape,
        blocks=[((tm, k), a.dtype), ((k, d), b.dtype), ((tm, d), F32), ((tm, d), F32), ((tm, d), BF16)],
    )(*args)
    return (outs[0], outs[1]) if with_norm else (outs[0], None)


def _rms_bwd_rows(dhv, xv, g, drv):
    rstd = lax.rsqrt(jnp.mean(xv * xv, axis=1, keepdims=True) + RMS_EPS)
    xh = xv * rstd
    dxh = dhv * g
    dx = drv + rstd * (dxh - xh * jnp.mean(dxh * xh, axis=1, keepdims=True))
    return dx, jnp.sum(dhv * xh, axis=0, keepdims=True)


def matmul_rms_bwd(pairs, x, gain, dres, *, tm, name):
    t, d = x.shape
    tm = min(tm, t)
    npairs = len(pairs)

    def body(*refs):
        ab = refs[:2 * npairs]
        x_ref, g_ref, dr_ref, dx_ref, dxb_ref, dg_ref = refs[2 * npairs:]
        dh = _dot(ab[0][...], ab[1][...], "nn")
        for q in range(1, npairs):
            dh = dh + _dot(ab[2 * q][...], ab[2 * q + 1][...], "nn")
        dx, part = _rms_bwd_rows(dh, x_ref[...], g_ref[...], dr_ref[...])
        dx_ref[...] = dx
        dxb_ref[...] = dx.astype(BF16)

        @pl.when(pl.program_id(0) == 0)
        def _():
            dg_ref[...] = part

        @pl.when(pl.program_id(0) > 0)
        def _():
            dg_ref[...] += part

    in_specs, args, blocks = [], [], []
    for a, b in pairs:
        k = a.shape[1]
        in_specs += [pl.BlockSpec((tm, k), lambda i: (i, 0)), pl.BlockSpec((k, d), lambda i: (0, 0))]
        args += [a, b]
        blocks += [((tm, k), a.dtype), ((k, d), b.dtype)]
    row = pl.BlockSpec((tm, d), lambda i: (i, 0))
    vec = pl.BlockSpec((1, d), lambda i: (0, 0))
    return _pcall(
        body, name=name, grid=(t // tm,), in_specs=in_specs + [row, vec, row], out_specs=[row, row, vec],
        out_shape=[jax.ShapeDtypeStruct((t, d), F32), jax.ShapeDtypeStruct((t, d), BF16), jax.ShapeDtypeStruct((1, d), F32)],
        blocks=blocks + [((tm, d), F32)] * 3 + [((tm, d), BF16)],
    )(*args, x, gain, dres)


def final_loss(x, gain, target, *, name, tm=512):
    t, d = x.shape
    tm = min(tm, t)

    def body(x_ref, g_ref, t_ref, dx_ref, dxb_ref, dg_ref, loss_ref):
        xv = x_ref[...]
        g = g_ref[...]
        rstd = lax.rsqrt(jnp.mean(xv * xv, axis=1, keepdims=True) + RMS_EPS)
        xh = xv * rstd
        err = xh * g - t_ref[...]
        dy = err * (1.0 / d)
        dxh = dy * g
        dx = rstd * (dxh - xh * jnp.mean(dxh * xh, axis=1, keepdims=True))
        dx_ref[...] = dx
        dxb_ref[...] = dx.astype(BF16)
        part = jnp.sum(dy * xh, axis=0, keepdims=True)
        sq = jnp.sum(jnp.sum(err * err, axis=1, keepdims=True), axis=0, keepdims=True) * (0.5 / d)
        lpart = jnp.broadcast_to(sq, (1, 128))

        @pl.when(pl.program_id(0) == 0)
        def _():
            dg_ref[...] = part
            loss_ref[...] = lpart

        @pl.when(pl.program_id(0) > 0)
        def _():
            dg_ref[...] += part
            loss_ref[...] += lpart

    row = pl.BlockSpec((tm, d), lambda i: (i, 0))
    vec = pl.BlockSpec((1, d), lambda i: (0, 0))
    return _pcall(
        body, name=name, grid=(t // tm,), in_specs=[row, vec, row],
        out_specs=[row, row, vec, pl.BlockSpec((1, 128), lambda i: (0, 0))],
        out_shape=[jax.ShapeDtypeStruct((t, d), F32), jax.ShapeDtypeStruct((t, d), BF16),
                   jax.ShapeDtypeStruct((1, d), F32), jax.ShapeDtypeStruct((1, 128), F32)],
        blocks=[((tm, d), F32)] * 3 + [((tm, d), BF16)],
    )(x, gain, target)


def rope_tables(t):
    pos = jnp.arange(t, dtype=F32)
    inv_freq = ROPE_THETA ** (-jnp.arange(0, ROPE_DIM, 2, dtype=F32) / ROPE_DIM)
    ang = pos[:, None] * inv_freq[None, :]
    cos, sin = jnp.cos(ang), jnp.sin(ang)
    half = ROPE_DIM // 2
    in_head = jnp.arange(128) % HEAD_DIM
    cosw, sinw = jnp.tile(cos, (1, 128 // half)), jnp.tile(sin, (1, 128 // half))
    c = jnp.where(in_head < ROPE_DIM, cosw, 1.0)
    sa = jnp.where(in_head < half, -sinw, 0.0)
    sb = jnp.where((in_head >= half) & (in_head < ROPE_DIM), sinw, 0.0)
    return jnp.concatenate([c, sa, sb], axis=1)


def _rotate(xv, cv, sav, sbv):
    halves = []
    for half in range(2):
        x = xv[:, 128 * half:128 * (half + 1)]
        halves.append(x * cv + pltpu.roll(x, 120, 1) * sav + pltpu.roll(x, 8, 1) * sbv)
    return jnp.concatenate(halves, axis=1)


STAGE_CHUNKS = 4


def _stage(tm):
    return dict(scratch_shapes=[pltpu.VMEM((STAGE_CHUNKS, tm, 128), F32)], scratch_bytes=STAGE_CHUNKS * tm * 128 * 4)


def _split_residues(stage_ref, val, out_ref, d, col, dtype):
    rows, width = val.shape
    if d == 1:
        out_ref[0, :, col:col + width] = val.astype(dtype)
        return
    chunks = width // 128
    for c in range(chunks):
        stage_ref[c] = val[:, 128 * c:128 * (c + 1)]
    for r in range(d):
        for c in range(chunks):
            out_ref[r, :, col + 128 * c:col + 128 * (c + 1)] = stage_ref[c, pl.ds(r, rows // d, stride=d), :].astype(dtype)


def _join_residues(stage_ref, in_ref, d, col=0, width=GROUP_W):
    if d == 1:
        return in_ref[0, :, col:col + width].astype(F32)
    rows = in_ref.shape[1] * d
    chunks = width // 128
    for r in range(d):
        for c in range(chunks):
            stage_ref[c, pl.ds(r, rows // d, stride=d), :] = in_ref[r, :, col + 128 * c:col + 128 * (c + 1)].astype(F32)
    return jnp.concatenate([stage_ref[c] for c in range(chunks)], axis=1)


def rope_split(proj, tables, *, name, tm=512):
    c = sa = sb = tables
    t = tables.shape[0]
    tm = min(tm, t)

    def body(*refs):
        pieces = refs[0:9]
        c_ref, sa_ref, sb_ref = refs[9:12]
        qk_out, v_out = refs[12:15], refs[15:18]
        stage = refs[18]
        cv, sav, sbv = c_ref[...], sa_ref[...], sb_ref[...]
        for g, d in enumerate(DILATIONS):
            for kind in range(3):
                xv = pieces[3 * kind + g][...].astype(F32)
                if kind < 2:
                    _split_residues(stage, _rotate(xv, cv, sav, sbv), qk_out[g], d, GROUP_W * kind, BF16)
                else:
                    _split_residues(stage, xv, v_out[g], d, 0, BF16)

    tabs = [pl.BlockSpec((tm, 128), functools.partial(lambda i, cb: (i, cb), cb=cb)) for cb in range(3)]
    in_specs = [pl.BlockSpec((tm, GROUP_W), functools.partial(lambda i, cb: (i, cb), cb=cb)) for cb in range(9)]
    out_specs = ([pl.BlockSpec((d, tm // d, 2 * GROUP_W), lambda i: (0, i, 0)) for d in DILATIONS]
                 + [pl.BlockSpec((d, tm // d, GROUP_W), lambda i: (0, i, 0)) for d in DILATIONS])
    out_shape = ([jax.ShapeDtypeStruct((d, t // d, 2 * GROUP_W), BF16) for d in DILATIONS]
                 + [jax.ShapeDtypeStruct((d, t // d, GROUP_W), BF16) for d in DILATIONS])
    outs = _pcall(
        body, name=name, grid=(t // tm,), in_specs=in_specs + tabs, out_specs=out_specs, out_shape=out_shape,
        blocks=[((tm, GROUP_W), BF16)] * 18 + [((tm, 128), F32)] * 3,
        **_stage(tm),
    )(*([proj] * 9), c, sa, sb)
    return outs[0:3], outs[3:6]


def rope_join(dqs, dks, dvs, sb_grads, dgate, tables, *, name, tm=512):
    c = sa = sb = tables
    t = tables.shape[0]
    tm = min(tm, t)

    def body(*refs):
        pieces, sb_refs, dgate_ref = refs[0:9], refs[9:12], refs[12]
        c_ref, sa_ref, sb_ref = refs[13:16]
        o_ref, stage = refs[16], refs[17]
        cv, sav, sbv = c_ref[...], -sa_ref[...], -sb_ref[...]
        for kind in range(3):
            for g, d in enumerate(DILATIONS):
                xv = _join_residues(stage, pieces[3 * kind + g], d)
                if kind < 2:
                    xv = _rotate(xv, cv, sav, sbv)
                col = GROUP_W * (3 * kind + g)
                o_ref[:, col:col + GROUP_W] = xv.astype(BF16)
        for j in range(3):
            o_ref[:, GROUP_W * (QS_BLK + j):GROUP_W * (QS_BLK + j + 1)] = sb_refs[j][...].astype(BF16)
        o_ref[:, D_MODEL * GATE_DIL_BLK:] = dgate_ref[...]

    tabs = [pl.BlockSpec((tm, 128), functools.partial(lambda i, cb: (i, cb), cb=cb)) for cb in range(3)]
    nat = lambda w: pl.BlockSpec((tm, w), lambda i: (i, 0))
    in_specs = [pl.BlockSpec((d, tm // d, GROUP_W), lambda i: (0, i, 0)) for _ in range(3) for d in DILATIONS]
    in_specs += [nat(GROUP_W)] * 3 + [nat(2 * D_MODEL)]
    return _pcall(
        body, name=name, grid=(t // tm,), in_specs=in_specs + tabs,
        out_specs=nat(D_IN), out_shape=jax.ShapeDtypeStruct((t, D_IN), BF16),
        blocks=[((tm, GROUP_W), F32)] * 12 + [((tm, 128), F32)] * 3 + [((tm, 2 * D_MODEL), BF16), ((tm, D_IN), BF16)],
        **_stage(tm),
    )(*dqs, *dks, *dvs, *sb_grads, dgate, c, sa, sb)


def _head_mask(h):
    lane = lax.broadcasted_iota(jnp.int32, (1, GROUP_W), 1)
    return (lane // HEAD_DIM) == h


def _band_mask_before():
    ri = lax.broadcasted_iota(jnp.int32, (4 * DIL_SPAN, DIL_SPAN), 0) % DIL_SPAN
    ci = lax.broadcasted_iota(jnp.int32, (4 * DIL_SPAN, DIL_SPAN), 1)
    return ci >= ri


def dil_fwd(qk, v, *, name):
    d, nsub, _ = qk.shape
    nblk = nsub // DIL_SPAN

    def body(q_ref, kc_ref, kp_ref, vc_ref, vp_ref, o_ref, lse_ref):
        nb = pl.program_id(1)
        kk = jnp.concatenate([kp_ref[0], kc_ref[0]], axis=0)
        vv = jnp.concatenate([vp_ref[0], vc_ref[0]], axis=0)
        s = _dot(_stack_heads(q_ref[0] * ATT_SCALE), kk, "nt")
        ri = lax.broadcasted_iota(jnp.int32, s.shape, 0) % DIL_SPAN
        ci = lax.broadcasted_iota(jnp.int32, s.shape, 1)
        valid = ((ci < DIL_SPAN) & (ci >= ri) & (nb > 0)) | ((ci >= DIL_SPAN) & (ci - DIL_SPAN <= ri))
        s = jnp.where(valid, s, -jnp.inf)
        m = jnp.max(s, axis=1, keepdims=True)
        p = jnp.exp(s - m)
        den = jnp.sum(p, axis=1, keepdims=True)
        o_ref[0] = _unstack_heads(_dot(p, vv, "nn") / den, DIL_SPAN)
        lse = m + jnp.log(den)
        for h in range(4):
            lse_ref[0, :, 128 * h:128 * (h + 1)] = jnp.broadcast_to(lse[DIL_SPAN * h:DIL_SPAN * (h + 1)], (DIL_SPAN, 128))

    blk = (1, DIL_SPAN, GROUP_W)
    sblk = (1, DIL_SPAN, 512)
    prv = lambda nb: jnp.maximum(nb - 1, 0)
    return _pcall(
        body, name=name, grid=(d, nblk),
        in_specs=[pl.BlockSpec(blk, lambda r, nb: (r, nb, 0)),
                  pl.BlockSpec(blk, lambda r, nb: (r, nb, 1)),
                  pl.BlockSpec(blk, lambda r, nb: (r, prv(nb), 1)),
                  pl.BlockSpec(blk, lambda r, nb: (r, nb, 0)),
                  pl.BlockSpec(blk, lambda r, nb: (r, prv(nb), 0))],
        out_specs=[pl.BlockSpec(blk, lambda r, nb: (r, nb, 0)), pl.BlockSpec(sblk, lambda r, nb: (r, nb, 0))],
        out_shape=[jax.ShapeDtypeStruct((d, nsub, GROUP_W), F32), jax.ShapeDtypeStruct((d, nsub, 512), F32)],
        blocks=[(blk, BF16)] * 5 + [(blk, F32), (sblk, F32)],
    )(qk, qk, qk, v, v)


def dil_merge(outs, lses, *, name, tm=512):
    t = outs[0].shape[0] * outs[0].shape[1]
    tm = min(tm, t)

    def body(o0, o1, o2, l0, l1, l2, o_ref, lse_ref, stage):
        ls = [_join_residues(stage, l, d, 0, 512) for l, d in zip((l0, l1, l2), DILATIONS)]
        m = jnp.maximum(jnp.maximum(ls[0], ls[1]), ls[2])
        tot = m + jnp.log(jnp.exp(ls[0] - m) + jnp.exp(ls[1] - m) + jnp.exp(ls[2] - m))
        lse_ref[...] = tot
        lane = lax.broadcasted_iota(jnp.int32, (1, 128), 1)
        first = lane < HEAD_DIM
        acc = jnp.zeros((tm, GROUP_W), F32)
        for og, lg, d in zip((o0, o1, o2), ls, DILATIONS):
            w = jnp.exp(lg - tot)
            wide = jnp.concatenate([jnp.where(first, w[:, 0:128], w[:, 128:256]),
                                    jnp.where(first, w[:, 256:384], w[:, 384:512])], axis=1)
            acc = acc + wide * _join_residues(stage, og, d)
        o_ref[...] = acc

    o_in = [pl.BlockSpec((d, tm // d, GROUP_W), lambda i: (0, i, 0)) for d in DILATIONS]
    l_in = [pl.BlockSpec((d, tm // d, 512), lambda i: (0, i, 0)) for d in DILATIONS]
    return _pcall(
        body, name=name, grid=(t // tm,), in_specs=o_in + l_in,
        out_specs=[pl.BlockSpec((tm, GROUP_W), lambda i: (i, 0)), pl.BlockSpec((tm, 512), lambda i: (i, 0))],
        out_shape=[jax.ShapeDtypeStruct((t, GROUP_W), F32), jax.ShapeDtypeStruct((t, 512), F32)],
        blocks=[((tm, GROUP_W), F32)] * 4 + [((tm, 512), F32)] * 4,
        **_stage(tm),
    )(*outs, *lses)


def dil_bwd_prep(do, o, lse, *, name, tm=512):
    t = do.shape[0]
    tm = min(tm, t)
    wide = DILATIONS[1:]

    def body(do_ref, o_ref, lse_ref, ds_ref, *rest):
        do_out, lse_out, ds_out = rest[0:2], rest[2:4], rest[4:6]
        stage = rest[6]
        dov = do_ref[...]
        prod = dov * o_ref[...]
        for h in range(4):
            s = jnp.sum(jnp.where(_head_mask(h), prod, 0.0), axis=1, keepdims=True)
            ds_ref[:, 128 * h:128 * (h + 1)] = jnp.broadcast_to(s, (tm, 128))
        for i, d in enumerate(wide):
            _split_residues(stage, dov, do_out[i], d, 0, BF16)
            _split_residues(stage, lse_ref[...], lse_out[i], d, 0, F32)
            _split_residues(stage, ds_ref[...], ds_out[i], d, 0, F32)

    nat = lambda w: pl.BlockSpec((tm, w), lambda i: (i, 0))
    res = lambda d, w: pl.BlockSpec((d, tm // d, w), lambda i: (0, i, 0))
    shape = lambda d, w, dt: jax.ShapeDtypeStruct((d, t // d, w), dt)
    outs = _pcall(
        body, name=name, grid=(t // tm,), in_specs=[nat(GROUP_W), nat(GROUP_W), nat(512)],
        out_specs=[nat(512)] + [res(d, GROUP_W) for d in wide] + [res(d, 512) for d in wide] * 2,
        out_shape=([jax.ShapeDtypeStruct((t, 512), F32)] + [shape(d, GROUP_W, BF16) for d in wide]
                   + [shape(d, 512, F32) for d in wide] * 2),
        blocks=[((tm, GROUP_W), F32)] * 3 + [((tm, 512), F32)] * 6,
        **_stage(tm),
    )(do, o, lse)
    return outs[0], outs[1:3], outs[3:5], outs[5:7]


def head_sums(a, b, *, name, tm=512):
    t = a.shape[0]
    tm = min(tm, t)

    def body(a_ref, b_ref, o_ref):
        prod = a_ref[...].astype(BF16).astype(F32) * b_ref[...]
        for h in range(4):
            s = jnp.sum(jnp.where(_head_mask(h), prod, 0.0), axis=1, keepdims=True)
            o_ref[:, 128 * h:128 * (h + 1)] = jnp.broadcast_to(s, (tm, 128))

    spec = pl.BlockSpec((tm, GROUP_W), lambda i: (i, 0))
    return _pcall(
        body, name=name, grid=(t // tm,), in_specs=[spec, spec],
        out_specs=pl.BlockSpec((tm, 512), lambda i: (i, 0)), out_shape=jax.ShapeDtypeStruct((t, 512), F32),
        blocks=[((tm, GROUP_W), F32)] * 2 + [((tm, 512), F32)],
    )(a, b)


def dil_bwd(qk, v, do, lse, dsum, *, name):
    d, nsub, _ = qk.shape
    nblk = nsub // DIL_SPAN

    def body(qa_ref, qb_ref, kc_ref, kp_ref, vc_ref, vp_ref, doa_ref, dob_ref, la_ref, lb_ref, sa_ref, sb_ref,
             dq_ref, dk_ref, dv_ref):
        nb = pl.program_id(1)
        nxt = _band_mask_before() & (nb < nblk - 1)
        kc, kp, vc, vp = kc_ref[0], kp_ref[0], vc_ref[0], vp_ref[0]
        qas, qbs = _stack_heads(qa_ref[0] * ATT_SCALE), _stack_heads(qb_ref[0] * ATT_SCALE)
        das, dbs = _stack_heads(doa_ref[0].astype(BF16)), _stack_heads(dob_ref[0].astype(BF16))
        stat = lambda ref: jnp.concatenate([ref[0, :, 128 * h:128 * (h + 1)] for h in range(4)], axis=0)
        la, lb, sa, sb = stat(la_ref), stat(lb_ref), stat(sa_ref), stat(sb_ref)

        def probs(qs, ds_, k, v, mask, l, s):
            p = jnp.where(mask, jnp.exp(_dot(qs, k, "nt") - l), 0.0)
            dsc = p * (_dot(ds_, v, "nt") - s)
            return p.astype(BF16), dsc.astype(BF16)

        wide = lambda a: jnp.concatenate([a, a], axis=1)
        ri = lax.broadcasted_iota(jnp.int32, (4 * DIL_SPAN, 2 * DIL_SPAN), 0) % DIL_SPAN
        ci = lax.broadcasted_iota(jnp.int32, (4 * DIL_SPAN, 2 * DIL_SPAN), 1)
        valid = ((ci < DIL_SPAN) & (ci >= ri) & (nb > 0)) | ((ci >= DIL_SPAN) & (ci - DIL_SPAN <= ri))
        p_a, ds_a = probs(qas, das, jnp.concatenate([kp, kc], axis=0), jnp.concatenate([vp, vc], axis=0),
                          valid, wide(la), wide(sa))
        p_nc, ds_nc = probs(qbs, dbs, kc, vc, nxt, lb, sb)
        dq_ref[0] = _unstack_heads(_dot(ds_a, jnp.concatenate([kp, kc], axis=0), "nn"), DIL_SPAN) * ATT_SCALE
        dk_ref[0] = _dot(ds_a[:, DIL_SPAN:], qas, "tn") + _dot(ds_nc, qbs, "tn")
        dv_ref[0] = _dot(p_a[:, DIL_SPAN:], das, "tn") + _dot(p_nc, dbs, "tn")

    blk = (1, DIL_SPAN, GROUP_W)
    sblk = (1, DIL_SPAN, 512)
    prv = lambda nb: jnp.maximum(nb - 1, 0)
    nxt_ = lambda nb: jnp.minimum(nb + 1, nblk - 1)
    cur_at = lambda c: pl.BlockSpec(blk, functools.partial(lambda r, nb, c: (r, nb, c), c=c))
    prv_at = lambda c: pl.BlockSpec(blk, functools.partial(lambda r, nb, c: (r, prv(nb), c), c=c))
    nxt_at = lambda c: pl.BlockSpec(blk, functools.partial(lambda r, nb, c: (r, nxt_(nb), c), c=c))
    s_cur = pl.BlockSpec(sblk, lambda r, nb: (r, nb, 0))
    s_nxt = pl.BlockSpec(sblk, lambda r, nb: (r, nxt_(nb), 0))
    o_spec = pl.BlockSpec(blk, lambda r, nb: (r, nb, 0))
    o_shape = jax.ShapeDtypeStruct((d, nsub, GROUP_W), F32)
    return _pcall(
        body, name=name, grid=(d, nblk),
        in_specs=[cur_at(0), nxt_at(0), cur_at(1), prv_at(1), cur_at(0), prv_at(0), cur_at(0), nxt_at(0),
                  s_cur, s_nxt, s_cur, s_nxt],
        out_specs=[o_spec, o_spec, o_spec], out_shape=[o_shape, o_shape, o_shape],
        blocks=[(blk, BF16)] * 6 + [(blk, F32)] * 5 + [(sblk, F32)] * 4,
    )(qk, qk, qk, qk, v, v, do, do, lse, lse, dsum, dsum)


def _tri_dot(x, b):
    hi = x.astype(BF16)
    lo = (x - hi.astype(F32)).astype(BF16)
    return _dot(jnp.concatenate([hi, lo], axis=1), jnp.concatenate([b, b], axis=0), "nn")


SB_TILE = 256
SB_ROWS = 512


def _stack_heads(a):
    return jnp.concatenate([jnp.where(_head_mask(h), a, jnp.zeros_like(a)) for h in range(4)], axis=0)


def _unstack_heads(acc, rows):
    out = acc[0:rows]
    for h in range(1, 4):
        out = jnp.where(_head_mask(h), acc[h * rows:(h + 1) * rows], out)
    return out


def _tri_masks(n):
    ri = lax.broadcasted_iota(jnp.int32, (n, n), 0)
    ci = lax.broadcasted_iota(jnp.int32, (n, n), 1)
    return (ri > ci).astype(BF16), (ri >= ci).astype(BF16)


def _sb_weights(qs, kt, after, c_keep, lead):
    z = _dot(qs, kt, "nt")
    lbeta = jnp.minimum(z, 0.0) - jnp.log(1.0 + jnp.exp(-jnp.abs(z)))
    lkeep = lbeta - z
    past = None
    if lead is not None:
        query = lax.broadcasted_iota(jnp.int32, z.shape, 0) % SB_ROWS
        past = lax.broadcasted_iota(jnp.int32, z.shape, 1) + lead < query
        lkeep = jnp.where(past, lkeep, 0.0)
    w = jnp.exp(lbeta + _tri_dot(lkeep, after) + c_keep)
    if lead is not None:
        w = jnp.where(past, w, 0.0)
    return z, past, lbeta, lkeep, w


def _sb_walk(qb, tile, carry):
    per = SB_ROWS // SB_TILE
    for i in reversed(range(per)):
        carry = tile(pl.multiple_of(qb * SB_ROWS + i * SB_TILE, SB_TILE), i * SB_TILE, i == per - 1, carry)
    past_tiles = qb * per
    return lax.fori_loop(0, past_tiles,
                         lambda it, c: tile(pl.multiple_of((past_tiles - 1 - it) * SB_TILE, SB_TILE), None, False, c), carry)


def sb_fwd(proj, *, name):
    t = proj.shape[0]
    n, m = SB_TILE, SB_ROWS
    assert t % m == 0

    def body(q_ref, k_ref, v_ref, o_ref, acc_ref):
        qb = pl.program_id(0)
        qs = _stack_heads(q_ref[...] * ATT_SCALE)
        after, _ = _tri_masks(n)

        def tile(off, lead, first, c_keep):
            kt = k_ref[pl.ds(off, n), :]
            vt = v_ref[pl.ds(off, n), :]
            _, _, _, lkeep, w = _sb_weights(qs, kt, after, c_keep, lead)
            pv = _tri_dot(w, vt)
            if first:
                acc_ref[...] = pv
            else:
                acc_ref[...] += pv
            return c_keep + jnp.sum(lkeep, axis=1, keepdims=True)

        _sb_walk(qb, tile, jnp.zeros((4 * m, 1), F32))
        o_ref[...] = _unstack_heads(acc_ref[...], m)

    full = lambda cb: pl.BlockSpec((t, GROUP_W), functools.partial(lambda i, cb: (0, cb), cb=cb))
    return _pcall(
        body, name=name, grid=(t // m,),
        in_specs=[pl.BlockSpec((m, GROUP_W), lambda i: (i, QS_BLK)), full(KS_BLK), full(VS_BLK)],
        out_specs=pl.BlockSpec((m, GROUP_W), lambda i: (i, 0)), out_shape=jax.ShapeDtypeStruct((t, GROUP_W), F32),
        blocks=[((m, GROUP_W), BF16), ((t, GROUP_W), BF16), ((t, GROUP_W), BF16), ((m, GROUP_W), F32)],
        scratch_shapes=[pltpu.VMEM((4 * m, GROUP_W), F32)], scratch_bytes=4 * m * GROUP_W * 4,
    )(proj, proj, proj)


def sb_bwd(proj, do, gtot, *, name):
    t = proj.shape[0]
    n, m = SB_TILE, SB_ROWS
    assert t % m == 0

    def body(q_ref, k_ref, v_ref, do_ref, gt_ref, dq_ref, dk_ref, dv_ref, acc_ref):
        qb = pl.program_id(0)

        @pl.when(qb == 0)
        def _():
            dk_ref[...] = jnp.zeros_like(dk_ref)
            dv_ref[...] = jnp.zeros_like(dv_ref)

        qs = _stack_heads(q_ref[...] * ATT_SCALE)
        dos = _stack_heads(do_ref[...].astype(BF16))
        gt = jnp.concatenate([jnp.max(gt_ref[:, 128 * h:128 * (h + 1)], axis=1, keepdims=True) for h in range(4)], axis=0)
        after, from_on = _tri_masks(n)

        def tile(off, lead, first, carry):
            c_keep, c_g = carry
            kt = k_ref[pl.ds(off, n), :]
            vt = v_ref[pl.ds(off, n), :]
            z, past, lbeta, lkeep, w = _sb_weights(qs, kt, after, c_keep, lead)
            gw = w * _dot(dos, vt, "nt")
            big_g = gt - (_tri_dot(gw, from_on) + c_g)
            dz = gw * jnp.exp(lbeta - z) - big_g * jnp.exp(lbeta)
            if lead is not None:
                dz = jnp.where(past, dz, 0.0)
            dz = dz.astype(BF16)
            dk_ref[pl.ds(off, n), :] += _dot(dz, qs, "tn")
            dv_ref[pl.ds(off, n), :] += _dot(w, dos, "tn")
            dq = _dot(dz, kt, "nn")
            if first:
                acc_ref[...] = dq
            else:
                acc_ref[...] += dq
            return c_keep + jnp.sum(lkeep, axis=1, keepdims=True), c_g + jnp.sum(gw, axis=1, keepdims=True)

        zero_col = jnp.zeros((4 * m, 1), F32)
        _sb_walk(qb, tile, (zero_col, zero_col))
        dq_ref[...] = _unstack_heads(acc_ref[...], m) * ATT_SCALE

    full = lambda cb: pl.BlockSpec((t, GROUP_W), functools.partial(lambda i, cb: (0, cb), cb=cb))
    whole = pl.BlockSpec((t, GROUP_W), lambda i: (0, 0))
    rowblk = pl.BlockSpec((m, GROUP_W), lambda i: (i, 0))
    shape = jax.ShapeDtypeStruct((t, GROUP_W), F32)
    return _pcall(
        body, name=name, grid=(t // m,),
        in_specs=[pl.BlockSpec((m, GROUP_W), lambda i: (i, QS_BLK)), full(KS_BLK), full(VS_BLK), rowblk,
                  pl.BlockSpec((m, 512), lambda i: (i, 0))],
        out_specs=[rowblk, whole, whole], out_shape=[shape, shape, shape],
        blocks=[((m, GROUP_W), BF16), ((t, GROUP_W), BF16), ((t, GROUP_W), BF16), ((m, GROUP_W), F32),
                ((m, 512), F32), ((m, GROUP_W), F32), ((t, GROUP_W), F32), ((t, GROUP_W), F32)],
        scratch_shapes=[pltpu.VMEM((4 * m, GROUP_W), F32)], scratch_bytes=4 * m * GROUP_W * 4,
    )(proj, proj, proj, do, gtot)


def _mesh_place():
    return lax.axis_index("x"), lax.axis_index("y"), lax.axis_index("c")


def _flip(place, mask):
    x, y, c = place
    return ((1 - x) if mask & 4 else x, (1 - y) if mask & 2 else y, (1 - c) if mask & 1 else c)


def _dev_index(place):
    x, y, c = place
    return 4 * x + 2 * y + c


HBM_SPEC = pl.BlockSpec(memory_space=pltpu.HBM)


def all_gather_rows(shard, after, *, name):
    rows, lanes = shard.shape

    def body(x_ref, after_ref, out_ref, send_sems, recv_sems, local_sem):
        me = _mesh_place()
        x, y, c = me
        sibling = _flip(me, 1)
        chips = [_flip(me, 4), _flip(me, 2), _flip(me, 6)]

        def copy(k, block, to, src=None):
            dst = out_ref.at[_dev_index(block)]
            return pltpu.make_async_remote_copy(
                src_ref=dst if src is None else src, dst_ref=dst, send_sem=send_sems.at[k], recv_sem=recv_sems.at[k],
                device_id=to, device_id_type=pl.DeviceIdType.MESH)

        mine = pltpu.make_async_copy(x_ref, out_ref.at[_dev_index(me)], local_sem)
        mine.start()
        first = [copy(0, me, sibling, src=x_ref)] + [copy(1 + j, me, chip, src=x_ref) for j, chip in enumerate(chips)]
        for cp in first:
            cp.start()
        passed = [copy(4 + j, chip, sibling) for j, chip in enumerate(chips)]
        for j, chip in enumerate(chips):
            copy(1 + j, chip, me).wait_recv()
            passed[j].start()
        copy(0, sibling, me).wait_recv()
        for j, chip in enumerate(chips):
            copy(4 + j, _flip(chip, 1), me).wait_recv()
        for cp in first + passed:
            cp.wait_send()
        mine.wait()

    return pl.pallas_call(
        body, name=name, in_specs=[HBM_SPEC, pl.BlockSpec(memory_space=pl.ANY)], out_specs=HBM_SPEC,
        out_shape=jax.ShapeDtypeStruct((N_DEV, rows, lanes), shard.dtype),
        scratch_shapes=[pltpu.SemaphoreType.DMA((7,)), pltpu.SemaphoreType.DMA((7,)), pltpu.SemaphoreType.DMA],
    )(shard, after)


SEM_SPEC = pl.BlockSpec(memory_space=pltpu.SEMAPHORE)
DATAFLOW_EFFECT = pltpu.SideEffectType.DATAFLOW_SIDE_EFFECTING


ALL_PEERS = tuple(range(1, N_DEV))
CHIP_PEERS = (1, 4, 2, 6)
OTHER_CHIPS = (4, 2, 6)


def _spread_copies(src_refs, land_refs, send_sems, recv_sems, per_peer, masks, arriving):
    me = _mesh_place()
    my = _dev_index(me)
    remote, local = [], []
    for t, (src_ref, land_ref) in enumerate(zip(src_refs, land_refs)):
        for i, mask in enumerate(masks):
            peer = _flip(me, mask)
            data_of = my if arriving else _dev_index(peer)
            slot = _dev_index(peer) if arriving else my
            k = t * len(masks) + i
            remote.append(pltpu.make_async_remote_copy(
                src_ref=src_ref.at[data_of] if per_peer else src_ref, dst_ref=land_ref.at[slot],
                send_sem=send_sems.at[k], recv_sem=recv_sems.at[k],
                device_id=peer, device_id_type=pl.DeviceIdType.MESH))
        local.append(pltpu.make_async_copy(src_ref.at[my] if per_peer else src_ref, land_ref.at[my],
                                           send_sems.at[len(src_refs) * len(masks) + t]))
    return remote, local


def spread_start(srcs, *, per_peer, name, masks=ALL_PEERS):
    nt = len(srcs)
    zones = [pltpu.HBM((N_DEV,) + (s.shape[1:] if per_peer else s.shape), s.dtype) for s in srcs]

    def body(*refs):
        src_refs, (send_sems, recv_sems) = refs[:nt], refs[nt:nt + 2]
        land_refs, token = refs[2 * nt + 2:3 * nt + 2], refs[3 * nt + 2]
        remote, local = _spread_copies(src_refs, land_refs, send_sems, recv_sems, per_peer, masks, arriving=False)
        for cp in remote + local:
            cp.start()
        token[...] = jnp.zeros_like(token)

    return pl.pallas_call(
        body, name=name, in_specs=(HBM_SPEC,) * nt,
        out_shape=(pltpu.SemaphoreType.DMA((nt * len(masks) + nt,)), pltpu.SemaphoreType.DMA((nt * len(masks),)),
                   *[pltpu.HBM(s.shape, s.dtype) for s in srcs], *zones, jax.ShapeDtypeStruct((8, 128), F32)),
        out_specs=(SEM_SPEC, SEM_SPEC) + (HBM_SPEC,) * (2 * nt) + (pl.BlockSpec(memory_space=pltpu.VMEM),),
        input_output_aliases={t: 2 + t for t in range(nt)},
        compiler_params=pltpu.CompilerParams(has_side_effects=DATAFLOW_EFFECT),
    )(*[pltpu.with_memory_space_constraint(s, pltpu.HBM) for s in srcs])


def spread_wait(started, after, *, per_peer, name, masks=ALL_PEERS):
    nt = (len(started) - 3) // 2
    send_sems, recv_sems = started[0:2]
    srcs_thru, lands_thru = started[2:2 + nt], started[2 + nt:2 + 2 * nt]

    def body(*refs):
        src_refs, land_refs = refs[:nt], refs[nt:2 * nt]
        send_sems, recv_sems = refs[2 * nt:2 * nt + 2]
        remote, local = _spread_copies(src_refs, land_refs, send_sems, recv_sems, per_peer, masks, arriving=True)
        for cp in remote:
            cp.wait_send()
            cp.wait_recv()
        for cp in local:
            cp.wait()

    outs = pl.pallas_call(
        body, name=name, in_specs=(HBM_SPEC,) * (2 * nt) + (SEM_SPEC, SEM_SPEC, pl.BlockSpec(memory_space=pl.ANY)),
        out_shape=tuple(pltpu.HBM(a.shape, a.dtype) for a in (*srcs_thru, *lands_thru)),
        out_specs=(HBM_SPEC,) * (2 * nt), input_output_aliases={t: t for t in range(2 * nt)},
        compiler_params=pltpu.CompilerParams(has_side_effects=DATAFLOW_EFFECT),
    )(*srcs_thru, *lands_thru, send_sems, recv_sems, after)
    return list(outs[nt:])


def _relay_copies(land_refs, send_sems, recv_sems, arriving):
    me = _mesh_place()
    sibling = _flip(me, 1)
    out = []
    for t, land_ref in enumerate(land_refs):
        for i, mask in enumerate(OTHER_CHIPS):
            slot = _dev_index(_flip(sibling if arriving else me, mask))
            k = t * len(OTHER_CHIPS) + i
            out.append(pltpu.make_async_remote_copy(
                src_ref=land_ref.at[slot], dst_ref=land_ref.at[slot], send_sem=send_sems.at[k], recv_sem=recv_sems.at[k],
                device_id=sibling, device_id_type=pl.DeviceIdType.MESH))
    return out


def relay_start(lands, *, name):
    nt = len(lands)
    n_sem = nt * len(OTHER_CHIPS)

    def body(*refs):
        for cp in _relay_copies(refs[:nt], refs[nt], refs[nt + 1], arriving=False):
            cp.start()

    return pl.pallas_call(
        body, name=name, in_specs=(HBM_SPEC,) * nt,
        out_shape=(pltpu.SemaphoreType.DMA((n_sem,)), pltpu.SemaphoreType.DMA((n_sem,)),
                   *[pltpu.HBM(a.shape, a.dtype) for a in lands]),
        out_specs=(SEM_SPEC, SEM_SPEC) + (HBM_SPEC,) * nt, input_output_aliases={t: 2 + t for t in range(nt)},
        compiler_params=pltpu.CompilerParams(has_side_effects=DATAFLOW_EFFECT),
    )(*[pltpu.with_memory_space_constraint(a, pltpu.HBM) for a in lands])


def relay_wait(started, *, name):
    send_sems, recv_sems = started[0:2]
    lands_thru = started[2:]
    nt = len(lands_thru)

    def body(*refs):
        for cp in _relay_copies(refs[:nt], refs[nt], refs[nt + 1], arriving=True):
            cp.wait_send()
            cp.wait_recv()

    return list(pl.pallas_call(
        body, name=name, in_specs=(HBM_SPEC,) * nt + (SEM_SPEC, SEM_SPEC),
        out_shape=tuple(pltpu.HBM(a.shape, a.dtype) for a in lands_thru), out_specs=(HBM_SPEC,) * nt,
        input_output_aliases={t: t for t in range(nt)},
        compiler_params=pltpu.CompilerParams(has_side_effects=DATAFLOW_EFFECT),
    )(*lands_thru, send_sems, recv_sems))


def sum_partials(parts, *, name, tr):
    _, rows, lanes = parts.shape
    assert rows % tr == 0

    def body(p_ref, g_ref):
        g = p_ref[0].astype(F32)
        for k in range(1, N_DEV):
            g = g + p_ref[k].astype(F32)
        g_ref[...] = g

    return _pcall(
        body, name=name, grid=(rows // tr,),
        in_specs=[pl.BlockSpec((N_DEV, tr, lanes), lambda i: (0, i, 0))],
        out_specs=pl.BlockSpec((tr, lanes), lambda i: (i, 0)), out_shape=jax.ShapeDtypeStruct((rows, lanes), F32),
        blocks=[((N_DEV, tr, lanes), parts.dtype), ((tr, lanes), F32)],
    )(parts)


def adamw(g, w, m, v, *, name, tr):
    nl, k, n = w.shape
    tr = max(c for c in range(8, min(tr, k) + 1, 8) if k % c == 0)
    bc1 = 1.0 - ADAM_B1 ** ADAM_STEP
    bc2 = 1.0 - ADAM_B2 ** ADAM_STEP

    def body(g_ref, w_ref, m_ref, v_ref, d_ref, mo_ref, vo_ref):
        gv = g_ref[...]
        m_new = ADAM_B1 * m_ref[...] + (1.0 - ADAM_B1) * gv
        v_new = ADAM_B2 * v_ref[...] + (1.0 - ADAM_B2) * (gv * gv)
        mo_ref[...] = m_new
        vo_ref[...] = v_new
        d_ref[...] = -ADAM_LR * ((m_new / bc1) / (jnp.sqrt(v_new / bc2) + ADAM_EPS) + ADAM_WD * w_ref[...])

    spec = pl.BlockSpec((1, tr, n), lambda l, i: (l, i, 0))
    shape = jax.ShapeDtypeStruct(w.shape, F32)
    return _pcall(
        body, name=name, grid=(nl, k // tr), in_specs=[spec] * 4, out_specs=[spec] * 3, out_shape=[shape] * 3,
        blocks=[((1, tr, n), F32)] * 7,
    )(g, w, m, v)


def sum_adamw(partials, w, m, v, *, name, tr, transposed=False):
    nl, k, n = w.shape
    assert nl == len(partials) == 2
    step = 128 if transposed else 8
    tr = max(c for c in range(step, min(tr, k) + 1, step) if k % c == 0)
    bc1 = 1.0 - ADAM_B1 ** ADAM_STEP
    bc2 = 1.0 - ADAM_B2 ** ADAM_STEP

    def body(p0_ref, p1_ref, w_ref, m_ref, v_ref, g_ref, d_ref, mo_ref, vo_ref):
        first = pl.program_id(0) == 0
        gv = jnp.where(first, p0_ref[0], p1_ref[0]).astype(F32)
        for s in range(1, N_DEV):
            gv = gv + jnp.where(first, p0_ref[s], p1_ref[s]).astype(F32)
        if transposed:
            gv = gv.T
        m_new = ADAM_B1 * m_ref[0] + (1.0 - ADAM_B1) * gv
        v_new = ADAM_B2 * v_ref[0] + (1.0 - ADAM_B2) * (gv * gv)
        g_ref[0] = gv
        mo_ref[0] = m_new
        vo_ref[0] = v_new
        d_ref[0] = -ADAM_LR * ((m_new / bc1) / (jnp.sqrt(v_new / bc2) + ADAM_EPS) + ADAM_WD * w_ref[0])

    spec = pl.BlockSpec((1, tr, n), lambda l, i: (l, i, 0))
    if transposed:
        p0spec = pl.BlockSpec((N_DEV, n, tr), lambda l, i: (0, 0, i * (1 - l)))
        p1spec = pl.BlockSpec((N_DEV, n, tr), lambda l, i: (0, 0, i * l))
    else:
        p0spec = pl.BlockSpec((N_DEV, tr, n), lambda l, i: (0, i * (1 - l), 0))
        p1spec = pl.BlockSpec((N_DEV, tr, n), lambda l, i: (0, i * l, 0))
    shape = jax.ShapeDtypeStruct(w.shape, F32)
    return _pcall(
        body, name=name, grid=(nl, k // tr), in_specs=[p0spec, p1spec, spec, spec, spec], out_specs=[spec] * 4,
        out_shape=[shape] * 4, blocks=[((N_DEV, tr, n), BF16)] * 2 + [((1, tr, n), F32)] * 7,
    )(partials[0], partials[1], w, m, v)


def travelling(a, by_cols):
    return jnp.swapaxes(a, -1, -2) if by_cols else a


def _row(v):
    return v.reshape(1, -1)


def ffn_fwd(x, h, w, pre, tag, next_gain):
    ga, gb, s = swiglu_fwd(h, w[pre + "_w_gate"], w[pre + "_w_up"], name=f"{tag}_gateup")
    if callable(w[pre + "_w_down"]):
        w[pre + "_w_down"] = w[pre + "_w_down"](s)
    out, h_next = matmul_res_norm(s, w[pre + "_w_down"], x, next_gain, scale=0.5, tm=512, name=f"{tag}_down")
    return out, h_next, (x, h, ga, gb, s)


def ffn_bwd_weights(dxb, saved, w, pre, tag, ship=None):
    x, h, a, b, s = saved
    token = None
    grads = {}

    def made(name, g):
        grads[name] = g
        return ship(name, g) if ship else None

    da, db = swiglu_bwd(dxb, w[pre + "_w_down"], a, b, scale=0.5, name=f"{tag}_dgateup")
    token = made(pre + "_w_down", matmul(s, dxb, "tn", tm=1408, tn=1024, tk=2048, out_dtype=BF16, scale=0.5, name=f"{tag}_gdown"))
    token = made(pre + "_w_gate", matmul(da, h, "tn", tm=1408, tn=1024, tk=2048, out_dtype=BF16, after=token, name=f"{tag}_ggate"))
    token = made(pre + "_w_up", matmul(db, h, "tn", tm=1408, tn=1024, tk=2048, out_dtype=BF16, after=token, name=f"{tag}_gup"))
    return grads, (da, db), token


def ffn_bwd_input(dx, rest, saved, gain, w, pre, tag):
    da, db = rest
    x = saved[0]
    return matmul_rms_bwd([(da, w[pre + "_w_gate"]), (db, w[pre + "_w_up"])], x, gain, dx, tm=256, name=f"{tag}_dh")


def mixer_fwd(x, h, w, tables, tag, next_gain):
    proj = matmul(h, w["w_in"], "nt", tm=512, tn=1280, tk=1024, out_dtype=BF16, name=f"{tag}_in")
    qks, vs = rope_split(proj, tables, name=f"{tag}_rope")
    outs, lses = [], []
    for g in range(N_DIL_GROUPS):
        o, lse = dil_fwd(qks[g], vs[g], name=f"{tag}_dil{g}")
        outs.append(o)
        lses.append(lse)
    odil, lse = dil_merge(outs, lses, name=f"{tag}_merge")
    osb = sb_fwd(proj, name=f"{tag}_sb")
    for n in ("w_proj_dil", "w_proj_sb", "w_out"):
        if callable(w[n]):
            w[n] = w[n](osb)
    y, u1, u2 = gate_fwd(odil, osb, w["w_proj_dil"], w["w_proj_sb"], proj, name=f"{tag}_gate")
    out, h_next = matmul_res_norm(y, w["w_out"], x, next_gain, scale=1.0, tm=512, name=f"{tag}_out")
    return out, h_next, (x, h, proj, qks, vs, odil, lse, osb, u1, u2, y)


def mixer_bwd_weights(dxb, saved, w, tables, tag):
    x, h, proj, qks, vs, odil, lse, osb, u1, u2, y = saved
    t = x.shape[0]
    g_out = matmul(y, dxb, "tn", tm=1024, tn=1024, tk=2048, out_dtype=BF16, name=f"{tag}_gout")
    du1, du2, dgate = gate_bwd(dxb, w["w_out"], u1, u2, proj, name=f"{tag}_dgate")
    g_pd = matmul(du1, odil, "tn", tm=1024, tn=256, tk=2048, out_dtype=BF16, name=f"{tag}_gpd")
    g_ps = matmul(du2, osb, "tn", tm=1024, tn=256, tk=2048, out_dtype=BF16, name=f"{tag}_gps")
    dodil = matmul(du1, w["w_proj_dil"], "nn", tm=512, tn=256, tk=1024, out_dtype=F32, name=f"{tag}_dodil")
    dosb = matmul(du2, w["w_proj_sb"], "nn", tm=512, tn=256, tk=1024, out_dtype=F32, name=f"{tag}_dosb")
    dsum, do_wide, lse_wide, dsum_wide = dil_bwd_prep(dodil, odil, lse, name=f"{tag}_dprep")
    dos = [dodil[None]] + list(do_wide)
    lss = [lse[None]] + list(lse_wide)
    dss = [dsum[None]] + list(dsum_wide)
    dqs, dks, dvs = [], [], []
    for g in range(N_DIL_GROUPS):
        dq, dk, dv = dil_bwd(qks[g], vs[g], dos[g], lss[g], dss[g], name=f"{tag}_ddil{g}")
        dqs.append(dq)
        dks.append(dk)
        dvs.append(dv)
    gtot = head_sums(dosb, osb, name=f"{tag}_gsum")
    sb_grads = sb_bwd(proj, dosb, gtot, name=f"{tag}_dsb")
    dproj = rope_join(dqs, dks, dvs, sb_grads, dgate, tables, name=f"{tag}_drope")
    g_in = matmul(dproj, h, "tn", tm=1280, tn=1024, tk=2048, out_dtype=BF16, name=f"{tag}_gin")
    return {"w_in": g_in, "w_proj_dil": g_pd, "w_proj_sb": g_ps, "w_out": g_out}, dproj


def mixer_bwd_input(dx, dproj, saved, gain, w, tag):
    x = saved[0]
    return matmul_rms_bwd([(dproj, w["w_in"])], x, gain, dx, tm=256, name=f"{tag}_dh")


def kernel(x, norm_ffn1, ffn1_w_gate, ffn1_w_up, ffn1_w_down, norm_mix, w_in, w_proj_dil, w_proj_sb, w_out, norm_ffn2, ffn2_w_gate, ffn2_w_up, ffn2_w_down, norm_final, loss_target, m_norm_ffn1, m_ffn1_w_gate, m_ffn1_w_up, m_ffn1_w_down, m_norm_mix, m_w_in, m_w_proj_dil, m_w_proj_sb, m_w_out, m_norm_ffn2, m_ffn2_w_gate, m_ffn2_w_up, m_ffn2_w_down, m_norm_final, v_norm_ffn1, v_ffn1_w_gate, v_ffn1_w_up, v_ffn1_w_down, v_norm_mix, v_w_in, v_w_proj_dil, v_w_proj_sb, v_w_out, v_norm_ffn2, v_ffn2_w_gate, v_ffn2_w_up, v_ffn2_w_down, v_norm_final):
    args = dict(locals())
    t = x.shape[1]
    xs = x.reshape(t, D_MODEL)
    target = loss_target.reshape(t, D_MODEL)
    tables = rope_tables(t)

    parts = [(l, p) for l in range(2) for p in SUBBLOCKS]
    gains = {n: args[n] for n in NORM_ROWS}

    shipments = []
    for l, p in parts:
        if (l, p) == parts[0]:
            shipments += [(l, p, SUBBLOCKS[p][:2], CHIP_PEERS), (l, p, SUBBLOCKS[p][2:], ALL_PEERS)]
        elif (l, p) == parts[1]:
            shipments += [(l, p, SUBBLOCKS[p][:1], CHIP_PEERS), (l, p, SUBBLOCKS[p][1:], ALL_PEERS)]
        else:
            shipments.append((l, p, SUBBLOCKS[p], ALL_PEERS))
    in_flight, order_token = [], jnp.zeros((1, 1), F32)
    for l, p, tensors, masks in shipments:
        shards = [travelling(args[n][l], by_cols).astype(BF16) for n, by_cols in tensors]
        shards[0] = shards[0] + order_token.astype(BF16)
        in_flight.append(spread_start(shards, per_peer=False, masks=masks, name=f"gather_start_l{l}_{tensors[0][0]}"))
        order_token = in_flight[-1][-1][0:1, 0:1]

    landed = {}

    def arrived(i, after):
        if i not in landed:
            landed[i] = wait_for(i, after)
        return landed[i]

    def wait_for(i, after):
        l, p, tensors, masks = shipments[i]
        tag = f"l{l}_{tensors[0][0]}"
        lands = spread_wait(in_flight[i], after, per_peer=False, masks=masks, name=f"gather_wait_{tag}")
        if masks is CHIP_PEERS:
            lands = relay_wait(relay_start(lands, name=f"gather_relay_{tag}"), name=f"gather_relayed_{tag}")
        return {n: land.reshape(-1, land.shape[-1]) for (n, _), land in zip(tensors, lands)}

    def weights_of(l, p, after):
        mine = [i for i, s in enumerate(shipments) if s[0:2] == (l, p)]
        w = arrived(mine[0], after)
        for i in mine[1:]:
            for n, _ in shipments[i][2]:
                w[n] = functools.partial(lambda after, i, n: arrived(i, after)[n], i=i, n=n)
        return w

    saved, weights = {}, {}
    act = xs
    h = rms_fwd(xs, _row(gains["norm_ffn1"][0]) + order_token, name="l0_ffn1_norm")
    for i, (l, p) in enumerate(parts):
        weights[(l, p)] = weights_of(l, p, h if i == 0 else act)
        nl, np_ = parts[i + 1] if i + 1 < len(parts) else (None, None)
        next_gain = _row(gains["norm_" + np_][nl]) if np_ else None
        if p == "mix":
            act, h, saved[(l, p)] = mixer_fwd(act, h, weights[(l, p)], tables, f"l{l}_mix", next_gain)
        else:
            act, h, saved[(l, p)] = ffn_fwd(act, h, weights[(l, p)], p, f"l{l}_{p}", next_gain)
    dx, dxb, g_final, loss_part = final_loss(act, _row(norm_final), target, name="loss_head")

    gain_grads, sent, sent_last = {}, {}, {}
    per_device = lambda g: g.reshape(N_DEV, -1, g.shape[-1])

    def ship_last(name, g):
        sent_last[name] = spread_start([per_device(g)], per_peer=True, name=f"reduce_start_{name}")
        return sent_last[name][-1]

    for l, p in reversed(parts):
        w, sv = weights[(l, p)], saved[(l, p)]
        if p == "mix":
            gw, rest = mixer_bwd_weights(dxb, sv, w, tables, f"l{l}_mix")
        elif (l, p) == parts[0]:
            gw, rest, token = ffn_bwd_weights(dxb, sv, w, p, f"l{l}_{p}", ship=ship_last)
        else:
            gw, rest, _ = ffn_bwd_weights(dxb, sv, w, p, f"l{l}_{p}")
        if (l, p) != parts[0]:
            sent[(l, p)] = spread_start([per_device(gw[n]) for n, _ in SUBBLOCKS[p]], per_peer=True, name=f"reduce_start_l{l}_{p}")
            token = sent[(l, p)][-1]
        gain = _row(gains["norm_" + p][l]) + token[0:1, 0:1]
        if p == "mix":
            dx, dxb, gain_grads[("norm_mix", l)] = mixer_bwd_input(dx, rest, sv, gain, w, f"l{l}_mix")
        else:
            dx, dxb, gain_grads[("norm_" + p, l)] = ffn_bwd_input(dx, rest, sv, gain, w, p, f"l{l}_{p}")

    partials, big_all = {}, [{}, {}, {}, {}]

    def receive(l, p, after):
        if (l, p) == parts[0]:
            for n, started in sent_last.items():
                partials.setdefault(n, [None, None])[l] = spread_wait(started, after, per_peer=True, name=f"reduce_wait_{n}")[0]
            return
        lands = spread_wait(sent[(l, p)], after, per_peer=True, name=f"reduce_wait_l{l}_{p}")
        for (n, _), land in zip(SUBBLOCKS[p], lands):
            partials.setdefault(n, [None, None])[l] = land

    def update(p):
        for n, by_cols in SUBBLOCKS[p]:
            if by_cols and args[n].shape[-1] % 128 == 0:
                outs = sum_adamw(partials[n], args[n], args["m_" + n], args["v_" + n], tr=256, transposed=True,
                                 name=f"update_{n}")
                for kind, arr in enumerate(outs):
                    big_all[kind][n] = arr
            else:
                outs = sum_adamw(partials[n], travelling(args[n], by_cols), travelling(args["m_" + n], by_cols),
                                 travelling(args["v_" + n], by_cols), tr=256, name=f"update_{n}")
                for kind, arr in enumerate(outs):
                    big_all[kind][n] = travelling(arr, by_cols)
        return outs[1]

    for l, p in reversed(parts[1:]):
        receive(l, p, dx)
    update("ffn2")
    done = update("mix")
    receive(*parts[0], done)
    done = update("ffn1")

    loss_row = jnp.pad(loss_part[:, :1], ((0, 0), (0, D_MODEL - 1)))
    small = jnp.concatenate([gain_grads[(n, l)] for n in NORM_ROWS for l in range(2)] + [g_final, loss_row], axis=0)
    small_g = sum_partials(all_gather_rows(small, done, name="gather_gain_grads"), tr=8, name="sum_gain_grads")
    zero_row = jnp.zeros((1, D_MODEL), F32)
    small_of = lambda pre: jnp.concatenate([args[pre + n] for n in NORM_ROWS] + [_row(args[pre + "norm_final"]), zero_row], axis=0)[None]
    small_out = adamw(small_g[None], small_of(""), small_of("m_"), small_of("v_"), tr=8, name="update_gains")
    small_all = [small_g] + [o[0] for o in small_out]

    def gains_of(s):
        out = {n: s[2 * i:2 * i + 2] for i, n in enumerate(NORM_ROWS)}
        out["norm_final"] = s[6]
        return out

    order = ["norm_ffn1", "ffn1_w_gate", "ffn1_w_up", "ffn1_w_down", "norm_mix", "w_in", "w_proj_dil", "w_proj_sb", "w_out",
             "norm_ffn2", "ffn2_w_gate", "ffn2_w_up", "ffn2_w_down", "norm_final"]
    results = []
    for kind in range(4):
        both = {**big_all[kind], **gains_of(small_all[kind])}
        results += [both[n] for n in order]
    loss = small_g[7, 0]
    return (loss, dx.reshape(1, t, D_MODEL), *results)
```

```python
import functools

import jax
import jax.numpy as jnp
from jax import lax
from jax.experimental import pallas as pl
from jax.experimental.pallas import tpu as pltpu

F32 = jnp.float32
BF16 = jnp.bfloat16

D_MODEL = 1024
HEAD_DIM = 64
GROUP_W = 256
D_IN = 5120
N_DIL_GROUPS = 3
DIL_SPAN = 128
DILATIONS = (1, 4, 16)
ROPE_THETA = 500000.0
ROPE_DIM = 16
RMS_EPS = 1e-6
ATT_SCALE = HEAD_DIM ** -0.5
QS_BLK, KS_BLK, VS_BLK = 9, 10, 11
GATE_DIL_BLK, GATE_SB_BLK = 3, 4

ADAM_LR, ADAM_B1, ADAM_B2, ADAM_EPS, ADAM_WD, ADAM_STEP = 0.001, 0.9, 0.999, 1e-08, 0.01, 10

N_DEV = 8
VMEM_PHYSICAL_V7X = 64 << 20
VMEM_TEMP_HEADROOM = 20 << 20

SUBBLOCKS = {
    "ffn1": (("ffn1_w_gate", True), ("ffn1_w_up", True), ("ffn1_w_down", False)),
    "mix": (("w_in", True), ("w_proj_dil", True), ("w_proj_sb", True), ("w_out", False)),
    "ffn2": (("ffn2_w_gate", True), ("ffn2_w_up", True), ("ffn2_w_down", False)),
}
NORM_ROWS = ("norm_ffn1", "norm_mix", "norm_ffn2")


def _nbytes(shape, dtype):
    n = 1
    for s in shape:
        n *= s
    return n * jnp.dtype(dtype).itemsize


def _pcall(body, *, name, grid, in_specs, out_specs, out_shape, blocks, scratch_shapes=(), scratch_bytes=0):
    need = 2 * sum(_nbytes(s, d) for s, d in blocks) + scratch_bytes + VMEM_TEMP_HEADROOM
    limit = min(need, VMEM_PHYSICAL_V7X - (4 << 20))
    in_hbm = lambda s: pltpu.HBM(s.shape, s.dtype)
    out_shape = [in_hbm(s) for s in out_shape] if isinstance(out_shape, (list, tuple)) else in_hbm(out_shape)
    call = pl.pallas_call(
        body, name=name, grid=grid, in_specs=in_specs, out_specs=out_specs, out_shape=out_shape,
        scratch_shapes=scratch_shapes,
        compiler_params=pltpu.CompilerParams(vmem_limit_bytes=limit),
    )
    return lambda *args: call(*[pltpu.with_memory_space_constraint(a, pltpu.HBM) for a in args])


def _dot(a, b, form):
    dn = {"nn": (((1,), (0,)), ((), ())), "nt": (((1,), (1,)), ((), ())), "tn": (((0,), (0,)), ((), ()))}[form]
    return lax.dot_general(a.astype(BF16), b.astype(BF16), dn, preferred_element_type=F32)


def _sigmoid(x):
    return 1.0 / (1.0 + jnp.exp(-x))


def matmul(a, b, form, *, tm, tn, tk, out_dtype, name, scale=1.0, after=None):
    if form == "tn":
        kdim, m = a.shape
        n = b.shape[1]
    else:
        m, kdim = a.shape
        n = b.shape[1] if form == "nn" else b.shape[0]
    tm, tn, tk = min(tm, m), min(tn, n), min(tk, kdim)
    assert m % tm == 0 and n % tn == 0 and kdim % tk == 0, (name, m, n, kdim, tm, tn, tk)
    nk = kdim // tk

    if form == "tn":
        a_blk, a_map = (tk, tm), (lambda j, i, k: (k, i))
    else:
        a_blk, a_map = (tm, tk), (lambda j, i, k: (i, k))
    if form == "nt":
        b_blk, b_map = (tn, tk), (lambda j, i, k: (j, k))
    else:
        b_blk, b_map = (tk, tn), (lambda j, i, k: (k, j))
    o_map = lambda j, i, k: (i, j)

    def body(a_ref, b_ref, *rest):
        o_ref, acc = (rest[1], rest[2:]) if after is not None else (rest[0], rest[1:])

        def finish(total):
            o_ref[...] = (total * scale if scale != 1.0 else total).astype(out_dtype)

        if nk == 1:
            finish(_dot(a_ref[...], b_ref[...], form))
        else:
            acc_ref, = acc
            k = pl.program_id(2)

            @pl.when(k == 0)
            def _():
                acc_ref[...] = _dot(a_ref[...], b_ref[...], form)

            @pl.when(k > 0)
            def _():
                acc_ref[...] += _dot(a_ref[...], b_ref[...], form)

            @pl.when(k == nk - 1)
            def _():
                finish(acc_ref[...])

    scratch = [pltpu.VMEM((tm, tn), F32)] if nk > 1 else []
    in_specs = [pl.BlockSpec(a_blk, a_map), pl.BlockSpec(b_blk, b_map)]
    args = [a, b]
    if after is not None:
        in_specs.append(pl.BlockSpec(memory_space=pl.ANY))
        args.append(after)
    return _pcall(
        body, name=name, grid=(n // tn, m // tm, nk), in_specs=in_specs,
        out_specs=pl.BlockSpec((tm, tn), o_map), out_shape=jax.ShapeDtypeStruct((m, n), out_dtype),
        blocks=[(a_blk, a.dtype), (b_blk, b.dtype), ((tm, tn), out_dtype)],
        scratch_shapes=scratch, scratch_bytes=(tm * tn * 4 if nk > 1 else 0),
    )(*args)


def swiglu_fwd(h, wg_t, wu_t, *, name, tm=1024, tn=1408):
    t, d = h.shape
    f = wg_t.shape[0]
    tm, tn = min(tm, t), min(tn, f)

    def body(h_ref, wg_ref, wu_ref, ga_ref, gb_ref, s_ref):
        hh = h_ref[...]
        a = _dot(hh, wg_ref[...], "nt")
        b = _dot(hh, wu_ref[...], "nt")
        sg = _sigmoid(a)
        silu = a * sg
        ga_ref[...] = (b * (sg * (1.0 + a * (1.0 - sg)))).astype(BF16)
        gb_ref[...] = silu.astype(BF16)
        s_ref[...] = (silu * b).astype(BF16)

    w_spec = pl.BlockSpec((tn, d), lambda j, i: (j, 0))
    o_spec = pl.BlockSpec((tm, tn), lambda j, i: (i, j))
    o_shape = jax.ShapeDtypeStruct((t, f), BF16)
    return _pcall(
        body, name=name, grid=(f // tn, t // tm),
        in_specs=[pl.BlockSpec((tm, d), lambda j, i: (i, 0)), w_spec, w_spec],
        out_specs=[o_spec, o_spec, o_spec], out_shape=[o_shape, o_shape, o_shape],
        blocks=[((tm, d), BF16), ((tn, d), BF16), ((tn, d), BF16)] + [((tm, tn), BF16)] * 3,
    )(h, wg_t, wu_t)


def swiglu_bwd(dyb, wd, ga, gb, *, name, scale, tm=1024, tn=1408):
    t, d = dyb.shape
    f = wd.shape[0]
    tm, tn = min(tm, t), min(tn, f)

    def body(dy_ref, wd_ref, ga_ref, gb_ref, da_ref, db_ref):
        ds = _dot(dy_ref[...], wd_ref[...], "nt") * scale
        da_ref[...] = (ds * ga_ref[...].astype(F32)).astype(BF16)
        db_ref[...] = (ds * gb_ref[...].astype(F32)).astype(BF16)

    o_spec = pl.BlockSpec((tm, tn), lambda j, i: (i, j))
    o_shape = jax.ShapeDtypeStruct((t, f), BF16)
    return _pcall(
        body, name=name, grid=(f // tn, t // tm),
        in_specs=[pl.BlockSpec((tm, d), lambda j, i: (i, 0)), pl.BlockSpec((tn, d), lambda j, i: (j, 0)), o_spec, o_spec],
        out_specs=[o_spec, o_spec], out_shape=[o_shape, o_shape],
        blocks=[((tm, d), BF16), ((tn, d), BF16)] + [((tm, tn), BF16)] * 4,
    )(dyb, wd, ga, gb)


def gate_fwd(odil, osb, wpd_t, wps_t, proj, *, name, tm=1024):
    t = odil.shape[0]
    tm = min(tm, t)

    def body(od_ref, os_ref, wpd_ref, wps_ref, g1_ref, g2_ref, y_ref, u1_ref, u2_ref):
        u1 = _dot(od_ref[...], wpd_ref[...], "nt")
        u2 = _dot(os_ref[...], wps_ref[...], "nt")
        y = _sigmoid(g1_ref[...].astype(F32)) * u1 + _sigmoid(g2_ref[...].astype(F32)) * u2
        y_ref[...] = y.astype(BF16)
        u1_ref[...] = u1.astype(BF16)
        u2_ref[...] = u2.astype(BF16)

    o_spec = pl.BlockSpec((tm, D_MODEL), lambda i: (i, 0))
    w_spec = pl.BlockSpec((D_MODEL, GROUP_W), lambda i: (0, 0))
    a_spec = pl.BlockSpec((tm, GROUP_W), lambda i: (i, 0))
    o_shape = jax.ShapeDtypeStruct((t, D_MODEL), BF16)
    return _pcall(
        body, name=name, grid=(t // tm,),
        in_specs=[a_spec, a_spec, w_spec, w_spec,
                  pl.BlockSpec((tm, D_MODEL), lambda i: (i, GATE_DIL_BLK)),
                  pl.BlockSpec((tm, D_MODEL), lambda i: (i, GATE_SB_BLK))],
        out_specs=[o_spec, o_spec, o_spec], out_shape=[o_shape, o_shape, o_shape],
        blocks=[((tm, GROUP_W), F32)] * 2 + [((D_MODEL, GROUP_W), BF16)] * 2 + [((tm, D_MODEL), BF16)] * 5,
    )(odil, osb, wpd_t, wps_t, proj, proj)


def gate_bwd(dxb, wout, u1, u2, proj, *, name, tm=512):
    t = dxb.shape[0]
    tm = min(tm, t)

    def body(dx_ref, w_ref, u1_ref, u2_ref, g1_ref, g2_ref, du1_ref, du2_ref, dg_ref):
        dy = _dot(dx_ref[...], w_ref[...], "nt")
        s1 = _sigmoid(g1_ref[...].astype(F32))
        s2 = _sigmoid(g2_ref[...].astype(F32))
        du1_ref[...] = (dy * s1).astype(BF16)
        du2_ref[...] = (dy * s2).astype(BF16)
        dg_ref[:, :D_MODEL] = (dy * u1_ref[...].astype(F32) * s1 * (1.0 - s1)).astype(BF16)
        dg_ref[:, D_MODEL:] = (dy * u2_ref[...].astype(F32) * s2 * (1.0 - s2)).astype(BF16)

    o_spec = pl.BlockSpec((tm, D_MODEL), lambda i: (i, 0))
    o_shape = jax.ShapeDtypeStruct((t, D_MODEL), BF16)
    return _pcall(
        body, name=name, grid=(t // tm,),
        in_specs=[o_spec, pl.BlockSpec((D_MODEL, D_MODEL), lambda i: (0, 0)), o_spec, o_spec,
                  pl.BlockSpec((tm, D_MODEL), lambda i: (i, GATE_DIL_BLK)),
                  pl.BlockSpec((tm, D_MODEL), lambda i: (i, GATE_SB_BLK))],
        out_specs=[o_spec, o_spec, pl.BlockSpec((tm, 2 * D_MODEL), lambda i: (i, 0))],
        out_shape=[o_shape, o_shape, jax.ShapeDtypeStruct((t, 2 * D_MODEL), BF16)],
        blocks=[((tm, D_MODEL), BF16)] * 9 + [((D_MODEL, D_MODEL), BF16)],
    )(dxb, wout, u1, u2, proj, proj)


def rms_fwd(x, gain, *, name, tm=512):
    t, d = x.shape
    tm = min(tm, t)

    def body(x_ref, g_ref, h_ref):
        xv = x_ref[...]
        rstd = lax.rsqrt(jnp.mean(xv * xv, axis=1, keepdims=True) + RMS_EPS)
        h_ref[...] = (xv * rstd * g_ref[...]).astype(BF16)

    return _pcall(
        body, name=name, grid=(t // tm,),
        in_specs=[pl.BlockSpec((tm, d), lambda i: (i, 0)), pl.BlockSpec((1, d), lambda i: (0, 0))],
        out_specs=pl.BlockSpec((tm, d), lambda i: (i, 0)), out_shape=jax.ShapeDtypeStruct((t, d), BF16),
        blocks=[((tm, d), F32), ((tm, d), BF16)],
    )(x, gain)


def matmul_res_norm(a, b, res, next_gain, *, scale, tm, name):
    t, k = a.shape
    d = b.shape[1]
    tm = min(tm, t)
    with_norm = next_gain is not None

    def body(a_ref, b_ref, r_ref, *rest):
        out = r_ref[...] + _dot(a_ref[...], b_ref[...], "nn") * scale
        if with_norm:
            g_ref, o_ref, h_ref = rest
            rstd = lax.rsqrt(jnp.mean(out * out, axis=1, keepdims=True) + RMS_EPS)
            h_ref[...] = (out * rstd * g_ref[...]).astype(BF16)
        else:
            o_ref, = rest
        o_ref[...] = out

    row = pl.BlockSpec((tm, d), lambda i: (i, 0))
    in_specs = [pl.BlockSpec((tm, k), lambda i: (i, 0)), pl.BlockSpec((k, d), lambda i: (0, 0)), row]
    args = [a, b, res]
    out_specs, out_shape = [row], [jax.ShapeDtypeStruct((t, d), F32)]
    if with_norm:
        in_specs.append(pl.BlockSpec((1, d), lambda i: (0, 0)))
        args.append(next_gain)
        out_specs.append(row)
        out_shape.append(jax.ShapeDtypeStruct((t, d), BF16))
    outs = _pcall(
        body, name=name, grid=(t // tm,), in_specs=in_specs, out_specs=out_specs, out_shape=out_shape,
        blocks=[((tm, k), a.dtype), ((k, d), b.dtype), ((tm, d), F32), ((tm, d), F32), ((tm, d), BF16)],
    )(*args)
    return (outs[0], outs[1]) if with_norm else (outs[0], None)


def _rms_bwd_rows(dhv, xv, g, drv):
    rstd = lax.rsqrt(jnp.mean(xv * xv, axis=1, keepdims=True) + RMS_EPS)
    xh = xv * rstd
    dxh = dhv * g
    dx = drv + rstd * (dxh - xh * jnp.mean(dxh * xh, axis=1, keepdims=True))
    return dx, jnp.sum(dhv * xh, axis=0, keepdims=True)


def matmul_rms_bwd(pairs, x, gain, dres, *, tm, name):
    t, d = x.shape
    tm = min(tm, t)
    npairs = len(pairs)

    def body(*refs):
        ab = refs[:2 * npairs]
        x_ref, g_ref, dr_ref, dx_ref, dxb_ref, dg_ref = refs[2 * npairs:]
        dh = _dot(ab[0][...], ab[1][...], "nn")
        for q in range(1, npairs):
            dh = dh + _dot(ab[2 * q][...], ab[2 * q + 1][...], "nn")
        dx, part = _rms_bwd_rows(dh, x_ref[...], g_ref[...], dr_ref[...])
        dx_ref[...] = dx
        dxb_ref[...] = dx.astype(BF16)

        @pl.when(pl.program_id(0) == 0)
        def _():
            dg_ref[...] = part

        @pl.when(pl.program_id(0) > 0)
        def _():
            dg_ref[...] += part

    in_specs, args, blocks = [], [], []
    for a, b in pairs:
        k = a.shape[1]
        in_specs += [pl.BlockSpec((tm, k), lambda i: (i, 0)), pl.BlockSpec((k, d), lambda i: (0, 0))]
        args += [a, b]
        blocks += [((tm, k), a.dtype), ((k, d), b.dtype)]
    row = pl.BlockSpec((tm, d), lambda i: (i, 0))
    vec = pl.BlockSpec((1, d), lambda i: (0, 0))
    return _pcall(
        body, name=name, grid=(t // tm,), in_specs=in_specs + [row, vec, row], out_specs=[row, row, vec],
        out_shape=[jax.ShapeDtypeStruct((t, d), F32), jax.ShapeDtypeStruct((t, d), BF16), jax.ShapeDtypeStruct((1, d), F32)],
        blocks=blocks + [((tm, d), F32)] * 3 + [((tm, d), BF16)],
    )(*args, x, gain, dres)


def final_loss(x, gain, target, *, name, tm=512):
    t, d = x.shape
    tm = min(tm, t)

    def body(x_ref, g_ref, t_ref, dx_ref, dxb_ref, dg_ref, loss_ref):
        xv = x_ref[...]
        g = g_ref[...]
        rstd = lax.rsqrt(jnp.mean(xv * xv, axis=1, keepdims=True) + RMS_EPS)
        xh = xv * rstd
        err = xh * g - t_ref[...]
        dy = err * (1.0 / d)
        dxh = dy * g
        dx = rstd * (dxh - xh * jnp.mean(dxh * xh, axis=1, keepdims=True))
        dx_ref[...] = dx
        dxb_ref[...] = dx.astype(BF16)
        part = jnp.sum(dy * xh, axis=0, keepdims=True)
        sq = jnp.sum(jnp.sum(err * err, axis=1, keepdims=True), axis=0, keepdims=True) * (0.5 / d)
        lpart = jnp.broadcast_to(sq, (1, 128))

        @pl.when(pl.program_id(0) == 0)
        def _():
            dg_ref[...] = part
            loss_ref[...] = lpart

        @pl.when(pl.program_id(0) > 0)
        def _():
            dg_ref[...] += part
            loss_ref[...] += lpart

    row = pl.BlockSpec((tm, d), lambda i: (i, 0))
    vec = pl.BlockSpec((1, d), lambda i: (0, 0))
    return _pcall(
        body, name=name, grid=(t // tm,), in_specs=[row, vec, row],
        out_specs=[row, row, vec, pl.BlockSpec((1, 128), lambda i: (0, 0))],
        out_shape=[jax.ShapeDtypeStruct((t, d), F32), jax.ShapeDtypeStruct((t, d), BF16),
                   jax.ShapeDtypeStruct((1, d), F32), jax.ShapeDtypeStruct((1, 128), F32)],
        blocks=[((tm, d), F32)] * 3 + [((tm, d), BF16)],
    )(x, gain, target)


def rope_tables(t):
    pos = jnp.arange(t, dtype=F32)
    inv_freq = ROPE_THETA ** (-jnp.arange(0, ROPE_DIM, 2, dtype=F32) / ROPE_DIM)
    ang = pos[:, None] * inv_freq[None, :]
    cos, sin = jnp.cos(ang), jnp.sin(ang)
    half = ROPE_DIM // 2
    in_head = jnp.arange(128) % HEAD_DIM
    cosw, sinw = jnp.tile(cos, (1, 128 // half)), jnp.tile(sin, (1, 128 // half))
    c = jnp.where(in_head < ROPE_DIM, cosw, 1.0)
    sa = jnp.where(in_head < half, -sinw, 0.0)
    sb = jnp.where((in_head >= half) & (in_head < ROPE_DIM), sinw, 0.0)
    return jnp.concatenate([c, sa, sb], axis=1)


def _rotate(xv, cv, sav, sbv):
    halves = []
    for half in range(2):
        x = xv[:, 128 * half:128 * (half + 1)]
        halves.append(x * cv + pltpu.roll(x, 120, 1) * sav + pltpu.roll(x, 8, 1) * sbv)
    return jnp.concatenate(halves, axis=1)


STAGE_CHUNKS = 4


def _stage(tm):
    return dict(scratch_shapes=[pltpu.VMEM((STAGE_CHUNKS, tm, 128), F32)], scratch_bytes=STAGE_CHUNKS * tm * 128 * 4)


def _split_residues(stage_ref, val, out_ref, d, col, dtype):
    rows, width = val.shape
    if d == 1:
        out_ref[0, :, col:col + width] = val.astype(dtype)
        return
    chunks = width // 128
    for c in range(chunks):
        stage_ref[c] = val[:, 128 * c:128 * (c + 1)]
    for r in range(d):
        for c in range(chunks):
            out_ref[r, :, col + 128 * c:col + 128 * (c + 1)] = stage_ref[c, pl.ds(r, rows // d, stride=d), :].astype(dtype)


def _join_residues(stage_ref, in_ref, d, col=0, width=GROUP_W):
    if d == 1:
        return in_ref[0, :, col:col + width].astype(F32)
    rows = in_ref.shape[1] * d
    chunks = width // 128
    for r in range(d):
        for c in range(chunks):
            stage_ref[c, pl.ds(r, rows // d, stride=d), :] = in_ref[r, :, col + 128 * c:col + 128 * (c + 1)].astype(F32)
    return jnp.concatenate([stage_ref[c] for c in range(chunks)], axis=1)


def rope_split(proj, tables, *, name, tm=512):
    c = sa = sb = tables
    t = tables.shape[0]
    tm = min(tm, t)

    def body(*refs):
        pieces = refs[0:9]
        c_ref, sa_ref, sb_ref = refs[9:12]
        qk_out, v_out = refs[12:15], refs[15:18]
        stage = refs[18]
        cv, sav, sbv = c_ref[...], sa_ref[...], sb_ref[...]
        for g, d in enumerate(DILATIONS):
            for kind in range(3):
                xv = pieces[3 * kind + g][...].astype(F32)
                if kind < 2:
                    _split_residues(stage, _rotate(xv, cv, sav, sbv), qk_out[g], d, GROUP_W * kind, BF16)
                else:
                    _split_residues(stage, xv, v_out[g], d, 0, BF16)

    tabs = [pl.BlockSpec((tm, 128), functools.partial(lambda i, cb: (i, cb), cb=cb)) for cb in range(3)]
    in_specs = [pl.BlockSpec((tm, GROUP_W), functools.partial(lambda i, cb: (i, cb), cb=cb)) for cb in range(9)]
    out_specs = ([pl.BlockSpec((d, tm // d, 2 * GROUP_W), lambda i: (0, i, 0)) for d in DILATIONS]
                 + [pl.BlockSpec((d, tm // d, GROUP_W), lambda i: (0, i, 0)) for d in DILATIONS])
    out_shape = ([jax.ShapeDtypeStruct((d, t // d, 2 * GROUP_W), BF16) for d in DILATIONS]
                 + [jax.ShapeDtypeStruct((d, t // d, GROUP_W), BF16) for d in DILATIONS])
    outs = _pcall(
        body, name=name, grid=(t // tm,), in_specs=in_specs + tabs, out_specs=out_specs, out_shape=out_shape,
        blocks=[((tm, GROUP_W), BF16)] * 18 + [((tm, 128), F32)] * 3,
        **_stage(tm),
    )(*([proj] * 9), c, sa, sb)
    return outs[0:3], outs[3:6]


def rope_join(dqs, dks, dvs, sb_grads, dgate, tables, *, name, tm=512):
    c = sa = sb = tables
    t = tables.shape[0]
    tm = min(tm, t)

    def body(*refs):
        pieces, sb_refs, dgate_ref = refs[0:9], refs[9:12], refs[12]
        c_ref, sa_ref, sb_ref = refs[13:16]
        o_ref, stage = refs[16], refs[17]
        cv, sav, sbv = c_ref[...], -sa_ref[...], -sb_ref[...]
        for kind in range(3):
            for g, d in enumerate(DILATIONS):
                xv = _join_residues(stage, pieces[3 * kind + g], d)
                if kind < 2:
                    xv = _rotate(xv, cv, sav, sbv)
                col = GROUP_W * (3 * kind + g)
                o_ref[:, col:col + GROUP_W] = xv.astype(BF16)
        for j in range(3):
            o_ref[:, GROUP_W * (QS_BLK + j):GROUP_W * (QS_BLK + j + 1)] = sb_refs[j][...].astype(BF16)
        o_ref[:, D_MODEL * GATE_DIL_BLK:] = dgate_ref[...]

    tabs = [pl.BlockSpec((tm, 128), functools.partial(lambda i, cb: (i, cb), cb=cb)) for cb in range(3)]
    nat = lambda w: pl.BlockSpec((tm, w), lambda i: (i, 0))
    in_specs = [pl.BlockSpec((d, tm // d, GROUP_W), lambda i: (0, i, 0)) for _ in range(3) for d in DILATIONS]
    in_specs += [nat(GROUP_W)] * 3 + [nat(2 * D_MODEL)]
    return _pcall(
        body, name=name, grid=(t // tm,), in_specs=in_specs + tabs,
        out_specs=nat(D_IN), out_shape=jax.ShapeDtypeStruct((t, D_IN), BF16),
        blocks=[((tm, GROUP_W), F32)] * 12 + [((tm, 128), F32)] * 3 + [((tm, 2 * D_MODEL), BF16), ((tm, D_IN), BF16)],
        **_stage(tm),
    )(*dqs, *dks, *dvs, *sb_grads, dgate, c, sa, sb)


def _head_mask(h):
    lane = lax.broadcasted_iota(jnp.int32, (1, GROUP_W), 1)
    return (lane // HEAD_DIM) == h


def _band_mask_before():
    ri = lax.broadcasted_iota(jnp.int32, (4 * DIL_SPAN, DIL_SPAN), 0) % DIL_SPAN
    ci = lax.broadcasted_iota(jnp.int32, (4 * DIL_SPAN, DIL_SPAN), 1)
    return ci >= ri


def dil_fwd(qk, v, *, name):
    d, nsub, _ = qk.shape
    nblk = nsub // DIL_SPAN

    def body(q_ref, kc_ref, kp_ref, vc_ref, vp_ref, o_ref, lse_ref):
        nb = pl.program_id(1)
        kk = jnp.concatenate([kp_ref[0], kc_ref[0]], axis=0)
        vv = jnp.concatenate([vp_ref[0], vc_ref[0]], axis=0)
        s = _dot(_stack_heads(q_ref[0] * ATT_SCALE), kk, "nt")
        ri = lax.broadcasted_iota(jnp.int32, s.shape, 0) % DIL_SPAN
        ci = lax.broadcasted_iota(jnp.int32, s.shape, 1)
        valid = ((ci < DIL_SPAN) & (ci >= ri) & (nb > 0)) | ((ci >= DIL_SPAN) & (ci - DIL_SPAN <= ri))
        s = jnp.where(valid, s, -jnp.inf)
        m = jnp.max(s, axis=1, keepdims=True)
        p = jnp.exp(s - m)
        den = jnp.sum(p, axis=1, keepdims=True)
        o_ref[0] = _unstack_heads(_dot(p, vv, "nn") / den, DIL_SPAN)
        lse = m + jnp.log(den)
        for h in range(4):
            lse_ref[0, :, 128 * h:128 * (h + 1)] = jnp.broadcast_to(lse[DIL_SPAN * h:DIL_SPAN * (h + 1)], (DIL_SPAN, 128))

    blk = (1, DIL_SPAN, GROUP_W)
    sblk = (1, DIL_SPAN, 512)
    prv = lambda nb: jnp.maximum(nb - 1, 0)
    return _pcall(
        body, name=name, grid=(d, nblk),
        in_specs=[pl.BlockSpec(blk, lambda r, nb: (r, nb, 0)),
                  pl.BlockSpec(blk, lambda r, nb: (r, nb, 1)),
                  pl.BlockSpec(blk, lambda r, nb: (r, prv(nb), 1)),
                  pl.BlockSpec(blk, lambda r, nb: (r, nb, 0)),
                  pl.BlockSpec(blk, lambda r, nb: (r, prv(nb), 0))],
        out_specs=[pl.BlockSpec(blk, lambda r, nb: (r, nb, 0)), pl.BlockSpec(sblk, lambda r, nb: (r, nb, 0))],
        out_shape=[jax.ShapeDtypeStruct((d, nsub, GROUP_W), F32), jax.ShapeDtypeStruct((d, nsub, 512), F32)],
        blocks=[(blk, BF16)] * 5 + [(blk, F32), (sblk, F32)],
    )(qk, qk, qk, v, v)


def dil_merge(outs, lses, *, name, tm=512):
    t = outs[0].shape[0] * outs[0].shape[1]
    tm = min(tm, t)

    def body(o0, o1, o2, l0, l1, l2, o_ref, lse_ref, stage):
        ls = [_join_residues(stage, l, d, 0, 512) for l, d in zip((l0, l1, l2), DILATIONS)]
        m = jnp.maximum(jnp.maximum(ls[0], ls[1]), ls[2])
        tot = m + jnp.log(jnp.exp(ls[0] - m) + jnp.exp(ls[1] - m) + jnp.exp(ls[2] - m))
        lse_ref[...] = tot
        lane = lax.broadcasted_iota(jnp.int32, (1, 128), 1)
        first = lane < HEAD_DIM
        acc = jnp.zeros((tm, GROUP_W), F32)
        for og, lg, d in zip((o0, o1, o2), ls, DILATIONS):
            w = jnp.exp(lg - tot)
            wide = jnp.concatenate([jnp.where(first, w[:, 0:128], w[:, 128:256]),
                                    jnp.where(first, w[:, 256:384], w[:, 384:512])], axis=1)
            acc = acc + wide * _join_residues(stage, og, d)
        o_ref[...] = acc

    o_in = [pl.BlockSpec((d, tm // d, GROUP_W), lambda i: (0, i, 0)) for d in DILATIONS]
    l_in = [pl.BlockSpec((d, tm // d, 512), lambda i: (0, i, 0)) for d in DILATIONS]
    return _pcall(
        body, name=name, grid=(t // tm,), in_specs=o_in + l_in,
        out_specs=[pl.BlockSpec((tm, GROUP_W), lambda i: (i, 0)), pl.BlockSpec((tm, 512), lambda i: (i, 0))],
        out_shape=[jax.ShapeDtypeStruct((t, GROUP_W), F32), jax.ShapeDtypeStruct((t, 512), F32)],
        blocks=[((tm, GROUP_W), F32)] * 4 + [((tm, 512), F32)] * 4,
        **_stage(tm),
    )(*outs, *lses)


def dil_bwd_prep(do, o, lse, *, name, tm=512):
    t = do.shape[0]
    tm = min(tm, t)
    wide = DILATIONS[1:]

    def body(do_ref, o_ref, lse_ref, ds_ref, *rest):
        do_out, lse_out, ds_out = rest[0:2], rest[2:4], rest[4:6]
        stage = rest[6]
        dov = do_ref[...]
        prod = dov * o_ref[...]
        for h in range(4):
            s = jnp.sum(jnp.where(_head_mask(h), prod, 0.0), axis=1, keepdims=True)
            ds_ref[:, 128 * h:128 * (h + 1)] = jnp.broadcast_to(s, (tm, 128))
        for i, d in enumerate(wide):
            _split_residues(stage, dov, do_out[i], d, 0, BF16)
            _split_residues(stage, lse_ref[...], lse_out[i], d, 0, F32)
            _split_residues(stage, ds_ref[...], ds_out[i], d, 0, F32)

    nat = lambda w: pl.BlockSpec((tm, w), lambda i: (i, 0))
    res = lambda d, w: pl.BlockSpec((d, tm // d, w), lambda i: (0, i, 0))
    shape = lambda d, w, dt: jax.ShapeDtypeStruct((d, t // d, w), dt)
    outs = _pcall(
        body, name=name, grid=(t // tm,), in_specs=[nat(GROUP_W), nat(GROUP_W), nat(512)],
        out_specs=[nat(512)] + [res(d, GROUP_W) for d in wide] + [res(d, 512) for d in wide] * 2,
        out_shape=([jax.ShapeDtypeStruct((t, 512), F32)] + [shape(d, GROUP_W, BF16) for d in wide]
                   + [shape(d, 512, F32) for d in wide] * 2),
        blocks=[((tm, GROUP_W), F32)] * 3 + [((tm, 512), F32)] * 6,
        **_stage(tm),
    )(do, o, lse)
    return outs[0], outs[1:3], outs[3:5], outs[5:7]


def head_sums(a, b, *, name, tm=512):
    t = a.shape[0]
    tm = min(tm, t)

    def body(a_ref, b_ref, o_ref):
        prod = a_ref[...].astype(BF16).astype(F32) * b_ref[...]
        for h in range(4):
            s = jnp.sum(jnp.where(_head_mask(h), prod, 0.0), axis=1, keepdims=True)
            o_ref[:, 128 * h:128 * (h + 1)] = jnp.broadcast_to(s, (tm, 128))

    spec = pl.BlockSpec((tm, GROUP_W), lambda i: (i, 0))
    return _pcall(
        body, name=name, grid=(t // tm,), in_specs=[spec, spec],
        out_specs=pl.BlockSpec((tm, 512), lambda i: (i, 0)), out_shape=jax.ShapeDtypeStruct((t, 512), F32),
        blocks=[((tm, GROUP_W), F32)] * 2 + [((tm, 512), F32)],
    )(a, b)


def dil_bwd(qk, v, do, lse, dsum, *, name):
    d, nsub, _ = qk.shape
    nblk = nsub // DIL_SPAN

    def body(qa_ref, qb_ref, kc_ref, kp_ref, vc_ref, vp_ref, doa_ref, dob_ref, la_ref, lb_ref, sa_ref, sb_ref,
             dq_ref, dk_ref, dv_ref):
        nb = pl.program_id(1)
        nxt = _band_mask_before() & (nb < nblk - 1)
        kc, kp, vc, vp = kc_ref[0], kp_ref[0], vc_ref[0], vp_ref[0]
        qas, qbs = _stack_heads(qa_ref[0] * ATT_SCALE), _stack_heads(qb_ref[0] * ATT_SCALE)
        das, dbs = _stack_heads(doa_ref[0].astype(BF16)), _stack_heads(dob_ref[0].astype(BF16))
        stat = lambda ref: jnp.concatenate([ref[0, :, 128 * h:128 * (h + 1)] for h in range(4)], axis=0)
        la, lb, sa, sb = stat(la_ref), stat(lb_ref), stat(sa_ref), stat(sb_ref)

        def probs(qs, ds_, k, v, mask, l, s):
            p = jnp.where(mask, jnp.exp(_dot(qs, k, "nt") - l), 0.0)
            dsc = p * (_dot(ds_, v, "nt") - s)
            return p.astype(BF16), dsc.astype(BF16)

        wide = lambda a: jnp.concatenate([a, a], axis=1)
        ri = lax.broadcasted_iota(jnp.int32, (4 * DIL_SPAN, 2 * DIL_SPAN), 0) % DIL_SPAN
        ci = lax.broadcasted_iota(jnp.int32, (4 * DIL_SPAN, 2 * DIL_SPAN), 1)
        valid = ((ci < DIL_SPAN) & (ci >= ri) & (nb > 0)) | ((ci >= DIL_SPAN) & (ci - DIL_SPAN <= ri))
        p_a, ds_a = probs(qas, das, jnp.concatenate([kp, kc], axis=0), jnp.concatenate([vp, vc], axis=0),
                          valid, wide(la), wide(sa))
        p_nc, ds_nc = probs(qbs, dbs, kc, vc, nxt, lb, sb)
        dq_ref[0] = _unstack_heads(_dot(ds_a, jnp.concatenate([kp, kc], axis=0), "nn"), DIL_SPAN) * ATT_SCALE
        dk_ref[0] = _dot(ds_a[:, DIL_SPAN:], qas, "tn") + _dot(ds_nc, qbs, "tn")
        dv_ref[0] = _dot(p_a[:, DIL_SPAN:], das, "tn") + _dot(p_nc, dbs, "tn")

    blk = (1, DIL_SPAN, GROUP_W)
    sblk = (1, DIL_SPAN, 512)
    prv = lambda nb: jnp.maximum(nb - 1, 0)
    nxt_ = lambda nb: jnp.minimum(nb + 1, nblk - 1)
    cur_at = lambda c: pl.BlockSpec(blk, functools.partial(lambda r, nb, c: (r, nb, c), c=c))
    prv_at = lambda c: pl.BlockSpec(blk, functools.partial(lambda r, nb, c: (r, prv(nb), c), c=c))
    nxt_at = lambda c: pl.BlockSpec(blk, functools.partial(lambda r, nb, c: (r, nxt_(nb), c), c=c))
    s_cur = pl.BlockSpec(sblk, lambda r, nb: (r, nb, 0))
    s_nxt = pl.BlockSpec(sblk, lambda r, nb: (r, nxt_(nb), 0))
    o_spec = pl.BlockSpec(blk, lambda r, nb: (r, nb, 0))
    o_shape = jax.ShapeDtypeStruct((d, nsub, GROUP_W), F32)
    return _pcall(
        body, name=name, grid=(d, nblk),
        in_specs=[cur_at(0), nxt_at(0), cur_at(1), prv_at(1), cur_at(0), prv_at(0), cur_at(0), nxt_at(0),
                  s_cur, s_nxt, s_cur, s_nxt],
        out_specs=[o_spec, o_spec, o_spec], out_shape=[o_shape, o_shape, o_shape],
        blocks=[(blk, BF16)] * 6 + [(blk, F32)] * 5 + [(sblk, F32)] * 4,
    )(qk, qk, qk, qk, v, v, do, do, lse, lse, dsum, dsum)


def _tri_dot(x, b):
    hi = x.astype(BF16)
    lo = (x - hi.astype(F32)).astype(BF16)
    return _dot(jnp.concatenate([hi, lo], axis=1), jnp.concatenate([b, b], axis=0), "nn")


SB_TILE = 256
SB_ROWS = 512


def _stack_heads(a):
    return jnp.concatenate([jnp.where(_head_mask(h), a, jnp.zeros_like(a)) for h in range(4)], axis=0)


def _unstack_heads(acc, rows):
    out = acc[0:rows]
    for h in range(1, 4):
        out = jnp.where(_head_mask(h), acc[h * rows:(h + 1) * rows], out)
    return out


def _tri_masks(n):
    ri = lax.broadcasted_iota(jnp.int32, (n, n), 0)
    ci = lax.broadcasted_iota(jnp.int32, (n, n), 1)
    return (ri > ci).astype(BF16), (ri >= ci).astype(BF16)


def _sb_weights(qs, kt, after, c_keep, lead):
    z = _dot(qs, kt, "nt")
    lbeta = jnp.minimum(z, 0.0) - jnp.log(1.0 + jnp.exp(-jnp.abs(z)))
    lkeep = lbeta - z
    past = None
    if lead is not None:
        query = lax.broadcasted_iota(jnp.int32, z.shape, 0) % SB_ROWS
        past = lax.broadcasted_iota(jnp.int32, z.shape, 1) + lead < query
        lkeep = jnp.where(past, lkeep, 0.0)
    w = jnp.exp(lbeta + _tri_dot(lkeep, after) + c_keep)
    if lead is not None:
        w = jnp.where(past, w, 0.0)
    return z, past, lbeta, lkeep, w


def _sb_walk(qb, tile, carry):
    per = SB_ROWS // SB_TILE
    for i in reversed(range(per)):
        carry = tile(pl.multiple_of(qb * SB_ROWS + i * SB_TILE, SB_TILE), i * SB_TILE, i == per - 1, carry)
    past_tiles = qb * per
    return lax.fori_loop(0, past_tiles,
                         lambda it, c: tile(pl.multiple_of((past_tiles - 1 - it) * SB_TILE, SB_TILE), None, False, c), carry)


def sb_fwd(proj, *, name):
    t = proj.shape[0]
    n, m = SB_TILE, SB_ROWS
    assert t % m == 0

    def body(q_ref, k_ref, v_ref, o_ref, acc_ref):
        qb = pl.program_id(0)
        qs = _stack_heads(q_ref[...] * ATT_SCALE)
        after, _ = _tri_masks(n)

        def tile(off, lead, first, c_keep):
            kt = k_ref[pl.ds(off, n), :]
            vt = v_ref[pl.ds(off, n), :]
            _, _, _, lkeep, w = _sb_weights(qs, kt, after, c_keep, lead)
            pv = _tri_dot(w, vt)
            if first:
                acc_ref[...] = pv
            else:
                acc_ref[...] += pv
            return c_keep + jnp.sum(lkeep, axis=1, keepdims=True)

        _sb_walk(qb, tile, jnp.zeros((4 * m, 1), F32))
        o_ref[...] = _unstack_heads(acc_ref[...], m)

    full = lambda cb: pl.BlockSpec((t, GROUP_W), functools.partial(lambda i, cb: (0, cb), cb=cb))
    return _pcall(
        body, name=name, grid=(t // m,),
        in_specs=[pl.BlockSpec((m, GROUP_W), lambda i: (i, QS_BLK)), full(KS_BLK), full(VS_BLK)],
        out_specs=pl.BlockSpec((m, GROUP_W), lambda i: (i, 0)), out_shape=jax.ShapeDtypeStruct((t, GROUP_W), F32),
        blocks=[((m, GROUP_W), BF16), ((t, GROUP_W), BF16), ((t, GROUP_W), BF16), ((m, GROUP_W), F32)],
        scratch_shapes=[pltpu.VMEM((4 * m, GROUP_W), F32)], scratch_bytes=4 * m * GROUP_W * 4,
    )(proj, proj, proj)


def sb_bwd(proj, do, gtot, *, name):
    t = proj.shape[0]
    n, m = SB_TILE, SB_ROWS
    assert t % m == 0

    def body(q_ref, k_ref, v_ref, do_ref, gt_ref, dq_ref, dk_ref, dv_ref, acc_ref):
        qb = pl.program_id(0)

        @pl.when(qb == 0)
        def _():
            dk_ref[...] = jnp.zeros_like(dk_ref)
            dv_ref[...] = jnp.zeros_like(dv_ref)

        qs = _stack_heads(q_ref[...] * ATT_SCALE)
        dos = _stack_heads(do_ref[...].astype(BF16))
        gt = jnp.concatenate([jnp.max(gt_ref[:, 128 * h:128 * (h + 1)], axis=1, keepdims=True) for h in range(4)], axis=0)
        after, from_on = _tri_masks(n)

        def tile(off, lead, first, carry):
            c_keep, c_g = carry
            kt = k_ref[pl.ds(off, n), :]
            vt = v_ref[pl.ds(off, n), :]
            z, past, lbeta, lkeep, w = _sb_weights(qs, kt, after, c_keep, lead)
            gw = w * _dot(dos, vt, "nt")
            big_g = gt - (_tri_dot(gw, from_on) + c_g)
            dz = gw * jnp.exp(lbeta - z) - big_g * jnp.exp(lbeta)
            if lead is not None:
                dz = jnp.where(past, dz, 0.0)
            dz = dz.astype(BF16)
            dk_ref[pl.ds(off, n), :] += _dot(dz, qs, "tn")
            dv_ref[pl.ds(off, n), :] += _dot(w, dos, "tn")
            dq = _dot(dz, kt, "nn")
            if first:
                acc_ref[...] = dq
            else:
                acc_ref[...] += dq
            return c_keep + jnp.sum(lkeep, axis=1, keepdims=True), c_g + jnp.sum(gw, axis=1, keepdims=True)

        zero_col = jnp.zeros((4 * m, 1), F32)
        _sb_walk(qb, tile, (zero_col, zero_col))
        dq_ref[...] = _unstack_heads(acc_ref[...], m) * ATT_SCALE

    full = lambda cb: pl.BlockSpec((t, GROUP_W), functools.partial(lambda i, cb: (0, cb), cb=cb))
    whole = pl.BlockSpec((t, GROUP_W), lambda i: (0, 0))
    rowblk = pl.BlockSpec((m, GROUP_W), lambda i: (i, 0))
    shape = jax.ShapeDtypeStruct((t, GROUP_W), F32)
    return _pcall(
        body, name=name, grid=(t // m,),
        in_specs=[pl.BlockSpec((m, GROUP_W), lambda i: (i, QS_BLK)), full(KS_BLK), full(VS_BLK), rowblk,
                  pl.BlockSpec((m, 512), lambda i: (i, 0))],
        out_specs=[rowblk, whole, whole], out_shape=[shape, shape, shape],
        blocks=[((m, GROUP_W), BF16), ((t, GROUP_W), BF16), ((t, GROUP_W), BF16), ((m, GROUP_W), F32),
                ((m, 512), F32), ((m, GROUP_W), F32), ((t, GROUP_W), F32), ((t, GROUP_W), F32)],
        scratch_shapes=[pltpu.VMEM((4 * m, GROUP_W), F32)], scratch_bytes=4 * m * GROUP_W * 4,
    )(proj, proj, proj, do, gtot)


def _mesh_place():
    return lax.axis_index("x"), lax.axis_index("y"), lax.axis_index("c")


def _flip(place, mask):
    x, y, c = place
    return ((1 - x) if mask & 4 else x, (1 - y) if mask & 2 else y, (1 - c) if mask & 1 else c)


def _dev_index(place):
    x, y, c = place
    return 4 * x + 2 * y + c


HBM_SPEC = pl.BlockSpec(memory_space=pltpu.HBM)


def all_gather_rows(shard, after, *, name):
    rows, lanes = shard.shape

    def body(x_ref, after_ref, out_ref, send_sems, recv_sems, local_sem):
        me = _mesh_place()
        x, y, c = me
        sibling = _flip(me, 1)
        chips = [_flip(me, 4), _flip(me, 2), _flip(me, 6)]

        def copy(k, block, to, src=None):
            dst = out_ref.at[_dev_index(block)]
            return pltpu.make_async_remote_copy(
                src_ref=dst if src is None else src, dst_ref=dst, send_sem=send_sems.at[k], recv_sem=recv_sems.at[k],
                device_id=to, device_id_type=pl.DeviceIdType.MESH)

        mine = pltpu.make_async_copy(x_ref, out_ref.at[_dev_index(me)], local_sem)
        mine.start()
        first = [copy(0, me, sibling, src=x_ref)] + [copy(1 + j, me, chip, src=x_ref) for j, chip in enumerate(chips)]
        for cp in first:
            cp.start()
        passed = [copy(4 + j, chip, sibling) for j, chip in enumerate(chips)]
        for j, chip in enumerate(chips):
            copy(1 + j, chip, me).wait_recv()
            passed[j].start()
        copy(0, sibling, me).wait_recv()
        for j, chip in enumerate(chips):
            copy(4 + j, _flip(chip, 1), me).wait_recv()
        for cp in first + passed:
            cp.wait_send()
        mine.wait()

    return pl.pallas_call(
        body, name=name, in_specs=[HBM_SPEC, pl.BlockSpec(memory_space=pl.ANY)], out_specs=HBM_SPEC,
        out_shape=jax.ShapeDtypeStruct((N_DEV, rows, lanes), shard.dtype),
        scratch_shapes=[pltpu.SemaphoreType.DMA((7,)), pltpu.SemaphoreType.DMA((7,)), pltpu.SemaphoreType.DMA],
    )(shard, after)


SEM_SPEC = pl.BlockSpec(memory_space=pltpu.SEMAPHORE)
DATAFLOW_EFFECT = pltpu.SideEffectType.DATAFLOW_SIDE_EFFECTING


ALL_PEERS = tuple(range(1, N_DEV))
CHIP_PEERS = (1, 4, 2, 6)
OTHER_CHIPS = (4, 2, 6)


def _spread_copies(src_refs, land_refs, send_sems, recv_sems, per_peer, masks, arriving):
    me = _mesh_place()
    my = _dev_index(me)
    remote, local = [], []
    for t, (src_ref, land_ref) in enumerate(zip(src_refs, land_refs)):
        for i, mask in enumerate(masks):
            peer = _flip(me, mask)
            data_of = my if arriving else _dev_index(peer)
            slot = _dev_index(peer) if arriving else my
            k = t * len(masks) + i
            remote.append(pltpu.make_async_remote_copy(
                src_ref=src_ref.at[data_of] if per_peer else src_ref, dst_ref=land_ref.at[slot],
                send_sem=send_sems.at[k], recv_sem=recv_sems.at[k],
                device_id=peer, device_id_type=pl.DeviceIdType.MESH))
        local.append(pltpu.make_async_copy(src_ref.at[my] if per_peer else src_ref, land_ref.at[my],
                                           send_sems.at[len(src_refs) * len(masks) + t]))
    return remote, local


def spread_start(srcs, *, per_peer, name, masks=ALL_PEERS):
    nt = len(srcs)
    zones = [pltpu.HBM((N_DEV,) + (s.shape[1:] if per_peer else s.shape), s.dtype) for s in srcs]

    def body(*refs):
        src_refs, (send_sems, recv_sems) = refs[:nt], refs[nt:nt + 2]
        land_refs, token = refs[2 * nt + 2:3 * nt + 2], refs[3 * nt + 2]
        remote, local = _spread_copies(src_refs, land_refs, send_sems, recv_sems, per_peer, masks, arriving=False)
        for cp in remote + local:
            cp.start()
        token[...] = jnp.zeros_like(token)

    return pl.pallas_call(
        body, name=name, in_specs=(HBM_SPEC,) * nt,
        out_shape=(pltpu.SemaphoreType.DMA((nt * len(masks) + nt,)), pltpu.SemaphoreType.DMA((nt * len(masks),)),
                   *[pltpu.HBM(s.shape, s.dtype) for s in srcs], *zones, jax.ShapeDtypeStruct((8, 128), F32)),
        out_specs=(SEM_SPEC, SEM_SPEC) + (HBM_SPEC,) * (2 * nt) + (pl.BlockSpec(memory_space=pltpu.VMEM),),
        input_output_aliases={t: 2 + t for t in range(nt)},
        compiler_params=pltpu.CompilerParams(has_side_effects=DATAFLOW_EFFECT),
    )(*[pltpu.with_memory_space_constraint(s, pltpu.HBM) for s in srcs])


def spread_wait(started, after, *, per_peer, name, masks=ALL_PEERS):
    nt = (len(started) - 3) // 2
    send_sems, recv_sems = started[0:2]
    srcs_thru, lands_thru = started[2:2 + nt], started[2 + nt:2 + 2 * nt]

    def body(*refs):
        src_refs, land_refs = refs[:nt], refs[nt:2 * nt]
        send_sems, recv_sems = refs[2 * nt:2 * nt + 2]
        remote, local = _spread_copies(src_refs, land_refs, send_sems, recv_sems, per_peer, masks, arriving=True)
        for cp in remote:
            cp.wait_send()
            cp.wait_recv()
        for cp in local:
            cp.wait()

    outs = pl.pallas_call(
        body, name=name, in_specs=(HBM_SPEC,) * (2 * nt) + (SEM_SPEC, SEM_SPEC, pl.BlockSpec(memory_space=pl.ANY)),
        out_shape=tuple(pltpu.HBM(a.shape, a.dtype) for a in (*srcs_thru, *lands_thru)),
        out_specs=(HBM_SPEC,) * (2 * nt), input_output_aliases={t: t for t in range(2 * nt)},
        compiler_params=pltpu.CompilerParams(has_side_effects=DATAFLOW_EFFECT),
    )(*srcs_thru, *lands_thru, send_sems, recv_sems, after)
    return list(outs[nt:])


def _relay_copies(land_refs, send_sems, recv_sems, arriving):
    me = _mesh_place()
    sibling = _flip(me, 1)
    out = []
    for t, land_ref in enumerate(land_refs):
        for i, mask in enumerate(OTHER_CHIPS):
            slot = _dev_index(_flip(sibling if arriving else me, mask))
            k = t * len(OTHER_CHIPS) + i
            out.append(pltpu.make_async_remote_copy(
                src_ref=land_ref.at[slot], dst_ref=land_ref.at[slot], send_sem=send_sems.at[k], recv_sem=recv_sems.at[k],
                device_id=sibling, device_id_type=pl.DeviceIdType.MESH))
    return out


def relay_start(lands, *, name):
    nt = len(lands)
    n_sem = nt * len(OTHER_CHIPS)

    def body(*refs):
        for cp in _relay_copies(refs[:nt], refs[nt], refs[nt + 1], arriving=False):
            cp.start()

    return pl.pallas_call(
        body, name=name, in_specs=(HBM_SPEC,) * nt,
        out_shape=(pltpu.SemaphoreType.DMA((n_sem,)), pltpu.SemaphoreType.DMA((n_sem,)),
                   *[pltpu.HBM(a.shape, a.dtype) for a in lands]),
        out_specs=(SEM_SPEC, SEM_SPEC) + (HBM_SPEC,) * nt, input_output_aliases={t: 2 + t for t in range(nt)},
        compiler_params=pltpu.CompilerParams(has_side_effects=DATAFLOW_EFFECT),
    )(*[pltpu.with_memory_space_constraint(a, pltpu.HBM) for a in lands])


def relay_wait(started, *, name):
    send_sems, recv_sems = started[0:2]
    lands_thru = started[2:]
    nt = len(lands_thru)

    def body(*refs):
        for cp in _relay_copies(refs[:nt], refs[nt], refs[nt + 1], arriving=True):
            cp.wait_send()
            cp.wait_recv()

    return list(pl.pallas_call(
        body, name=name, in_specs=(HBM_SPEC,) * nt + (SEM_SPEC, SEM_SPEC),
        out_shape=tuple(pltpu.HBM(a.shape, a.dtype) for a in lands_thru), out_specs=(HBM_SPEC,) * nt,
        input_output_aliases={t: t for t in range(nt)},
        compiler_params=pltpu.CompilerParams(has_side_effects=DATAFLOW_EFFECT),
    )(*lands_thru, send_sems, recv_sems))


def sum_partials(parts, *, name, tr):
    _, rows, lanes = parts.shape
    assert rows % tr == 0

    def body(p_ref, g_ref):
        g = p_ref[0].astype(F32)
        for k in range(1, N_DEV):
            g = g + p_ref[k].astype(F32)
        g_ref[...] = g

    return _pcall(
        body, name=name, grid=(rows // tr,),
        in_specs=[pl.BlockSpec((N_DEV, tr, lanes), lambda i: (0, i, 0))],
        out_specs=pl.BlockSpec((tr, lanes), lambda i: (i, 0)), out_shape=jax.ShapeDtypeStruct((rows, lanes), F32),
        blocks=[((N_DEV, tr, lanes), parts.dtype), ((tr, lanes), F32)],
    )(parts)


def adamw(g, w, m, v, *, name, tr):
    nl, k, n = w.shape
    tr = max(c for c in range(8, min(tr, k) + 1, 8) if k % c == 0)
    bc1 = 1.0 - ADAM_B1 ** ADAM_STEP
    bc2 = 1.0 - ADAM_B2 ** ADAM_STEP

    def body(g_ref, w_ref, m_ref, v_ref, d_ref, mo_ref, vo_ref):
        gv = g_ref[...]
        m_new = ADAM_B1 * m_ref[...] + (1.0 - ADAM_B1) * gv
        v_new = ADAM_B2 * v_ref[...] + (1.0 - ADAM_B2) * (gv * gv)
        mo_ref[...] = m_new
        vo_ref[...] = v_new
        d_ref[...] = -ADAM_LR * ((m_new / bc1) / (jnp.sqrt(v_new / bc2) + ADAM_EPS) + ADAM_WD * w_ref[...])

    spec = pl.BlockSpec((1, tr, n), lambda l, i: (l, i, 0))
    shape = jax.ShapeDtypeStruct(w.shape, F32)
    return _pcall(
        body, name=name, grid=(nl, k // tr), in_specs=[spec] * 4, out_specs=[spec] * 3, out_shape=[shape] * 3,
        blocks=[((1, tr, n), F32)] * 7,
    )(g, w, m, v)


def sum_adamw(partials, w, m, v, *, name, tr, transposed=False):
    nl, k, n = w.shape
    assert nl == len(partials) == 2
    step = 128 if transposed else 8
    tr = max(c for c in range(step, min(tr, k) + 1, step) if k % c == 0)
    bc1 = 1.0 - ADAM_B1 ** ADAM_STEP
    bc2 = 1.0 - ADAM_B2 ** ADAM_STEP

    def body(p0_ref, p1_ref, w_ref, m_ref, v_ref, g_ref, d_ref, mo_ref, vo_ref):
        first = pl.program_id(0) == 0
        gv = jnp.where(first, p0_ref[0], p1_ref[0]).astype(F32)
        for s in range(1, N_DEV):
            gv = gv + jnp.where(first, p0_ref[s], p1_ref[s]).astype(F32)
        if transposed:
            gv = gv.T
        m_new = ADAM_B1 * m_ref[0] + (1.0 - ADAM_B1) * gv
        v_new = ADAM_B2 * v_ref[0] + (1.0 - ADAM_B2) * (gv * gv)
        g_ref[0] = gv
        mo_ref[0] = m_new
        vo_ref[0] = v_new
        d_ref[0] = -ADAM_LR * ((m_new / bc1) / (jnp.sqrt(v_new / bc2) + ADAM_EPS) + ADAM_WD * w_ref[0])

    spec = pl.BlockSpec((1, tr, n), lambda l, i: (l, i, 0))
    if transposed:
        p0spec = pl.BlockSpec((N_DEV, n, tr), lambda l, i: (0, 0, i * (1 - l)))
        p1spec = pl.BlockSpec((N_DEV, n, tr), lambda l, i: (0, 0, i * l))
    else:
        p0spec = pl.BlockSpec((N_DEV, tr, n), lambda l, i: (0, i * (1 - l), 0))
        p1spec = pl.BlockSpec((N_DEV, tr, n), lambda l, i: (0, i * l, 0))
    shape = jax.ShapeDtypeStruct(w.shape, F32)
    return _pcall(
        body, name=name, grid=(nl, k // tr), in_specs=[p0spec, p1spec, spec, spec, spec], out_specs=[spec] * 4,
        out_shape=[shape] * 4, blocks=[((N_DEV, tr, n), BF16)] * 2 + [((1, tr, n), F32)] * 7,
    )(partials[0], partials[1], w, m, v)


def travelling(a, by_cols):
    return jnp.swapaxes(a, -1, -2) if by_cols else a


def _row(v):
    return v.reshape(1, -1)


def ffn_fwd(x, h, w, pre, tag, next_gain):
    ga, gb, s = swiglu_fwd(h, w[pre + "_w_gate"], w[pre + "_w_up"], name=f"{tag}_gateup")
    if callable(w[pre + "_w_down"]):
        w[pre + "_w_down"] = w[pre + "_w_down"](s)
    out, h_next = matmul_res_norm(s, w[pre + "_w_down"], x, next_gain, scale=0.5, tm=512, name=f"{tag}_down")
    return out, h_next, (x, h, ga, gb, s)


def ffn_bwd_weights(dxb, saved, w, pre, tag, ship=None):
    x, h, a, b, s = saved
    token = None
    grads = {}

    def made(name, g):
        grads[name] = g
        return ship(name, g) if ship else None

    da, db = swiglu_bwd(dxb, w[pre + "_w_down"], a, b, scale=0.5, name=f"{tag}_dgateup")
    token = made(pre + "_w_down", matmul(s, dxb, "tn", tm=1408, tn=1024, tk=2048, out_dtype=BF16, scale=0.5, name=f"{tag}_gdown"))
    token = made(pre + "_w_gate", matmul(da, h, "tn", tm=1408, tn=1024, tk=2048, out_dtype=BF16, after=token, name=f"{tag}_ggate"))
    token = made(pre + "_w_up", matmul(db, h, "tn", tm=1408, tn=1024, tk=2048, out_dtype=BF16, after=token, name=f"{tag}_gup"))
    return grads, (da, db), token


def ffn_bwd_input(dx, rest, saved, gain, w, pre, tag):
    da, db = rest
    x = saved[0]
    return matmul_rms_bwd([(da, w[pre + "_w_gate"]), (db, w[pre + "_w_up"])], x, gain, dx, tm=256, name=f"{tag}_dh")


def mixer_fwd(x, h, w, tables, tag, next_gain):
    proj = matmul(h, w["w_in"], "nt", tm=1024, tn=1280, tk=1024, out_dtype=BF16, name=f"{tag}_in")
    qks, vs = rope_split(proj, tables, name=f"{tag}_rope")
    outs, lses = [], []
    for g in range(N_DIL_GROUPS):
        o, lse = dil_fwd(qks[g], vs[g], name=f"{tag}_dil{g}")
        outs.append(o)
        lses.append(lse)
    odil, lse = dil_merge(outs, lses, name=f"{tag}_merge")
    osb = sb_fwd(proj, name=f"{tag}_sb")
    for n in ("w_proj_dil", "w_proj_sb", "w_out"):
        if callable(w[n]):
            w[n] = w[n](osb)
    y, u1, u2 = gate_fwd(odil, osb, w["w_proj_dil"], w["w_proj_sb"], proj, name=f"{tag}_gate")
    out, h_next = matmul_res_norm(y, w["w_out"], x, next_gain, scale=1.0, tm=512, name=f"{tag}_out")
    return out, h_next, (x, h, proj, qks, vs, odil, lse, osb, u1, u2, y)


def mixer_bwd_weights(dxb, saved, w, tables, tag):
    x, h, proj, qks, vs, odil, lse, osb, u1, u2, y = saved
    t = x.shape[0]
    g_out = matmul(y, dxb, "tn", tm=1024, tn=1024, tk=2048, out_dtype=BF16, name=f"{tag}_gout")
    du1, du2, dgate = gate_bwd(dxb, w["w_out"], u1, u2, proj, name=f"{tag}_dgate")
    g_pd = matmul(du1, odil, "tn", tm=1024, tn=256, tk=2048, out_dtype=BF16, name=f"{tag}_gpd")
    g_ps = matmul(du2, osb, "tn", tm=1024, tn=256, tk=2048, out_dtype=BF16, name=f"{tag}_gps")
    dodil = matmul(du1, w["w_proj_dil"], "nn", tm=512, tn=256, tk=1024, out_dtype=F32, name=f"{tag}_dodil")
    dosb = matmul(du2, w["w_proj_sb"], "nn", tm=512, tn=256, tk=1024, out_dtype=F32, name=f"{tag}_dosb")
    dsum, do_wide, lse_wide, dsum_wide = dil_bwd_prep(dodil, odil, lse, name=f"{tag}_dprep")
    dos = [dodil[None]] + list(do_wide)
    lss = [lse[None]] + list(lse_wide)
    dss = [dsum[None]] + list(dsum_wide)
    dqs, dks, dvs = [], [], []
    for g in range(N_DIL_GROUPS):
        dq, dk, dv = dil_bwd(qks[g], vs[g], dos[g], lss[g], dss[g], name=f"{tag}_ddil{g}")
        dqs.append(dq)
        dks.append(dk)
        dvs.append(dv)
    gtot = head_sums(dosb, osb, name=f"{tag}_gsum")
    sb_grads = sb_bwd(proj, dosb, gtot, name=f"{tag}_dsb")
    dproj = rope_join(dqs, dks, dvs, sb_grads, dgate, tables, name=f"{tag}_drope")
    g_in = matmul(dproj, h, "tn", tm=1280, tn=1024, tk=2048, out_dtype=BF16, name=f"{tag}_gin")
    return {"w_in": g_in, "w_proj_dil": g_pd, "w_proj_sb": g_ps, "w_out": g_out}, dproj


def mixer_bwd_input(dx, dproj, saved, gain, w, tag):
    x = saved[0]
    return matmul_rms_bwd([(dproj, w["w_in"])], x, gain, dx, tm=256, name=f"{tag}_dh")


def kernel(x, norm_ffn1, ffn1_w_gate, ffn1_w_up, ffn1_w_down, norm_mix, w_in, w_proj_dil, w_proj_sb, w_out, norm_ffn2, ffn2_w_gate, ffn2_w_up, ffn2_w_down, norm_final, loss_target, m_norm_ffn1, m_ffn1_w_gate, m_ffn1_w_up, m_ffn1_w_down, m_norm_mix, m_w_in, m_w_proj_dil, m_w_proj_sb, m_w_out, m_norm_ffn2, m_ffn2_w_gate, m_ffn2_w_up, m_ffn2_w_down, m_norm_final, v_norm_ffn1, v_ffn1_w_gate, v_ffn1_w_up, v_ffn1_w_down, v_norm_mix, v_w_in, v_w_proj_dil, v_w_proj_sb, v_w_out, v_norm_ffn2, v_ffn2_w_gate, v_ffn2_w_up, v_ffn2_w_down, v_norm_final):
    args = dict(locals())
    t = x.shape[1]
    xs = x.reshape(t, D_MODEL)
    target = loss_target.reshape(t, D_MODEL)
    tables = rope_tables(t)

    parts = [(l, p) for l in range(2) for p in SUBBLOCKS]
    gains = {n: args[n] for n in NORM_ROWS}

    shipments = []
    for l, p in parts:
        if (l, p) == parts[0]:
            shipments += [(l, p, SUBBLOCKS[p][:2], CHIP_PEERS), (l, p, SUBBLOCKS[p][2:], ALL_PEERS)]
        elif (l, p) == parts[1]:
            shipments += [(l, p, SUBBLOCKS[p][:1], CHIP_PEERS), (l, p, SUBBLOCKS[p][1:], ALL_PEERS)]
        else:
            shipments.append((l, p, SUBBLOCKS[p], ALL_PEERS))
    in_flight, order_token = [], jnp.zeros((1, 1), F32)
    for l, p, tensors, masks in shipments:
        shards = [travelling(args[n][l], by_cols).astype(BF16) for n, by_cols in tensors]
        shards[0] = shards[0] + order_token.astype(BF16)
        in_flight.append(spread_start(shards, per_peer=False, masks=masks, name=f"gather_start_l{l}_{tensors[0][0]}"))
        order_token = in_flight[-1][-1][0:1, 0:1]

    landed = {}

    def arrived(i, after):
        if i not in landed:
            landed[i] = wait_for(i, after)
        return landed[i]

    def wait_for(i, after):
        l, p, tensors, masks = shipments[i]
        tag = f"l{l}_{tensors[0][0]}"
        lands = spread_wait(in_flight[i], after, per_peer=False, masks=masks, name=f"gather_wait_{tag}")
        if masks is CHIP_PEERS:
            lands = relay_wait(relay_start(lands, name=f"gather_relay_{tag}"), name=f"gather_relayed_{tag}")
        return {n: land.reshape(-1, land.shape[-1]) for (n, _), land in zip(tensors, lands)}

    def weights_of(l, p, after):
        mine = [i for i, s in enumerate(shipments) if s[0:2] == (l, p)]
        w = arrived(mine[0], after)
        for i in mine[1:]:
            for n, _ in shipments[i][2]:
                w[n] = functools.partial(lambda after, i, n: arrived(i, after)[n], i=i, n=n)
        return w

    saved, weights = {}, {}
    act = xs
    h = rms_fwd(xs, _row(gains["norm_ffn1"][0]) + order_token, name="l0_ffn1_norm")
    for i, (l, p) in enumerate(parts):
        weights[(l, p)] = weights_of(l, p, h if i == 0 else act)
        nl, np_ = parts[i + 1] if i + 1 < len(parts) else (None, None)
        next_gain = _row(gains["norm_" + np_][nl]) if np_ else None
        if p == "mix":
            act, h, saved[(l, p)] = mixer_fwd(act, h, weights[(l, p)], tables, f"l{l}_mix", next_gain)
        else:
            act, h, saved[(l, p)] = ffn_fwd(act, h, weights[(l, p)], p, f"l{l}_{p}", next_gain)
    dx, dxb, g_final, loss_part = final_loss(act, _row(norm_final), target, name="loss_head")

    gain_grads, sent, sent_last = {}, {}, {}
    per_device = lambda g: g.reshape(N_DEV, -1, g.shape[-1])

    def ship_last(name, g):
        sent_last[name] = spread_start([per_device(g)], per_peer=True, name=f"reduce_start_{name}")
        return sent_last[name][-1]

    for l, p in reversed(parts):
        w, sv = weights[(l, p)], saved[(l, p)]
        if p == "mix":
            gw, rest = mixer_bwd_weights(dxb, sv, w, tables, f"l{l}_mix")
        elif (l, p) == parts[0]:
            gw, rest, token = ffn_bwd_weights(dxb, sv, w, p, f"l{l}_{p}", ship=ship_last)
        else:
            gw, rest, _ = ffn_bwd_weights(dxb, sv, w, p, f"l{l}_{p}")
        if (l, p) != parts[0]:
            sent[(l, p)] = spread_start([per_device(gw[n]) for n, _ in SUBBLOCKS[p]], per_peer=True, name=f"reduce_start_l{l}_{p}")
            token = sent[(l, p)][-1]
        gain = _row(gains["norm_" + p][l]) + token[0:1, 0:1]
        if p == "mix":
            dx, dxb, gain_grads[("norm_mix", l)] = mixer_bwd_input(dx, rest, sv, gain, w, f"l{l}_mix")
        else:
            dx, dxb, gain_grads[("norm_" + p, l)] = ffn_bwd_input(dx, rest, sv, gain, w, p, f"l{l}_{p}")

    partials, big_all = {}, [{}, {}, {}, {}]

    def receive(l, p, after):
        if (l, p) == parts[0]:
            for n, started in sent_last.items():
                partials.setdefault(n, [None, None])[l] = spread_wait(started, after, per_peer=True, name=f"reduce_wait_{n}")[0]
            return
        lands = spread_wait(sent[(l, p)], after, per_peer=True, name=f"reduce_wait_l{l}_{p}")
        for (n, _), land in zip(SUBBLOCKS[p], lands):
            partials.setdefault(n, [None, None])[l] = land

    def update(p):
        for n, by_cols in SUBBLOCKS[p]:
            if by_cols and args[n].shape[-1] % 128 == 0:
                outs = sum_adamw(partials[n], args[n], args["m_" + n], args["v_" + n], tr=256, transposed=True,
                                 name=f"update_{n}")
                for kind, arr in enumerate(outs):
                    big_all[kind][n] = arr
            else:
                outs = sum_adamw(partials[n], travelling(args[n], by_cols), travelling(args["m_" + n], by_cols),
                                 travelling(args["v_" + n], by_cols), tr=256, name=f"update_{n}")
                for kind, arr in enumerate(outs):
                    big_all[kind][n] = travelling(arr, by_cols)
        return outs[1]

    for l, p in reversed(parts[1:]):
        receive(l, p, dx)
    update("ffn2")
    done = update("mix")
    receive(*parts[0], done)
    done = update("ffn1")

    loss_row = jnp.pad(loss_part[:, :1], ((0, 0), (0, D_MODEL - 1)))
    small = jnp.concatenate([gain_grads[(n, l)] for n in NORM_ROWS for l in range(2)] + [g_final, loss_row], axis=0)
    small_g = sum_partials(all_gather_rows(small, done, name="gather_gain_grads"), tr=8, name="sum_gain_grads")
    zero_row = jnp.zeros((1, D_MODEL), F32)
    small_of = lambda pre: jnp.concatenate([args[pre + n] for n in NORM_ROWS] + [_row(args[pre + "norm_final"]), zero_row], axis=0)[None]
    small_out = adamw(small_g[None], small_of(""), small_of("m_"), small_of("v_"), tr=8, name="update_gains")
    small_all = [small_g] + [o[0] for o in small_out]

    def gains_of(s):
        out = {n: s[2 * i:2 * i + 2] for i, n in enumerate(NORM_ROWS)}
        out["norm_final"] = s[6]
        return out

    order = ["norm_ffn1", "ffn1_w_gate", "ffn1_w_up", "ffn1_w_down", "norm_mix", "w_in", "w_proj_dil", "w_proj_sb", "w_out",
             "norm_ffn2", "ffn2_w_gate", "ffn2_w_up", "ffn2_w_down", "norm_final"]
    results = []
    for kind in range(4):
        both = {**big_all[kind], **gains_of(small_all[kind])}
        results += [both[n] for n in order]
    loss = small_g[7, 0]
    return (loss, dx.reshape(1, t, D_MODEL), *results)
```

```python
import functools

import jax
import jax.numpy as jnp
from jax import lax
from jax.experimental import pallas as pl
from jax.experimental.pallas import tpu as pltpu

F32 = jnp.float32
BF16 = jnp.bfloat16

D_MODEL = 1024
HEAD_DIM = 64
GROUP_W = 256
D_IN = 5120
N_DIL_GROUPS = 3
DIL_SPAN = 128
DILATIONS = (1, 4, 16)
ROPE_THETA = 500000.0
ROPE_DIM = 16
RMS_EPS = 1e-6
ATT_SCALE = HEAD_DIM ** -0.5
QS_BLK, KS_BLK, VS_BLK = 9, 10, 11
GATE_DIL_BLK, GATE_SB_BLK = 3, 4

ADAM_LR, ADAM_B1, ADAM_B2, ADAM_EPS, ADAM_WD, ADAM_STEP = 0.001, 0.9, 0.999, 1e-08, 0.01, 10

N_DEV = 8
VMEM_PHYSICAL_V7X = 64 << 20
VMEM_TEMP_HEADROOM = 20 << 20

SUBBLOCKS = {
    "ffn1": (("ffn1_w_gate", True), ("ffn1_w_up", True), ("ffn1_w_down", False)),
    "mix": (("w_in", True), ("w_proj_dil", True), ("w_proj_sb", True), ("w_out", False)),
    "ffn2": (("ffn2_w_gate", True), ("ffn2_w_up", True), ("ffn2_w_down", False)),
}
NORM_ROWS = ("norm_ffn1", "norm_mix", "norm_ffn2")


def _nbytes(shape, dtype):
    n = 1
    for s in shape:
        n *= s
    return n * jnp.dtype(dtype).itemsize


def _pcall(body, *, name, grid, in_specs, out_specs, out_shape, blocks, scratch_shapes=(), scratch_bytes=0):
    need = 2 * sum(_nbytes(s, d) for s, d in blocks) + scratch_bytes + VMEM_TEMP_HEADROOM
    limit = min(need, VMEM_PHYSICAL_V7X - (4 << 20))
    in_hbm = lambda s: pltpu.HBM(s.shape, s.dtype)
    out_shape = [in_hbm(s) for s in out_shape] if isinstance(out_shape, (list, tuple)) else in_hbm(out_shape)
    call = pl.pallas_call(
        body, name=name, grid=grid, in_specs=in_specs, out_specs=out_specs, out_shape=out_shape,
        scratch_shapes=scratch_shapes,
        compiler_params=pltpu.CompilerParams(vmem_limit_bytes=limit),
    )
    return lambda *args: call(*[pltpu.with_memory_space_constraint(a, pltpu.HBM) for a in args])


def _dot(a, b, form):
    dn = {"nn": (((1,), (0,)), ((), ())), "nt": (((1,), (1,)), ((), ())), "tn": (((0,), (0,)), ((), ()))}[form]
    return lax.dot_general(a.astype(BF16), b.astype(BF16), dn, preferred_element_type=F32)


def _sigmoid(x):
    return 1.0 / (1.0 + jnp.exp(-x))


def matmul(a, b, form, *, tm, tn, tk, out_dtype, name, scale=1.0, after=None):
    if form == "tn":
        kdim, m = a.shape
        n = b.shape[1]
    else:
        m, kdim = a.shape
        n = b.shape[1] if form == "nn" else b.shape[0]
    tm, tn, tk = min(tm, m), min(tn, n), min(tk, kdim)
    assert m % tm == 0 and n % tn == 0 and kdim % tk == 0, (name, m, n, kdim, tm, tn, tk)
    nk = kdim // tk

    if form == "tn":
        a_blk, a_map = (tk, tm), (lambda j, i, k: (k, i))
    else:
        a_blk, a_map = (tm, tk), (lambda j, i, k: (i, k))
    if form == "nt":
        b_blk, b_map = (tn, tk), (lambda j, i, k: (j, k))
    else:
        b_blk, b_map = (tk, tn), (lambda j, i, k: (k, j))
    o_map = lambda j, i, k: (i, j)

    def body(a_ref, b_ref, *rest):
        o_ref, acc = (rest[1], rest[2:]) if after is not None else (rest[0], rest[1:])

        def finish(total):
            o_ref[...] = (total * scale if scale != 1.0 else total).astype(out_dtype)

        if nk == 1:
            finish(_dot(a_ref[...], b_ref[...], form))
        else:
            acc_ref, = acc
            k = pl.program_id(2)

            @pl.when(k == 0)
            def _():
                acc_ref[...] = _dot(a_ref[...], b_ref[...], form)

            @pl.when(k > 0)
            def _():
                acc_ref[...] += _dot(a_ref[...], b_ref[...], form)

            @pl.when(k == nk - 1)
            def _():
                finish(acc_ref[...])

    scratch = [pltpu.VMEM((tm, tn), F32)] if nk > 1 else []
    in_specs = [pl.BlockSpec(a_blk, a_map), pl.BlockSpec(b_blk, b_map)]
    args = [a, b]
    if after is not None:
        in_specs.append(pl.BlockSpec(memory_space=pl.ANY))
        args.append(after)
    return _pcall(
        body, name=name, grid=(n // tn, m // tm, nk), in_specs=in_specs,
        out_specs=pl.BlockSpec((tm, tn), o_map), out_shape=jax.ShapeDtypeStruct((m, n), out_dtype),
        blocks=[(a_blk, a.dtype), (b_blk, b.dtype), ((tm, tn), out_dtype)],
        scratch_shapes=scratch, scratch_bytes=(tm * tn * 4 if nk > 1 else 0),
    )(*args)


def swiglu_fwd(h, wg_t, wu_t, *, name, tm=1024, tn=1408):
    t, d = h.shape
    f = wg_t.shape[0]
    tm, tn = min(tm, t), min(tn, f)

    def body(h_ref, wg_ref, wu_ref, ga_ref, gb_ref, s_ref):
        hh = h_ref[...]
        a = _dot(hh, wg_ref[...], "nt")
        b = _dot(hh, wu_ref[...], "nt")
        sg = _sigmoid(a)
        silu = a * sg
        ga_ref[...] = (b * (sg * (1.0 + a * (1.0 - sg)))).astype(BF16)
        gb_ref[...] = silu.astype(BF16)
        s_ref[...] = (silu * b).astype(BF16)

    w_spec = pl.BlockSpec((tn, d), lambda j, i: (j, 0))
    o_spec = pl.BlockSpec((tm, tn), lambda j, i: (i, j))
    o_shape = jax.ShapeDtypeStruct((t, f), BF16)
    return _pcall(
        body, name=name, grid=(f // tn, t // tm),
        in_specs=[pl.BlockSpec((tm, d), lambda j, i: (i, 0)), w_spec, w_spec],
        out_specs=[o_spec, o_spec, o_spec], out_shape=[o_shape, o_shape, o_shape],
        blocks=[((tm, d), BF16), ((tn, d), BF16), ((tn, d), BF16)] + [((tm, tn), BF16)] * 3,
    )(h, wg_t, wu_t)


def swiglu_bwd(dyb, wd, ga, gb, *, name, scale, tm=1024, tn=1408):
    t, d = dyb.shape
    f = wd.shape[0]
    tm, tn = min(tm, t), min(tn, f)

    def body(dy_ref, wd_ref, ga_ref, gb_ref, da_ref, db_ref):
        ds = _dot(dy_ref[...], wd_ref[...], "nt") * scale
        da_ref[...] = (ds * ga_ref[...].astype(F32)).astype(BF16)
        db_ref[...] = (ds * gb_ref[...].astype(F32)).astype(BF16)

    o_spec = pl.BlockSpec((tm, tn), lambda j, i: (i, j))
    o_shape = jax.ShapeDtypeStruct((t, f), BF16)
    return _pcall(
        body, name=name, grid=(f // tn, t // tm),
        in_specs=[pl.BlockSpec((tm, d), lambda j, i: (i, 0)), pl.BlockSpec((tn, d), lambda j, i: (j, 0)), o_spec, o_spec],
        out_specs=[o_spec, o_spec], out_shape=[o_shape, o_shape],
        blocks=[((tm, d), BF16), ((tn, d), BF16)] + [((tm, tn), BF16)] * 4,
    )(dyb, wd, ga, gb)


def gate_fwd(odil, osb, wpd_t, wps_t, proj, *, name, tm=1024):
    t = odil.shape[0]
    tm = min(tm, t)

    def body(od_ref, os_ref, wpd_ref, wps_ref, g1_ref, g2_ref, y_ref, u1_ref, u2_ref):
        u1 = _dot(od_ref[...], wpd_ref[...], "nt")
        u2 = _dot(os_ref[...], wps_ref[...], "nt")
        y = _sigmoid(g1_ref[...].astype(F32)) * u1 + _sigmoid(g2_ref[...].astype(F32)) * u2
        y_ref[...] = y.astype(BF16)
        u1_ref[...] = u1.astype(BF16)
        u2_ref[...] = u2.astype(BF16)

    o_spec = pl.BlockSpec((tm, D_MODEL), lambda i: (i, 0))
    w_spec = pl.BlockSpec((D_MODEL, GROUP_W), lambda i: (0, 0))
    a_spec = pl.BlockSpec((tm, GROUP_W), lambda i: (i, 0))
    o_shape = jax.ShapeDtypeStruct((t, D_MODEL), BF16)
    return _pcall(
        body, name=name, grid=(t // tm,),
        in_specs=[a_spec, a_spec, w_spec, w_spec,
                  pl.BlockSpec((tm, D_MODEL), lambda i: (i, GATE_DIL_BLK)),
                  pl.BlockSpec((tm, D_MODEL), lambda i: (i, GATE_SB_BLK))],
        out_specs=[o_spec, o_spec, o_spec], out_shape=[o_shape, o_shape, o_shape],
        blocks=[((tm, GROUP_W), F32)] * 2 + [((D_MODEL, GROUP_W), BF16)] * 2 + [((tm, D_MODEL), BF16)] * 5,
    )(odil, osb, wpd_t, wps_t, proj, proj)


def gate_bwd(dxb, wout, u1, u2, proj, *, name, tm=1024):
    t = dxb.shape[0]
    tm = min(tm, t)

    def body(dx_ref, w_ref, u1_ref, u2_ref, g1_ref, g2_ref, du1_ref, du2_ref, dg_ref):
        dy = _dot(dx_ref[...], w_ref[...], "nt")
        s1 = _sigmoid(g1_ref[...].astype(F32))
        s2 = _sigmoid(g2_ref[...].astype(F32))
        du1_ref[...] = (dy * s1).astype(BF16)
        du2_ref[...] = (dy * s2).astype(BF16)
        dg_ref[:, :D_MODEL] = (dy * u1_ref[...].astype(F32) * s1 * (1.0 - s1)).astype(BF16)
        dg_ref[:, D_MODEL:] = (dy * u2_ref[...].astype(F32) * s2 * (1.0 - s2)).astype(BF16)

    o_spec = pl.BlockSpec((tm, D_MODEL), lambda i: (i, 0))
    o_shape = jax.ShapeDtypeStruct((t, D_MODEL), BF16)
    return _pcall(
        body, name=name, grid=(t // tm,),
        in_specs=[o_spec, pl.BlockSpec((D_MODEL, D_MODEL), lambda i: (0, 0)), o_spec, o_spec,
                  pl.BlockSpec((tm, D_MODEL), lambda i: (i, GATE_DIL_BLK)),
                  pl.BlockSpec((tm, D_MODEL), lambda i: (i, GATE_SB_BLK))],
        out_specs=[o_spec, o_spec, pl.BlockSpec((tm, 2 * D_MODEL), lambda i: (i, 0))],
        out_shape=[o_shape, o_shape, jax.ShapeDtypeStruct((t, 2 * D_MODEL), BF16)],
        blocks=[((tm, D_MODEL), BF16)] * 9 + [((D_MODEL, D_MODEL), BF16)],
    )(dxb, wout, u1, u2, proj, proj)


def rms_fwd(x, gain, *, name, tm=512):
    t, d = x.shape
    tm = min(tm, t)

    def body(x_ref, g_ref, h_ref):
        xv = x_ref[...]
        rstd = lax.rsqrt(jnp.mean(xv * xv, axis=1, keepdims=True) + RMS_EPS)
        h_ref[...] = (xv * rstd * g_ref[...]).astype(BF16)

    return _pcall(
        body, name=name, grid=(t // tm,),
        in_specs=[pl.BlockSpec((tm, d), lambda i: (i, 0)), pl.BlockSpec((1, d), lambda i: (0, 0))],
        out_specs=pl.BlockSpec((tm, d), lambda i: (i, 0)), out_shape=jax.ShapeDtypeStruct((t, d), BF16),
        blocks=[((tm, d), F32), ((tm, d), BF16)],
    )(x, gain)


def matmul_res_norm(a, b, res, next_gain, *, scale, tm, name):
    t, k = a.shape
    d = b.shape[1]
    tm = min(tm, t)
    with_norm = next_gain is not None

    def body(a_ref, b_ref, r_ref, *rest):
        out = r_ref[...] + _dot(a_ref[...], b_ref[...], "nn") * scale
        if with_norm:
            g_ref, o_ref, h_ref = rest
            rstd = lax.rsqrt(jnp.mean(out * out, axis=1, keepdims=True) + RMS_EPS)
            h_ref[...] = (out * rstd * g_ref[...]).astype(BF16)
        else:
            o_ref, = rest
        o_ref[...] = out

    row = pl.BlockSpec((tm, d), lambda i: (i, 0))
    in_specs = [pl.BlockSpec((tm, k), lambda i: (i, 0)), pl.BlockSpec((k, d), lambda i: (0, 0)), row]
    args = [a, b, res]
    out_specs, out_shape = [row], [jax.ShapeDtypeStruct((t, d), F32)]
    if with_norm:
        in_specs.append(pl.BlockSpec((1, d), lambda i: (0, 0)))
        args.append(next_gain)
        out_specs.append(row)
        out_shape.append(jax.ShapeDtypeStruct((t, d), BF16))
    outs = _pcall(
        body, name=name, grid=(t // tm,), in_specs=in_specs, out_specs=out_specs, out_shape=out_shape,
        blocks=[((tm, k), a.dtype), ((k, d), b.dtype), ((tm, d), F32), ((tm, d), F32), ((tm, d), BF16)],
    )(*args)
    return (outs[0], outs[1]) if with_norm else (outs[0], None)


def _rms_bwd_rows(dhv, xv, g, drv):
    rstd = lax.rsqrt(jnp.mean(xv * xv, axis=1, keepdims=True) + RMS_EPS)
    xh = xv * rstd
    dxh = dhv * g
    dx = drv + rstd * (dxh - xh * jnp.mean(dxh * xh, axis=1, keepdims=True))
    return dx, jnp.sum(dhv * xh, axis=0, keepdims=True)


def matmul_rms_bwd(pairs, x, gain, dres, *, tm, name):
    t, d = x.shape
    tm = min(tm, t)
    npairs = len(pairs)

    def body(*refs):
        ab = refs[:2 * npairs]
        x_ref, g_ref, dr_ref, dx_ref, dxb_ref, dg_ref = refs[2 * npairs:]
        dh = _dot(ab[0][...], ab[1][...], "nn")
        for q in range(1, npairs):
            dh = dh + _dot(ab[2 * q][...], ab[2 * q + 1][...], "nn")
        dx, part = _rms_bwd_rows(dh, x_ref[...], g_ref[...], dr_ref[...])
        dx_ref[...] = dx
        dxb_ref[...] = dx.astype(BF16)

        @pl.when(pl.program_id(0) == 0)
        def _():
            dg_ref[...] = part

        @pl.when(pl.program_id(0) > 0)
        def _():
            dg_ref[...] += part

    in_specs, args, blocks = [], [], []
    for a, b in pairs:
        k = a.shape[1]
        in_specs += [pl.BlockSpec((tm, k), lambda i: (i, 0)), pl.BlockSpec((k, d), lambda i: (0, 0))]
        args += [a, b]
        blocks += [((tm, k), a.dtype), ((k, d), b.dtype)]
    row = pl.BlockSpec((tm, d), lambda i: (i, 0))
    vec = pl.BlockSpec((1, d), lambda i: (0, 0))
    return _pcall(
        body, name=name, grid=(t // tm,), in_specs=in_specs + [row, vec, row], out_specs=[row, row, vec],
        out_shape=[jax.ShapeDtypeStruct((t, d), F32), jax.ShapeDtypeStruct((t, d), BF16), jax.ShapeDtypeStruct((1, d), F32)],
        blocks=blocks + [((tm, d), F32)] * 3 + [((tm, d), BF16)],
    )(*args, x, gain, dres)


def final_loss(x, gain, target, *, name, tm=512):
    t, d = x.shape
    tm = min(tm, t)

    def body(x_ref, g_ref, t_ref, dx_ref, dxb_ref, dg_ref, loss_ref):
        xv = x_ref[...]
        g = g_ref[...]
        rstd = lax.rsqrt(jnp.mean(xv * xv, axis=1, keepdims=True) + RMS_EPS)
        xh = xv * rstd
        err = xh * g - t_ref[...]
        dy = err * (1.0 / d)
        dxh = dy * g
        dx = rstd * (dxh - xh * jnp.mean(dxh * xh, axis=1, keepdims=True))
        dx_ref[...] = dx
        dxb_ref[...] = dx.astype(BF16)
        part = jnp.sum(dy * xh, axis=0, keepdims=True)
        sq = jnp.sum(jnp.sum(err * err, axis=1, keepdims=True), axis=0, keepdims=True) * (0.5 / d)
        lpart = jnp.broadcast_to(sq, (1, 128))

        @pl.when(pl.program_id(0) == 0)
        def _():
            dg_ref[...] = part
            loss_ref[...] = lpart

        @pl.when(pl.program_id(0) > 0)
        def _():
            dg_ref[...] += part
            loss_ref[...] += lpart

    row = pl.BlockSpec((tm, d), lambda i: (i, 0))
    vec = pl.BlockSpec((1, d), lambda i: (0, 0))
    return _pcall(
        body, name=name, grid=(t // tm,), in_specs=[row, vec, row],
        out_specs=[row, row, vec, pl.BlockSpec((1, 128), lambda i: (0, 0))],
        out_shape=[jax.ShapeDtypeStruct((t, d), F32), jax.ShapeDtypeStruct((t, d), BF16),
                   jax.ShapeDtypeStruct((1, d), F32), jax.ShapeDtypeStruct((1, 128), F32)],
        blocks=[((tm, d), F32)] * 3 + [((tm, d), BF16)],
    )(x, gain, target)


def rope_tables(t):
    pos = jnp.arange(t, dtype=F32)
    inv_freq = ROPE_THETA ** (-jnp.arange(0, ROPE_DIM, 2, dtype=F32) / ROPE_DIM)
    ang = pos[:, None] * inv_freq[None, :]
    cos, sin = jnp.cos(ang), jnp.sin(ang)
    half = ROPE_DIM // 2
    in_head = jnp.arange(128) % HEAD_DIM
    cosw, sinw = jnp.tile(cos, (1, 128 // half)), jnp.tile(sin, (1, 128 // half))
    c = jnp.where(in_head < ROPE_DIM, cosw, 1.0)
    sa = jnp.where(in_head < half, -sinw, 0.0)
    sb = jnp.where((in_head >= half) & (in_head < ROPE_DIM), sinw, 0.0)
    return jnp.concatenate([c, sa, sb], axis=1)


def _rotate(xv, cv, sav, sbv):
    halves = []
    for half in range(2):
        x = xv[:, 128 * half:128 * (half + 1)]
        halves.append(x * cv + pltpu.roll(x, 120, 1) * sav + pltpu.roll(x, 8, 1) * sbv)
    return jnp.concatenate(halves, axis=1)


STAGE_CHUNKS = 4


def _stage(tm):
    return dict(scratch_shapes=[pltpu.VMEM((STAGE_CHUNKS, tm, 128), F32)], scratch_bytes=STAGE_CHUNKS * tm * 128 * 4)


def _split_residues(stage_ref, val, out_ref, d, col, dtype):
    rows, width = val.shape
    if d == 1:
        out_ref[0, :, col:col + width] = val.astype(dtype)
        return
    chunks = width // 128
    for c in range(chunks):
        stage_ref[c] = val[:, 128 * c:128 * (c + 1)]
    for r in range(d):
        for c in range(chunks):
            out_ref[r, :, col + 128 * c:col + 128 * (c + 1)] = stage_ref[c, pl.ds(r, rows // d, stride=d), :].astype(dtype)


def _join_residues(stage_ref, in_ref, d, col=0, width=GROUP_W):
    if d == 1:
        return in_ref[0, :, col:col + width].astype(F32)
    rows = in_ref.shape[1] * d
    chunks = width // 128
    for r in range(d):
        for c in range(chunks):
            stage_ref[c, pl.ds(r, rows // d, stride=d), :] = in_ref[r, :, col + 128 * c:col + 128 * (c + 1)].astype(F32)
    return jnp.concatenate([stage_ref[c] for c in range(chunks)], axis=1)


def rope_split(proj, tables, *, name, tm=512):
    c = sa = sb = tables
    t = tables.shape[0]
    tm = min(tm, t)

    def body(*refs):
        pieces = refs[0:9]
        c_ref, sa_ref, sb_ref = refs[9:12]
        qk_out, v_out = refs[12:15], refs[15:18]
        stage = refs[18]
        cv, sav, sbv = c_ref[...], sa_ref[...], sb_ref[...]
        for g, d in enumerate(DILATIONS):
            for kind in range(3):
                xv = pieces[3 * kind + g][...].astype(F32)
                if kind < 2:
                    _split_residues(stage, _rotate(xv, cv, sav, sbv), qk_out[g], d, GROUP_W * kind, BF16)
                else:
                    _split_residues(stage, xv, v_out[g], d, 0, BF16)

    tabs = [pl.BlockSpec((tm, 128), functools.partial(lambda i, cb: (i, cb), cb=cb)) for cb in range(3)]
    in_specs = [pl.BlockSpec((tm, GROUP_W), functools.partial(lambda i, cb: (i, cb), cb=cb)) for cb in range(9)]
    out_specs = ([pl.BlockSpec((d, tm // d, 2 * GROUP_W), lambda i: (0, i, 0)) for d in DILATIONS]
                 + [pl.BlockSpec((d, tm // d, GROUP_W), lambda i: (0, i, 0)) for d in DILATIONS])
    out_shape = ([jax.ShapeDtypeStruct((d, t // d, 2 * GROUP_W), BF16) for d in DILATIONS]
                 + [jax.ShapeDtypeStruct((d, t // d, GROUP_W), BF16) for d in DILATIONS])
    outs = _pcall(
        body, name=name, grid=(t // tm,), in_specs=in_specs + tabs, out_specs=out_specs, out_shape=out_shape,
        blocks=[((tm, GROUP_W), BF16)] * 18 + [((tm, 128), F32)] * 3,
        **_stage(tm),
    )(*([proj] * 9), c, sa, sb)
    return outs[0:3], outs[3:6]


def rope_join(dqs, dks, dvs, sb_grads, dgate, tables, *, name, tm=512):
    c = sa = sb = tables
    t = tables.shape[0]
    tm = min(tm, t)

    def body(*refs):
        pieces, sb_refs, dgate_ref = refs[0:9], refs[9:12], refs[12]
        c_ref, sa_ref, sb_ref = refs[13:16]
        o_ref, stage = refs[16], refs[17]
        cv, sav, sbv = c_ref[...], -sa_ref[...], -sb_ref[...]
        for kind in range(3):
            for g, d in enumerate(DILATIONS):
                xv = _join_residues(stage, pieces[3 * kind + g], d)
                if kind < 2:
                    xv = _rotate(xv, cv, sav, sbv)
                col = GROUP_W * (3 * kind + g)
                o_ref[:, col:col + GROUP_W] = xv.astype(BF16)
        for j in range(3):
            o_ref[:, GROUP_W * (QS_BLK + j):GROUP_W * (QS_BLK + j + 1)] = sb_refs[j][...].astype(BF16)
        o_ref[:, D_MODEL * GATE_DIL_BLK:] = dgate_ref[...]

    tabs = [pl.BlockSpec((tm, 128), functools.partial(lambda i, cb: (i, cb), cb=cb)) for cb in range(3)]
    nat = lambda w: pl.BlockSpec((tm, w), lambda i: (i, 0))
    in_specs = [pl.BlockSpec((d, tm // d, GROUP_W), lambda i: (0, i, 0)) for _ in range(3) for d in DILATIONS]
    in_specs += [nat(GROUP_W)] * 3 + [nat(2 * D_MODEL)]
    return _pcall(
        body, name=name, grid=(t // tm,), in_specs=in_specs + tabs,
        out_specs=nat(D_IN), out_shape=jax.ShapeDtypeStruct((t, D_IN), BF16),
        blocks=[((tm, GROUP_W), F32)] * 12 + [((tm, 128), F32)] * 3 + [((tm, 2 * D_MODEL), BF16), ((tm, D_IN), BF16)],
        **_stage(tm),
    )(*dqs, *dks, *dvs, *sb_grads, dgate, c, sa, sb)


def _head_mask(h):
    lane = lax.broadcasted_iota(jnp.int32, (1, GROUP_W), 1)
    return (lane // HEAD_DIM) == h


def _band_mask_before():
    ri = lax.broadcasted_iota(jnp.int32, (4 * DIL_SPAN, DIL_SPAN), 0) % DIL_SPAN
    ci = lax.broadcasted_iota(jnp.int32, (4 * DIL_SPAN, DIL_SPAN), 1)
    return ci >= ri


def dil_fwd(qk, v, *, name):
    d, nsub, _ = qk.shape
    nblk = nsub // DIL_SPAN

    def body(q_ref, kc_ref, kp_ref, vc_ref, vp_ref, o_ref, lse_ref):
        nb = pl.program_id(1)
        kk = jnp.concatenate([kp_ref[0], kc_ref[0]], axis=0)
        vv = jnp.concatenate([vp_ref[0], vc_ref[0]], axis=0)
        s = _dot(_stack_heads(q_ref[0] * ATT_SCALE), kk, "nt")
        ri = lax.broadcasted_iota(jnp.int32, s.shape, 0) % DIL_SPAN
        ci = lax.broadcasted_iota(jnp.int32, s.shape, 1)
        valid = ((ci < DIL_SPAN) & (ci >= ri) & (nb > 0)) | ((ci >= DIL_SPAN) & (ci - DIL_SPAN <= ri))
        s = jnp.where(valid, s, -jnp.inf)
        m = jnp.max(s, axis=1, keepdims=True)
        p = jnp.exp(s - m)
        den = jnp.sum(p, axis=1, keepdims=True)
        o_ref[0] = _unstack_heads(_dot(p, vv, "nn") / den, DIL_SPAN)
        lse = m + jnp.log(den)
        for h in range(4):
            lse_ref[0, :, 128 * h:128 * (h + 1)] = jnp.broadcast_to(lse[DIL_SPAN * h:DIL_SPAN * (h + 1)], (DIL_SPAN, 128))

    blk = (1, DIL_SPAN, GROUP_W)
    sblk = (1, DIL_SPAN, 512)
    prv = lambda nb: jnp.maximum(nb - 1, 0)
    return _pcall(
        body, name=name, grid=(d, nblk),
        in_specs=[pl.BlockSpec(blk, lambda r, nb: (r, nb, 0)),
                  pl.BlockSpec(blk, lambda r, nb: (r, nb, 1)),
                  pl.BlockSpec(blk, lambda r, nb: (r, prv(nb), 1)),
                  pl.BlockSpec(blk, lambda r, nb: (r, nb, 0)),
                  pl.BlockSpec(blk, lambda r, nb: (r, prv(nb), 0))],
        out_specs=[pl.BlockSpec(blk, lambda r, nb: (r, nb, 0)), pl.BlockSpec(sblk, lambda r, nb: (r, nb, 0))],
        out_shape=[jax.ShapeDtypeStruct((d, nsub, GROUP_W), F32), jax.ShapeDtypeStruct((d, nsub, 512), F32)],
        blocks=[(blk, BF16)] * 5 + [(blk, F32), (sblk, F32)],
    )(qk, qk, qk, v, v)


def dil_merge(outs, lses, *, name, tm=1024):
    t = outs[0].shape[0] * outs[0].shape[1]
    tm = min(tm, t)

    def body(o0, o1, o2, l0, l1, l2, o_ref, lse_ref, stage):
        ls = [_join_residues(stage, l, d, 0, 512) for l, d in zip((l0, l1, l2), DILATIONS)]
        m = jnp.maximum(jnp.maximum(ls[0], ls[1]), ls[2])
        tot = m + jnp.log(jnp.exp(ls[0] - m) + jnp.exp(ls[1] - m) + jnp.exp(ls[2] - m))
        lse_ref[...] = tot
        lane = lax.broadcasted_iota(jnp.int32, (1, 128), 1)
        first = lane < HEAD_DIM
        acc = jnp.zeros((tm, GROUP_W), F32)
        for og, lg, d in zip((o0, o1, o2), ls, DILATIONS):
            w = jnp.exp(lg - tot)
            wide = jnp.concatenate([jnp.where(first, w[:, 0:128], w[:, 128:256]),
                                    jnp.where(first, w[:, 256:384], w[:, 384:512])], axis=1)
            acc = acc + wide * _join_residues(stage, og, d)
        o_ref[...] = acc

    o_in = [pl.BlockSpec((d, tm // d, GROUP_W), lambda i: (0, i, 0)) for d in DILATIONS]
    l_in = [pl.BlockSpec((d, tm // d, 512), lambda i: (0, i, 0)) for d in DILATIONS]
    return _pcall(
        body, name=name, grid=(t // tm,), in_specs=o_in + l_in,
        out_specs=[pl.BlockSpec((tm, GROUP_W), lambda i: (i, 0)), pl.BlockSpec((tm, 512), lambda i: (i, 0))],
        out_shape=[jax.ShapeDtypeStruct((t, GROUP_W), F32), jax.ShapeDtypeStruct((t, 512), F32)],
        blocks=[((tm, GROUP_W), F32)] * 4 + [((tm, 512), F32)] * 4,
        **_stage(tm),
    )(*outs, *lses)


def dil_bwd_prep(do, o, lse, *, name, tm=1024):
    t = do.shape[0]
    tm = min(tm, t)
    wide = DILATIONS[1:]

    def body(do_ref, o_ref, lse_ref, ds_ref, *rest):
        do_out, lse_out, ds_out = rest[0:2], rest[2:4], rest[4:6]
        stage = rest[6]
        dov = do_ref[...]
        prod = dov * o_ref[...]
        for h in range(4):
            s = jnp.sum(jnp.where(_head_mask(h), prod, 0.0), axis=1, keepdims=True)
            ds_ref[:, 128 * h:128 * (h + 1)] = jnp.broadcast_to(s, (tm, 128))
        for i, d in enumerate(wide):
            _split_residues(stage, dov, do_out[i], d, 0, BF16)
            _split_residues(stage, lse_ref[...], lse_out[i], d, 0, F32)
            _split_residues(stage, ds_ref[...], ds_out[i], d, 0, F32)

    nat = lambda w: pl.BlockSpec((tm, w), lambda i: (i, 0))
    res = lambda d, w: pl.BlockSpec((d, tm // d, w), lambda i: (0, i, 0))
    shape = lambda d, w, dt: jax.ShapeDtypeStruct((d, t // d, w), dt)
    outs = _pcall(
        body, name=name, grid=(t // tm,), in_specs=[nat(GROUP_W), nat(GROUP_W), nat(512)],
        out_specs=[nat(512)] + [res(d, GROUP_W) for d in wide] + [res(d, 512) for d in wide] * 2,
        out_shape=([jax.ShapeDtypeStruct((t, 512), F32)] + [shape(d, GROUP_W, BF16) for d in wide]
                   + [shape(d, 512, F32) for d in wide] * 2),
        blocks=[((tm, GROUP_W), F32)] * 3 + [((tm, 512), F32)] * 6,
        **_stage(tm),
    )(do, o, lse)
    return outs[0], outs[1:3], outs[3:5], outs[5:7]


def head_sums(a, b, *, name, tm=512):
    t = a.shape[0]
    tm = min(tm, t)

    def body(a_ref, b_ref, o_ref):
        prod = a_ref[...].astype(BF16).astype(F32) * b_ref[...]
        for h in range(4):
            s = jnp.sum(jnp.where(_head_mask(h), prod, 0.0), axis=1, keepdims=True)
            o_ref[:, 128 * h:128 * (h + 1)] = jnp.broadcast_to(s, (tm, 128))

    spec = pl.BlockSpec((tm, GROUP_W), lambda i: (i, 0))
    return _pcall(
        body, name=name, grid=(t // tm,), in_specs=[spec, spec],
        out_specs=pl.BlockSpec((tm, 512), lambda i: (i, 0)), out_shape=jax.ShapeDtypeStruct((t, 512), F32),
        blocks=[((tm, GROUP_W), F32)] * 2 + [((tm, 512), F32)],
    )(a, b)


def dil_bwd(qk, v, do, lse, dsum, *, name):
    d, nsub, _ = qk.shape
    nblk = nsub // DIL_SPAN

    def body(qa_ref, qb_ref, kc_ref, kp_ref, vc_ref, vp_ref, doa_ref, dob_ref, la_ref, lb_ref, sa_ref, sb_ref,
             dq_ref, dk_ref, dv_ref):
        nb = pl.program_id(1)
        nxt = _band_mask_before() & (nb < nblk - 1)
        kc, kp, vc, vp = kc_ref[0], kp_ref[0], vc_ref[0], vp_ref[0]
        qas, qbs = _stack_heads(qa_ref[0] * ATT_SCALE), _stack_heads(qb_ref[0] * ATT_SCALE)
        das, dbs = _stack_heads(doa_ref[0].astype(BF16)), _stack_heads(dob_ref[0].astype(BF16))
        stat = lambda ref: jnp.concatenate([ref[0, :, 128 * h:128 * (h + 1)] for h in range(4)], axis=0)
        la, lb, sa, sb = stat(la_ref), stat(lb_ref), stat(sa_ref), stat(sb_ref)

        def probs(qs, ds_, k, v, mask, l, s):
            p = jnp.where(mask, jnp.exp(_dot(qs, k, "nt") - l), 0.0)
            dsc = p * (_dot(ds_, v, "nt") - s)
            return p.astype(BF16), dsc.astype(BF16)

        wide = lambda a: jnp.concatenate([a, a], axis=1)
        ri = lax.broadcasted_iota(jnp.int32, (4 * DIL_SPAN, 2 * DIL_SPAN), 0) % DIL_SPAN
        ci = lax.broadcasted_iota(jnp.int32, (4 * DIL_SPAN, 2 * DIL_SPAN), 1)
        valid = ((ci < DIL_SPAN) & (ci >= ri) & (nb > 0)) | ((ci >= DIL_SPAN) & (ci - DIL_SPAN <= ri))
        p_a, ds_a = probs(qas, das, jnp.concatenate([kp, kc], axis=0), jnp.concatenate([vp, vc], axis=0),
                          valid, wide(la), wide(sa))
        p_nc, ds_nc = probs(qbs, dbs, kc, vc, nxt, lb, sb)
        dq_ref[0] = _unstack_heads(_dot(ds_a, jnp.concatenate([kp, kc], axis=0), "nn"), DIL_SPAN) * ATT_SCALE
        dk_ref[0] = _dot(ds_a[:, DIL_SPAN:], qas, "tn") + _dot(ds_nc, qbs, "tn")
        dv_ref[0] = _dot(p_a[:, DIL_SPAN:], das, "tn") + _dot(p_nc, dbs, "tn")

    blk = (1, DIL_SPAN, GROUP_W)
    sblk = (1, DIL_SPAN, 512)
    prv = lambda nb: jnp.maximum(nb - 1, 0)
    nxt_ = lambda nb: jnp.minimum(nb + 1, nblk - 1)
    cur_at = lambda c: pl.BlockSpec(blk, functools.partial(lambda r, nb, c: (r, nb, c), c=c))
    prv_at = lambda c: pl.BlockSpec(blk, functools.partial(lambda r, nb, c: (r, prv(nb), c), c=c))
    nxt_at = lambda c: pl.BlockSpec(blk, functools.partial(lambda r, nb, c: (r, nxt_(nb), c), c=c))
    s_cur = pl.BlockSpec(sblk, lambda r, nb: (r, nb, 0))
    s_nxt = pl.BlockSpec(sblk, lambda r, nb: (r, nxt_(nb), 0))
    o_spec = pl.BlockSpec(blk, lambda r, nb: (r, nb, 0))
    o_shape = jax.ShapeDtypeStruct((d, nsub, GROUP_W), F32)
    return _pcall(
        body, name=name, grid=(d, nblk),
        in_specs=[cur_at(0), nxt_at(0), cur_at(1), prv_at(1), cur_at(0), prv_at(0), cur_at(0), nxt_at(0),
                  s_cur, s_nxt, s_cur, s_nxt],
        out_specs=[o_spec, o_spec, o_spec], out_shape=[o_shape, o_shape, o_shape],
        blocks=[(blk, BF16)] * 6 + [(blk, F32)] * 5 + [(sblk, F32)] * 4,
    )(qk, qk, qk, qk, v, v, do, do, lse, lse, dsum, dsum)


def _tri_dot(x, b):
    hi = x.astype(BF16)
    lo = (x - hi.astype(F32)).astype(BF16)
    return _dot(jnp.concatenate([hi, lo], axis=1), jnp.concatenate([b, b], axis=0), "nn")


SB_TILE = 256
SB_ROWS = 512


def _stack_heads(a):
    return jnp.concatenate([jnp.where(_head_mask(h), a, jnp.zeros_like(a)) for h in range(4)], axis=0)


def _unstack_heads(acc, rows):
    out = acc[0:rows]
    for h in range(1, 4):
        out = jnp.where(_head_mask(h), acc[h * rows:(h + 1) * rows], out)
    return out


def _tri_masks(n):
    ri = lax.broadcasted_iota(jnp.int32, (n, n), 0)
    ci = lax.broadcasted_iota(jnp.int32, (n, n), 1)
    return (ri > ci).astype(BF16), (ri >= ci).astype(BF16)


def _sb_weights(qs, kt, after, c_keep, lead):
    z = _dot(qs, kt, "nt")
    lbeta = jnp.minimum(z, 0.0) - jnp.log(1.0 + jnp.exp(-jnp.abs(z)))
    lkeep = lbeta - z
    past = None
    if lead is not None:
        query = lax.broadcasted_iota(jnp.int32, z.shape, 0) % SB_ROWS
        past = lax.broadcasted_iota(jnp.int32, z.shape, 1) + lead < query
        lkeep = jnp.where(past, lkeep, 0.0)
    w = jnp.exp(lbeta + _tri_dot(lkeep, after) + c_keep)
    if lead is not None:
        w = jnp.where(past, w, 0.0)
    return z, past, lbeta, lkeep, w


def _sb_walk(qb, tile, carry):
    per = SB_ROWS // SB_TILE
    for i in reversed(range(per)):
        carry = tile(pl.multiple_of(qb * SB_ROWS + i * SB_TILE, SB_TILE), i * SB_TILE, i == per - 1, carry)
    past_tiles = qb * per
    return lax.fori_loop(0, past_tiles,
                         lambda it, c: tile(pl.multiple_of((past_tiles - 1 - it) * SB_TILE, SB_TILE), None, False, c), carry)


def sb_fwd(proj, *, name):
    t = proj.shape[0]
    n, m = SB_TILE, SB_ROWS
    assert t % m == 0

    def body(q_ref, k_ref, v_ref, o_ref, acc_ref):
        qb = pl.program_id(0)
        qs = _stack_heads(q_ref[...] * ATT_SCALE)
        after, _ = _tri_masks(n)

        def tile(off, lead, first, c_keep):
            kt = k_ref[pl.ds(off, n), :]
            vt = v_ref[pl.ds(off, n), :]
            _, _, _, lkeep, w = _sb_weights(qs, kt, after, c_keep, lead)
            pv = _tri_dot(w, vt)
            if first:
                acc_ref[...] = pv
            else:
                acc_ref[...] += pv
            return c_keep + jnp.sum(lkeep, axis=1, keepdims=True)

        _sb_walk(qb, tile, jnp.zeros((4 * m, 1), F32))
        o_ref[...] = _unstack_heads(acc_ref[...], m)

    full = lambda cb: pl.BlockSpec((t, GROUP_W), functools.partial(lambda i, cb: (0, cb), cb=cb))
    return _pcall(
        body, name=name, grid=(t // m,),
        in_specs=[pl.BlockSpec((m, GROUP_W), lambda i: (i, QS_BLK)), full(KS_BLK), full(VS_BLK)],
        out_specs=pl.BlockSpec((m, GROUP_W), lambda i: (i, 0)), out_shape=jax.ShapeDtypeStruct((t, GROUP_W), F32),
        blocks=[((m, GROUP_W), BF16), ((t, GROUP_W), BF16), ((t, GROUP_W), BF16), ((m, GROUP_W), F32)],
        scratch_shapes=[pltpu.VMEM((4 * m, GROUP_W), F32)], scratch_bytes=4 * m * GROUP_W * 4,
    )(proj, proj, proj)


def sb_bwd(proj, do, gtot, *, name):
    t = proj.shape[0]
    n, m = SB_TILE, SB_ROWS
    assert t % m == 0

    def body(q_ref, k_ref, v_ref, do_ref, gt_ref, dq_ref, dk_ref, dv_ref, acc_ref):
        qb = pl.program_id(0)

        @pl.when(qb == 0)
        def _():
            dk_ref[...] = jnp.zeros_like(dk_ref)
            dv_ref[...] = jnp.zeros_like(dv_ref)

        qs = _stack_heads(q_ref[...] * ATT_SCALE)
        dos = _stack_heads(do_ref[...].astype(BF16))
        gt = jnp.concatenate([jnp.max(gt_ref[:, 128 * h:128 * (h + 1)], axis=1, keepdims=True) for h in range(4)], axis=0)
        after, from_on = _tri_masks(n)

        def tile(off, lead, first, carry):
            c_keep, c_g = carry
            kt = k_ref[pl.ds(off, n), :]
            vt = v_ref[pl.ds(off, n), :]
            z, past, lbeta, lkeep, w = _sb_weights(qs, kt, after, c_keep, lead)
            gw = w * _dot(dos, vt, "nt")
            big_g = gt - (_tri_dot(gw, from_on) + c_g)
            dz = gw * jnp.exp(lbeta - z) - big_g * jnp.exp(lbeta)
            if lead is not None:
                dz = jnp.where(past, dz, 0.0)
            dz = dz.astype(BF16)
            dk_ref[pl.ds(off, n), :] += _dot(dz, qs, "tn")
            dv_ref[pl.ds(off, n), :] += _dot(w, dos, "tn")
            dq = _dot(dz, kt, "nn")
            if first:
                acc_ref[...] = dq
            else:
                acc_ref[...] += dq
            return c_keep + jnp.sum(lkeep, axis=1, keepdims=True), c_g + jnp.sum(gw, axis=1, keepdims=True)

        zero_col = jnp.zeros((4 * m, 1), F32)
        _sb_walk(qb, tile, (zero_col, zero_col))
        dq_ref[...] = _unstack_heads(acc_ref[...], m) * ATT_SCALE

    full = lambda cb: pl.BlockSpec((t, GROUP_W), functools.partial(lambda i, cb: (0, cb), cb=cb))
    whole = pl.BlockSpec((t, GROUP_W), lambda i: (0, 0))
    rowblk = pl.BlockSpec((m, GROUP_W), lambda i: (i, 0))
    shape = jax.ShapeDtypeStruct((t, GROUP_W), F32)
    return _pcall(
        body, name=name, grid=(t // m,),
        in_specs=[pl.BlockSpec((m, GROUP_W), lambda i: (i, QS_BLK)), full(KS_BLK), full(VS_BLK), rowblk,
                  pl.BlockSpec((m, 512), lambda i: (i, 0))],
        out_specs=[rowblk, whole, whole], out_shape=[shape, shape, shape],
        blocks=[((m, GROUP_W), BF16), ((t, GROUP_W), BF16), ((t, GROUP_W), BF16), ((m, GROUP_W), F32),
                ((m, 512), F32), ((m, GROUP_W), F32), ((t, GROUP_W), F32), ((t, GROUP_W), F32)],
        scratch_shapes=[pltpu.VMEM((4 * m, GROUP_W), F32)], scratch_bytes=4 * m * GROUP_W * 4,
    )(proj, proj, proj, do, gtot)


def _mesh_place():
    return lax.axis_index("x"), lax.axis_index("y"), lax.axis_index("c")


def _flip(place, mask):
    x, y, c = place
    return ((1 - x) if mask & 4 else x, (1 - y) if mask & 2 else y, (1 - c) if mask & 1 else c)


def _dev_index(place):
    x, y, c = place
    return 4 * x + 2 * y + c


HBM_SPEC = pl.BlockSpec(memory_space=pltpu.HBM)


def all_gather_rows(shard, after, *, name):
    rows, lanes = shard.shape

    def body(x_ref, after_ref, out_ref, send_sems, recv_sems, local_sem):
        me = _mesh_place()
        x, y, c = me
        sibling = _flip(me, 1)
        chips = [_flip(me, 4), _flip(me, 2), _flip(me, 6)]

        def copy(k, block, to, src=None):
            dst = out_ref.at[_dev_index(block)]
            return pltpu.make_async_remote_copy(
                src_ref=dst if src is None else src, dst_ref=dst, send_sem=send_sems.at[k], recv_sem=recv_sems.at[k],
                device_id=to, device_id_type=pl.DeviceIdType.MESH)

        mine = pltpu.make_async_copy(x_ref, out_ref.at[_dev_index(me)], local_sem)
        mine.start()
        first = [copy(0, me, sibling, src=x_ref)] + [copy(1 + j, me, chip, src=x_ref) for j, chip in enumerate(chips)]
        for cp in first:
            cp.start()
        passed = [copy(4 + j, chip, sibling) for j, chip in enumerate(chips)]
        for j, chip in enumerate(chips):
            copy(1 + j, chip, me).wait_recv()
            passed[j].start()
        copy(0, sibling, me).wait_recv()
        for j, chip in enumerate(chips):
            copy(4 + j, _flip(chip, 1), me).wait_recv()
        for cp in first + passed:
            cp.wait_send()
        mine.wait()

    return pl.pallas_call(
        body, name=name, in_specs=[HBM_SPEC, pl.BlockSpec(memory_space=pl.ANY)], out_specs=HBM_SPEC,
        out_shape=jax.ShapeDtypeStruct((N_DEV, rows, lanes), shard.dtype),
        scratch_shapes=[pltpu.SemaphoreType.DMA((7,)), pltpu.SemaphoreType.DMA((7,)), pltpu.SemaphoreType.DMA],
    )(shard, after)


SEM_SPEC = pl.BlockSpec(memory_space=pltpu.SEMAPHORE)
DATAFLOW_EFFECT = pltpu.SideEffectType.DATAFLOW_SIDE_EFFECTING


ALL_PEERS = tuple(range(1, N_DEV))
CHIP_PEERS = (1, 4, 2, 6)
OTHER_CHIPS = (4, 2, 6)


def _spread_copies(src_refs, land_refs, send_sems, recv_sems, per_peer, masks, arriving):
    me = _mesh_place()
    my = _dev_index(me)
    remote, local = [], []
    for t, (src_ref, land_ref) in enumerate(zip(src_refs, land_refs)):
        for i, mask in enumerate(masks):
            peer = _flip(me, mask)
            data_of = my if arriving else _dev_index(peer)
            slot = _dev_index(peer) if arriving else my
            k = t * len(masks) + i
            remote.append(pltpu.make_async_remote_copy(
                src_ref=src_ref.at[data_of] if per_peer else src_ref, dst_ref=land_ref.at[slot],
                send_sem=send_sems.at[k], recv_sem=recv_sems.at[k],
                device_id=peer, device_id_type=pl.DeviceIdType.MESH))
        local.append(pltpu.make_async_copy(src_ref.at[my] if per_peer else src_ref, land_ref.at[my],
                                           send_sems.at[len(src_refs) * len(masks) + t]))
    return remote, local


def spread_start(srcs, *, per_peer, name, masks=ALL_PEERS):
    nt = len(srcs)
    zones = [pltpu.HBM((N_DEV,) + (s.shape[1:] if per_peer else s.shape), s.dtype) for s in srcs]

    def body(*refs):
        src_refs, (send_sems, recv_sems) = refs[:nt], refs[nt:nt + 2]
        land_refs, token = refs[2 * nt + 2:3 * nt + 2], refs[3 * nt + 2]
        remote, local = _spread_copies(src_refs, land_refs, send_sems, recv_sems, per_peer, masks, arriving=False)
        for cp in remote + local:
            cp.start()
        token[...] = jnp.zeros_like(token)

    return pl.pallas_call(
        body, name=name, in_specs=(HBM_SPEC,) * nt,
        out_shape=(pltpu.SemaphoreType.DMA((nt * len(masks) + nt,)), pltpu.SemaphoreType.DMA((nt * len(masks),)),
                   *[pltpu.HBM(s.shape, s.dtype) for s in srcs], *zones, jax.ShapeDtypeStruct((8, 128), F32)),
        out_specs=(SEM_SPEC, SEM_SPEC) + (HBM_SPEC,) * (2 * nt) + (pl.BlockSpec(memory_space=pltpu.VMEM),),
        input_output_aliases={t: 2 + t for t in range(nt)},
        compiler_params=pltpu.CompilerParams(has_side_effects=DATAFLOW_EFFECT),
    )(*[pltpu.with_memory_space_constraint(s, pltpu.HBM) for s in srcs])


def spread_wait(started, after, *, per_peer, name, masks=ALL_PEERS):
    nt = (len(started) - 3) // 2
    send_sems, recv_sems = started[0:2]
    srcs_thru, lands_thru = started[2:2 + nt], started[2 + nt:2 + 2 * nt]

    def body(*refs):
        src_refs, land_refs = refs[:nt], refs[nt:2 * nt]
        send_sems, recv_sems = refs[2 * nt:2 * nt + 2]
        remote, local = _spread_copies(src_refs, land_refs, send_sems, recv_sems, per_peer, masks, arriving=True)
        for cp in remote:
            cp.wait_send()
            cp.wait_recv()
        for cp in local:
            cp.wait()

    outs = pl.pallas_call(
        body, name=name, in_specs=(HBM_SPEC,) * (2 * nt) + (SEM_SPEC, SEM_SPEC, pl.BlockSpec(memory_space=pl.ANY)),
        out_shape=tuple(pltpu.HBM(a.shape, a.dtype) for a in (*srcs_thru, *lands_thru)),
        out_specs=(HBM_SPEC,) * (2 * nt), input_output_aliases={t: t for t in range(2 * nt)},
        compiler_params=pltpu.CompilerParams(has_side_effects=DATAFLOW_EFFECT),
    )(*srcs_thru, *lands_thru, send_sems, recv_sems, after)
    return list(outs[nt:])


def _relay_copies(land_refs, send_sems, recv_sems, arriving):
    me = _mesh_place()
    sibling = _flip(me, 1)
    out = []
    for t, land_ref in enumerate(land_refs):
        for i, mask in enumerate(OTHER_CHIPS):
            slot = _dev_index(_flip(sibling if arriving else me, mask))
            k = t * len(OTHER_CHIPS) + i
            out.append(pltpu.make_async_remote_copy(
                src_ref=land_ref.at[slot], dst_ref=land_ref.at[slot], send_sem=send_sems.at[k], recv_sem=recv_sems.at[k],
                device_id=sibling, device_id_type=pl.DeviceIdType.MESH))
    return out


def relay_start(lands, *, name):
    nt = len(lands)
    n_sem = nt * len(OTHER_CHIPS)

    def body(*refs):
        for cp in _relay_copies(refs[:nt], refs[nt], refs[nt + 1], arriving=False):
            cp.start()

    return pl.pallas_call(
        body, name=name, in_specs=(HBM_SPEC,) * nt,
        out_shape=(pltpu.SemaphoreType.DMA((n_sem,)), pltpu.SemaphoreType.DMA((n_sem,)),
                   *[pltpu.HBM(a.shape, a.dtype) for a in lands]),
        out_specs=(SEM_SPEC, SEM_SPEC) + (HBM_SPEC,) * nt, input_output_aliases={t: 2 + t for t in range(nt)},
        compiler_params=pltpu.CompilerParams(has_side_effects=DATAFLOW_EFFECT),
    )(*[pltpu.with_memory_space_constraint(a, pltpu.HBM) for a in lands])


def relay_wait(started, *, name):
    send_sems, recv_sems = started[0:2]
    lands_thru = started[2:]
    nt = len(lands_thru)

    def body(*refs):
        for cp in _relay_copies(refs[:nt], refs[nt], refs[nt + 1], arriving=True):
            cp.wait_send()
            cp.wait_recv()

    return list(pl.pallas_call(
        body, name=name, in_specs=(HBM_SPEC,) * nt + (SEM_SPEC, SEM_SPEC),
        out_shape=tuple(pltpu.HBM(a.shape, a.dtype) for a in lands_thru), out_specs=(HBM_SPEC,) * nt,
        input_output_aliases={t: t for t in range(nt)},
        compiler_params=pltpu.CompilerParams(has_side_effects=DATAFLOW_EFFECT),
    )(*lands_thru, send_sems, recv_sems))


def sum_partials(parts, *, name, tr):
    _, rows, lanes = parts.shape
    assert rows % tr == 0

    def body(p_ref, g_ref):
        g = p_ref[0].astype(F32)
        for k in range(1, N_DEV):
            g = g + p_ref[k].astype(F32)
        g_ref[...] = g

    return _pcall(
        body, name=name, grid=(rows // tr,),
        in_specs=[pl.BlockSpec((N_DEV, tr, lanes), lambda i: (0, i, 0))],
        out_specs=pl.BlockSpec((tr, lanes), lambda i: (i, 0)), out_shape=jax.ShapeDtypeStruct((rows, lanes), F32),
        blocks=[((N_DEV, tr, lanes), parts.dtype), ((tr, lanes), F32)],
    )(parts)


def adamw(g, w, m, v, *, name, tr):
    nl, k, n = w.shape
    tr = max(c for c in range(8, min(tr, k) + 1, 8) if k % c == 0)
    bc1 = 1.0 - ADAM_B1 ** ADAM_STEP
    bc2 = 1.0 - ADAM_B2 ** ADAM_STEP

    def body(g_ref, w_ref, m_ref, v_ref, d_ref, mo_ref, vo_ref):
        gv = g_ref[...]
        m_new = ADAM_B1 * m_ref[...] + (1.0 - ADAM_B1) * gv
        v_new = ADAM_B2 * v_ref[...] + (1.0 - ADAM_B2) * (gv * gv)
        mo_ref[...] = m_new
        vo_ref[...] = v_new
        d_ref[...] = -ADAM_LR * ((m_new / bc1) / (jnp.sqrt(v_new / bc2) + ADAM_EPS) + ADAM_WD * w_ref[...])

    spec = pl.BlockSpec((1, tr, n), lambda l, i: (l, i, 0))
    shape = jax.ShapeDtypeStruct(w.shape, F32)
    return _pcall(
        body, name=name, grid=(nl, k // tr), in_specs=[spec] * 4, out_specs=[spec] * 3, out_shape=[shape] * 3,
        blocks=[((1, tr, n), F32)] * 7,
    )(g, w, m, v)


def sum_adamw(partials, w, m, v, *, name, tr, transposed=False):
    nl, k, n = w.shape
    assert nl == len(partials) == 2
    step = 128 if transposed else 8
    tr = max(c for c in range(step, min(tr, k) + 1, step) if k % c == 0)
    bc1 = 1.0 - ADAM_B1 ** ADAM_STEP
    bc2 = 1.0 - ADAM_B2 ** ADAM_STEP

    def body(p0_ref, p1_ref, w_ref, m_ref, v_ref, g_ref, d_ref, mo_ref, vo_ref):
        first = pl.program_id(0) == 0
        gv = jnp.where(first, p0_ref[0], p1_ref[0]).astype(F32)
        for s in range(1, N_DEV):
            gv = gv + jnp.where(first, p0_ref[s], p1_ref[s]).astype(F32)
        if transposed:
            gv = gv.T
        m_new = ADAM_B1 * m_ref[0] + (1.0 - ADAM_B1) * gv
        v_new = ADAM_B2 * v_ref[0] + (1.0 - ADAM_B2) * (gv * gv)
        g_ref[0] = gv
        mo_ref[0] = m_new
        vo_ref[0] = v_new
        d_ref[0] = -ADAM_LR * ((m_new / bc1) / (jnp.sqrt(v_new / bc2) + ADAM_EPS) + ADAM_WD * w_ref[0])

    spec = pl.BlockSpec((1, tr, n), lambda l, i: (l, i, 0))
    if transposed:
        p0spec = pl.BlockSpec((N_DEV, n, tr), lambda l, i: (0, 0, i * (1 - l)))
        p1spec = pl.BlockSpec((N_DEV, n, tr), lambda l, i: (0, 0, i * l))
    else:
        p0spec = pl.BlockSpec((N_DEV, tr, n), lambda l, i: (0, i * (1 - l), 0))
        p1spec = pl.BlockSpec((N_DEV, tr, n), lambda l, i: (0, i * l, 0))
    shape = jax.ShapeDtypeStruct(w.shape, F32)
    return _pcall(
        body, name=name, grid=(nl, k // tr), in_specs=[p0spec, p1spec, spec, spec, spec], out_specs=[spec] * 4,
        out_shape=[shape] * 4, blocks=[((N_DEV, tr, n), BF16)] * 2 + [((1, tr, n), F32)] * 7,
    )(partials[0], partials[1], w, m, v)


def travelling(a, by_cols):
    return jnp.swapaxes(a, -1, -2) if by_cols else a


def _row(v):
    return v.reshape(1, -1)


def ffn_fwd(x, h, w, pre, tag, next_gain):
    ga, gb, s = swiglu_fwd(h, w[pre + "_w_gate"], w[pre + "_w_up"], name=f"{tag}_gateup")
    if callable(w[pre + "_w_down"]):
        w[pre + "_w_down"] = w[pre + "_w_down"](s)
    out, h_next = matmul_res_norm(s, w[pre + "_w_down"], x, next_gain, scale=0.5, tm=512, name=f"{tag}_down")
    return out, h_next, (x, h, ga, gb, s)


def ffn_bwd_weights(dxb, saved, w, pre, tag, ship=None):
    x, h, a, b, s = saved
    token = None
    grads = {}

    def made(name, g):
        grads[name] = g
        return ship(name, g) if ship else None

    da, db = swiglu_bwd(dxb, w[pre + "_w_down"], a, b, scale=0.5, name=f"{tag}_dgateup")
    token = made(pre + "_w_down", matmul(s, dxb, "tn", tm=1408, tn=1024, tk=2048, out_dtype=BF16, scale=0.5, name=f"{tag}_gdown"))
    token = made(pre + "_w_gate", matmul(da, h, "tn", tm=1408, tn=1024, tk=2048, out_dtype=BF16, after=token, name=f"{tag}_ggate"))
    token = made(pre + "_w_up", matmul(db, h, "tn", tm=1408, tn=1024, tk=2048, out_dtype=BF16, after=token, name=f"{tag}_gup"))
    return grads, (da, db), token


def ffn_bwd_input(dx, rest, saved, gain, w, pre, tag):
    da, db = rest
    x = saved[0]
    return matmul_rms_bwd([(da, w[pre + "_w_gate"]), (db, w[pre + "_w_up"])], x, gain, dx, tm=256, name=f"{tag}_dh")


def mixer_fwd(x, h, w, tables, tag, next_gain):
    proj = matmul(h, w["w_in"], "nt", tm=1024, tn=1280, tk=1024, out_dtype=BF16, name=f"{tag}_in")
    qks, vs = rope_split(proj, tables, name=f"{tag}_rope")
    outs, lses = [], []
    for g in range(N_DIL_GROUPS):
        o, lse = dil_fwd(qks[g], vs[g], name=f"{tag}_dil{g}")
        outs.append(o)
        lses.append(lse)
    odil, lse = dil_merge(outs, lses, name=f"{tag}_merge")
    osb = sb_fwd(proj, name=f"{tag}_sb")
    for n in ("w_proj_dil", "w_proj_sb", "w_out"):
        if callable(w[n]):
            w[n] = w[n](osb)
    y, u1, u2 = gate_fwd(odil, osb, w["w_proj_dil"], w["w_proj_sb"], proj, name=f"{tag}_gate")
    out, h_next = matmul_res_norm(y, w["w_out"], x, next_gain, scale=1.0, tm=512, name=f"{tag}_out")
    return out, h_next, (x, h, proj, qks, vs, odil, lse, osb, u1, u2, y)


def mixer_bwd_weights(dxb, saved, w, tables, tag):
    x, h, proj, qks, vs, odil, lse, osb, u1, u2, y = saved
    t = x.shape[0]
    g_out = matmul(y, dxb, "tn", tm=1024, tn=1024, tk=2048, out_dtype=BF16, name=f"{tag}_gout")
    du1, du2, dgate = gate_bwd(dxb, w["w_out"], u1, u2, proj, name=f"{tag}_dgate")
    g_pd = matmul(du1, odil, "tn", tm=1024, tn=256, tk=2048, out_dtype=BF16, name=f"{tag}_gpd")
    g_ps = matmul(du2, osb, "tn", tm=1024, tn=256, tk=2048, out_dtype=BF16, name=f"{tag}_gps")
    dodil = matmul(du1, w["w_proj_dil"], "nn", tm=512, tn=256, tk=1024, out_dtype=F32, name=f"{tag}_dodil")
    dosb = matmul(du2, w["w_proj_sb"], "nn", tm=512, tn=256, tk=1024, out_dtype=F32, name=f"{tag}_dosb")
    dsum, do_wide, lse_wide, dsum_wide = dil_bwd_prep(dodil, odil, lse, name=f"{tag}_dprep")
    dos = [dodil[None]] + list(do_wide)
    lss = [lse[None]] + list(lse_wide)
    dss = [dsum[None]] + list(dsum_wide)
    dqs, dks, dvs = [], [], []
    for g in range(N_DIL_GROUPS):
        dq, dk, dv = dil_bwd(qks[g], vs[g], dos[g], lss[g], dss[g], name=f"{tag}_ddil{g}")
        dqs.append(dq)
        dks.append(dk)
        dvs.append(dv)
    gtot = head_sums(dosb, osb, name=f"{tag}_gsum")
    sb_grads = sb_bwd(proj, dosb, gtot, name=f"{tag}_dsb")
    dproj = rope_join(dqs, dks, dvs, sb_grads, dgate, tables, name=f"{tag}_drope")
    g_in = matmul(dproj, h, "tn", tm=1280, tn=1024, tk=2048, out_dtype=BF16, name=f"{tag}_gin")
    return {"w_in": g_in, "w_proj_dil": g_pd, "w_proj_sb": g_ps, "w_out": g_out}, dproj


def mixer_bwd_input(dx, dproj, saved, gain, w, tag):
    x = saved[0]
    return matmul_rms_bwd([(dproj, w["w_in"])], x, gain, dx, tm=256, name=f"{tag}_dh")


def kernel(x, norm_ffn1, ffn1_w_gate, ffn1_w_up, ffn1_w_down, norm_mix, w_in, w_proj_dil, w_proj_sb, w_out, norm_ffn2, ffn2_w_gate, ffn2_w_up, ffn2_w_down, norm_final, loss_target, m_norm_ffn1, m_ffn1_w_gate, m_ffn1_w_up, m_ffn1_w_down, m_norm_mix, m_w_in, m_w_proj_dil, m_w_proj_sb, m_w_out, m_norm_ffn2, m_ffn2_w_gate, m_ffn2_w_up, m_ffn2_w_down, m_norm_final, v_norm_ffn1, v_ffn1_w_gate, v_ffn1_w_up, v_ffn1_w_down, v_norm_mix, v_w_in, v_w_proj_dil, v_w_proj_sb, v_w_out, v_norm_ffn2, v_ffn2_w_gate, v_ffn2_w_up, v_ffn2_w_down, v_norm_final):
    args = dict(locals())
    t = x.shape[1]
    xs = x.reshape(t, D_MODEL)
    target = loss_target.reshape(t, D_MODEL)
    tables = rope_tables(t)

    parts = [(l, p) for l in range(2) for p in SUBBLOCKS]
    gains = {n: args[n] for n in NORM_ROWS}

    shipments = []
    for l, p in parts:
        if (l, p) == parts[0]:
            shipments += [(l, p, SUBBLOCKS[p][:2], CHIP_PEERS), (l, p, SUBBLOCKS[p][2:], ALL_PEERS)]
        elif (l, p) == parts[1]:
            shipments += [(l, p, SUBBLOCKS[p][:1], CHIP_PEERS), (l, p, SUBBLOCKS[p][1:], ALL_PEERS)]
        else:
            shipments.append((l, p, SUBBLOCKS[p], ALL_PEERS))
    in_flight, order_token = [], jnp.zeros((1, 1), F32)
    for l, p, tensors, masks in shipments:
        shards = [travelling(args[n][l], by_cols).astype(BF16) for n, by_cols in tensors]
        shards[0] = shards[0] + order_token.astype(BF16)
        in_flight.append(spread_start(shards, per_peer=False, masks=masks, name=f"gather_start_l{l}_{tensors[0][0]}"))
        order_token = in_flight[-1][-1][0:1, 0:1]

    landed = {}

    def arrived(i, after):
        if i not in landed:
            landed[i] = wait_for(i, after)
        return landed[i]

    def wait_for(i, after):
        l, p, tensors, masks = shipments[i]
        tag = f"l{l}_{tensors[0][0]}"
        lands = spread_wait(in_flight[i], after, per_peer=False, masks=masks, name=f"gather_wait_{tag}")
        if masks is CHIP_PEERS:
            lands = relay_wait(relay_start(lands, name=f"gather_relay_{tag}"), name=f"gather_relayed_{tag}")
        return {n: land.reshape(-1, land.shape[-1]) for (n, _), land in zip(tensors, lands)}

    def weights_of(l, p, after):
        mine = [i for i, s in enumerate(shipments) if s[0:2] == (l, p)]
        w = arrived(mine[0], after)
        for i in mine[1:]:
            for n, _ in shipments[i][2]:
                w[n] = functools.partial(lambda after, i, n: arrived(i, after)[n], i=i, n=n)
        return w

    saved, weights = {}, {}
    act = xs
    h = rms_fwd(xs, _row(gains["norm_ffn1"][0]) + order_token, name="l0_ffn1_norm")
    for i, (l, p) in enumerate(parts):
        weights[(l, p)] = weights_of(l, p, h if i == 0 else act)
        nl, np_ = parts[i + 1] if i + 1 < len(parts) else (None, None)
        next_gain = _row(gains["norm_" + np_][nl]) if np_ else None
        if p == "mix":
            act, h, saved[(l, p)] = mixer_fwd(act, h, weights[(l, p)], tables, f"l{l}_mix", next_gain)
        else:
            act, h, saved[(l, p)] = ffn_fwd(act, h, weights[(l, p)], p, f"l{l}_{p}", next_gain)
    dx, dxb, g_final, loss_part = final_loss(act, _row(norm_final), target, name="loss_head")

    gain_grads, sent, sent_last = {}, {}, {}
    per_device = lambda g: g.reshape(N_DEV, -1, g.shape[-1])

    def ship_last(name, g):
        sent_last[name] = spread_start([per_device(g)], per_peer=True, name=f"reduce_start_{name}")
        return sent_last[name][-1]

    for l, p in reversed(parts):
        w, sv = weights[(l, p)], saved[(l, p)]
        if p == "mix":
            gw, rest = mixer_bwd_weights(dxb, sv, w, tables, f"l{l}_mix")
        elif (l, p) == parts[0]:
            gw, rest, token = ffn_bwd_weights(dxb, sv, w, p, f"l{l}_{p}", ship=ship_last)
        else:
            gw, rest, _ = ffn_bwd_weights(dxb, sv, w, p, f"l{l}_{p}")
        if (l, p) != parts[0]:
            sent[(l, p)] = spread_start([per_device(gw[n]) for n, _ in SUBBLOCKS[p]], per_peer=True, name=f"reduce_start_l{l}_{p}")
            token = sent[(l, p)][-1]
        gain = _row(gains["norm_" + p][l]) + token[0:1, 0:1]
        if p == "mix":
            dx, dxb, gain_grads[("norm_mix", l)] = mixer_bwd_input(dx, rest, sv, gain, w, f"l{l}_mix")
        else:
            dx, dxb, gain_grads[("norm_" + p, l)] = ffn_bwd_input(dx, rest, sv, gain, w, p, f"l{l}_{p}")

    partials, big_all = {}, [{}, {}, {}, {}]

    def receive(l, p, after):
        if (l, p) == parts[0]:
            for n, started in sent_last.items():
                partials.setdefault(n, [None, None])[l] = spread_wait(started, after, per_peer=True, name=f"reduce_wait_{n}")[0]
            return
        lands = spread_wait(sent[(l, p)], after, per_peer=True, name=f"reduce_wait_l{l}_{p}")
        for (n, _), land in zip(SUBBLOCKS[p], lands):
            partials.setdefault(n, [None, None])[l] = land

    def update(p):
        for n, by_cols in SUBBLOCKS[p]:
            if by_cols and args[n].shape[-1] % 128 == 0:
                outs = sum_adamw(partials[n], args[n], args["m_" + n], args["v_" + n], tr=256, transposed=True,
                                 name=f"update_{n}")
                for kind, arr in enumerate(outs):
                    big_all[kind][n] = arr
            else:
                outs = sum_adamw(partials[n], travelling(args[n], by_cols), travelling(args["m_" + n], by_cols),
                                 travelling(args["v_" + n], by_cols), tr=256, name=f"update_{n}")
                for kind, arr in enumerate(outs):
                    big_all[kind][n] = travelling(arr, by_cols)
        return outs[1]

    for l, p in reversed(parts[1:]):
        receive(l, p, dx)
    update("ffn2")
    done = update("mix")
    receive(*parts[0], done)
    done = update("ffn1")

    loss_row = jnp.pad(loss_part[:, :1], ((0, 0), (0, D_MODEL - 1)))
    small = jnp.concatenate([gain_grads[(n, l)] for n in NORM_ROWS for l in range(2)] + [g_final, loss_row], axis=0)
    small_g = sum_partials(all_gather_rows(small, done, name="gather_gain_grads"), tr=8, name="sum_gain_grads")
    zero_row = jnp.zeros((1, D_MODEL), F32)
    small_of = lambda pre: jnp.concatenate([args[pre + n] for n in NORM_ROWS] + [_row(args[pre + "norm_final"]), zero_row], axis=0)[None]
    small_out = adamw(small_g[None], small_of(""), small_of("m_"), small_of("v_"), tr=8, name="update_gains")
    small_all = [small_g] + [o[0] for o in small_out]

    def gains_of(s):
        out = {n: s[2 * i:2 * i + 2] for i, n in enumerate(NORM_ROWS)}
        out["norm_final"] = s[6]
        return out

    order = ["norm_ffn1", "ffn1_w_gate", "ffn1_w_up", "ffn1_w_down", "norm_mix", "w_in", "w_proj_dil", "w_proj_sb", "w_out",
             "norm_ffn2", "ffn2_w_gate", "ffn2_w_up", "ffn2_w_down", "norm_final"]
    results = []
    for kind in range(4):
        both = {**big_all[kind], **gains_of(small_all[kind])}
        results += [both[n] for n in order]
    loss = small_g[7, 0]
    return (loss, dx.reshape(1, t, D_MODEL), *results)
```
